```python
import math
import jax, jax.numpy as jnp
from jax import lax
import numpy as np

D_MODEL = 1024
BATCH = 1
SEQ = 16384
DEPTH = 2
DEC_BATCH = 8
DEC_SEQ = 2048
PAST_LEN = 128

D_MIX = D_MODEL
D_HYENA = D_MIX // 2
HYENA_GROUPS = 8
HYENA_ORDER = 2
SHORT_CONV = 3
FILTER_BANDS = 16
FILTER_EMB = 1 + 2 * FILTER_BANDS
FILTER_HIDDEN = 64
N_DIR = 2
DECAY_MIN = -math.log(1e-2) / 1.5
DECAY_MAX = -math.log(1e-2) / 0.3
DECAY_SHIFT = 0.05
N_HEADS = 8
QK_NOPE = 64
QK_ROPE = 32
QK_HEAD = QK_NOPE + QK_ROPE
V_HEAD = 64
D_ATTN = N_HEADS * V_HEAD
Q_LORA = 256
KV_LORA = 128
ROPE_THETA = 10000.0
Q_BLOCK = 128
N_EXPERTS = 16
EC_CAPACITY = 2
D_EXPERT = 512
EPS = 1e-6
COL_HYENA = (HYENA_ORDER + 1) * D_HYENA
COL_Q = COL_HYENA + Q_LORA
COL_KV = COL_Q + KV_LORA
D_IN = COL_KV + QK_ROPE

kernel_name = "hymba_hyena_mla_ec_encoder"


def rms_norm(x, g):
    xf = x.astype(jnp.float32)
    y = xf * lax.rsqrt(jnp.mean(xf * xf, axis=-1, keepdims=True) + EPS)
    return (y * g.astype(jnp.float32)).astype(x.dtype)


def short_conv(u, w, b):
    L = u.shape[1]
    up = jnp.pad(u, ((0, 0), (1, 1), (0, 0)))
    return up[:, :L] * w[0] + up[:, 1:L + 1] * w[1] + up[:, 2:] * w[2] + b


def hyena_filters(L, w1, b1, w2, b2, w3, freq, decay):
    f32 = jnp.float32
    t_idx = jnp.arange(L, dtype=f32)[:, None]
    t = t_idx / (L - 1)
    bands = jnp.linspace(1e-4, FILTER_BANDS - 1, FILTER_BANDS, dtype=f32)
    ang = 2.0 * math.pi * bands * t_idx / L
    feats = jnp.concatenate([t, jnp.cos(ang), -jnp.sin(ang)], axis=-1)
    fr = freq.astype(f32)
    h = jnp.sin(fr * (feats @ w1.astype(f32) + b1.astype(f32)))
    h = jnp.sin(fr * (h @ w2.astype(f32) + b2.astype(f32)))
    h = (h @ w3.astype(f32)).reshape(L, N_DIR, HYENA_ORDER, D_HYENA)
    window = jnp.exp(-t[:, :, None, None] * decay.astype(f32)) + DECAY_SHIFT
    h = h * window
    k = jnp.concatenate([h[:, 0],
                         jnp.zeros((1, HYENA_ORDER, D_HYENA), f32),
                         h[:0:-1, 1]], axis=0)
    k = k / jnp.sum(jnp.abs(k), axis=0, keepdims=True)
    return jnp.fft.rfft(k, axis=0)


def fft_conv(z, kf, bias):
    L = z.shape[1]
    zf32 = z.astype(jnp.float32)
    zf = jnp.fft.rfft(zf32, n=2 * L, axis=1)
    y = jnp.fft.irfft(zf * kf, n=2 * L, axis=1)[:, :L]
    return y + zf32 * bias.astype(jnp.float32)


def hyena_mixer(u, conv_w, conv_b, w1, b1, w2, b2, w3, freq, decay, bias):
    L = u.shape[1]
    uc = short_conv(u, conv_w, conv_b)
    v, x1, x2 = jnp.split(uc, HYENA_ORDER + 1, axis=-1)
    kf = hyena_filters(L, w1, b1, w2, b2, w3, freq, decay)
    z = x1.astype(jnp.float32) * fft_conv(v, kf[:, 0], bias[0])
    z = x2.astype(jnp.float32) * fft_conv(z, kf[:, 1], bias[1])
    return z.astype(u.dtype)


def rope(x, cos, sin):
    x1, x2 = jnp.split(x, 2, axis=-1)
    return x * cos + jnp.concatenate([-x2, x1], axis=-1) * sin


def mla_mixer(cq, ckv, krope, q_a_norm, w_uq, kv_a_norm, w_ukv, q_norm, k_norm):
    f32 = jnp.float32
    B, L, _ = cq.shape
    q = (rms_norm(cq, q_a_norm) @ w_uq).reshape(B, L, N_HEADS, QK_HEAD)
    kv = (rms_norm(ckv, kv_a_norm) @ w_ukv).reshape(B, L, N_HEADS, QK_NOPE + V_HEAD)
    k_nope, v = kv[..., :QK_NOPE], kv[..., QK_NOPE:]
    k = jnp.concatenate([k_nope, jnp.broadcast_to(krope[:, :, None, :], (B, L, N_HEADS, QK_ROPE))], axis=-1)
    q = rms_norm(q, q_norm).astype(f32)
    k = rms_norm(k, k_norm).astype(f32)
    pos = jnp.arange(L, dtype=f32)
    inv_freq = ROPE_THETA ** (-jnp.arange(0, QK_ROPE, 2, dtype=f32) / QK_ROPE)
    ang = pos[:, None] * inv_freq
    ang = jnp.concatenate([ang, ang], axis=-1)[:, None, :]
    cos, sin = jnp.cos(ang), jnp.sin(ang)
    q = jnp.concatenate([q[..., :QK_NOPE], rope(q[..., QK_NOPE:], cos, sin)], axis=-1) * (QK_HEAD ** -0.5)
    k = jnp.concatenate([k[..., :QK_NOPE], rope(k[..., QK_NOPE:], cos, sin)], axis=-1)
    v = v.astype(f32)
    qb = q.reshape(B, L // Q_BLOCK, Q_BLOCK, N_HEADS, QK_HEAD).transpose(1, 0, 2, 3, 4)

    def attend(q_blk):
        s = jnp.einsum('bqhd,bkhd->bhqk', q_blk, k)
        p = jax.nn.softmax(s, axis=-1)
        return jnp.einsum('bhqk,bkhd->bqhd', p, v)

    o = lax.map(attend, qb)
    return o.transpose(1, 0, 2, 3, 4).reshape(B, L, D_ATTN).astype(cq.dtype)


def ec_moe(x, w_router, w_gate, w_up, w_down):
    B, L, D = x.shape
    N = B * L
    cap = max(1, EC_CAPACITY * N // N_EXPERTS)
    xt = x.reshape(N, D)
    aff = jax.nn.softmax((xt @ w_router).astype(jnp.float32), axis=-1)
    gates, idx = lax.top_k(aff.T, cap)
    xs = xt[idx]
    h = jax.nn.silu(jnp.einsum('ecd,edf->ecf', xs, w_gate)) * jnp.einsum('ecd,edf->ecf', xs, w_up)
    y = jnp.einsum('ecf,efd->ecd', h, w_down) * gates[..., None].astype(x.dtype)
    out = jnp.zeros((N, D), x.dtype).at[idx.reshape(-1)].add(y.reshape(-1, D).astype(x.dtype))
    return out.reshape(B, L, D)


def setup_inputs(seed: int = 0) -> dict:
    key = jax.random.key(seed)
    ks = jax.random.split(key, 32)
    f32 = jnp.float32

    def nrm(k, shape, scale):
        return jax.random.normal(k, shape, f32) * scale

    def gain(k, shape):
        return 1.0 + 0.05 * jax.random.normal(k, shape, f32)

    return {
        "x_prompt": nrm(ks[0], (BATCH, SEQ, D_MODEL), 1.0),
        "x_sample": nrm(ks[1], (DEC_BATCH, DEC_SEQ, D_MODEL), 1.0),
        "attn_norm": gain(ks[2], (DEPTH, D_MODEL)),
        "w_in": nrm(ks[3], (DEPTH, D_MODEL, D_IN), D_MODEL ** -0.5),
        "conv_w": nrm(ks[4], (DEPTH, SHORT_CONV, COL_HYENA), SHORT_CONV ** -0.5),
        "conv_b": nrm(ks[5], (DEPTH, COL_HYENA), 0.02),
        "filt_w1": nrm(ks[6], (DEPTH, FILTER_EMB, FILTER_HIDDEN), FILTER_EMB ** -0.5),
        "filt_b1": nrm(ks[7], (DEPTH, FILTER_HIDDEN), 0.02),
        "filt_w2": nrm(ks[8], (DEPTH, FILTER_HIDDEN, FILTER_HIDDEN), FILTER_HIDDEN ** -0.5),
        "filt_b2": nrm(ks[9], (DEPTH, FILTER_HIDDEN), 0.02),
        "filt_w3": nrm(ks[10], (DEPTH, FILTER_HIDDEN, N_DIR * HYENA_ORDER * D_HYENA), FILTER_HIDDEN ** -0.5),
        "filt_freq": gain(ks[11], (DEPTH, FILTER_HIDDEN)),
        "filt_decay": jax.random.uniform(ks[12], (DEPTH, N_DIR, HYENA_ORDER, D_HYENA), f32, DECAY_MIN, DECAY_MAX),
        "hyena_bias": nrm(ks[13], (DEPTH, HYENA_ORDER, D_HYENA), 1.0),
        "q_a_norm": gain(ks[14], (DEPTH, Q_LORA)),
        "w_uq": nrm(ks[15], (DEPTH, Q_LORA, N_HEADS * QK_HEAD), Q_LORA ** -0.5),
        "kv_a_norm": gain(ks[16], (DEPTH, KV_LORA)),
        "w_ukv": nrm(ks[17], (DEPTH, KV_LORA, N_HEADS * (QK_NOPE + V_HEAD)), KV_LORA ** -0.5),
        "q_norm": gain(ks[18], (DEPTH, QK_HEAD)),
        "k_norm": gain(ks[19], (DEPTH, QK_HEAD)),
        "out_norm": gain(ks[20], (DEPTH, D_MIX)),
        "w_out": nrm(ks[21], (DEPTH, D_MIX, D_MODEL), D_MIX ** -0.5),
        "mlp_norm": gain(ks[22], (DEPTH, D_MODEL)),
        "w_router": nrm(ks[23], (DEPTH, D_MODEL, N_EXPERTS), D_MODEL ** -0.5),
        "w_gate": nrm(ks[24], (DEPTH, N_EXPERTS, D_MODEL, D_EXPERT), D_MODEL ** -0.5),
        "w_up": nrm(ks[25], (DEPTH, N_EXPERTS, D_MODEL, D_EXPERT), D_MODEL ** -0.5),
        "w_down": nrm(ks[26], (DEPTH, N_EXPERTS, D_EXPERT, D_MODEL), D_EXPERT ** -0.5),
    }


def reference(x_prompt, x_sample, attn_norm, w_in, conv_w, conv_b, filt_w1, filt_b1, filt_w2, filt_b2,
              filt_w3, filt_freq, filt_decay, hyena_bias, q_a_norm, w_uq, kv_a_norm, w_ukv, q_norm, k_norm,
              out_norm, w_out, mlp_norm, w_router, w_gate, w_up, w_down):

    def trunk(x):
        B, L, _ = x.shape
        for l in range(DEPTH):
            u = rms_norm(x, attn_norm[l]) @ w_in[l]
            y_h = hyena_mixer(u[..., :COL_HYENA], conv_w[l], conv_b[l], filt_w1[l], filt_b1[l],
                              filt_w2[l], filt_b2[l], filt_w3[l], filt_freq[l], filt_decay[l], hyena_bias[l])
            y_a = mla_mixer(u[..., COL_HYENA:COL_Q], u[..., COL_Q:COL_KV], u[..., COL_KV:],
                            q_a_norm[l], w_uq[l], kv_a_norm[l], w_ukv[l], q_norm[l], k_norm[l])
            y = jnp.concatenate([y_h, y_a], axis=-1).reshape(B, L, 2, D_MIX // 2)
            y = rms_norm(y, out_norm[l].reshape(2, D_MIX // 2)).reshape(B, L, D_MIX)
            x = x + y @ w_out[l]
            x = x + ec_moe(rms_norm(x, mlp_norm[l]), w_router[l], w_gate[l], w_up[l], w_down[l])
        return x

    y_prompt = trunk(x_prompt)
    y_sample = trunk(x_sample)
    return (y_prompt, y_sample)
```

```python
import functools
import math

import jax
import jax.numpy as jnp
from jax import lax
from jax.experimental import pallas as pl
from jax.experimental.pallas import tpu as pltpu

F32 = jnp.float32
BF16 = jnp.bfloat16
I32 = jnp.int32
HIGHEST = lax.Precision.HIGHEST

D_MODEL = 1024
DEPTH = 2
D_HYENA = 512
HYENA_ORDER = 2
FILTER_BANDS = 16
FILTER_HIDDEN = 64
N_DIR = 2
DECAY_SHIFT = 0.05
N_HEADS = 8
QK_NOPE = 64
QK_ROPE = 32
QK_HEAD = QK_NOPE + QK_ROPE
V_HEAD = 64
D_ATTN = N_HEADS * V_HEAD
Q_LORA = 256
KV_LORA = 128
ROPE_THETA = 10000.0
N_EXPERTS = 16
EC_CAPACITY = 2
D_EXPERT = 512
EPS = 1e-6
COL_HYENA = (HYENA_ORDER + 1) * D_HYENA
COL_Q = COL_HYENA + Q_LORA
COL_KV = COL_Q + KV_LORA
D_IN = COL_KV + QK_ROPE
D_IN_PAD = 2048

LANES = 128
SLOT = 128
TOK_BLOCK = 256
SLOT_TILE = 256
MIB = 1024 * 1024


def _params(sem, vmem_mib):
    return pltpu.CompilerParams(dimension_semantics=sem, vmem_limit_bytes=vmem_mib * MIB)


def _rms(x):
    return x * lax.rsqrt(jnp.mean(x * x, axis=-1, keepdims=True) + EPS)


def _inproj_body(x_ref, g_ref, w_ref, uh_ref, cq_ref, ckvr_ref):
    xn = _rms(x_ref[...]) * g_ref[...]
    u = jnp.dot(xn.astype(BF16), w_ref[...], preferred_element_type=F32)
    uh_ref[...] = u[:, :COL_HYENA]
    cq_ref[...] = u[:, COL_HYENA:COL_Q]
    ckvr_ref[...] = u[:, COL_Q:]


def _inproj(x2d, g, w_pad):
    n = x2d.shape[0]
    tm = min(512, n)
    return pl.pallas_call(
        _inproj_body,
        grid=(n // tm,),
        in_specs=[pl.BlockSpec((tm, D_MODEL), lambda i: (i, 0)),
                  pl.BlockSpec((1, D_MODEL), lambda i: (0, 0)),
                  pl.BlockSpec((D_MODEL, D_IN_PAD), lambda i: (0, 0))],
        out_specs=[pl.BlockSpec((tm, COL_HYENA), lambda i: (i, 0)),
                   pl.BlockSpec((tm, Q_LORA), lambda i: (i, 0)),
                   pl.BlockSpec((tm, D_IN_PAD - COL_Q), lambda i: (i, 0))],
        out_shape=[jax.ShapeDtypeStruct((n, COL_HYENA), F32),
                   jax.ShapeDtypeStruct((n, Q_LORA), F32),
                   jax.ShapeDtypeStruct((n, D_IN_PAD - COL_Q), F32)],
        compiler_params=_params(("parallel",), 40),
        name="inproj",
    )(x2d, g, w_pad)


def _sconv_body(u_ref, prev_ref, next_ref, w_ref, b_ref, v_ref, x1_ref, x2_ref):
    i = pl.program_id(1)
    last = pl.num_programs(1) - 1
    u = u_ref[0]
    tl = u.shape[0]
    prev_row = jnp.where(i == 0, 0.0, prev_ref[0, 7:8, :])
    next_row = jnp.where(i == last, 0.0, next_ref[0, 0:1, :])
    row = lax.broadcasted_iota(I32, u.shape, 0)
    up = jnp.where(row == 0, prev_row, pltpu.roll(u, 1, axis=0))
    dn = jnp.where(row == tl - 1, next_row, pltpu.roll(u, tl - 1, axis=0))
    y = up * w_ref[0:1, :] + u * w_ref[1:2, :] + dn * w_ref[2:3, :] + b_ref[...]
    v_ref[0] = y[:, :D_HYENA]
    x1_ref[0] = y[:, D_HYENA:2 * D_HYENA]
    x2_ref[0] = y[:, 2 * D_HYENA:]


def _sconv(uh, w, b):
    bsz, seq, c = uh.shape
    tl = min(512, seq)
    r = tl // 8
    nblk8 = seq // 8
    out = jax.ShapeDtypeStruct((bsz, seq, D_HYENA), F32)
    ospec = pl.BlockSpec((1, tl, D_HYENA), lambda bi, i: (bi, i, 0))
    return pl.pallas_call(
        _sconv_body,
        grid=(bsz, seq // tl),
        in_specs=[pl.BlockSpec((1, tl, c), lambda bi, i: (bi, i, 0)),
                  pl.BlockSpec((1, 8, c), lambda bi, i: (bi, jnp.maximum(i * r - 1, 0), 0)),
                  pl.BlockSpec((1, 8, c), lambda bi, i: (bi, jnp.minimum((i + 1) * r, nblk8 - 1), 0)),
                  pl.BlockSpec((3, c), lambda bi, i: (0, 0)),
                  pl.BlockSpec((1, c), lambda bi, i: (0, 0))],
        out_specs=[ospec, ospec, ospec],
        out_shape=[out, out, out],
        compiler_params=_params(("parallel", "parallel"), 40),
        name="sconv",
    )(uh, uh, uh, w, b)


def _filter_body(bands_ref, w1t_ref, w1c_ref, w1s_ref, b1_ref, w2_ref, b2_ref, fr_ref, w3_ref, dec_ref,
                 k_ref, sum_ref, *, seq):
    i = pl.program_id(0)
    tr = k_ref.shape[0]
    n = i * tr + lax.broadcasted_iota(I32, (tr, 1), 0)
    t_idx = jnp.where(n < seq, n, 2 * seq - n).astype(F32)
    t = t_idx / (seq - 1)
    ang = 2.0 * math.pi * bands_ref[...] * t_idx / seq
    fr = fr_ref[...]
    pre = (t * w1t_ref[...]
           + jnp.dot(jnp.cos(ang), w1c_ref[...], precision=HIGHEST, preferred_element_type=F32)
           - jnp.dot(jnp.sin(ang), w1s_ref[...], precision=HIGHEST, preferred_element_type=F32))
    h = jnp.sin(fr * (pre + b1_ref[...]))
    h = jnp.sin(fr * (jnp.dot(h, w2_ref[...], precision=HIGHEST, preferred_element_type=F32) + b2_ref[...]))
    h = jnp.dot(h, w3_ref[0], precision=HIGHEST, preferred_element_type=F32)
    window = jnp.exp(-t * dec_ref[0]) + DECAY_SHIFT
    k = jnp.where(n == seq, 0.0, h * window)
    k_ref[...] = k

    @pl.when(i == 0)
    def _():
        sum_ref[...] = jnp.zeros_like(sum_ref)

    sum_ref[...] += jnp.sum(jnp.abs(k), axis=0, keepdims=True)


def _filters(seq, fw):
    cols = HYENA_ORDER * D_HYENA
    tr = min(512, seq)
    half = seq // tr
    const = lambda shape: pl.BlockSpec(shape, lambda i: (0,) * len(shape))
    return pl.pallas_call(
        functools.partial(_filter_body, seq=seq),
        grid=(2 * seq // tr,),
        in_specs=[const((1, LANES)), const((1, LANES)), const((LANES, LANES)), const((LANES, LANES)),
                  const((1, LANES)), const((LANES, LANES)), const((1, LANES)), const((1, LANES)),
                  pl.BlockSpec((1, LANES, cols), lambda i: (i // half, 0, 0)),
                  pl.BlockSpec((1, 1, cols), lambda i: (i // half, 0, 0))],
        out_specs=[pl.BlockSpec((tr, cols), lambda i: (i, 0)),
                   pl.BlockSpec((1, cols), lambda i: (0, 0))],
        out_shape=[jax.ShapeDtypeStruct((2 * seq, cols), F32),
                   jax.ShapeDtypeStruct((1, cols), F32)],
        compiler_params=_params(("arbitrary",), 32),
        name="filter_gen",
    )(fw["bands"], fw["w1t"], fw["w1c"], fw["w1s"], fw["b1"], fw["w2"], fw["b2"], fw["freq"],
      fw["w3"], fw["decay"])


def _fft_dims(seq):
    n2 = 128 if 2 * seq >= 32768 else 64
    n1 = 2 * seq // n2
    return n1, n2


def _dft_tables(n1, n2):
    n = n1 * n2
    n1h = n1 // 2
    k1 = jnp.arange(n1h, dtype=I32)[:, None]
    m1 = jnp.arange(n1, dtype=I32)[None, :]
    ang = (2.0 * math.pi / n1) * ((k1 * m1) % n1).astype(F32)
    top = jnp.cos(ang)
    bot = -jnp.sin(ang)
    nyq = jnp.where(m1 % 2 == 0, 1.0, -1.0).astype(F32)
    bot = jnp.concatenate([nyq, bot[1:]], axis=0)
    fa = jnp.concatenate([top, bot], axis=0)
    weight = jnp.where((jnp.arange(n1) % n1h) == 0, 1.0, 2.0).astype(F32) / n
    fi = (fa[:, :n1h] * weight[:, None]).T

    kk = jnp.arange(n1h + 1, dtype=I32)[:, None, None]
    k2 = jnp.arange(n2, dtype=I32)[None, :, None]
    m2 = jnp.arange(n2, dtype=I32)[None, None, :]
    phi = (2.0 * math.pi / n) * ((m2 * (kk + n1 * k2)) % n).astype(F32)
    gr = jnp.cos(phi)
    gi = -jnp.sin(phi)
    blk = jnp.concatenate([jnp.concatenate([gr, -gi], axis=2),
                           jnp.concatenate([gi, gr], axis=2)], axis=1)
    left = (jnp.arange(2 * n2) < n2)[None, :]
    g0 = jnp.where(left, blk[0], 0.0)
    gf = jnp.concatenate([g0[None], blk[1:n1h]], axis=0)
    gnyq = jnp.concatenate([jnp.zeros((2 * n2, n2), F32), blk[n1h][:, :n2]], axis=1)
    return dict(fa_full=fa.astype(BF16), fa_half=fa[:, :n1h].astype(BF16), fi=fi.astype(BF16),
                gf=gf.astype(BF16), gnyq=gnyq.astype(BF16),
                ginv=jnp.transpose(gf, (0, 2, 1)).astype(BF16), m2=gnyq.T.astype(BF16))


def _fft_a_body(x_ref, f_ref, o_ref):
    a = jnp.dot(f_ref[...], x_ref[0].astype(BF16), preferred_element_type=F32)
    n1h = o_ref.shape[2]
    o_ref[0, 0] = a[:n1h].astype(BF16)
    o_ref[0, 1] = a[n1h:].astype(BF16)


def _fft_a(x3, fmat):
    bsz, r, nc = x3.shape
    n1 = fmat.shape[0]
    tn = min(2048, nc)
    return pl.pallas_call(
        _fft_a_body,
        grid=(bsz, nc // tn),
        in_specs=[pl.BlockSpec((1, r, tn), lambda b, j: (b, 0, j)),
                  pl.BlockSpec((n1, r), lambda b, j: (0, 0))],
        out_specs=pl.BlockSpec((1, 2, n1 // 2, tn), lambda b, j: (b, 0, 0, j)),
        out_shape=jax.ShapeDtypeStruct((bsz, 2, n1 // 2, nc), BF16),
        compiler_params=_params(("parallel", "parallel"), 32),
        name="fft_stage_a",
    )(x3, fmat)


def _fft_b_body(ar_ref, ai_ref, g_ref, gn_ref, sc_ref, x_ref, xn_ref, *, kb):
    j = pl.program_id(2)
    inv = 1.0 / sc_ref[...]
    for kk in range(kb):
        rhs = jnp.concatenate([ar_ref[0, 0, kk], ai_ref[0, 0, kk]], axis=0)
        x_ref[0, kk] = jnp.dot(g_ref[kk], rhs, preferred_element_type=F32) * inv

    @pl.when(j == 0)
    def _():
        rhs = jnp.concatenate([ar_ref[0, 0, 0], ai_ref[0, 0, 0]], axis=0)
        xn_ref[0] = jnp.dot(gn_ref[...], rhs, preferred_element_type=F32) * inv


def _fft_b(a5, tabs, scale):
    bsz, _, n1h, n2, c = a5.shape
    kb = 8
    tc = min(512, c)
    return pl.pallas_call(
        functools.partial(_fft_b_body, kb=kb),
        grid=(bsz, c // tc, n1h // kb),
        in_specs=[pl.BlockSpec((1, 1, kb, n2, tc), lambda b, ci, j: (b, 0, j, 0, ci)),
                  pl.BlockSpec((1, 1, kb, n2, tc), lambda b, ci, j: (b, 1, j, 0, ci)),
                  pl.BlockSpec((kb, 2 * n2, 2 * n2), lambda b, ci, j: (j, 0, 0)),
                  pl.BlockSpec((2 * n2, 2 * n2), lambda b, ci, j: (0, 0)),
                  pl.BlockSpec((1, tc), lambda b, ci, j: (0, ci))],
        out_specs=[pl.BlockSpec((1, kb, 2 * n2, tc), lambda b, ci, j: (b, j, 0, ci)),
                   pl.BlockSpec((1, 2 * n2, tc), lambda b, ci, j: (b, 0, ci))],
        out_shape=[jax.ShapeDtypeStruct((bsz, n1h, 2 * n2, c), F32),
                   jax.ShapeDtypeStruct((bsz, 2 * n2, c), F32)],
        compiler_params=_params(("parallel", "parallel", "arbitrary"), 40),
        name="fft_stage_b",
    )(a5, a5, tabs["gf"], tabs["gnyq"], scale)


def _cmul(x, k, n2):
    xr, xi = x[:n2], x[n2:]
    kr, ki = k[:n2], k[n2:]
    return jnp.concatenate([xr * kr - xi * ki, xr * ki + xi * kr], axis=0).astype(BF16)


def _ifft_b_body(x_ref, xn_ref, k_ref, kn_ref, g_ref, m2_ref, o_ref, *, kb):
    j = pl.program_id(1)
    n2 = o_ref.shape[3]
    for kk in range(kb):
        acc = jnp.dot(g_ref[kk], _cmul(x_ref[0, kk], k_ref[0, kk], n2), preferred_element_type=F32)
        o_ref[0, 0, kk] = acc[:n2].astype(BF16)
        o_ref[0, 1, kk] = acc[n2:].astype(BF16)

    @pl.when(j == 0)
    def _():
        acc = jnp.dot(m2_ref[...], _cmul(xn_ref[0], kn_ref[0], n2), preferred_element_type=F32)
        o_ref[0, 1, 0] = acc[n2:].astype(BF16)


def _ifft_b(xs, xnyq, kf, kfnyq, order, tabs):
    bsz, n1h, tn2, c = xs.shape
    n2 = tn2 // 2
    kb = 8
    return pl.pallas_call(
        functools.partial(_ifft_b_body, kb=kb),
        grid=(bsz, n1h // kb),
        in_specs=[pl.BlockSpec((1, kb, tn2, c), lambda b, j: (b, j, 0, 0)),
                  pl.BlockSpec((1, tn2, c), lambda b, j: (b, 0, 0)),
                  pl.BlockSpec((1, kb, tn2, c), lambda b, j: (0, j, 0, order)),
                  pl.BlockSpec((1, tn2, c), lambda b, j: (0, 0, order)),
                  pl.BlockSpec((kb, tn2, tn2), lambda b, j: (j, 0, 0)),
                  pl.BlockSpec((tn2, tn2), lambda b, j: (0, 0))],
        out_specs=pl.BlockSpec((1, 2, kb, n2, c), lambda b, j: (b, 0, j, 0, 0)),
        out_shape=jax.ShapeDtypeStruct((bsz, 2, n1h, n2, c), BF16),
        compiler_params=_params(("parallel", "arbitrary"), 40),
        name="ifft_stage_b",
    )(xs, xnyq, kf, kfnyq, tabs["ginv"], tabs["m2"])


def _ifft_a_body(b_ref, f_ref, z_ref, gate_ref, bias_ref, o_ref):
    y = jnp.dot(f_ref[...], b_ref[0], preferred_element_type=F32)
    o_ref[0] = gate_ref[0] * (y + z_ref[0] * bias_ref[...])


def _ifft_a(b3, fi, z3, gate3, bias_t):
    bsz, n1, nc = b3.shape
    n1h = n1 // 2
    tn = bias_t.shape[1]
    return pl.pallas_call(
        _ifft_a_body,
        grid=(bsz, nc // tn),
        in_specs=[pl.BlockSpec((1, n1, tn), lambda b, j: (b, 0, j)),
                  pl.BlockSpec((n1h, n1), lambda b, j: (0, 0)),
                  pl.BlockSpec((1, n1h, tn), lambda b, j: (b, 0, j)),
                  pl.BlockSpec((1, n1h, tn), lambda b, j: (b, 0, j)),
                  pl.BlockSpec((1, tn), lambda b, j: (0, 0))],
        out_specs=pl.BlockSpec((1, n1h, tn), lambda b, j: (b, 0, j)),
        out_shape=jax.ShapeDtypeStruct((bsz, n1h, nc), F32),
        compiler_params=_params(("parallel", "parallel"), 32),
        name="ifft_stage_a",
    )(b3, fi, z3, gate3, bias_t)


def _hyena(uh, conv_w, conv_b, fw, bias):
    bsz, seq, _ = uh.shape
    c = D_HYENA
    n1, n2 = _fft_dims(seq)
    n1h = n1 // 2
    tabs = _dft_tables(n1, n2)
    v, x1, x2 = _sconv(uh, conv_w, conv_b)

    kcirc, ksum = _filters(seq, fw)
    cols = HYENA_ORDER * c
    ka = _fft_a(kcirc.reshape(1, n1, n2 * cols), tabs["fa_full"])
    kf, kfnyq = _fft_b(ka.reshape(1, 2, n1h, n2, cols), tabs, ksum)

    tn = min(2048, n2 * c)
    ones = jnp.ones((1, c), F32)
    z = v
    for order, gate in enumerate((x1, x2)):
        za = _fft_a(z.reshape(bsz, n1h, n2 * c), tabs["fa_half"])
        zs, znyq = _fft_b(za.reshape(bsz, 2, n1h, n2, c), tabs, ones)
        zb = _ifft_b(zs, znyq, kf, kfnyq, order, tabs)
        bias_t = jnp.tile(bias[order][None, :], (1, tn // c))
        z = _ifft_a(zb.reshape(bsz, n1, n2 * c), tabs["fi"], z.reshape(bsz, n1h, n2 * c),
                    gate.reshape(bsz, n1h, n2 * c), bias_t).reshape(bsz, seq, c)
    return z


def _qkv_body(cq_ref, ckvr_ref, cos_ref, sin_ref, wq_ref, wk_ref, wv_ref, qa_ref, kva_ref, qg_ref, kg_ref,
              q_ref, k_ref, v_ref):
    cqn = _rms(cq_ref[0]) * qa_ref[...]
    qs = jnp.dot(cqn.astype(BF16), wq_ref[...], preferred_element_type=F32)
    ck = ckvr_ref[0]
    ckvn = (_rms(ck[:, :KV_LORA]) * kva_ref[...]).astype(BF16)
    ks = jnp.dot(ckvn, wk_ref[...], preferred_element_type=F32)
    vs = jnp.dot(ckvn, wv_ref[...], preferred_element_type=F32)
    krope = pltpu.roll(ck[:, KV_LORA:], QK_NOPE, axis=1)
    cos = cos_ref[...]
    sin = sin_ref[...]
    lane = lax.broadcasted_iota(I32, cos.shape, 1)
    half = QK_ROPE // 2
    first = (lane >= QK_NOPE) & (lane < QK_NOPE + half)
    second = (lane >= QK_NOPE + half) & (lane < QK_HEAD)

    def head(xh, gain, scale):
        ms = jnp.sum(xh * xh, axis=-1, keepdims=True) * (1.0 / QK_HEAD)
        xn = xh * lax.rsqrt(ms + EPS) * gain
        rot = jnp.where(first, -pltpu.roll(xn, SLOT - half, axis=1),
                        jnp.where(second, pltpu.roll(xn, half, axis=1), 0.0))
        return (xn * cos + rot * sin) * scale

    for h in range(N_HEADS):
        sl = slice(h * SLOT, (h + 1) * SLOT)
        q_ref[0, h] = head(qs[:, sl], qg_ref[...], QK_HEAD ** -0.5).astype(BF16)
        k_ref[0, h] = head(ks[:, sl] + krope, kg_ref[...], 1.0).astype(BF16)
        v_ref[0, h] = vs[:, sl].astype(BF16)


def _qkv(cq, ckvr, cos_t, sin_t, aw):
    bsz, seq, _ = cq.shape
    tm = min(512, seq)
    const = lambda shape: pl.BlockSpec(shape, lambda b, i: (0,) * len(shape))
    hspec = pl.BlockSpec((1, N_HEADS, tm, SLOT), lambda b, i: (b, 0, i, 0))
    hshape = jax.ShapeDtypeStruct((bsz, N_HEADS, seq, SLOT), BF16)
    return pl.pallas_call(
        _qkv_body,
        grid=(bsz, seq // tm),
        in_specs=[pl.BlockSpec((1, tm, Q_LORA), lambda b, i: (b, i, 0)),
                  pl.BlockSpec((1, tm, D_IN_PAD - COL_Q), lambda b, i: (b, i, 0)),
                  pl.BlockSpec((tm, SLOT), lambda b, i: (i, 0)),
                  pl.BlockSpec((tm, SLOT), lambda b, i: (i, 0)),
                  const((Q_LORA, N_HEADS * SLOT)), const((KV_LORA, N_HEADS * SLOT)),
                  const((KV_LORA, N_HEADS * SLOT)),
                  const((1, Q_LORA)), const((1, KV_LORA)), const((1, SLOT)), const((1, SLOT))],
        out_specs=[hspec, hspec, hspec],
        out_shape=[hshape, hshape, hshape],
        compiler_params=_params(("parallel", "parallel"), 40),
        name="qkv_prep",
    )(cq, ckvr, cos_t, sin_t, aw["wq"], aw["wk"], aw["wv"], aw["qa"], aw["kva"], aw["qg"], aw["kg"])


def _flash_body(q_ref, k_ref, v_ref, o_ref, m_ref, l_ref, acc_ref, *, tk, nk):
    m_ref[...] = jnp.full(m_ref.shape, -jnp.inf, F32)
    l_ref[...] = jnp.zeros(l_ref.shape, F32)
    acc_ref[...] = jnp.zeros(acc_ref.shape, F32)
    lane = lax.broadcasted_iota(I32, acc_ref.shape, 1)
    low = lane < V_HEAD

    def step(i, carry):
        ks = pl.multiple_of(i * tk, tk)
        pv = []
        alpha = []
        for hh in range(2):
            q = q_ref[0, hh]
            k = k_ref[0, hh, pl.ds(ks, tk), :]
            v = v_ref[0, hh, pl.ds(ks, tk), :]
            s = lax.dot_general(q, k, (((1,), (1,)), ((), ())), preferred_element_type=F32)
            m_prev = m_ref[hh]
            m_new = jnp.maximum(m_prev, jnp.max(s, axis=1, keepdims=True))
            a = jnp.exp(m_prev - m_new)
            p = jnp.exp(s - m_new)
            l_ref[hh] = a * l_ref[hh] + jnp.sum(p, axis=1, keepdims=True)
            m_ref[hh] = m_new
            pv.append(jnp.dot(p.astype(BF16), v, preferred_element_type=F32))
            alpha.append(a)
        acc_ref[...] = acc_ref[...] * jnp.where(low, alpha[0], alpha[1]) + pv[0] + pv[1]
        return carry

    lax.fori_loop(0, nk, step, 0)
    o_ref[0] = acc_ref[...] / jnp.where(low, l_ref[0], l_ref[1])


def _flash(q, k, v):
    bsz, nh, seq, _ = q.shape
    tq = min(512, seq)
    tk = min(512, seq)
    return pl.pallas_call(
        functools.partial(_flash_body, tk=tk, nk=seq // tk),
        grid=(bsz, nh // 2, seq // tq),
        in_specs=[pl.BlockSpec((1, 2, tq, SLOT), lambda b, hp, i: (b, hp, i, 0)),
                  pl.BlockSpec((1, 2, seq, SLOT), lambda b, hp, i: (b, hp, 0, 0)),
                  pl.BlockSpec((1, 2, seq, SLOT), lambda b, hp, i: (b, hp, 0, 0))],
        out_specs=pl.BlockSpec((1, tq, 2 * V_HEAD), lambda b, hp, i: (b, i, hp)),
        out_shape=jax.ShapeDtypeStruct((bsz, seq, nh * V_HEAD), F32),
        scratch_shapes=[pltpu.VMEM((2, tq, 1), F32), pltpu.VMEM((2, tq, 1), F32),
                        pltpu.VMEM((tq, 2 * V_HEAD), F32)],
        compiler_params=_params(("parallel", "parallel", "parallel"), 48),
        name="flash_attn",
    )(q, k, v)


def _outproj_body(yh_ref, ya_ref, x_ref, og_ref, w_ref, mg_ref, wr_ref, xm_ref, xn_ref, aff_ref):
    og = og_ref[...]
    half = D_MODEL // 2
    y = jnp.concatenate([_rms(yh_ref[...]) * og[:, :half], _rms(ya_ref[...]) * og[:, half:]], axis=1)
    xm = x_ref[...] + jnp.dot(y.astype(BF16), w_ref[...], preferred_element_type=F32)
    xm_ref[...] = xm
    xn = _rms(xm) * mg_ref[...]
    xn_ref[...] = xn.astype(BF16)
    logits = jnp.dot(xn, wr_ref[...], precision=HIGHEST, preferred_element_type=F32)
    lane = lax.broadcasted_iota(I32, logits.shape, 1)
    logits = jnp.where(lane < N_EXPERTS, logits, -jnp.inf)
    e = jnp.exp(logits - jnp.max(logits, axis=-1, keepdims=True))
    aff = e / jnp.sum(e, axis=-1, keepdims=True)
    aff_ref[...] = aff.T[:N_EXPERTS]


def _outproj(yh, ya, x2d, og, w_out, mg, wr_pad):
    n = x2d.shape[0]
    tm = min(512, n)
    half = D_MODEL // 2
    const = lambda shape: pl.BlockSpec(shape, lambda i: (0,) * len(shape))
    return pl.pallas_call(
        _outproj_body,
        grid=(n // tm,),
        in_specs=[pl.BlockSpec((tm, half), lambda i: (i, 0)),
                  pl.BlockSpec((tm, half), lambda i: (i, 0)),
                  pl.BlockSpec((tm, D_MODEL), lambda i: (i, 0)),
                  const((1, D_MODEL)), const((D_MODEL, D_MODEL)), const((1, D_MODEL)),
                  const((D_MODEL, LANES))],
        out_specs=[pl.BlockSpec((tm, D_MODEL), lambda i: (i, 0)),
                   pl.BlockSpec((tm, D_MODEL), lambda i: (i, 0)),
                   pl.BlockSpec((N_EXPERTS, tm), lambda i: (0, i))],
        out_shape=[jax.ShapeDtypeStruct((n, D_MODEL), F32),
                   jax.ShapeDtypeStruct((n, D_MODEL), BF16),
                   jax.ShapeDtypeStruct((N_EXPERTS, n), F32)],
        compiler_params=_params(("parallel",), 40),
        name="outproj_router",
    )(yh, ya, x2d, og, w_out, mg, wr_pad)


def _select_body(aff_ref, upper_ref, lower_ref, pos_ref, off_ref, *, cap):
    a = aff_ref[0]
    upper = upper_ref[...]
    lower = lower_ref[...]

    def count(mask):
        return jnp.sum(jnp.sum(mask.astype(F32), axis=1, keepdims=True), axis=0, keepdims=True)

    def bit_step(i, thr):
        cand = thr | jnp.left_shift(jnp.int32(1), 30 - i)
        ok = count(a >= pltpu.bitcast(cand, F32)) >= cap
        return jnp.where(ok, cand, thr)

    thr_bits = lax.fori_loop(0, 31, bit_step, jnp.zeros((1, 1), I32))
    thr = pltpu.bitcast(thr_bits, F32)
    gt = a > thr
    eq = a == thr
    need = cap - count(gt)

    def prefix(mask):
        within = jnp.dot(mask.astype(BF16), upper, preferred_element_type=F32)
        total = within[:, LANES - 1:LANES]
        offs = jnp.dot(lower, jnp.broadcast_to(total, within.shape).astype(BF16), preferred_element_type=F32)
        return within, offs

    w_eq, o_eq = prefix(eq)
    sel = gt | (eq & (o_eq + w_eq <= need))
    w_sel, o_sel = prefix(sel)
    pos_ref[0] = jnp.where(sel, o_sel + w_sel - 1.0, -1.0).astype(I32)
    off_ref[0] = o_sel.astype(I32)


def _select(aff_t, cap):
    ne, n = aff_t.shape
    rows = n // LANES
    upper = (jnp.arange(LANES)[:, None] <= jnp.arange(LANES)[None, :]).astype(BF16)
    lower = (jnp.arange(rows)[None, :] < jnp.arange(rows)[:, None]).astype(BF16)
    blk = pl.BlockSpec((1, rows, LANES), lambda e: (e, 0, 0))
    pos, off = pl.pallas_call(
        functools.partial(_select_body, cap=cap),
        grid=(ne,),
        in_specs=[blk, pl.BlockSpec((LANES, LANES), lambda e: (0, 0)),
                  pl.BlockSpec((rows, rows), lambda e: (0, 0))],
        out_specs=[blk, blk],
        out_shape=[jax.ShapeDtypeStruct((ne, rows, LANES), I32),
                   jax.ShapeDtypeStruct((ne, rows, LANES), I32)],
        compiler_params=_params(("parallel",), 32),
        name="ec_select",
    )(aff_t.reshape(ne, rows, LANES), upper, lower)
    return pos.reshape(ne, n), off[:, :, 0]


def _ffn_body(j0_ref, pos_ref, aff_ref, x_ref, wg_ref, wu_ref, wd_ref, y_ref, xs_ref, gs_ref, *, nb, nt):
    e = pl.program_id(0)
    b = pl.program_id(1)

    @pl.when(b == 0)
    def _():
        xs_ref[...] = jnp.zeros(xs_ref.shape, F32)
        gs_ref[...] = jnp.zeros(gs_ref.shape, F32)

    pos = pos_ref[0]
    aff = aff_ref[0]
    xb = x_ref[...]
    slot = lax.broadcasted_iota(I32, (SLOT_TILE, TOK_BLOCK), 0)
    j0 = j0_ref[e * nb + b]
    for d in range(2):
        jt = j0 + d
        jc = jnp.minimum(jt, nt - 1)
        hit = ((pos - jt * SLOT_TILE) == slot) & (jt < nt)
        xs_ref[jc] += jnp.dot(hit.astype(BF16), xb, preferred_element_type=F32)
        gs_ref[jc] += jnp.sum(jnp.where(hit, aff, 0.0), axis=1, keepdims=True)

    @pl.when(b == nb - 1)
    def _():
        for j in range(nt):
            xt = xs_ref[j].astype(BF16)
            g = jnp.dot(xt, wg_ref[0], preferred_element_type=F32)
            u = jnp.dot(xt, wu_ref[0], preferred_element_type=F32)
            h = (g * jax.nn.sigmoid(g) * u).astype(BF16)
            y = jnp.dot(h, wd_ref[0], preferred_element_type=F32) * gs_ref[j]
            y_ref[0, j] = y.astype(BF16)


def _ffn(j0, pos3, aff3, xn, wg, wu, wd, cap):
    ne = wg.shape[0]
    n = xn.shape[0]
    nb = n // TOK_BLOCK
    nt = cap // SLOT_TILE
    grid_spec = pltpu.PrefetchScalarGridSpec(
        num_scalar_prefetch=1,
        grid=(ne, nb),
        in_specs=[pl.BlockSpec((1, 1, TOK_BLOCK), lambda e, b, j0: (e, 0, b)),
                  pl.BlockSpec((1, 1, TOK_BLOCK), lambda e, b, j0: (e, 0, b)),
                  pl.BlockSpec((TOK_BLOCK, D_MODEL), lambda e, b, j0: (b, 0)),
                  pl.BlockSpec((1, D_MODEL, D_EXPERT), lambda e, b, j0: (e, 0, 0)),
                  pl.BlockSpec((1, D_MODEL, D_EXPERT), lambda e, b, j0: (e, 0, 0)),
                  pl.BlockSpec((1, D_EXPERT, D_MODEL), lambda e, b, j0: (e, 0, 0))],
        out_specs=pl.BlockSpec((1, nt, SLOT_TILE, D_MODEL), lambda e, b, j0: (e, 0, 0, 0)),
        scratch_shapes=[pltpu.VMEM((nt, SLOT_TILE, D_MODEL), F32),
                        pltpu.VMEM((nt, SLOT_TILE, 1), F32)])
    return pl.pallas_call(
        functools.partial(_ffn_body, nb=nb, nt=nt),
        grid_spec=grid_spec,
        out_shape=jax.ShapeDtypeStruct((ne, nt, SLOT_TILE, D_MODEL), BF16),
        compiler_params=_params(("parallel", "arbitrary"), 48),
        name="ec_ffn",
    )(j0, pos3, aff3, xn, wg, wu, wd)


def _combine_body(j0_ref, post_ref, xm_ref, y0_ref, y1_ref, o_ref, *, nb, nt):
    b = pl.program_id(0)
    e = pl.program_id(1)

    @pl.when(e == 0)
    def _():
        o_ref[...] = xm_ref[...]

    post = post_ref[...]
    lane = lax.broadcasted_iota(I32, post.shape, 1)
    col = jnp.sum(jnp.where(lane == e, post, 0), axis=1, keepdims=True)
    slot = lax.broadcasted_iota(I32, (TOK_BLOCK, SLOT_TILE), 1)
    j0 = j0_ref[e * nb + b]
    for d, y_ref in enumerate((y0_ref, y1_ref)):
        jt = j0 + d
        hit = ((col - jt * SLOT_TILE) == slot) & (jt < nt)
        o_ref[...] += jnp.dot(hit.astype(BF16), y_ref[0, 0], preferred_element_type=F32)


def _combine(j0, pos_t, xm, y4):
    ne, nt = y4.shape[0], y4.shape[1]
    n = xm.shape[0]
    nb = n // TOK_BLOCK

    def ymap(d):
        return lambda b, e, j0: (e, jnp.minimum(j0[e * nb + b] + d, nt - 1), 0, 0)

    grid_spec = pltpu.PrefetchScalarGridSpec(
        num_scalar_prefetch=1,
        grid=(nb, ne),
        in_specs=[pl.BlockSpec((TOK_BLOCK, ne), lambda b, e, j0: (b, 0)),
                  pl.BlockSpec((TOK_BLOCK, D_MODEL), lambda b, e, j0: (b, 0)),
                  pl.BlockSpec((1, 1, SLOT_TILE, D_MODEL), ymap(0)),
                  pl.BlockSpec((1, 1, SLOT_TILE, D_MODEL), ymap(1))],
        out_specs=pl.BlockSpec((TOK_BLOCK, D_MODEL), lambda b, e, j0: (b, 0)))
    return pl.pallas_call(
        functools.partial(_combine_body, nb=nb, nt=nt),
        grid_spec=grid_spec,
        out_shape=jax.ShapeDtypeStruct((n, D_MODEL), F32),
        compiler_params=_params(("parallel", "arbitrary"), 32),
        name="ec_combine",
    )(j0, pos_t, xm, y4, y4)


def _ec_moe(xm, xn, aff_t, wg, wu, wd):
    ne, n = aff_t.shape
    cap = max(1, EC_CAPACITY * n // N_EXPERTS)
    nb = n // TOK_BLOCK
    pos, off = _select(aff_t, cap)
    rows_per_block = TOK_BLOCK // LANES
    j0 = (off[:, ::rows_per_block] // SLOT_TILE).reshape(ne * nb).astype(I32)
    y4 = _ffn(j0, pos.reshape(ne, 1, n), aff_t.reshape(ne, 1, n), xn, wg, wu, wd, cap)
    return _combine(j0, pos.T, xm, y4)


def _slot_cols(w, head_width, offset=0):
    k = w.shape[0]
    w3 = w.reshape(k, N_HEADS, head_width)
    return jnp.pad(w3, ((0, 0), (0, 0), (offset, SLOT - head_width - offset))).reshape(k, N_HEADS * SLOT)


def _attn_weights(q_a_norm, w_uq, kv_a_norm, w_ukv, q_norm, k_norm):
    wkv = w_ukv.reshape(KV_LORA, N_HEADS, QK_NOPE + V_HEAD)
    wk = _slot_cols(wkv[:, :, :QK_NOPE].reshape(KV_LORA, -1), QK_NOPE)
    wv3 = wkv[:, :, QK_NOPE:]
    odd = (jnp.arange(N_HEADS) % 2 == 1)[None, :, None]
    wv_lo = jnp.pad(wv3, ((0, 0), (0, 0), (0, SLOT - V_HEAD)))
    wv_hi = jnp.pad(wv3, ((0, 0), (0, 0), (SLOT - V_HEAD, 0)))
    wv = jnp.where(odd, wv_hi, wv_lo).reshape(KV_LORA, N_HEADS * SLOT)
    pad_gain = lambda g: jnp.pad(g, (0, SLOT - QK_HEAD))[None, :]
    return dict(wq=_slot_cols(w_uq, QK_HEAD).astype(BF16), wk=wk.astype(BF16), wv=wv.astype(BF16),
                qa=q_a_norm[None, :], kva=kv_a_norm[None, :], qg=pad_gain(q_norm), kg=pad_gain(k_norm))


def _rope_tables(seq):
    pos = jnp.arange(seq, dtype=F32)
    inv_freq = ROPE_THETA ** (-jnp.arange(0, QK_ROPE, 2, dtype=F32) / QK_ROPE)
    ang = pos[:, None] * inv_freq
    ang = jnp.concatenate([ang, ang], axis=-1)
    pad = lambda t, fill: jnp.concatenate(
        [jnp.full((seq, QK_NOPE), fill, F32), t, jnp.full((seq, SLOT - QK_HEAD), fill, F32)], axis=1)
    return pad(jnp.cos(ang), 1.0), pad(jnp.sin(ang), 0.0)


def _filter_weights(w1, b1, w2, b2, w3, freq, decay):
    hid = FILTER_HIDDEN
    padc = lambda a: jnp.pad(a, ((0, 0), (0, LANES - a.shape[1])))
    pad2 = lambda a: jnp.pad(a, ((0, LANES - a.shape[0]), (0, LANES - a.shape[1])))
    bands = jnp.linspace(1e-4, FILTER_BANDS - 1, FILTER_BANDS, dtype=F32)
    cols = HYENA_ORDER * D_HYENA
    w3d = jnp.transpose(w3.reshape(hid, N_DIR, cols), (1, 0, 2))
    return dict(bands=padc(bands[None, :]), w1t=padc(w1[0:1]), w1c=pad2(w1[1:1 + FILTER_BANDS]),
                w1s=pad2(w1[1 + FILTER_BANDS:]), b1=padc(b1[None, :]), w2=pad2(w2), b2=padc(b2[None, :]),
                freq=padc(freq[None, :]), w3=jnp.pad(w3d, ((0, 0), (0, LANES - hid), (0, 0))),
                decay=decay.reshape(N_DIR, 1, cols))


def _trunk(x, p):
    bsz, seq, _ = x.shape
    n = bsz * seq
    cos_t, sin_t = _rope_tables(seq)
    x2 = x.reshape(n, D_MODEL)
    for l in range(DEPTH):
        uh, cq, ckvr = _inproj(x2, p["attn_norm"][l][None, :], p["w_in"][l])
        y_h = _hyena(uh.reshape(bsz, seq, COL_HYENA), p["conv_w"][l], p["conv_b"][l][None, :],
                     p["filt"][l], p["hyena_bias"][l])
        q, k, v = _qkv(cq.reshape(bsz, seq, -1), ckvr.reshape(bsz, seq, -1), cos_t, sin_t, p["attn"][l])
        y_a = _flash(q, k, v)
        xm, xn, aff_t = _outproj(y_h.reshape(n, D_HYENA), y_a.reshape(n, D_ATTN), x2,
                                 p["out_norm"][l][None, :], p["w_out"][l], p["mlp_norm"][l][None, :],
                                 p["w_router"][l])
        x2 = _ec_moe(xm, xn, aff_t, p["w_gate"][l], p["w_up"][l], p["w_down"][l])
    return x2.reshape(bsz, seq, D_MODEL)


def kernel(x_prompt, x_sample, attn_norm, w_in, conv_w, conv_b, filt_w1, filt_b1, filt_w2, filt_b2, filt_w3,
           filt_freq, filt_decay, hyena_bias, q_a_norm, w_uq, kv_a_norm, w_ukv, q_norm, k_norm, out_norm,
           w_out, mlp_norm, w_router, w_gate, w_up, w_down):
    p = dict(
        attn_norm=attn_norm,
        w_in=jnp.pad(w_in, ((0, 0), (0, 0), (0, D_IN_PAD - D_IN))).astype(BF16),
        conv_w=conv_w, conv_b=conv_b, hyena_bias=hyena_bias,
        filt=[_filter_weights(filt_w1[l], filt_b1[l], filt_w2[l], filt_b2[l], filt_w3[l], filt_freq[l],
                              filt_decay[l]) for l in range(DEPTH)],
        attn=[_attn_weights(q_a_norm[l], w_uq[l], kv_a_norm[l], w_ukv[l], q_norm[l], k_norm[l])
              for l in range(DEPTH)],
        out_norm=out_norm, w_out=w_out.astype(BF16), mlp_norm=mlp_norm,
        w_router=jnp.pad(w_router, ((0, 0), (0, 0), (0, LANES - N_EXPERTS))),
        w_gate=w_gate.astype(BF16), w_up=w_up.astype(BF16), w_down=w_down.astype(BF16))
    return (_trunk(x_prompt, p), _trunk(x_sample, p))
```

```python
import functools
import math

import jax
import jax.numpy as jnp
from jax import lax
from jax.experimental import pallas as pl
from jax.experimental.pallas import tpu as pltpu

F32 = jnp.float32
BF16 = jnp.bfloat16
I32 = jnp.int32
HIGHEST = lax.Precision.HIGHEST

D_MODEL = 1024
DEPTH = 2
D_HYENA = 512
HYENA_ORDER = 2
FILTER_BANDS = 16
FILTER_HIDDEN = 64
N_DIR = 2
DECAY_SHIFT = 0.05
N_HEADS = 8
QK_NOPE = 64
QK_ROPE = 32
QK_HEAD = QK_NOPE + QK_ROPE
V_HEAD = 64
D_ATTN = N_HEADS * V_HEAD
Q_LORA = 256
KV_LORA = 128
ROPE_THETA = 10000.0
N_EXPERTS = 16
EC_CAPACITY = 2
D_EXPERT = 512
EPS = 1e-6
COL_HYENA = (HYENA_ORDER + 1) * D_HYENA
COL_Q = COL_HYENA + Q_LORA
COL_KV = COL_Q + KV_LORA
D_IN = COL_KV + QK_ROPE
D_IN_PAD = 2048

LANES = 128
SLOT = 128
TOK_BLOCK = 256
SLOT_TILE = 256
ATT_TQ = 512
ATT_QC = 256
ATT_TK = 256
MIB = 1024 * 1024


def _params(sem, vmem_mib):
    return pltpu.CompilerParams(dimension_semantics=sem, vmem_limit_bytes=vmem_mib * MIB)


def _rms(x):
    return x * lax.rsqrt(jnp.mean(x * x, axis=-1, keepdims=True) + EPS)


def _inproj_body(x_ref, g_ref, w_ref, uh_ref, cq_ref, ckvr_ref):
    xn = _rms(x_ref[...]) * g_ref[...]
    u = jnp.dot(xn.astype(BF16), w_ref[...], preferred_element_type=F32)
    uh_ref[...] = u[:, :COL_HYENA]
    cq_ref[...] = u[:, COL_HYENA:COL_Q]
    ckvr_ref[...] = u[:, COL_Q:]


def _inproj(x2d, g, w_pad):
    n = x2d.shape[0]
    tm = min(512, n)
    return pl.pallas_call(
        _inproj_body,
        grid=(n // tm,),
        in_specs=[pl.BlockSpec((tm, D_MODEL), lambda i: (i, 0)),
                  pl.BlockSpec((1, D_MODEL), lambda i: (0, 0)),
                  pl.BlockSpec((D_MODEL, D_IN_PAD), lambda i: (0, 0))],
        out_specs=[pl.BlockSpec((tm, COL_HYENA), lambda i: (i, 0)),
                   pl.BlockSpec((tm, Q_LORA), lambda i: (i, 0)),
                   pl.BlockSpec((tm, D_IN_PAD - COL_Q), lambda i: (i, 0))],
        out_shape=[jax.ShapeDtypeStruct((n, COL_HYENA), F32),
                   jax.ShapeDtypeStruct((n, Q_LORA), F32),
                   jax.ShapeDtypeStruct((n, D_IN_PAD - COL_Q), F32)],
        compiler_params=_params(("parallel",), 40),
        name="inproj",
    )(x2d, g, w_pad)


def _sconv_body(u_ref, prev_ref, next_ref, w_ref, b_ref, v_ref, x1_ref, x2_ref):
    i = pl.program_id(1)
    last = pl.num_programs(1) - 1
    u = u_ref[0]
    tl = u.shape[0]
    prev_row = jnp.where(i == 0, 0.0, prev_ref[0, 7:8, :])
    next_row = jnp.where(i == last, 0.0, next_ref[0, 0:1, :])
    row = lax.broadcasted_iota(I32, u.shape, 0)
    up = jnp.where(row == 0, prev_row, pltpu.roll(u, 1, axis=0))
    dn = jnp.where(row == tl - 1, next_row, pltpu.roll(u, tl - 1, axis=0))
    y = up * w_ref[0:1, :] + u * w_ref[1:2, :] + dn * w_ref[2:3, :] + b_ref[...]
    v_ref[0] = y[:, :D_HYENA]
    x1_ref[0] = y[:, D_HYENA:2 * D_HYENA]
    x2_ref[0] = y[:, 2 * D_HYENA:]


def _sconv(uh, w, b):
    bsz, seq, c = uh.shape
    tl = min(512, seq)
    r = tl // 8
    nblk8 = seq // 8
    out = jax.ShapeDtypeStruct((bsz, seq, D_HYENA), F32)
    ospec = pl.BlockSpec((1, tl, D_HYENA), lambda bi, i: (bi, i, 0))
    return pl.pallas_call(
        _sconv_body,
        grid=(bsz, seq // tl),
        in_specs=[pl.BlockSpec((1, tl, c), lambda bi, i: (bi, i, 0)),
                  pl.BlockSpec((1, 8, c), lambda bi, i: (bi, jnp.maximum(i * r - 1, 0), 0)),
                  pl.BlockSpec((1, 8, c), lambda bi, i: (bi, jnp.minimum((i + 1) * r, nblk8 - 1), 0)),
                  pl.BlockSpec((3, c), lambda bi, i: (0, 0)),
                  pl.BlockSpec((1, c), lambda bi, i: (0, 0))],
        out_specs=[ospec, ospec, ospec],
        out_shape=[out, out, out],
        compiler_params=_params(("parallel", "parallel"), 40),
        name="sconv",
    )(uh, uh, uh, w, b)


def _filter_body(bands_ref, w1t_ref, w1c_ref, w1s_ref, b1_ref, w2_ref, b2_ref, fr_ref, w3_ref, dec_ref,
                 k_ref, sum_ref, *, seq):
    i = pl.program_id(0)
    tr = k_ref.shape[0]
    n = i * tr + lax.broadcasted_iota(I32, (tr, 1), 0)
    t_idx = jnp.where(n < seq, n, 2 * seq - n).astype(F32)
    t = t_idx / (seq - 1)
    ang = 2.0 * math.pi * bands_ref[...] * t_idx / seq
    fr = fr_ref[...]
    pre = (t * w1t_ref[...]
           + jnp.dot(jnp.cos(ang), w1c_ref[...], precision=HIGHEST, preferred_element_type=F32)
           - jnp.dot(jnp.sin(ang), w1s_ref[...], precision=HIGHEST, preferred_element_type=F32))
    h = jnp.sin(fr * (pre + b1_ref[...]))
    h = jnp.sin(fr * (jnp.dot(h, w2_ref[...], precision=HIGHEST, preferred_element_type=F32) + b2_ref[...]))
    h = jnp.dot(h, w3_ref[0], precision=HIGHEST, preferred_element_type=F32)
    window = jnp.exp(-t * dec_ref[0]) + DECAY_SHIFT
    k = jnp.where(n == seq, 0.0, h * window)
    k_ref[...] = k

    @pl.when(i == 0)
    def _():
        sum_ref[...] = jnp.zeros_like(sum_ref)

    sum_ref[...] += jnp.sum(jnp.abs(k), axis=0, keepdims=True)


def _filters(seq, fw):
    cols = HYENA_ORDER * D_HYENA
    tr = min(512, seq)
    half = seq // tr
    const = lambda shape: pl.BlockSpec(shape, lambda i: (0,) * len(shape))
    return pl.pallas_call(
        functools.partial(_filter_body, seq=seq),
        grid=(2 * seq // tr,),
        in_specs=[const((1, LANES)), const((1, LANES)), const((LANES, LANES)), const((LANES, LANES)),
                  const((1, LANES)), const((LANES, LANES)), const((1, LANES)), const((1, LANES)),
                  pl.BlockSpec((1, LANES, cols), lambda i: (i // half, 0, 0)),
                  pl.BlockSpec((1, 1, cols), lambda i: (i // half, 0, 0))],
        out_specs=[pl.BlockSpec((tr, cols), lambda i: (i, 0)),
                   pl.BlockSpec((1, cols), lambda i: (0, 0))],
        out_shape=[jax.ShapeDtypeStruct((2 * seq, cols), F32),
                   jax.ShapeDtypeStruct((1, cols), F32)],
        compiler_params=_params(("arbitrary",), 32),
        name="filter_gen",
    )(fw["bands"], fw["w1t"], fw["w1c"], fw["w1s"], fw["b1"], fw["w2"], fw["b2"], fw["freq"],
      fw["w3"], fw["decay"])


def _fft_dims(seq):
    n2 = 128 if 2 * seq >= 32768 else 64
    n1 = 2 * seq // n2
    return n1, n2


def _dft_tables(n1, n2):
    n = n1 * n2
    n1h = n1 // 2
    k1 = jnp.arange(n1h, dtype=I32)[:, None]
    m1 = jnp.arange(n1, dtype=I32)[None, :]
    ang = (2.0 * math.pi / n1) * ((k1 * m1) % n1).astype(F32)
    top = jnp.cos(ang)
    bot = -jnp.sin(ang)
    nyq = jnp.where(m1 % 2 == 0, 1.0, -1.0).astype(F32)
    bot = jnp.concatenate([nyq, bot[1:]], axis=0)
    fa = jnp.concatenate([top, bot], axis=0)
    weight = jnp.where((jnp.arange(n1) % n1h) == 0, 1.0, 2.0).astype(F32) / n
    fi = (fa[:, :n1h] * weight[:, None]).T

    kk = jnp.arange(n1h + 1, dtype=I32)[:, None, None]
    k2 = jnp.arange(n2, dtype=I32)[None, :, None]
    m2 = jnp.arange(n2, dtype=I32)[None, None, :]
    phi = (2.0 * math.pi / n) * ((m2 * (kk + n1 * k2)) % n).astype(F32)
    gr = jnp.cos(phi)
    gi = -jnp.sin(phi)
    blk = jnp.concatenate([jnp.concatenate([gr, -gi], axis=2),
                           jnp.concatenate([gi, gr], axis=2)], axis=1)
    left = (jnp.arange(2 * n2) < n2)[None, :]
    g0 = jnp.where(left, blk[0], 0.0)
    gf = jnp.concatenate([g0[None], blk[1:n1h]], axis=0)
    gnyq = jnp.concatenate([jnp.zeros((2 * n2, n2), F32), blk[n1h][:, :n2]], axis=1)
    return dict(fa_full=fa.astype(BF16), fa_half=fa[:, :n1h].astype(BF16), fi=fi.astype(BF16),
                gf=gf.astype(BF16), gnyq=gnyq.astype(BF16),
                ginv=jnp.transpose(gf, (0, 2, 1)).astype(BF16), m2=gnyq.T.astype(BF16))


def _fft_a_body(x_ref, f_ref, o_ref):
    a = jnp.dot(f_ref[...], x_ref[0].astype(BF16), preferred_element_type=F32)
    n1h = o_ref.shape[2]
    o_ref[0, 0] = a[:n1h].astype(BF16)
    o_ref[0, 1] = a[n1h:].astype(BF16)


def _fft_a(x3, fmat):
    bsz, r, nc = x3.shape
    n1 = fmat.shape[0]
    tn = min(2048, nc)
    return pl.pallas_call(
        _fft_a_body,
        grid=(bsz, nc // tn),
        in_specs=[pl.BlockSpec((1, r, tn), lambda b, j: (b, 0, j)),
                  pl.BlockSpec((n1, r), lambda b, j: (0, 0))],
        out_specs=pl.BlockSpec((1, 2, n1 // 2, tn), lambda b, j: (b, 0, 0, j)),
        out_shape=jax.ShapeDtypeStruct((bsz, 2, n1 // 2, nc), BF16),
        compiler_params=_params(("parallel", "parallel"), 32),
        name="fft_stage_a",
    )(x3, fmat)


def _fft_b_body(ar_ref, ai_ref, g_ref, gn_ref, sc_ref, x_ref, xn_ref, *, kb):
    j = pl.program_id(2)
    inv = 1.0 / sc_ref[...]
    for kk in range(kb):
        rhs = jnp.concatenate([ar_ref[0, 0, kk], ai_ref[0, 0, kk]], axis=0)
        x_ref[0, kk] = jnp.dot(g_ref[kk], rhs, preferred_element_type=F32) * inv

    @pl.when(j == 0)
    def _():
        rhs = jnp.concatenate([ar_ref[0, 0, 0], ai_ref[0, 0, 0]], axis=0)
        xn_ref[0] = jnp.dot(gn_ref[...], rhs, preferred_element_type=F32) * inv


def _fft_b(a5, tabs, scale):
    bsz, _, n1h, n2, c = a5.shape
    kb = 8
    tc = min(512, c)
    return pl.pallas_call(
        functools.partial(_fft_b_body, kb=kb),
        grid=(bsz, c // tc, n1h // kb),
        in_specs=[pl.BlockSpec((1, 1, kb, n2, tc), lambda b, ci, j: (b, 0, j, 0, ci)),
                  pl.BlockSpec((1, 1, kb, n2, tc), lambda b, ci, j: (b, 1, j, 0, ci)),
                  pl.BlockSpec((kb, 2 * n2, 2 * n2), lambda b, ci, j: (j, 0, 0)),
                  pl.BlockSpec((2 * n2, 2 * n2), lambda b, ci, j: (0, 0)),
                  pl.BlockSpec((1, tc), lambda b, ci, j: (0, ci))],
        out_specs=[pl.BlockSpec((1, kb, 2 * n2, tc), lambda b, ci, j: (b, j, 0, ci)),
                   pl.BlockSpec((1, 2 * n2, tc), lambda b, ci, j: (b, 0, ci))],
        out_shape=[jax.ShapeDtypeStruct((bsz, n1h, 2 * n2, c), F32),
                   jax.ShapeDtypeStruct((bsz, 2 * n2, c), F32)],
        compiler_params=_params(("parallel", "parallel", "arbitrary"), 40),
        name="fft_stage_b",
    )(a5, a5, tabs["gf"], tabs["gnyq"], scale)


def _cmul(x, k, n2):
    xr, xi = x[:n2], x[n2:]
    kr, ki = k[:n2], k[n2:]
    return jnp.concatenate([xr * kr - xi * ki, xr * ki + xi * kr], axis=0).astype(BF16)


def _ifft_b_body(x_ref, xn_ref, k_ref, kn_ref, g_ref, m2_ref, o_ref, *, kb):
    j = pl.program_id(1)
    n2 = o_ref.shape[3]
    for kk in range(kb):
        acc = jnp.dot(g_ref[kk], _cmul(x_ref[0, kk], k_ref[0, kk], n2), preferred_element_type=F32)
        o_ref[0, 0, kk] = acc[:n2].astype(BF16)
        o_ref[0, 1, kk] = acc[n2:].astype(BF16)

    @pl.when(j == 0)
    def _():
        acc = jnp.dot(m2_ref[...], _cmul(xn_ref[0], kn_ref[0], n2), preferred_element_type=F32)
        o_ref[0, 1, 0] = acc[n2:].astype(BF16)


def _ifft_b(xs, xnyq, kf, kfnyq, order, tabs):
    bsz, n1h, tn2, c = xs.shape
    n2 = tn2 // 2
    kb = 8
    return pl.pallas_call(
        functools.partial(_ifft_b_body, kb=kb),
        grid=(bsz, n1h // kb),
        in_specs=[pl.BlockSpec((1, kb, tn2, c), lambda b, j: (b, j, 0, 0)),
                  pl.BlockSpec((1, tn2, c), lambda b, j: (b, 0, 0)),
                  pl.BlockSpec((1, kb, tn2, c), lambda b, j: (0, j, 0, order)),
                  pl.BlockSpec((1, tn2, c), lambda b, j: (0, 0, order)),
                  pl.BlockSpec((kb, tn2, tn2), lambda b, j: (j, 0, 0)),
                  pl.BlockSpec((tn2, tn2), lambda b, j: (0, 0))],
        out_specs=pl.BlockSpec((1, 2, kb, n2, c), lambda b, j: (b, 0, j, 0, 0)),
        out_shape=jax.ShapeDtypeStruct((bsz, 2, n1h, n2, c), BF16),
        compiler_params=_params(("parallel", "arbitrary"), 40),
        name="ifft_stage_b",
    )(xs, xnyq, kf, kfnyq, tabs["ginv"], tabs["m2"])


def _ifft_a_body(b_ref, f_ref, z_ref, gate_ref, bias_ref, o_ref):
    y = jnp.dot(f_ref[...], b_ref[0], preferred_element_type=F32)
    o_ref[0] = gate_ref[0] * (y + z_ref[0] * bias_ref[...])


def _ifft_a(b3, fi, z3, gate3, bias_t):
    bsz, n1, nc = b3.shape
    n1h = n1 // 2
    tn = bias_t.shape[1]
    return pl.pallas_call(
        _ifft_a_body,
        grid=(bsz, nc // tn),
        in_specs=[pl.BlockSpec((1, n1, tn), lambda b, j: (b, 0, j)),
                  pl.BlockSpec((n1h, n1), lambda b, j: (0, 0)),
                  pl.BlockSpec((1, n1h, tn), lambda b, j: (b, 0, j)),
                  pl.BlockSpec((1, n1h, tn), lambda b, j: (b, 0, j)),
                  pl.BlockSpec((1, tn), lambda b, j: (0, 0))],
        out_specs=pl.BlockSpec((1, n1h, tn), lambda b, j: (b, 0, j)),
        out_shape=jax.ShapeDtypeStruct((bsz, n1h, nc), F32),
        compiler_params=_params(("parallel", "parallel"), 32),
        name="ifft_stage_a",
    )(b3, fi, z3, gate3, bias_t)


def _hyena(uh, conv_w, conv_b, fw, bias):
    bsz, seq, _ = uh.shape
    c = D_HYENA
    n1, n2 = _fft_dims(seq)
    n1h = n1 // 2
    tabs = _dft_tables(n1, n2)
    v, x1, x2 = _sconv(uh, conv_w, conv_b)

    kcirc, ksum = _filters(seq, fw)
    cols = HYENA_ORDER * c
    ka = _fft_a(kcirc.reshape(1, n1, n2 * cols), tabs["fa_full"])
    kf, kfnyq = _fft_b(ka.reshape(1, 2, n1h, n2, cols), tabs, ksum)

    tn = min(2048, n2 * c)
    ones = jnp.ones((1, c), F32)
    z = v
    for order, gate in enumerate((x1, x2)):
        za = _fft_a(z.reshape(bsz, n1h, n2 * c), tabs["fa_half"])
        zs, znyq = _fft_b(za.reshape(bsz, 2, n1h, n2, c), tabs, ones)
        zb = _ifft_b(zs, znyq, kf, kfnyq, order, tabs)
        bias_t = jnp.tile(bias[order][None, :], (1, tn // c))
        z = _ifft_a(zb.reshape(bsz, n1, n2 * c), tabs["fi"], z.reshape(bsz, n1h, n2 * c),
                    gate.reshape(bsz, n1h, n2 * c), bias_t).reshape(bsz, seq, c)
    return z


def _qkv_body(cq_ref, ckvr_ref, cos_ref, sin_ref, wq_ref, wk_ref, wv_ref, qa_ref, kva_ref, qg_ref, kg_ref,
              qt_ref, k_ref, vt_ref):
    cqn = _rms(cq_ref[0]) * qa_ref[...]
    qs = jnp.dot(cqn.astype(BF16), wq_ref[...], preferred_element_type=F32)
    ck = ckvr_ref[0]
    ckvn = (_rms(ck[:, :KV_LORA]) * kva_ref[...]).astype(BF16)
    ks = jnp.dot(ckvn, wk_ref[...], preferred_element_type=F32)
    vs = jnp.dot(ckvn, wv_ref[...], preferred_element_type=F32)
    krope = pltpu.roll(ck[:, KV_LORA:], QK_NOPE, axis=1)
    cos = cos_ref[...]
    sin = sin_ref[...]
    lane = lax.broadcasted_iota(I32, cos.shape, 1)
    half = QK_ROPE // 2
    first = (lane >= QK_NOPE) & (lane < QK_NOPE + half)
    second = (lane >= QK_NOPE + half) & (lane < QK_HEAD)

    def head(xh, gain, scale):
        ms = jnp.sum(xh * xh, axis=-1, keepdims=True) * (1.0 / QK_HEAD)
        xn = xh * lax.rsqrt(ms + EPS) * gain
        rot = jnp.where(first, -pltpu.roll(xn, SLOT - half, axis=1),
                        jnp.where(second, pltpu.roll(xn, half, axis=1), 0.0))
        return (xn * cos + rot * sin) * scale

    q_scale = QK_HEAD ** -0.5 * math.log2(math.e)
    for h in range(N_HEADS):
        sl = slice(h * SLOT, (h + 1) * SLOT)
        qt_ref[0, h] = head(qs[:, sl], qg_ref[...], q_scale).T.astype(BF16)
        k_ref[0, h] = head(ks[:, sl] + krope, kg_ref[...], 1.0).astype(BF16)
    for hp in range(N_HEADS // 2):
        vt_ref[0, hp] = vs[:, hp * 2 * V_HEAD:(hp + 1) * 2 * V_HEAD].T.astype(BF16)


def _qkv(cq, ckvr, cos_t, sin_t, aw):
    bsz, seq, _ = cq.shape
    tm = min(512, seq)
    const = lambda shape: pl.BlockSpec(shape, lambda b, i: (0,) * len(shape))
    return pl.pallas_call(
        _qkv_body,
        grid=(bsz, seq // tm),
        in_specs=[pl.BlockSpec((1, tm, Q_LORA), lambda b, i: (b, i, 0)),
                  pl.BlockSpec((1, tm, D_IN_PAD - COL_Q), lambda b, i: (b, i, 0)),
                  pl.BlockSpec((tm, SLOT), lambda b, i: (i, 0)),
                  pl.BlockSpec((tm, SLOT), lambda b, i: (i, 0)),
                  const((Q_LORA, N_HEADS * SLOT)), const((KV_LORA, N_HEADS * SLOT)),
                  const((KV_LORA, D_ATTN)),
                  const((1, Q_LORA)), const((1, KV_LORA)), const((1, SLOT)), const((1, SLOT))],
        out_specs=[pl.BlockSpec((1, N_HEADS, SLOT, tm), lambda b, i: (b, 0, 0, i)),
                   pl.BlockSpec((1, N_HEADS, tm, SLOT), lambda b, i: (b, 0, i, 0)),
                   pl.BlockSpec((1, N_HEADS // 2, 2 * V_HEAD, tm), lambda b, i: (b, 0, 0, i))],
        out_shape=[jax.ShapeDtypeStruct((bsz, N_HEADS, SLOT, seq), BF16),
                   jax.ShapeDtypeStruct((bsz, N_HEADS, seq, SLOT), BF16),
                   jax.ShapeDtypeStruct((bsz, N_HEADS // 2, 2 * V_HEAD, seq), BF16)],
        compiler_params=_params(("parallel", "parallel"), 40),
        name="qkv_prep",
    )(cq, ckvr, cos_t, sin_t, aw["wq"], aw["wk"], aw["wv"], aw["qa"], aw["kva"], aw["qg"], aw["kg"])


def _flash_body(qt_ref, k_ref, vt_ref, o_ref, s0_ref, s1_ref, acc_ref, *, tk, nk, nqc):
    chains = [(hh, qc) for hh in range(2) for qc in range(nqc)]
    acc_ref[...] = jnp.zeros(acc_ref.shape, F32)

    def qk(t, s_ref):
        ks = pl.multiple_of(t * tk, tk)
        tile_max = []
        for c, (hh, qc) in enumerate(chains):
            k = k_ref[0, hh, pl.ds(ks, tk), :]
            s = jnp.dot(k, qt_ref[0, hh, :, qc * ATT_QC:(qc + 1) * ATT_QC], preferred_element_type=F32)
            s_ref[c] = s
            tile_max.append(jnp.max(s, axis=0, keepdims=True))
        return tile_max

    def softmax_pv(t, s_ref, tile_max, m, l):
        ks = pl.multiple_of(t * tk, tk)
        m_out, l_out = [], []
        for c, (hh, qc) in enumerate(chains):
            m_new = jnp.maximum(m[c], tile_max[c])
            a = jnp.exp2(m[c] - m_new)
            p = jnp.exp2(s_ref[c] - m_new)
            l_out.append(a * l[c] + jnp.sum(p, axis=0, keepdims=True))
            m_out.append(m_new)
            vt = vt_ref[0, 0, hh * V_HEAD:(hh + 1) * V_HEAD, pl.ds(ks, tk)]
            acc_ref[c] = acc_ref[c] * a + jnp.dot(vt, p.astype(BF16), preferred_element_type=F32)
        return m_out, l_out

    def pair(u, carry):
        tile_max, m, l = carry
        t = 2 * u
        tile_max1 = qk(t + 1, s1_ref)
        m, l = softmax_pv(t, s0_ref, tile_max, m, l)
        tile_max0 = qk(jnp.minimum(t + 2, nk - 1), s0_ref)
        m, l = softmax_pv(t + 1, s1_ref, tile_max1, m, l)
        return tile_max0, m, l

    nc = len(chains)
    init = (qk(0, s0_ref), [jnp.full((1, ATT_QC), -jnp.inf, F32)] * nc, [jnp.zeros((1, ATT_QC), F32)] * nc)
    _, m, l = lax.fori_loop(0, nk // 2, pair, init)
    heads = [jnp.concatenate([acc_ref[hh * nqc + qc] * (1.0 / l[hh * nqc + qc]) for qc in range(nqc)], axis=1)
             for hh in range(2)]
    o_ref[0] = jnp.concatenate(heads, axis=0).T


def _flash(qt, k, vt):
    bsz, nh, _, seq = qt.shape
    tq = min(ATT_TQ, seq)
    tk = min(ATT_TK, seq)
    nqc = tq // ATT_QC
    nk = seq // tk
    assert nk % 2 == 0
    return pl.pallas_call(
        functools.partial(_flash_body, tk=tk, nk=nk, nqc=nqc),
        grid=(bsz, nh // 2, seq // tq),
        in_specs=[pl.BlockSpec((1, 2, SLOT, tq), lambda b, hp, i: (b, hp, 0, i)),
                  pl.BlockSpec((1, 2, seq, SLOT), lambda b, hp, i: (b, hp, 0, 0)),
                  pl.BlockSpec((1, 1, 2 * V_HEAD, seq), lambda b, hp, i: (b, hp, 0, 0))],
        out_specs=pl.BlockSpec((1, tq, 2 * V_HEAD), lambda b, hp, i: (b, i, hp)),
        out_shape=jax.ShapeDtypeStruct((bsz, seq, nh * V_HEAD), F32),
        scratch_shapes=[pltpu.VMEM((2 * nqc, tk, ATT_QC), F32), pltpu.VMEM((2 * nqc, tk, ATT_QC), F32),
                        pltpu.VMEM((2 * nqc, V_HEAD, ATT_QC), F32)],
        compiler_params=_params(("parallel", "parallel", "parallel"), 48),
        name="flash_attn",
    )(qt, k, vt)


def _outproj_body(yh_ref, ya_ref, x_ref, og_ref, w_ref, mg_ref, wr_ref, xm_ref, xn_ref, aff_ref):
    og = og_ref[...]
    half = D_MODEL // 2
    y = jnp.concatenate([_rms(yh_ref[...]) * og[:, :half], _rms(ya_ref[...]) * og[:, half:]], axis=1)
    xm = x_ref[...] + jnp.dot(y.astype(BF16), w_ref[...], preferred_element_type=F32)
    xm_ref[...] = xm
    xn = _rms(xm) * mg_ref[...]
    xn_ref[...] = xn.astype(BF16)
    logits = jnp.dot(xn, wr_ref[...], precision=HIGHEST, preferred_element_type=F32)
    lane = lax.broadcasted_iota(I32, logits.shape, 1)
    logits = jnp.where(lane < N_EXPERTS, logits, -jnp.inf)
    e = jnp.exp(logits - jnp.max(logits, axis=-1, keepdims=True))
    aff = e / jnp.sum(e, axis=-1, keepdims=True)
    aff_ref[...] = aff.T[:N_EXPERTS]


def _outproj(yh, ya, x2d, og, w_out, mg, wr_pad):
    n = x2d.shape[0]
    tm = min(512, n)
    half = D_MODEL // 2
    const = lambda shape: pl.BlockSpec(shape, lambda i: (0,) * len(shape))
    return pl.pallas_call(
        _outproj_body,
        grid=(n // tm,),
        in_specs=[pl.BlockSpec((tm, half), lambda i: (i, 0)),
                  pl.BlockSpec((tm, half), lambda i: (i, 0)),
                  pl.BlockSpec((tm, D_MODEL), lambda i: (i, 0)),
                  const((1, D_MODEL)), const((D_MODEL, D_MODEL)), const((1, D_MODEL)),
                  const((D_MODEL, LANES))],
        out_specs=[pl.BlockSpec((tm, D_MODEL), lambda i: (i, 0)),
                   pl.BlockSpec((tm, D_MODEL), lambda i: (i, 0)),
                   pl.BlockSpec((N_EXPERTS, tm), lambda i: (0, i))],
        out_shape=[jax.ShapeDtypeStruct((n, D_MODEL), F32),
                   jax.ShapeDtypeStruct((n, D_MODEL), BF16),
                   jax.ShapeDtypeStruct((N_EXPERTS, n), F32)],
        compiler_params=_params(("parallel",), 40),
        name="outproj_router",
    )(yh, ya, x2d, og, w_out, mg, wr_pad)


def _select_body(aff_ref, upper_ref, lower_ref, pos_ref, off_ref, *, cap):
    bits = pltpu.bitcast(aff_ref[0], I32)
    upper = upper_ref[...]
    lower = lower_ref[...]

    def count(mask):
        return jnp.sum(jnp.sum(mask.astype(F32), axis=1, keepdims=True), axis=0, keepdims=True)

    def bit_step(i, thr):
        cand = thr | jnp.left_shift(jnp.int32(1), 30 - i)
        return jnp.where(count(bits >= cand) >= cap, cand, thr)

    thr = lax.fori_loop(0, 31, bit_step, jnp.zeros((1, 1), I32))
    gt = bits > thr
    eq = bits == thr
    need = cap - count(gt)

    def prefix(mask):
        within = jnp.dot(mask.astype(BF16), upper, preferred_element_type=F32)
        total = within[:, LANES - 1:LANES]
        offs = jnp.dot(lower, jnp.broadcast_to(total, within.shape).astype(BF16), preferred_element_type=F32)
        return within, offs

    w_eq, o_eq = prefix(eq)
    sel = gt | (eq & (o_eq + w_eq <= need))
    w_sel, o_sel = prefix(sel)
    pos_ref[0] = jnp.where(sel, o_sel + w_sel - 1.0, -1.0).astype(I32)
    off_ref[0] = o_sel.astype(I32)


def _select(aff_t, cap):
    ne, n = aff_t.shape
    rows = n // LANES
    upper = (jnp.arange(LANES)[:, None] <= jnp.arange(LANES)[None, :]).astype(BF16)
    lower = (jnp.arange(rows)[None, :] < jnp.arange(rows)[:, None]).astype(BF16)
    blk = pl.BlockSpec((1, rows, LANES), lambda e: (e, 0, 0))
    pos, off = pl.pallas_call(
        functools.partial(_select_body, cap=cap),
        grid=(ne,),
        in_specs=[blk, pl.BlockSpec((LANES, LANES), lambda e: (0, 0)),
                  pl.BlockSpec((rows, rows), lambda e: (0, 0))],
        out_specs=[blk, blk],
        out_shape=[jax.ShapeDtypeStruct((ne, rows, LANES), I32),
                   jax.ShapeDtypeStruct((ne, rows, LANES), I32)],
        compiler_params=_params(("parallel",), 32),
        name="ec_select",
    )(aff_t.reshape(ne, rows, LANES), upper, lower)
    return pos.reshape(ne, n), off[:, :, 0]


def _ffn_body(j0_ref, pos_ref, aff_ref, x_ref, wg_ref, wu_ref, wd_ref, y_ref, xs_ref, gs_ref, *, nb, nt):
    e = pl.program_id(0)
    b = pl.program_id(1)

    @pl.when(b == 0)
    def _():
        xs_ref[...] = jnp.zeros(xs_ref.shape, F32)
        gs_ref[...] = jnp.zeros(gs_ref.shape, F32)

    pos = pos_ref[0]
    aff = aff_ref[0]
    xb = x_ref[...]
    slot = lax.broadcasted_iota(I32, (SLOT_TILE, TOK_BLOCK), 0)
    j0 = j0_ref[e * nb + b]
    for d in range(2):
        jt = j0 + d
        jc = jnp.minimum(jt, nt - 1)
        hit = ((pos - jt * SLOT_TILE) == slot) & (jt < nt)
        xs_ref[jc] += jnp.dot(hit.astype(BF16), xb, preferred_element_type=F32)
        gs_ref[jc] += jnp.sum(jnp.where(hit, aff, 0.0), axis=1, keepdims=True)

    @pl.when(b == nb - 1)
    def _():
        for j in range(nt):
            xt = xs_ref[j].astype(BF16)
            g = jnp.dot(xt, wg_ref[0], preferred_element_type=F32)
            u = jnp.dot(xt, wu_ref[0], preferred_element_type=F32)
            h = (g * jax.nn.sigmoid(g) * u).astype(BF16)
            y = jnp.dot(h, wd_ref[0], preferred_element_type=F32) * gs_ref[j]
            y_ref[0, j] = y.astype(BF16)


def _ffn(j0, pos3, aff3, xn, wg, wu, wd, cap):
    ne = wg.shape[0]
    n = xn.shape[0]
    nb = n // TOK_BLOCK
    nt = cap // SLOT_TILE
    grid_spec = pltpu.PrefetchScalarGridSpec(
        num_scalar_prefetch=1,
        grid=(ne, nb),
        in_specs=[pl.BlockSpec((1, 1, TOK_BLOCK), lambda e, b, j0: (e, 0, b)),
                  pl.BlockSpec((1, 1, TOK_BLOCK), lambda e, b, j0: (e, 0, b)),
                  pl.BlockSpec((TOK_BLOCK, D_MODEL), lambda e, b, j0: (b, 0)),
                  pl.BlockSpec((1, D_MODEL, D_EXPERT), lambda e, b, j0: (e, 0, 0)),
                  pl.BlockSpec((1, D_MODEL, D_EXPERT), lambda e, b, j0: (e, 0, 0)),
                  pl.BlockSpec((1, D_EXPERT, D_MODEL), lambda e, b, j0: (e, 0, 0))],
        out_specs=pl.BlockSpec((1, nt, SLOT_TILE, D_MODEL), lambda e, b, j0: (e, 0, 0, 0)),
        scratch_shapes=[pltpu.VMEM((nt, SLOT_TILE, D_MODEL), F32),
                        pltpu.VMEM((nt, SLOT_TILE, 1), F32)])
    return pl.pallas_call(
        functools.partial(_ffn_body, nb=nb, nt=nt),
        grid_spec=grid_spec,
        out_shape=jax.ShapeDtypeStruct((ne, nt, SLOT_TILE, D_MODEL), BF16),
        compiler_params=_params(("parallel", "arbitrary"), 48),
        name="ec_ffn",
    )(j0, pos3, aff3, xn, wg, wu, wd)


def _combine_body(j0_ref, post_ref, xm_ref, y0_ref, y1_ref, o_ref, *, nb, nt):
    b = pl.program_id(0)
    e = pl.program_id(1)

    @pl.when(e == 0)
    def _():
        o_ref[...] = xm_ref[...]

    post = post_ref[...]
    lane = lax.broadcasted_iota(I32, post.shape, 1)
    col = jnp.sum(jnp.where(lane == e, post, 0), axis=1, keepdims=True)
    slot = lax.broadcasted_iota(I32, (TOK_BLOCK, SLOT_TILE), 1)
    j0 = j0_ref[e * nb + b]
    for d, y_ref in enumerate((y0_ref, y1_ref)):
        jt = j0 + d
        hit = ((col - jt * SLOT_TILE) == slot) & (jt < nt)
        o_ref[...] += jnp.dot(hit.astype(BF16), y_ref[0, 0], preferred_element_type=F32)


def _combine(j0, pos_t, xm, y4):
    ne, nt = y4.shape[0], y4.shape[1]
    n = xm.shape[0]
    nb = n // TOK_BLOCK

    def ymap(d):
        return lambda b, e, j0: (e, jnp.minimum(j0[e * nb + b] + d, nt - 1), 0, 0)

    grid_spec = pltpu.PrefetchScalarGridSpec(
        num_scalar_prefetch=1,
        grid=(nb, ne),
        in_specs=[pl.BlockSpec((TOK_BLOCK, ne), lambda b, e, j0: (b, 0)),
                  pl.BlockSpec((TOK_BLOCK, D_MODEL), lambda b, e, j0: (b, 0)),
                  pl.BlockSpec((1, 1, SLOT_TILE, D_MODEL), ymap(0)),
                  pl.BlockSpec((1, 1, SLOT_TILE, D_MODEL), ymap(1))],
        out_specs=pl.BlockSpec((TOK_BLOCK, D_MODEL), lambda b, e, j0: (b, 0)))
    return pl.pallas_call(
        functools.partial(_combine_body, nb=nb, nt=nt),
        grid_spec=grid_spec,
        out_shape=jax.ShapeDtypeStruct((n, D_MODEL), F32),
        compiler_params=_params(("parallel", "arbitrary"), 32),
        name="ec_combine",
    )(j0, pos_t, xm, y4, y4)


def _ec_moe(xm, xn, aff_t, wg, wu, wd):
    ne, n = aff_t.shape
    cap = max(1, EC_CAPACITY * n // N_EXPERTS)
    nb = n // TOK_BLOCK
    pos, off = _select(aff_t, cap)
    rows_per_block = TOK_BLOCK // LANES
    j0 = (off[:, ::rows_per_block] // SLOT_TILE).reshape(ne * nb).astype(I32)
    y4 = _ffn(j0, pos.reshape(ne, 1, n), aff_t.reshape(ne, 1, n), xn, wg, wu, wd, cap)
    return _combine(j0, pos.T, xm, y4)


def _slot_cols(w, head_width):
    k = w.shape[0]
    w3 = w.reshape(k, N_HEADS, head_width)
    return jnp.pad(w3, ((0, 0), (0, 0), (0, SLOT - head_width))).reshape(k, N_HEADS * SLOT)


def _attn_weights(q_a_norm, w_uq, kv_a_norm, w_ukv, q_norm, k_norm):
    wkv = w_ukv.reshape(KV_LORA, N_HEADS, QK_NOPE + V_HEAD)
    wk = _slot_cols(wkv[:, :, :QK_NOPE].reshape(KV_LORA, -1), QK_NOPE)
    wv = wkv[:, :, QK_NOPE:].reshape(KV_LORA, D_ATTN)
    pad_gain = lambda g: jnp.pad(g, (0, SLOT - QK_HEAD))[None, :]
    return dict(wq=_slot_cols(w_uq, QK_HEAD).astype(BF16), wk=wk.astype(BF16), wv=wv.astype(BF16),
                qa=q_a_norm[None, :], kva=kv_a_norm[None, :], qg=pad_gain(q_norm), kg=pad_gain(k_norm))


def _rope_tables(seq):
    pos = jnp.arange(seq, dtype=F32)
    inv_freq = ROPE_THETA ** (-jnp.arange(0, QK_ROPE, 2, dtype=F32) / QK_ROPE)
    ang = pos[:, None] * inv_freq
    ang = jnp.concatenate([ang, ang], axis=-1)
    pad = lambda t, fill: jnp.concatenate(
        [jnp.full((seq, QK_NOPE), fill, F32), t, jnp.full((seq, SLOT - QK_HEAD), fill, F32)], axis=1)
    return pad(jnp.cos(ang), 1.0), pad(jnp.sin(ang), 0.0)


def _filter_weights(w1, b1, w2, b2, w3, freq, decay):
    hid = FILTER_HIDDEN
    padc = lambda a: jnp.pad(a, ((0, 0), (0, LANES - a.shape[1])))
    pad2 = lambda a: jnp.pad(a, ((0, LANES - a.shape[0]), (0, LANES - a.shape[1])))
    bands = jnp.linspace(1e-4, FILTER_BANDS - 1, FILTER_BANDS, dtype=F32)
    cols = HYENA_ORDER * D_HYENA
    w3d = jnp.transpose(w3.reshape(hid, N_DIR, cols), (1, 0, 2))
    return dict(bands=padc(bands[None, :]), w1t=padc(w1[0:1]), w1c=pad2(w1[1:1 + FILTER_BANDS]),
                w1s=pad2(w1[1 + FILTER_BANDS:]), b1=padc(b1[None, :]), w2=pad2(w2), b2=padc(b2[None, :]),
                freq=padc(freq[None, :]), w3=jnp.pad(w3d, ((0, 0), (0, LANES - hid), (0, 0))),
                decay=decay.reshape(N_DIR, 1, cols))


def _trunk(x, p):
    bsz, seq, _ = x.shape
    n = bsz * seq
    cos_t, sin_t = _rope_tables(seq)
    x2 = x.reshape(n, D_MODEL)
    for l in range(DEPTH):
        uh, cq, ckvr = _inproj(x2, p["attn_norm"][l][None, :], p["w_in"][l])
        y_h = _hyena(uh.reshape(bsz, seq, COL_HYENA), p["conv_w"][l], p["conv_b"][l][None, :],
                     p["filt"][l], p["hyena_bias"][l])
        qt, k, vt = _qkv(cq.reshape(bsz, seq, -1), ckvr.reshape(bsz, seq, -1), cos_t, sin_t, p["attn"][l])
        y_a = _flash(qt, k, vt)
        xm, xn, aff_t = _outproj(y_h.reshape(n, D_HYENA), y_a.reshape(n, D_ATTN), x2,
                                 p["out_norm"][l][None, :], p["w_out"][l], p["mlp_norm"][l][None, :],
                                 p["w_router"][l])
        x2 = _ec_moe(xm, xn, aff_t, p["w_gate"][l], p["w_up"][l], p["w_down"][l])
    return x2.reshape(bsz, seq, D_MODEL)


def kernel(x_prompt, x_sample, attn_norm, w_in, conv_w, conv_b, filt_w1, filt_b1, filt_w2, filt_b2, filt_w3,
           filt_freq, filt_decay, hyena_bias, q_a_norm, w_uq, kv_a_norm, w_ukv, q_norm, k_norm, out_norm,
           w_out, mlp_norm, w_router, w_gate, w_up, w_down):
    p = dict(
        attn_norm=attn_norm,
        w_in=jnp.pad(w_in, ((0, 0), (0, 0), (0, D_IN_PAD - D_IN))).astype(BF16),
        conv_w=conv_w, conv_b=conv_b, hyena_bias=hyena_bias,
        filt=[_filter_weights(filt_w1[l], filt_b1[l], filt_w2[l], filt_b2[l], filt_w3[l], filt_freq[l],
                              filt_decay[l]) for l in range(DEPTH)],
        attn=[_attn_weights(q_a_norm[l], w_uq[l], kv_a_norm[l], w_ukv[l], q_norm[l], k_norm[l])
              for l in range(DEPTH)],
        out_norm=out_norm, w_out=w_out.astype(BF16), mlp_norm=mlp_norm,
        w_router=jnp.pad(w_router, ((0, 0), (0, 0), (0, LANES - N_EXPERTS))),
        w_gate=w_gate.astype(BF16), w_up=w_up.astype(BF16), w_down=w_down.astype(BF16))
    return (_trunk(x_prompt, p), _trunk(x_sample, p))
```

```python
import functools
import math

import jax
import jax.numpy as jnp
from jax import lax
from jax.experimental import pallas as pl
from jax.experimental.pallas import tpu as pltpu

F32 = jnp.float32
BF16 = jnp.bfloat16
I32 = jnp.int32
HIGHEST = lax.Precision.HIGHEST

D_MODEL = 1024
DEPTH = 2
D_HYENA = 512
HYENA_ORDER = 2
FILTER_BANDS = 16
FILTER_HIDDEN = 64
N_DIR = 2
DECAY_SHIFT = 0.05
N_HEADS = 8
QK_NOPE = 64
QK_ROPE = 32
QK_HEAD = QK_NOPE + QK_ROPE
V_HEAD = 64
D_ATTN = N_HEADS * V_HEAD
Q_LORA = 256
KV_LORA = 128
ROPE_THETA = 10000.0
N_EXPERTS = 16
EC_CAPACITY = 2
D_EXPERT = 512
EPS = 1e-6
COL_HYENA = (HYENA_ORDER + 1) * D_HYENA
COL_Q = COL_HYENA + Q_LORA
COL_KV = COL_Q + KV_LORA
D_IN = COL_KV + QK_ROPE
D_IN_PAD = 2048

LANES = 128
SLOT = 128
TOK_BLOCK = 256
SLOT_TILE = 256
ALIGN = 16
ATT_TQ = 512
ATT_QC = 256
ATT_TK = 256
MIB = 1024 * 1024


def _params(sem, vmem_mib):
    return pltpu.CompilerParams(dimension_semantics=sem, vmem_limit_bytes=vmem_mib * MIB)


def _rms(x):
    return x * lax.rsqrt(jnp.mean(x * x, axis=-1, keepdims=True) + EPS)


def _inproj_body(x_ref, g_ref, w_ref, uh_ref, cq_ref, ckvr_ref):
    xn = _rms(x_ref[...]) * g_ref[...]
    u = jnp.dot(xn.astype(BF16), w_ref[...], preferred_element_type=F32)
    uh_ref[...] = u[:, :COL_HYENA]
    cq_ref[...] = u[:, COL_HYENA:COL_Q]
    ckvr_ref[...] = u[:, COL_Q:]


def _inproj(x2d, g, w_pad):
    n = x2d.shape[0]
    tm = min(512, n)
    return pl.pallas_call(
        _inproj_body,
        grid=(n // tm,),
        in_specs=[pl.BlockSpec((tm, D_MODEL), lambda i: (i, 0)),
                  pl.BlockSpec((1, D_MODEL), lambda i: (0, 0)),
                  pl.BlockSpec((D_MODEL, D_IN_PAD), lambda i: (0, 0))],
        out_specs=[pl.BlockSpec((tm, COL_HYENA), lambda i: (i, 0)),
                   pl.BlockSpec((tm, Q_LORA), lambda i: (i, 0)),
                   pl.BlockSpec((tm, D_IN_PAD - COL_Q), lambda i: (i, 0))],
        out_shape=[jax.ShapeDtypeStruct((n, COL_HYENA), F32),
                   jax.ShapeDtypeStruct((n, Q_LORA), F32),
                   jax.ShapeDtypeStruct((n, D_IN_PAD - COL_Q), F32)],
        compiler_params=_params(("parallel",), 40),
        name="inproj",
    )(x2d, g, w_pad)


def _sconv_body(u_ref, prev_ref, next_ref, w_ref, b_ref, v_ref, x1_ref, x2_ref):
    i = pl.program_id(1)
    last = pl.num_programs(1) - 1
    u = u_ref[0]
    tl = u.shape[0]
    prev_row = jnp.where(i == 0, 0.0, prev_ref[0, 7:8, :])
    next_row = jnp.where(i == last, 0.0, next_ref[0, 0:1, :])
    row = lax.broadcasted_iota(I32, u.shape, 0)
    up = jnp.where(row == 0, prev_row, pltpu.roll(u, 1, axis=0))
    dn = jnp.where(row == tl - 1, next_row, pltpu.roll(u, tl - 1, axis=0))
    y = up * w_ref[0:1, :] + u * w_ref[1:2, :] + dn * w_ref[2:3, :] + b_ref[...]
    v_ref[0] = y[:, :D_HYENA]
    x1_ref[0] = y[:, D_HYENA:2 * D_HYENA]
    x2_ref[0] = y[:, 2 * D_HYENA:]


def _sconv(uh, w, b):
    bsz, seq, c = uh.shape
    tl = min(512, seq)
    r = tl // 8
    nblk8 = seq // 8
    out = jax.ShapeDtypeStruct((bsz, seq, D_HYENA), F32)
    ospec = pl.BlockSpec((1, tl, D_HYENA), lambda bi, i: (bi, i, 0))
    return pl.pallas_call(
        _sconv_body,
        grid=(bsz, seq // tl),
        in_specs=[pl.BlockSpec((1, tl, c), lambda bi, i: (bi, i, 0)),
                  pl.BlockSpec((1, 8, c), lambda bi, i: (bi, jnp.maximum(i * r - 1, 0), 0)),
                  pl.BlockSpec((1, 8, c), lambda bi, i: (bi, jnp.minimum((i + 1) * r, nblk8 - 1), 0)),
                  pl.BlockSpec((3, c), lambda bi, i: (0, 0)),
                  pl.BlockSpec((1, c), lambda bi, i: (0, 0))],
        out_specs=[ospec, ospec, ospec],
        out_shape=[out, out, out],
        compiler_params=_params(("parallel", "parallel"), 40),
        name="sconv",
    )(uh, uh, uh, w, b)


def _filter_body(bands_ref, w1t_ref, w1c_ref, w1s_ref, b1_ref, w2_ref, b2_ref, fr_ref, w3_ref, dec_ref,
                 k_ref, sum_ref, *, seq):
    i = pl.program_id(0)
    tr = k_ref.shape[0]
    n = i * tr + lax.broadcasted_iota(I32, (tr, 1), 0)
    t_idx = jnp.where(n < seq, n, 2 * seq - n).astype(F32)
    t = t_idx / (seq - 1)
    ang = 2.0 * math.pi * bands_ref[...] * t_idx / seq
    fr = fr_ref[...]
    pre = (t * w1t_ref[...]
           + jnp.dot(jnp.cos(ang), w1c_ref[...], precision=HIGHEST, preferred_element_type=F32)
           - jnp.dot(jnp.sin(ang), w1s_ref[...], precision=HIGHEST, preferred_element_type=F32))
    h = jnp.sin(fr * (pre + b1_ref[...]))
    h = jnp.sin(fr * (jnp.dot(h, w2_ref[...], precision=HIGHEST, preferred_element_type=F32) + b2_ref[...]))
    h = jnp.dot(h, w3_ref[0], precision=HIGHEST, preferred_element_type=F32)
    window = jnp.exp(-t * dec_ref[0]) + DECAY_SHIFT
    k = jnp.where(n == seq, 0.0, h * window)
    k_ref[...] = k

    @pl.when(i == 0)
    def _():
        sum_ref[...] = jnp.zeros_like(sum_ref)

    sum_ref[...] += jnp.sum(jnp.abs(k), axis=0, keepdims=True)


def _filters(seq, fw):
    cols = HYENA_ORDER * D_HYENA
    tr = min(512, seq)
    half = seq // tr
    const = lambda shape: pl.BlockSpec(shape, lambda i: (0,) * len(shape))
    return pl.pallas_call(
        functools.partial(_filter_body, seq=seq),
        grid=(2 * seq // tr,),
        in_specs=[const((1, LANES)), const((1, LANES)), const((LANES, LANES)), const((LANES, LANES)),
                  const((1, LANES)), const((LANES, LANES)), const((1, LANES)), const((1, LANES)),
                  pl.BlockSpec((1, LANES, cols), lambda i: (i // half, 0, 0)),
                  pl.BlockSpec((1, 1, cols), lambda i: (i // half, 0, 0))],
        out_specs=[pl.BlockSpec((tr, cols), lambda i: (i, 0)),
                   pl.BlockSpec((1, cols), lambda i: (0, 0))],
        out_shape=[jax.ShapeDtypeStruct((2 * seq, cols), F32),
                   jax.ShapeDtypeStruct((1, cols), F32)],
        compiler_params=_params(("arbitrary",), 32),
        name="filter_gen",
    )(fw["bands"], fw["w1t"], fw["w1c"], fw["w1s"], fw["b1"], fw["w2"], fw["b2"], fw["freq"],
      fw["w3"], fw["decay"])


def _fft_dims(seq):
    n2 = 128 if 2 * seq >= 32768 else 64
    n1 = 2 * seq // n2
    return n1, n2


def _dft_tables(n1, n2):
    n = n1 * n2
    n1h = n1 // 2
    k1 = jnp.arange(n1h, dtype=I32)[:, None]
    m1 = jnp.arange(n1, dtype=I32)[None, :]
    ang = (2.0 * math.pi / n1) * ((k1 * m1) % n1).astype(F32)
    top = jnp.cos(ang)
    bot = -jnp.sin(ang)
    nyq = jnp.where(m1 % 2 == 0, 1.0, -1.0).astype(F32)
    bot = jnp.concatenate([nyq, bot[1:]], axis=0)
    fa = jnp.concatenate([top, bot], axis=0)
    weight = jnp.where((jnp.arange(n1) % n1h) == 0, 1.0, 2.0).astype(F32) / n
    fi = (fa[:, :n1h] * weight[:, None]).T

    kk = jnp.arange(n1h + 1, dtype=I32)[:, None, None]
    k2 = jnp.arange(n2, dtype=I32)[None, :, None]
    m2 = jnp.arange(n2, dtype=I32)[None, None, :]
    phi = (2.0 * math.pi / n) * ((m2 * (kk + n1 * k2)) % n).astype(F32)
    gr = jnp.cos(phi)
    gi = -jnp.sin(phi)
    blk = jnp.concatenate([jnp.concatenate([gr, -gi], axis=2),
                           jnp.concatenate([gi, gr], axis=2)], axis=1)
    left = (jnp.arange(2 * n2) < n2)[None, :]
    g0 = jnp.where(left, blk[0], 0.0)
    gf = jnp.concatenate([g0[None], blk[1:n1h]], axis=0)
    gnyq = jnp.concatenate([jnp.zeros((2 * n2, n2), F32), blk[n1h][:, :n2]], axis=1)
    return dict(fa_full=fa.astype(BF16), fa_half=fa[:, :n1h].astype(BF16), fi=fi.astype(BF16),
                gf=gf.astype(BF16), gnyq=gnyq.astype(BF16),
                ginv=jnp.transpose(gf, (0, 2, 1)).astype(BF16), m2=gnyq.T.astype(BF16))


def _fft_a_body(x_ref, f_ref, o_ref):
    a = jnp.dot(f_ref[...], x_ref[0].astype(BF16), preferred_element_type=F32)
    n1h = o_ref.shape[2]
    o_ref[0, 0] = a[:n1h].astype(BF16)
    o_ref[0, 1] = a[n1h:].astype(BF16)


def _fft_a(x3, fmat):
    bsz, r, nc = x3.shape
    n1 = fmat.shape[0]
    tn = min(2048, nc)
    return pl.pallas_call(
        _fft_a_body,
        grid=(bsz, nc // tn),
        in_specs=[pl.BlockSpec((1, r, tn), lambda b, j: (b, 0, j)),
                  pl.BlockSpec((n1, r), lambda b, j: (0, 0))],
        out_specs=pl.BlockSpec((1, 2, n1 // 2, tn), lambda b, j: (b, 0, 0, j)),
        out_shape=jax.ShapeDtypeStruct((bsz, 2, n1 // 2, nc), BF16),
        compiler_params=_params(("parallel", "parallel"), 32),
        name="fft_stage_a",
    )(x3, fmat)


def _fft_b_body(ar_ref, ai_ref, g_ref, gn_ref, sc_ref, x_ref, xn_ref, *, kb):
    j = pl.program_id(2)
    inv = 1.0 / sc_ref[...]
    for kk in range(kb):
        rhs = jnp.concatenate([ar_ref[0, 0, kk], ai_ref[0, 0, kk]], axis=0)
        x_ref[0, kk] = jnp.dot(g_ref[kk], rhs, preferred_element_type=F32) * inv

    @pl.when(j == 0)
    def _():
        rhs = jnp.concatenate([ar_ref[0, 0, 0], ai_ref[0, 0, 0]], axis=0)
        xn_ref[0] = jnp.dot(gn_ref[...], rhs, preferred_element_type=F32) * inv


def _fft_b(a5, tabs, scale):
    bsz, _, n1h, n2, c = a5.shape
    kb = 8
    tc = min(512, c)
    return pl.pallas_call(
        functools.partial(_fft_b_body, kb=kb),
        grid=(bsz, c // tc, n1h // kb),
        in_specs=[pl.BlockSpec((1, 1, kb, n2, tc), lambda b, ci, j: (b, 0, j, 0, ci)),
                  pl.BlockSpec((1, 1, kb, n2, tc), lambda b, ci, j: (b, 1, j, 0, ci)),
                  pl.BlockSpec((kb, 2 * n2, 2 * n2), lambda b, ci, j: (j, 0, 0)),
                  pl.BlockSpec((2 * n2, 2 * n2), lambda b, ci, j: (0, 0)),
                  pl.BlockSpec((1, tc), lambda b, ci, j: (0, ci))],
        out_specs=[pl.BlockSpec((1, kb, 2 * n2, tc), lambda b, ci, j: (b, j, 0, ci)),
                   pl.BlockSpec((1, 2 * n2, tc), lambda b, ci, j: (b, 0, ci))],
        out_shape=[jax.ShapeDtypeStruct((bsz, n1h, 2 * n2, c), F32),
                   jax.ShapeDtypeStruct((bsz, 2 * n2, c), F32)],
        compiler_params=_params(("parallel", "parallel", "arbitrary"), 40),
        name="fft_stage_b",
    )(a5, a5, tabs["gf"], tabs["gnyq"], scale)


def _cmul(x, k, n2):
    xr, xi = x[:n2], x[n2:]
    kr, ki = k[:n2], k[n2:]
    return jnp.concatenate([xr * kr - xi * ki, xr * ki + xi * kr], axis=0).astype(BF16)


def _ifft_b_body(x_ref, xn_ref, k_ref, kn_ref, g_ref, m2_ref, o_ref, *, kb):
    j = pl.program_id(1)
    n2 = o_ref.shape[3]
    for kk in range(kb):
        acc = jnp.dot(g_ref[kk], _cmul(x_ref[0, kk], k_ref[0, kk], n2), preferred_element_type=F32)
        o_ref[0, 0, kk] = acc[:n2].astype(BF16)
        o_ref[0, 1, kk] = acc[n2:].astype(BF16)

    @pl.when(j == 0)
    def _():
        acc = jnp.dot(m2_ref[...], _cmul(xn_ref[0], kn_ref[0], n2), preferred_element_type=F32)
        o_ref[0, 1, 0] = acc[n2:].astype(BF16)


def _ifft_b(xs, xnyq, kf, kfnyq, order, tabs):
    bsz, n1h, tn2, c = xs.shape
    n2 = tn2 // 2
    kb = 8
    return pl.pallas_call(
        functools.partial(_ifft_b_body, kb=kb),
        grid=(bsz, n1h // kb),
        in_specs=[pl.BlockSpec((1, kb, tn2, c), lambda b, j: (b, j, 0, 0)),
                  pl.BlockSpec((1, tn2, c), lambda b, j: (b, 0, 0)),
                  pl.BlockSpec((1, kb, tn2, c), lambda b, j: (0, j, 0, order)),
                  pl.BlockSpec((1, tn2, c), lambda b, j: (0, 0, order)),
                  pl.BlockSpec((kb, tn2, tn2), lambda b, j: (j, 0, 0)),
                  pl.BlockSpec((tn2, tn2), lambda b, j: (0, 0))],
        out_specs=pl.BlockSpec((1, 2, kb, n2, c), lambda b, j: (b, 0, j, 0, 0)),
        out_shape=jax.ShapeDtypeStruct((bsz, 2, n1h, n2, c), BF16),
        compiler_params=_params(("parallel", "arbitrary"), 40),
        name="ifft_stage_b",
    )(xs, xnyq, kf, kfnyq, tabs["ginv"], tabs["m2"])


def _ifft_a_body(b_ref, f_ref, z_ref, gate_ref, bias_ref, o_ref):
    y = jnp.dot(f_ref[...], b_ref[0], preferred_element_type=F32)
    o_ref[0] = gate_ref[0] * (y + z_ref[0] * bias_ref[...])


def _ifft_a(b3, fi, z3, gate3, bias_t):
    bsz, n1, nc = b3.shape
    n1h = n1 // 2
    tn = bias_t.shape[1]
    return pl.pallas_call(
        _ifft_a_body,
        grid=(bsz, nc // tn),
        in_specs=[pl.BlockSpec((1, n1, tn), lambda b, j: (b, 0, j)),
                  pl.BlockSpec((n1h, n1), lambda b, j: (0, 0)),
                  pl.BlockSpec((1, n1h, tn), lambda b, j: (b, 0, j)),
                  pl.BlockSpec((1, n1h, tn), lambda b, j: (b, 0, j)),
                  pl.BlockSpec((1, tn), lambda b, j: (0, 0))],
        out_specs=pl.BlockSpec((1, n1h, tn), lambda b, j: (b, 0, j)),
        out_shape=jax.ShapeDtypeStruct((bsz, n1h, nc), F32),
        compiler_params=_params(("parallel", "parallel"), 32),
        name="ifft_stage_a",
    )(b3, fi, z3, gate3, bias_t)


def _hyena(uh, conv_w, conv_b, fw, bias):
    bsz, seq, _ = uh.shape
    c = D_HYENA
    n1, n2 = _fft_dims(seq)
    n1h = n1 // 2
    tabs = _dft_tables(n1, n2)
    v, x1, x2 = _sconv(uh, conv_w, conv_b)

    kcirc, ksum = _filters(seq, fw)
    cols = HYENA_ORDER * c
    ka = _fft_a(kcirc.reshape(1, n1, n2 * cols), tabs["fa_full"])
    kf, kfnyq = _fft_b(ka.reshape(1, 2, n1h, n2, cols), tabs, ksum)

    tn = min(2048, n2 * c)
    ones = jnp.ones((1, c), F32)
    z = v
    for order, gate in enumerate((x1, x2)):
        za = _fft_a(z.reshape(bsz, n1h, n2 * c), tabs["fa_half"])
        zs, znyq = _fft_b(za.reshape(bsz, 2, n1h, n2, c), tabs, ones)
        zb = _ifft_b(zs, znyq, kf, kfnyq, order, tabs)
        bias_t = jnp.tile(bias[order][None, :], (1, tn // c))
        z = _ifft_a(zb.reshape(bsz, n1, n2 * c), tabs["fi"], z.reshape(bsz, n1h, n2 * c),
                    gate.reshape(bsz, n1h, n2 * c), bias_t).reshape(bsz, seq, c)
    return z


def _qkv_body(cq_ref, ckvr_ref, cos_ref, sin_ref, wq_ref, wk_ref, wv_ref, qa_ref, kva_ref, qg_ref, kg_ref,
              qt_ref, k_ref, vt_ref):
    cqn = _rms(cq_ref[0]) * qa_ref[...]
    qs = jnp.dot(cqn.astype(BF16), wq_ref[...], preferred_element_type=F32)
    ck = ckvr_ref[0]
    ckvn = (_rms(ck[:, :KV_LORA]) * kva_ref[...]).astype(BF16)
    ks = jnp.dot(ckvn, wk_ref[...], preferred_element_type=F32)
    vs = jnp.dot(ckvn, wv_ref[...], preferred_element_type=F32)
    krope = pltpu.roll(ck[:, KV_LORA:], QK_NOPE, axis=1)
    cos = cos_ref[...]
    sin = sin_ref[...]
    lane = lax.broadcasted_iota(I32, cos.shape, 1)
    half = QK_ROPE // 2
    first = (lane >= QK_NOPE) & (lane < QK_NOPE + half)
    second = (lane >= QK_NOPE + half) & (lane < QK_HEAD)

    def head(xh, gain, scale):
        ms = jnp.sum(xh * xh, axis=-1, keepdims=True) * (1.0 / QK_HEAD)
        xn = xh * lax.rsqrt(ms + EPS) * gain
        rot = jnp.where(first, -pltpu.roll(xn, SLOT - half, axis=1),
                        jnp.where(second, pltpu.roll(xn, half, axis=1), 0.0))
        return (xn * cos + rot * sin) * scale

    q_scale = QK_HEAD ** -0.5 * math.log2(math.e)
    for h in range(N_HEADS):
        sl = slice(h * SLOT, (h + 1) * SLOT)
        qt_ref[0, h] = head(qs[:, sl], qg_ref[...], q_scale).T.astype(BF16)
        k_ref[0, h] = head(ks[:, sl] + krope, kg_ref[...], 1.0).astype(BF16)
    for hp in range(N_HEADS // 2):
        vt_ref[0, hp] = vs[:, hp * 2 * V_HEAD:(hp + 1) * 2 * V_HEAD].T.astype(BF16)


def _qkv(cq, ckvr, cos_t, sin_t, aw):
    bsz, seq, _ = cq.shape
    tm = min(512, seq)
    const = lambda shape: pl.BlockSpec(shape, lambda b, i: (0,) * len(shape))
    return pl.pallas_call(
        _qkv_body,
        grid=(bsz, seq // tm),
        in_specs=[pl.BlockSpec((1, tm, Q_LORA), lambda b, i: (b, i, 0)),
                  pl.BlockSpec((1, tm, D_IN_PAD - COL_Q), lambda b, i: (b, i, 0)),
                  pl.BlockSpec((tm, SLOT), lambda b, i: (i, 0)),
                  pl.BlockSpec((tm, SLOT), lambda b, i: (i, 0)),
                  const((Q_LORA, N_HEADS * SLOT)), const((KV_LORA, N_HEADS * SLOT)),
                  const((KV_LORA, D_ATTN)),
                  const((1, Q_LORA)), const((1, KV_LORA)), const((1, SLOT)), const((1, SLOT))],
        out_specs=[pl.BlockSpec((1, N_HEADS, SLOT, tm), lambda b, i: (b, 0, 0, i)),
                   pl.BlockSpec((1, N_HEADS, tm, SLOT), lambda b, i: (b, 0, i, 0)),
                   pl.BlockSpec((1, N_HEADS // 2, 2 * V_HEAD, tm), lambda b, i: (b, 0, 0, i))],
        out_shape=[jax.ShapeDtypeStruct((bsz, N_HEADS, SLOT, seq), BF16),
                   jax.ShapeDtypeStruct((bsz, N_HEADS, seq, SLOT), BF16),
                   jax.ShapeDtypeStruct((bsz, N_HEADS // 2, 2 * V_HEAD, seq), BF16)],
        compiler_params=_params(("parallel", "parallel"), 40),
        name="qkv_prep",
    )(cq, ckvr, cos_t, sin_t, aw["wq"], aw["wk"], aw["wv"], aw["qa"], aw["kva"], aw["qg"], aw["kg"])


def _flash_body(qt_ref, k_ref, vt_ref, o_ref, s0_ref, s1_ref, acc_ref, *, tk, nk, nqc):
    chains = [(hh, qc) for hh in range(2) for qc in range(nqc)]
    acc_ref[...] = jnp.zeros(acc_ref.shape, F32)

    def qk(t, s_ref):
        ks = pl.multiple_of(t * tk, tk)
        tile_max = []
        for c, (hh, qc) in enumerate(chains):
            k = k_ref[0, hh, pl.ds(ks, tk), :]
            s = jnp.dot(k, qt_ref[0, hh, :, qc * ATT_QC:(qc + 1) * ATT_QC], preferred_element_type=F32)
            s_ref[c] = s
            tile_max.append(jnp.max(s, axis=0, keepdims=True))
        return tile_max

    def softmax_pv(t, s_ref, tile_max, m, l):
        ks = pl.multiple_of(t * tk, tk)
        m_out, l_out = [], []
        for c, (hh, qc) in enumerate(chains):
            m_new = jnp.maximum(m[c], tile_max[c])
            a = jnp.exp2(m[c] - m_new)
            p = jnp.exp2(s_ref[c] - m_new)
            l_out.append(a * l[c] + jnp.sum(p, axis=0, keepdims=True))
            m_out.append(m_new)
            vt = vt_ref[0, 0, hh * V_HEAD:(hh + 1) * V_HEAD, pl.ds(ks, tk)]
            acc_ref[c] = acc_ref[c] * a + jnp.dot(vt, p.astype(BF16), preferred_element_type=F32)
        return m_out, l_out

    def pair(u, carry):
        tile_max, m, l = carry
        t = 2 * u
        tile_max1 = qk(t + 1, s1_ref)
        m, l = softmax_pv(t, s0_ref, tile_max, m, l)
        tile_max0 = qk(jnp.minimum(t + 2, nk - 1), s0_ref)
        m, l = softmax_pv(t + 1, s1_ref, tile_max1, m, l)
        return tile_max0, m, l

    nc = len(chains)
    init = (qk(0, s0_ref), [jnp.full((1, ATT_QC), -jnp.inf, F32)] * nc, [jnp.zeros((1, ATT_QC), F32)] * nc)
    _, m, l = lax.fori_loop(0, nk // 2, pair, init)
    heads = [jnp.concatenate([acc_ref[hh * nqc + qc] * (1.0 / l[hh * nqc + qc]) for qc in range(nqc)], axis=1)
             for hh in range(2)]
    o_ref[0] = jnp.concatenate(heads, axis=0).T


def _flash(qt, k, vt):
    bsz, nh, _, seq = qt.shape
    tq = min(ATT_TQ, seq)
    tk = min(ATT_TK, seq)
    nqc = tq // ATT_QC
    nk = seq // tk
    assert nk % 2 == 0
    return pl.pallas_call(
        functools.partial(_flash_body, tk=tk, nk=nk, nqc=nqc),
        grid=(bsz, nh // 2, seq // tq),
        in_specs=[pl.BlockSpec((1, 2, SLOT, tq), lambda b, hp, i: (b, hp, 0, i)),
                  pl.BlockSpec((1, 2, seq, SLOT), lambda b, hp, i: (b, hp, 0, 0)),
                  pl.BlockSpec((1, 1, 2 * V_HEAD, seq), lambda b, hp, i: (b, hp, 0, 0))],
        out_specs=pl.BlockSpec((1, tq, 2 * V_HEAD), lambda b, hp, i: (b, i, hp)),
        out_shape=jax.ShapeDtypeStruct((bsz, seq, nh * V_HEAD), F32),
        scratch_shapes=[pltpu.VMEM((2 * nqc, tk, ATT_QC), F32), pltpu.VMEM((2 * nqc, tk, ATT_QC), F32),
                        pltpu.VMEM((2 * nqc, V_HEAD, ATT_QC), F32)],
        compiler_params=_params(("parallel", "parallel", "parallel"), 48),
        name="flash_attn",
    )(qt, k, vt)


def _outproj_body(yh_ref, ya_ref, x_ref, og_ref, w_ref, mg_ref, wr_ref, xm_ref, xn_ref, aff_ref, affn_ref):
    og = og_ref[...]
    half = D_MODEL // 2
    y = jnp.concatenate([_rms(yh_ref[...]) * og[:, :half], _rms(ya_ref[...]) * og[:, half:]], axis=1)
    xm = x_ref[...] + jnp.dot(y.astype(BF16), w_ref[...], preferred_element_type=F32)
    xm_ref[...] = xm
    xn = _rms(xm) * mg_ref[...]
    xn_ref[...] = xn.astype(BF16)
    logits = jnp.dot(xn, wr_ref[...], precision=HIGHEST, preferred_element_type=F32)
    lane = lax.broadcasted_iota(I32, logits.shape, 1)
    logits = jnp.where(lane < N_EXPERTS, logits, -jnp.inf)
    e = jnp.exp(logits - jnp.max(logits, axis=-1, keepdims=True))
    aff = e / jnp.sum(e, axis=-1, keepdims=True)
    aff_ref[...] = aff.T[:N_EXPERTS]
    affn_ref[...] = aff[:, :N_EXPERTS]


def _outproj(yh, ya, x2d, og, w_out, mg, wr_pad):
    n = x2d.shape[0]
    tm = min(512, n)
    half = D_MODEL // 2
    const = lambda shape: pl.BlockSpec(shape, lambda i: (0,) * len(shape))
    return pl.pallas_call(
        _outproj_body,
        grid=(n // tm,),
        in_specs=[pl.BlockSpec((tm, half), lambda i: (i, 0)),
                  pl.BlockSpec((tm, half), lambda i: (i, 0)),
                  pl.BlockSpec((tm, D_MODEL), lambda i: (i, 0)),
                  const((1, D_MODEL)), const((D_MODEL, D_MODEL)), const((1, D_MODEL)),
                  const((D_MODEL, LANES))],
        out_specs=[pl.BlockSpec((tm, D_MODEL), lambda i: (i, 0)),
                   pl.BlockSpec((tm, D_MODEL), lambda i: (i, 0)),
                   pl.BlockSpec((N_EXPERTS, tm), lambda i: (0, i)),
                   pl.BlockSpec((tm, N_EXPERTS), lambda i: (i, 0))],
        out_shape=[jax.ShapeDtypeStruct((n, D_MODEL), F32),
                   jax.ShapeDtypeStruct((n, D_MODEL), BF16),
                   jax.ShapeDtypeStruct((N_EXPERTS, n), F32),
                   jax.ShapeDtypeStruct((n, N_EXPERTS), F32)],
        compiler_params=_params(("parallel",), 40),
        name="outproj_router",
    )(yh, ya, x2d, og, w_out, mg, wr_pad)


def _select_body(aff_ref, upper_ref, lower_ref, pos_ref, off_ref, *, cap):
    bits = pltpu.bitcast(aff_ref[0], I32)
    upper = upper_ref[...]
    lower = lower_ref[...]

    def count(mask):
        return jnp.sum(jnp.sum(mask.astype(F32), axis=1, keepdims=True), axis=0, keepdims=True)

    def bit_step(i, thr):
        cand = thr | jnp.left_shift(jnp.int32(1), 30 - i)
        return jnp.where(count(bits >= cand) >= cap, cand, thr)

    thr = lax.fori_loop(0, 31, bit_step, jnp.zeros((1, 1), I32))
    gt = bits > thr
    eq = bits == thr
    need = cap - count(gt)

    def prefix(mask):
        within = jnp.dot(mask.astype(BF16), upper, preferred_element_type=F32)
        total = within[:, LANES - 1:LANES]
        offs = jnp.dot(lower, jnp.broadcast_to(total, within.shape).astype(BF16), preferred_element_type=F32)
        return within, offs

    w_eq, o_eq = prefix(eq)
    sel = gt | (eq & (o_eq + w_eq <= need))
    w_sel, o_sel = prefix(sel)
    pos_ref[0] = jnp.where(sel, o_sel + w_sel - 1.0, -1.0).astype(I32)
    off_ref[0] = o_sel.astype(I32)


def _select(aff_t, cap):
    ne, n = aff_t.shape
    rows = n // LANES
    upper = (jnp.arange(LANES)[:, None] <= jnp.arange(LANES)[None, :]).astype(BF16)
    lower = (jnp.arange(rows)[None, :] < jnp.arange(rows)[:, None]).astype(BF16)
    blk = pl.BlockSpec((1, rows, LANES), lambda e: (e, 0, 0))
    pos, off = pl.pallas_call(
        functools.partial(_select_body, cap=cap),
        grid=(ne,),
        in_specs=[blk, pl.BlockSpec((LANES, LANES), lambda e: (0, 0)),
                  pl.BlockSpec((rows, rows), lambda e: (0, 0))],
        out_specs=[blk, blk],
        out_shape=[jax.ShapeDtypeStruct((ne, rows, LANES), I32),
                   jax.ShapeDtypeStruct((ne, rows, LANES), I32)],
        compiler_params=_params(("parallel",), 32),
        name="ec_select",
    )(aff_t.reshape(ne, rows, LANES), upper, lower)
    return pos.reshape(ne, n), off[:, :, 0]


def _ffn_body(off_ref, pos_ref, x_ref, wg_ref, wu_ref, wd_ref, y_ref, xs_ref, *, nb, sub, cap):
    e = pl.program_id(0)
    sb = pl.program_id(1)

    @pl.when(sb == 0)
    def _():
        xs_ref[...] = jnp.zeros(xs_ref.shape, BF16)

    row = lax.broadcasted_iota(I32, (TOK_BLOCK + ALIGN, TOK_BLOCK), 0)
    for i in range(sub):
        base = pl.multiple_of(off_ref[e * nb + sb * sub + i] * ALIGN, ALIGN)
        tok = slice(i * TOK_BLOCK, (i + 1) * TOK_BLOCK)
        hit = (pos_ref[0, :, tok] - base) == row
        win = jnp.dot(hit.astype(BF16), x_ref[tok, :], preferred_element_type=F32).astype(BF16)
        xs_ref[pl.ds(base, ALIGN), :] += win[:ALIGN]
        xs_ref[pl.ds(base + ALIGN, TOK_BLOCK), :] = win[ALIGN:]

    @pl.when(sb == pl.num_programs(1) - 1)
    def _():
        for j in range(cap // SLOT_TILE):
            rows = slice(j * SLOT_TILE, (j + 1) * SLOT_TILE)
            xt = xs_ref[rows, :]
            g = jnp.dot(xt, wg_ref[0], preferred_element_type=F32)
            u = jnp.dot(xt, wu_ref[0], preferred_element_type=F32)
            h = (g * jax.nn.sigmoid(g) * u).astype(BF16)
            y_ref[0, rows, :] = jnp.dot(h, wd_ref[0], preferred_element_type=F32).astype(BF16)
        y_ref[0, cap:, :] = jnp.zeros((y_ref.shape[1] - cap, D_MODEL), BF16)


def _ffn(base, pos3, xn, wg, wu, wd, cap):
    ne = wg.shape[0]
    n = xn.shape[0]
    nb = n // TOK_BLOCK
    sub = min(8, nb)
    cap_pad = cap + TOK_BLOCK + ALIGN
    grid_spec = pltpu.PrefetchScalarGridSpec(
        num_scalar_prefetch=1,
        grid=(ne, nb // sub),
        in_specs=[pl.BlockSpec((1, 1, sub * TOK_BLOCK), lambda e, s, off: (e, 0, s)),
                  pl.BlockSpec((sub * TOK_BLOCK, D_MODEL), lambda e, s, off: (s, 0)),
                  pl.BlockSpec((1, D_MODEL, D_EXPERT), lambda e, s, off: (e, 0, 0)),
                  pl.BlockSpec((1, D_MODEL, D_EXPERT), lambda e, s, off: (e, 0, 0)),
                  pl.BlockSpec((1, D_EXPERT, D_MODEL), lambda e, s, off: (e, 0, 0))],
        out_specs=pl.BlockSpec((1, cap_pad, D_MODEL), lambda e, s, off: (e, 0, 0)),
        scratch_shapes=[pltpu.VMEM((cap_pad, D_MODEL), BF16)])
    return pl.pallas_call(
        functools.partial(_ffn_body, nb=nb, sub=sub, cap=cap),
        grid_spec=grid_spec,
        out_shape=jax.ShapeDtypeStruct((ne, cap_pad, D_MODEL), BF16),
        compiler_params=_params(("parallel", "arbitrary"), 48),
        name="ec_ffn",
    )(base, pos3, xn, wg, wu, wd)


def _combine_body(off_ref, spill_ref, post_ref, affn_ref, xm_ref, *refs, nb, ne):
    win_refs, tail_refs, o_ref = refs[:ne], refs[ne:2 * ne], refs[2 * ne]
    b = pl.program_id(0)
    post = post_ref[...]
    affn = affn_ref[...]

    def expand(y_refs, shift):
        width = y_refs[0].shape[0]
        lane = lax.broadcasted_iota(I32, (TOK_BLOCK, width), 1)
        total = jnp.zeros((TOK_BLOCK, D_MODEL), F32)
        for e in range(ne):
            rel = post[:, e:e + 1] - (off_ref[e * nb + b] * ALIGN + shift)
            hit = (rel == lane).astype(BF16)
            total = total + affn[:, e:e + 1] * jnp.dot(hit, y_refs[e][...], preferred_element_type=F32)
        return total

    o_ref[...] = xm_ref[...] + expand(win_refs, 0)

    @pl.when(spill_ref[b] != 0)
    def _():
        o_ref[...] += expand(tail_refs, TOK_BLOCK)


def _combine(base, spill, pos_t, aff_n, xm, y):
    ne = y.shape[0]
    n = xm.shape[0]
    nb = n // TOK_BLOCK

    def window(e, rows, shift):
        return pl.BlockSpec((pl.Squeezed(), pl.Element(rows), pl.Element(D_MODEL)),
                            lambda b, off, sp: (e, (off[e * nb + b] + shift // ALIGN) * ALIGN, 0))

    grid_spec = pltpu.PrefetchScalarGridSpec(
        num_scalar_prefetch=2,
        grid=(nb,),
        in_specs=([pl.BlockSpec((TOK_BLOCK, ne), lambda b, off, sp: (b, 0)),
                   pl.BlockSpec((TOK_BLOCK, ne), lambda b, off, sp: (b, 0)),
                   pl.BlockSpec((TOK_BLOCK, D_MODEL), lambda b, off, sp: (b, 0))]
                  + [window(e, TOK_BLOCK, 0) for e in range(ne)]
                  + [window(e, ALIGN, TOK_BLOCK) for e in range(ne)]),
        out_specs=pl.BlockSpec((TOK_BLOCK, D_MODEL), lambda b, off, sp: (b, 0)))
    return pl.pallas_call(
        functools.partial(_combine_body, nb=nb, ne=ne),
        grid_spec=grid_spec,
        out_shape=jax.ShapeDtypeStruct((n, D_MODEL), F32),
        compiler_params=_params(("parallel",), 48),
        name="ec_combine",
    )(base, spill, pos_t, aff_n, xm, *([y] * (2 * ne)))


def _ec_moe(xm, xn, aff_t, aff_n, wg, wu, wd):
    ne, n = aff_t.shape
    cap = max(1, EC_CAPACITY * n // N_EXPERTS)
    nb = n // TOK_BLOCK
    pos, off = _select(aff_t, cap)
    first = off[:, ::TOK_BLOCK // LANES]
    count = jnp.concatenate([first[:, 1:], jnp.full((ne, 1), cap, I32)], axis=1) - first
    base = first // ALIGN
    spill = jnp.any(first - base * ALIGN + count > TOK_BLOCK, axis=0).astype(I32)
    base = base.reshape(ne * nb).astype(I32)
    y = _ffn(base, pos.reshape(ne, 1, n), xn, wg, wu, wd, cap)
    return _combine(base, spill, pos.T, aff_n, xm, y)


def _slot_cols(w, head_width):
    k = w.shape[0]
    w3 = w.reshape(k, N_HEADS, head_width)
    return jnp.pad(w3, ((0, 0), (0, 0), (0, SLOT - head_width))).reshape(k, N_HEADS * SLOT)


def _attn_weights(q_a_norm, w_uq, kv_a_norm, w_ukv, q_norm, k_norm):
    wkv = w_ukv.reshape(KV_LORA, N_HEADS, QK_NOPE + V_HEAD)
    wk = _slot_cols(wkv[:, :, :QK_NOPE].reshape(KV_LORA, -1), QK_NOPE)
    wv = wkv[:, :, QK_NOPE:].reshape(KV_LORA, D_ATTN)
    pad_gain = lambda g: jnp.pad(g, (0, SLOT - QK_HEAD))[None, :]
    return dict(wq=_slot_cols(w_uq, QK_HEAD).astype(BF16), wk=wk.astype(BF16), wv=wv.astype(BF16),
                qa=q_a_norm[None, :], kva=kv_a_norm[None, :], qg=pad_gain(q_norm), kg=pad_gain(k_norm))


def _rope_tables(seq):
    pos = jnp.arange(seq, dtype=F32)
    inv_freq = ROPE_THETA ** (-jnp.arange(0, QK_ROPE, 2, dtype=F32) / QK_ROPE)
    ang = pos[:, None] * inv_freq
    ang = jnp.concatenate([ang, ang], axis=-1)
    pad = lambda t, fill: jnp.concatenate(
        [jnp.full((seq, QK_NOPE), fill, F32), t, jnp.full((seq, SLOT - QK_HEAD), fill, F32)], axis=1)
    return pad(jnp.cos(ang), 1.0), pad(jnp.sin(ang), 0.0)


def _filter_weights(w1, b1, w2, b2, w3, freq, decay):
    hid = FILTER_HIDDEN
    padc = lambda a: jnp.pad(a, ((0, 0), (0, LANES - a.shape[1])))
    pad2 = lambda a: jnp.pad(a, ((0, LANES - a.shape[0]), (0, LANES - a.shape[1])))
    bands = jnp.linspace(1e-4, FILTER_BANDS - 1, FILTER_BANDS, dtype=F32)
    cols = HYENA_ORDER * D_HYENA
    w3d = jnp.transpose(w3.reshape(hid, N_DIR, cols), (1, 0, 2))
    return dict(bands=padc(bands[None, :]), w1t=padc(w1[0:1]), w1c=pad2(w1[1:1 + FILTER_BANDS]),
                w1s=pad2(w1[1 + FILTER_BANDS:]), b1=padc(b1[None, :]), w2=pad2(w2), b2=padc(b2[None, :]),
                freq=padc(freq[None, :]), w3=jnp.pad(w3d, ((0, 0), (0, LANES - hid), (0, 0))),
                decay=decay.reshape(N_DIR, 1, cols))


def _trunk(x, p):
    bsz, seq, _ = x.shape
    n = bsz * seq
    cos_t, sin_t = _rope_tables(seq)
    x2 = x.reshape(n, D_MODEL)
    for l in range(DEPTH):
        uh, cq, ckvr = _inproj(x2, p["attn_norm"][l][None, :], p["w_in"][l])
        y_h = _hyena(uh.reshape(bsz, seq, COL_HYENA), p["conv_w"][l], p["conv_b"][l][None, :],
                     p["filt"][l], p["hyena_bias"][l])
        qt, k, vt = _qkv(cq.reshape(bsz, seq, -1), ckvr.reshape(bsz, seq, -1), cos_t, sin_t, p["attn"][l])
        y_a = _flash(qt, k, vt)
        xm, xn, aff_t, aff_n = _outproj(y_h.reshape(n, D_HYENA), y_a.reshape(n, D_ATTN), x2,
                                 p["out_norm"][l][None, :], p["w_out"][l], p["mlp_norm"][l][None, :],
                                 p["w_router"][l])
        x2 = _ec_moe(xm, xn, aff_t, aff_n, p["w_gate"][l], p["w_up"][l], p["w_down"][l])
    return x2.reshape(bsz, seq, D_MODEL)


def kernel(x_prompt, x_sample, attn_norm, w_in, conv_w, conv_b, filt_w1, filt_b1, filt_w2, filt_b2, filt_w3,
           filt_freq, filt_decay, hyena_bias, q_a_norm, w_uq, kv_a_norm, w_ukv, q_norm, k_norm, out_norm,
           w_out, mlp_norm, w_router, w_gate, w_up, w_down):
    p = dict(
        attn_norm=attn_norm,
        w_in=jnp.pad(w_in, ((0, 0), (0, 0), (0, D_IN_PAD - D_IN))).astype(BF16),
        conv_w=conv_w, conv_b=conv_b, hyena_bias=hyena_bias,
        filt=[_filter_weights(filt_w1[l], filt_b1[l], filt_w2[l], filt_b2[l], filt_w3[l], filt_freq[l],
                              filt_decay[l]) for l in range(DEPTH)],
        attn=[_attn_weights(q_a_norm[l], w_uq[l], kv_a_norm[l], w_ukv[l], q_norm[l], k_norm[l])
              for l in range(DEPTH)],
        out_norm=out_norm, w_out=w_out.astype(BF16), mlp_norm=mlp_norm,
        w_router=jnp.pad(w_router, ((0, 0), (0, 0), (0, LANES - N_EXPERTS))),
        w_gate=w_gate.astype(BF16), w_up=w_up.astype(BF16), w_down=w_down.astype(BF16))
    return (_trunk(x_prompt, p), _trunk(x_sample, p))
```

```python
import functools
import math

import jax
import jax.numpy as jnp
from jax import lax
from jax.experimental import pallas as pl
from jax.experimental.pallas import tpu as pltpu

F32 = jnp.float32
BF16 = jnp.bfloat16
I32 = jnp.int32
HIGHEST = lax.Precision.HIGHEST

D_MODEL = 1024
DEPTH = 2
D_HYENA = 512
HYENA_ORDER = 2
FILTER_BANDS = 16
FILTER_HIDDEN = 64
N_DIR = 2
DECAY_SHIFT = 0.05
N_HEADS = 8
QK_NOPE = 64
QK_ROPE = 32
QK_HEAD = QK_NOPE + QK_ROPE
V_HEAD = 64
V_AUG = V_HEAD + 16
D_ATTN = N_HEADS * V_HEAD
Q_LORA = 256
KV_LORA = 128
ROPE_THETA = 10000.0
N_EXPERTS = 16
EC_CAPACITY = 2
D_EXPERT = 512
EPS = 1e-6
COL_HYENA = (HYENA_ORDER + 1) * D_HYENA
COL_Q = COL_HYENA + Q_LORA
COL_KV = COL_Q + KV_LORA
D_IN = COL_KV + QK_ROPE
D_IN_PAD = 2048

LANES = 128
SLOT = 128
TOK_BLOCK = 256
SLOT_TILE = 256
ALIGN = 16
ATT_TQ = 512
ATT_QC = 256
ATT_TK = 256
MIB = 1024 * 1024


def _params(sem, vmem_mib):
    return pltpu.CompilerParams(dimension_semantics=sem, vmem_limit_bytes=vmem_mib * MIB)


def _rms(x):
    return x * lax.rsqrt(jnp.mean(x * x, axis=-1, keepdims=True) + EPS)


def _inproj_body(x_ref, g_ref, w_ref, uh_ref, cq_ref, ckvr_ref):
    xn = _rms(x_ref[...]) * g_ref[...]
    u = jnp.dot(xn.astype(BF16), w_ref[...], preferred_element_type=F32)
    uh_ref[...] = u[:, :COL_HYENA]
    cq_ref[...] = u[:, COL_HYENA:COL_Q]
    ckvr_ref[...] = u[:, COL_Q:]


def _inproj(x2d, g, w_pad):
    n = x2d.shape[0]
    tm = min(512, n)
    return pl.pallas_call(
        _inproj_body,
        grid=(n // tm,),
        in_specs=[pl.BlockSpec((tm, D_MODEL), lambda i: (i, 0)),
                  pl.BlockSpec((1, D_MODEL), lambda i: (0, 0)),
                  pl.BlockSpec((D_MODEL, D_IN_PAD), lambda i: (0, 0))],
        out_specs=[pl.BlockSpec((tm, COL_HYENA), lambda i: (i, 0)),
                   pl.BlockSpec((tm, Q_LORA), lambda i: (i, 0)),
                   pl.BlockSpec((tm, D_IN_PAD - COL_Q), lambda i: (i, 0))],
        out_shape=[jax.ShapeDtypeStruct((n, COL_HYENA), F32),
                   jax.ShapeDtypeStruct((n, Q_LORA), F32),
                   jax.ShapeDtypeStruct((n, D_IN_PAD - COL_Q), F32)],
        compiler_params=_params(("parallel",), 40),
        name="inproj",
    )(x2d, g, w_pad)


def _sconv_body(u_ref, prev_ref, next_ref, w_ref, b_ref, v_ref, x1_ref, x2_ref):
    i = pl.program_id(1)
    last = pl.num_programs(1) - 1
    u = u_ref[0]
    tl = u.shape[0]
    prev_row = jnp.where(i == 0, 0.0, prev_ref[0, 7:8, :])
    next_row = jnp.where(i == last, 0.0, next_ref[0, 0:1, :])
    row = lax.broadcasted_iota(I32, u.shape, 0)
    up = jnp.where(row == 0, prev_row, pltpu.roll(u, 1, axis=0))
    dn = jnp.where(row == tl - 1, next_row, pltpu.roll(u, tl - 1, axis=0))
    y = up * w_ref[0:1, :] + u * w_ref[1:2, :] + dn * w_ref[2:3, :] + b_ref[...]
    v_ref[0] = y[:, :D_HYENA]
    x1_ref[0] = y[:, D_HYENA:2 * D_HYENA]
    x2_ref[0] = y[:, 2 * D_HYENA:]


def _sconv(uh, w, b):
    bsz, seq, c = uh.shape
    tl = min(512, seq)
    r = tl // 8
    nblk8 = seq // 8
    out = jax.ShapeDtypeStruct((bsz, seq, D_HYENA), F32)
    ospec = pl.BlockSpec((1, tl, D_HYENA), lambda bi, i: (bi, i, 0))
    return pl.pallas_call(
        _sconv_body,
        grid=(bsz, seq // tl),
        in_specs=[pl.BlockSpec((1, tl, c), lambda bi, i: (bi, i, 0)),
                  pl.BlockSpec((1, 8, c), lambda bi, i: (bi, jnp.maximum(i * r - 1, 0), 0)),
                  pl.BlockSpec((1, 8, c), lambda bi, i: (bi, jnp.minimum((i + 1) * r, nblk8 - 1), 0)),
                  pl.BlockSpec((3, c), lambda bi, i: (0, 0)),
                  pl.BlockSpec((1, c), lambda bi, i: (0, 0))],
        out_specs=[ospec, ospec, ospec],
        out_shape=[out, out, out],
        compiler_params=_params(("parallel", "parallel"), 40),
        name="sconv",
    )(uh, uh, uh, w, b)


def _filter_body(bands_ref, w1t_ref, w1c_ref, w1s_ref, b1_ref, w2_ref, b2_ref, fr_ref, w3_ref, dec_ref,
                 k_ref, sum_ref, *, seq):
    i = pl.program_id(0)
    tr = k_ref.shape[0]
    n = i * tr + lax.broadcasted_iota(I32, (tr, 1), 0)
    t_idx = jnp.where(n < seq, n, 2 * seq - n).astype(F32)
    t = t_idx / (seq - 1)
    ang = 2.0 * math.pi * bands_ref[...] * t_idx / seq
    fr = fr_ref[...]
    pre = (t * w1t_ref[...]
           + jnp.dot(jnp.cos(ang), w1c_ref[...], precision=HIGHEST, preferred_element_type=F32)
           - jnp.dot(jnp.sin(ang), w1s_ref[...], precision=HIGHEST, preferred_element_type=F32))
    h = jnp.sin(fr * (pre + b1_ref[...]))
    h = jnp.sin(fr * (jnp.dot(h, w2_ref[...], precision=HIGHEST, preferred_element_type=F32) + b2_ref[...]))
    h = jnp.dot(h, w3_ref[0], precision=HIGHEST, preferred_element_type=F32)
    window = jnp.exp(-t * dec_ref[0]) + DECAY_SHIFT
    k = jnp.where(n == seq, 0.0, h * window)
    k_ref[...] = k

    @pl.when(i == 0)
    def _():
        sum_ref[...] = jnp.zeros_like(sum_ref)

    sum_ref[...] += jnp.sum(jnp.abs(k), axis=0, keepdims=True)


def _filters(seq, fw):
    cols = HYENA_ORDER * D_HYENA
    tr = min(512, seq)
    half = seq // tr
    const = lambda shape: pl.BlockSpec(shape, lambda i: (0,) * len(shape))
    return pl.pallas_call(
        functools.partial(_filter_body, seq=seq),
        grid=(2 * seq // tr,),
        in_specs=[const((1, LANES)), const((1, LANES)), const((LANES, LANES)), const((LANES, LANES)),
                  const((1, LANES)), const((LANES, LANES)), const((1, LANES)), const((1, LANES)),
                  pl.BlockSpec((1, LANES, cols), lambda i: (i // half, 0, 0)),
                  pl.BlockSpec((1, 1, cols), lambda i: (i // half, 0, 0))],
        out_specs=[pl.BlockSpec((tr, cols), lambda i: (i, 0)),
                   pl.BlockSpec((1, cols), lambda i: (0, 0))],
        out_shape=[jax.ShapeDtypeStruct((2 * seq, cols), F32),
                   jax.ShapeDtypeStruct((1, cols), F32)],
        compiler_params=_params(("arbitrary",), 32),
        name="filter_gen",
    )(fw["bands"], fw["w1t"], fw["w1c"], fw["w1s"], fw["b1"], fw["w2"], fw["b2"], fw["freq"],
      fw["w3"], fw["decay"])


def _fft_dims(seq):
    n2 = 128 if 2 * seq >= 32768 else 64
    n1 = 2 * seq // n2
    return n1, n2


def _dft_tables(n1, n2):
    n = n1 * n2
    n1h = n1 // 2
    k1 = jnp.arange(n1h, dtype=I32)[:, None]
    m1 = jnp.arange(n1, dtype=I32)[None, :]
    ang = (2.0 * math.pi / n1) * ((k1 * m1) % n1).astype(F32)
    top = jnp.cos(ang)
    bot = -jnp.sin(ang)
    nyq = jnp.where(m1 % 2 == 0, 1.0, -1.0).astype(F32)
    bot = jnp.concatenate([nyq, bot[1:]], axis=0)
    fa = jnp.concatenate([top, bot], axis=0)
    weight = jnp.where((jnp.arange(n1) % n1h) == 0, 1.0, 2.0).astype(F32) / n
    fi = (fa[:, :n1h] * weight[:, None]).T

    kk = jnp.arange(n1h + 1, dtype=I32)[:, None, None]
    k2 = jnp.arange(n2, dtype=I32)[None, :, None]
    m2 = jnp.arange(n2, dtype=I32)[None, None, :]
    phi = (2.0 * math.pi / n) * ((m2 * (kk + n1 * k2)) % n).astype(F32)
    gr = jnp.cos(phi)
    gi = -jnp.sin(phi)
    blk = jnp.concatenate([jnp.concatenate([gr, -gi], axis=2),
                           jnp.concatenate([gi, gr], axis=2)], axis=1)
    left = (jnp.arange(2 * n2) < n2)[None, :]
    g0 = jnp.where(left, blk[0], 0.0)
    gf = jnp.concatenate([g0[None], blk[1:n1h]], axis=0)
    gnyq = jnp.concatenate([jnp.zeros((2 * n2, n2), F32), blk[n1h][:, :n2]], axis=1)
    return dict(fa_full=fa.astype(BF16), fa_half=fa[:, :n1h].astype(BF16), fi=fi.astype(BF16),
                gf=gf.astype(BF16), gnyq=gnyq.astype(BF16),
                ginv=jnp.transpose(gf, (0, 2, 1)).astype(BF16), m2=gnyq.T.astype(BF16))


def _fft_a_body(x_ref, f_ref, o_ref):
    a = jnp.dot(f_ref[...], x_ref[0].astype(BF16), preferred_element_type=F32)
    n1h = o_ref.shape[2]
    o_ref[0, 0] = a[:n1h].astype(BF16)
    o_ref[0, 1] = a[n1h:].astype(BF16)


def _fft_a(x3, fmat):
    bsz, r, nc = x3.shape
    n1 = fmat.shape[0]
    tn = min(2048, nc)
    return pl.pallas_call(
        _fft_a_body,
        grid=(bsz, nc // tn),
        in_specs=[pl.BlockSpec((1, r, tn), lambda b, j: (b, 0, j)),
                  pl.BlockSpec((n1, r), lambda b, j: (0, 0))],
        out_specs=pl.BlockSpec((1, 2, n1 // 2, tn), lambda b, j: (b, 0, 0, j)),
        out_shape=jax.ShapeDtypeStruct((bsz, 2, n1 // 2, nc), BF16),
        compiler_params=_params(("parallel", "parallel"), 32),
        name="fft_stage_a",
    )(x3, fmat)


def _fft_b_body(ar_ref, ai_ref, g_ref, gn_ref, sc_ref, x_ref, xn_ref, *, kb):
    j = pl.program_id(2)
    inv = 1.0 / sc_ref[...]
    for kk in range(kb):
        rhs = jnp.concatenate([ar_ref[0, 0, kk], ai_ref[0, 0, kk]], axis=0)
        x_ref[0, kk] = (jnp.dot(g_ref[kk], rhs, preferred_element_type=F32) * inv).astype(BF16)

    @pl.when(j == 0)
    def _():
        rhs = jnp.concatenate([ar_ref[0, 0, 0], ai_ref[0, 0, 0]], axis=0)
        xn_ref[0] = (jnp.dot(gn_ref[...], rhs, preferred_element_type=F32) * inv).astype(BF16)


def _fft_b(a5, tabs, scale):
    bsz, _, n1h, n2, c = a5.shape
    kb = 8
    tc = min(512, c)
    return pl.pallas_call(
        functools.partial(_fft_b_body, kb=kb),
        grid=(bsz, c // tc, n1h // kb),
        in_specs=[pl.BlockSpec((1, 1, kb, n2, tc), lambda b, ci, j: (b, 0, j, 0, ci)),
                  pl.BlockSpec((1, 1, kb, n2, tc), lambda b, ci, j: (b, 1, j, 0, ci)),
                  pl.BlockSpec((kb, 2 * n2, 2 * n2), lambda b, ci, j: (j, 0, 0)),
                  pl.BlockSpec((2 * n2, 2 * n2), lambda b, ci, j: (0, 0)),
                  pl.BlockSpec((1, tc), lambda b, ci, j: (0, ci))],
        out_specs=[pl.BlockSpec((1, kb, 2 * n2, tc), lambda b, ci, j: (b, j, 0, ci)),
                   pl.BlockSpec((1, 2 * n2, tc), lambda b, ci, j: (b, 0, ci))],
        out_shape=[jax.ShapeDtypeStruct((bsz, n1h, 2 * n2, c), BF16),
                   jax.ShapeDtypeStruct((bsz, 2 * n2, c), BF16)],
        compiler_params=_params(("parallel", "parallel", "arbitrary"), 40),
        name="fft_stage_b",
    )(a5, a5, tabs["gf"], tabs["gnyq"], scale)


def _cmul(x, k, n2):
    xr, xi = x[:n2], x[n2:]
    kr, ki = k[:n2], k[n2:]
    return jnp.concatenate([xr * kr - xi * ki, xr * ki + xi * kr], axis=0).astype(BF16)


def _spec_body(ar_ref, ai_ref, k_ref, kn_ref, g_ref, gn_ref, gi_ref, m2_ref, o_ref, *, kb):
    j = pl.program_id(1)
    n2 = o_ref.shape[3]

    def through(fwd, a_re, a_im, kf, inv):
        x = jnp.dot(fwd, jnp.concatenate([a_re, a_im], axis=0), preferred_element_type=F32)
        return jnp.dot(inv, _cmul(x, kf.astype(F32), n2), preferred_element_type=F32)

    for kk in range(kb):
        acc = through(g_ref[kk], ar_ref[0, 0, kk], ai_ref[0, 0, kk], k_ref[0, kk], gi_ref[kk])
        o_ref[0, 0, kk] = acc[:n2].astype(BF16)
        o_ref[0, 1, kk] = acc[n2:].astype(BF16)

    @pl.when(j == 0)
    def _():
        acc = through(gn_ref[...], ar_ref[0, 0, 0], ai_ref[0, 0, 0], kn_ref[0], m2_ref[...])
        o_ref[0, 1, 0] = acc[n2:].astype(BF16)


def _spec_conv(a5, kf, kfnyq, order, tabs):
    bsz, _, n1h, n2, c = a5.shape
    tn2 = 2 * n2
    kb = 8
    return pl.pallas_call(
        functools.partial(_spec_body, kb=kb),
        grid=(bsz, n1h // kb),
        in_specs=[pl.BlockSpec((1, 1, kb, n2, c), lambda b, j: (b, 0, j, 0, 0)),
                  pl.BlockSpec((1, 1, kb, n2, c), lambda b, j: (b, 1, j, 0, 0)),
                  pl.BlockSpec((1, kb, tn2, c), lambda b, j: (0, j, 0, order)),
                  pl.BlockSpec((1, tn2, c), lambda b, j: (0, 0, order)),
                  pl.BlockSpec((kb, tn2, tn2), lambda b, j: (j, 0, 0)),
                  pl.BlockSpec((tn2, tn2), lambda b, j: (0, 0)),
                  pl.BlockSpec((kb, tn2, tn2), lambda b, j: (j, 0, 0)),
                  pl.BlockSpec((tn2, tn2), lambda b, j: (0, 0))],
        out_specs=pl.BlockSpec((1, 2, kb, n2, c), lambda b, j: (b, 0, j, 0, 0)),
        out_shape=jax.ShapeDtypeStruct((bsz, 2, n1h, n2, c), BF16),
        compiler_params=_params(("parallel", "arbitrary"), 48),
        name="spectral_conv",
    )(a5, a5, kf, kfnyq, tabs["gf"], tabs["gnyq"], tabs["ginv"], tabs["m2"])


def _ifft_a_body(b_ref, f_ref, z_ref, gate_ref, bias_ref, o_ref):
    y = jnp.dot(f_ref[...], b_ref[0], preferred_element_type=F32)
    o_ref[0] = gate_ref[0] * (y + z_ref[0] * bias_ref[...])


def _ifft_a(b3, fi, z3, gate3, bias_t):
    bsz, n1, nc = b3.shape
    n1h = n1 // 2
    tn = bias_t.shape[1]
    return pl.pallas_call(
        _ifft_a_body,
        grid=(bsz, nc // tn),
        in_specs=[pl.BlockSpec((1, n1, tn), lambda b, j: (b, 0, j)),
                  pl.BlockSpec((n1h, n1), lambda b, j: (0, 0)),
                  pl.BlockSpec((1, n1h, tn), lambda b, j: (b, 0, j)),
                  pl.BlockSpec((1, n1h, tn), lambda b, j: (b, 0, j)),
                  pl.BlockSpec((1, tn), lambda b, j: (0, 0))],
        out_specs=pl.BlockSpec((1, n1h, tn), lambda b, j: (b, 0, j)),
        out_shape=jax.ShapeDtypeStruct((bsz, n1h, nc), F32),
        compiler_params=_params(("parallel", "parallel"), 32),
        name="ifft_stage_a",
    )(b3, fi, z3, gate3, bias_t)


def _hyena(uh, conv_w, conv_b, fw, bias):
    bsz, seq, _ = uh.shape
    c = D_HYENA
    n1, n2 = _fft_dims(seq)
    n1h = n1 // 2
    tabs = _dft_tables(n1, n2)
    v, x1, x2 = _sconv(uh, conv_w, conv_b)

    kcirc, ksum = _filters(seq, fw)
    cols = HYENA_ORDER * c
    ka = _fft_a(kcirc.reshape(1, n1, n2 * cols), tabs["fa_full"])
    kf, kfnyq = _fft_b(ka.reshape(1, 2, n1h, n2, cols), tabs, ksum)

    tn = min(2048, n2 * c)
    z = v
    for order, gate in enumerate((x1, x2)):
        za = _fft_a(z.reshape(bsz, n1h, n2 * c), tabs["fa_half"])
        zb = _spec_conv(za.reshape(bsz, 2, n1h, n2, c), kf, kfnyq, order, tabs)
        bias_t = jnp.tile(bias[order][None, :], (1, tn // c))
        z = _ifft_a(zb.reshape(bsz, n1, n2 * c), tabs["fi"], z.reshape(bsz, n1h, n2 * c),
                    gate.reshape(bsz, n1h, n2 * c), bias_t).reshape(bsz, seq, c)
    return z


def _qkv_body(cq_ref, ckvr_ref, cos_ref, sin_ref, wq_ref, wk_ref, wv_ref, qa_ref, kva_ref, qg_ref, kg_ref,
              qt_ref, k_ref, vt_ref):
    cqn = _rms(cq_ref[0]) * qa_ref[...]
    qs = jnp.dot(cqn.astype(BF16), wq_ref[...], preferred_element_type=F32)
    ck = ckvr_ref[0]
    ckvn = (_rms(ck[:, :KV_LORA]) * kva_ref[...]).astype(BF16)
    ks = jnp.dot(ckvn, wk_ref[...], preferred_element_type=F32)
    vs = jnp.dot(ckvn, wv_ref[...], preferred_element_type=F32)
    krope = pltpu.roll(ck[:, KV_LORA:], QK_NOPE, axis=1)
    cos = cos_ref[...]
    sin = sin_ref[...]
    lane = lax.broadcasted_iota(I32, cos.shape, 1)
    half = QK_ROPE // 2
    first = (lane >= QK_NOPE) & (lane < QK_NOPE + half)
    second = (lane >= QK_NOPE + half) & (lane < QK_HEAD)

    def head(xh, gain, scale):
        ms = jnp.sum(xh * xh, axis=-1, keepdims=True) * (1.0 / QK_HEAD)
        xn = xh * lax.rsqrt(ms + EPS) * gain
        rot = jnp.where(first, -pltpu.roll(xn, SLOT - half, axis=1),
                        jnp.where(second, pltpu.roll(xn, half, axis=1), 0.0))
        return (xn * cos + rot * sin) * scale

    q_scale = QK_HEAD ** -0.5 * math.log2(math.e)
    for h in range(N_HEADS):
        sl = slice(h * SLOT, (h + 1) * SLOT)
        qt_ref[0, h] = head(qs[:, sl], qg_ref[...], q_scale).T.astype(BF16)
        k_ref[0, h] = head(ks[:, sl] + krope, kg_ref[...], 1.0).astype(BF16)
    aug = lax.broadcasted_iota(I32, (V_AUG - V_HEAD, cos.shape[0]), 0)
    ones_row = jnp.where(aug == 0, 1.0, 0.0)
    for hp in range(N_HEADS // 2):
        pair_t = vs[:, hp * 2 * V_HEAD:(hp + 1) * 2 * V_HEAD].T
        vt_ref[0, hp] = jnp.concatenate(
            [pair_t[:V_HEAD], ones_row, pair_t[V_HEAD:], ones_row], axis=0).astype(BF16)


def _qkv(cq, ckvr, cos_t, sin_t, aw):
    bsz, seq, _ = cq.shape
    tm = min(512, seq)
    const = lambda shape: pl.BlockSpec(shape, lambda b, i: (0,) * len(shape))
    return pl.pallas_call(
        _qkv_body,
        grid=(bsz, seq // tm),
        in_specs=[pl.BlockSpec((1, tm, Q_LORA), lambda b, i: (b, i, 0)),
                  pl.BlockSpec((1, tm, D_IN_PAD - COL_Q), lambda b, i: (b, i, 0)),
                  pl.BlockSpec((tm, SLOT), lambda b, i: (i, 0)),
                  pl.BlockSpec((tm, SLOT), lambda b, i: (i, 0)),
                  const((Q_LORA, N_HEADS * SLOT)), const((KV_LORA, N_HEADS * SLOT)),
                  const((KV_LORA, D_ATTN)),
                  const((1, Q_LORA)), const((1, KV_LORA)), const((1, SLOT)), const((1, SLOT))],
        out_specs=[pl.BlockSpec((1, N_HEADS, SLOT, tm), lambda b, i: (b, 0, 0, i)),
                   pl.BlockSpec((1, N_HEADS, tm, SLOT), lambda b, i: (b, 0, i, 0)),
                   pl.BlockSpec((1, N_HEADS // 2, 2 * V_AUG, tm), lambda b, i: (b, 0, 0, i))],
        out_shape=[jax.ShapeDtypeStruct((bsz, N_HEADS, SLOT, seq), BF16),
                   jax.ShapeDtypeStruct((bsz, N_HEADS, seq, SLOT), BF16),
                   jax.ShapeDtypeStruct((bsz, N_HEADS // 2, 2 * V_AUG, seq), BF16)],
        compiler_params=_params(("parallel", "parallel"), 40),
        name="qkv_prep",
    )(cq, ckvr, cos_t, sin_t, aw["wq"], aw["wk"], aw["wv"], aw["qa"], aw["kva"], aw["qg"], aw["kg"])


def _flash_body(qt_ref, k_ref, vt_ref, o_ref, s0_ref, s1_ref, acc_ref, *, tk, nk, nqc):
    chains = [(hh, qc) for hh in range(2) for qc in range(nqc)]
    acc_ref[...] = jnp.zeros(acc_ref.shape, F32)

    def qk(t, s_ref):
        ks = pl.multiple_of(t * tk, tk)
        tile_max = []
        for c, (hh, qc) in enumerate(chains):
            k = k_ref[0, hh, pl.ds(ks, tk), :]
            s = jnp.dot(k, qt_ref[0, hh, :, qc * ATT_QC:(qc + 1) * ATT_QC], preferred_element_type=F32)
            s_ref[c] = s
            tile_max.append(jnp.max(s, axis=0, keepdims=True))
        return tile_max

    def softmax_pv(t, s_ref, tile_max, m):
        ks = pl.multiple_of(t * tk, tk)
        m_out = []
        for c, (hh, qc) in enumerate(chains):
            m_new = jnp.maximum(m[c], tile_max[c])
            a = jnp.exp2(m[c] - m_new)
            p = jnp.exp2(s_ref[c] - m_new)
            m_out.append(m_new)
            vt = vt_ref[0, 0, hh * V_AUG:(hh + 1) * V_AUG, pl.ds(ks, tk)]
            acc_ref[c] = acc_ref[c] * a + jnp.dot(vt, p.astype(BF16), preferred_element_type=F32)
        return m_out

    def pair(u, carry):
        tile_max, m = carry
        t = 2 * u
        tile_max1 = qk(t + 1, s1_ref)
        m = softmax_pv(t, s0_ref, tile_max, m)
        tile_max0 = qk(jnp.minimum(t + 2, nk - 1), s0_ref)
        m = softmax_pv(t + 1, s1_ref, tile_max1, m)
        return tile_max0, m

    init = (qk(0, s0_ref), [jnp.full((1, ATT_QC), -jnp.inf, F32)] * len(chains))
    lax.fori_loop(0, nk // 2, pair, init)

    def normalised(c):
        acc = acc_ref[c]
        return acc[:V_HEAD] * (1.0 / acc[V_HEAD:V_HEAD + 1])

    heads = [jnp.concatenate([normalised(hh * nqc + qc) for qc in range(nqc)], axis=1) for hh in range(2)]
    o_ref[0] = jnp.concatenate(heads, axis=0).T


def _flash(qt, k, vt):
    bsz, nh, _, seq = qt.shape
    tq = min(ATT_TQ, seq)
    tk = min(ATT_TK, seq)
    nqc = tq // ATT_QC
    nk = seq // tk
    assert nk % 2 == 0
    return pl.pallas_call(
        functools.partial(_flash_body, tk=tk, nk=nk, nqc=nqc),
        grid=(bsz, nh // 2, seq // tq),
        in_specs=[pl.BlockSpec((1, 2, SLOT, tq), lambda b, hp, i: (b, hp, 0, i)),
                  pl.BlockSpec((1, 2, seq, SLOT), lambda b, hp, i: (b, hp, 0, 0)),
                  pl.BlockSpec((1, 1, 2 * V_AUG, seq), lambda b, hp, i: (b, hp, 0, 0))],
        out_specs=pl.BlockSpec((1, tq, 2 * V_HEAD), lambda b, hp, i: (b, i, hp)),
        out_shape=jax.ShapeDtypeStruct((bsz, seq, nh * V_HEAD), F32),
        scratch_shapes=[pltpu.VMEM((2 * nqc, tk, ATT_QC), F32), pltpu.VMEM((2 * nqc, tk, ATT_QC), F32),
                        pltpu.VMEM((2 * nqc, V_AUG, ATT_QC), F32)],
        compiler_params=_params(("parallel", "parallel", "parallel"), 48),
        name="flash_attn",
    )(qt, k, vt)


def _outproj_body(yh_ref, ya_ref, x_ref, og_ref, w_ref, mg_ref, wr_ref, xm_ref, xn_ref, aff_ref, affn_ref):
    og = og_ref[...]
    half = D_MODEL // 2
    y = jnp.concatenate([_rms(yh_ref[...]) * og[:, :half], _rms(ya_ref[...]) * og[:, half:]], axis=1)
    xm = x_ref[...] + jnp.dot(y.astype(BF16), w_ref[...], preferred_element_type=F32)
    xm_ref[...] = xm
    xn = _rms(xm) * mg_ref[...]
    xn_ref[...] = xn.astype(BF16)
    logits = jnp.dot(xn, wr_ref[...], precision=HIGHEST, preferred_element_type=F32)
    lane = lax.broadcasted_iota(I32, logits.shape, 1)
    logits = jnp.where(lane < N_EXPERTS, logits, -jnp.inf)
    e = jnp.exp(logits - jnp.max(logits, axis=-1, keepdims=True))
    aff = e / jnp.sum(e, axis=-1, keepdims=True)
    aff_ref[...] = aff.T[:N_EXPERTS]
    affn_ref[...] = aff[:, :N_EXPERTS]


def _outproj(yh, ya, x2d, og, w_out, mg, wr_pad):
    n = x2d.shape[0]
    tm = min(512, n)
    half = D_MODEL // 2
    const = lambda shape: pl.BlockSpec(shape, lambda i: (0,) * len(shape))
    return pl.pallas_call(
        _outproj_body,
        grid=(n // tm,),
        in_specs=[pl.BlockSpec((tm, half), lambda i: (i, 0)),
                  pl.BlockSpec((tm, half), lambda i: (i, 0)),
                  pl.BlockSpec((tm, D_MODEL), lambda i: (i, 0)),
                  const((1, D_MODEL)), const((D_MODEL, D_MODEL)), const((1, D_MODEL)),
                  const((D_MODEL, LANES))],
        out_specs=[pl.BlockSpec((tm, D_MODEL), lambda i: (i, 0)),
                   pl.BlockSpec((tm, D_MODEL), lambda i: (i, 0)),
                   pl.BlockSpec((N_EXPERTS, tm), lambda i: (0, i)),
                   pl.BlockSpec((tm, N_EXPERTS), lambda i: (i, 0))],
        out_shape=[jax.ShapeDtypeStruct((n, D_MODEL), F32),
                   jax.ShapeDtypeStruct((n, D_MODEL), BF16),
                   jax.ShapeDtypeStruct((N_EXPERTS, n), F32),
                   jax.ShapeDtypeStruct((n, N_EXPERTS), F32)],
        compiler_params=_params(("parallel",), 40),
        name="outproj_router",
    )(yh, ya, x2d, og, w_out, mg, wr_pad)


def _select_body(aff_ref, upper_ref, lower_ref, pos_ref, off_ref, *, cap):
    bits = pltpu.bitcast(aff_ref[0], I32)
    upper = upper_ref[...]
    lower = lower_ref[...]

    def count(mask):
        return jnp.sum(jnp.sum(mask.astype(F32), axis=1, keepdims=True), axis=0, keepdims=True)

    def bit_step(i, thr):
        cand = thr | jnp.left_shift(jnp.int32(1), 30 - i)
        return jnp.where(count(bits >= cand) >= cap, cand, thr)

    thr = lax.fori_loop(0, 31, bit_step, jnp.zeros((1, 1), I32))
    gt = bits > thr
    eq = bits == thr
    need = cap - count(gt)

    def prefix(mask):
        within = jnp.dot(mask.astype(BF16), upper, preferred_element_type=F32)
        total = within[:, LANES - 1:LANES]
        offs = jnp.dot(lower, jnp.broadcast_to(total, within.shape).astype(BF16), preferred_element_type=F32)
        return within, offs

    w_eq, o_eq = prefix(eq)
    sel = gt | (eq & (o_eq + w_eq <= need))
    w_sel, o_sel = prefix(sel)
    pos_ref[0] = jnp.where(sel, o_sel + w_sel - 1.0, -1.0).astype(I32)
    off_ref[0] = o_sel.astype(I32)


def _select(aff_t, cap):
    ne, n = aff_t.shape
    rows = n // LANES
    upper = (jnp.arange(LANES)[:, None] <= jnp.arange(LANES)[None, :]).astype(BF16)
    lower = (jnp.arange(rows)[None, :] < jnp.arange(rows)[:, None]).astype(BF16)
    blk = pl.BlockSpec((1, rows, LANES), lambda e: (e, 0, 0))
    pos, off = pl.pallas_call(
        functools.partial(_select_body, cap=cap),
        grid=(ne,),
        in_specs=[blk, pl.BlockSpec((LANES, LANES), lambda e: (0, 0)),
                  pl.BlockSpec((rows, rows), lambda e: (0, 0))],
        out_specs=[blk, blk],
        out_shape=[jax.ShapeDtypeStruct((ne, rows, LANES), I32),
                   jax.ShapeDtypeStruct((ne, rows, LANES), I32)],
        compiler_params=_params(("parallel",), 32),
        name="ec_select",
    )(aff_t.reshape(ne, rows, LANES), upper, lower)
    return pos.reshape(ne, n), off[:, :, 0]


def _ffn_body(off_ref, pos_ref, x_ref, wg_ref, wu_ref, wd_ref, y_ref, xs_ref, *, nb, sub, cap):
    e = pl.program_id(0)
    sb = pl.program_id(1)

    @pl.when(sb == 0)
    def _():
        xs_ref[...] = jnp.zeros(xs_ref.shape, BF16)

    row = lax.broadcasted_iota(I32, (TOK_BLOCK + ALIGN, TOK_BLOCK), 0)
    for i in range(sub):
        base = pl.multiple_of(off_ref[e * nb + sb * sub + i] * ALIGN, ALIGN)
        tok = slice(i * TOK_BLOCK, (i + 1) * TOK_BLOCK)
        hit = (pos_ref[0, :, tok] - base) == row
        win = jnp.dot(hit.astype(BF16), x_ref[tok, :], preferred_element_type=F32).astype(BF16)
        xs_ref[pl.ds(base, ALIGN), :] += win[:ALIGN]
        xs_ref[pl.ds(base + ALIGN, TOK_BLOCK), :] = win[ALIGN:]

    @pl.when(sb == pl.num_programs(1) - 1)
    def _():
        for j in range(cap // SLOT_TILE):
            rows = slice(j * SLOT_TILE, (j + 1) * SLOT_TILE)
            xt = xs_ref[rows, :]
            g = jnp.dot(xt, wg_ref[0], preferred_element_type=F32)
            u = jnp.dot(xt, wu_ref[0], preferred_element_type=F32)
            h = (g * jax.nn.sigmoid(g) * u).astype(BF16)
            y_ref[0, rows, :] = jnp.dot(h, wd_ref[0], preferred_element_type=F32).astype(BF16)
        y_ref[0, cap:, :] = jnp.zeros((y_ref.shape[1] - cap, D_MODEL), BF16)


def _ffn(base, pos3, xn, wg, wu, wd, cap):
    ne = wg.shape[0]
    n = xn.shape[0]
    nb = n // TOK_BLOCK
    sub = min(8, nb)
    cap_pad = cap + TOK_BLOCK + ALIGN
    grid_spec = pltpu.PrefetchScalarGridSpec(
        num_scalar_prefetch=1,
        grid=(ne, nb // sub),
        in_specs=[pl.BlockSpec((1, 1, sub * TOK_BLOCK), lambda e, s, off: (e, 0, s)),
                  pl.BlockSpec((sub * TOK_BLOCK, D_MODEL), lambda e, s, off: (s, 0)),
                  pl.BlockSpec((1, D_MODEL, D_EXPERT), lambda e, s, off: (e, 0, 0)),
                  pl.BlockSpec((1, D_MODEL, D_EXPERT), lambda e, s, off: (e, 0, 0)),
                  pl.BlockSpec((1, D_EXPERT, D_MODEL), lambda e, s, off: (e, 0, 0))],
        out_specs=pl.BlockSpec((1, cap_pad, D_MODEL), lambda e, s, off: (e, 0, 0)),
        scratch_shapes=[pltpu.VMEM((cap_pad, D_MODEL), BF16)])
    return pl.pallas_call(
        functools.partial(_ffn_body, nb=nb, sub=sub, cap=cap),
        grid_spec=grid_spec,
        out_shape=jax.ShapeDtypeStruct((ne, cap_pad, D_MODEL), BF16),
        compiler_params=_params(("parallel", "arbitrary"), 48),
        name="ec_ffn",
    )(base, pos3, xn, wg, wu, wd)


def _combine_body(off_ref, spill_ref, post_ref, affn_ref, xm_ref, *refs, nb, ne):
    win_refs, tail_refs, o_ref = refs[:ne], refs[ne:2 * ne], refs[2 * ne]
    b = pl.program_id(0)
    post = post_ref[...]
    affn = affn_ref[...]

    def expand(y_refs, shift):
        width = y_refs[0].shape[0]
        lane = lax.broadcasted_iota(I32, (TOK_BLOCK, width), 1)
        total = jnp.zeros((TOK_BLOCK, D_MODEL), F32)
        for e in range(ne):
            rel = post[:, e:e + 1] - (off_ref[e * nb + b] * ALIGN + shift)
            hit = (rel == lane).astype(BF16)
            total = total + affn[:, e:e + 1] * jnp.dot(hit, y_refs[e][...], preferred_element_type=F32)
        return total

    o_ref[...] = xm_ref[...] + expand(win_refs, 0)

    @pl.when(spill_ref[b] != 0)
    def _():
        o_ref[...] += expand(tail_refs, TOK_BLOCK)


def _combine(base, spill, pos_t, aff_n, xm, y):
    ne = y.shape[0]
    n = xm.shape[0]
    nb = n // TOK_BLOCK

    def window(e, rows, shift):
        return pl.BlockSpec((pl.Squeezed(), pl.Element(rows), pl.Element(D_MODEL)),
                            lambda b, off, sp: (e, (off[e * nb + b] + shift // ALIGN) * ALIGN, 0))

    grid_spec = pltpu.PrefetchScalarGridSpec(
        num_scalar_prefetch=2,
        grid=(nb,),
        in_specs=([pl.BlockSpec((TOK_BLOCK, ne), lambda b, off, sp: (b, 0)),
                   pl.BlockSpec((TOK_BLOCK, ne), lambda b, off, sp: (b, 0)),
                   pl.BlockSpec((TOK_BLOCK, D_MODEL), lambda b, off, sp: (b, 0))]
                  + [window(e, TOK_BLOCK, 0) for e in range(ne)]
                  + [window(e, ALIGN, TOK_BLOCK) for e in range(ne)]),
        out_specs=pl.BlockSpec((TOK_BLOCK, D_MODEL), lambda b, off, sp: (b, 0)))
    return pl.pallas_call(
        functools.partial(_combine_body, nb=nb, ne=ne),
        grid_spec=grid_spec,
        out_shape=jax.ShapeDtypeStruct((n, D_MODEL), F32),
        compiler_params=_params(("parallel",), 48),
        name="ec_combine",
    )(base, spill, pos_t, aff_n, xm, *([y] * (2 * ne)))


def _ec_moe(xm, xn, aff_t, aff_n, wg, wu, wd):
    ne, n = aff_t.shape
    cap = max(1, EC_CAPACITY * n // N_EXPERTS)
    nb = n // TOK_BLOCK
    pos, off = _select(aff_t, cap)
    first = off[:, ::TOK_BLOCK // LANES]
    count = jnp.concatenate([first[:, 1:], jnp.full((ne, 1), cap, I32)], axis=1) - first
    base = first // ALIGN
    spill = jnp.any(first - base * ALIGN + count > TOK_BLOCK, axis=0).astype(I32)
    base = base.reshape(ne * nb).astype(I32)
    y = _ffn(base, pos.reshape(ne, 1, n), xn, wg, wu, wd, cap)
    return _combine(base, spill, pos.T, aff_n, xm, y)


def _slot_cols(w, head_width):
    k = w.shape[0]
    w3 = w.reshape(k, N_HEADS, head_width)
    return jnp.pad(w3, ((0, 0), (0, 0), (0, SLOT - head_width))).reshape(k, N_HEADS * SLOT)


def _attn_weights(q_a_norm, w_uq, kv_a_norm, w_ukv, q_norm, k_norm):
    wkv = w_ukv.reshape(KV_LORA, N_HEADS, QK_NOPE + V_HEAD)
    wk = _slot_cols(wkv[:, :, :QK_NOPE].reshape(KV_LORA, -1), QK_NOPE)
    wv = wkv[:, :, QK_NOPE:].reshape(KV_LORA, D_ATTN)
    pad_gain = lambda g: jnp.pad(g, (0, SLOT - QK_HEAD))[None, :]
    return dict(wq=_slot_cols(w_uq, QK_HEAD).astype(BF16), wk=wk.astype(BF16), wv=wv.astype(BF16),
                qa=q_a_norm[None, :], kva=kv_a_norm[None, :], qg=pad_gain(q_norm), kg=pad_gain(k_norm))


def _rope_tables(seq):
    pos = jnp.arange(seq, dtype=F32)
    inv_freq = ROPE_THETA ** (-jnp.arange(0, QK_ROPE, 2, dtype=F32) / QK_ROPE)
    ang = pos[:, None] * inv_freq
    ang = jnp.concatenate([ang, ang], axis=-1)
    pad = lambda t, fill: jnp.concatenate(
        [jnp.full((seq, QK_NOPE), fill, F32), t, jnp.full((seq, SLOT - QK_HEAD), fill, F32)], axis=1)
    return pad(jnp.cos(ang), 1.0), pad(jnp.sin(ang), 0.0)


def _filter_weights(w1, b1, w2, b2, w3, freq, decay):
    hid = FILTER_HIDDEN
    padc = lambda a: jnp.pad(a, ((0, 0), (0, LANES - a.shape[1])))
    pad2 = lambda a: jnp.pad(a, ((0, LANES - a.shape[0]), (0, LANES - a.shape[1])))
    bands = jnp.linspace(1e-4, FILTER_BANDS - 1, FILTER_BANDS, dtype=F32)
    cols = HYENA_ORDER * D_HYENA
    w3d = jnp.transpose(w3.reshape(hid, N_DIR, cols), (1, 0, 2))
    return dict(bands=padc(bands[None, :]), w1t=padc(w1[0:1]), w1c=pad2(w1[1:1 + FILTER_BANDS]),
                w1s=pad2(w1[1 + FILTER_BANDS:]), b1=padc(b1[None, :]), w2=pad2(w2), b2=padc(b2[None, :]),
                freq=padc(freq[None, :]), w3=jnp.pad(w3d, ((0, 0), (0, LANES - hid), (0, 0))),
                decay=decay.reshape(N_DIR, 1, cols))


def _trunk(x, p):
    bsz, seq, _ = x.shape
    n = bsz * seq
    cos_t, sin_t = _rope_tables(seq)
    x2 = x.reshape(n, D_MODEL)
    for l in range(DEPTH):
        uh, cq, ckvr = _inproj(x2, p["attn_norm"][l][None, :], p["w_in"][l])
        y_h = _hyena(uh.reshape(bsz, seq, COL_HYENA), p["conv_w"][l], p["conv_b"][l][None, :],
                     p["filt"][l], p["hyena_bias"][l])
        qt, k, vt = _qkv(cq.reshape(bsz, seq, -1), ckvr.reshape(bsz, seq, -1), cos_t, sin_t, p["attn"][l])
        y_a = _flash(qt, k, vt)
        xm, xn, aff_t, aff_n = _outproj(y_h.reshape(n, D_HYENA), y_a.reshape(n, D_ATTN), x2,
                                 p["out_norm"][l][None, :], p["w_out"][l], p["mlp_norm"][l][None, :],
                                 p["w_router"][l])
        x2 = _ec_moe(xm, xn, aff_t, aff_n, p["w_gate"][l], p["w_up"][l], p["w_down"][l])
    return x2.reshape(bsz, seq, D_MODEL)


def kernel(x_prompt, x_sample, attn_norm, w_in, conv_w, conv_b, filt_w1, filt_b1, filt_w2, filt_b2, filt_w3,
           filt_freq, filt_decay, hyena_bias, q_a_norm, w_uq, kv_a_norm, w_ukv, q_norm, k_norm, out_norm,
           w_out, mlp_norm, w_router, w_gate, w_up, w_down):
    p = dict(
        attn_norm=attn_norm,
        w_in=jnp.pad(w_in, ((0, 0), (0, 0), (0, D_IN_PAD - D_IN))).astype(BF16),
        conv_w=conv_w, conv_b=conv_b, hyena_bias=hyena_bias,
        filt=[_filter_weights(filt_w1[l], filt_b1[l], filt_w2[l], filt_b2[l], filt_w3[l], filt_freq[l],
                              filt_decay[l]) for l in range(DEPTH)],
        attn=[_attn_weights(q_a_norm[l], w_uq[l], kv_a_norm[l], w_ukv[l], q_norm[l], k_norm[l])
              for l in range(DEPTH)],
        out_norm=out_norm, w_out=w_out.astype(BF16), mlp_norm=mlp_norm,
        w_router=jnp.pad(w_router, ((0, 0), (0, 0), (0, LANES - N_EXPERTS))),
        w_gate=w_gate.astype(BF16), w_up=w_up.astype(BF16), w_down=w_down.astype(BF16))
    return (_trunk(x_prompt, p), _trunk(x_sample, p))
```

```python
import functools
import math

import jax
import jax.numpy as jnp
from jax import lax
from jax.experimental import pallas as pl
from jax.experimental.pallas import tpu as pltpu

F32 = jnp.float32
BF16 = jnp.bfloat16
I32 = jnp.int32
HIGHEST = lax.Precision.HIGHEST

D_MODEL = 1024
DEPTH = 2
D_HYENA = 512
HYENA_ORDER = 2
FILTER_BANDS = 16
FILTER_HIDDEN = 64
N_DIR = 2
DECAY_SHIFT = 0.05
N_HEADS = 8
QK_NOPE = 64
QK_ROPE = 32
QK_HEAD = QK_NOPE + QK_ROPE
V_HEAD = 64
V_AUG = V_HEAD + 16
D_ATTN = N_HEADS * V_HEAD
Q_LORA = 256
KV_LORA = 128
ROPE_THETA = 10000.0
N_EXPERTS = 16
EC_CAPACITY = 2
D_EXPERT = 512
EPS = 1e-6
COL_HYENA = (HYENA_ORDER + 1) * D_HYENA
COL_Q = COL_HYENA + Q_LORA
COL_KV = COL_Q + KV_LORA
D_IN = COL_KV + QK_ROPE
D_IN_PAD = 2048

LANES = 128
SLOT = 128
TOK_BLOCK = 256
SLOT_TILE = 256
ALIGN = 16
GATHER_ROWS = 64
ATT_TQ = 512
ATT_QC = 256
ATT_TK = 256
ATT_UNROLL = 8
MIB = 1024 * 1024


def _params(sem, vmem_mib):
    return pltpu.CompilerParams(dimension_semantics=sem, vmem_limit_bytes=vmem_mib * MIB)


def _rms(x):
    return x * lax.rsqrt(jnp.mean(x * x, axis=-1, keepdims=True) + EPS)


def _inproj_body(x_ref, g_ref, w_ref, uh_ref, cq_ref, ckvr_ref):
    xn = _rms(x_ref[...]) * g_ref[...]
    u = jnp.dot(xn.astype(BF16), w_ref[...], preferred_element_type=F32)
    uh_ref[...] = u[:, :COL_HYENA]
    cq_ref[...] = u[:, COL_HYENA:COL_Q]
    ckvr_ref[...] = u[:, COL_Q:]


def _inproj(x2d, g, w_pad):
    n = x2d.shape[0]
    tm = min(512, n)
    return pl.pallas_call(
        _inproj_body,
        grid=(n // tm,),
        in_specs=[pl.BlockSpec((tm, D_MODEL), lambda i: (i, 0)),
                  pl.BlockSpec((1, D_MODEL), lambda i: (0, 0)),
                  pl.BlockSpec((D_MODEL, D_IN_PAD), lambda i: (0, 0))],
        out_specs=[pl.BlockSpec((tm, COL_HYENA), lambda i: (i, 0)),
                   pl.BlockSpec((tm, Q_LORA), lambda i: (i, 0)),
                   pl.BlockSpec((tm, D_IN_PAD - COL_Q), lambda i: (i, 0))],
        out_shape=[jax.ShapeDtypeStruct((n, COL_HYENA), F32),
                   jax.ShapeDtypeStruct((n, Q_LORA), F32),
                   jax.ShapeDtypeStruct((n, D_IN_PAD - COL_Q), F32)],
        compiler_params=_params(("parallel",), 40),
        name="inproj",
    )(x2d, g, w_pad)


def _sconv_body(u_ref, prev_ref, next_ref, w_ref, b_ref, v_ref, x1_ref, x2_ref):
    i = pl.program_id(1)
    last = pl.num_programs(1) - 1
    u = u_ref[0]
    tl = u.shape[0]
    prev_row = jnp.where(i == 0, 0.0, prev_ref[0, 7:8, :])
    next_row = jnp.where(i == last, 0.0, next_ref[0, 0:1, :])
    row = lax.broadcasted_iota(I32, u.shape, 0)
    up = jnp.where(row == 0, prev_row, pltpu.roll(u, 1, axis=0))
    dn = jnp.where(row == tl - 1, next_row, pltpu.roll(u, tl - 1, axis=0))
    y = up * w_ref[0:1, :] + u * w_ref[1:2, :] + dn * w_ref[2:3, :] + b_ref[...]
    v_ref[0] = y[:, :D_HYENA]
    x1_ref[0] = y[:, D_HYENA:2 * D_HYENA]
    x2_ref[0] = y[:, 2 * D_HYENA:]


def _sconv(uh, w, b):
    bsz, seq, c = uh.shape
    tl = min(512, seq)
    r = tl // 8
    nblk8 = seq // 8
    out = jax.ShapeDtypeStruct((bsz, seq, D_HYENA), F32)
    ospec = pl.BlockSpec((1, tl, D_HYENA), lambda bi, i: (bi, i, 0))
    return pl.pallas_call(
        _sconv_body,
        grid=(bsz, seq // tl),
        in_specs=[pl.BlockSpec((1, tl, c), lambda bi, i: (bi, i, 0)),
                  pl.BlockSpec((1, 8, c), lambda bi, i: (bi, jnp.maximum(i * r - 1, 0), 0)),
                  pl.BlockSpec((1, 8, c), lambda bi, i: (bi, jnp.minimum((i + 1) * r, nblk8 - 1), 0)),
                  pl.BlockSpec((3, c), lambda bi, i: (0, 0)),
                  pl.BlockSpec((1, c), lambda bi, i: (0, 0))],
        out_specs=[ospec, ospec, ospec],
        out_shape=[out, out, out],
        compiler_params=_params(("parallel", "parallel"), 40),
        name="sconv",
    )(uh, uh, uh, w, b)


def _filter_body(bands_ref, w1t_ref, w1c_ref, w1s_ref, b1_ref, w2_ref, b2_ref, fr_ref, w3_ref, dec_ref,
                 k_ref, sum_ref, *, seq):
    i = pl.program_id(0)
    tr = k_ref.shape[0]
    n = i * tr + lax.broadcasted_iota(I32, (tr, 1), 0)
    t_idx = jnp.where(n < seq, n, 2 * seq - n).astype(F32)
    t = t_idx / (seq - 1)
    ang = 2.0 * math.pi * bands_ref[...] * t_idx / seq
    fr = fr_ref[...]
    pre = (t * w1t_ref[...]
           + jnp.dot(jnp.cos(ang), w1c_ref[...], precision=HIGHEST, preferred_element_type=F32)
           - jnp.dot(jnp.sin(ang), w1s_ref[...], precision=HIGHEST, preferred_element_type=F32))
    h = jnp.sin(fr * (pre + b1_ref[...]))
    h = jnp.sin(fr * (jnp.dot(h, w2_ref[...], precision=HIGHEST, preferred_element_type=F32) + b2_ref[...]))
    h = jnp.dot(h, w3_ref[0], precision=HIGHEST, preferred_element_type=F32)
    window = jnp.exp(-t * dec_ref[0]) + DECAY_SHIFT
    k = jnp.where(n == seq, 0.0, h * window)
    k_ref[...] = k

    @pl.when(i == 0)
    def _():
        sum_ref[...] = jnp.zeros_like(sum_ref)

    sum_ref[...] += jnp.sum(jnp.abs(k), axis=0, keepdims=True)


def _filters(seq, fw):
    cols = HYENA_ORDER * D_HYENA
    tr = min(512, seq)
    half = seq // tr
    const = lambda shape: pl.BlockSpec(shape, lambda i: (0,) * len(shape))
    return pl.pallas_call(
        functools.partial(_filter_body, seq=seq),
        grid=(2 * seq // tr,),
        in_specs=[const((1, LANES)), const((1, LANES)), const((LANES, LANES)), const((LANES, LANES)),
                  const((1, LANES)), const((LANES, LANES)), const((1, LANES)), const((1, LANES)),
                  pl.BlockSpec((1, LANES, cols), lambda i: (i // half, 0, 0)),
                  pl.BlockSpec((1, 1, cols), lambda i: (i // half, 0, 0))],
        out_specs=[pl.BlockSpec((tr, cols), lambda i: (i, 0)),
                   pl.BlockSpec((1, cols), lambda i: (0, 0))],
        out_shape=[jax.ShapeDtypeStruct((2 * seq, cols), F32),
                   jax.ShapeDtypeStruct((1, cols), F32)],
        compiler_params=_params(("arbitrary",), 32),
        name="filter_gen",
    )(fw["bands"], fw["w1t"], fw["w1c"], fw["w1s"], fw["b1"], fw["w2"], fw["b2"], fw["freq"],
      fw["w3"], fw["decay"])


def _fft_dims(seq):
    n2 = 128 if 2 * seq >= 32768 else 64
    n1 = 2 * seq // n2
    return n1, n2


def _dft_tables(n1, n2):
    n = n1 * n2
    n1h = n1 // 2
    k1 = jnp.arange(n1h, dtype=I32)[:, None]
    m1 = jnp.arange(n1, dtype=I32)[None, :]
    ang = (2.0 * math.pi / n1) * ((k1 * m1) % n1).astype(F32)
    top = jnp.cos(ang)
    bot = -jnp.sin(ang)
    nyq = jnp.where(m1 % 2 == 0, 1.0, -1.0).astype(F32)
    bot = jnp.concatenate([nyq, bot[1:]], axis=0)
    fa = jnp.concatenate([top, bot], axis=0)
    weight = jnp.where((jnp.arange(n1) % n1h) == 0, 1.0, 2.0).astype(F32) / n
    fi = (fa[:, :n1h] * weight[:, None]).T

    kk = jnp.arange(n1h + 1, dtype=I32)[:, None, None]
    k2 = jnp.arange(n2, dtype=I32)[None, :, None]
    m2 = jnp.arange(n2, dtype=I32)[None, None, :]
    phi = (2.0 * math.pi / n) * ((m2 * (kk + n1 * k2)) % n).astype(F32)
    gr = jnp.cos(phi)
    gi = -jnp.sin(phi)
    blk = jnp.concatenate([jnp.concatenate([gr, -gi], axis=2),
                           jnp.concatenate([gi, gr], axis=2)], axis=1)
    left = (jnp.arange(2 * n2) < n2)[None, :]
    g0 = jnp.where(left, blk[0], 0.0)
    gf = jnp.concatenate([g0[None], blk[1:n1h]], axis=0)
    gnyq = jnp.concatenate([jnp.zeros((2 * n2, n2), F32), blk[n1h][:, :n2]], axis=1)
    return dict(fa_full=fa.astype(BF16), fa_half=fa[:, :n1h].astype(BF16), fi=fi.astype(BF16),
                gf=gf.astype(BF16), gnyq=gnyq.astype(BF16),
                ginv=jnp.transpose(gf, (0, 2, 1)).astype(BF16), m2=gnyq.T.astype(BF16))


def _fft_a_body(x_ref, f_ref, o_ref):
    a = jnp.dot(f_ref[...], x_ref[0].astype(BF16), preferred_element_type=F32)
    n1h = o_ref.shape[2]
    o_ref[0, 0] = a[:n1h].astype(BF16)
    o_ref[0, 1] = a[n1h:].astype(BF16)


def _fft_a(x3, fmat):
    bsz, r, nc = x3.shape
    n1 = fmat.shape[0]
    tn = min(2048, nc)
    return pl.pallas_call(
        _fft_a_body,
        grid=(bsz, nc // tn),
        in_specs=[pl.BlockSpec((1, r, tn), lambda b, j: (b, 0, j)),
                  pl.BlockSpec((n1, r), lambda b, j: (0, 0))],
        out_specs=pl.BlockSpec((1, 2, n1 // 2, tn), lambda b, j: (b, 0, 0, j)),
        out_shape=jax.ShapeDtypeStruct((bsz, 2, n1 // 2, nc), BF16),
        compiler_params=_params(("parallel", "parallel"), 32),
        name="fft_stage_a",
    )(x3, fmat)


def _fft_b_body(ar_ref, ai_ref, g_ref, gn_ref, sc_ref, x_ref, xn_ref, *, kb):
    j = pl.program_id(2)
    inv = 1.0 / sc_ref[...]
    for kk in range(kb):
        rhs = jnp.concatenate([ar_ref[0, 0, kk], ai_ref[0, 0, kk]], axis=0)
        x_ref[0, kk] = (jnp.dot(g_ref[kk], rhs, preferred_element_type=F32) * inv).astype(BF16)

    @pl.when(j == 0)
    def _():
        rhs = jnp.concatenate([ar_ref[0, 0, 0], ai_ref[0, 0, 0]], axis=0)
        xn_ref[0] = (jnp.dot(gn_ref[...], rhs, preferred_element_type=F32) * inv).astype(BF16)


def _fft_b(a5, tabs, scale):
    bsz, _, n1h, n2, c = a5.shape
    kb = 8
    tc = min(512, c)
    return pl.pallas_call(
        functools.partial(_fft_b_body, kb=kb),
        grid=(bsz, c // tc, n1h // kb),
        in_specs=[pl.BlockSpec((1, 1, kb, n2, tc), lambda b, ci, j: (b, 0, j, 0, ci)),
                  pl.BlockSpec((1, 1, kb, n2, tc), lambda b, ci, j: (b, 1, j, 0, ci)),
                  pl.BlockSpec((kb, 2 * n2, 2 * n2), lambda b, ci, j: (j, 0, 0)),
                  pl.BlockSpec((2 * n2, 2 * n2), lambda b, ci, j: (0, 0)),
                  pl.BlockSpec((1, tc), lambda b, ci, j: (0, ci))],
        out_specs=[pl.BlockSpec((1, kb, 2 * n2, tc), lambda b, ci, j: (b, j, 0, ci)),
                   pl.BlockSpec((1, 2 * n2, tc), lambda b, ci, j: (b, 0, ci))],
        out_shape=[jax.ShapeDtypeStruct((bsz, n1h, 2 * n2, c), BF16),
                   jax.ShapeDtypeStruct((bsz, 2 * n2, c), BF16)],
        compiler_params=_params(("parallel", "parallel", "arbitrary"), 40),
        name="fft_stage_b",
    )(a5, a5, tabs["gf"], tabs["gnyq"], scale)


def _cmul(x, k, n2):
    xr, xi = x[:n2], x[n2:]
    kr, ki = k[:n2], k[n2:]
    return jnp.concatenate([xr * kr - xi * ki, xr * ki + xi * kr], axis=0).astype(BF16)


def _spec_body(ar_ref, ai_ref, k_ref, kn_ref, g_ref, gn_ref, gi_ref, m2_ref, o_ref, *, kb):
    j = pl.program_id(1)
    n2 = o_ref.shape[3]

    def through(fwd, a_re, a_im, kf, inv):
        x = jnp.dot(fwd, jnp.concatenate([a_re, a_im], axis=0), preferred_element_type=F32)
        return jnp.dot(inv, _cmul(x, kf.astype(F32), n2), preferred_element_type=F32)

    for kk in range(kb):
        acc = through(g_ref[kk], ar_ref[0, 0, kk], ai_ref[0, 0, kk], k_ref[0, kk], gi_ref[kk])
        o_ref[0, 0, kk] = acc[:n2].astype(BF16)
        o_ref[0, 1, kk] = acc[n2:].astype(BF16)

    @pl.when(j == 0)
    def _():
        acc = through(gn_ref[...], ar_ref[0, 0, 0], ai_ref[0, 0, 0], kn_ref[0], m2_ref[...])
        o_ref[0, 1, 0] = acc[n2:].astype(BF16)


def _spec_conv(a5, kf, kfnyq, order, tabs):
    bsz, _, n1h, n2, c = a5.shape
    tn2 = 2 * n2
    kb = 8
    return pl.pallas_call(
        functools.partial(_spec_body, kb=kb),
        grid=(bsz, n1h // kb),
        in_specs=[pl.BlockSpec((1, 1, kb, n2, c), lambda b, j: (b, 0, j, 0, 0)),
                  pl.BlockSpec((1, 1, kb, n2, c), lambda b, j: (b, 1, j, 0, 0)),
                  pl.BlockSpec((1, kb, tn2, c), lambda b, j: (0, j, 0, order)),
                  pl.BlockSpec((1, tn2, c), lambda b, j: (0, 0, order)),
                  pl.BlockSpec((kb, tn2, tn2), lambda b, j: (j, 0, 0)),
                  pl.BlockSpec((tn2, tn2), lambda b, j: (0, 0)),
                  pl.BlockSpec((kb, tn2, tn2), lambda b, j: (j, 0, 0)),
                  pl.BlockSpec((tn2, tn2), lambda b, j: (0, 0))],
        out_specs=pl.BlockSpec((1, 2, kb, n2, c), lambda b, j: (b, 0, j, 0, 0)),
        out_shape=jax.ShapeDtypeStruct((bsz, 2, n1h, n2, c), BF16),
        compiler_params=_params(("parallel", "arbitrary"), 48),
        name="spectral_conv",
    )(a5, a5, kf, kfnyq, tabs["gf"], tabs["gnyq"], tabs["ginv"], tabs["m2"])


def _ifft_a_body(b_ref, f_ref, z_ref, gate_ref, bias_ref, o_ref):
    y = jnp.dot(f_ref[...], b_ref[0], preferred_element_type=F32)
    o_ref[0] = gate_ref[0] * (y + z_ref[0] * bias_ref[...])


def _ifft_a(b3, fi, z3, gate3, bias_t):
    bsz, n1, nc = b3.shape
    n1h = n1 // 2
    tn = bias_t.shape[1]
    return pl.pallas_call(
        _ifft_a_body,
        grid=(bsz, nc // tn),
        in_specs=[pl.BlockSpec((1, n1, tn), lambda b, j: (b, 0, j)),
                  pl.BlockSpec((n1h, n1), lambda b, j: (0, 0)),
                  pl.BlockSpec((1, n1h, tn), lambda b, j: (b, 0, j)),
                  pl.BlockSpec((1, n1h, tn), lambda b, j: (b, 0, j)),
                  pl.BlockSpec((1, tn), lambda b, j: (0, 0))],
        out_specs=pl.BlockSpec((1, n1h, tn), lambda b, j: (b, 0, j)),
        out_shape=jax.ShapeDtypeStruct((bsz, n1h, nc), F32),
        compiler_params=_params(("parallel", "parallel"), 32),
        name="ifft_stage_a",
    )(b3, fi, z3, gate3, bias_t)


def _hyena(uh, conv_w, conv_b, fw, bias):
    bsz, seq, _ = uh.shape
    c = D_HYENA
    n1, n2 = _fft_dims(seq)
    n1h = n1 // 2
    tabs = _dft_tables(n1, n2)
    v, x1, x2 = _sconv(uh, conv_w, conv_b)

    kcirc, ksum = _filters(seq, fw)
    cols = HYENA_ORDER * c
    ka = _fft_a(kcirc.reshape(1, n1, n2 * cols), tabs["fa_full"])
    kf, kfnyq = _fft_b(ka.reshape(1, 2, n1h, n2, cols), tabs, ksum)

    tn = min(2048, n2 * c)
    z = v
    for order, gate in enumerate((x1, x2)):
        za = _fft_a(z.reshape(bsz, n1h, n2 * c), tabs["fa_half"])
        zb = _spec_conv(za.reshape(bsz, 2, n1h, n2, c), kf, kfnyq, order, tabs)
        bias_t = jnp.tile(bias[order][None, :], (1, tn // c))
        z = _ifft_a(zb.reshape(bsz, n1, n2 * c), tabs["fi"], z.reshape(bsz, n1h, n2 * c),
                    gate.reshape(bsz, n1h, n2 * c), bias_t).reshape(bsz, seq, c)
    return z


def _qkv_body(cq_ref, ckvr_ref, cos_ref, sin_ref, wq_ref, wk_ref, wv_ref, qa_ref, kva_ref, qg_ref, kg_ref,
              qt_ref, k_ref, vt_ref):
    cqn = _rms(cq_ref[0]) * qa_ref[...]
    qs = jnp.dot(cqn.astype(BF16), wq_ref[...], preferred_element_type=F32)
    ck = ckvr_ref[0]
    ckvn = (_rms(ck[:, :KV_LORA]) * kva_ref[...]).astype(BF16)
    ks = jnp.dot(ckvn, wk_ref[...], preferred_element_type=F32)
    vs = jnp.dot(ckvn, wv_ref[...], preferred_element_type=F32)
    krope = pltpu.roll(ck[:, KV_LORA:], QK_NOPE, axis=1)
    cos = cos_ref[...]
    sin = sin_ref[...]
    lane = lax.broadcasted_iota(I32, cos.shape, 1)
    half = QK_ROPE // 2
    first = (lane >= QK_NOPE) & (lane < QK_NOPE + half)
    second = (lane >= QK_NOPE + half) & (lane < QK_HEAD)

    def head(xh, gain, scale):
        ms = jnp.sum(xh * xh, axis=-1, keepdims=True) * (1.0 / QK_HEAD)
        xn = xh * lax.rsqrt(ms + EPS) * gain
        rot = jnp.where(first, -pltpu.roll(xn, SLOT - half, axis=1),
                        jnp.where(second, pltpu.roll(xn, half, axis=1), 0.0))
        return (xn * cos + rot * sin) * scale

    q_scale = QK_HEAD ** -0.5 * math.log2(math.e)
    for h in range(N_HEADS):
        sl = slice(h * SLOT, (h + 1) * SLOT)
        qt_ref[0, h] = head(qs[:, sl], qg_ref[...], q_scale).T.astype(BF16)
        k_ref[0, h] = head(ks[:, sl] + krope, kg_ref[...], 1.0).astype(BF16)
    aug = lax.broadcasted_iota(I32, (V_AUG - V_HEAD, cos.shape[0]), 0)
    ones_row = jnp.where(aug == 0, 1.0, 0.0)
    for hp in range(N_HEADS // 2):
        pair_t = vs[:, hp * 2 * V_HEAD:(hp + 1) * 2 * V_HEAD].T
        vt_ref[0, hp] = jnp.concatenate(
            [pair_t[:V_HEAD], ones_row, pair_t[V_HEAD:], ones_row], axis=0).astype(BF16)


def _qkv(cq, ckvr, cos_t, sin_t, aw):
    bsz, seq, _ = cq.shape
    tm = min(512, seq)
    const = lambda shape: pl.BlockSpec(shape, lambda b, i: (0,) * len(shape))
    return pl.pallas_call(
        _qkv_body,
        grid=(bsz, seq // tm),
        in_specs=[pl.BlockSpec((1, tm, Q_LORA), lambda b, i: (b, i, 0)),
                  pl.BlockSpec((1, tm, D_IN_PAD - COL_Q), lambda b, i: (b, i, 0)),
                  pl.BlockSpec((tm, SLOT), lambda b, i: (i, 0)),
                  pl.BlockSpec((tm, SLOT), lambda b, i: (i, 0)),
                  const((Q_LORA, N_HEADS * SLOT)), const((KV_LORA, N_HEADS * SLOT)),
                  const((KV_LORA, D_ATTN)),
                  const((1, Q_LORA)), const((1, KV_LORA)), const((1, SLOT)), const((1, SLOT))],
        out_specs=[pl.BlockSpec((1, N_HEADS, SLOT, tm), lambda b, i: (b, 0, 0, i)),
                   pl.BlockSpec((1, N_HEADS, tm, SLOT), lambda b, i: (b, 0, i, 0)),
                   pl.BlockSpec((1, N_HEADS // 2, 2 * V_AUG, tm), lambda b, i: (b, 0, 0, i))],
        out_shape=[jax.ShapeDtypeStruct((bsz, N_HEADS, SLOT, seq), BF16),
                   jax.ShapeDtypeStruct((bsz, N_HEADS, seq, SLOT), BF16),
                   jax.ShapeDtypeStruct((bsz, N_HEADS // 2, 2 * V_AUG, seq), BF16)],
        compiler_params=_params(("parallel", "parallel"), 40),
        name="qkv_prep",
    )(cq, ckvr, cos_t, sin_t, aw["wq"], aw["wk"], aw["wv"], aw["qa"], aw["kva"], aw["qg"], aw["kg"])


def _flash_body(qt_ref, k_ref, vt_ref, o_ref, s0_ref, s1_ref, acc_ref, *, tk, nk, nqc, unroll):
    chains = [(hh, qc) for hh in range(2) for qc in range(nqc)]
    acc_ref[...] = jnp.zeros(acc_ref.shape, F32)

    def qk(t, s_ref):
        ks = pl.multiple_of(t * tk, tk)
        tile_max = []
        for c, (hh, qc) in enumerate(chains):
            k = k_ref[0, hh, pl.ds(ks, tk), :]
            s = jnp.dot(k, qt_ref[0, hh, :, qc * ATT_QC:(qc + 1) * ATT_QC], preferred_element_type=F32)
            s_ref[c] = s
            tile_max.append(jnp.max(s, axis=0, keepdims=True))
        return tile_max

    def softmax_pv(t, s_ref, tile_max, m):
        ks = pl.multiple_of(t * tk, tk)
        m_out = []
        for c, (hh, qc) in enumerate(chains):
            m_new = jnp.maximum(m[c], tile_max[c])
            a = jnp.exp2(m[c] - m_new)
            p = jnp.exp2(s_ref[c] - m_new)
            m_out.append(m_new)
            vt = vt_ref[0, 0, hh * V_AUG:(hh + 1) * V_AUG, pl.ds(ks, tk)]
            acc_ref[c] = acc_ref[c] * a + jnp.dot(vt, p.astype(BF16), preferred_element_type=F32)
        return m_out

    def group(u, carry):
        tile_max, m = carry
        for i in range(0, unroll, 2):
            t = unroll * u + i
            tile_max1 = qk(t + 1, s1_ref)
            m = softmax_pv(t, s0_ref, tile_max, m)
            tile_max = qk(jnp.minimum(t + 2, nk - 1), s0_ref)
            m = softmax_pv(t + 1, s1_ref, tile_max1, m)
        return tile_max, m

    init = (qk(0, s0_ref), [jnp.full((1, ATT_QC), -jnp.inf, F32)] * len(chains))
    lax.fori_loop(0, nk // unroll, group, init)

    def normalised(c):
        acc = acc_ref[c]
        return acc[:V_HEAD] * (1.0 / acc[V_HEAD:V_HEAD + 1])

    heads = [jnp.concatenate([normalised(hh * nqc + qc) for qc in range(nqc)], axis=1) for hh in range(2)]
    o_ref[0] = jnp.concatenate(heads, axis=0).T


def _flash(qt, k, vt):
    bsz, nh, _, seq = qt.shape
    tq = min(ATT_TQ, seq)
    tk = min(ATT_TK, seq)
    nqc = tq // ATT_QC
    nk = seq // tk
    unroll = min(ATT_UNROLL, nk)
    assert nk % unroll == 0 and unroll % 2 == 0
    return pl.pallas_call(
        functools.partial(_flash_body, tk=tk, nk=nk, nqc=nqc, unroll=unroll),
        grid=(bsz, nh // 2, seq // tq),
        in_specs=[pl.BlockSpec((1, 2, SLOT, tq), lambda b, hp, i: (b, hp, 0, i)),
                  pl.BlockSpec((1, 2, seq, SLOT), lambda b, hp, i: (b, hp, 0, 0)),
                  pl.BlockSpec((1, 1, 2 * V_AUG, seq), lambda b, hp, i: (b, hp, 0, 0))],
        out_specs=pl.BlockSpec((1, tq, 2 * V_HEAD), lambda b, hp, i: (b, i, hp)),
        out_shape=jax.ShapeDtypeStruct((bsz, seq, nh * V_HEAD), F32),
        scratch_shapes=[pltpu.VMEM((2 * nqc, tk, ATT_QC), F32), pltpu.VMEM((2 * nqc, tk, ATT_QC), F32),
                        pltpu.VMEM((2 * nqc, V_AUG, ATT_QC), F32)],
        compiler_params=_params(("parallel", "parallel", "parallel"), 48),
        name="flash_attn",
    )(qt, k, vt)


def _outproj_body(yh_ref, ya_ref, x_ref, og_ref, w_ref, mg_ref, wr_ref, xm_ref, xn_ref, aff_ref, affn_ref):
    og = og_ref[...]
    half = D_MODEL // 2
    y = jnp.concatenate([_rms(yh_ref[...]) * og[:, :half], _rms(ya_ref[...]) * og[:, half:]], axis=1)
    xm = x_ref[...] + jnp.dot(y.astype(BF16), w_ref[...], preferred_element_type=F32)
    xm_ref[...] = xm
    xn = _rms(xm) * mg_ref[...]
    xn_ref[...] = xn.astype(BF16)
    logits = jnp.dot(xn, wr_ref[...], precision=HIGHEST, preferred_element_type=F32)
    lane = lax.broadcasted_iota(I32, logits.shape, 1)
    logits = jnp.where(lane < N_EXPERTS, logits, -jnp.inf)
    e = jnp.exp(logits - jnp.max(logits, axis=-1, keepdims=True))
    aff = e / jnp.sum(e, axis=-1, keepdims=True)
    aff_ref[...] = aff.T[:N_EXPERTS]
    affn_ref[...] = aff[:, :N_EXPERTS]


def _outproj(yh, ya, x2d, og, w_out, mg, wr_pad):
    n = x2d.shape[0]
    tm = min(512, n)
    half = D_MODEL // 2
    const = lambda shape: pl.BlockSpec(shape, lambda i: (0,) * len(shape))
    return pl.pallas_call(
        _outproj_body,
        grid=(n // tm,),
        in_specs=[pl.BlockSpec((tm, half), lambda i: (i, 0)),
                  pl.BlockSpec((tm, half), lambda i: (i, 0)),
                  pl.BlockSpec((tm, D_MODEL), lambda i: (i, 0)),
                  const((1, D_MODEL)), const((D_MODEL, D_MODEL)), const((1, D_MODEL)),
                  const((D_MODEL, LANES))],
        out_specs=[pl.BlockSpec((tm, D_MODEL), lambda i: (i, 0)),
                   pl.BlockSpec((tm, D_MODEL), lambda i: (i, 0)),
                   pl.BlockSpec((N_EXPERTS, tm), lambda i: (0, i)),
                   pl.BlockSpec((tm, N_EXPERTS), lambda i: (i, 0))],
        out_shape=[jax.ShapeDtypeStruct((n, D_MODEL), F32),
                   jax.ShapeDtypeStruct((n, D_MODEL), BF16),
                   jax.ShapeDtypeStruct((N_EXPERTS, n), F32),
                   jax.ShapeDtypeStruct((n, N_EXPERTS), F32)],
        compiler_params=_params(("parallel",), 40),
        name="outproj_router",
    )(yh, ya, x2d, og, w_out, mg, wr_pad)


def _select_body(aff_ref, upper_ref, lower_ref, pos_ref, off_ref, *, cap):
    bits = pltpu.bitcast(aff_ref[0], I32)
    upper = upper_ref[...]
    lower = lower_ref[...]

    def count(mask):
        return jnp.sum(jnp.sum(mask.astype(F32), axis=1, keepdims=True), axis=0, keepdims=True)

    def bit_step(i, thr):
        cand = thr | jnp.left_shift(jnp.int32(1), 30 - i)
        return jnp.where(count(bits >= cand) >= cap, cand, thr)

    thr = lax.fori_loop(0, 31, bit_step, jnp.zeros((1, 1), I32))
    gt = bits > thr
    eq = bits == thr
    need = cap - count(gt)

    def prefix(mask):
        within = jnp.dot(mask.astype(BF16), upper, preferred_element_type=F32)
        total = within[:, LANES - 1:LANES]
        offs = jnp.dot(lower, jnp.broadcast_to(total, within.shape).astype(BF16), preferred_element_type=F32)
        return within, offs

    w_eq, o_eq = prefix(eq)
    sel = gt | (eq & (o_eq + w_eq <= need))
    w_sel, o_sel = prefix(sel)
    pos_ref[0] = jnp.where(sel, o_sel + w_sel - 1.0, -1.0).astype(I32)
    off_ref[0] = o_sel.astype(I32)


def _select(aff_t, cap):
    ne, n = aff_t.shape
    rows = n // LANES
    upper = (jnp.arange(LANES)[:, None] <= jnp.arange(LANES)[None, :]).astype(BF16)
    lower = (jnp.arange(rows)[None, :] < jnp.arange(rows)[:, None]).astype(BF16)
    blk = pl.BlockSpec((1, rows, LANES), lambda e: (e, 0, 0))
    pos, off = pl.pallas_call(
        functools.partial(_select_body, cap=cap),
        grid=(ne,),
        in_specs=[blk, pl.BlockSpec((LANES, LANES), lambda e: (0, 0)),
                  pl.BlockSpec((rows, rows), lambda e: (0, 0))],
        out_specs=[blk, blk],
        out_shape=[jax.ShapeDtypeStruct((ne, rows, LANES), I32),
                   jax.ShapeDtypeStruct((ne, rows, LANES), I32)],
        compiler_params=_params(("parallel",), 32),
        name="ec_select",
    )(aff_t.reshape(ne, rows, LANES), upper, lower)
    return pos.reshape(ne, n), off[:, :, 0]


def _ffn_body(off_ref, nch_ref, pos_ref, x_ref, wg_ref, wu_ref, wd_ref, y_ref, xs_ref, *, nb, sub, cap):
    e = pl.program_id(0)
    sb = pl.program_id(1)

    @pl.when(sb == 0)
    def _():
        xs_ref[...] = jnp.zeros(xs_ref.shape, BF16)

    row = lax.broadcasted_iota(I32, (GATHER_ROWS, TOK_BLOCK), 0)

    def gather(i, c):
        blk = e * nb + sb * sub + i
        base = off_ref[blk] * ALIGN
        tok = slice(i * TOK_BLOCK, (i + 1) * TOK_BLOCK)
        start = pl.multiple_of(base + c * GATHER_ROWS, ALIGN)
        hit = (pos_ref[0, :, tok] - start) == row
        win = jnp.dot(hit.astype(BF16), x_ref[tok, :], preferred_element_type=F32)
        xs_ref[pl.ds(start, GATHER_ROWS), :] += win.astype(BF16)

    for i in range(sub):
        gather(i, 0)
    for i in range(sub):
        lax.fori_loop(1, nch_ref[e * nb + sb * sub + i], lambda c, carry, i=i: (gather(i, c), carry)[1], 0)

    @pl.when(sb == pl.num_programs(1) - 1)
    def _():
        for j in range(cap // SLOT_TILE):
            rows = slice(j * SLOT_TILE, (j + 1) * SLOT_TILE)
            xt = xs_ref[rows, :]
            g = jnp.dot(xt, wg_ref[0], preferred_element_type=F32)
            u = jnp.dot(xt, wu_ref[0], preferred_element_type=F32)
            h = (g * jax.nn.sigmoid(g) * u).astype(BF16)
            y_ref[0, rows, :] = jnp.dot(h, wd_ref[0], preferred_element_type=F32).astype(BF16)
        y_ref[0, cap:, :] = jnp.zeros((y_ref.shape[1] - cap, D_MODEL), BF16)


def _ffn(base, nchunk, pos3, xn, wg, wu, wd, cap):
    ne = wg.shape[0]
    n = xn.shape[0]
    nb = n // TOK_BLOCK
    sub = min(8, nb)
    cap_pad = cap + TOK_BLOCK + ALIGN
    grid_spec = pltpu.PrefetchScalarGridSpec(
        num_scalar_prefetch=2,
        grid=(ne, nb // sub),
        in_specs=[pl.BlockSpec((1, 1, sub * TOK_BLOCK), lambda e, s, off, nch: (e, 0, s)),
                  pl.BlockSpec((sub * TOK_BLOCK, D_MODEL), lambda e, s, off, nch: (s, 0)),
                  pl.BlockSpec((1, D_MODEL, D_EXPERT), lambda e, s, off, nch: (e, 0, 0)),
                  pl.BlockSpec((1, D_MODEL, D_EXPERT), lambda e, s, off, nch: (e, 0, 0)),
                  pl.BlockSpec((1, D_EXPERT, D_MODEL), lambda e, s, off, nch: (e, 0, 0))],
        out_specs=pl.BlockSpec((1, cap_pad, D_MODEL), lambda e, s, off, nch: (e, 0, 0)),
        scratch_shapes=[pltpu.VMEM((cap_pad, D_MODEL), BF16)])
    return pl.pallas_call(
        functools.partial(_ffn_body, nb=nb, sub=sub, cap=cap),
        grid_spec=grid_spec,
        out_shape=jax.ShapeDtypeStruct((ne, cap_pad, D_MODEL), BF16),
        compiler_params=_params(("parallel", "arbitrary"), 48),
        name="ec_ffn",
    )(base, nchunk, pos3, xn, wg, wu, wd)


def _combine_body(off_ref, spill_ref, post_ref, affn_ref, xm_ref, *refs, nb, ne):
    win_refs, tail_refs, o_ref = refs[:ne], refs[ne:2 * ne], refs[2 * ne]
    b = pl.program_id(0)
    post = post_ref[...]
    affn = affn_ref[...]

    def expand(y_refs, shift):
        width = y_refs[0].shape[0]
        lane = lax.broadcasted_iota(I32, (TOK_BLOCK, width), 1)
        total = jnp.zeros((TOK_BLOCK, D_MODEL), F32)
        for e in range(ne):
            rel = post[:, e:e + 1] - (off_ref[e * nb + b] * ALIGN + shift)
            hit = (rel == lane).astype(BF16)
            total = total + affn[:, e:e + 1] * jnp.dot(hit, y_refs[e][...], preferred_element_type=F32)
        return total

    o_ref[...] = xm_ref[...] + expand(win_refs, 0)

    @pl.when(spill_ref[b] != 0)
    def _():
        o_ref[...] += expand(tail_refs, TOK_BLOCK)


def _combine(base, spill, pos_t, aff_n, xm, y):
    ne = y.shape[0]
    n = xm.shape[0]
    nb = n // TOK_BLOCK

    def window(e, rows, shift):
        return pl.BlockSpec((pl.Squeezed(), pl.Element(rows), pl.Element(D_MODEL)),
                            lambda b, off, sp: (e, (off[e * nb + b] + shift // ALIGN) * ALIGN, 0))

    grid_spec = pltpu.PrefetchScalarGridSpec(
        num_scalar_prefetch=2,
        grid=(nb,),
        in_specs=([pl.BlockSpec((TOK_BLOCK, ne), lambda b, off, sp: (b, 0)),
                   pl.BlockSpec((TOK_BLOCK, ne), lambda b, off, sp: (b, 0)),
                   pl.BlockSpec((TOK_BLOCK, D_MODEL), lambda b, off, sp: (b, 0))]
                  + [window(e, TOK_BLOCK, 0) for e in range(ne)]
                  + [window(e, ALIGN, TOK_BLOCK) for e in range(ne)]),
        out_specs=pl.BlockSpec((TOK_BLOCK, D_MODEL), lambda b, off, sp: (b, 0)))
    return pl.pallas_call(
        functools.partial(_combine_body, nb=nb, ne=ne),
        grid_spec=grid_spec,
        out_shape=jax.ShapeDtypeStruct((n, D_MODEL), F32),
        compiler_params=_params(("parallel",), 48),
        name="ec_combine",
    )(base, spill, pos_t, aff_n, xm, *([y] * (2 * ne)))


def _ec_moe(xm, xn, aff_t, aff_n, wg, wu, wd):
    ne, n = aff_t.shape
    cap = max(1, EC_CAPACITY * n // N_EXPERTS)
    nb = n // TOK_BLOCK
    pos, off = _select(aff_t, cap)
    first = off[:, ::TOK_BLOCK // LANES]
    count = jnp.concatenate([first[:, 1:], jnp.full((ne, 1), cap, I32)], axis=1) - first
    base = first // ALIGN
    span = jnp.where(count > 0, first - base * ALIGN + count, 0)
    spill = jnp.any(span > TOK_BLOCK, axis=0).astype(I32)
    nchunk = ((span + GATHER_ROWS - 1) // GATHER_ROWS).reshape(ne * nb).astype(I32)
    base = base.reshape(ne * nb).astype(I32)
    y = _ffn(base, nchunk, pos.reshape(ne, 1, n), xn, wg, wu, wd, cap)
    return _combine(base, spill, pos.T, aff_n, xm, y)


def _slot_cols(w, head_width):
    k = w.shape[0]
    w3 = w.reshape(k, N_HEADS, head_width)
    return jnp.pad(w3, ((0, 0), (0, 0), (0, SLOT - head_width))).reshape(k, N_HEADS * SLOT)


def _attn_weights(q_a_norm, w_uq, kv_a_norm, w_ukv, q_norm, k_norm):
    wkv = w_ukv.reshape(KV_LORA, N_HEADS, QK_NOPE + V_HEAD)
    wk = _slot_cols(wkv[:, :, :QK_NOPE].reshape(KV_LORA, -1), QK_NOPE)
    wv = wkv[:, :, QK_NOPE:].reshape(KV_LORA, D_ATTN)
    pad_gain = lambda g: jnp.pad(g, (0, SLOT - QK_HEAD))[None, :]
    return dict(wq=_slot_cols(w_uq, QK_HEAD).astype(BF16), wk=wk.astype(BF16), wv=wv.astype(BF16),
                qa=q_a_norm[None, :], kva=kv_a_norm[None, :], qg=pad_gain(q_norm), kg=pad_gain(k_norm))


def _rope_tables(seq):
    pos = jnp.arange(seq, dtype=F32)
    inv_freq = ROPE_THETA ** (-jnp.arange(0, QK_ROPE, 2, dtype=F32) / QK_ROPE)
    ang = pos[:, None] * inv_freq
    ang = jnp.concatenate([ang, ang], axis=-1)
    pad = lambda t, fill: jnp.concatenate(
        [jnp.full((seq, QK_NOPE), fill, F32), t, jnp.full((seq, SLOT - QK_HEAD), fill, F32)], axis=1)
    return pad(jnp.cos(ang), 1.0), pad(jnp.sin(ang), 0.0)


def _filter_weights(w1, b1, w2, b2, w3, freq, decay):
    hid = FILTER_HIDDEN
    padc = lambda a: jnp.pad(a, ((0, 0), (0, LANES - a.shape[1])))
    pad2 = lambda a: jnp.pad(a, ((0, LANES - a.shape[0]), (0, LANES - a.shape[1])))
    bands = jnp.linspace(1e-4, FILTER_BANDS - 1, FILTER_BANDS, dtype=F32)
    cols = HYENA_ORDER * D_HYENA
    w3d = jnp.transpose(w3.reshape(hid, N_DIR, cols), (1, 0, 2))
    return dict(bands=padc(bands[None, :]), w1t=padc(w1[0:1]), w1c=pad2(w1[1:1 + FILTER_BANDS]),
                w1s=pad2(w1[1 + FILTER_BANDS:]), b1=padc(b1[None, :]), w2=pad2(w2), b2=padc(b2[None, :]),
                freq=padc(freq[None, :]), w3=jnp.pad(w3d, ((0, 0), (0, LANES - hid), (0, 0))),
                decay=decay.reshape(N_DIR, 1, cols))


def _trunk(x, p):
    bsz, seq, _ = x.shape
    n = bsz * seq
    cos_t, sin_t = _rope_tables(seq)
    x2 = x.reshape(n, D_MODEL)
    for l in range(DEPTH):
        uh, cq, ckvr = _inproj(x2, p["attn_norm"][l][None, :], p["w_in"][l])
        y_h = _hyena(uh.reshape(bsz, seq, COL_HYENA), p["conv_w"][l], p["conv_b"][l][None, :],
                     p["filt"][l], p["hyena_bias"][l])
        qt, k, vt = _qkv(cq.reshape(bsz, seq, -1), ckvr.reshape(bsz, seq, -1), cos_t, sin_t, p["attn"][l])
        y_a = _flash(qt, k, vt)
        xm, xn, aff_t, aff_n = _outproj(y_h.reshape(n, D_HYENA), y_a.reshape(n, D_ATTN), x2,
                                 p["out_norm"][l][None, :], p["w_out"][l], p["mlp_norm"][l][None, :],
                                 p["w_router"][l])
        x2 = _ec_moe(xm, xn, aff_t, aff_n, p["w_gate"][l], p["w_up"][l], p["w_down"][l])
    return x2.reshape(bsz, seq, D_MODEL)


def kernel(x_prompt, x_sample, attn_norm, w_in, conv_w, conv_b, filt_w1, filt_b1, filt_w2, filt_b2, filt_w3,
           filt_freq, filt_decay, hyena_bias, q_a_norm, w_uq, kv_a_norm, w_ukv, q_norm, k_norm, out_norm,
           w_out, mlp_norm, w_router, w_gate, w_up, w_down):
    p = dict(
        attn_norm=attn_norm,
        w_in=jnp.pad(w_in, ((0, 0), (0, 0), (0, D_IN_PAD - D_IN))).astype(BF16),
        conv_w=conv_w, conv_b=conv_b, hyena_bias=hyena_bias,
        filt=[_filter_weights(filt_w1[l], filt_b1[l], filt_w2[l], filt_b2[l], filt_w3[l], filt_freq[l],
                              filt_decay[l]) for l in range(DEPTH)],
        attn=[_attn_weights(q_a_norm[l], w_uq[l], kv_a_norm[l], w_ukv[l], q_norm[l], k_norm[l])
              for l in range(DEPTH)],
        out_norm=out_norm, w_out=w_out.astype(BF16), mlp_norm=mlp_norm,
        w_router=jnp.pad(w_router, ((0, 0), (0, 0), (0, LANES - N_EXPERTS))),
        w_gate=w_gate.astype(BF16), w_up=w_up.astype(BF16), w_down=w_down.astype(BF16))
    return (_trunk(x_prompt, p), _trunk(x_sample, p))
```

```python
import functools
import math

import jax
import jax.numpy as jnp
from jax import lax
from jax.experimental import pallas as pl
from jax.experimental.pallas import tpu as pltpu

F32 = jnp.float32
BF16 = jnp.bfloat16
I32 = jnp.int32
HIGHEST = lax.Precision.HIGHEST

D_MODEL = 1024
DEPTH = 2
D_HYENA = 512
HYENA_ORDER = 2
FILTER_BANDS = 16
FILTER_HIDDEN = 64
N_DIR = 2
DECAY_SHIFT = 0.05
N_HEADS = 8
QK_NOPE = 64
QK_ROPE = 32
QK_HEAD = QK_NOPE + QK_ROPE
V_HEAD = 64
V_AUG = V_HEAD + 16
D_ATTN = N_HEADS * V_HEAD
Q_LORA = 256
KV_LORA = 128
ROPE_THETA = 10000.0
N_EXPERTS = 16
EC_CAPACITY = 2
D_EXPERT = 512
EPS = 1e-6
COL_HYENA = (HYENA_ORDER + 1) * D_HYENA
COL_Q = COL_HYENA + Q_LORA
COL_KV = COL_Q + KV_LORA
D_IN = COL_KV + QK_ROPE
D_IN_PAD = 2048

LANES = 128
SLOT = 128
TOK_BLOCK = 256
SLOT_TILE = 256
ALIGN = 16
GATHER_ROWS = 64
ATT_TQ = 512
ATT_QC = 256
ATT_TK = 256
ATT_UNROLL = 8
MIB = 1024 * 1024


def _params(sem, vmem_mib):
    return pltpu.CompilerParams(dimension_semantics=sem, vmem_limit_bytes=vmem_mib * MIB)


def _rms(x):
    return x * lax.rsqrt(jnp.mean(x * x, axis=-1, keepdims=True) + EPS)


def _inproj_body(x_ref, g_ref, w_ref, uh_ref, cq_ref, ckvr_ref):
    xn = _rms(x_ref[...]) * g_ref[...]
    u = jnp.dot(xn.astype(BF16), w_ref[...], preferred_element_type=F32)
    uh_ref[...] = u[:, :COL_HYENA]
    cq_ref[...] = u[:, COL_HYENA:COL_Q]
    ckvr_ref[...] = u[:, COL_Q:]


def _inproj(x2d, g, w_pad):
    n = x2d.shape[0]
    tm = min(512, n)
    return pl.pallas_call(
        _inproj_body,
        grid=(n // tm,),
        in_specs=[pl.BlockSpec((tm, D_MODEL), lambda i: (i, 0)),
                  pl.BlockSpec((1, D_MODEL), lambda i: (0, 0)),
                  pl.BlockSpec((D_MODEL, D_IN_PAD), lambda i: (0, 0))],
        out_specs=[pl.BlockSpec((tm, COL_HYENA), lambda i: (i, 0)),
                   pl.BlockSpec((tm, Q_LORA), lambda i: (i, 0)),
                   pl.BlockSpec((tm, D_IN_PAD - COL_Q), lambda i: (i, 0))],
        out_shape=[jax.ShapeDtypeStruct((n, COL_HYENA), F32),
                   jax.ShapeDtypeStruct((n, Q_LORA), F32),
                   jax.ShapeDtypeStruct((n, D_IN_PAD - COL_Q), F32)],
        compiler_params=_params(("parallel",), 40),
        name="inproj",
    )(x2d, g, w_pad)


def _sconv_body(u_ref, prev_ref, next_ref, w_ref, b_ref, v_ref, x1_ref, x2_ref):
    i = pl.program_id(1)
    last = pl.num_programs(1) - 1
    u = u_ref[0]
    tl = u.shape[0]
    prev_row = jnp.where(i == 0, 0.0, prev_ref[0, 7:8, :])
    next_row = jnp.where(i == last, 0.0, next_ref[0, 0:1, :])
    row = lax.broadcasted_iota(I32, u.shape, 0)
    up = jnp.where(row == 0, prev_row, pltpu.roll(u, 1, axis=0))
    dn = jnp.where(row == tl - 1, next_row, pltpu.roll(u, tl - 1, axis=0))
    y = up * w_ref[0:1, :] + u * w_ref[1:2, :] + dn * w_ref[2:3, :] + b_ref[...]
    v_ref[0] = y[:, :D_HYENA]
    x1_ref[0] = y[:, D_HYENA:2 * D_HYENA]
    x2_ref[0] = y[:, 2 * D_HYENA:]


def _sconv(uh, w, b):
    bsz, seq, c = uh.shape
    tl = min(512, seq)
    r = tl // 8
    nblk8 = seq // 8
    out = jax.ShapeDtypeStruct((bsz, seq, D_HYENA), F32)
    ospec = pl.BlockSpec((1, tl, D_HYENA), lambda bi, i: (bi, i, 0))
    return pl.pallas_call(
        _sconv_body,
        grid=(bsz, seq // tl),
        in_specs=[pl.BlockSpec((1, tl, c), lambda bi, i: (bi, i, 0)),
                  pl.BlockSpec((1, 8, c), lambda bi, i: (bi, jnp.maximum(i * r - 1, 0), 0)),
                  pl.BlockSpec((1, 8, c), lambda bi, i: (bi, jnp.minimum((i + 1) * r, nblk8 - 1), 0)),
                  pl.BlockSpec((3, c), lambda bi, i: (0, 0)),
                  pl.BlockSpec((1, c), lambda bi, i: (0, 0))],
        out_specs=[ospec, ospec, ospec],
        out_shape=[out, out, out],
        compiler_params=_params(("parallel", "parallel"), 40),
        name="sconv",
    )(uh, uh, uh, w, b)


def _filter_body(bands_ref, phase_ref, w1t_ref, w1_ref, b1_ref, w2_ref, b2_ref, fr_ref, w3h_ref, w3l_ref,
                 dec_ref, k_ref, sum_ref, *, seq):
    i = pl.program_id(0)
    tr = k_ref.shape[0]
    hp = tr // 2
    n_lo = i * tr + lax.broadcasted_iota(I32, (hp, 1), 0)
    n_hi = n_lo + hp
    t_of = lambda n: jnp.where(n < seq, n, 2 * seq - n).astype(F32)
    lane = lax.broadcasted_iota(I32, (hp, LANES), 1)
    t_idx = jnp.where(lane < FILTER_HIDDEN, t_of(n_lo), t_of(n_hi))
    feats = jnp.sin(2.0 * math.pi * bands_ref[...] * t_idx / seq + phase_ref[...])
    fr = fr_ref[...]
    pre = (t_idx / (seq - 1)) * w1t_ref[...] + jnp.dot(feats, w1_ref[...], precision=HIGHEST,
                                                      preferred_element_type=F32)
    h = jnp.sin(fr * (pre + b1_ref[...]))
    h = jnp.sin(fr * (jnp.dot(h, w2_ref[...], precision=HIGHEST, preferred_element_type=F32) + b2_ref[...]))
    h_hi = h.astype(BF16)
    h_lo = (h - h_hi.astype(F32)).astype(BF16)

    @pl.when(i == 0)
    def _():
        sum_ref[...] = jnp.zeros_like(sum_ref)

    for j, n in enumerate((n_lo, n_hi)):
        out = (jnp.dot(h_hi, w3h_ref[0, j], preferred_element_type=F32)
               + jnp.dot(h_lo, w3h_ref[0, j], preferred_element_type=F32)
               + jnp.dot(h_hi, w3l_ref[0, j], preferred_element_type=F32))
        window = jnp.exp(-(t_of(n) / (seq - 1)) * dec_ref[0]) + DECAY_SHIFT
        k = jnp.where(n == seq, 0.0, out * window)
        k_ref[j * hp:(j + 1) * hp, :] = k.astype(BF16)
        sum_ref[...] += jnp.sum(jnp.abs(k), axis=0, keepdims=True)


def _filters(seq, fw):
    cols = HYENA_ORDER * D_HYENA
    tr = min(512, seq)
    half = seq // tr
    const = lambda shape: pl.BlockSpec(shape, lambda i: (0,) * len(shape))
    return pl.pallas_call(
        functools.partial(_filter_body, seq=seq),
        grid=(2 * seq // tr,),
        in_specs=[const((1, LANES)), const((1, LANES)), const((1, LANES)), const((LANES, LANES)),
                  const((1, LANES)), const((LANES, LANES)), const((1, LANES)), const((1, LANES)),
                  pl.BlockSpec((1, 2, LANES, cols), lambda i: (i // half, 0, 0, 0)),
                  pl.BlockSpec((1, 2, LANES, cols), lambda i: (i // half, 0, 0, 0)),
                  pl.BlockSpec((1, 1, cols), lambda i: (i // half, 0, 0))],
        out_specs=[pl.BlockSpec((tr, cols), lambda i: (i, 0)),
                   pl.BlockSpec((1, cols), lambda i: (0, 0))],
        out_shape=[jax.ShapeDtypeStruct((2 * seq, cols), BF16),
                   jax.ShapeDtypeStruct((1, cols), F32)],
        compiler_params=_params(("arbitrary",), 32),
        name="filter_gen",
    )(fw["bands"], fw["phase"], fw["w1t"], fw["w1"], fw["b1"], fw["w2"], fw["b2"], fw["freq"],
      fw["w3_hi"], fw["w3_lo"], fw["decay"])


def _fft_dims(seq):
    n2 = 128 if 2 * seq >= 32768 else 64
    n1 = 2 * seq // n2
    return n1, n2


def _dft_tables(n1, n2):
    n = n1 * n2
    n1h = n1 // 2
    k1 = jnp.arange(n1h, dtype=I32)[:, None]
    m1 = jnp.arange(n1, dtype=I32)[None, :]
    ang = (2.0 * math.pi / n1) * ((k1 * m1) % n1).astype(F32)
    top = jnp.cos(ang)
    bot = -jnp.sin(ang)
    nyq = jnp.where(m1 % 2 == 0, 1.0, -1.0).astype(F32)
    bot = jnp.concatenate([nyq, bot[1:]], axis=0)
    fa = jnp.concatenate([top, bot], axis=0)
    weight = jnp.where((jnp.arange(n1) % n1h) == 0, 1.0, 2.0).astype(F32) / n
    fi = (fa[:, :n1h] * weight[:, None]).T

    kk = jnp.arange(n1h + 1, dtype=I32)[:, None, None]
    k2 = jnp.arange(n2, dtype=I32)[None, :, None]
    m2 = jnp.arange(n2, dtype=I32)[None, None, :]
    phi = (2.0 * math.pi / n) * ((m2 * (kk + n1 * k2)) % n).astype(F32)
    gr = jnp.cos(phi)
    gi = -jnp.sin(phi)
    blk = jnp.concatenate([jnp.concatenate([gr, -gi], axis=2),
                           jnp.concatenate([gi, gr], axis=2)], axis=1)
    left = (jnp.arange(2 * n2) < n2)[None, :]
    g0 = jnp.where(left, blk[0], 0.0)
    gf = jnp.concatenate([g0[None], blk[1:n1h]], axis=0)
    gnyq = jnp.concatenate([jnp.zeros((2 * n2, n2), F32), blk[n1h][:, :n2]], axis=1)
    return dict(fa_full=fa.astype(BF16), fa_half=fa[:, :n1h].astype(BF16), fi=fi.astype(BF16),
                gf=gf.astype(BF16), gnyq=gnyq.astype(BF16),
                ginv=jnp.transpose(gf, (0, 2, 1)).astype(BF16), m2=gnyq.T.astype(BF16))


def _fft_a_body(x_ref, f_ref, o_ref):
    a = jnp.dot(f_ref[...], x_ref[0].astype(BF16), preferred_element_type=F32)
    n1h = o_ref.shape[2]
    o_ref[0, 0] = a[:n1h].astype(BF16)
    o_ref[0, 1] = a[n1h:].astype(BF16)


def _fft_a(x3, fmat):
    bsz, r, nc = x3.shape
    n1 = fmat.shape[0]
    tn = min(4096, nc)
    return pl.pallas_call(
        _fft_a_body,
        grid=(bsz, nc // tn),
        in_specs=[pl.BlockSpec((1, r, tn), lambda b, j: (b, 0, j)),
                  pl.BlockSpec((n1, r), lambda b, j: (0, 0))],
        out_specs=pl.BlockSpec((1, 2, n1 // 2, tn), lambda b, j: (b, 0, 0, j)),
        out_shape=jax.ShapeDtypeStruct((bsz, 2, n1 // 2, nc), BF16),
        compiler_params=_params(("parallel", "parallel"), 32),
        name="fft_stage_a",
    )(x3, fmat)


def _fft_b_body(ar_ref, ai_ref, g_ref, gn_ref, sc_ref, x_ref, xn_ref, *, kb):
    j = pl.program_id(2)
    inv = 1.0 / sc_ref[...]
    for kk in range(kb):
        rhs = jnp.concatenate([ar_ref[0, 0, kk], ai_ref[0, 0, kk]], axis=0)
        x_ref[0, kk] = (jnp.dot(g_ref[kk], rhs, preferred_element_type=F32) * inv).astype(BF16)

    @pl.when(j == 0)
    def _():
        rhs = jnp.concatenate([ar_ref[0, 0, 0], ai_ref[0, 0, 0]], axis=0)
        xn_ref[0] = (jnp.dot(gn_ref[...], rhs, preferred_element_type=F32) * inv).astype(BF16)


def _fft_b(a5, tabs, scale):
    bsz, _, n1h, n2, c = a5.shape
    kb = 8
    tc = min(512, c)
    return pl.pallas_call(
        functools.partial(_fft_b_body, kb=kb),
        grid=(bsz, c // tc, n1h // kb),
        in_specs=[pl.BlockSpec((1, 1, kb, n2, tc), lambda b, ci, j: (b, 0, j, 0, ci)),
                  pl.BlockSpec((1, 1, kb, n2, tc), lambda b, ci, j: (b, 1, j, 0, ci)),
                  pl.BlockSpec((kb, 2 * n2, 2 * n2), lambda b, ci, j: (j, 0, 0)),
                  pl.BlockSpec((2 * n2, 2 * n2), lambda b, ci, j: (0, 0)),
                  pl.BlockSpec((1, tc), lambda b, ci, j: (0, ci))],
        out_specs=[pl.BlockSpec((1, kb, 2 * n2, tc), lambda b, ci, j: (b, j, 0, ci)),
                   pl.BlockSpec((1, 2 * n2, tc), lambda b, ci, j: (b, 0, ci))],
        out_shape=[jax.ShapeDtypeStruct((bsz, n1h, 2 * n2, c), BF16),
                   jax.ShapeDtypeStruct((bsz, 2 * n2, c), BF16)],
        compiler_params=_params(("parallel", "parallel", "arbitrary"), 40),
        name="fft_stage_b",
    )(a5, a5, tabs["gf"], tabs["gnyq"], scale)


def _cmul(x, k, n2):
    xr, xi = x[:n2], x[n2:]
    kr, ki = k[:n2], k[n2:]
    return jnp.concatenate([xr * kr - xi * ki, xr * ki + xi * kr], axis=0).astype(BF16)


def _spec_body(ar_ref, ai_ref, k_ref, kn_ref, g_ref, gn_ref, gi_ref, m2_ref, o_ref, *, kb):
    j = pl.program_id(1)
    n2 = o_ref.shape[3]

    def through(fwd, a_re, a_im, kf, inv):
        x = jnp.dot(fwd, jnp.concatenate([a_re, a_im], axis=0), preferred_element_type=F32)
        return jnp.dot(inv, _cmul(x, kf.astype(F32), n2), preferred_element_type=F32)

    for kk in range(kb):
        acc = through(g_ref[kk], ar_ref[0, 0, kk], ai_ref[0, 0, kk], k_ref[0, kk], gi_ref[kk])
        o_ref[0, 0, kk] = acc[:n2].astype(BF16)
        o_ref[0, 1, kk] = acc[n2:].astype(BF16)

    @pl.when(j == 0)
    def _():
        acc = through(gn_ref[...], ar_ref[0, 0, 0], ai_ref[0, 0, 0], kn_ref[0], m2_ref[...])
        o_ref[0, 1, 0] = acc[n2:].astype(BF16)


def _spec_conv(a5, kf, kfnyq, order, tabs):
    bsz, _, n1h, n2, c = a5.shape
    tn2 = 2 * n2
    kb = 8
    return pl.pallas_call(
        functools.partial(_spec_body, kb=kb),
        grid=(bsz, n1h // kb),
        in_specs=[pl.BlockSpec((1, 1, kb, n2, c), lambda b, j: (b, 0, j, 0, 0)),
                  pl.BlockSpec((1, 1, kb, n2, c), lambda b, j: (b, 1, j, 0, 0)),
                  pl.BlockSpec((1, kb, tn2, c), lambda b, j: (0, j, 0, order)),
                  pl.BlockSpec((1, tn2, c), lambda b, j: (0, 0, order)),
                  pl.BlockSpec((kb, tn2, tn2), lambda b, j: (j, 0, 0)),
                  pl.BlockSpec((tn2, tn2), lambda b, j: (0, 0)),
                  pl.BlockSpec((kb, tn2, tn2), lambda b, j: (j, 0, 0)),
                  pl.BlockSpec((tn2, tn2), lambda b, j: (0, 0))],
        out_specs=pl.BlockSpec((1, 2, kb, n2, c), lambda b, j: (b, 0, j, 0, 0)),
        out_shape=jax.ShapeDtypeStruct((bsz, 2, n1h, n2, c), BF16),
        compiler_params=_params(("parallel", "arbitrary"), 48),
        name="spectral_conv",
    )(a5, a5, kf, kfnyq, tabs["gf"], tabs["gnyq"], tabs["ginv"], tabs["m2"])


def _ifft_a_body(b_ref, f_ref, z_ref, gate_ref, bias_ref, o_ref):
    y = jnp.dot(f_ref[...], b_ref[0], preferred_element_type=F32)
    o_ref[0] = gate_ref[0] * (y + z_ref[0] * bias_ref[...])


def _ifft_a(b3, fi, z3, gate3, bias_t):
    bsz, n1, nc = b3.shape
    n1h = n1 // 2
    tn = bias_t.shape[1]
    return pl.pallas_call(
        _ifft_a_body,
        grid=(bsz, nc // tn),
        in_specs=[pl.BlockSpec((1, n1, tn), lambda b, j: (b, 0, j)),
                  pl.BlockSpec((n1h, n1), lambda b, j: (0, 0)),
                  pl.BlockSpec((1, n1h, tn), lambda b, j: (b, 0, j)),
                  pl.BlockSpec((1, n1h, tn), lambda b, j: (b, 0, j)),
                  pl.BlockSpec((1, tn), lambda b, j: (0, 0))],
        out_specs=pl.BlockSpec((1, n1h, tn), lambda b, j: (b, 0, j)),
        out_shape=jax.ShapeDtypeStruct((bsz, n1h, nc), F32),
        compiler_params=_params(("parallel", "parallel"), 32),
        name="ifft_stage_a",
    )(b3, fi, z3, gate3, bias_t)


def _hyena(uh, conv_w, conv_b, fw, bias):
    bsz, seq, _ = uh.shape
    c = D_HYENA
    n1, n2 = _fft_dims(seq)
    n1h = n1 // 2
    tabs = _dft_tables(n1, n2)
    v, x1, x2 = _sconv(uh, conv_w, conv_b)

    kcirc, ksum = _filters(seq, fw)
    cols = HYENA_ORDER * c
    ka = _fft_a(kcirc.reshape(1, n1, n2 * cols), tabs["fa_full"])
    kf, kfnyq = _fft_b(ka.reshape(1, 2, n1h, n2, cols), tabs, ksum)

    tn = min(4096, n2 * c)
    z = v
    for order, gate in enumerate((x1, x2)):
        za = _fft_a(z.reshape(bsz, n1h, n2 * c), tabs["fa_half"])
        zb = _spec_conv(za.reshape(bsz, 2, n1h, n2, c), kf, kfnyq, order, tabs)
        bias_t = jnp.tile(bias[order][None, :], (1, tn // c))
        z = _ifft_a(zb.reshape(bsz, n1, n2 * c), tabs["fi"], z.reshape(bsz, n1h, n2 * c),
                    gate.reshape(bsz, n1h, n2 * c), bias_t).reshape(bsz, seq, c)
    return z


def _qkv_body(cq_ref, ckvr_ref, cos_ref, sin_ref, wq_ref, wk_ref, wv_ref, qa_ref, kva_ref, qg_ref, kg_ref,
              qt_ref, k_ref, vt_ref):
    cqn = _rms(cq_ref[0]) * qa_ref[...]
    qs = jnp.dot(cqn.astype(BF16), wq_ref[...], preferred_element_type=F32)
    ck = ckvr_ref[0]
    ckvn = (_rms(ck[:, :KV_LORA]) * kva_ref[...]).astype(BF16)
    ks = jnp.dot(ckvn, wk_ref[...], preferred_element_type=F32)
    vs = jnp.dot(ckvn, wv_ref[...], preferred_element_type=F32)
    krope = pltpu.roll(ck[:, KV_LORA:], QK_NOPE, axis=1)
    cos = cos_ref[...]
    sin = sin_ref[...]
    lane = lax.broadcasted_iota(I32, cos.shape, 1)
    half = QK_ROPE // 2
    first = (lane >= QK_NOPE) & (lane < QK_NOPE + half)
    second = (lane >= QK_NOPE + half) & (lane < QK_HEAD)

    def head(xh, gain, scale):
        ms = jnp.sum(xh * xh, axis=-1, keepdims=True) * (1.0 / QK_HEAD)
        xn = xh * lax.rsqrt(ms + EPS) * gain
        rot = jnp.where(first, -pltpu.roll(xn, SLOT - half, axis=1),
                        jnp.where(second, pltpu.roll(xn, half, axis=1), 0.0))
        return (xn * cos + rot * sin) * scale

    q_scale = QK_HEAD ** -0.5 * math.log2(math.e)
    for h in range(N_HEADS):
        sl = slice(h * SLOT, (h + 1) * SLOT)
        qt_ref[0, h] = head(qs[:, sl], qg_ref[...], q_scale).T.astype(BF16)
        k_ref[0, h] = head(ks[:, sl] + krope, kg_ref[...], 1.0).astype(BF16)
    aug = lax.broadcasted_iota(I32, (V_AUG - V_HEAD, cos.shape[0]), 0)
    ones_row = jnp.where(aug == 0, 1.0, 0.0)
    for hp in range(N_HEADS // 2):
        pair_t = vs[:, hp * 2 * V_HEAD:(hp + 1) * 2 * V_HEAD].T
        vt_ref[0, hp] = jnp.concatenate(
            [pair_t[:V_HEAD], ones_row, pair_t[V_HEAD:], ones_row], axis=0).astype(BF16)


def _qkv(cq, ckvr, cos_t, sin_t, aw):
    bsz, seq, _ = cq.shape
    tm = min(512, seq)
    const = lambda shape: pl.BlockSpec(shape, lambda b, i: (0,) * len(shape))
    return pl.pallas_call(
        _qkv_body,
        grid=(bsz, seq // tm),
        in_specs=[pl.BlockSpec((1, tm, Q_LORA), lambda b, i: (b, i, 0)),
                  pl.BlockSpec((1, tm, D_IN_PAD - COL_Q), lambda b, i: (b, i, 0)),
                  pl.BlockSpec((tm, SLOT), lambda b, i: (i, 0)),
                  pl.BlockSpec((tm, SLOT), lambda b, i: (i, 0)),
                  const((Q_LORA, N_HEADS * SLOT)), const((KV_LORA, N_HEADS * SLOT)),
                  const((KV_LORA, D_ATTN)),
                  const((1, Q_LORA)), const((1, KV_LORA)), const((1, SLOT)), const((1, SLOT))],
        out_specs=[pl.BlockSpec((1, N_HEADS, SLOT, tm), lambda b, i: (b, 0, 0, i)),
                   pl.BlockSpec((1, N_HEADS, tm, SLOT), lambda b, i: (b, 0, i, 0)),
                   pl.BlockSpec((1, N_HEADS // 2, 2 * V_AUG, tm), lambda b, i: (b, 0, 0, i))],
        out_shape=[jax.ShapeDtypeStruct((bsz, N_HEADS, SLOT, seq), BF16),
                   jax.ShapeDtypeStruct((bsz, N_HEADS, seq, SLOT), BF16),
                   jax.ShapeDtypeStruct((bsz, N_HEADS // 2, 2 * V_AUG, seq), BF16)],
        compiler_params=_params(("parallel", "parallel"), 40),
        name="qkv_prep",
    )(cq, ckvr, cos_t, sin_t, aw["wq"], aw["wk"], aw["wv"], aw["qa"], aw["kva"], aw["qg"], aw["kg"])


def _flash_body(qt_ref, k_ref, vt_ref, o_ref, s0_ref, s1_ref, acc_ref, *, tk, nk, nqc, unroll):
    chains = [(hh, qc) for hh in range(2) for qc in range(nqc)]
    acc_ref[...] = jnp.zeros(acc_ref.shape, F32)

    def qk(t, s_ref):
        ks = pl.multiple_of(t * tk, tk)
        tile_max = []
        for c, (hh, qc) in enumerate(chains):
            k = k_ref[0, hh, pl.ds(ks, tk), :]
            s = jnp.dot(k, qt_ref[0, hh, :, qc * ATT_QC:(qc + 1) * ATT_QC], preferred_element_type=F32)
            s_ref[c] = s
            tile_max.append(jnp.max(s, axis=0, keepdims=True))
        return tile_max

    def softmax_pv(t, s_ref, tile_max, m):
        ks = pl.multiple_of(t * tk, tk)
        m_out = []
        for c, (hh, qc) in enumerate(chains):
            m_new = jnp.maximum(m[c], tile_max[c])
            a = jnp.exp2(m[c] - m_new)
            p = jnp.exp2(s_ref[c] - m_new)
            m_out.append(m_new)
            vt = vt_ref[0, 0, hh * V_AUG:(hh + 1) * V_AUG, pl.ds(ks, tk)]
            acc_ref[c] = acc_ref[c] * a + jnp.dot(vt, p.astype(BF16), preferred_element_type=F32)
        return m_out

    def group(u, carry):
        tile_max, m = carry
        for i in range(0, unroll, 2):
            t = unroll * u + i
            tile_max1 = qk(t + 1, s1_ref)
            m = softmax_pv(t, s0_ref, tile_max, m)
            tile_max = qk(jnp.minimum(t + 2, nk - 1), s0_ref)
            m = softmax_pv(t + 1, s1_ref, tile_max1, m)
        return tile_max, m

    init = (qk(0, s0_ref), [jnp.full((1, ATT_QC), -jnp.inf, F32)] * len(chains))
    lax.fori_loop(0, nk // unroll, group, init)

    def normalised(c):
        acc = acc_ref[c]
        return acc[:V_HEAD] * (1.0 / acc[V_HEAD:V_HEAD + 1])

    heads = [jnp.concatenate([normalised(hh * nqc + qc) for qc in range(nqc)], axis=1) for hh in range(2)]
    o_ref[0] = jnp.concatenate(heads, axis=0).T


def _flash(qt, k, vt):
    bsz, nh, _, seq = qt.shape
    tq = min(ATT_TQ, seq)
    tk = min(ATT_TK, seq)
    nqc = tq // ATT_QC
    nk = seq // tk
    unroll = min(ATT_UNROLL, nk)
    assert nk % unroll == 0 and unroll % 2 == 0
    return pl.pallas_call(
        functools.partial(_flash_body, tk=tk, nk=nk, nqc=nqc, unroll=unroll),
        grid=(bsz, nh // 2, seq // tq),
        in_specs=[pl.BlockSpec((1, 2, SLOT, tq), lambda b, hp, i: (b, hp, 0, i)),
                  pl.BlockSpec((1, 2, seq, SLOT), lambda b, hp, i: (b, hp, 0, 0)),
                  pl.BlockSpec((1, 1, 2 * V_AUG, seq), lambda b, hp, i: (b, hp, 0, 0))],
        out_specs=pl.BlockSpec((1, tq, 2 * V_HEAD), lambda b, hp, i: (b, i, hp)),
        out_shape=jax.ShapeDtypeStruct((bsz, seq, nh * V_HEAD), F32),
        scratch_shapes=[pltpu.VMEM((2 * nqc, tk, ATT_QC), F32), pltpu.VMEM((2 * nqc, tk, ATT_QC), F32),
                        pltpu.VMEM((2 * nqc, V_AUG, ATT_QC), F32)],
        compiler_params=_params(("parallel", "parallel", "parallel"), 48),
        name="flash_attn",
    )(qt, k, vt)


def _outproj_body(yh_ref, ya_ref, x_ref, og_ref, w_ref, mg_ref, wr_ref, xm_ref, xn_ref, aff_ref, affn_ref):
    og = og_ref[...]
    half = D_MODEL // 2
    y = jnp.concatenate([_rms(yh_ref[...]) * og[:, :half], _rms(ya_ref[...]) * og[:, half:]], axis=1)
    xm = x_ref[...] + jnp.dot(y.astype(BF16), w_ref[...], preferred_element_type=F32)
    xm_ref[...] = xm
    xn = _rms(xm) * mg_ref[...]
    xn_hi = xn.astype(BF16)
    xn_ref[...] = xn_hi
    xn_lo = (xn - xn_hi.astype(F32)).astype(BF16)
    logits = (jnp.dot(xn_hi, wr_ref[0], preferred_element_type=F32)
              + jnp.dot(xn_lo, wr_ref[0], preferred_element_type=F32)
              + jnp.dot(xn_hi, wr_ref[1], preferred_element_type=F32))
    lane = lax.broadcasted_iota(I32, logits.shape, 1)
    logits = jnp.where(lane < N_EXPERTS, logits, -jnp.inf)
    e = jnp.exp(logits - jnp.max(logits, axis=-1, keepdims=True))
    aff = e / jnp.sum(e, axis=-1, keepdims=True)
    aff_ref[...] = aff.T[:N_EXPERTS]
    affn_ref[...] = aff[:, :N_EXPERTS]


def _outproj(yh, ya, x2d, og, w_out, mg, wr_pad):
    n = x2d.shape[0]
    tm = min(512, n)
    half = D_MODEL // 2
    const = lambda shape: pl.BlockSpec(shape, lambda i: (0,) * len(shape))
    return pl.pallas_call(
        _outproj_body,
        grid=(n // tm,),
        in_specs=[pl.BlockSpec((tm, half), lambda i: (i, 0)),
                  pl.BlockSpec((tm, half), lambda i: (i, 0)),
                  pl.BlockSpec((tm, D_MODEL), lambda i: (i, 0)),
                  const((1, D_MODEL)), const((D_MODEL, D_MODEL)), const((1, D_MODEL)),
                  const((2, D_MODEL, LANES))],
        out_specs=[pl.BlockSpec((tm, D_MODEL), lambda i: (i, 0)),
                   pl.BlockSpec((tm, D_MODEL), lambda i: (i, 0)),
                   pl.BlockSpec((N_EXPERTS, tm), lambda i: (0, i)),
                   pl.BlockSpec((tm, N_EXPERTS), lambda i: (i, 0))],
        out_shape=[jax.ShapeDtypeStruct((n, D_MODEL), F32),
                   jax.ShapeDtypeStruct((n, D_MODEL), BF16),
                   jax.ShapeDtypeStruct((N_EXPERTS, n), F32),
                   jax.ShapeDtypeStruct((n, N_EXPERTS), F32)],
        compiler_params=_params(("parallel",), 40),
        name="outproj_router",
    )(yh, ya, x2d, og, w_out, mg, wr_pad)


def _select_body(aff_ref, upper_ref, lower_ref, pos_ref, off_ref, *, cap):
    bits = pltpu.bitcast(aff_ref[0], I32)
    upper = upper_ref[...]
    lower = lower_ref[...]

    def count(mask):
        return jnp.sum(jnp.sum(mask.astype(F32), axis=1, keepdims=True), axis=0, keepdims=True)

    def bit_step(i, thr):
        cand = thr | jnp.left_shift(jnp.int32(1), 30 - i)
        return jnp.where(count(bits >= cand) >= cap, cand, thr)

    thr = lax.fori_loop(0, 31, bit_step, jnp.zeros((1, 1), I32))
    gt = bits > thr
    eq = bits == thr
    need = cap - count(gt)

    def prefix(mask):
        within = jnp.dot(mask.astype(BF16), upper, preferred_element_type=F32)
        total = within[:, LANES - 1:LANES]
        offs = jnp.dot(lower, jnp.broadcast_to(total, within.shape).astype(BF16), preferred_element_type=F32)
        return within, offs

    w_eq, o_eq = prefix(eq)
    sel = gt | (eq & (o_eq + w_eq <= need))
    w_sel, o_sel = prefix(sel)
    pos_ref[0] = jnp.where(sel, o_sel + w_sel - 1.0, -1.0).astype(I32)
    off_ref[0] = o_sel.astype(I32)


def _select(aff_t, cap):
    ne, n = aff_t.shape
    rows = n // LANES
    upper = (jnp.arange(LANES)[:, None] <= jnp.arange(LANES)[None, :]).astype(BF16)
    lower = (jnp.arange(rows)[None, :] < jnp.arange(rows)[:, None]).astype(BF16)
    blk = pl.BlockSpec((1, rows, LANES), lambda e: (e, 0, 0))
    pos, off = pl.pallas_call(
        functools.partial(_select_body, cap=cap),
        grid=(ne,),
        in_specs=[blk, pl.BlockSpec((LANES, LANES), lambda e: (0, 0)),
                  pl.BlockSpec((rows, rows), lambda e: (0, 0))],
        out_specs=[blk, blk],
        out_shape=[jax.ShapeDtypeStruct((ne, rows, LANES), I32),
                   jax.ShapeDtypeStruct((ne, rows, LANES), I32)],
        compiler_params=_params(("parallel",), 32),
        name="ec_select",
    )(aff_t.reshape(ne, rows, LANES), upper, lower)
    return pos.reshape(ne, n), off[:, :, 0]


def _ffn_body(off_ref, nch_ref, pos_ref, x_ref, wg_ref, wu_ref, wd_ref, y_ref, xs_ref, *, nb, sub, cap):
    e = pl.program_id(0)
    sb = pl.program_id(1)

    @pl.when(sb == 0)
    def _():
        xs_ref[...] = jnp.zeros(xs_ref.shape, BF16)

    row = lax.broadcasted_iota(I32, (GATHER_ROWS, TOK_BLOCK), 0)

    def gather(i, c):
        blk = e * nb + sb * sub + i
        base = off_ref[blk] * ALIGN
        tok = slice(i * TOK_BLOCK, (i + 1) * TOK_BLOCK)
        start = pl.multiple_of(base + c * GATHER_ROWS, ALIGN)
        hit = (pos_ref[0, :, tok] - start) == row
        win = jnp.dot(hit.astype(BF16), x_ref[tok, :], preferred_element_type=F32)
        xs_ref[pl.ds(start, GATHER_ROWS), :] += win.astype(BF16)

    for i in range(sub):
        gather(i, 0)
    for i in range(sub):
        lax.fori_loop(1, nch_ref[e * nb + sb * sub + i], lambda c, carry, i=i: (gather(i, c), carry)[1], 0)

    @pl.when(sb == pl.num_programs(1) - 1)
    def _():
        for j in range(cap // SLOT_TILE):
            rows = slice(j * SLOT_TILE, (j + 1) * SLOT_TILE)
            xt = xs_ref[rows, :]
            g = jnp.dot(xt, wg_ref[0], preferred_element_type=F32)
            u = jnp.dot(xt, wu_ref[0], preferred_element_type=F32)
            h = (g * jax.nn.sigmoid(g) * u).astype(BF16)
            y_ref[0, rows, :] = jnp.dot(h, wd_ref[0], preferred_element_type=F32).astype(BF16)
        y_ref[0, cap:, :] = jnp.zeros((y_ref.shape[1] - cap, D_MODEL), BF16)


def _ffn(base, nchunk, pos3, xn, wg, wu, wd, cap):
    ne = wg.shape[0]
    n = xn.shape[0]
    nb = n // TOK_BLOCK
    sub = min(8, nb)
    cap_pad = cap + TOK_BLOCK + ALIGN
    grid_spec = pltpu.PrefetchScalarGridSpec(
        num_scalar_prefetch=2,
        grid=(ne, nb // sub),
        in_specs=[pl.BlockSpec((1, 1, sub * TOK_BLOCK), lambda e, s, off, nch: (e, 0, s)),
                  pl.BlockSpec((sub * TOK_BLOCK, D_MODEL), lambda e, s, off, nch: (s, 0)),
                  pl.BlockSpec((1, D_MODEL, D_EXPERT), lambda e, s, off, nch: (e, 0, 0)),
                  pl.BlockSpec((1, D_MODEL, D_EXPERT), lambda e, s, off, nch: (e, 0, 0)),
                  pl.BlockSpec((1, D_EXPERT, D_MODEL), lambda e, s, off, nch: (e, 0, 0))],
        out_specs=pl.BlockSpec((1, cap_pad, D_MODEL), lambda e, s, off, nch: (e, 0, 0)),
        scratch_shapes=[pltpu.VMEM((cap_pad, D_MODEL), BF16)])
    return pl.pallas_call(
        functools.partial(_ffn_body, nb=nb, sub=sub, cap=cap),
        grid_spec=grid_spec,
        out_shape=jax.ShapeDtypeStruct((ne, cap_pad, D_MODEL), BF16),
        compiler_params=_params(("parallel", "arbitrary"), 48),
        name="ec_ffn",
    )(base, nchunk, pos3, xn, wg, wu, wd)


def _combine_body(off_ref, spill_ref, post_ref, affn_ref, xm_ref, *refs, nb, ne):
    win_refs, tail_refs, o_ref = refs[:ne], refs[ne:2 * ne], refs[2 * ne]
    b = pl.program_id(0)
    post = post_ref[...]
    affn = affn_ref[...]

    def expand(y_refs, shift):
        width = y_refs[0].shape[0]
        lane = lax.broadcasted_iota(I32, (TOK_BLOCK, width), 1)
        total = jnp.zeros((TOK_BLOCK, D_MODEL), F32)
        for e in range(ne):
            rel = post[:, e:e + 1] - (off_ref[e * nb + b] * ALIGN + shift)
            hit = (rel == lane).astype(BF16)
            total = total + affn[:, e:e + 1] * jnp.dot(hit, y_refs[e][...], preferred_element_type=F32)
        return total

    o_ref[...] = xm_ref[...] + expand(win_refs, 0)

    @pl.when(spill_ref[b] != 0)
    def _():
        o_ref[...] += expand(tail_refs, TOK_BLOCK)


def _combine(base, spill, pos_t, aff_n, xm, y):
    ne = y.shape[0]
    n = xm.shape[0]
    nb = n // TOK_BLOCK

    def window(e, rows, shift):
        return pl.BlockSpec((pl.Squeezed(), pl.Element(rows), pl.Element(D_MODEL)),
                            lambda b, off, sp: (e, (off[e * nb + b] + shift // ALIGN) * ALIGN, 0))

    grid_spec = pltpu.PrefetchScalarGridSpec(
        num_scalar_prefetch=2,
        grid=(nb,),
        in_specs=([pl.BlockSpec((TOK_BLOCK, ne), lambda b, off, sp: (b, 0)),
                   pl.BlockSpec((TOK_BLOCK, ne), lambda b, off, sp: (b, 0)),
                   pl.BlockSpec((TOK_BLOCK, D_MODEL), lambda b, off, sp: (b, 0))]
                  + [window(e, TOK_BLOCK, 0) for e in range(ne)]
                  + [window(e, ALIGN, TOK_BLOCK) for e in range(ne)]),
        out_specs=pl.BlockSpec((TOK_BLOCK, D_MODEL), lambda b, off, sp: (b, 0)))
    return pl.pallas_call(
        functools.partial(_combine_body, nb=nb, ne=ne),
        grid_spec=grid_spec,
        out_shape=jax.ShapeDtypeStruct((n, D_MODEL), F32),
        compiler_params=_params(("parallel",), 48),
        name="ec_combine",
    )(base, spill, pos_t, aff_n, xm, *([y] * (2 * ne)))


def _ec_moe(xm, xn, aff_t, aff_n, wg, wu, wd):
    ne, n = aff_t.shape
    cap = max(1, EC_CAPACITY * n // N_EXPERTS)
    nb = n // TOK_BLOCK
    pos, off = _select(aff_t, cap)
    first = off[:, ::TOK_BLOCK // LANES]
    count = jnp.concatenate([first[:, 1:], jnp.full((ne, 1), cap, I32)], axis=1) - first
    base = first // ALIGN
    span = jnp.where(count > 0, first - base * ALIGN + count, 0)
    spill = jnp.any(span > TOK_BLOCK, axis=0).astype(I32)
    nchunk = ((span + GATHER_ROWS - 1) // GATHER_ROWS).reshape(ne * nb).astype(I32)
    base = base.reshape(ne * nb).astype(I32)
    y = _ffn(base, nchunk, pos.reshape(ne, 1, n), xn, wg, wu, wd, cap)
    return _combine(base, spill, pos.T, aff_n, xm, y)


def _slot_cols(w, head_width):
    k = w.shape[0]
    w3 = w.reshape(k, N_HEADS, head_width)
    return jnp.pad(w3, ((0, 0), (0, 0), (0, SLOT - head_width))).reshape(k, N_HEADS * SLOT)


def _attn_weights(q_a_norm, w_uq, kv_a_norm, w_ukv, q_norm, k_norm):
    wkv = w_ukv.reshape(KV_LORA, N_HEADS, QK_NOPE + V_HEAD)
    wk = _slot_cols(wkv[:, :, :QK_NOPE].reshape(KV_LORA, -1), QK_NOPE)
    wv = wkv[:, :, QK_NOPE:].reshape(KV_LORA, D_ATTN)
    pad_gain = lambda g: jnp.pad(g, (0, SLOT - QK_HEAD))[None, :]
    return dict(wq=_slot_cols(w_uq, QK_HEAD).astype(BF16), wk=wk.astype(BF16), wv=wv.astype(BF16),
                qa=q_a_norm[None, :], kva=kv_a_norm[None, :], qg=pad_gain(q_norm), kg=pad_gain(k_norm))


def _rope_tables(seq):
    pos = jnp.arange(seq, dtype=F32)
    inv_freq = ROPE_THETA ** (-jnp.arange(0, QK_ROPE, 2, dtype=F32) / QK_ROPE)
    ang = pos[:, None] * inv_freq
    ang = jnp.concatenate([ang, ang], axis=-1)
    pad = lambda t, fill: jnp.concatenate(
        [jnp.full((seq, QK_NOPE), fill, F32), t, jnp.full((seq, SLOT - QK_HEAD), fill, F32)], axis=1)
    return pad(jnp.cos(ang), 1.0), pad(jnp.sin(ang), 0.0)


def _hi_lo(w):
    hi = w.astype(BF16)
    return jnp.stack([hi, (w - hi.astype(F32)).astype(BF16)], axis=1)


def _filter_weights(w1, b1, w2, b2, w3, freq, decay):
    hid, nbands = FILTER_HIDDEN, FILTER_BANDS
    assert 2 * hid == LANES and 2 * nbands <= hid
    both = lambda row: jnp.concatenate([row, row])[None, :]
    blockdiag = lambda a: jnp.concatenate(
        [jnp.concatenate([a, jnp.zeros_like(a)], axis=1), jnp.concatenate([jnp.zeros_like(a), a], axis=1)], axis=0)
    bands = jnp.linspace(1e-4, nbands - 1, nbands, dtype=F32)
    rest = jnp.zeros((hid - 2 * nbands,), F32)
    band_half = jnp.concatenate([bands, bands, rest])
    phase_half = jnp.concatenate([jnp.zeros((nbands,), F32), jnp.full((nbands,), 0.5 * math.pi, F32), rest])
    w1_half = jnp.concatenate([-w1[1 + nbands:], w1[1:1 + nbands], jnp.zeros((hid - 2 * nbands, hid), F32)], axis=0)
    cols = HYENA_ORDER * D_HYENA
    w3d = jnp.transpose(w3.reshape(hid, N_DIR, cols), (1, 0, 2))
    w3p = jnp.stack([jnp.pad(w3d, ((0, 0), (0, hid), (0, 0))), jnp.pad(w3d, ((0, 0), (hid, 0), (0, 0)))], axis=1)
    w3_hi = w3p.astype(BF16)
    w3_lo = (w3p - w3_hi.astype(F32)).astype(BF16)
    return dict(bands=both(band_half), phase=both(phase_half), w1t=both(w1[0]), w1=blockdiag(w1_half),
                b1=both(b1), w2=blockdiag(w2), b2=both(b2), freq=both(freq), w3_hi=w3_hi, w3_lo=w3_lo,
                decay=decay.reshape(N_DIR, 1, cols))


def _trunk(x, p):
    bsz, seq, _ = x.shape
    n = bsz * seq
    cos_t, sin_t = _rope_tables(seq)
    x2 = x.reshape(n, D_MODEL)
    for l in range(DEPTH):
        uh, cq, ckvr = _inproj(x2, p["attn_norm"][l][None, :], p["w_in"][l])
        y_h = _hyena(uh.reshape(bsz, seq, COL_HYENA), p["conv_w"][l], p["conv_b"][l][None, :],
                     p["filt"][l], p["hyena_bias"][l])
        qt, k, vt = _qkv(cq.reshape(bsz, seq, -1), ckvr.reshape(bsz, seq, -1), cos_t, sin_t, p["attn"][l])
        y_a = _flash(qt, k, vt)
        xm, xn, aff_t, aff_n = _outproj(y_h.reshape(n, D_HYENA), y_a.reshape(n, D_ATTN), x2,
                                 p["out_norm"][l][None, :], p["w_out"][l], p["mlp_norm"][l][None, :],
                                 p["w_router"][l])
        x2 = _ec_moe(xm, xn, aff_t, aff_n, p["w_gate"][l], p["w_up"][l], p["w_down"][l])
    return x2.reshape(bsz, seq, D_MODEL)


def kernel(x_prompt, x_sample, attn_norm, w_in, conv_w, conv_b, filt_w1, filt_b1, filt_w2, filt_b2, filt_w3,
           filt_freq, filt_decay, hyena_bias, q_a_norm, w_uq, kv_a_norm, w_ukv, q_norm, k_norm, out_norm,
           w_out, mlp_norm, w_router, w_gate, w_up, w_down):
    p = dict(
        attn_norm=attn_norm,
        w_in=jnp.pad(w_in, ((0, 0), (0, 0), (0, D_IN_PAD - D_IN))).astype(BF16),
        conv_w=conv_w, conv_b=conv_b, hyena_bias=hyena_bias,
        filt=[_filter_weights(filt_w1[l], filt_b1[l], filt_w2[l], filt_b2[l], filt_w3[l], filt_freq[l],
                              filt_decay[l]) for l in range(DEPTH)],
        attn=[_attn_weights(q_a_norm[l], w_uq[l], kv_a_norm[l], w_ukv[l], q_norm[l], k_norm[l])
              for l in range(DEPTH)],
        out_norm=out_norm, w_out=w_out.astype(BF16), mlp_norm=mlp_norm,
        w_router=_hi_lo(jnp.pad(w_router, ((0, 0), (0, 0), (0, LANES - N_EXPERTS)))),
        w_gate=w_gate.astype(BF16), w_up=w_up.astype(BF16), w_down=w_down.astype(BF16))
    return (_trunk(x_prompt, p), _trunk(x_sample, p))
```

```python
import functools
import math

import jax
import jax.numpy as jnp
from jax import lax
from jax.experimental import pallas as pl
from jax.experimental.pallas import tpu as pltpu

F32 = jnp.float32
BF16 = jnp.bfloat16
I32 = jnp.int32
HIGHEST = lax.Precision.HIGHEST

D_MODEL = 1024
DEPTH = 2
D_HYENA = 512
HYENA_ORDER = 2
FILTER_BANDS = 16
FILTER_HIDDEN = 64
N_DIR = 2
DECAY_SHIFT = 0.05
N_HEADS = 8
QK_NOPE = 64
QK_ROPE = 32
QK_HEAD = QK_NOPE + QK_ROPE
V_HEAD = 64
V_AUG = V_HEAD + 16
D_ATTN = N_HEADS * V_HEAD
Q_LORA = 256
KV_LORA = 128
ROPE_THETA = 10000.0
N_EXPERTS = 16
EC_CAPACITY = 2
D_EXPERT = 512
EPS = 1e-6
COL_HYENA = (HYENA_ORDER + 1) * D_HYENA
COL_Q = COL_HYENA + Q_LORA
COL_KV = COL_Q + KV_LORA
D_IN = COL_KV + QK_ROPE
D_IN_PAD = 2048

LANES = 128
SLOT = 128
TOK_BLOCK = 256
SLOT_TILE = 256
ALIGN = 16
GATHER_ROWS = 64
COMBINE_FAST_ROWS = 128
ATT_TQ = 512
ATT_QC = 256
ATT_TK = 256
ATT_UNROLL = 16
MIB = 1024 * 1024


def _params(sem, vmem_mib):
    return pltpu.CompilerParams(dimension_semantics=sem, vmem_limit_bytes=vmem_mib * MIB)


def _rms(x):
    return x * lax.rsqrt(jnp.mean(x * x, axis=-1, keepdims=True) + EPS)


def _inproj_body(x_ref, g_ref, w_ref, uh_ref, cq_ref, ckvr_ref):
    xn = _rms(x_ref[...]) * g_ref[...]
    u = jnp.dot(xn.astype(BF16), w_ref[...], preferred_element_type=F32)
    uh_ref[...] = u[:, :COL_HYENA]
    cq_ref[...] = u[:, COL_HYENA:COL_Q]
    ckvr_ref[...] = u[:, COL_Q:]


def _inproj(x2d, g, w_pad):
    n = x2d.shape[0]
    tm = min(512, n)
    return pl.pallas_call(
        _inproj_body,
        grid=(n // tm,),
        in_specs=[pl.BlockSpec((tm, D_MODEL), lambda i: (i, 0)),
                  pl.BlockSpec((1, D_MODEL), lambda i: (0, 0)),
                  pl.BlockSpec((D_MODEL, D_IN_PAD), lambda i: (0, 0))],
        out_specs=[pl.BlockSpec((tm, COL_HYENA), lambda i: (i, 0)),
                   pl.BlockSpec((tm, Q_LORA), lambda i: (i, 0)),
                   pl.BlockSpec((tm, D_IN_PAD - COL_Q), lambda i: (i, 0))],
        out_shape=[jax.ShapeDtypeStruct((n, COL_HYENA), F32),
                   jax.ShapeDtypeStruct((n, Q_LORA), F32),
                   jax.ShapeDtypeStruct((n, D_IN_PAD - COL_Q), F32)],
        compiler_params=_params(("parallel",), 40),
        name="inproj",
    )(x2d, g, w_pad)


def _sconv_body(u_ref, prev_ref, next_ref, w_ref, b_ref, v_ref, x1_ref, x2_ref):
    i = pl.program_id(1)
    last = pl.num_programs(1) - 1
    u = u_ref[0]
    tl = u.shape[0]
    prev_row = jnp.where(i == 0, 0.0, prev_ref[0, 7:8, :])
    next_row = jnp.where(i == last, 0.0, next_ref[0, 0:1, :])
    row = lax.broadcasted_iota(I32, u.shape, 0)
    up = jnp.where(row == 0, prev_row, pltpu.roll(u, 1, axis=0))
    dn = jnp.where(row == tl - 1, next_row, pltpu.roll(u, tl - 1, axis=0))
    y = up * w_ref[0:1, :] + u * w_ref[1:2, :] + dn * w_ref[2:3, :] + b_ref[...]
    v_ref[0] = y[:, :D_HYENA]
    x1_ref[0] = y[:, D_HYENA:2 * D_HYENA]
    x2_ref[0] = y[:, 2 * D_HYENA:]


def _sconv(uh, w, b):
    bsz, seq, c = uh.shape
    tl = min(512, seq)
    r = tl // 8
    nblk8 = seq // 8
    out = jax.ShapeDtypeStruct((bsz, seq, D_HYENA), F32)
    ospec = pl.BlockSpec((1, tl, D_HYENA), lambda bi, i: (bi, i, 0))
    return pl.pallas_call(
        _sconv_body,
        grid=(bsz, seq // tl),
        in_specs=[pl.BlockSpec((1, tl, c), lambda bi, i: (bi, i, 0)),
                  pl.BlockSpec((1, 8, c), lambda bi, i: (bi, jnp.maximum(i * r - 1, 0), 0)),
                  pl.BlockSpec((1, 8, c), lambda bi, i: (bi, jnp.minimum((i + 1) * r, nblk8 - 1), 0)),
                  pl.BlockSpec((3, c), lambda bi, i: (0, 0)),
                  pl.BlockSpec((1, c), lambda bi, i: (0, 0))],
        out_specs=[ospec, ospec, ospec],
        out_shape=[out, out, out],
        compiler_params=_params(("parallel", "parallel"), 40),
        name="sconv",
    )(uh, uh, uh, w, b)


def _filter_body(bands_ref, phase_ref, w1t_ref, w1_ref, b1_ref, w2_ref, b2_ref, fr_ref, w3h_ref, w3l_ref,
                 dec_ref, k_ref, sum_ref, *, seq):
    i = pl.program_id(0)
    tr = k_ref.shape[0]
    hp = tr // 2
    n_lo = i * tr + lax.broadcasted_iota(I32, (hp, 1), 0)
    n_hi = n_lo + hp
    t_of = lambda n: jnp.where(n < seq, n, 2 * seq - n).astype(F32)
    lane = lax.broadcasted_iota(I32, (hp, LANES), 1)
    t_idx = jnp.where(lane < FILTER_HIDDEN, t_of(n_lo), t_of(n_hi))
    feats = jnp.sin(2.0 * math.pi * bands_ref[...] * t_idx / seq + phase_ref[...])
    fr = fr_ref[...]
    pre = (t_idx / (seq - 1)) * w1t_ref[...] + jnp.dot(feats, w1_ref[...], precision=HIGHEST,
                                                      preferred_element_type=F32)
    h = jnp.sin(fr * (pre + b1_ref[...]))
    h = jnp.sin(fr * (jnp.dot(h, w2_ref[...], precision=HIGHEST, preferred_element_type=F32) + b2_ref[...]))
    h_hi = h.astype(BF16)
    h_lo = (h - h_hi.astype(F32)).astype(BF16)

    @pl.when(i == 0)
    def _():
        sum_ref[...] = jnp.zeros_like(sum_ref)

    for j, n in enumerate((n_lo, n_hi)):
        out = (jnp.dot(h_hi, w3h_ref[0, j], preferred_element_type=F32)
               + jnp.dot(h_lo, w3h_ref[0, j], preferred_element_type=F32)
               + jnp.dot(h_hi, w3l_ref[0, j], preferred_element_type=F32))
        window = jnp.exp(-(t_of(n) / (seq - 1)) * dec_ref[0]) + DECAY_SHIFT
        k = jnp.where(n == seq, 0.0, out * window)
        k_ref[j * hp:(j + 1) * hp, :] = k.astype(BF16)
        sum_ref[...] += jnp.sum(jnp.abs(k), axis=0, keepdims=True)


def _filters(seq, fw):
    cols = HYENA_ORDER * D_HYENA
    tr = min(512, seq)
    half = seq // tr
    const = lambda shape: pl.BlockSpec(shape, lambda i: (0,) * len(shape))
    return pl.pallas_call(
        functools.partial(_filter_body, seq=seq),
        grid=(2 * seq // tr,),
        in_specs=[const((1, LANES)), const((1, LANES)), const((1, LANES)), const((LANES, LANES)),
                  const((1, LANES)), const((LANES, LANES)), const((1, LANES)), const((1, LANES)),
                  pl.BlockSpec((1, 2, LANES, cols), lambda i: (i // half, 0, 0, 0)),
                  pl.BlockSpec((1, 2, LANES, cols), lambda i: (i // half, 0, 0, 0)),
                  pl.BlockSpec((1, 1, cols), lambda i: (i // half, 0, 0))],
        out_specs=[pl.BlockSpec((tr, cols), lambda i: (i, 0)),
                   pl.BlockSpec((1, cols), lambda i: (0, 0))],
        out_shape=[jax.ShapeDtypeStruct((2 * seq, cols), BF16),
                   jax.ShapeDtypeStruct((1, cols), F32)],
        compiler_params=_params(("arbitrary",), 32),
        name="filter_gen",
    )(fw["bands"], fw["phase"], fw["w1t"], fw["w1"], fw["b1"], fw["w2"], fw["b2"], fw["freq"],
      fw["w3_hi"], fw["w3_lo"], fw["decay"])


def _fft_dims(seq):
    n2 = 128 if 2 * seq >= 32768 else 64
    n1 = 2 * seq // n2
    return n1, n2


def _dft_tables(n1, n2):
    n = n1 * n2
    n1h = n1 // 2
    k1 = jnp.arange(n1h, dtype=I32)[:, None]
    m1 = jnp.arange(n1, dtype=I32)[None, :]
    ang = (2.0 * math.pi / n1) * ((k1 * m1) % n1).astype(F32)
    top = jnp.cos(ang)
    bot = -jnp.sin(ang)
    nyq = jnp.where(m1 % 2 == 0, 1.0, -1.0).astype(F32)
    bot = jnp.concatenate([nyq, bot[1:]], axis=0)
    fa = jnp.concatenate([top, bot], axis=0)
    weight = jnp.where((jnp.arange(n1) % n1h) == 0, 1.0, 2.0).astype(F32) / n
    fi = (fa[:, :n1h] * weight[:, None]).T

    kk = jnp.arange(n1h + 1, dtype=I32)[:, None, None]
    k2 = jnp.arange(n2, dtype=I32)[None, :, None]
    m2 = jnp.arange(n2, dtype=I32)[None, None, :]
    phi = (2.0 * math.pi / n) * ((m2 * (kk + n1 * k2)) % n).astype(F32)
    gr = jnp.cos(phi)
    gi = -jnp.sin(phi)
    blk = jnp.concatenate([jnp.concatenate([gr, -gi], axis=2),
                           jnp.concatenate([gi, gr], axis=2)], axis=1)
    left = (jnp.arange(2 * n2) < n2)[None, :]
    g0 = jnp.where(left, blk[0], 0.0)
    gf = jnp.concatenate([g0[None], blk[1:n1h]], axis=0)
    gnyq = jnp.concatenate([jnp.zeros((2 * n2, n2), F32), blk[n1h][:, :n2]], axis=1)
    return dict(fa_full=fa.astype(BF16), fa_half=fa[:, :n1h].astype(BF16), fi=fi.astype(BF16),
                gf=gf.astype(BF16), gnyq=gnyq.astype(BF16),
                ginv=jnp.transpose(gf, (0, 2, 1)).astype(BF16), m2=gnyq.T.astype(BF16))


def _fft_a_body(x_ref, f_ref, o_ref):
    a = jnp.dot(f_ref[...], x_ref[0].astype(BF16), preferred_element_type=F32)
    n1h = o_ref.shape[2]
    o_ref[0, 0] = a[:n1h].astype(BF16)
    o_ref[0, 1] = a[n1h:].astype(BF16)


def _fft_a(x3, fmat):
    bsz, r, nc = x3.shape
    n1 = fmat.shape[0]
    tn = min(8192, nc)
    return pl.pallas_call(
        _fft_a_body,
        grid=(bsz, nc // tn),
        in_specs=[pl.BlockSpec((1, r, tn), lambda b, j: (b, 0, j)),
                  pl.BlockSpec((n1, r), lambda b, j: (0, 0))],
        out_specs=pl.BlockSpec((1, 2, n1 // 2, tn), lambda b, j: (b, 0, 0, j)),
        out_shape=jax.ShapeDtypeStruct((bsz, 2, n1 // 2, nc), BF16),
        compiler_params=_params(("parallel", "parallel"), 48),
        name="fft_stage_a",
    )(x3, fmat)


def _fft_b_body(ar_ref, ai_ref, g_ref, gn_ref, sc_ref, x_ref, xn_ref, *, kb):
    j = pl.program_id(2)
    inv = 1.0 / sc_ref[...]
    for kk in range(kb):
        rhs = jnp.concatenate([ar_ref[0, 0, kk], ai_ref[0, 0, kk]], axis=0)
        x_ref[0, kk] = (jnp.dot(g_ref[kk], rhs, preferred_element_type=F32) * inv).astype(BF16)

    @pl.when(j == 0)
    def _():
        rhs = jnp.concatenate([ar_ref[0, 0, 0], ai_ref[0, 0, 0]], axis=0)
        xn_ref[0] = (jnp.dot(gn_ref[...], rhs, preferred_element_type=F32) * inv).astype(BF16)


def _fft_b(a5, tabs, scale):
    bsz, _, n1h, n2, c = a5.shape
    kb = 8
    tc = min(512, c)
    return pl.pallas_call(
        functools.partial(_fft_b_body, kb=kb),
        grid=(bsz, c // tc, n1h // kb),
        in_specs=[pl.BlockSpec((1, 1, kb, n2, tc), lambda b, ci, j: (b, 0, j, 0, ci)),
                  pl.BlockSpec((1, 1, kb, n2, tc), lambda b, ci, j: (b, 1, j, 0, ci)),
                  pl.BlockSpec((kb, 2 * n2, 2 * n2), lambda b, ci, j: (j, 0, 0)),
                  pl.BlockSpec((2 * n2, 2 * n2), lambda b, ci, j: (0, 0)),
                  pl.BlockSpec((1, tc), lambda b, ci, j: (0, ci))],
        out_specs=[pl.BlockSpec((1, kb, 2 * n2, tc), lambda b, ci, j: (b, j, 0, ci)),
                   pl.BlockSpec((1, 2 * n2, tc), lambda b, ci, j: (b, 0, ci))],
        out_shape=[jax.ShapeDtypeStruct((bsz, n1h, 2 * n2, c), BF16),
                   jax.ShapeDtypeStruct((bsz, 2 * n2, c), BF16)],
        compiler_params=_params(("parallel", "parallel", "arbitrary"), 40),
        name="fft_stage_b",
    )(a5, a5, tabs["gf"], tabs["gnyq"], scale)


def _cmul(x, k, n2):
    xr, xi = x[:n2], x[n2:]
    kr, ki = k[:n2], k[n2:]
    return jnp.concatenate([xr * kr - xi * ki, xr * ki + xi * kr], axis=0).astype(BF16)


def _spec_body(ar_ref, ai_ref, k_ref, kn_ref, g_ref, gn_ref, gi_ref, m2_ref, o_ref, *, kb):
    j = pl.program_id(1)
    n2 = o_ref.shape[3]

    def through(fwd, a_re, a_im, kf, inv):
        x = jnp.dot(fwd, jnp.concatenate([a_re, a_im], axis=0), preferred_element_type=F32)
        return jnp.dot(inv, _cmul(x, kf.astype(F32), n2), preferred_element_type=F32)

    for kk in range(kb):
        acc = through(g_ref[kk], ar_ref[0, 0, kk], ai_ref[0, 0, kk], k_ref[0, kk], gi_ref[kk])
        o_ref[0, 0, kk] = acc[:n2].astype(BF16)
        o_ref[0, 1, kk] = acc[n2:].astype(BF16)

    @pl.when(j == 0)
    def _():
        acc = through(gn_ref[...], ar_ref[0, 0, 0], ai_ref[0, 0, 0], kn_ref[0], m2_ref[...])
        o_ref[0, 1, 0] = acc[n2:].astype(BF16)


def _spec_conv(a5, kf, kfnyq, order, tabs):
    bsz, _, n1h, n2, c = a5.shape
    tn2 = 2 * n2
    kb = 8
    return pl.pallas_call(
        functools.partial(_spec_body, kb=kb),
        grid=(bsz, n1h // kb),
        in_specs=[pl.BlockSpec((1, 1, kb, n2, c), lambda b, j: (b, 0, j, 0, 0)),
                  pl.BlockSpec((1, 1, kb, n2, c), lambda b, j: (b, 1, j, 0, 0)),
                  pl.BlockSpec((1, kb, tn2, c), lambda b, j: (0, j, 0, order)),
                  pl.BlockSpec((1, tn2, c), lambda b, j: (0, 0, order)),
                  pl.BlockSpec((kb, tn2, tn2), lambda b, j: (j, 0, 0)),
                  pl.BlockSpec((tn2, tn2), lambda b, j: (0, 0)),
                  pl.BlockSpec((kb, tn2, tn2), lambda b, j: (j, 0, 0)),
                  pl.BlockSpec((tn2, tn2), lambda b, j: (0, 0))],
        out_specs=pl.BlockSpec((1, 2, kb, n2, c), lambda b, j: (b, 0, j, 0, 0)),
        out_shape=jax.ShapeDtypeStruct((bsz, 2, n1h, n2, c), BF16),
        compiler_params=_params(("parallel", "arbitrary"), 48),
        name="spectral_conv",
    )(a5, a5, kf, kfnyq, tabs["gf"], tabs["gnyq"], tabs["ginv"], tabs["m2"])


def _ifft_a_body(b_ref, f_ref, z_ref, gate_ref, bias_ref, o_ref):
    y = jnp.dot(f_ref[...], b_ref[0], preferred_element_type=F32)
    o_ref[0] = gate_ref[0] * (y + z_ref[0] * bias_ref[...])


def _ifft_a(b3, fi, z3, gate3, bias_t):
    bsz, n1, nc = b3.shape
    n1h = n1 // 2
    tn = bias_t.shape[1]
    return pl.pallas_call(
        _ifft_a_body,
        grid=(bsz, nc // tn),
        in_specs=[pl.BlockSpec((1, n1, tn), lambda b, j: (b, 0, j)),
                  pl.BlockSpec((n1h, n1), lambda b, j: (0, 0)),
                  pl.BlockSpec((1, n1h, tn), lambda b, j: (b, 0, j)),
                  pl.BlockSpec((1, n1h, tn), lambda b, j: (b, 0, j)),
                  pl.BlockSpec((1, tn), lambda b, j: (0, 0))],
        out_specs=pl.BlockSpec((1, n1h, tn), lambda b, j: (b, 0, j)),
        out_shape=jax.ShapeDtypeStruct((bsz, n1h, nc), F32),
        compiler_params=_params(("parallel", "parallel"), 48),
        name="ifft_stage_a",
    )(b3, fi, z3, gate3, bias_t)


def _hyena(uh, conv_w, conv_b, fw, bias):
    bsz, seq, _ = uh.shape
    c = D_HYENA
    n1, n2 = _fft_dims(seq)
    n1h = n1 // 2
    tabs = _dft_tables(n1, n2)
    v, x1, x2 = _sconv(uh, conv_w, conv_b)

    kcirc, ksum = _filters(seq, fw)
    cols = HYENA_ORDER * c
    ka = _fft_a(kcirc.reshape(1, n1, n2 * cols), tabs["fa_full"])
    kf, kfnyq = _fft_b(ka.reshape(1, 2, n1h, n2, cols), tabs, ksum)

    tn = min(8192, n2 * c)
    z = v
    for order, gate in enumerate((x1, x2)):
        za = _fft_a(z.reshape(bsz, n1h, n2 * c), tabs["fa_half"])
        zb = _spec_conv(za.reshape(bsz, 2, n1h, n2, c), kf, kfnyq, order, tabs)
        bias_t = jnp.tile(bias[order][None, :], (1, tn // c))
        z = _ifft_a(zb.reshape(bsz, n1, n2 * c), tabs["fi"], z.reshape(bsz, n1h, n2 * c),
                    gate.reshape(bsz, n1h, n2 * c), bias_t).reshape(bsz, seq, c)
    return z


def _qkv_body(cq_ref, ckvr_ref, cos_ref, sin_ref, wq_ref, wk_ref, wv_ref, qa_ref, kva_ref, qg_ref, kg_ref,
              qt_ref, k_ref, vt_ref):
    cqn = _rms(cq_ref[0]) * qa_ref[...]
    qs = jnp.dot(cqn.astype(BF16), wq_ref[...], preferred_element_type=F32)
    ck = ckvr_ref[0]
    ckvn = (_rms(ck[:, :KV_LORA]) * kva_ref[...]).astype(BF16)
    ks = jnp.dot(ckvn, wk_ref[...], preferred_element_type=F32)
    vs = jnp.dot(ckvn, wv_ref[...], preferred_element_type=F32)
    krope = pltpu.roll(ck[:, KV_LORA:], QK_NOPE, axis=1)
    cos = cos_ref[...]
    sin = sin_ref[...]
    lane = lax.broadcasted_iota(I32, cos.shape, 1)
    half = QK_ROPE // 2
    first = (lane >= QK_NOPE) & (lane < QK_NOPE + half)
    second = (lane >= QK_NOPE + half) & (lane < QK_HEAD)

    def head(xh, gain, scale):
        ms = jnp.sum(xh * xh, axis=-1, keepdims=True) * (1.0 / QK_HEAD)
        xn = xh * lax.rsqrt(ms + EPS) * gain
        rot = jnp.where(first, -pltpu.roll(xn, SLOT - half, axis=1),
                        jnp.where(second, pltpu.roll(xn, half, axis=1), 0.0))
        return (xn * cos + rot * sin) * scale

    q_scale = QK_HEAD ** -0.5 * math.log2(math.e)
    for h in range(N_HEADS):
        sl = slice(h * SLOT, (h + 1) * SLOT)
        qt_ref[0, h] = head(qs[:, sl], qg_ref[...], q_scale).T.astype(BF16)
        k_ref[0, h] = head(ks[:, sl] + krope, kg_ref[...], 1.0).astype(BF16)
    aug = lax.broadcasted_iota(I32, (V_AUG - V_HEAD, cos.shape[0]), 0)
    ones_row = jnp.where(aug == 0, 1.0, 0.0)
    for hp in range(N_HEADS // 2):
        pair_t = vs[:, hp * 2 * V_HEAD:(hp + 1) * 2 * V_HEAD].T
        vt_ref[0, hp] = jnp.concatenate(
            [pair_t[:V_HEAD], ones_row, pair_t[V_HEAD:], ones_row], axis=0).astype(BF16)


def _qkv(cq, ckvr, cos_t, sin_t, aw):
    bsz, seq, _ = cq.shape
    tm = min(512, seq)
    const = lambda shape: pl.BlockSpec(shape, lambda b, i: (0,) * len(shape))
    return pl.pallas_call(
        _qkv_body,
        grid=(bsz, seq // tm),
        in_specs=[pl.BlockSpec((1, tm, Q_LORA), lambda b, i: (b, i, 0)),
                  pl.BlockSpec((1, tm, D_IN_PAD - COL_Q), lambda b, i: (b, i, 0)),
                  pl.BlockSpec((tm, SLOT), lambda b, i: (i, 0)),
                  pl.BlockSpec((tm, SLOT), lambda b, i: (i, 0)),
                  const((Q_LORA, N_HEADS * SLOT)), const((KV_LORA, N_HEADS * SLOT)),
                  const((KV_LORA, D_ATTN)),
                  const((1, Q_LORA)), const((1, KV_LORA)), const((1, SLOT)), const((1, SLOT))],
        out_specs=[pl.BlockSpec((1, N_HEADS, SLOT, tm), lambda b, i: (b, 0, 0, i)),
                   pl.BlockSpec((1, N_HEADS, tm, SLOT), lambda b, i: (b, 0, i, 0)),
                   pl.BlockSpec((1, N_HEADS // 2, 2 * V_AUG, tm), lambda b, i: (b, 0, 0, i))],
        out_shape=[jax.ShapeDtypeStruct((bsz, N_HEADS, SLOT, seq), BF16),
                   jax.ShapeDtypeStruct((bsz, N_HEADS, seq, SLOT), BF16),
                   jax.ShapeDtypeStruct((bsz, N_HEADS // 2, 2 * V_AUG, seq), BF16)],
        compiler_params=_params(("parallel", "parallel"), 40),
        name="qkv_prep",
    )(cq, ckvr, cos_t, sin_t, aw["wq"], aw["wk"], aw["wv"], aw["qa"], aw["kva"], aw["qg"], aw["kg"])


def _flash_body(qt_ref, k_ref, vt_ref, o_ref, s0_ref, s1_ref, acc_ref, *, tk, nk, nqc, unroll):
    chains = [(hh, qc) for hh in range(2) for qc in range(nqc)]
    acc_ref[...] = jnp.zeros(acc_ref.shape, F32)

    def qk(t, s_ref):
        ks = pl.multiple_of(t * tk, tk)
        tile_max = []
        for c, (hh, qc) in enumerate(chains):
            k = k_ref[0, hh, pl.ds(ks, tk), :]
            s = jnp.dot(k, qt_ref[0, hh, :, qc * ATT_QC:(qc + 1) * ATT_QC], preferred_element_type=F32)
            s_ref[c] = s
            tile_max.append(jnp.max(s, axis=0, keepdims=True))
        return tile_max

    def softmax_pv(t, s_ref, tile_max, m):
        ks = pl.multiple_of(t * tk, tk)
        m_out = []
        for c, (hh, qc) in enumerate(chains):
            m_new = jnp.maximum(m[c], tile_max[c])
            a = jnp.exp2(m[c] - m_new)
            p = jnp.exp2(s_ref[c] - m_new)
            m_out.append(m_new)
            vt = vt_ref[0, 0, hh * V_AUG:(hh + 1) * V_AUG, pl.ds(ks, tk)]
            acc_ref[c] = acc_ref[c] * a + jnp.dot(vt, p.astype(BF16), preferred_element_type=F32)
        return m_out

    def group(u, carry):
        tile_max, m = carry
        for i in range(0, unroll, 2):
            t = unroll * u + i
            tile_max1 = qk(t + 1, s1_ref)
            m = softmax_pv(t, s0_ref, tile_max, m)
            tile_max = qk(jnp.minimum(t + 2, nk - 1), s0_ref)
            m = softmax_pv(t + 1, s1_ref, tile_max1, m)
        return tile_max, m

    init = (qk(0, s0_ref), [jnp.full((1, ATT_QC), -jnp.inf, F32)] * len(chains))
    lax.fori_loop(0, nk // unroll, group, init)

    def normalised(c):
        acc = acc_ref[c]
        return acc[:V_HEAD] * (1.0 / acc[V_HEAD:V_HEAD + 1])

    heads = [jnp.concatenate([normalised(hh * nqc + qc) for qc in range(nqc)], axis=1) for hh in range(2)]
    o_ref[0] = jnp.concatenate(heads, axis=0).T


def _flash(qt, k, vt):
    bsz, nh, _, seq = qt.shape
    tq = min(ATT_TQ, seq)
    tk = min(ATT_TK, seq)
    nqc = tq // ATT_QC
    nk = seq // tk
    unroll = min(ATT_UNROLL, nk)
    assert nk % unroll == 0 and unroll % 2 == 0
    return pl.pallas_call(
        functools.partial(_flash_body, tk=tk, nk=nk, nqc=nqc, unroll=unroll),
        grid=(bsz, nh // 2, seq // tq),
        in_specs=[pl.BlockSpec((1, 2, SLOT, tq), lambda b, hp, i: (b, hp, 0, i)),
                  pl.BlockSpec((1, 2, seq, SLOT), lambda b, hp, i: (b, hp, 0, 0)),
                  pl.BlockSpec((1, 1, 2 * V_AUG, seq), lambda b, hp, i: (b, hp, 0, 0))],
        out_specs=pl.BlockSpec((1, tq, 2 * V_HEAD), lambda b, hp, i: (b, i, hp)),
        out_shape=jax.ShapeDtypeStruct((bsz, seq, nh * V_HEAD), F32),
        scratch_shapes=[pltpu.VMEM((2 * nqc, tk, ATT_QC), F32), pltpu.VMEM((2 * nqc, tk, ATT_QC), F32),
                        pltpu.VMEM((2 * nqc, V_AUG, ATT_QC), F32)],
        compiler_params=_params(("parallel", "parallel", "parallel"), 48),
        name="flash_attn",
    )(qt, k, vt)


def _outproj_body(yh_ref, ya_ref, x_ref, og_ref, w_ref, mg_ref, wr_ref, xm_ref, xn_ref, aff_ref, affn_ref):
    og = og_ref[...]
    half = D_MODEL // 2
    y = jnp.concatenate([_rms(yh_ref[...]) * og[:, :half], _rms(ya_ref[...]) * og[:, half:]], axis=1)
    xm = x_ref[...] + jnp.dot(y.astype(BF16), w_ref[...], preferred_element_type=F32)
    xm_ref[...] = xm
    xn = _rms(xm) * mg_ref[...]
    xn_hi = xn.astype(BF16)
    xn_ref[...] = xn_hi
    xn_lo = (xn - xn_hi.astype(F32)).astype(BF16)
    logits = (jnp.dot(xn_hi, wr_ref[0], preferred_element_type=F32)
              + jnp.dot(xn_lo, wr_ref[0], preferred_element_type=F32)
              + jnp.dot(xn_hi, wr_ref[1], preferred_element_type=F32))
    lane = lax.broadcasted_iota(I32, logits.shape, 1)
    logits = jnp.where(lane < N_EXPERTS, logits, -jnp.inf)
    e = jnp.exp(logits - jnp.max(logits, axis=-1, keepdims=True))
    aff = e / jnp.sum(e, axis=-1, keepdims=True)
    aff_ref[...] = aff.T[:N_EXPERTS]
    affn_ref[...] = aff[:, :N_EXPERTS]


def _outproj(yh, ya, x2d, og, w_out, mg, wr_pad):
    n = x2d.shape[0]
    tm = min(512, n)
    half = D_MODEL // 2
    const = lambda shape: pl.BlockSpec(shape, lambda i: (0,) * len(shape))
    return pl.pallas_call(
        _outproj_body,
        grid=(n // tm,),
        in_specs=[pl.BlockSpec((tm, half), lambda i: (i, 0)),
                  pl.BlockSpec((tm, half), lambda i: (i, 0)),
                  pl.BlockSpec((tm, D_MODEL), lambda i: (i, 0)),
                  const((1, D_MODEL)), const((D_MODEL, D_MODEL)), const((1, D_MODEL)),
                  const((2, D_MODEL, LANES))],
        out_specs=[pl.BlockSpec((tm, D_MODEL), lambda i: (i, 0)),
                   pl.BlockSpec((tm, D_MODEL), lambda i: (i, 0)),
                   pl.BlockSpec((N_EXPERTS, tm), lambda i: (0, i)),
                   pl.BlockSpec((tm, N_EXPERTS), lambda i: (i, 0))],
        out_shape=[jax.ShapeDtypeStruct((n, D_MODEL), F32),
                   jax.ShapeDtypeStruct((n, D_MODEL), BF16),
                   jax.ShapeDtypeStruct((N_EXPERTS, n), F32),
                   jax.ShapeDtypeStruct((n, N_EXPERTS), F32)],
        compiler_params=_params(("parallel",), 40),
        name="outproj_router",
    )(yh, ya, x2d, og, w_out, mg, wr_pad)


def _select_body(aff_ref, upper_ref, lower_ref, pos_ref, off_ref, *, cap):
    bits = pltpu.bitcast(aff_ref[0], I32)
    upper = upper_ref[...]
    lower = lower_ref[...]

    def count(mask):
        return jnp.sum(jnp.sum(mask.astype(F32), axis=1, keepdims=True), axis=0, keepdims=True)

    def bit_step(i, thr):
        cand = thr | jnp.left_shift(jnp.int32(1), 30 - i)
        return jnp.where(count(bits >= cand) >= cap, cand, thr)

    thr = lax.fori_loop(0, 31, bit_step, jnp.zeros((1, 1), I32))
    gt = bits > thr
    eq = bits == thr
    need = cap - count(gt)

    def prefix(mask):
        within = jnp.dot(mask.astype(BF16), upper, preferred_element_type=F32)
        total = within[:, LANES - 1:LANES]
        offs = jnp.dot(lower, jnp.broadcast_to(total, within.shape).astype(BF16), preferred_element_type=F32)
        return within, offs

    w_eq, o_eq = prefix(eq)
    sel = gt | (eq & (o_eq + w_eq <= need))
    w_sel, o_sel = prefix(sel)
    pos_ref[0] = jnp.where(sel, o_sel + w_sel - 1.0, -1.0).astype(I32)
    off_ref[0] = o_sel.astype(I32)


def _select(aff_t, cap):
    ne, n = aff_t.shape
    rows = n // LANES
    upper = (jnp.arange(LANES)[:, None] <= jnp.arange(LANES)[None, :]).astype(BF16)
    lower = (jnp.arange(rows)[None, :] < jnp.arange(rows)[:, None]).astype(BF16)
    blk = pl.BlockSpec((1, rows, LANES), lambda e: (e, 0, 0))
    pos, off = pl.pallas_call(
        functools.partial(_select_body, cap=cap),
        grid=(ne,),
        in_specs=[blk, pl.BlockSpec((LANES, LANES), lambda e: (0, 0)),
                  pl.BlockSpec((rows, rows), lambda e: (0, 0))],
        out_specs=[blk, blk],
        out_shape=[jax.ShapeDtypeStruct((ne, rows, LANES), I32),
                   jax.ShapeDtypeStruct((ne, rows, LANES), I32)],
        compiler_params=_params(("parallel",), 32),
        name="ec_select",
    )(aff_t.reshape(ne, rows, LANES), upper, lower)
    return pos.reshape(ne, n), off[:, :, 0]


def _ffn_body(off_ref, nch_ref, pos_ref, x_ref, wg_ref, wu_ref, wd_ref, y_ref, xs_ref, *, nb, sub, cap):
    e = pl.program_id(0)
    sb = pl.program_id(1)

    @pl.when(sb == 0)
    def _():
        xs_ref[...] = jnp.zeros(xs_ref.shape, BF16)

    row = lax.broadcasted_iota(I32, (GATHER_ROWS, TOK_BLOCK), 0)

    def gather(i, c):
        blk = e * nb + sb * sub + i
        base = off_ref[blk] * ALIGN
        tok = slice(i * TOK_BLOCK, (i + 1) * TOK_BLOCK)
        start = pl.multiple_of(base + c * GATHER_ROWS, ALIGN)
        hit = (pos_ref[0, :, tok] - start) == row
        win = jnp.dot(hit.astype(BF16), x_ref[tok, :], preferred_element_type=F32)
        xs_ref[pl.ds(start, GATHER_ROWS), :] += win.astype(BF16)

    for i in range(sub):
        gather(i, 0)
    for i in range(sub):
        lax.fori_loop(1, nch_ref[e * nb + sb * sub + i], lambda c, carry, i=i: (gather(i, c), carry)[1], 0)

    @pl.when(sb == pl.num_programs(1) - 1)
    def _():
        for j in range(cap // SLOT_TILE):
            rows = slice(j * SLOT_TILE, (j + 1) * SLOT_TILE)
            xt = xs_ref[rows, :]
            g = jnp.dot(xt, wg_ref[0], preferred_element_type=F32)
            u = jnp.dot(xt, wu_ref[0], preferred_element_type=F32)
            h = (g * jax.nn.sigmoid(g) * u).astype(BF16)
            y_ref[0, rows, :] = jnp.dot(h, wd_ref[0], preferred_element_type=F32).astype(BF16)
        y_ref[0, cap:, :] = jnp.zeros((y_ref.shape[1] - cap, D_MODEL), BF16)


def _ffn(base, nchunk, pos3, xn, wg, wu, wd, cap):
    ne = wg.shape[0]
    n = xn.shape[0]
    nb = n // TOK_BLOCK
    sub = min(8, nb)
    cap_pad = cap + TOK_BLOCK + ALIGN
    grid_spec = pltpu.PrefetchScalarGridSpec(
        num_scalar_prefetch=2,
        grid=(ne, nb // sub),
        in_specs=[pl.BlockSpec((1, 1, sub * TOK_BLOCK), lambda e, s, off, nch: (e, 0, s)),
                  pl.BlockSpec((sub * TOK_BLOCK, D_MODEL), lambda e, s, off, nch: (s, 0)),
                  pl.BlockSpec((1, D_MODEL, D_EXPERT), lambda e, s, off, nch: (e, 0, 0)),
                  pl.BlockSpec((1, D_MODEL, D_EXPERT), lambda e, s, off, nch: (e, 0, 0)),
                  pl.BlockSpec((1, D_EXPERT, D_MODEL), lambda e, s, off, nch: (e, 0, 0))],
        out_specs=pl.BlockSpec((1, cap_pad, D_MODEL), lambda e, s, off, nch: (e, 0, 0)),
        scratch_shapes=[pltpu.VMEM((cap_pad, D_MODEL), BF16)])
    return pl.pallas_call(
        functools.partial(_ffn_body, nb=nb, sub=sub, cap=cap),
        grid_spec=grid_spec,
        out_shape=jax.ShapeDtypeStruct((ne, cap_pad, D_MODEL), BF16),
        compiler_params=_params(("parallel", "arbitrary"), 48),
        name="ec_ffn",
    )(base, nchunk, pos3, xn, wg, wu, wd)


def _combine_body(off_ref, spill_ref, post_ref, affn_ref, xm_ref, *refs, nb, ne, win_rows):
    win_refs, tail_refs, o_ref = refs[:ne], refs[ne:-1], refs[-1]
    b = pl.program_id(0)
    post = post_ref[...]
    affn = affn_ref[...]

    def expand(y_refs, shift):
        width = y_refs[0].shape[0]
        lane = lax.broadcasted_iota(I32, (TOK_BLOCK, width), 1)
        total = jnp.zeros((TOK_BLOCK, D_MODEL), F32)
        for e in range(ne):
            rel = post[:, e:e + 1] - (off_ref[e * nb + b] * ALIGN + shift)
            hit = (rel == lane).astype(BF16)
            total = total + affn[:, e:e + 1] * jnp.dot(hit, y_refs[e][...], preferred_element_type=F32)
        return total

    o_ref[...] = xm_ref[...] + expand(win_refs, 0)

    if tail_refs:
        @pl.when(spill_ref[b] != 0)
        def _():
            o_ref[...] += expand(tail_refs, win_rows)


def _combine(base, spill, pos_t, aff_n, xm, y, win_rows):
    ne = y.shape[0]
    n = xm.shape[0]
    nb = n // TOK_BLOCK
    tail_rows = ALIGN if win_rows == TOK_BLOCK else 0

    def window(e, rows, shift):
        return pl.BlockSpec((pl.Squeezed(), pl.Element(rows), pl.Element(D_MODEL)),
                            lambda b, off, sp: (e, (off[e * nb + b] + shift // ALIGN) * ALIGN, 0))

    tails = [window(e, tail_rows, win_rows) for e in range(ne)] if tail_rows else []
    grid_spec = pltpu.PrefetchScalarGridSpec(
        num_scalar_prefetch=2,
        grid=(nb,),
        in_specs=([pl.BlockSpec((TOK_BLOCK, ne), lambda b, off, sp: (b, 0)),
                   pl.BlockSpec((TOK_BLOCK, ne), lambda b, off, sp: (b, 0)),
                   pl.BlockSpec((TOK_BLOCK, D_MODEL), lambda b, off, sp: (b, 0))]
                  + [window(e, win_rows, 0) for e in range(ne)] + tails),
        out_specs=pl.BlockSpec((TOK_BLOCK, D_MODEL), lambda b, off, sp: (b, 0)))
    return pl.pallas_call(
        functools.partial(_combine_body, nb=nb, ne=ne, win_rows=win_rows),
        grid_spec=grid_spec,
        out_shape=jax.ShapeDtypeStruct((n, D_MODEL), F32),
        compiler_params=_params(("parallel",), 48),
        name="ec_combine",
    )(base, spill, pos_t, aff_n, xm, *([y] * (ne + len(tails))))


def _ec_moe(xm, xn, aff_t, aff_n, wg, wu, wd):
    ne, n = aff_t.shape
    cap = max(1, EC_CAPACITY * n // N_EXPERTS)
    nb = n // TOK_BLOCK
    pos, off = _select(aff_t, cap)
    first = off[:, ::TOK_BLOCK // LANES]
    count = jnp.concatenate([first[:, 1:], jnp.full((ne, 1), cap, I32)], axis=1) - first
    base = first // ALIGN
    span = jnp.where(count > 0, first - base * ALIGN + count, 0)
    spill = jnp.any(span > TOK_BLOCK, axis=0).astype(I32)
    nchunk = ((span + GATHER_ROWS - 1) // GATHER_ROWS).reshape(ne * nb).astype(I32)
    base = base.reshape(ne * nb).astype(I32)
    y = _ffn(base, nchunk, pos.reshape(ne, 1, n), xn, wg, wu, wd, cap)
    pos_t = pos.T
    return lax.cond(jnp.max(span) <= COMBINE_FAST_ROWS,
                    lambda: _combine(base, spill, pos_t, aff_n, xm, y, COMBINE_FAST_ROWS),
                    lambda: _combine(base, spill, pos_t, aff_n, xm, y, TOK_BLOCK))


def _slot_cols(w, head_width):
    k = w.shape[0]
    w3 = w.reshape(k, N_HEADS, head_width)
    return jnp.pad(w3, ((0, 0), (0, 0), (0, SLOT - head_width))).reshape(k, N_HEADS * SLOT)


def _attn_weights(q_a_norm, w_uq, kv_a_norm, w_ukv, q_norm, k_norm):
    wkv = w_ukv.reshape(KV_LORA, N_HEADS, QK_NOPE + V_HEAD)
    wk = _slot_cols(wkv[:, :, :QK_NOPE].reshape(KV_LORA, -1), QK_NOPE)
    wv = wkv[:, :, QK_NOPE:].reshape(KV_LORA, D_ATTN)
    pad_gain = lambda g: jnp.pad(g, (0, SLOT - QK_HEAD))[None, :]
    return dict(wq=_slot_cols(w_uq, QK_HEAD).astype(BF16), wk=wk.astype(BF16), wv=wv.astype(BF16),
                qa=q_a_norm[None, :], kva=kv_a_norm[None, :], qg=pad_gain(q_norm), kg=pad_gain(k_norm))


def _rope_tables(seq):
    pos = jnp.arange(seq, dtype=F32)
    inv_freq = ROPE_THETA ** (-jnp.arange(0, QK_ROPE, 2, dtype=F32) / QK_ROPE)
    ang = pos[:, None] * inv_freq
    ang = jnp.concatenate([ang, ang], axis=-1)
    pad = lambda t, fill: jnp.concatenate(
        [jnp.full((seq, QK_NOPE), fill, F32), t, jnp.full((seq, SLOT - QK_HEAD), fill, F32)], axis=1)
    return pad(jnp.cos(ang), 1.0), pad(jnp.sin(ang), 0.0)


def _hi_lo(w):
    hi = w.astype(BF16)
    return jnp.stack([hi, (w - hi.astype(F32)).astype(BF16)], axis=1)


def _filter_weights(w1, b1, w2, b2, w3, freq, decay):
    hid, nbands = FILTER_HIDDEN, FILTER_BANDS
    assert 2 * hid == LANES and 2 * nbands <= hid
    both = lambda row: jnp.concatenate([row, row])[None, :]
    blockdiag = lambda a: jnp.concatenate(
        [jnp.concatenate([a, jnp.zeros_like(a)], axis=1), jnp.concatenate([jnp.zeros_like(a), a], axis=1)], axis=0)
    bands = jnp.linspace(1e-4, nbands - 1, nbands, dtype=F32)
    rest = jnp.zeros((hid - 2 * nbands,), F32)
    band_half = jnp.concatenate([bands, bands, rest])
    phase_half = jnp.concatenate([jnp.zeros((nbands,), F32), jnp.full((nbands,), 0.5 * math.pi, F32), rest])
    w1_half = jnp.concatenate([-w1[1 + nbands:], w1[1:1 + nbands], jnp.zeros((hid - 2 * nbands, hid), F32)], axis=0)
    cols = HYENA_ORDER * D_HYENA
    w3d = jnp.transpose(w3.reshape(hid, N_DIR, cols), (1, 0, 2))
    w3p = jnp.stack([jnp.pad(w3d, ((0, 0), (0, hid), (0, 0))), jnp.pad(w3d, ((0, 0), (hid, 0), (0, 0)))], axis=1)
    w3_hi = w3p.astype(BF16)
    w3_lo = (w3p - w3_hi.astype(F32)).astype(BF16)
    return dict(bands=both(band_half), phase=both(phase_half), w1t=both(w1[0]), w1=blockdiag(w1_half),
                b1=both(b1), w2=blockdiag(w2), b2=both(b2), freq=both(freq), w3_hi=w3_hi, w3_lo=w3_lo,
                decay=decay.reshape(N_DIR, 1, cols))


def _trunk(x, p):
    bsz, seq, _ = x.shape
    n = bsz * seq
    cos_t, sin_t = _rope_tables(seq)
    x2 = x.reshape(n, D_MODEL)
    for l in range(DEPTH):
        uh, cq, ckvr = _inproj(x2, p["attn_norm"][l][None, :], p["w_in"][l])
        y_h = _hyena(uh.reshape(bsz, seq, COL_HYENA), p["conv_w"][l], p["conv_b"][l][None, :],
                     p["filt"][l], p["hyena_bias"][l])
        qt, k, vt = _qkv(cq.reshape(bsz, seq, -1), ckvr.reshape(bsz, seq, -1), cos_t, sin_t, p["attn"][l])
        y_a = _flash(qt, k, vt)
        xm, xn, aff_t, aff_n = _outproj(y_h.reshape(n, D_HYENA), y_a.reshape(n, D_ATTN), x2,
                                 p["out_norm"][l][None, :], p["w_out"][l], p["mlp_norm"][l][None, :],
                                 p["w_router"][l])
        x2 = _ec_moe(xm, xn, aff_t, aff_n, p["w_gate"][l], p["w_up"][l], p["w_down"][l])
    return x2.reshape(bsz, seq, D_MODEL)


def kernel(x_prompt, x_sample, attn_norm, w_in, conv_w, conv_b, filt_w1, filt_b1, filt_w2, filt_b2, filt_w3,
           filt_freq, filt_decay, hyena_bias, q_a_norm, w_uq, kv_a_norm, w_ukv, q_norm, k_norm, out_norm,
           w_out, mlp_norm, w_router, w_gate, w_up, w_down):
    p = dict(
        attn_norm=attn_norm,
        w_in=jnp.pad(w_in, ((0, 0), (0, 0), (0, D_IN_PAD - D_IN))).astype(BF16),
        conv_w=conv_w, conv_b=conv_b, hyena_bias=hyena_bias,
        filt=[_filter_weights(filt_w1[l], filt_b1[l], filt_w2[l], filt_b2[l], filt_w3[l], filt_freq[l],
                              filt_decay[l]) for l in range(DEPTH)],
        attn=[_attn_weights(q_a_norm[l], w_uq[l], kv_a_norm[l], w_ukv[l], q_norm[l], k_norm[l])
              for l in range(DEPTH)],
        out_norm=out_norm, w_out=w_out.astype(BF16), mlp_norm=mlp_norm,
        w_router=_hi_lo(jnp.pad(w_router, ((0, 0), (0, 0), (0, LANES - N_EXPERTS)))),
        w_gate=w_gate.astype(BF16), w_up=w_up.astype(BF16), w_down=w_down.astype(BF16))
    return (_trunk(x_prompt, p), _trunk(x_sample, p))
```

```python
import functools
import math

import jax
import jax.numpy as jnp
from jax import lax
from jax.experimental import pallas as pl
from jax.experimental.pallas import tpu as pltpu

F32 = jnp.float32
BF16 = jnp.bfloat16
I32 = jnp.int32
HIGHEST = lax.Precision.HIGHEST

D_MODEL = 1024
DEPTH = 2
D_HYENA = 512
HYENA_ORDER = 2
FILTER_BANDS = 16
FILTER_HIDDEN = 64
N_DIR = 2
DECAY_SHIFT = 0.05
N_HEADS = 8
QK_NOPE = 64
QK_ROPE = 32
QK_HEAD = QK_NOPE + QK_ROPE
V_HEAD = 64
V_AUG = V_HEAD + 16
D_ATTN = N_HEADS * V_HEAD
Q_LORA = 256
KV_LORA = 128
ROPE_THETA = 10000.0
N_EXPERTS = 16
EC_CAPACITY = 2
D_EXPERT = 512
EPS = 1e-6
COL_HYENA = (HYENA_ORDER + 1) * D_HYENA
COL_Q = COL_HYENA + Q_LORA
COL_KV = COL_Q + KV_LORA
D_IN = COL_KV + QK_ROPE
D_IN_PAD = 2048

LANES = 128
SLOT = 128
TOK_BLOCK = 256
SLOT_TILE = 256
ALIGN = 16
GATHER_ROWS = 64
COMBINE_FAST_ROWS = 128
ATT_TQ = 512
ATT_QC = 256
ATT_TK = 256
ATT_UNROLL = 16
MIB = 1024 * 1024


def _params(sem, vmem_mib):
    return pltpu.CompilerParams(dimension_semantics=sem, vmem_limit_bytes=vmem_mib * MIB)


def _rms(x):
    return x * lax.rsqrt(jnp.mean(x * x, axis=-1, keepdims=True) + EPS)


def _inproj_body(x_ref, g_ref, w_ref, uh_ref, cq_ref, ckvr_ref):
    xn = _rms(x_ref[...]) * g_ref[...]
    u = jnp.dot(xn.astype(BF16), w_ref[...], preferred_element_type=F32)
    uh_ref[...] = u[:, :COL_HYENA]
    cq_ref[...] = u[:, COL_HYENA:COL_Q]
    ckvr_ref[...] = u[:, COL_Q:]


def _inproj(x2d, g, w_pad):
    n = x2d.shape[0]
    tm = min(512, n)
    return pl.pallas_call(
        _inproj_body,
        grid=(n // tm,),
        in_specs=[pl.BlockSpec((tm, D_MODEL), lambda i: (i, 0)),
                  pl.BlockSpec((1, D_MODEL), lambda i: (0, 0)),
                  pl.BlockSpec((D_MODEL, D_IN_PAD), lambda i: (0, 0))],
        out_specs=[pl.BlockSpec((tm, COL_HYENA), lambda i: (i, 0)),
                   pl.BlockSpec((tm, Q_LORA), lambda i: (i, 0)),
                   pl.BlockSpec((tm, D_IN_PAD - COL_Q), lambda i: (i, 0))],
        out_shape=[jax.ShapeDtypeStruct((n, COL_HYENA), F32),
                   jax.ShapeDtypeStruct((n, Q_LORA), F32),
                   jax.ShapeDtypeStruct((n, D_IN_PAD - COL_Q), F32)],
        compiler_params=_params(("parallel",), 40),
        name="inproj",
    )(x2d, g, w_pad)


def _sconv_body(u_ref, prev_ref, next_ref, w_ref, b_ref, v_ref, x1_ref, x2_ref):
    i = pl.program_id(1)
    last = pl.num_programs(1) - 1
    u = u_ref[0]
    tl = u.shape[0]
    prev_row = jnp.where(i == 0, 0.0, prev_ref[0, 7:8, :])
    next_row = jnp.where(i == last, 0.0, next_ref[0, 0:1, :])
    row = lax.broadcasted_iota(I32, u.shape, 0)
    up = jnp.where(row == 0, prev_row, pltpu.roll(u, 1, axis=0))
    dn = jnp.where(row == tl - 1, next_row, pltpu.roll(u, tl - 1, axis=0))
    y = up * w_ref[0:1, :] + u * w_ref[1:2, :] + dn * w_ref[2:3, :] + b_ref[...]
    v_ref[0] = y[:, :D_HYENA]
    x1_ref[0] = y[:, D_HYENA:2 * D_HYENA]
    x2_ref[0] = y[:, 2 * D_HYENA:]


def _sconv(uh, w, b):
    bsz, seq, c = uh.shape
    tl = min(512, seq)
    r = tl // 8
    nblk8 = seq // 8
    out = jax.ShapeDtypeStruct((bsz, seq, D_HYENA), F32)
    ospec = pl.BlockSpec((1, tl, D_HYENA), lambda bi, i: (bi, i, 0))
    return pl.pallas_call(
        _sconv_body,
        grid=(bsz, seq // tl),
        in_specs=[pl.BlockSpec((1, tl, c), lambda bi, i: (bi, i, 0)),
                  pl.BlockSpec((1, 8, c), lambda bi, i: (bi, jnp.maximum(i * r - 1, 0), 0)),
                  pl.BlockSpec((1, 8, c), lambda bi, i: (bi, jnp.minimum((i + 1) * r, nblk8 - 1), 0)),
                  pl.BlockSpec((3, c), lambda bi, i: (0, 0)),
                  pl.BlockSpec((1, c), lambda bi, i: (0, 0))],
        out_specs=[ospec, ospec, ospec],
        out_shape=[out, out, out],
        compiler_params=_params(("parallel", "parallel"), 40),
        name="sconv",
    )(uh, uh, uh, w, b)


def _filter_body(bands_ref, phase_ref, w1t_ref, w1_ref, b1_ref, w2_ref, b2_ref, fr_ref, w3h_ref, w3l_ref,
                 dec_ref, k_ref, sum_ref, *, seq, n2):
    i = pl.program_id(0)
    hp = k_ref.shape[0] // 2
    n_lo = lax.broadcasted_iota(I32, (hp, 1), 0) * n2 + i
    n_hi = n_lo + seq
    t_of = lambda n: jnp.where(n < seq, n, 2 * seq - n).astype(F32)
    lane = lax.broadcasted_iota(I32, (hp, LANES), 1)
    t_idx = jnp.where(lane < FILTER_HIDDEN, t_of(n_lo), t_of(n_hi))
    feats = jnp.sin(2.0 * math.pi * bands_ref[...] * t_idx / seq + phase_ref[...])
    fr = fr_ref[...]
    pre = (t_idx / (seq - 1)) * w1t_ref[...] + jnp.dot(feats, w1_ref[...], precision=HIGHEST,
                                                      preferred_element_type=F32)
    h = jnp.sin(fr * (pre + b1_ref[...]))
    h = jnp.sin(fr * (jnp.dot(h, w2_ref[...], precision=HIGHEST, preferred_element_type=F32) + b2_ref[...]))
    h_hi = h.astype(BF16)
    h_lo = (h - h_hi.astype(F32)).astype(BF16)

    @pl.when(i == 0)
    def _():
        sum_ref[...] = jnp.zeros_like(sum_ref)

    for j, n in enumerate((n_lo, n_hi)):
        out = (jnp.dot(h_hi, w3h_ref[j], preferred_element_type=F32)
               + jnp.dot(h_lo, w3h_ref[j], preferred_element_type=F32)
               + jnp.dot(h_hi, w3l_ref[j], preferred_element_type=F32))
        window = jnp.exp(-(t_of(n) / (seq - 1)) * dec_ref[j]) + DECAY_SHIFT
        k = jnp.where(n == seq, 0.0, out * window)
        k_ref[j * hp:(j + 1) * hp, :] = k.astype(BF16)
        sum_ref[...] += jnp.sum(jnp.abs(k), axis=0, keepdims=True)


def _filters(seq, n1, n2, fw):
    cols = HYENA_ORDER * D_HYENA
    const = lambda shape: pl.BlockSpec(shape, lambda i: (0,) * len(shape))
    return pl.pallas_call(
        functools.partial(_filter_body, seq=seq, n2=n2),
        grid=(n2,),
        in_specs=[const((1, LANES)), const((1, LANES)), const((1, LANES)), const((LANES, LANES)),
                  const((1, LANES)), const((LANES, LANES)), const((1, LANES)), const((1, LANES)),
                  const((N_DIR, LANES, cols)), const((N_DIR, LANES, cols)), const((N_DIR, 1, cols))],
        out_specs=[pl.BlockSpec((n1, cols), lambda i: (0, i)),
                   pl.BlockSpec((1, cols), lambda i: (0, 0))],
        out_shape=[jax.ShapeDtypeStruct((n1, n2 * cols), BF16),
                   jax.ShapeDtypeStruct((1, cols), F32)],
        compiler_params=_params(("arbitrary",), 32),
        name="filter_gen",
    )(fw["bands"], fw["phase"], fw["w1t"], fw["w1"], fw["b1"], fw["w2"], fw["b2"], fw["freq"],
      fw["w3_hi"], fw["w3_lo"], fw["decay"])


def _fft_dims(seq):
    n2 = 128 if 2 * seq >= 32768 else 64
    n1 = 2 * seq // n2
    return n1, n2


def _dft_tables(n1, n2):
    n = n1 * n2
    n1h = n1 // 2
    k1 = jnp.arange(n1h, dtype=I32)[:, None]
    m1 = jnp.arange(n1, dtype=I32)[None, :]
    ang = (2.0 * math.pi / n1) * ((k1 * m1) % n1).astype(F32)
    top = jnp.cos(ang)
    bot = -jnp.sin(ang)
    nyq = jnp.where(m1 % 2 == 0, 1.0, -1.0).astype(F32)
    bot = jnp.concatenate([nyq, bot[1:]], axis=0)
    fa = jnp.concatenate([top, bot], axis=0)
    weight = jnp.where((jnp.arange(n1) % n1h) == 0, 1.0, 2.0).astype(F32) / n
    fi = (fa[:, :n1h] * weight[:, None]).T

    kk = jnp.arange(n1h + 1, dtype=I32)[:, None, None]
    k2 = jnp.arange(n2, dtype=I32)[None, :, None]
    m2 = jnp.arange(n2, dtype=I32)[None, None, :]
    phi = (2.0 * math.pi / n) * ((m2 * (kk + n1 * k2)) % n).astype(F32)
    gr = jnp.cos(phi)
    gi = -jnp.sin(phi)
    blk = jnp.concatenate([jnp.concatenate([gr, -gi], axis=2),
                           jnp.concatenate([gi, gr], axis=2)], axis=1)
    left = (jnp.arange(2 * n2) < n2)[None, :]
    g0 = jnp.where(left, blk[0], 0.0)
    gf = jnp.concatenate([g0[None], blk[1:n1h]], axis=0)
    gnyq = jnp.concatenate([jnp.zeros((2 * n2, n2), F32), blk[n1h][:, :n2]], axis=1)
    return dict(fa_full=fa.astype(BF16), fa_half=fa[:, :n1h].astype(BF16), fi=fi.astype(BF16),
                gf=gf.astype(BF16), gnyq=gnyq.astype(BF16),
                ginv=jnp.transpose(gf, (0, 2, 1)).astype(BF16), m2=gnyq.T.astype(BF16))


def _fft_a_body(x_ref, f_ref, o_ref):
    a = jnp.dot(f_ref[...], x_ref[0].astype(BF16), preferred_element_type=F32)
    n1h = o_ref.shape[2]
    o_ref[0, 0] = a[:n1h].astype(BF16)
    o_ref[0, 1] = a[n1h:].astype(BF16)


def _fft_a(x3, fmat):
    bsz, r, nc = x3.shape
    n1 = fmat.shape[0]
    tn = min(8192, nc)
    return pl.pallas_call(
        _fft_a_body,
        grid=(bsz, nc // tn),
        in_specs=[pl.BlockSpec((1, r, tn), lambda b, j: (b, 0, j)),
                  pl.BlockSpec((n1, r), lambda b, j: (0, 0))],
        out_specs=pl.BlockSpec((1, 2, n1 // 2, tn), lambda b, j: (b, 0, 0, j)),
        out_shape=jax.ShapeDtypeStruct((bsz, 2, n1 // 2, nc), BF16),
        compiler_params=_params(("parallel", "parallel"), 48),
        name="fft_stage_a",
    )(x3, fmat)


def _fft_b_body(ar_ref, ai_ref, g_ref, gn_ref, sc_ref, x_ref, xn_ref, *, kb):
    j = pl.program_id(2)
    inv = 1.0 / sc_ref[...]
    for kk in range(kb):
        rhs = jnp.concatenate([ar_ref[0, 0, kk], ai_ref[0, 0, kk]], axis=0)
        x_ref[0, kk] = (jnp.dot(g_ref[kk], rhs, preferred_element_type=F32) * inv).astype(BF16)

    @pl.when(j == 0)
    def _():
        rhs = jnp.concatenate([ar_ref[0, 0, 0], ai_ref[0, 0, 0]], axis=0)
        xn_ref[0] = (jnp.dot(gn_ref[...], rhs, preferred_element_type=F32) * inv).astype(BF16)


def _fft_b(a5, tabs, scale):
    bsz, _, n1h, n2, c = a5.shape
    kb = 8
    tc = min(512, c)
    return pl.pallas_call(
        functools.partial(_fft_b_body, kb=kb),
        grid=(bsz, c // tc, n1h // kb),
        in_specs=[pl.BlockSpec((1, 1, kb, n2, tc), lambda b, ci, j: (b, 0, j, 0, ci)),
                  pl.BlockSpec((1, 1, kb, n2, tc), lambda b, ci, j: (b, 1, j, 0, ci)),
                  pl.BlockSpec((kb, 2 * n2, 2 * n2), lambda b, ci, j: (j, 0, 0)),
                  pl.BlockSpec((2 * n2, 2 * n2), lambda b, ci, j: (0, 0)),
                  pl.BlockSpec((1, tc), lambda b, ci, j: (0, ci))],
        out_specs=[pl.BlockSpec((1, kb, 2 * n2, tc), lambda b, ci, j: (b, j, 0, ci)),
                   pl.BlockSpec((1, 2 * n2, tc), lambda b, ci, j: (b, 0, ci))],
        out_shape=[jax.ShapeDtypeStruct((bsz, n1h, 2 * n2, c), BF16),
                   jax.ShapeDtypeStruct((bsz, 2 * n2, c), BF16)],
        compiler_params=_params(("parallel", "parallel", "arbitrary"), 40),
        name="fft_stage_b",
    )(a5, a5, tabs["gf"], tabs["gnyq"], scale)


def _cmul(x, k, n2):
    xr, xi = x[:n2], x[n2:]
    kr, ki = k[:n2], k[n2:]
    return jnp.concatenate([xr * kr - xi * ki, xr * ki + xi * kr], axis=0).astype(BF16)


def _spec_body(ar_ref, ai_ref, k_ref, kn_ref, g_ref, gn_ref, gi_ref, m2_ref, o_ref, *, kb):
    j = pl.program_id(1)
    n2 = o_ref.shape[3]

    def through(fwd, a_re, a_im, kf, inv):
        x = jnp.dot(fwd, jnp.concatenate([a_re, a_im], axis=0), preferred_element_type=F32)
        return jnp.dot(inv, _cmul(x, kf.astype(F32), n2), preferred_element_type=F32)

    for kk in range(kb):
        acc = through(g_ref[kk], ar_ref[0, 0, kk], ai_ref[0, 0, kk], k_ref[0, kk], gi_ref[kk])
        o_ref[0, 0, kk] = acc[:n2].astype(BF16)
        o_ref[0, 1, kk] = acc[n2:].astype(BF16)

    @pl.when(j == 0)
    def _():
        acc = through(gn_ref[...], ar_ref[0, 0, 0], ai_ref[0, 0, 0], kn_ref[0], m2_ref[...])
        o_ref[0, 1, 0] = acc[n2:].astype(BF16)


def _spec_conv(a5, kf, kfnyq, order, tabs):
    bsz, _, n1h, n2, c = a5.shape
    tn2 = 2 * n2
    kb = 8
    return pl.pallas_call(
        functools.partial(_spec_body, kb=kb),
        grid=(bsz, n1h // kb),
        in_specs=[pl.BlockSpec((1, 1, kb, n2, c), lambda b, j: (b, 0, j, 0, 0)),
                  pl.BlockSpec((1, 1, kb, n2, c), lambda b, j: (b, 1, j, 0, 0)),
                  pl.BlockSpec((1, kb, tn2, c), lambda b, j: (0, j, 0, order)),
                  pl.BlockSpec((1, tn2, c), lambda b, j: (0, 0, order)),
                  pl.BlockSpec((kb, tn2, tn2), lambda b, j: (j, 0, 0)),
                  pl.BlockSpec((tn2, tn2), lambda b, j: (0, 0)),
                  pl.BlockSpec((kb, tn2, tn2), lambda b, j: (j, 0, 0)),
                  pl.BlockSpec((tn2, tn2), lambda b, j: (0, 0))],
        out_specs=pl.BlockSpec((1, 2, kb, n2, c), lambda b, j: (b, 0, j, 0, 0)),
        out_shape=jax.ShapeDtypeStruct((bsz, 2, n1h, n2, c), BF16),
        compiler_params=_params(("parallel", "arbitrary"), 48),
        name="spectral_conv",
    )(a5, a5, kf, kfnyq, tabs["gf"], tabs["gnyq"], tabs["ginv"], tabs["m2"])


def _ifft_a_body(b_ref, f_ref, z_ref, gate_ref, bias_ref, o_ref):
    y = jnp.dot(f_ref[...], b_ref[0], preferred_element_type=F32)
    o_ref[0] = gate_ref[0] * (y + z_ref[0] * bias_ref[...])


def _ifft_a(b3, fi, z3, gate3, bias_t):
    bsz, n1, nc = b3.shape
    n1h = n1 // 2
    tn = bias_t.shape[1]
    return pl.pallas_call(
        _ifft_a_body,
        grid=(bsz, nc // tn),
        in_specs=[pl.BlockSpec((1, n1, tn), lambda b, j: (b, 0, j)),
                  pl.BlockSpec((n1h, n1), lambda b, j: (0, 0)),
                  pl.BlockSpec((1, n1h, tn), lambda b, j: (b, 0, j)),
                  pl.BlockSpec((1, n1h, tn), lambda b, j: (b, 0, j)),
                  pl.BlockSpec((1, tn), lambda b, j: (0, 0))],
        out_specs=pl.BlockSpec((1, n1h, tn), lambda b, j: (b, 0, j)),
        out_shape=jax.ShapeDtypeStruct((bsz, n1h, nc), F32),
        compiler_params=_params(("parallel", "parallel"), 48),
        name="ifft_stage_a",
    )(b3, fi, z3, gate3, bias_t)


def _hyena(uh, conv_w, conv_b, fw, bias):
    bsz, seq, _ = uh.shape
    c = D_HYENA
    n1, n2 = _fft_dims(seq)
    n1h = n1 // 2
    tabs = _dft_tables(n1, n2)
    v, x1, x2 = _sconv(uh, conv_w, conv_b)

    kcirc, ksum = _filters(seq, n1, n2, fw)
    cols = HYENA_ORDER * c
    ka = _fft_a(kcirc[None], tabs["fa_full"])
    kf, kfnyq = _fft_b(ka.reshape(1, 2, n1h, n2, cols), tabs, ksum)

    tn = min(8192, n2 * c)
    z = v
    for order, gate in enumerate((x1, x2)):
        za = _fft_a(z.reshape(bsz, n1h, n2 * c), tabs["fa_half"])
        zb = _spec_conv(za.reshape(bsz, 2, n1h, n2, c), kf, kfnyq, order, tabs)
        bias_t = jnp.tile(bias[order][None, :], (1, tn // c))
        z = _ifft_a(zb.reshape(bsz, n1, n2 * c), tabs["fi"], z.reshape(bsz, n1h, n2 * c),
                    gate.reshape(bsz, n1h, n2 * c), bias_t).reshape(bsz, seq, c)
    return z


def _qkv_body(cq_ref, ckvr_ref, cos_ref, sin_ref, wq_ref, wk_ref, wv_ref, qa_ref, kva_ref, qg_ref, kg_ref,
              qt_ref, k_ref, vt_ref):
    cqn = _rms(cq_ref[0]) * qa_ref[...]
    qs = jnp.dot(cqn.astype(BF16), wq_ref[...], preferred_element_type=F32)
    ck = ckvr_ref[0]
    ckvn = (_rms(ck[:, :KV_LORA]) * kva_ref[...]).astype(BF16)
    ks = jnp.dot(ckvn, wk_ref[...], preferred_element_type=F32)
    vs = jnp.dot(ckvn, wv_ref[...], preferred_element_type=F32)
    krope = pltpu.roll(ck[:, KV_LORA:], QK_NOPE, axis=1)
    cos = cos_ref[...]
    sin = sin_ref[...]
    lane = lax.broadcasted_iota(I32, cos.shape, 1)
    half = QK_ROPE // 2
    first = (lane >= QK_NOPE) & (lane < QK_NOPE + half)
    second = (lane >= QK_NOPE + half) & (lane < QK_HEAD)

    def head(xh, gain, scale):
        ms = jnp.sum(xh * xh, axis=-1, keepdims=True) * (1.0 / QK_HEAD)
        xn = xh * lax.rsqrt(ms + EPS) * gain
        rot = jnp.where(first, -pltpu.roll(xn, SLOT - half, axis=1),
                        jnp.where(second, pltpu.roll(xn, half, axis=1), 0.0))
        return (xn * cos + rot * sin) * scale

    q_scale = QK_HEAD ** -0.5 * math.log2(math.e)
    for h in range(N_HEADS):
        sl = slice(h * SLOT, (h + 1) * SLOT)
        qt_ref[0, h] = head(qs[:, sl], qg_ref[...], q_scale).T.astype(BF16)
        k_ref[0, h] = head(ks[:, sl] + krope, kg_ref[...], 1.0).astype(BF16)
    aug = lax.broadcasted_iota(I32, (V_AUG - V_HEAD, cos.shape[0]), 0)
    ones_row = jnp.where(aug == 0, 1.0, 0.0)
    for hp in range(N_HEADS // 2):
        pair_t = vs[:, hp * 2 * V_HEAD:(hp + 1) * 2 * V_HEAD].T
        vt_ref[0, hp] = jnp.concatenate(
            [pair_t[:V_HEAD], ones_row, pair_t[V_HEAD:], ones_row], axis=0).astype(BF16)


def _qkv(cq, ckvr, cos_t, sin_t, aw):
    bsz, seq, _ = cq.shape
    tm = min(512, seq)
    const = lambda shape: pl.BlockSpec(shape, lambda b, i: (0,) * len(shape))
    return pl.pallas_call(
        _qkv_body,
        grid=(bsz, seq // tm),
        in_specs=[pl.BlockSpec((1, tm, Q_LORA), lambda b, i: (b, i, 0)),
                  pl.BlockSpec((1, tm, D_IN_PAD - COL_Q), lambda b, i: (b, i, 0)),
                  pl.BlockSpec((tm, SLOT), lambda b, i: (i, 0)),
                  pl.BlockSpec((tm, SLOT), lambda b, i: (i, 0)),
                  const((Q_LORA, N_HEADS * SLOT)), const((KV_LORA, N_HEADS * SLOT)),
                  const((KV_LORA, D_ATTN)),
                  const((1, Q_LORA)), const((1, KV_LORA)), const((1, SLOT)), const((1, SLOT))],
        out_specs=[pl.BlockSpec((1, N_HEADS, SLOT, tm), lambda b, i: (b, 0, 0, i)),
                   pl.BlockSpec((1, N_HEADS, tm, SLOT), lambda b, i: (b, 0, i, 0)),
                   pl.BlockSpec((1, N_HEADS // 2, 2 * V_AUG, tm), lambda b, i: (b, 0, 0, i))],
        out_shape=[jax.ShapeDtypeStruct((bsz, N_HEADS, SLOT, seq), BF16),
                   jax.ShapeDtypeStruct((bsz, N_HEADS, seq, SLOT), BF16),
                   jax.ShapeDtypeStruct((bsz, N_HEADS // 2, 2 * V_AUG, seq), BF16)],
        compiler_params=_params(("parallel", "parallel"), 40),
        name="qkv_prep",
    )(cq, ckvr, cos_t, sin_t, aw["wq"], aw["wk"], aw["wv"], aw["qa"], aw["kva"], aw["qg"], aw["kg"])


def _flash_body(qt_ref, k_ref, vt_ref, o_ref, s0_ref, s1_ref, acc_ref, *, tk, nk, nqc, unroll):
    chains = [(hh, qc) for hh in range(2) for qc in range(nqc)]
    acc_ref[...] = jnp.zeros(acc_ref.shape, F32)

    def qk(t, s_ref):
        ks = pl.multiple_of(t * tk, tk)
        tile_max = []
        for c, (hh, qc) in enumerate(chains):
            k = k_ref[0, hh, pl.ds(ks, tk), :]
            s = jnp.dot(k, qt_ref[0, hh, :, qc * ATT_QC:(qc + 1) * ATT_QC], preferred_element_type=F32)
            s_ref[c] = s
            tile_max.append(jnp.max(s, axis=0, keepdims=True))
        return tile_max

    def softmax_pv(t, s_ref, tile_max, m):
        ks = pl.multiple_of(t * tk, tk)
        m_out = []
        for c, (hh, qc) in enumerate(chains):
            m_new = jnp.maximum(m[c], tile_max[c])
            a = jnp.exp2(m[c] - m_new)
            p = jnp.exp2(s_ref[c] - m_new)
            m_out.append(m_new)
            vt = vt_ref[0, 0, hh * V_AUG:(hh + 1) * V_AUG, pl.ds(ks, tk)]
            acc_ref[c] = acc_ref[c] * a + jnp.dot(vt, p.astype(BF16), preferred_element_type=F32)
        return m_out

    def group(u, carry):
        tile_max, m = carry
        for i in range(0, unroll, 2):
            t = unroll * u + i
            tile_max1 = qk(t + 1, s1_ref)
            m = softmax_pv(t, s0_ref, tile_max, m)
            tile_max = qk(jnp.minimum(t + 2, nk - 1), s0_ref)
            m = softmax_pv(t + 1, s1_ref, tile_max1, m)
        return tile_max, m

    init = (qk(0, s0_ref), [jnp.full((1, ATT_QC), -jnp.inf, F32)] * len(chains))
    lax.fori_loop(0, nk // unroll, group, init)

    def normalised(c):
        acc = acc_ref[c]
        return acc[:V_HEAD] * (1.0 / acc[V_HEAD:V_HEAD + 1])

    heads = [jnp.concatenate([normalised(hh * nqc + qc) for qc in range(nqc)], axis=1) for hh in range(2)]
    o_ref[0] = jnp.concatenate(heads, axis=0).T


def _flash(qt, k, vt):
    bsz, nh, _, seq = qt.shape
    tq = min(ATT_TQ, seq)
    tk = min(ATT_TK, seq)
    nqc = tq // ATT_QC
    nk = seq // tk
    unroll = min(ATT_UNROLL, nk)
    assert nk % unroll == 0 and unroll % 2 == 0
    return pl.pallas_call(
        functools.partial(_flash_body, tk=tk, nk=nk, nqc=nqc, unroll=unroll),
        grid=(bsz, nh // 2, seq // tq),
        in_specs=[pl.BlockSpec((1, 2, SLOT, tq), lambda b, hp, i: (b, hp, 0, i)),
                  pl.BlockSpec((1, 2, seq, SLOT), lambda b, hp, i: (b, hp, 0, 0)),
                  pl.BlockSpec((1, 1, 2 * V_AUG, seq), lambda b, hp, i: (b, hp, 0, 0))],
        out_specs=pl.BlockSpec((1, tq, 2 * V_HEAD), lambda b, hp, i: (b, i, hp)),
        out_shape=jax.ShapeDtypeStruct((bsz, seq, nh * V_HEAD), F32),
        scratch_shapes=[pltpu.VMEM((2 * nqc, tk, ATT_QC), F32), pltpu.VMEM((2 * nqc, tk, ATT_QC), F32),
                        pltpu.VMEM((2 * nqc, V_AUG, ATT_QC), F32)],
        compiler_params=_params(("parallel", "parallel", "parallel"), 48),
        name="flash_attn",
    )(qt, k, vt)


def _outproj_body(yh_ref, ya_ref, x_ref, og_ref, w_ref, mg_ref, wr_ref, xm_ref, xn_ref, aff_ref, affn_ref):
    og = og_ref[...]
    half = D_MODEL // 2
    y = jnp.concatenate([_rms(yh_ref[...]) * og[:, :half], _rms(ya_ref[...]) * og[:, half:]], axis=1)
    xm = x_ref[...] + jnp.dot(y.astype(BF16), w_ref[...], preferred_element_type=F32)
    xm_ref[...] = xm
    xn = _rms(xm) * mg_ref[...]
    xn_hi = xn.astype(BF16)
    xn_ref[...] = xn_hi
    xn_lo = (xn - xn_hi.astype(F32)).astype(BF16)
    logits = (jnp.dot(xn_hi, wr_ref[0], preferred_element_type=F32)
              + jnp.dot(xn_lo, wr_ref[0], preferred_element_type=F32)
              + jnp.dot(xn_hi, wr_ref[1], preferred_element_type=F32))
    lane = lax.broadcasted_iota(I32, logits.shape, 1)
    logits = jnp.where(lane < N_EXPERTS, logits, -jnp.inf)
    e = jnp.exp(logits - jnp.max(logits, axis=-1, keepdims=True))
    aff = e / jnp.sum(e, axis=-1, keepdims=True)
    aff_ref[...] = aff.T[:N_EXPERTS]
    affn_ref[...] = aff[:, :N_EXPERTS]


def _outproj(yh, ya, x2d, og, w_out, mg, wr_pad):
    n = x2d.shape[0]
    tm = min(512, n)
    half = D_MODEL // 2
    const = lambda shape: pl.BlockSpec(shape, lambda i: (0,) * len(shape))
    return pl.pallas_call(
        _outproj_body,
        grid=(n // tm,),
        in_specs=[pl.BlockSpec((tm, half), lambda i: (i, 0)),
                  pl.BlockSpec((tm, half), lambda i: (i, 0)),
                  pl.BlockSpec((tm, D_MODEL), lambda i: (i, 0)),
                  const((1, D_MODEL)), const((D_MODEL, D_MODEL)), const((1, D_MODEL)),
                  const((2, D_MODEL, LANES))],
        out_specs=[pl.BlockSpec((tm, D_MODEL), lambda i: (i, 0)),
                   pl.BlockSpec((tm, D_MODEL), lambda i: (i, 0)),
                   pl.BlockSpec((N_EXPERTS, tm), lambda i: (0, i)),
                   pl.BlockSpec((tm, N_EXPERTS), lambda i: (i, 0))],
        out_shape=[jax.ShapeDtypeStruct((n, D_MODEL), F32),
                   jax.ShapeDtypeStruct((n, D_MODEL), BF16),
                   jax.ShapeDtypeStruct((N_EXPERTS, n), F32),
                   jax.ShapeDtypeStruct((n, N_EXPERTS), F32)],
        compiler_params=_params(("parallel",), 40),
        name="outproj_router",
    )(yh, ya, x2d, og, w_out, mg, wr_pad)


def _select_body(aff_ref, upper_ref, lower_ref, pos_ref, off_ref, *, cap):
    group = aff_ref.shape[0]
    bits = [pltpu.bitcast(aff_ref[x], I32) for x in range(group)]
    upper = upper_ref[...]
    lower = lower_ref[...]

    def count(mask):
        return jnp.sum(jnp.sum(mask.astype(F32), axis=1, keepdims=True), axis=0, keepdims=True)

    def bit_step(i, thr):
        bit = jnp.left_shift(jnp.int32(1), 30 - i)
        return tuple(jnp.where(count(bits[x] >= (thr[x] | bit)) >= cap, thr[x] | bit, thr[x])
                     for x in range(group))

    thr = lax.fori_loop(0, 31, bit_step, tuple(jnp.zeros((1, 1), I32) for _ in range(group)))

    def prefix(mask):
        within = jnp.dot(mask.astype(BF16), upper, preferred_element_type=F32)
        total = within[:, LANES - 1:LANES]
        offs = jnp.dot(lower, jnp.broadcast_to(total, within.shape).astype(BF16), preferred_element_type=F32)
        return within, offs

    for x in range(group):
        gt = bits[x] > thr[x]
        eq = bits[x] == thr[x]
        need = cap - count(gt)
        w_eq, o_eq = prefix(eq)
        sel = gt | (eq & (o_eq + w_eq <= need))
        w_sel, o_sel = prefix(sel)
        pos_ref[x] = jnp.where(sel, o_sel + w_sel - 1.0, -1.0).astype(I32)
        off_ref[x] = o_sel.astype(I32)


def _select(aff_t, cap):
    ne, n = aff_t.shape
    rows = n // LANES
    upper = (jnp.arange(LANES)[:, None] <= jnp.arange(LANES)[None, :]).astype(BF16)
    lower = (jnp.arange(rows)[None, :] < jnp.arange(rows)[:, None]).astype(BF16)
    group = 4
    blk = pl.BlockSpec((group, rows, LANES), lambda e: (e, 0, 0))
    pos, off = pl.pallas_call(
        functools.partial(_select_body, cap=cap),
        grid=(ne // group,),
        in_specs=[blk, pl.BlockSpec((LANES, LANES), lambda e: (0, 0)),
                  pl.BlockSpec((rows, rows), lambda e: (0, 0))],
        out_specs=[blk, blk],
        out_shape=[jax.ShapeDtypeStruct((ne, rows, LANES), I32),
                   jax.ShapeDtypeStruct((ne, rows, LANES), I32)],
        compiler_params=_params(("parallel",), 32),
        name="ec_select",
    )(aff_t.reshape(ne, rows, LANES), upper, lower)
    return pos.reshape(ne, n), off[:, :, 0]


def _ffn_body(off_ref, nch_ref, pos_ref, x_ref, wg_ref, wu_ref, wd_ref, y_ref, xs_ref, *, nb, sub, cap):
    e = pl.program_id(0)
    sb = pl.program_id(1)

    @pl.when(sb == 0)
    def _():
        xs_ref[...] = jnp.zeros(xs_ref.shape, BF16)

    row = lax.broadcasted_iota(I32, (GATHER_ROWS, TOK_BLOCK), 0)

    def gather(i, c):
        blk = e * nb + sb * sub + i
        base = off_ref[blk] * ALIGN
        tok = slice(i * TOK_BLOCK, (i + 1) * TOK_BLOCK)
        start = pl.multiple_of(base + c * GATHER_ROWS, ALIGN)
        hit = (pos_ref[0, :, tok] - start) == row
        win = jnp.dot(hit.astype(BF16), x_ref[tok, :], preferred_element_type=F32)
        xs_ref[pl.ds(start, GATHER_ROWS), :] += win.astype(BF16)

    for i in range(sub):
        gather(i, 0)
    for i in range(sub):
        lax.fori_loop(1, nch_ref[e * nb + sb * sub + i], lambda c, carry, i=i: (gather(i, c), carry)[1], 0)

    @pl.when(sb == pl.num_programs(1) - 1)
    def _():
        for j in range(cap // SLOT_TILE):
            rows = slice(j * SLOT_TILE, (j + 1) * SLOT_TILE)
            xt = xs_ref[rows, :]
            g = jnp.dot(xt, wg_ref[0], preferred_element_type=F32)
            u = jnp.dot(xt, wu_ref[0], preferred_element_type=F32)
            h = (g * jax.nn.sigmoid(g) * u).astype(BF16)
            y_ref[0, rows, :] = jnp.dot(h, wd_ref[0], preferred_element_type=F32).astype(BF16)
        y_ref[0, cap:, :] = jnp.zeros((y_ref.shape[1] - cap, D_MODEL), BF16)


def _ffn(base, nchunk, pos3, xn, wg, wu, wd, cap):
    ne = wg.shape[0]
    n = xn.shape[0]
    nb = n // TOK_BLOCK
    sub = min(8, nb)
    cap_pad = cap + TOK_BLOCK + ALIGN
    grid_spec = pltpu.PrefetchScalarGridSpec(
        num_scalar_prefetch=2,
        grid=(ne, nb // sub),
        in_specs=[pl.BlockSpec((1, 1, sub * TOK_BLOCK), lambda e, s, off, nch: (e, 0, s)),
                  pl.BlockSpec((sub * TOK_BLOCK, D_MODEL), lambda e, s, off, nch: (s, 0)),
                  pl.BlockSpec((1, D_MODEL, D_EXPERT), lambda e, s, off, nch: (e, 0, 0)),
                  pl.BlockSpec((1, D_MODEL, D_EXPERT), lambda e, s, off, nch: (e, 0, 0)),
                  pl.BlockSpec((1, D_EXPERT, D_MODEL), lambda e, s, off, nch: (e, 0, 0))],
        out_specs=pl.BlockSpec((1, cap_pad, D_MODEL), lambda e, s, off, nch: (e, 0, 0)),
        scratch_shapes=[pltpu.VMEM((cap_pad, D_MODEL), BF16)])
    return pl.pallas_call(
        functools.partial(_ffn_body, nb=nb, sub=sub, cap=cap),
        grid_spec=grid_spec,
        out_shape=jax.ShapeDtypeStruct((ne, cap_pad, D_MODEL), BF16),
        compiler_params=_params(("parallel", "arbitrary"), 48),
        name="ec_ffn",
    )(base, nchunk, pos3, xn, wg, wu, wd)


def _combine_body(off_ref, spill_ref, post_ref, affn_ref, xm_ref, *refs, nb, ne, win_rows):
    win_refs, tail_refs, o_ref = refs[:ne], refs[ne:-1], refs[-1]
    b = pl.program_id(0)
    post = post_ref[...]
    affn = affn_ref[...]

    def expand(y_refs, shift):
        width = y_refs[0].shape[0]
        lane = lax.broadcasted_iota(I32, (TOK_BLOCK, width), 1)
        total = jnp.zeros((TOK_BLOCK, D_MODEL), F32)
        for e in range(ne):
            rel = post[:, e:e + 1] - (off_ref[e * nb + b] * ALIGN + shift)
            hit = (rel == lane).astype(BF16)
            total = total + affn[:, e:e + 1] * jnp.dot(hit, y_refs[e][...], preferred_element_type=F32)
        return total

    o_ref[...] = xm_ref[...] + expand(win_refs, 0)

    if tail_refs:
        @pl.when(spill_ref[b] != 0)
        def _():
            o_ref[...] += expand(tail_refs, win_rows)


def _combine(base, spill, pos_t, aff_n, xm, y, win_rows):
    ne = y.shape[0]
    n = xm.shape[0]
    nb = n // TOK_BLOCK
    tail_rows = ALIGN if win_rows == TOK_BLOCK else 0

    def window(e, rows, shift):
        return pl.BlockSpec((pl.Squeezed(), pl.Element(rows), pl.Element(D_MODEL)),
                            lambda b, off, sp: (e, (off[e * nb + b] + shift // ALIGN) * ALIGN, 0))

    tails = [window(e, tail_rows, win_rows) for e in range(ne)] if tail_rows else []
    grid_spec = pltpu.PrefetchScalarGridSpec(
        num_scalar_prefetch=2,
        grid=(nb,),
        in_specs=([pl.BlockSpec((TOK_BLOCK, ne), lambda b, off, sp: (b, 0)),
                   pl.BlockSpec((TOK_BLOCK, ne), lambda b, off, sp: (b, 0)),
                   pl.BlockSpec((TOK_BLOCK, D_MODEL), lambda b, off, sp: (b, 0))]
                  + [window(e, win_rows, 0) for e in range(ne)] + tails),
        out_specs=pl.BlockSpec((TOK_BLOCK, D_MODEL), lambda b, off, sp: (b, 0)))
    return pl.pallas_call(
        functools.partial(_combine_body, nb=nb, ne=ne, win_rows=win_rows),
        grid_spec=grid_spec,
        out_shape=jax.ShapeDtypeStruct((n, D_MODEL), F32),
        compiler_params=_params(("parallel",), 48),
        name="ec_combine",
    )(base, spill, pos_t, aff_n, xm, *([y] * (ne + len(tails))))


def _ec_moe(xm, xn, aff_t, aff_n, wg, wu, wd):
    ne, n = aff_t.shape
    cap = max(1, EC_CAPACITY * n // N_EXPERTS)
    nb = n // TOK_BLOCK
    pos, off = _select(aff_t, cap)
    first = off[:, ::TOK_BLOCK // LANES]
    count = jnp.concatenate([first[:, 1:], jnp.full((ne, 1), cap, I32)], axis=1) - first
    base = first // ALIGN
    span = jnp.where(count > 0, first - base * ALIGN + count, 0)
    spill = jnp.any(span > TOK_BLOCK, axis=0).astype(I32)
    nchunk = ((span + GATHER_ROWS - 1) // GATHER_ROWS).reshape(ne * nb).astype(I32)
    base = base.reshape(ne * nb).astype(I32)
    y = _ffn(base, nchunk, pos.reshape(ne, 1, n), xn, wg, wu, wd, cap)
    pos_t = pos.T
    return lax.cond(jnp.max(span) <= COMBINE_FAST_ROWS,
                    lambda: _combine(base, spill, pos_t, aff_n, xm, y, COMBINE_FAST_ROWS),
                    lambda: _combine(base, spill, pos_t, aff_n, xm, y, TOK_BLOCK))


def _slot_cols(w, head_width):
    k = w.shape[0]
    w3 = w.reshape(k, N_HEADS, head_width)
    return jnp.pad(w3, ((0, 0), (0, 0), (0, SLOT - head_width))).reshape(k, N_HEADS * SLOT)


def _attn_weights(q_a_norm, w_uq, kv_a_norm, w_ukv, q_norm, k_norm):
    wkv = w_ukv.reshape(KV_LORA, N_HEADS, QK_NOPE + V_HEAD)
    wk = _slot_cols(wkv[:, :, :QK_NOPE].reshape(KV_LORA, -1), QK_NOPE)
    wv = wkv[:, :, QK_NOPE:].reshape(KV_LORA, D_ATTN)
    pad_gain = lambda g: jnp.pad(g, (0, SLOT - QK_HEAD))[None, :]
    return dict(wq=_slot_cols(w_uq, QK_HEAD).astype(BF16), wk=wk.astype(BF16), wv=wv.astype(BF16),
                qa=q_a_norm[None, :], kva=kv_a_norm[None, :], qg=pad_gain(q_norm), kg=pad_gain(k_norm))


def _rope_tables(seq):
    pos = jnp.arange(seq, dtype=F32)
    inv_freq = ROPE_THETA ** (-jnp.arange(0, QK_ROPE, 2, dtype=F32) / QK_ROPE)
    ang = pos[:, None] * inv_freq
    ang = jnp.concatenate([ang, ang], axis=-1)
    pad = lambda t, fill: jnp.concatenate(
        [jnp.full((seq, QK_NOPE), fill, F32), t, jnp.full((seq, SLOT - QK_HEAD), fill, F32)], axis=1)
    return pad(jnp.cos(ang), 1.0), pad(jnp.sin(ang), 0.0)


def _hi_lo(w):
    hi = w.astype(BF16)
    return jnp.stack([hi, (w - hi.astype(F32)).astype(BF16)], axis=1)


def _filter_weights(w1, b1, w2, b2, w3, freq, decay):
    hid, nbands = FILTER_HIDDEN, FILTER_BANDS
    assert 2 * hid == LANES and 2 * nbands <= hid
    both = lambda row: jnp.concatenate([row, row])[None, :]
    blockdiag = lambda a: jnp.concatenate(
        [jnp.concatenate([a, jnp.zeros_like(a)], axis=1), jnp.concatenate([jnp.zeros_like(a), a], axis=1)], axis=0)
    bands = jnp.linspace(1e-4, nbands - 1, nbands, dtype=F32)
    rest = jnp.zeros((hid - 2 * nbands,), F32)
    band_half = jnp.concatenate([bands, bands, rest])
    phase_half = jnp.concatenate([jnp.zeros((nbands,), F32), jnp.full((nbands,), 0.5 * math.pi, F32), rest])
    w1_half = jnp.concatenate([-w1[1 + nbands:], w1[1:1 + nbands], jnp.zeros((hid - 2 * nbands, hid), F32)], axis=0)
    cols = HYENA_ORDER * D_HYENA
    w3d = jnp.transpose(w3.reshape(hid, N_DIR, cols), (1, 0, 2))
    w3p = jnp.stack([jnp.pad(w3d[0], ((0, hid), (0, 0))), jnp.pad(w3d[1], ((hid, 0), (0, 0)))], axis=0)
    w3_hi = w3p.astype(BF16)
    w3_lo = (w3p - w3_hi.astype(F32)).astype(BF16)
    return dict(bands=both(band_half), phase=both(phase_half), w1t=both(w1[0]), w1=blockdiag(w1_half),
                b1=both(b1), w2=blockdiag(w2), b2=both(b2), freq=both(freq), w3_hi=w3_hi, w3_lo=w3_lo,
                decay=decay.reshape(N_DIR, 1, cols))


def _trunk(x, p):
    bsz, seq, _ = x.shape
    n = bsz * seq
    cos_t, sin_t = _rope_tables(seq)
    x2 = x.reshape(n, D_MODEL)
    for l in range(DEPTH):
        uh, cq, ckvr = _inproj(x2, p["attn_norm"][l][None, :], p["w_in"][l])
        y_h = _hyena(uh.reshape(bsz, seq, COL_HYENA), p["conv_w"][l], p["conv_b"][l][None, :],
                     p["filt"][l], p["hyena_bias"][l])
        qt, k, vt = _qkv(cq.reshape(bsz, seq, -1), ckvr.reshape(bsz, seq, -1), cos_t, sin_t, p["attn"][l])
        y_a = _flash(qt, k, vt)
        xm, xn, aff_t, aff_n = _outproj(y_h.reshape(n, D_HYENA), y_a.reshape(n, D_ATTN), x2,
                                 p["out_norm"][l][None, :], p["w_out"][l], p["mlp_norm"][l][None, :],
                                 p["w_router"][l])
        x2 = _ec_moe(xm, xn, aff_t, aff_n, p["w_gate"][l], p["w_up"][l], p["w_down"][l])
    return x2.reshape(bsz, seq, D_MODEL)


def kernel(x_prompt, x_sample, attn_norm, w_in, conv_w, conv_b, filt_w1, filt_b1, filt_w2, filt_b2, filt_w3,
           filt_freq, filt_decay, hyena_bias, q_a_norm, w_uq, kv_a_norm, w_ukv, q_norm, k_norm, out_norm,
           w_out, mlp_norm, w_router, w_gate, w_up, w_down):
    p = dict(
        attn_norm=attn_norm,
        w_in=jnp.pad(w_in, ((0, 0), (0, 0), (0, D_IN_PAD - D_IN))).astype(BF16),
        conv_w=conv_w, conv_b=conv_b, hyena_bias=hyena_bias,
        filt=[_filter_weights(filt_w1[l], filt_b1[l], filt_w2[l], filt_b2[l], filt_w3[l], filt_freq[l],
                              filt_decay[l]) for l in range(DEPTH)],
        attn=[_attn_weights(q_a_norm[l], w_uq[l], kv_a_norm[l], w_ukv[l], q_norm[l], k_norm[l])
              for l in range(DEPTH)],
        out_norm=out_norm, w_out=w_out.astype(BF16), mlp_norm=mlp_norm,
        w_router=_hi_lo(jnp.pad(w_router, ((0, 0), (0, 0), (0, LANES - N_EXPERTS)))),
        w_gate=w_gate.astype(BF16), w_up=w_up.astype(BF16), w_down=w_down.astype(BF16))
    return (_trunk(x_prompt, p), _trunk(x_sample, p))
```

```python
import functools
import math

import jax
import jax.numpy as jnp
from jax import lax
from jax.experimental import pallas as pl
from jax.experimental.pallas import tpu as pltpu

F32 = jnp.float32
BF16 = jnp.bfloat16
I32 = jnp.int32
HIGHEST = lax.Precision.HIGHEST

D_MODEL = 1024
DEPTH = 2
D_HYENA = 512
HYENA_ORDER = 2
FILTER_BANDS = 16
FILTER_HIDDEN = 64
N_DIR = 2
DECAY_SHIFT = 0.05
N_HEADS = 8
QK_NOPE = 64
QK_ROPE = 32
QK_HEAD = QK_NOPE + QK_ROPE
V_HEAD = 64
V_AUG = V_HEAD + 16
D_ATTN = N_HEADS * V_HEAD
Q_LORA = 256
KV_LORA = 128
ROPE_THETA = 10000.0
N_EXPERTS = 16
EC_CAPACITY = 2
D_EXPERT = 512
EPS = 1e-6
COL_HYENA = (HYENA_ORDER + 1) * D_HYENA
COL_Q = COL_HYENA + Q_LORA
COL_KV = COL_Q + KV_LORA
D_IN = COL_KV + QK_ROPE
D_IN_PAD = 2048

LANES = 128
SLOT = 128
TOK_BLOCK = 256
SLOT_TILE = 256
ALIGN = 16
GATHER_ROWS = 64
COMBINE_FAST_ROWS = 128
ATT_TQ = 512
ATT_QC = 256
ATT_TK = 256
ATT_UNROLL = 16
MIB = 1024 * 1024


def _params(sem, vmem_mib):
    return pltpu.CompilerParams(dimension_semantics=sem, vmem_limit_bytes=vmem_mib * MIB)


def _rms(x):
    return x * lax.rsqrt(jnp.mean(x * x, axis=-1, keepdims=True) + EPS)


def _inproj_body(x_ref, g_ref, w_ref, uh_ref, cq_ref, ckvr_ref):
    xn = _rms(x_ref[...]) * g_ref[...]
    u = jnp.dot(xn.astype(BF16), w_ref[...], preferred_element_type=F32)
    uh_ref[...] = u[:, :COL_HYENA]
    cq_ref[...] = u[:, COL_HYENA:COL_Q]
    ckvr_ref[...] = u[:, COL_Q:]


def _inproj(x2d, g, w_pad):
    n = x2d.shape[0]
    tm = min(512, n)
    return pl.pallas_call(
        _inproj_body,
        grid=(n // tm,),
        in_specs=[pl.BlockSpec((tm, D_MODEL), lambda i: (i, 0)),
                  pl.BlockSpec((1, D_MODEL), lambda i: (0, 0)),
                  pl.BlockSpec((D_MODEL, D_IN_PAD), lambda i: (0, 0))],
        out_specs=[pl.BlockSpec((tm, COL_HYENA), lambda i: (i, 0)),
                   pl.BlockSpec((tm, Q_LORA), lambda i: (i, 0)),
                   pl.BlockSpec((tm, D_IN_PAD - COL_Q), lambda i: (i, 0))],
        out_shape=[jax.ShapeDtypeStruct((n, COL_HYENA), F32),
                   jax.ShapeDtypeStruct((n, Q_LORA), F32),
                   jax.ShapeDtypeStruct((n, D_IN_PAD - COL_Q), F32)],
        compiler_params=_params(("parallel",), 40),
        name="inproj",
    )(x2d, g, w_pad)


def _sconv_body(u_ref, prev_ref, next_ref, w_ref, b_ref, v_ref, x1_ref, x2_ref):
    i = pl.program_id(1)
    last = pl.num_programs(1) - 1
    u = u_ref[0]
    tl = u.shape[0]
    prev_row = jnp.where(i == 0, 0.0, prev_ref[0, 7:8, :])
    next_row = jnp.where(i == last, 0.0, next_ref[0, 0:1, :])
    row = lax.broadcasted_iota(I32, u.shape, 0)
    up = jnp.where(row == 0, prev_row, pltpu.roll(u, 1, axis=0))
    dn = jnp.where(row == tl - 1, next_row, pltpu.roll(u, tl - 1, axis=0))
    y = up * w_ref[0:1, :] + u * w_ref[1:2, :] + dn * w_ref[2:3, :] + b_ref[...]
    rows, flat = v_ref.shape[1:]
    for k, o_ref in enumerate((v_ref, x1_ref, x2_ref)):
        o_ref[0] = y[:, k * D_HYENA:(k + 1) * D_HYENA].reshape(rows, tl // rows, D_HYENA).reshape(rows, flat)


def _sconv(uh, w, b, n2):
    bsz, seq, c = uh.shape
    rows = 8
    tl = rows * n2
    r = tl // 8
    nblk8 = seq // 8
    out = jax.ShapeDtypeStruct((bsz, seq // n2, n2 * D_HYENA), F32)
    ospec = pl.BlockSpec((1, rows, n2 * D_HYENA), lambda bi, i: (bi, i, 0))
    return pl.pallas_call(
        _sconv_body,
        grid=(bsz, seq // tl),
        in_specs=[pl.BlockSpec((1, tl, c), lambda bi, i: (bi, i, 0)),
                  pl.BlockSpec((1, 8, c), lambda bi, i: (bi, jnp.maximum(i * r - 1, 0), 0)),
                  pl.BlockSpec((1, 8, c), lambda bi, i: (bi, jnp.minimum((i + 1) * r, nblk8 - 1), 0)),
                  pl.BlockSpec((3, c), lambda bi, i: (0, 0)),
                  pl.BlockSpec((1, c), lambda bi, i: (0, 0))],
        out_specs=[ospec, ospec, ospec],
        out_shape=[out, out, out],
        compiler_params=_params(("parallel", "parallel"), 40),
        name="sconv",
    )(uh, uh, uh, w, b)


def _filter_body(bands_ref, phase_ref, w1t_ref, w1_ref, b1_ref, w2_ref, b2_ref, fr_ref, w3h_ref, w3l_ref,
                 dec_ref, k_ref, sum_ref, *, seq, n2):
    i = pl.program_id(0)
    hp = k_ref.shape[0] // 2
    cols = sum_ref.shape[1]
    jb = k_ref.shape[1] // cols
    row = lax.broadcasted_iota(I32, (jb * hp, 1), 0)
    n_lo = (row % hp) * n2 + i * jb + row // hp
    n_hi = n_lo + seq
    t_of = lambda n: jnp.where(n < seq, n, 2 * seq - n).astype(F32)
    lane = lax.broadcasted_iota(I32, (jb * hp, LANES), 1)
    t_idx = jnp.where(lane < FILTER_HIDDEN, t_of(n_lo), t_of(n_hi))
    feats = jnp.sin(2.0 * math.pi * bands_ref[...] * t_idx / seq + phase_ref[...])
    fr = fr_ref[...]
    pre = (t_idx / (seq - 1)) * w1t_ref[...] + jnp.dot(feats, w1_ref[...], precision=HIGHEST,
                                                      preferred_element_type=F32)
    h = jnp.sin(fr * (pre + b1_ref[...]))
    h = jnp.sin(fr * (jnp.dot(h, w2_ref[...], precision=HIGHEST, preferred_element_type=F32) + b2_ref[...]))
    h_hi = h.astype(BF16)
    h_lo = (h - h_hi.astype(F32)).astype(BF16)

    @pl.when(i == 0)
    def _():
        sum_ref[...] = jnp.zeros_like(sum_ref)

    for j, n in enumerate((n_lo, n_hi)):
        out = (jnp.dot(h_hi, w3h_ref[j], preferred_element_type=F32)
               + jnp.dot(h_lo, w3h_ref[j], preferred_element_type=F32)
               + jnp.dot(h_hi, w3l_ref[j], preferred_element_type=F32))
        window = jnp.exp(-(t_of(n) / (seq - 1)) * dec_ref[j]) + DECAY_SHIFT
        k = jnp.where(n == seq, 0.0, out * window)
        sum_ref[...] += jnp.sum(jnp.abs(k), axis=0, keepdims=True)
        for jj in range(jb):
            k_ref[j * hp:(j + 1) * hp, jj * cols:(jj + 1) * cols] = k[jj * hp:(jj + 1) * hp].astype(BF16)


def _filters(seq, n1, n2, fw):
    cols = HYENA_ORDER * D_HYENA
    const = lambda shape: pl.BlockSpec(shape, lambda i: (0,) * len(shape))
    jb = max(1, 1024 // n1)
    return pl.pallas_call(
        functools.partial(_filter_body, seq=seq, n2=n2),
        grid=(n2 // jb,),
        in_specs=[const((1, LANES)), const((1, LANES)), const((1, LANES)), const((LANES, LANES)),
                  const((1, LANES)), const((LANES, LANES)), const((1, LANES)), const((1, LANES)),
                  const((N_DIR, LANES, cols)), const((N_DIR, LANES, cols)), const((N_DIR, 1, cols))],
        out_specs=[pl.BlockSpec((n1, jb * cols), lambda i: (0, i)),
                   pl.BlockSpec((1, cols), lambda i: (0, 0))],
        out_shape=[jax.ShapeDtypeStruct((n1, n2 * cols), BF16),
                   jax.ShapeDtypeStruct((1, cols), F32)],
        compiler_params=_params(("arbitrary",), 32),
        name="filter_gen",
    )(fw["bands"], fw["phase"], fw["w1t"], fw["w1"], fw["b1"], fw["w2"], fw["b2"], fw["freq"],
      fw["w3_hi"], fw["w3_lo"], fw["decay"])


def _fft_dims(seq):
    n2 = 128 if 2 * seq >= 32768 else 64
    n1 = 2 * seq // n2
    return n1, n2


def _dft_tables(n1, n2):
    n = n1 * n2
    n1h = n1 // 2
    k1 = jnp.arange(n1h, dtype=I32)[:, None]
    m1 = jnp.arange(n1, dtype=I32)[None, :]
    ang = (2.0 * math.pi / n1) * ((k1 * m1) % n1).astype(F32)
    top = jnp.cos(ang)
    bot = -jnp.sin(ang)
    nyq = jnp.where(m1 % 2 == 0, 1.0, -1.0).astype(F32)
    bot = jnp.concatenate([nyq, bot[1:]], axis=0)
    fa = jnp.concatenate([top, bot], axis=0)
    weight = jnp.where((jnp.arange(n1) % n1h) == 0, 1.0, 2.0).astype(F32) / n
    fi = (fa[:, :n1h] * weight[:, None]).T

    kk = jnp.arange(n1h + 1, dtype=I32)[:, None, None]
    k2 = jnp.arange(n2, dtype=I32)[None, :, None]
    m2 = jnp.arange(n2, dtype=I32)[None, None, :]
    phi = (2.0 * math.pi / n) * ((m2 * (kk + n1 * k2)) % n).astype(F32)
    gr = jnp.cos(phi)
    gi = -jnp.sin(phi)
    blk = jnp.concatenate([jnp.concatenate([gr, -gi], axis=2),
                           jnp.concatenate([gi, gr], axis=2)], axis=1)
    left = (jnp.arange(2 * n2) < n2)[None, :]
    g0 = jnp.where(left, blk[0], 0.0)
    gf = jnp.concatenate([g0[None], blk[1:n1h]], axis=0)
    gnyq = jnp.concatenate([jnp.zeros((2 * n2, n2), F32), blk[n1h][:, :n2]], axis=1)
    return dict(fa_full=fa.astype(BF16), fa_half=fa[:, :n1h].astype(BF16), fi=fi.astype(BF16),
                gf=gf.astype(BF16), gnyq=gnyq.astype(BF16),
                ginv=jnp.transpose(gf, (0, 2, 1)).astype(BF16), m2=gnyq.T.astype(BF16))


def _fft_a_body(x_ref, f_ref, o_ref):
    a = jnp.dot(f_ref[...], x_ref[0].astype(BF16), preferred_element_type=F32)
    n1h = o_ref.shape[2]
    o_ref[0, 0] = a[:n1h].astype(BF16)
    o_ref[0, 1] = a[n1h:].astype(BF16)


def _fft_a(x3, fmat):
    bsz, r, nc = x3.shape
    n1 = fmat.shape[0]
    tn = min(8192, nc)
    return pl.pallas_call(
        _fft_a_body,
        grid=(bsz, nc // tn),
        in_specs=[pl.BlockSpec((1, r, tn), lambda b, j: (b, 0, j)),
                  pl.BlockSpec((n1, r), lambda b, j: (0, 0))],
        out_specs=pl.BlockSpec((1, 2, n1 // 2, tn), lambda b, j: (b, 0, 0, j)),
        out_shape=jax.ShapeDtypeStruct((bsz, 2, n1 // 2, nc), BF16),
        compiler_params=_params(("parallel", "parallel"), 48),
        name="fft_stage_a",
    )(x3, fmat)


def _fft_b_body(ar_ref, ai_ref, g_ref, gn_ref, sc_ref, x_ref, xn_ref, *, kb, n2):
    j = pl.program_id(1)
    c = sc_ref.shape[1]
    inv = 1.0 / sc_ref[...]
    a_re = ar_ref[0, 0].reshape(kb, n2, c)
    a_im = ai_ref[0, 0].reshape(kb, n2, c)
    for kk in range(kb):
        rhs = jnp.concatenate([a_re[kk], a_im[kk]], axis=0)
        x_ref[0, kk] = (jnp.dot(g_ref[kk], rhs, preferred_element_type=F32) * inv).astype(BF16)

    @pl.when(j == 0)
    def _():
        rhs = jnp.concatenate([a_re[0], a_im[0]], axis=0)
        xn_ref[0] = (jnp.dot(gn_ref[...], rhs, preferred_element_type=F32) * inv).astype(BF16)


def _fft_b(a4, tabs, scale):
    bsz, _, n1h, nc = a4.shape
    c = scale.shape[1]
    n2 = nc // c
    kb = 16
    return pl.pallas_call(
        functools.partial(_fft_b_body, kb=kb, n2=n2),
        grid=(bsz, n1h // kb),
        in_specs=[pl.BlockSpec((1, 1, kb, nc), lambda b, j: (b, 0, j, 0)),
                  pl.BlockSpec((1, 1, kb, nc), lambda b, j: (b, 1, j, 0)),
                  pl.BlockSpec((kb, 2 * n2, 2 * n2), lambda b, j: (j, 0, 0)),
                  pl.BlockSpec((2 * n2, 2 * n2), lambda b, j: (0, 0)),
                  pl.BlockSpec((1, c), lambda b, j: (0, 0))],
        out_specs=[pl.BlockSpec((1, kb, 2 * n2, c), lambda b, j: (b, j, 0, 0)),
                   pl.BlockSpec((1, 2 * n2, c), lambda b, j: (b, 0, 0))],
        out_shape=[jax.ShapeDtypeStruct((bsz, n1h, 2 * n2, c), BF16),
                   jax.ShapeDtypeStruct((bsz, 2 * n2, c), BF16)],
        compiler_params=_params(("parallel", "arbitrary"), 48),
        name="fft_stage_b",
    )(a4, a4, tabs["gf"], tabs["gnyq"], scale)


def _cmul(x, k, n2):
    xr, xi = x[:n2], x[n2:]
    kr, ki = k[:n2], k[n2:]
    return jnp.concatenate([xr * kr - xi * ki, xr * ki + xi * kr], axis=0).astype(BF16)


def _spec_body(ar_ref, ai_ref, k_ref, kn_ref, g_ref, gn_ref, gi_ref, m2_ref, o_ref, *, kb, n2):
    j = pl.program_id(1)
    c = ar_ref.shape[3] // n2
    a_re = ar_ref[0, 0].reshape(kb, n2, c)
    a_im = ai_ref[0, 0].reshape(kb, n2, c)

    def through(fwd, rhs, kf, inv):
        x = jnp.dot(fwd, rhs, preferred_element_type=F32)
        return jnp.dot(inv, _cmul(x, kf.astype(F32), n2), preferred_element_type=F32)

    re_rows, im_rows = [], []
    for kk in range(kb):
        rhs = jnp.concatenate([a_re[kk], a_im[kk]], axis=0)
        acc = through(g_ref[kk], rhs, k_ref[0, kk], gi_ref[kk])
        if kk == 0:
            nyq = through(gn_ref[...], rhs, kn_ref[0], m2_ref[...])
            acc = jnp.where(j == 0, jnp.concatenate([acc[:n2], nyq[n2:]], axis=0), acc)
        re_rows.append(acc[:n2].astype(BF16))
        im_rows.append(acc[n2:].astype(BF16))
    o_ref[0, 0] = jnp.stack(re_rows, axis=0).reshape(kb, n2 * c)
    o_ref[0, 1] = jnp.stack(im_rows, axis=0).reshape(kb, n2 * c)


def _spec_conv(a4, kf, kfnyq, order, tabs):
    bsz, _, n1h, nc = a4.shape
    tn2 = kf.shape[2]
    n2 = tn2 // 2
    c = nc // n2
    kb = 16
    return pl.pallas_call(
        functools.partial(_spec_body, kb=kb, n2=n2),
        grid=(bsz, n1h // kb),
        in_specs=[pl.BlockSpec((1, 1, kb, nc), lambda b, j: (b, 0, j, 0)),
                  pl.BlockSpec((1, 1, kb, nc), lambda b, j: (b, 1, j, 0)),
                  pl.BlockSpec((1, kb, tn2, c), lambda b, j: (0, j, 0, order)),
                  pl.BlockSpec((1, tn2, c), lambda b, j: (0, 0, order)),
                  pl.BlockSpec((kb, tn2, tn2), lambda b, j: (j, 0, 0)),
                  pl.BlockSpec((tn2, tn2), lambda b, j: (0, 0)),
                  pl.BlockSpec((kb, tn2, tn2), lambda b, j: (j, 0, 0)),
                  pl.BlockSpec((tn2, tn2), lambda b, j: (0, 0))],
        out_specs=pl.BlockSpec((1, 2, kb, nc), lambda b, j: (b, 0, j, 0)),
        out_shape=jax.ShapeDtypeStruct((bsz, 2, n1h, nc), BF16),
        compiler_params=_params(("parallel", "arbitrary"), 48),
        name="spectral_conv",
    )(a4, a4, kf, kfnyq, tabs["gf"], tabs["gnyq"], tabs["ginv"], tabs["m2"])


def _ifft_a_body(b_ref, f_ref, z_ref, gate_ref, bias_ref, o_ref):
    y = jnp.dot(f_ref[...], b_ref[0], preferred_element_type=F32)
    res = gate_ref[0] * (y + z_ref[0] * bias_ref[...])
    o_ref[0] = res.reshape(o_ref.shape[1:])


def _ifft_a(b3, fi, z3, gate3, bias_t, time_major):
    bsz, n1, nc = b3.shape
    n1h = n1 // 2
    tn = bias_t.shape[1]
    c = D_HYENA
    if time_major:
        out_spec = pl.BlockSpec((1, n1h, tn // c, c), lambda b, j: (b, 0, j, 0))
        out_shape = jax.ShapeDtypeStruct((bsz, n1h, nc // c, c), F32)
    else:
        out_spec = pl.BlockSpec((1, n1h, tn), lambda b, j: (b, 0, j))
        out_shape = jax.ShapeDtypeStruct((bsz, n1h, nc), F32)
    return pl.pallas_call(
        _ifft_a_body,
        grid=(bsz, nc // tn),
        in_specs=[pl.BlockSpec((1, n1, tn), lambda b, j: (b, 0, j)),
                  pl.BlockSpec((n1h, n1), lambda b, j: (0, 0)),
                  pl.BlockSpec((1, n1h, tn), lambda b, j: (b, 0, j)),
                  pl.BlockSpec((1, n1h, tn), lambda b, j: (b, 0, j)),
                  pl.BlockSpec((1, tn), lambda b, j: (0, 0))],
        out_specs=out_spec,
        out_shape=out_shape,
        compiler_params=_params(("parallel", "parallel"), 48),
        name="ifft_stage_a",
    )(b3, fi, z3, gate3, bias_t)


def _hyena(uh, conv_w, conv_b, fw, bias):
    bsz, seq, _ = uh.shape
    c = D_HYENA
    n1, n2 = _fft_dims(seq)
    n1h = n1 // 2
    tabs = _dft_tables(n1, n2)
    v, x1, x2 = _sconv(uh, conv_w, conv_b, n2)

    kcirc, ksum = _filters(seq, n1, n2, fw)
    cols = HYENA_ORDER * c
    ka = _fft_a(kcirc[None], tabs["fa_full"])
    kf, kfnyq = _fft_b(ka, tabs, ksum)

    tn = min(8192, n2 * c)
    z = v
    for order, gate in enumerate((x1, x2)):
        za = _fft_a(z, tabs["fa_half"])
        zb = _spec_conv(za, kf, kfnyq, order, tabs)
        bias_t = jnp.tile(bias[order][None, :], (1, tn // c))
        z = _ifft_a(zb.reshape(bsz, n1, n2 * c), tabs["fi"], z, gate, bias_t,
                    time_major=order == HYENA_ORDER - 1)
    return z.reshape(bsz, seq, c)


def _qkv_body(cq_ref, ckvr_ref, cos_ref, sin_ref, wq_ref, wk_ref, wv_ref, qa_ref, kva_ref, qg_ref, kg_ref,
              qt_ref, k_ref, vt_ref):
    cqn = _rms(cq_ref[0]) * qa_ref[...]
    qs = jnp.dot(cqn.astype(BF16), wq_ref[...], preferred_element_type=F32)
    ck = ckvr_ref[0]
    ckvn = (_rms(ck[:, :KV_LORA]) * kva_ref[...]).astype(BF16)
    ks = jnp.dot(ckvn, wk_ref[...], preferred_element_type=F32)
    vs = jnp.dot(ckvn, wv_ref[...], preferred_element_type=F32)
    krope = pltpu.roll(ck[:, KV_LORA:], QK_NOPE, axis=1)
    cos = cos_ref[...]
    sin = sin_ref[...]
    lane = lax.broadcasted_iota(I32, cos.shape, 1)
    half = QK_ROPE // 2
    first = (lane >= QK_NOPE) & (lane < QK_NOPE + half)
    second = (lane >= QK_NOPE + half) & (lane < QK_HEAD)

    def head(xh, gain, scale):
        ms = jnp.sum(xh * xh, axis=-1, keepdims=True) * (1.0 / QK_HEAD)
        xn = xh * lax.rsqrt(ms + EPS) * gain
        rot = jnp.where(first, -pltpu.roll(xn, SLOT - half, axis=1),
                        jnp.where(second, pltpu.roll(xn, half, axis=1), 0.0))
        return (xn * cos + rot * sin) * scale

    q_scale = QK_HEAD ** -0.5 * math.log2(math.e)
    for h in range(N_HEADS):
        sl = slice(h * SLOT, (h + 1) * SLOT)
        qt_ref[0, h] = head(qs[:, sl], qg_ref[...], q_scale).T.astype(BF16)
        k_ref[0, h] = head(ks[:, sl] + krope, kg_ref[...], 1.0).astype(BF16)
    aug = lax.broadcasted_iota(I32, (V_AUG - V_HEAD, cos.shape[0]), 0)
    ones_row = jnp.where(aug == 0, 1.0, 0.0)
    for hp in range(N_HEADS // 2):
        pair_t = vs[:, hp * 2 * V_HEAD:(hp + 1) * 2 * V_HEAD].T
        vt_ref[0, hp] = jnp.concatenate(
            [pair_t[:V_HEAD], ones_row, pair_t[V_HEAD:], ones_row], axis=0).astype(BF16)


def _qkv(cq, ckvr, cos_t, sin_t, aw):
    bsz, seq, _ = cq.shape
    tm = min(512, seq)
    const = lambda shape: pl.BlockSpec(shape, lambda b, i: (0,) * len(shape))
    return pl.pallas_call(
        _qkv_body,
        grid=(bsz, seq // tm),
        in_specs=[pl.BlockSpec((1, tm, Q_LORA), lambda b, i: (b, i, 0)),
                  pl.BlockSpec((1, tm, D_IN_PAD - COL_Q), lambda b, i: (b, i, 0)),
                  pl.BlockSpec((tm, SLOT), lambda b, i: (i, 0)),
                  pl.BlockSpec((tm, SLOT), lambda b, i: (i, 0)),
                  const((Q_LORA, N_HEADS * SLOT)), const((KV_LORA, N_HEADS * SLOT)),
                  const((KV_LORA, D_ATTN)),
                  const((1, Q_LORA)), const((1, KV_LORA)), const((1, SLOT)), const((1, SLOT))],
        out_specs=[pl.BlockSpec((1, N_HEADS, SLOT, tm), lambda b, i: (b, 0, 0, i)),
                   pl.BlockSpec((1, N_HEADS, tm, SLOT), lambda b, i: (b, 0, i, 0)),
                   pl.BlockSpec((1, N_HEADS // 2, 2 * V_AUG, tm), lambda b, i: (b, 0, 0, i))],
        out_shape=[jax.ShapeDtypeStruct((bsz, N_HEADS, SLOT, seq), BF16),
                   jax.ShapeDtypeStruct((bsz, N_HEADS, seq, SLOT), BF16),
                   jax.ShapeDtypeStruct((bsz, N_HEADS // 2, 2 * V_AUG, seq), BF16)],
        compiler_params=_params(("parallel", "parallel"), 40),
        name="qkv_prep",
    )(cq, ckvr, cos_t, sin_t, aw["wq"], aw["wk"], aw["wv"], aw["qa"], aw["kva"], aw["qg"], aw["kg"])


def _flash_body(qt_ref, k_ref, vt_ref, o_ref, s0_ref, s1_ref, acc_ref, *, tk, nk, nqc, unroll):
    chains = [(hh, qc) for hh in range(2) for qc in range(nqc)]
    acc_ref[...] = jnp.zeros(acc_ref.shape, F32)

    def qk(t, s_ref):
        ks = pl.multiple_of(t * tk, tk)
        tile_max = []
        for c, (hh, qc) in enumerate(chains):
            k = k_ref[0, hh, pl.ds(ks, tk), :]
            s = jnp.dot(k, qt_ref[0, hh, :, qc * ATT_QC:(qc + 1) * ATT_QC], preferred_element_type=F32)
            s_ref[c] = s
            tile_max.append(jnp.max(s, axis=0, keepdims=True))
        return tile_max

    def softmax_pv(t, s_ref, tile_max, m):
        ks = pl.multiple_of(t * tk, tk)
        m_out = []
        for c, (hh, qc) in enumerate(chains):
            m_new = jnp.maximum(m[c], tile_max[c])
            a = jnp.exp2(m[c] - m_new)
            p = jnp.exp2(s_ref[c] - m_new)
            m_out.append(m_new)
            vt = vt_ref[0, 0, hh * V_AUG:(hh + 1) * V_AUG, pl.ds(ks, tk)]
            acc_ref[c] = acc_ref[c] * a + jnp.dot(vt, p.astype(BF16), preferred_element_type=F32)
        return m_out

    def group(u, carry):
        tile_max, m = carry
        for i in range(0, unroll, 2):
            t = unroll * u + i
            tile_max1 = qk(t + 1, s1_ref)
            m = softmax_pv(t, s0_ref, tile_max, m)
            tile_max = qk(jnp.minimum(t + 2, nk - 1), s0_ref)
            m = softmax_pv(t + 1, s1_ref, tile_max1, m)
        return tile_max, m

    init = (qk(0, s0_ref), [jnp.full((1, ATT_QC), -jnp.inf, F32)] * len(chains))
    lax.fori_loop(0, nk // unroll, group, init)

    def normalised(c):
        acc = acc_ref[c]
        return acc[:V_HEAD] * (1.0 / acc[V_HEAD:V_HEAD + 1])

    heads = [jnp.concatenate([normalised(hh * nqc + qc) for qc in range(nqc)], axis=1) for hh in range(2)]
    o_ref[0] = jnp.concatenate(heads, axis=0).T


def _flash(qt, k, vt):
    bsz, nh, _, seq = qt.shape
    tq = min(ATT_TQ, seq)
    tk = min(ATT_TK, seq)
    nqc = tq // ATT_QC
    nk = seq // tk
    unroll = min(ATT_UNROLL, nk)
    assert nk % unroll == 0 and unroll % 2 == 0
    return pl.pallas_call(
        functools.partial(_flash_body, tk=tk, nk=nk, nqc=nqc, unroll=unroll),
        grid=(bsz, nh // 2, seq // tq),
        in_specs=[pl.BlockSpec((1, 2, SLOT, tq), lambda b, hp, i: (b, hp, 0, i)),
                  pl.BlockSpec((1, 2, seq, SLOT), lambda b, hp, i: (b, hp, 0, 0)),
                  pl.BlockSpec((1, 1, 2 * V_AUG, seq), lambda b, hp, i: (b, hp, 0, 0))],
        out_specs=pl.BlockSpec((1, tq, 2 * V_HEAD), lambda b, hp, i: (b, i, hp)),
        out_shape=jax.ShapeDtypeStruct((bsz, seq, nh * V_HEAD), F32),
        scratch_shapes=[pltpu.VMEM((2 * nqc, tk, ATT_QC), F32), pltpu.VMEM((2 * nqc, tk, ATT_QC), F32),
                        pltpu.VMEM((2 * nqc, V_AUG, ATT_QC), F32)],
        compiler_params=_params(("parallel", "parallel", "parallel"), 48),
        name="flash_attn",
    )(qt, k, vt)


def _outproj_body(yh_ref, ya_ref, x_ref, og_ref, w_ref, mg_ref, wr_ref, xm_ref, xn_ref, aff_ref, affn_ref):
    og = og_ref[...]
    half = D_MODEL // 2
    y = jnp.concatenate([_rms(yh_ref[...]) * og[:, :half], _rms(ya_ref[...]) * og[:, half:]], axis=1)
    xm = x_ref[...] + jnp.dot(y.astype(BF16), w_ref[...], preferred_element_type=F32)
    xm_ref[...] = xm
    xn = _rms(xm) * mg_ref[...]
    xn_hi = xn.astype(BF16)
    xn_ref[...] = xn_hi
    xn_lo = (xn - xn_hi.astype(F32)).astype(BF16)
    logits = (jnp.dot(xn_hi, wr_ref[0], preferred_element_type=F32)
              + jnp.dot(xn_lo, wr_ref[0], preferred_element_type=F32)
              + jnp.dot(xn_hi, wr_ref[1], preferred_element_type=F32))
    lane = lax.broadcasted_iota(I32, logits.shape, 1)
    logits = jnp.where(lane < N_EXPERTS, logits, -jnp.inf)
    e = jnp.exp(logits - jnp.max(logits, axis=-1, keepdims=True))
    aff = e / jnp.sum(e, axis=-1, keepdims=True)
    aff_ref[...] = aff.T[:N_EXPERTS]
    affn_ref[...] = aff[:, :N_EXPERTS]


def _outproj(yh, ya, x2d, og, w_out, mg, wr_pad):
    n = x2d.shape[0]
    tm = min(512, n)
    half = D_MODEL // 2
    const = lambda shape: pl.BlockSpec(shape, lambda i: (0,) * len(shape))
    return pl.pallas_call(
        _outproj_body,
        grid=(n // tm,),
        in_specs=[pl.BlockSpec((tm, half), lambda i: (i, 0)),
                  pl.BlockSpec((tm, half), lambda i: (i, 0)),
                  pl.BlockSpec((tm, D_MODEL), lambda i: (i, 0)),
                  const((1, D_MODEL)), const((D_MODEL, D_MODEL)), const((1, D_MODEL)),
                  const((2, D_MODEL, LANES))],
        out_specs=[pl.BlockSpec((tm, D_MODEL), lambda i: (i, 0)),
                   pl.BlockSpec((tm, D_MODEL), lambda i: (i, 0)),
                   pl.BlockSpec((N_EXPERTS, tm), lambda i: (0, i)),
                   pl.BlockSpec((tm, N_EXPERTS), lambda i: (i, 0))],
        out_shape=[jax.ShapeDtypeStruct((n, D_MODEL), F32),
                   jax.ShapeDtypeStruct((n, D_MODEL), BF16),
                   jax.ShapeDtypeStruct((N_EXPERTS, n), F32),
                   jax.ShapeDtypeStruct((n, N_EXPERTS), F32)],
        compiler_params=_params(("parallel",), 40),
        name="outproj_router",
    )(yh, ya, x2d, og, w_out, mg, wr_pad)


def _select_body(aff_ref, upper_ref, lower_ref, pos_ref, off_ref, *, cap):
    group = aff_ref.shape[0]
    bits = [pltpu.bitcast(aff_ref[x], I32) for x in range(group)]
    upper = upper_ref[...]
    lower = lower_ref[...]

    def count(mask):
        return jnp.sum(jnp.sum(mask.astype(F32), axis=1, keepdims=True), axis=0, keepdims=True)

    def bit_step(i, thr):
        bit = jnp.left_shift(jnp.int32(1), 30 - i)
        return tuple(jnp.where(count(bits[x] >= (thr[x] | bit)) >= cap, thr[x] | bit, thr[x])
                     for x in range(group))

    thr = lax.fori_loop(0, 31, bit_step, tuple(jnp.zeros((1, 1), I32) for _ in range(group)))

    def prefix(mask):
        within = jnp.dot(mask.astype(BF16), upper, preferred_element_type=F32)
        total = within[:, LANES - 1:LANES]
        offs = jnp.dot(lower, jnp.broadcast_to(total, within.shape).astype(BF16), preferred_element_type=F32)
        return within, offs

    for x in range(group):
        gt = bits[x] > thr[x]
        eq = bits[x] == thr[x]
        need = cap - count(gt)
        w_eq, o_eq = prefix(eq)
        sel = gt | (eq & (o_eq + w_eq <= need))
        w_sel, o_sel = prefix(sel)
        pos_ref[x] = jnp.where(sel, o_sel + w_sel - 1.0, -1.0).astype(I32)
        off_ref[x] = o_sel.astype(I32)


def _select(aff_t, cap):
    ne, n = aff_t.shape
    rows = n // LANES
    upper = (jnp.arange(LANES)[:, None] <= jnp.arange(LANES)[None, :]).astype(BF16)
    lower = (jnp.arange(rows)[None, :] < jnp.arange(rows)[:, None]).astype(BF16)
    group = 4
    blk = pl.BlockSpec((group, rows, LANES), lambda e: (e, 0, 0))
    pos, off = pl.pallas_call(
        functools.partial(_select_body, cap=cap),
        grid=(ne // group,),
        in_specs=[blk, pl.BlockSpec((LANES, LANES), lambda e: (0, 0)),
                  pl.BlockSpec((rows, rows), lambda e: (0, 0))],
        out_specs=[blk, blk],
        out_shape=[jax.ShapeDtypeStruct((ne, rows, LANES), I32),
                   jax.ShapeDtypeStruct((ne, rows, LANES), I32)],
        compiler_params=_params(("parallel",), 32),
        name="ec_select",
    )(aff_t.reshape(ne, rows, LANES), upper, lower)
    return pos.reshape(ne, n), off[:, :, 0]


def _ffn_body(off_ref, nch_ref, pos_ref, x_ref, wg_ref, wu_ref, wd_ref, y_ref, xs_ref, *, nb, sub, cap):
    e = pl.program_id(0)
    sb = pl.program_id(1)

    @pl.when(sb == 0)
    def _():
        xs_ref[...] = jnp.zeros(xs_ref.shape, BF16)

    row = lax.broadcasted_iota(I32, (GATHER_ROWS, TOK_BLOCK), 0)

    def gather(i, c):
        blk = e * nb + sb * sub + i
        base = off_ref[blk] * ALIGN
        tok = slice(i * TOK_BLOCK, (i + 1) * TOK_BLOCK)
        start = pl.multiple_of(base + c * GATHER_ROWS, ALIGN)
        hit = (pos_ref[0, :, tok] - start) == row
        win = jnp.dot(hit.astype(BF16), x_ref[tok, :], preferred_element_type=F32)
        xs_ref[pl.ds(start, GATHER_ROWS), :] += win.astype(BF16)

    for i in range(sub):
        gather(i, 0)
    for i in range(sub):
        lax.fori_loop(1, nch_ref[e * nb + sb * sub + i], lambda c, carry, i=i: (gather(i, c), carry)[1], 0)

    @pl.when(sb == pl.num_programs(1) - 1)
    def _():
        for j in range(cap // SLOT_TILE):
            rows = slice(j * SLOT_TILE, (j + 1) * SLOT_TILE)
            xt = xs_ref[rows, :]
            g = jnp.dot(xt, wg_ref[0], preferred_element_type=F32)
            u = jnp.dot(xt, wu_ref[0], preferred_element_type=F32)
            h = (g * jax.nn.sigmoid(g) * u).astype(BF16)
            y_ref[0, rows, :] = jnp.dot(h, wd_ref[0], preferred_element_type=F32).astype(BF16)
        y_ref[0, cap:, :] = jnp.zeros((y_ref.shape[1] - cap, D_MODEL), BF16)


def _ffn(base, nchunk, pos3, xn, wg, wu, wd, cap):
    ne = wg.shape[0]
    n = xn.shape[0]
    nb = n // TOK_BLOCK
    sub = min(8, nb)
    cap_pad = cap + TOK_BLOCK + ALIGN
    grid_spec = pltpu.PrefetchScalarGridSpec(
        num_scalar_prefetch=2,
        grid=(ne, nb // sub),
        in_specs=[pl.BlockSpec((1, 1, sub * TOK_BLOCK), lambda e, s, off, nch: (e, 0, s)),
                  pl.BlockSpec((sub * TOK_BLOCK, D_MODEL), lambda e, s, off, nch: (s, 0)),
                  pl.BlockSpec((1, D_MODEL, D_EXPERT), lambda e, s, off, nch: (e, 0, 0)),
                  pl.BlockSpec((1, D_MODEL, D_EXPERT), lambda e, s, off, nch: (e, 0, 0)),
                  pl.BlockSpec((1, D_EXPERT, D_MODEL), lambda e, s, off, nch: (e, 0, 0))],
        out_specs=pl.BlockSpec((1, cap_pad, D_MODEL), lambda e, s, off, nch: (e, 0, 0)),
        scratch_shapes=[pltpu.VMEM((cap_pad, D_MODEL), BF16)])
    return pl.pallas_call(
        functools.partial(_ffn_body, nb=nb, sub=sub, cap=cap),
        grid_spec=grid_spec,
        out_shape=jax.ShapeDtypeStruct((ne, cap_pad, D_MODEL), BF16),
        compiler_params=_params(("parallel", "arbitrary"), 48),
        name="ec_ffn",
    )(base, nchunk, pos3, xn, wg, wu, wd)


def _combine_body(off_ref, spill_ref, post_ref, affn_ref, xm_ref, *refs, nb, ne, win_rows):
    win_refs, tail_refs, o_ref = refs[:ne], refs[ne:-1], refs[-1]
    b = pl.program_id(0)
    post = post_ref[...]
    affn = affn_ref[...]

    def expand(y_refs, shift):
        width = y_refs[0].shape[0]
        lane = lax.broadcasted_iota(I32, (TOK_BLOCK, width), 1)
        total = jnp.zeros((TOK_BLOCK, D_MODEL), F32)
        for e in range(ne):
            rel = post[:, e:e + 1] - (off_ref[e * nb + b] * ALIGN + shift)
            hit = (rel == lane).astype(BF16)
            total = total + affn[:, e:e + 1] * jnp.dot(hit, y_refs[e][...], preferred_element_type=F32)
        return total

    o_ref[...] = xm_ref[...] + expand(win_refs, 0)

    if tail_refs:
        @pl.when(spill_ref[b] != 0)
        def _():
            o_ref[...] += expand(tail_refs, win_rows)


def _combine(base, spill, pos_t, aff_n, xm, y, win_rows):
    ne = y.shape[0]
    n = xm.shape[0]
    nb = n // TOK_BLOCK
    tail_rows = ALIGN if win_rows == TOK_BLOCK else 0

    def window(e, rows, shift):
        return pl.BlockSpec((pl.Squeezed(), pl.Element(rows), pl.Element(D_MODEL)),
                            lambda b, off, sp: (e, (off[e * nb + b] + shift // ALIGN) * ALIGN, 0))

    tails = [window(e, tail_rows, win_rows) for e in range(ne)] if tail_rows else []
    grid_spec = pltpu.PrefetchScalarGridSpec(
        num_scalar_prefetch=2,
        grid=(nb,),
        in_specs=([pl.BlockSpec((TOK_BLOCK, ne), lambda b, off, sp: (b, 0)),
                   pl.BlockSpec((TOK_BLOCK, ne), lambda b, off, sp: (b, 0)),
                   pl.BlockSpec((TOK_BLOCK, D_MODEL), lambda b, off, sp: (b, 0))]
                  + [window(e, win_rows, 0) for e in range(ne)] + tails),
        out_specs=pl.BlockSpec((TOK_BLOCK, D_MODEL), lambda b, off, sp: (b, 0)))
    return pl.pallas_call(
        functools.partial(_combine_body, nb=nb, ne=ne, win_rows=win_rows),
        grid_spec=grid_spec,
        out_shape=jax.ShapeDtypeStruct((n, D_MODEL), F32),
        compiler_params=_params(("parallel",), 48),
        name="ec_combine",
    )(base, spill, pos_t, aff_n, xm, *([y] * (ne + len(tails))))


def _ec_moe(xm, xn, aff_t, aff_n, wg, wu, wd):
    ne, n = aff_t.shape
    cap = max(1, EC_CAPACITY * n // N_EXPERTS)
    nb = n // TOK_BLOCK
    pos, off = _select(aff_t, cap)
    first = off[:, ::TOK_BLOCK // LANES]
    count = jnp.concatenate([first[:, 1:], jnp.full((ne, 1), cap, I32)], axis=1) - first
    base = first // ALIGN
    span = jnp.where(count > 0, first - base * ALIGN + count, 0)
    spill = jnp.any(span > TOK_BLOCK, axis=0).astype(I32)
    nchunk = ((span + GATHER_ROWS - 1) // GATHER_ROWS).reshape(ne * nb).astype(I32)
    base = base.reshape(ne * nb).astype(I32)
    y = _ffn(base, nchunk, pos.reshape(ne, 1, n), xn, wg, wu, wd, cap)
    pos_t = pos.T
    return lax.cond(jnp.max(span) <= COMBINE_FAST_ROWS,
                    lambda: _combine(base, spill, pos_t, aff_n, xm, y, COMBINE_FAST_ROWS),
                    lambda: _combine(base, spill, pos_t, aff_n, xm, y, TOK_BLOCK))


def _slot_cols(w, head_width):
    k = w.shape[0]
    w3 = w.reshape(k, N_HEADS, head_width)
    return jnp.pad(w3, ((0, 0), (0, 0), (0, SLOT - head_width))).reshape(k, N_HEADS * SLOT)


def _attn_weights(q_a_norm, w_uq, kv_a_norm, w_ukv, q_norm, k_norm):
    wkv = w_ukv.reshape(KV_LORA, N_HEADS, QK_NOPE + V_HEAD)
    wk = _slot_cols(wkv[:, :, :QK_NOPE].reshape(KV_LORA, -1), QK_NOPE)
    wv = wkv[:, :, QK_NOPE:].reshape(KV_LORA, D_ATTN)
    pad_gain = lambda g: jnp.pad(g, (0, SLOT - QK_HEAD))[None, :]
    return dict(wq=_slot_cols(w_uq, QK_HEAD).astype(BF16), wk=wk.astype(BF16), wv=wv.astype(BF16),
                qa=q_a_norm[None, :], kva=kv_a_norm[None, :], qg=pad_gain(q_norm), kg=pad_gain(k_norm))


def _rope_tables(seq):
    pos = jnp.arange(seq, dtype=F32)
    inv_freq = ROPE_THETA ** (-jnp.arange(0, QK_ROPE, 2, dtype=F32) / QK_ROPE)
    ang = pos[:, None] * inv_freq
    ang = jnp.concatenate([ang, ang], axis=-1)
    pad = lambda t, fill: jnp.concatenate(
        [jnp.full((seq, QK_NOPE), fill, F32), t, jnp.full((seq, SLOT - QK_HEAD), fill, F32)], axis=1)
    return pad(jnp.cos(ang), 1.0), pad(jnp.sin(ang), 0.0)


def _hi_lo(w):
    hi = w.astype(BF16)
    return jnp.stack([hi, (w - hi.astype(F32)).astype(BF16)], axis=1)


def _filter_weights(w1, b1, w2, b2, w3, freq, decay):
    hid, nbands = FILTER_HIDDEN, FILTER_BANDS
    assert 2 * hid == LANES and 2 * nbands <= hid
    both = lambda row: jnp.concatenate([row, row])[None, :]
    blockdiag = lambda a: jnp.concatenate(
        [jnp.concatenate([a, jnp.zeros_like(a)], axis=1), jnp.concatenate([jnp.zeros_like(a), a], axis=1)], axis=0)
    bands = jnp.linspace(1e-4, nbands - 1, nbands, dtype=F32)
    rest = jnp.zeros((hid - 2 * nbands,), F32)
    band_half = jnp.concatenate([bands, bands, rest])
    phase_half = jnp.concatenate([jnp.zeros((nbands,), F32), jnp.full((nbands,), 0.5 * math.pi, F32), rest])
    w1_half = jnp.concatenate([-w1[1 + nbands:], w1[1:1 + nbands], jnp.zeros((hid - 2 * nbands, hid), F32)], axis=0)
    cols = HYENA_ORDER * D_HYENA
    w3d = jnp.transpose(w3.reshape(hid, N_DIR, cols), (1, 0, 2))
    w3p = jnp.stack([jnp.pad(w3d[0], ((0, hid), (0, 0))), jnp.pad(w3d[1], ((hid, 0), (0, 0)))], axis=0)
    w3_hi = w3p.astype(BF16)
    w3_lo = (w3p - w3_hi.astype(F32)).astype(BF16)
    return dict(bands=both(band_half), phase=both(phase_half), w1t=both(w1[0]), w1=blockdiag(w1_half),
                b1=both(b1), w2=blockdiag(w2), b2=both(b2), freq=both(freq), w3_hi=w3_hi, w3_lo=w3_lo,
                decay=decay.reshape(N_DIR, 1, cols))


def _trunk(x, p):
    bsz, seq, _ = x.shape
    n = bsz * seq
    cos_t, sin_t = _rope_tables(seq)
    x2 = x.reshape(n, D_MODEL)
    for l in range(DEPTH):
        uh, cq, ckvr = _inproj(x2, p["attn_norm"][l][None, :], p["w_in"][l])
        y_h = _hyena(uh.reshape(bsz, seq, COL_HYENA), p["conv_w"][l], p["conv_b"][l][None, :],
                     p["filt"][l], p["hyena_bias"][l])
        qt, k, vt = _qkv(cq.reshape(bsz, seq, -1), ckvr.reshape(bsz, seq, -1), cos_t, sin_t, p["attn"][l])
        y_a = _flash(qt, k, vt)
        xm, xn, aff_t, aff_n = _outproj(y_h.reshape(n, D_HYENA), y_a.reshape(n, D_ATTN), x2,
                                 p["out_norm"][l][None, :], p["w_out"][l], p["mlp_norm"][l][None, :],
                                 p["w_router"][l])
        x2 = _ec_moe(xm, xn, aff_t, aff_n, p["w_gate"][l], p["w_up"][l], p["w_down"][l])
    return x2.reshape(bsz, seq, D_MODEL)


def kernel(x_prompt, x_sample, attn_norm, w_in, conv_w, conv_b, filt_w1, filt_b1, filt_w2, filt_b2, filt_w3,
           filt_freq, filt_decay, hyena_bias, q_a_norm, w_uq, kv_a_norm, w_ukv, q_norm, k_norm, out_norm,
           w_out, mlp_norm, w_router, w_gate, w_up, w_down):
    p = dict(
        attn_norm=attn_norm,
        w_in=jnp.pad(w_in, ((0, 0), (0, 0), (0, D_IN_PAD - D_IN))).astype(BF16),
        conv_w=conv_w, conv_b=conv_b, hyena_bias=hyena_bias,
        filt=[_filter_weights(filt_w1[l], filt_b1[l], filt_w2[l], filt_b2[l], filt_w3[l], filt_freq[l],
                              filt_decay[l]) for l in range(DEPTH)],
        attn=[_attn_weights(q_a_norm[l], w_uq[l], kv_a_norm[l], w_ukv[l], q_norm[l], k_norm[l])
              for l in range(DEPTH)],
        out_norm=out_norm, w_out=w_out.astype(BF16), mlp_norm=mlp_norm,
        w_router=_hi_lo(jnp.pad(w_router, ((0, 0), (0, 0), (0, LANES - N_EXPERTS)))),
        w_gate=w_gate.astype(BF16), w_up=w_up.astype(BF16), w_down=w_down.astype(BF16))
    return (_trunk(x_prompt, p), _trunk(x_sample, p))
```

```python
import functools
import math

import jax
import jax.numpy as jnp
from jax import lax
from jax.experimental import pallas as pl
from jax.experimental.pallas import tpu as pltpu

F32 = jnp.float32
BF16 = jnp.bfloat16
I32 = jnp.int32
HIGHEST = lax.Precision.HIGHEST

D_MODEL = 1024
DEPTH = 2
D_HYENA = 512
HYENA_ORDER = 2
FILTER_BANDS = 16
FILTER_HIDDEN = 64
N_DIR = 2
DECAY_SHIFT = 0.05
N_HEADS = 8
QK_NOPE = 64
QK_ROPE = 32
QK_HEAD = QK_NOPE + QK_ROPE
V_HEAD = 64
V_AUG = V_HEAD + 16
D_ATTN = N_HEADS * V_HEAD
Q_LORA = 256
KV_LORA = 128
ROPE_THETA = 10000.0
N_EXPERTS = 16
EC_CAPACITY = 2
D_EXPERT = 512
EPS = 1e-6
COL_HYENA = (HYENA_ORDER + 1) * D_HYENA
COL_Q = COL_HYENA + Q_LORA
COL_KV = COL_Q + KV_LORA
D_IN = COL_KV + QK_ROPE
D_IN_PAD = 2048

LANES = 128
SLOT = 128
TOK_BLOCK = 256
SLOT_TILE = 256
ALIGN = 16
GATHER_ROWS = 64
COMBINE_FAST_ROWS = 128
ATT_TQ = 512
ATT_QC = 256
ATT_TK = 256
ATT_UNROLL = 16
MIB = 1024 * 1024


def _params(sem, vmem_mib):
    return pltpu.CompilerParams(dimension_semantics=sem, vmem_limit_bytes=vmem_mib * MIB)


def _rms(x):
    return x * lax.rsqrt(jnp.mean(x * x, axis=-1, keepdims=True) + EPS)


def _inproj_body(x_ref, g_ref, w_ref, uh_ref, cq_ref, ckvr_ref):
    xn = _rms(x_ref[...]) * g_ref[...]
    u = jnp.dot(xn.astype(BF16), w_ref[...], preferred_element_type=F32)
    uh_ref[...] = u[:, :COL_HYENA]
    cq_ref[...] = u[:, COL_HYENA:COL_Q]
    ckvr_ref[...] = u[:, COL_Q:]


def _inproj(x2d, g, w_pad):
    n = x2d.shape[0]
    tm = min(512, n)
    return pl.pallas_call(
        _inproj_body,
        grid=(n // tm,),
        in_specs=[pl.BlockSpec((tm, D_MODEL), lambda i: (i, 0)),
                  pl.BlockSpec((1, D_MODEL), lambda i: (0, 0)),
                  pl.BlockSpec((D_MODEL, D_IN_PAD), lambda i: (0, 0))],
        out_specs=[pl.BlockSpec((tm, COL_HYENA), lambda i: (i, 0)),
                   pl.BlockSpec((tm, Q_LORA), lambda i: (i, 0)),
                   pl.BlockSpec((tm, D_IN_PAD - COL_Q), lambda i: (i, 0))],
        out_shape=[jax.ShapeDtypeStruct((n, COL_HYENA), F32),
                   jax.ShapeDtypeStruct((n, Q_LORA), F32),
                   jax.ShapeDtypeStruct((n, D_IN_PAD - COL_Q), F32)],
        compiler_params=_params(("parallel",), 40),
        name="inproj",
    )(x2d, g, w_pad)


def _sconv_body(u_ref, prev_ref, next_ref, w_ref, b_ref, v_ref, x1_ref, x2_ref):
    i = pl.program_id(1)
    last = pl.num_programs(1) - 1
    u = u_ref[0]
    tl = u.shape[0]
    prev_row = jnp.where(i == 0, 0.0, prev_ref[0, 7:8, :])
    next_row = jnp.where(i == last, 0.0, next_ref[0, 0:1, :])
    row = lax.broadcasted_iota(I32, u.shape, 0)
    up = jnp.where(row == 0, prev_row, pltpu.roll(u, 1, axis=0))
    dn = jnp.where(row == tl - 1, next_row, pltpu.roll(u, tl - 1, axis=0))
    y = up * w_ref[0:1, :] + u * w_ref[1:2, :] + dn * w_ref[2:3, :] + b_ref[...]
    rows, flat = v_ref.shape[1:]
    for k, o_ref in enumerate((v_ref, x1_ref, x2_ref)):
        o_ref[0] = y[:, k * D_HYENA:(k + 1) * D_HYENA].reshape(rows, tl // rows, D_HYENA).reshape(rows, flat)


def _sconv(uh, w, b, n2):
    bsz, seq, c = uh.shape
    rows = 8
    tl = rows * n2
    r = tl // 8
    nblk8 = seq // 8
    out = jax.ShapeDtypeStruct((bsz, seq // n2, n2 * D_HYENA), F32)
    ospec = pl.BlockSpec((1, rows, n2 * D_HYENA), lambda bi, i: (bi, i, 0))
    return pl.pallas_call(
        _sconv_body,
        grid=(bsz, seq // tl),
        in_specs=[pl.BlockSpec((1, tl, c), lambda bi, i: (bi, i, 0)),
                  pl.BlockSpec((1, 8, c), lambda bi, i: (bi, jnp.maximum(i * r - 1, 0), 0)),
                  pl.BlockSpec((1, 8, c), lambda bi, i: (bi, jnp.minimum((i + 1) * r, nblk8 - 1), 0)),
                  pl.BlockSpec((3, c), lambda bi, i: (0, 0)),
                  pl.BlockSpec((1, c), lambda bi, i: (0, 0))],
        out_specs=[ospec, ospec, ospec],
        out_shape=[out, out, out],
        compiler_params=_params(("parallel", "parallel"), 40),
        name="sconv",
    )(uh, uh, uh, w, b)


def _filter_body(bands_ref, phase_ref, w1t_ref, w1_ref, b1_ref, w2_ref, b2_ref, fr_ref, w3h_ref, w3l_ref,
                 dec_ref, k_ref, sum_ref, *, seq, n2):
    i = pl.program_id(0)
    hp = k_ref.shape[0] // 2
    cols = sum_ref.shape[1]
    jb = k_ref.shape[1] // cols
    row = lax.broadcasted_iota(I32, (jb * hp, 1), 0)
    n_lo = (row % hp) * n2 + i * jb + row // hp
    n_hi = n_lo + seq
    t_of = lambda n: jnp.where(n < seq, n, 2 * seq - n).astype(F32)
    lane = lax.broadcasted_iota(I32, (jb * hp, LANES), 1)
    t_idx = jnp.where(lane < FILTER_HIDDEN, t_of(n_lo), t_of(n_hi))
    feats = jnp.sin(2.0 * math.pi * bands_ref[...] * t_idx / seq + phase_ref[...])
    fr = fr_ref[...]
    pre = (t_idx / (seq - 1)) * w1t_ref[...] + jnp.dot(feats, w1_ref[...], precision=HIGHEST,
                                                      preferred_element_type=F32)
    h = jnp.sin(fr * (pre + b1_ref[...]))
    h = jnp.sin(fr * (jnp.dot(h, w2_ref[...], precision=HIGHEST, preferred_element_type=F32) + b2_ref[...]))
    h_hi = h.astype(BF16)
    h_lo = (h - h_hi.astype(F32)).astype(BF16)

    @pl.when(i == 0)
    def _():
        sum_ref[...] = jnp.zeros_like(sum_ref)

    for j, n in enumerate((n_lo, n_hi)):
        out = (jnp.dot(h_hi, w3h_ref[j], preferred_element_type=F32)
               + jnp.dot(h_lo, w3h_ref[j], preferred_element_type=F32)
               + jnp.dot(h_hi, w3l_ref[j], preferred_element_type=F32))
        window = jnp.exp(-(t_of(n) / (seq - 1)) * dec_ref[j]) + DECAY_SHIFT
        k = jnp.where(n == seq, 0.0, out * window)
        sum_ref[...] += jnp.sum(jnp.abs(k), axis=0, keepdims=True)
        for jj in range(jb):
            k_ref[j * hp:(j + 1) * hp, jj * cols:(jj + 1) * cols] = k[jj * hp:(jj + 1) * hp].astype(BF16)


def _filters(seq, n1, n2, fw):
    cols = HYENA_ORDER * D_HYENA
    const = lambda shape: pl.BlockSpec(shape, lambda i: (0,) * len(shape))
    jb = max(1, 1024 // n1)
    return pl.pallas_call(
        functools.partial(_filter_body, seq=seq, n2=n2),
        grid=(n2 // jb,),
        in_specs=[const((1, LANES)), const((1, LANES)), const((1, LANES)), const((LANES, LANES)),
                  const((1, LANES)), const((LANES, LANES)), const((1, LANES)), const((1, LANES)),
                  const((N_DIR, LANES, cols)), const((N_DIR, LANES, cols)), const((N_DIR, 1, cols))],
        out_specs=[pl.BlockSpec((n1, jb * cols), lambda i: (0, i)),
                   pl.BlockSpec((1, cols), lambda i: (0, 0))],
        out_shape=[jax.ShapeDtypeStruct((n1, n2 * cols), BF16),
                   jax.ShapeDtypeStruct((1, cols), F32)],
        compiler_params=_params(("arbitrary",), 32),
        name="filter_gen",
    )(fw["bands"], fw["phase"], fw["w1t"], fw["w1"], fw["b1"], fw["w2"], fw["b2"], fw["freq"],
      fw["w3_hi"], fw["w3_lo"], fw["decay"])


def _fft_dims(seq):
    n2 = 128 if 2 * seq >= 32768 else 64
    n1 = 2 * seq // n2
    return n1, n2


def _dft_tables(n1, n2):
    n = n1 * n2
    n1h = n1 // 2
    k1 = jnp.arange(n1h, dtype=I32)[:, None]
    m1 = jnp.arange(n1, dtype=I32)[None, :]
    ang = (2.0 * math.pi / n1) * ((k1 * m1) % n1).astype(F32)
    top = jnp.cos(ang)
    bot = -jnp.sin(ang)
    nyq = jnp.where(m1 % 2 == 0, 1.0, -1.0).astype(F32)
    bot = jnp.concatenate([nyq, bot[1:]], axis=0)
    fa = jnp.concatenate([top, bot], axis=0)
    weight = jnp.where((jnp.arange(n1) % n1h) == 0, 1.0, 2.0).astype(F32) / n
    fi = (fa[:, :n1h] * weight[:, None]).T

    kk = jnp.arange(n1h + 1, dtype=I32)[:, None, None]
    k2 = jnp.arange(n2, dtype=I32)[None, :, None]
    m2 = jnp.arange(n2, dtype=I32)[None, None, :]
    phi = (2.0 * math.pi / n) * ((m2 * (kk + n1 * k2)) % n).astype(F32)
    gr = jnp.cos(phi)
    gi = -jnp.sin(phi)
    blk = jnp.concatenate([jnp.concatenate([gr, -gi], axis=2),
                           jnp.concatenate([gi, gr], axis=2)], axis=1)
    left = (jnp.arange(2 * n2) < n2)[None, :]
    g0 = jnp.where(left, blk[0], 0.0)
    gf = jnp.concatenate([g0[None], blk[1:n1h]], axis=0)
    gnyq = jnp.concatenate([jnp.zeros((2 * n2, n2), F32), blk[n1h][:, :n2]], axis=1)
    return dict(fa_full=fa.astype(BF16), fa_half=fa[:, :n1h].astype(BF16), fi=fi.astype(BF16),
                gf=gf.astype(BF16), gnyq=gnyq.astype(BF16),
                ginv=jnp.transpose(gf, (0, 2, 1)).astype(BF16), m2=gnyq.T.astype(BF16))


def _fft_a_body(x_ref, f_ref, o_ref):
    a = jnp.dot(f_ref[...], x_ref[0].astype(BF16), preferred_element_type=F32)
    n1h = o_ref.shape[2]
    o_ref[0, 0] = a[:n1h].astype(BF16)
    o_ref[0, 1] = a[n1h:].astype(BF16)


def _fft_a(x3, fmat):
    bsz, r, nc = x3.shape
    n1 = fmat.shape[0]
    tn = min(8192, nc)
    return pl.pallas_call(
        _fft_a_body,
        grid=(bsz, nc // tn),
        in_specs=[pl.BlockSpec((1, r, tn), lambda b, j: (b, 0, j)),
                  pl.BlockSpec((n1, r), lambda b, j: (0, 0))],
        out_specs=pl.BlockSpec((1, 2, n1 // 2, tn), lambda b, j: (b, 0, 0, j)),
        out_shape=jax.ShapeDtypeStruct((bsz, 2, n1 // 2, nc), BF16),
        compiler_params=_params(("parallel", "parallel"), 48),
        name="fft_stage_a",
    )(x3, fmat)


def _fft_b_body(ar_ref, ai_ref, g_ref, gn_ref, sc_ref, x_ref, xn_ref, *, kb, n2):
    j = pl.program_id(1)
    c = sc_ref.shape[1]
    inv = 1.0 / sc_ref[...]
    a_re = ar_ref[0, 0].reshape(kb, n2, c)
    a_im = ai_ref[0, 0].reshape(kb, n2, c)
    for kk in range(kb):
        rhs = jnp.concatenate([a_re[kk], a_im[kk]], axis=0)
        x_ref[0, kk] = (jnp.dot(g_ref[kk], rhs, preferred_element_type=F32) * inv).astype(BF16)

    @pl.when(j == 0)
    def _():
        rhs = jnp.concatenate([a_re[0], a_im[0]], axis=0)
        xn_ref[0] = (jnp.dot(gn_ref[...], rhs, preferred_element_type=F32) * inv).astype(BF16)


def _fft_b(a4, tabs, scale):
    bsz, _, n1h, nc = a4.shape
    c = scale.shape[1]
    n2 = nc // c
    kb = 16
    return pl.pallas_call(
        functools.partial(_fft_b_body, kb=kb, n2=n2),
        grid=(bsz, n1h // kb),
        in_specs=[pl.BlockSpec((1, 1, kb, nc), lambda b, j: (b, 0, j, 0)),
                  pl.BlockSpec((1, 1, kb, nc), lambda b, j: (b, 1, j, 0)),
                  pl.BlockSpec((kb, 2 * n2, 2 * n2), lambda b, j: (j, 0, 0)),
                  pl.BlockSpec((2 * n2, 2 * n2), lambda b, j: (0, 0)),
                  pl.BlockSpec((1, c), lambda b, j: (0, 0))],
        out_specs=[pl.BlockSpec((1, kb, 2 * n2, c), lambda b, j: (b, j, 0, 0)),
                   pl.BlockSpec((1, 2 * n2, c), lambda b, j: (b, 0, 0))],
        out_shape=[jax.ShapeDtypeStruct((bsz, n1h, 2 * n2, c), BF16),
                   jax.ShapeDtypeStruct((bsz, 2 * n2, c), BF16)],
        compiler_params=_params(("parallel", "arbitrary"), 48),
        name="fft_stage_b",
    )(a4, a4, tabs["gf"], tabs["gnyq"], scale)


def _cmul(x, k, n2):
    xr, xi = x[:n2], x[n2:]
    kr, ki = k[:n2], k[n2:]
    return jnp.concatenate([xr * kr - xi * ki, xr * ki + xi * kr], axis=0).astype(BF16)


def _spec_body(ar_ref, ai_ref, k_ref, kn_ref, g_ref, gn_ref, gi_ref, m2_ref, o_ref, *, kb, n2):
    j = pl.program_id(1)
    c = ar_ref.shape[3] // n2
    a_re = ar_ref[0, 0].reshape(kb, n2, c)
    a_im = ai_ref[0, 0].reshape(kb, n2, c)

    def through(fwd, rhs, kf, inv):
        x = jnp.dot(fwd, rhs, preferred_element_type=F32)
        return jnp.dot(inv, _cmul(x, kf.astype(F32), n2), preferred_element_type=F32)

    re_rows, im_rows = [], []
    for kk in range(kb):
        rhs = jnp.concatenate([a_re[kk], a_im[kk]], axis=0)
        acc = through(g_ref[kk], rhs, k_ref[0, kk], gi_ref[kk])
        if kk == 0:
            nyq = through(gn_ref[...], rhs, kn_ref[0], m2_ref[...])
            acc = jnp.where(j == 0, jnp.concatenate([acc[:n2], nyq[n2:]], axis=0), acc)
        re_rows.append(acc[:n2].astype(BF16))
        im_rows.append(acc[n2:].astype(BF16))
    o_ref[0, 0] = jnp.stack(re_rows, axis=0).reshape(kb, n2 * c)
    o_ref[0, 1] = jnp.stack(im_rows, axis=0).reshape(kb, n2 * c)


def _spec_conv(a4, kf, kfnyq, order, tabs):
    bsz, _, n1h, nc = a4.shape
    tn2 = kf.shape[2]
    n2 = tn2 // 2
    c = nc // n2
    kb = 16
    return pl.pallas_call(
        functools.partial(_spec_body, kb=kb, n2=n2),
        grid=(bsz, n1h // kb),
        in_specs=[pl.BlockSpec((1, 1, kb, nc), lambda b, j: (b, 0, j, 0)),
                  pl.BlockSpec((1, 1, kb, nc), lambda b, j: (b, 1, j, 0)),
                  pl.BlockSpec((1, kb, tn2, c), lambda b, j: (0, j, 0, order)),
                  pl.BlockSpec((1, tn2, c), lambda b, j: (0, 0, order)),
                  pl.BlockSpec((kb, tn2, tn2), lambda b, j: (j, 0, 0)),
                  pl.BlockSpec((tn2, tn2), lambda b, j: (0, 0)),
                  pl.BlockSpec((kb, tn2, tn2), lambda b, j: (j, 0, 0)),
                  pl.BlockSpec((tn2, tn2), lambda b, j: (0, 0))],
        out_specs=pl.BlockSpec((1, 2, kb, nc), lambda b, j: (b, 0, j, 0)),
        out_shape=jax.ShapeDtypeStruct((bsz, 2, n1h, nc), BF16),
        compiler_params=_params(("parallel", "arbitrary"), 48),
        name="spectral_conv",
    )(a4, a4, kf, kfnyq, tabs["gf"], tabs["gnyq"], tabs["ginv"], tabs["m2"])


def _ifft_a_body(b_ref, f_ref, z_ref, gate_ref, bias_ref, o_ref):
    y = jnp.dot(f_ref[...], b_ref[0], preferred_element_type=F32)
    res = gate_ref[0] * (y + z_ref[0] * bias_ref[...])
    o_ref[0] = res.reshape(o_ref.shape[1:])


def _ifft_a(b3, fi, z3, gate3, bias_t, time_major):
    bsz, n1, nc = b3.shape
    n1h = n1 // 2
    tn = bias_t.shape[1]
    c = D_HYENA
    if time_major:
        out_spec = pl.BlockSpec((1, n1h, tn // c, c), lambda b, j: (b, 0, j, 0))
        out_shape = jax.ShapeDtypeStruct((bsz, n1h, nc // c, c), F32)
    else:
        out_spec = pl.BlockSpec((1, n1h, tn), lambda b, j: (b, 0, j))
        out_shape = jax.ShapeDtypeStruct((bsz, n1h, nc), F32)
    return pl.pallas_call(
        _ifft_a_body,
        grid=(bsz, nc // tn),
        in_specs=[pl.BlockSpec((1, n1, tn), lambda b, j: (b, 0, j)),
                  pl.BlockSpec((n1h, n1), lambda b, j: (0, 0)),
                  pl.BlockSpec((1, n1h, tn), lambda b, j: (b, 0, j)),
                  pl.BlockSpec((1, n1h, tn), lambda b, j: (b, 0, j)),
                  pl.BlockSpec((1, tn), lambda b, j: (0, 0))],
        out_specs=out_spec,
        out_shape=out_shape,
        compiler_params=_params(("parallel", "parallel"), 48),
        name="ifft_stage_a",
    )(b3, fi, z3, gate3, bias_t)


def _hyena(uh, conv_w, conv_b, fw, bias):
    bsz, seq, _ = uh.shape
    c = D_HYENA
    n1, n2 = _fft_dims(seq)
    n1h = n1 // 2
    tabs = _dft_tables(n1, n2)
    v, x1, x2 = _sconv(uh, conv_w, conv_b, n2)

    kcirc, ksum = _filters(seq, n1, n2, fw)
    cols = HYENA_ORDER * c
    ka = _fft_a(kcirc[None], tabs["fa_full"])
    kf, kfnyq = _fft_b(ka, tabs, ksum)

    tn = min(8192, n2 * c)
    z = v
    for order, gate in enumerate((x1, x2)):
        za = _fft_a(z, tabs["fa_half"])
        zb = _spec_conv(za, kf, kfnyq, order, tabs)
        bias_t = jnp.tile(bias[order][None, :], (1, tn // c))
        z = _ifft_a(zb.reshape(bsz, n1, n2 * c), tabs["fi"], z, gate, bias_t,
                    time_major=order == HYENA_ORDER - 1)
    return z.reshape(bsz, seq, c)


def _qkv_body(cq_ref, ckvr_ref, cos_ref, sin_ref, wq_ref, wk_ref, wv_ref, qa_ref, kva_ref, qg_ref, kg_ref,
              qt_ref, k_ref, vt_ref):
    cqn = _rms(cq_ref[0]) * qa_ref[...]
    qs = jnp.dot(cqn.astype(BF16), wq_ref[...], preferred_element_type=F32)
    ck = ckvr_ref[0]
    ckvn = (_rms(ck[:, :KV_LORA]) * kva_ref[...]).astype(BF16)
    ks = jnp.dot(ckvn, wk_ref[...], preferred_element_type=F32)
    vs = jnp.dot(ckvn, wv_ref[...], preferred_element_type=F32)
    krope = pltpu.roll(ck[:, KV_LORA:], QK_NOPE, axis=1)
    cos = cos_ref[...]
    sin = sin_ref[...]
    lane = lax.broadcasted_iota(I32, cos.shape, 1)
    half = QK_ROPE // 2
    first = (lane >= QK_NOPE) & (lane < QK_NOPE + half)
    second = (lane >= QK_NOPE + half) & (lane < QK_HEAD)

    def head(xh, gain, scale):
        ms = jnp.sum(xh * xh, axis=-1, keepdims=True) * (1.0 / QK_HEAD)
        xn = xh * lax.rsqrt(ms + EPS) * gain
        rot = jnp.where(first, -pltpu.roll(xn, SLOT - half, axis=1),
                        jnp.where(second, pltpu.roll(xn, half, axis=1), 0.0))
        return (xn * cos + rot * sin) * scale

    q_scale = QK_HEAD ** -0.5 * math.log2(math.e)
    eye = (lax.broadcasted_iota(I32, (SLOT, SLOT), 0) == lax.broadcasted_iota(I32, (SLOT, SLOT), 1)).astype(BF16)
    transpose = lambda x: lax.dot_general(eye, x.astype(BF16), (((1,), (1,)), ((), ())),
                                          preferred_element_type=F32).astype(BF16)
    for h in range(N_HEADS):
        sl = slice(h * SLOT, (h + 1) * SLOT)
        qt_ref[0, h] = transpose(head(qs[:, sl], qg_ref[...], q_scale))
        k_ref[0, h] = head(ks[:, sl] + krope, kg_ref[...], 1.0).astype(BF16)
    aug = lax.broadcasted_iota(I32, (V_AUG - V_HEAD, cos.shape[0]), 0)
    ones_row = jnp.where(aug == 0, 1.0, 0.0).astype(BF16)
    for hp in range(N_HEADS // 2):
        pair_t = transpose(vs[:, hp * 2 * V_HEAD:(hp + 1) * 2 * V_HEAD])
        vt_ref[0, hp] = jnp.concatenate([pair_t[:V_HEAD], ones_row, pair_t[V_HEAD:], ones_row], axis=0)


def _qkv(cq, ckvr, cos_t, sin_t, aw):
    bsz, seq, _ = cq.shape
    tm = min(512, seq)
    const = lambda shape: pl.BlockSpec(shape, lambda b, i: (0,) * len(shape))
    return pl.pallas_call(
        _qkv_body,
        grid=(bsz, seq // tm),
        in_specs=[pl.BlockSpec((1, tm, Q_LORA), lambda b, i: (b, i, 0)),
                  pl.BlockSpec((1, tm, D_IN_PAD - COL_Q), lambda b, i: (b, i, 0)),
                  pl.BlockSpec((tm, SLOT), lambda b, i: (i, 0)),
                  pl.BlockSpec((tm, SLOT), lambda b, i: (i, 0)),
                  const((Q_LORA, N_HEADS * SLOT)), const((KV_LORA, N_HEADS * SLOT)),
                  const((KV_LORA, D_ATTN)),
                  const((1, Q_LORA)), const((1, KV_LORA)), const((1, SLOT)), const((1, SLOT))],
        out_specs=[pl.BlockSpec((1, N_HEADS, SLOT, tm), lambda b, i: (b, 0, 0, i)),
                   pl.BlockSpec((1, N_HEADS, tm, SLOT), lambda b, i: (b, 0, i, 0)),
                   pl.BlockSpec((1, N_HEADS // 2, 2 * V_AUG, tm), lambda b, i: (b, 0, 0, i))],
        out_shape=[jax.ShapeDtypeStruct((bsz, N_HEADS, SLOT, seq), BF16),
                   jax.ShapeDtypeStruct((bsz, N_HEADS, seq, SLOT), BF16),
                   jax.ShapeDtypeStruct((bsz, N_HEADS // 2, 2 * V_AUG, seq), BF16)],
        compiler_params=_params(("parallel", "parallel"), 40),
        name="qkv_prep",
    )(cq, ckvr, cos_t, sin_t, aw["wq"], aw["wk"], aw["wv"], aw["qa"], aw["kva"], aw["qg"], aw["kg"])


def _flash_body(qt_ref, k_ref, vt_ref, o_ref, s0_ref, s1_ref, acc_ref, *, tk, nk, nqc, unroll):
    chains = [(hh, qc) for hh in range(2) for qc in range(nqc)]
    acc_ref[...] = jnp.zeros(acc_ref.shape, F32)

    def qk(t, s_ref):
        ks = pl.multiple_of(t * tk, tk)
        tile_max = []
        for c, (hh, qc) in enumerate(chains):
            k = k_ref[0, hh, pl.ds(ks, tk), :]
            s = jnp.dot(k, qt_ref[0, hh, :, qc * ATT_QC:(qc + 1) * ATT_QC], preferred_element_type=F32)
            s_ref[c] = s
            tile_max.append(jnp.max(s, axis=0, keepdims=True))
        return tile_max

    def softmax_pv(t, s_ref, tile_max, m):
        ks = pl.multiple_of(t * tk, tk)
        m_out = []
        for c, (hh, qc) in enumerate(chains):
            m_new = jnp.maximum(m[c], tile_max[c])
            a = jnp.exp2(m[c] - m_new)
            p = jnp.exp2(s_ref[c] - m_new)
            m_out.append(m_new)
            vt = vt_ref[0, 0, hh * V_AUG:(hh + 1) * V_AUG, pl.ds(ks, tk)]
            acc_ref[c] = acc_ref[c] * a + jnp.dot(vt, p.astype(BF16), preferred_element_type=F32)
        return m_out

    def group(u, carry):
        tile_max, m = carry
        for i in range(0, unroll, 2):
            t = unroll * u + i
            tile_max1 = qk(t + 1, s1_ref)
            m = softmax_pv(t, s0_ref, tile_max, m)
            tile_max = qk(jnp.minimum(t + 2, nk - 1), s0_ref)
            m = softmax_pv(t + 1, s1_ref, tile_max1, m)
        return tile_max, m

    init = (qk(0, s0_ref), [jnp.full((1, ATT_QC), -jnp.inf, F32)] * len(chains))
    lax.fori_loop(0, nk // unroll, group, init)

    def normalised(c):
        acc = acc_ref[c]
        return acc[:V_HEAD] * (1.0 / acc[V_HEAD:V_HEAD + 1])

    heads = [jnp.concatenate([normalised(hh * nqc + qc) for qc in range(nqc)], axis=1) for hh in range(2)]
    o_ref[0] = jnp.concatenate(heads, axis=0).T


def _flash(qt, k, vt):
    bsz, nh, _, seq = qt.shape
    tq = min(ATT_TQ, seq)
    tk = min(ATT_TK, seq)
    nqc = tq // ATT_QC
    nk = seq // tk
    unroll = min(ATT_UNROLL, nk)
    assert nk % unroll == 0 and unroll % 2 == 0
    return pl.pallas_call(
        functools.partial(_flash_body, tk=tk, nk=nk, nqc=nqc, unroll=unroll),
        grid=(bsz, nh // 2, seq // tq),
        in_specs=[pl.BlockSpec((1, 2, SLOT, tq), lambda b, hp, i: (b, hp, 0, i)),
                  pl.BlockSpec((1, 2, seq, SLOT), lambda b, hp, i: (b, hp, 0, 0)),
                  pl.BlockSpec((1, 1, 2 * V_AUG, seq), lambda b, hp, i: (b, hp, 0, 0))],
        out_specs=pl.BlockSpec((1, tq, 2 * V_HEAD), lambda b, hp, i: (b, i, hp)),
        out_shape=jax.ShapeDtypeStruct((bsz, seq, nh * V_HEAD), F32),
        scratch_shapes=[pltpu.VMEM((2 * nqc, tk, ATT_QC), F32), pltpu.VMEM((2 * nqc, tk, ATT_QC), F32),
                        pltpu.VMEM((2 * nqc, V_AUG, ATT_QC), F32)],
        compiler_params=_params(("parallel", "parallel", "parallel"), 48),
        name="flash_attn",
    )(qt, k, vt)


def _outproj_body(yh_ref, ya_ref, x_ref, og_ref, w_ref, mg_ref, wr_ref, xm_ref, xn_ref, aff_ref, affn_ref):
    og = og_ref[...]
    half = D_MODEL // 2
    y = jnp.concatenate([_rms(yh_ref[...]) * og[:, :half], _rms(ya_ref[...]) * og[:, half:]], axis=1)
    xm = x_ref[...] + jnp.dot(y.astype(BF16), w_ref[...], preferred_element_type=F32)
    xm_ref[...] = xm
    xn = _rms(xm) * mg_ref[...]
    xn_hi = xn.astype(BF16)
    xn_ref[...] = xn_hi
    xn_lo = (xn - xn_hi.astype(F32)).astype(BF16)
    logits = (jnp.dot(xn_hi, wr_ref[0], preferred_element_type=F32)
              + jnp.dot(xn_lo, wr_ref[0], preferred_element_type=F32)
              + jnp.dot(xn_hi, wr_ref[1], preferred_element_type=F32))
    lane = lax.broadcasted_iota(I32, logits.shape, 1)
    logits = jnp.where(lane < N_EXPERTS, logits, -jnp.inf)
    e = jnp.exp(logits - jnp.max(logits, axis=-1, keepdims=True))
    aff = e / jnp.sum(e, axis=-1, keepdims=True)
    aff_ref[...] = aff.T[:N_EXPERTS]
    affn_ref[...] = aff[:, :N_EXPERTS]


def _outproj(yh, ya, x2d, og, w_out, mg, wr_pad):
    n = x2d.shape[0]
    tm = min(512, n)
    half = D_MODEL // 2
    const = lambda shape: pl.BlockSpec(shape, lambda i: (0,) * len(shape))
    return pl.pallas_call(
        _outproj_body,
        grid=(n // tm,),
        in_specs=[pl.BlockSpec((tm, half), lambda i: (i, 0)),
                  pl.BlockSpec((tm, half), lambda i: (i, 0)),
                  pl.BlockSpec((tm, D_MODEL), lambda i: (i, 0)),
                  const((1, D_MODEL)), const((D_MODEL, D_MODEL)), const((1, D_MODEL)),
                  const((2, D_MODEL, LANES))],
        out_specs=[pl.BlockSpec((tm, D_MODEL), lambda i: (i, 0)),
                   pl.BlockSpec((tm, D_MODEL), lambda i: (i, 0)),
                   pl.BlockSpec((N_EXPERTS, tm), lambda i: (0, i)),
                   pl.BlockSpec((tm, N_EXPERTS), lambda i: (i, 0))],
        out_shape=[jax.ShapeDtypeStruct((n, D_MODEL), F32),
                   jax.ShapeDtypeStruct((n, D_MODEL), BF16),
                   jax.ShapeDtypeStruct((N_EXPERTS, n), F32),
                   jax.ShapeDtypeStruct((n, N_EXPERTS), F32)],
        compiler_params=_params(("parallel",), 40),
        name="outproj_router",
    )(yh, ya, x2d, og, w_out, mg, wr_pad)


def _select_body(aff_ref, upper_ref, lower_ref, pos_ref, off_ref, *, cap):
    group = aff_ref.shape[0]
    bits = [pltpu.bitcast(aff_ref[x], I32) for x in range(group)]
    upper = upper_ref[...]
    lower = lower_ref[...]

    def count(mask):
        return jnp.sum(jnp.sum(mask.astype(F32), axis=1, keepdims=True), axis=0, keepdims=True)

    def bit_step(i, thr):
        bit = jnp.left_shift(jnp.int32(1), 30 - i)
        return tuple(jnp.where(count(bits[x] >= (thr[x] | bit)) >= cap, thr[x] | bit, thr[x])
                     for x in range(group))

    thr = lax.fori_loop(0, 31, bit_step, tuple(jnp.zeros((1, 1), I32) for _ in range(group)))

    def prefix(mask):
        within = jnp.dot(mask.astype(BF16), upper, preferred_element_type=F32)
        total = within[:, LANES - 1:LANES]
        offs = jnp.dot(lower, jnp.broadcast_to(total, within.shape).astype(BF16), preferred_element_type=F32)
        return within, offs

    for x in range(group):
        gt = bits[x] > thr[x]
        eq = bits[x] == thr[x]
        need = cap - count(gt)
        w_eq, o_eq = prefix(eq)
        sel = gt | (eq & (o_eq + w_eq <= need))
        w_sel, o_sel = prefix(sel)
        pos_ref[x] = jnp.where(sel, o_sel + w_sel - 1.0, -1.0).astype(I32)
        off_ref[x] = o_sel.astype(I32)


def _select(aff_t, cap):
    ne, n = aff_t.shape
    rows = n // LANES
    upper = (jnp.arange(LANES)[:, None] <= jnp.arange(LANES)[None, :]).astype(BF16)
    lower = (jnp.arange(rows)[None, :] < jnp.arange(rows)[:, None]).astype(BF16)
    group = 4
    blk = pl.BlockSpec((group, rows, LANES), lambda e: (e, 0, 0))
    pos, off = pl.pallas_call(
        functools.partial(_select_body, cap=cap),
        grid=(ne // group,),
        in_specs=[blk, pl.BlockSpec((LANES, LANES), lambda e: (0, 0)),
                  pl.BlockSpec((rows, rows), lambda e: (0, 0))],
        out_specs=[blk, blk],
        out_shape=[jax.ShapeDtypeStruct((ne, rows, LANES), I32),
                   jax.ShapeDtypeStruct((ne, rows, LANES), I32)],
        compiler_params=_params(("parallel",), 32),
        name="ec_select",
    )(aff_t.reshape(ne, rows, LANES), upper, lower)
    return pos.reshape(ne, n), off[:, :, 0]


def _ffn_body(off_ref, nch_ref, pos_ref, x_ref, wg_ref, wu_ref, wd_ref, y_ref, xs_ref, *, nb, sub, cap):
    e = pl.program_id(0)
    sb = pl.program_id(1)

    @pl.when(sb == 0)
    def _():
        xs_ref[...] = jnp.zeros(xs_ref.shape, BF16)

    row = lax.broadcasted_iota(I32, (GATHER_ROWS, TOK_BLOCK), 0)

    def gather(i, c):
        blk = e * nb + sb * sub + i
        base = off_ref[blk] * ALIGN
        tok = slice(i * TOK_BLOCK, (i + 1) * TOK_BLOCK)
        start = pl.multiple_of(base + c * GATHER_ROWS, ALIGN)
        hit = (pos_ref[0, :, tok] - start) == row
        win = jnp.dot(hit.astype(BF16), x_ref[tok, :], preferred_element_type=F32)
        xs_ref[pl.ds(start, GATHER_ROWS), :] += win.astype(BF16)

    for i in range(sub):
        gather(i, 0)
    for i in range(sub):
        lax.fori_loop(1, nch_ref[e * nb + sb * sub + i], lambda c, carry, i=i: (gather(i, c), carry)[1], 0)

    @pl.when(sb == pl.num_programs(1) - 1)
    def _():
        for j in range(cap // SLOT_TILE):
            rows = slice(j * SLOT_TILE, (j + 1) * SLOT_TILE)
            xt = xs_ref[rows, :]
            g = jnp.dot(xt, wg_ref[0], preferred_element_type=F32)
            u = jnp.dot(xt, wu_ref[0], preferred_element_type=F32)
            h = (g * jax.nn.sigmoid(g) * u).astype(BF16)
            y_ref[0, rows, :] = jnp.dot(h, wd_ref[0], preferred_element_type=F32).astype(BF16)
        y_ref[0, cap:, :] = jnp.zeros((y_ref.shape[1] - cap, D_MODEL), BF16)


def _ffn(base, nchunk, pos3, xn, wg, wu, wd, cap):
    ne = wg.shape[0]
    n = xn.shape[0]
    nb = n // TOK_BLOCK
    sub = min(8, nb)
    cap_pad = cap + TOK_BLOCK + ALIGN
    grid_spec = pltpu.PrefetchScalarGridSpec(
        num_scalar_prefetch=2,
        grid=(ne, nb // sub),
        in_specs=[pl.BlockSpec((1, 1, sub * TOK_BLOCK), lambda e, s, off, nch: (e, 0, s)),
                  pl.BlockSpec((sub * TOK_BLOCK, D_MODEL), lambda e, s, off, nch: (s, 0)),
                  pl.BlockSpec((1, D_MODEL, D_EXPERT), lambda e, s, off, nch: (e, 0, 0)),
                  pl.BlockSpec((1, D_MODEL, D_EXPERT), lambda e, s, off, nch: (e, 0, 0)),
                  pl.BlockSpec((1, D_EXPERT, D_MODEL), lambda e, s, off, nch: (e, 0, 0))],
        out_specs=pl.BlockSpec((1, cap_pad, D_MODEL), lambda e, s, off, nch: (e, 0, 0)),
        scratch_shapes=[pltpu.VMEM((cap_pad, D_MODEL), BF16)])
    return pl.pallas_call(
        functools.partial(_ffn_body, nb=nb, sub=sub, cap=cap),
        grid_spec=grid_spec,
        out_shape=jax.ShapeDtypeStruct((ne, cap_pad, D_MODEL), BF16),
        compiler_params=_params(("parallel", "arbitrary"), 48),
        name="ec_ffn",
    )(base, nchunk, pos3, xn, wg, wu, wd)


def _combine_body(off_ref, spill_ref, post_ref, affn_ref, xm_ref, *refs, nb, ne, win_rows):
    win_refs, tail_refs, o_ref = refs[:ne], refs[ne:-1], refs[-1]
    b = pl.program_id(0)
    post = post_ref[...]
    affn = affn_ref[...]

    def expand(y_refs, shift):
        width = y_refs[0].shape[0]
        lane = lax.broadcasted_iota(I32, (TOK_BLOCK, width), 1)
        total = jnp.zeros((TOK_BLOCK, D_MODEL), F32)
        for e in range(ne):
            rel = post[:, e:e + 1] - (off_ref[e * nb + b] * ALIGN + shift)
            hit = (rel == lane).astype(BF16)
            total = total + affn[:, e:e + 1] * jnp.dot(hit, y_refs[e][...], preferred_element_type=F32)
        return total

    def expand_pairs(y_refs):
        width = y_refs[0].shape[0]
        lane = lax.broadcasted_iota(I32, (TOK_BLOCK, 2 * width), 1)
        total = jnp.zeros((TOK_BLOCK, D_MODEL), F32)
        for e in range(0, ne, 2):
            rel0 = post[:, e:e + 1] - off_ref[e * nb + b] * ALIGN
            rel1 = post[:, e + 1:e + 2] - off_ref[(e + 1) * nb + b] * ALIGN + width
            gates = (jnp.where(rel0 == lane, affn[:, e:e + 1], 0.0)
                     + jnp.where((rel1 == lane) & (rel1 >= width), affn[:, e + 1:e + 2], 0.0))
            rows = jnp.concatenate([y_refs[e][...], y_refs[e + 1][...]], axis=0)
            total = total + jnp.dot(gates.astype(BF16), rows, preferred_element_type=F32)
        return total

    if not tail_refs and 2 * win_rows <= TOK_BLOCK and ne % 2 == 0:
        o_ref[...] = xm_ref[...] + expand_pairs(win_refs)
    else:
        o_ref[...] = xm_ref[...] + expand(win_refs, 0)

    if tail_refs:
        @pl.when(spill_ref[b] != 0)
        def _():
            o_ref[...] += expand(tail_refs, win_rows)


def _combine(base, spill, pos_t, aff_n, xm, y, win_rows):
    ne = y.shape[0]
    n = xm.shape[0]
    nb = n // TOK_BLOCK
    tail_rows = ALIGN if win_rows == TOK_BLOCK else 0

    def window(e, rows, shift):
        return pl.BlockSpec((pl.Squeezed(), pl.Element(rows), pl.Element(D_MODEL)),
                            lambda b, off, sp: (e, (off[e * nb + b] + shift // ALIGN) * ALIGN, 0))

    tails = [window(e, tail_rows, win_rows) for e in range(ne)] if tail_rows else []
    grid_spec = pltpu.PrefetchScalarGridSpec(
        num_scalar_prefetch=2,
        grid=(nb,),
        in_specs=([pl.BlockSpec((TOK_BLOCK, ne), lambda b, off, sp: (b, 0)),
                   pl.BlockSpec((TOK_BLOCK, ne), lambda b, off, sp: (b, 0)),
                   pl.BlockSpec((TOK_BLOCK, D_MODEL), lambda b, off, sp: (b, 0))]
                  + [window(e, win_rows, 0) for e in range(ne)] + tails),
        out_specs=pl.BlockSpec((TOK_BLOCK, D_MODEL), lambda b, off, sp: (b, 0)))
    return pl.pallas_call(
        functools.partial(_combine_body, nb=nb, ne=ne, win_rows=win_rows),
        grid_spec=grid_spec,
        out_shape=jax.ShapeDtypeStruct((n, D_MODEL), F32),
        compiler_params=_params(("parallel",), 48),
        name="ec_combine",
    )(base, spill, pos_t, aff_n, xm, *([y] * (ne + len(tails))))


def _ec_moe(xm, xn, aff_t, aff_n, wg, wu, wd):
    ne, n = aff_t.shape
    cap = max(1, EC_CAPACITY * n // N_EXPERTS)
    nb = n // TOK_BLOCK
    pos, off = _select(aff_t, cap)
    first = off[:, ::TOK_BLOCK // LANES]
    count = jnp.concatenate([first[:, 1:], jnp.full((ne, 1), cap, I32)], axis=1) - first
    base = first // ALIGN
    span = jnp.where(count > 0, first - base * ALIGN + count, 0)
    spill = jnp.any(span > TOK_BLOCK, axis=0).astype(I32)
    nchunk = ((span + GATHER_ROWS - 1) // GATHER_ROWS).reshape(ne * nb).astype(I32)
    base = base.reshape(ne * nb).astype(I32)
    y = _ffn(base, nchunk, pos.reshape(ne, 1, n), xn, wg, wu, wd, cap)
    pos_t = pos.T
    return lax.cond(jnp.max(span) <= COMBINE_FAST_ROWS,
                    lambda: _combine(base, spill, pos_t, aff_n, xm, y, COMBINE_FAST_ROWS),
                    lambda: _combine(base, spill, pos_t, aff_n, xm, y, TOK_BLOCK))


def _slot_cols(w, head_width):
    k = w.shape[0]
    w3 = w.reshape(k, N_HEADS, head_width)
    return jnp.pad(w3, ((0, 0), (0, 0), (0, SLOT - head_width))).reshape(k, N_HEADS * SLOT)


def _attn_weights(q_a_norm, w_uq, kv_a_norm, w_ukv, q_norm, k_norm):
    wkv = w_ukv.reshape(KV_LORA, N_HEADS, QK_NOPE + V_HEAD)
    wk = _slot_cols(wkv[:, :, :QK_NOPE].reshape(KV_LORA, -1), QK_NOPE)
    wv = wkv[:, :, QK_NOPE:].reshape(KV_LORA, D_ATTN)
    pad_gain = lambda g: jnp.pad(g, (0, SLOT - QK_HEAD))[None, :]
    return dict(wq=_slot_cols(w_uq, QK_HEAD).astype(BF16), wk=wk.astype(BF16), wv=wv.astype(BF16),
                qa=q_a_norm[None, :], kva=kv_a_norm[None, :], qg=pad_gain(q_norm), kg=pad_gain(k_norm))


def _rope_tables(seq):
    pos = jnp.arange(seq, dtype=F32)
    inv_freq = ROPE_THETA ** (-jnp.arange(0, QK_ROPE, 2, dtype=F32) / QK_ROPE)
    ang = pos[:, None] * inv_freq
    ang = jnp.concatenate([ang, ang], axis=-1)
    pad = lambda t, fill: jnp.concatenate(
        [jnp.full((seq, QK_NOPE), fill, F32), t, jnp.full((seq, SLOT - QK_HEAD), fill, F32)], axis=1)
    return pad(jnp.cos(ang), 1.0), pad(jnp.sin(ang), 0.0)


def _hi_lo(w):
    hi = w.astype(BF16)
    return jnp.stack([hi, (w - hi.astype(F32)).astype(BF16)], axis=1)


def _filter_weights(w1, b1, w2, b2, w3, freq, decay):
    hid, nbands = FILTER_HIDDEN, FILTER_BANDS
    assert 2 * hid == LANES and 2 * nbands <= hid
    both = lambda row: jnp.concatenate([row, row])[None, :]
    blockdiag = lambda a: jnp.concatenate(
        [jnp.concatenate([a, jnp.zeros_like(a)], axis=1), jnp.concatenate([jnp.zeros_like(a), a], axis=1)], axis=0)
    bands = jnp.linspace(1e-4, nbands - 1, nbands, dtype=F32)
    rest = jnp.zeros((hid - 2 * nbands,), F32)
    band_half = jnp.concatenate([bands, bands, rest])
    phase_half = jnp.concatenate([jnp.zeros((nbands,), F32), jnp.full((nbands,), 0.5 * math.pi, F32), rest])
    w1_half = jnp.concatenate([-w1[1 + nbands:], w1[1:1 + nbands], jnp.zeros((hid - 2 * nbands, hid), F32)], axis=0)
    cols = HYENA_ORDER * D_HYENA
    w3d = jnp.transpose(w3.reshape(hid, N_DIR, cols), (1, 0, 2))
    w3p = jnp.stack([jnp.pad(w3d[0], ((0, hid), (0, 0))), jnp.pad(w3d[1], ((hid, 0), (0, 0)))], axis=0)
    w3_hi = w3p.astype(BF16)
    w3_lo = (w3p - w3_hi.astype(F32)).astype(BF16)
    return dict(bands=both(band_half), phase=both(phase_half), w1t=both(w1[0]), w1=blockdiag(w1_half),
                b1=both(b1), w2=blockdiag(w2), b2=both(b2), freq=both(freq), w3_hi=w3_hi, w3_lo=w3_lo,
                decay=decay.reshape(N_DIR, 1, cols))


def _trunk(x, p):
    bsz, seq, _ = x.shape
    n = bsz * seq
    cos_t, sin_t = _rope_tables(seq)
    x2 = x.reshape(n, D_MODEL)
    for l in range(DEPTH):
        uh, cq, ckvr = _inproj(x2, p["attn_norm"][l][None, :], p["w_in"][l])
        y_h = _hyena(uh.reshape(bsz, seq, COL_HYENA), p["conv_w"][l], p["conv_b"][l][None, :],
                     p["filt"][l], p["hyena_bias"][l])
        qt, k, vt = _qkv(cq.reshape(bsz, seq, -1), ckvr.reshape(bsz, seq, -1), cos_t, sin_t, p["attn"][l])
        y_a = _flash(qt, k, vt)
        xm, xn, aff_t, aff_n = _outproj(y_h.reshape(n, D_HYENA), y_a.reshape(n, D_ATTN), x2,
                                 p["out_norm"][l][None, :], p["w_out"][l], p["mlp_norm"][l][None, :],
                                 p["w_router"][l])
        x2 = _ec_moe(xm, xn, aff_t, aff_n, p["w_gate"][l], p["w_up"][l], p["w_down"][l])
    return x2.reshape(bsz, seq, D_MODEL)


def kernel(x_prompt, x_sample, attn_norm, w_in, conv_w, conv_b, filt_w1, filt_b1, filt_w2, filt_b2, filt_w3,
           filt_freq, filt_decay, hyena_bias, q_a_norm, w_uq, kv_a_norm, w_ukv, q_norm, k_norm, out_norm,
           w_out, mlp_norm, w_router, w_gate, w_up, w_down):
    p = dict(
        attn_norm=attn_norm,
        w_in=jnp.pad(w_in, ((0, 0), (0, 0), (0, D_IN_PAD - D_IN))).astype(BF16),
        conv_w=conv_w, conv_b=conv_b, hyena_bias=hyena_bias,
        filt=[_filter_weights(filt_w1[l], filt_b1[l], filt_w2[l], filt_b2[l], filt_w3[l], filt_freq[l],
                              filt_decay[l]) for l in range(DEPTH)],
        attn=[_attn_weights(q_a_norm[l], w_uq[l], kv_a_norm[l], w_ukv[l], q_norm[l], k_norm[l])
              for l in range(DEPTH)],
        out_norm=out_norm, w_out=w_out.astype(BF16), mlp_norm=mlp_norm,
        w_router=_hi_lo(jnp.pad(w_router, ((0, 0), (0, 0), (0, LANES - N_EXPERTS)))),
        w_gate=w_gate.astype(BF16), w_up=w_up.astype(BF16), w_down=w_down.astype(BF16))
    return (_trunk(x_prompt, p), _trunk(x_sample, p))
```

```python
import functools
import math

import jax
import jax.numpy as jnp
from jax import lax
from jax.experimental import pallas as pl
from jax.experimental.pallas import tpu as pltpu

F32 = jnp.float32
BF16 = jnp.bfloat16
I32 = jnp.int32
HIGHEST = lax.Precision.HIGHEST

D_MODEL = 1024
DEPTH = 2
D_HYENA = 512
HYENA_ORDER = 2
FILTER_BANDS = 16
FILTER_HIDDEN = 64
N_DIR = 2
DECAY_SHIFT = 0.05
N_HEADS = 8
QK_NOPE = 64
QK_ROPE = 32
QK_HEAD = QK_NOPE + QK_ROPE
V_HEAD = 64
V_AUG = V_HEAD + 16
D_ATTN = N_HEADS * V_HEAD
Q_LORA = 256
KV_LORA = 128
ROPE_THETA = 10000.0
N_EXPERTS = 16
EC_CAPACITY = 2
D_EXPERT = 512
EPS = 1e-6
COL_HYENA = (HYENA_ORDER + 1) * D_HYENA
COL_Q = COL_HYENA + Q_LORA
COL_KV = COL_Q + KV_LORA
D_IN = COL_KV + QK_ROPE
D_IN_PAD = 2048

LANES = 128
SLOT = 128
TOK_BLOCK = 256
SLOT_TILE = 256
ALIGN = 16
GATHER_ROWS = 64
COMBINE_FAST_ROWS = 128
ATT_TQ = 512
ATT_QC = 256
ATT_TK = 256
ATT_UNROLL = 16
MIB = 1024 * 1024


def _params(sem, vmem_mib):
    return pltpu.CompilerParams(dimension_semantics=sem, vmem_limit_bytes=vmem_mib * MIB)


def _rms(x):
    return x * lax.rsqrt(jnp.mean(x * x, axis=-1, keepdims=True) + EPS)


def _inproj_body(x_ref, g_ref, w_ref, uh_ref, cq_ref, ckvr_ref):
    xn = _rms(x_ref[...]) * g_ref[...]
    u = jnp.dot(xn.astype(BF16), w_ref[...], preferred_element_type=F32)
    uh_ref[...] = u[:, :COL_HYENA]
    cq_ref[...] = u[:, COL_HYENA:COL_Q]
    ckvr_ref[...] = u[:, COL_Q:]


def _inproj(x2d, g, w_pad):
    n = x2d.shape[0]
    tm = min(512, n)
    return pl.pallas_call(
        _inproj_body,
        grid=(n // tm,),
        in_specs=[pl.BlockSpec((tm, D_MODEL), lambda i: (i, 0)),
                  pl.BlockSpec((1, D_MODEL), lambda i: (0, 0)),
                  pl.BlockSpec((D_MODEL, D_IN_PAD), lambda i: (0, 0))],
        out_specs=[pl.BlockSpec((tm, COL_HYENA), lambda i: (i, 0)),
                   pl.BlockSpec((tm, Q_LORA), lambda i: (i, 0)),
                   pl.BlockSpec((tm, D_IN_PAD - COL_Q), lambda i: (i, 0))],
        out_shape=[jax.ShapeDtypeStruct((n, COL_HYENA), F32),
                   jax.ShapeDtypeStruct((n, Q_LORA), F32),
                   jax.ShapeDtypeStruct((n, D_IN_PAD - COL_Q), F32)],
        compiler_params=_params(("parallel",), 40),
        name="inproj",
    )(x2d, g, w_pad)


def _sconv_body(u_ref, prev_ref, next_ref, w_ref, b_ref, v_ref, x1_ref, x2_ref):
    i = pl.program_id(1)
    last = pl.num_programs(1) - 1
    u = u_ref[0]
    tl = u.shape[0]
    prev_row = jnp.where(i == 0, 0.0, prev_ref[0, 7:8, :])
    next_row = jnp.where(i == last, 0.0, next_ref[0, 0:1, :])
    row = lax.broadcasted_iota(I32, u.shape, 0)
    up = jnp.where(row == 0, prev_row, pltpu.roll(u, 1, axis=0))
    dn = jnp.where(row == tl - 1, next_row, pltpu.roll(u, tl - 1, axis=0))
    y = up * w_ref[0:1, :] + u * w_ref[1:2, :] + dn * w_ref[2:3, :] + b_ref[...]
    rows, flat = v_ref.shape[1:]
    for k, o_ref in enumerate((v_ref, x1_ref, x2_ref)):
        o_ref[0] = y[:, k * D_HYENA:(k + 1) * D_HYENA].reshape(rows, tl // rows, D_HYENA).reshape(rows, flat)


def _sconv(uh, w, b, n2):
    bsz, seq, c = uh.shape
    rows = 8
    tl = rows * n2
    r = tl // 8
    nblk8 = seq // 8
    out = jax.ShapeDtypeStruct((bsz, seq // n2, n2 * D_HYENA), F32)
    ospec = pl.BlockSpec((1, rows, n2 * D_HYENA), lambda bi, i: (bi, i, 0))
    return pl.pallas_call(
        _sconv_body,
        grid=(bsz, seq // tl),
        in_specs=[pl.BlockSpec((1, tl, c), lambda bi, i: (bi, i, 0)),
                  pl.BlockSpec((1, 8, c), lambda bi, i: (bi, jnp.maximum(i * r - 1, 0), 0)),
                  pl.BlockSpec((1, 8, c), lambda bi, i: (bi, jnp.minimum((i + 1) * r, nblk8 - 1), 0)),
                  pl.BlockSpec((3, c), lambda bi, i: (0, 0)),
                  pl.BlockSpec((1, c), lambda bi, i: (0, 0))],
        out_specs=[ospec, ospec, ospec],
        out_shape=[out, out, out],
        compiler_params=_params(("parallel", "parallel"), 40),
        name="sconv",
    )(uh, uh, uh, w, b)


def _filter_body(bands_ref, phase_ref, w1t_ref, w1_ref, b1_ref, w2_ref, b2_ref, fr_ref, w3h_ref, w3l_ref,
                 dec_ref, k_ref, sum_ref, *, seq, n2):
    i = pl.program_id(0)
    hp = k_ref.shape[0] // 2
    cols = sum_ref.shape[1]
    jb = k_ref.shape[1] // cols
    row = lax.broadcasted_iota(I32, (jb * hp, 1), 0)
    n_lo = (row % hp) * n2 + i * jb + row // hp
    n_hi = n_lo + seq
    t_of = lambda n: jnp.where(n < seq, n, 2 * seq - n).astype(F32)
    lane = lax.broadcasted_iota(I32, (jb * hp, LANES), 1)
    t_idx = jnp.where(lane < FILTER_HIDDEN, t_of(n_lo), t_of(n_hi))
    feats = jnp.sin(2.0 * math.pi * bands_ref[...] * t_idx / seq + phase_ref[...])
    fr = fr_ref[...]
    pre = (t_idx / (seq - 1)) * w1t_ref[...] + jnp.dot(feats, w1_ref[...], precision=HIGHEST,
                                                      preferred_element_type=F32)
    h = jnp.sin(fr * (pre + b1_ref[...]))
    h = jnp.sin(fr * (jnp.dot(h, w2_ref[...], precision=HIGHEST, preferred_element_type=F32) + b2_ref[...]))
    h_hi = h.astype(BF16)
    h_lo = (h - h_hi.astype(F32)).astype(BF16)

    @pl.when(i == 0)
    def _():
        sum_ref[...] = jnp.zeros_like(sum_ref)

    for j, n in enumerate((n_lo, n_hi)):
        out = (jnp.dot(h_hi, w3h_ref[j], preferred_element_type=F32)
               + jnp.dot(h_lo, w3h_ref[j], preferred_element_type=F32)
               + jnp.dot(h_hi, w3l_ref[j], preferred_element_type=F32))
        window = jnp.exp(-(t_of(n) / (seq - 1)) * dec_ref[j]) + DECAY_SHIFT
        k = jnp.where(n == seq, 0.0, out * window)
        sum_ref[...] += jnp.sum(jnp.abs(k), axis=0, keepdims=True)
        for jj in range(jb):
            k_ref[j * hp:(j + 1) * hp, jj * cols:(jj + 1) * cols] = k[jj * hp:(jj + 1) * hp].astype(BF16)


def _filters(seq, n1, n2, fw):
    cols = HYENA_ORDER * D_HYENA
    const = lambda shape: pl.BlockSpec(shape, lambda i: (0,) * len(shape))
    jb = max(1, 1024 // n1)
    return pl.pallas_call(
        functools.partial(_filter_body, seq=seq, n2=n2),
        grid=(n2 // jb,),
        in_specs=[const((1, LANES)), const((1, LANES)), const((1, LANES)), const((LANES, LANES)),
                  const((1, LANES)), const((LANES, LANES)), const((1, LANES)), const((1, LANES)),
                  const((N_DIR, LANES, cols)), const((N_DIR, LANES, cols)), const((N_DIR, 1, cols))],
        out_specs=[pl.BlockSpec((n1, jb * cols), lambda i: (0, i)),
                   pl.BlockSpec((1, cols), lambda i: (0, 0))],
        out_shape=[jax.ShapeDtypeStruct((n1, n2 * cols), BF16),
                   jax.ShapeDtypeStruct((1, cols), F32)],
        compiler_params=_params(("arbitrary",), 32),
        name="filter_gen",
    )(fw["bands"], fw["phase"], fw["w1t"], fw["w1"], fw["b1"], fw["w2"], fw["b2"], fw["freq"],
      fw["w3_hi"], fw["w3_lo"], fw["decay"])


def _fft_dims(seq):
    n2 = 128 if 2 * seq >= 32768 else 64
    n1 = 2 * seq // n2
    return n1, n2


def _dft_tables(n1, n2):
    n = n1 * n2
    n1h = n1 // 2
    k1 = jnp.arange(n1h, dtype=I32)[:, None]
    m1 = jnp.arange(n1, dtype=I32)[None, :]
    ang = (2.0 * math.pi / n1) * ((k1 * m1) % n1).astype(F32)
    top = jnp.cos(ang)
    bot = -jnp.sin(ang)
    nyq = jnp.where(m1 % 2 == 0, 1.0, -1.0).astype(F32)
    bot = jnp.concatenate([nyq, bot[1:]], axis=0)
    fa = jnp.concatenate([top, bot], axis=0)
    weight = jnp.where((jnp.arange(n1) % n1h) == 0, 1.0, 2.0).astype(F32) / n
    fi = (fa[:, :n1h] * weight[:, None]).T

    kk = jnp.arange(n1h + 1, dtype=I32)[:, None, None]
    k2 = jnp.arange(n2, dtype=I32)[None, :, None]
    m2 = jnp.arange(n2, dtype=I32)[None, None, :]
    phi = (2.0 * math.pi / n) * ((m2 * (kk + n1 * k2)) % n).astype(F32)
    gr = jnp.cos(phi)
    gi = -jnp.sin(phi)
    blk = jnp.concatenate([jnp.concatenate([gr, -gi], axis=2),
                           jnp.concatenate([gi, gr], axis=2)], axis=1)
    left = (jnp.arange(2 * n2) < n2)[None, :]
    g0 = jnp.where(left, blk[0], 0.0)
    gf = jnp.concatenate([g0[None], blk[1:n1h]], axis=0)
    gnyq = jnp.concatenate([jnp.zeros((2 * n2, n2), F32), blk[n1h][:, :n2]], axis=1)
    return dict(fa_full=fa.astype(BF16), fa_half=fa[:, :n1h].astype(BF16), fi=fi.astype(BF16),
                gf=gf.astype(BF16), gnyq=gnyq.astype(BF16),
                ginv=jnp.transpose(gf, (0, 2, 1)).astype(BF16), m2=gnyq.T.astype(BF16))


def _fft_a_body(x_ref, f_ref, o_ref):
    a = jnp.dot(f_ref[...], x_ref[0].astype(BF16), preferred_element_type=F32)
    n1h = o_ref.shape[2]
    o_ref[0, 0] = a[:n1h].astype(BF16)
    o_ref[0, 1] = a[n1h:].astype(BF16)


def _fft_a(x3, fmat):
    bsz, r, nc = x3.shape
    n1 = fmat.shape[0]
    tn = min(8192, nc)
    return pl.pallas_call(
        _fft_a_body,
        grid=(bsz, nc // tn),
        in_specs=[pl.BlockSpec((1, r, tn), lambda b, j: (b, 0, j)),
                  pl.BlockSpec((n1, r), lambda b, j: (0, 0))],
        out_specs=pl.BlockSpec((1, 2, n1 // 2, tn), lambda b, j: (b, 0, 0, j)),
        out_shape=jax.ShapeDtypeStruct((bsz, 2, n1 // 2, nc), BF16),
        compiler_params=_params(("parallel", "parallel"), 48),
        name="fft_stage_a",
    )(x3, fmat)


def _fft_b_body(ar_ref, ai_ref, g_ref, gn_ref, sc_ref, x_ref, xn_ref, *, kb, n2):
    j = pl.program_id(1)
    c = sc_ref.shape[1]
    inv = 1.0 / sc_ref[...]
    a_re = ar_ref[0, 0].reshape(kb, n2, c)
    a_im = ai_ref[0, 0].reshape(kb, n2, c)
    for kk in range(kb):
        rhs = jnp.concatenate([a_re[kk], a_im[kk]], axis=0)
        x_ref[0, kk] = (jnp.dot(g_ref[kk], rhs, preferred_element_type=F32) * inv).astype(BF16)

    @pl.when(j == 0)
    def _():
        rhs = jnp.concatenate([a_re[0], a_im[0]], axis=0)
        xn_ref[0] = (jnp.dot(gn_ref[...], rhs, preferred_element_type=F32) * inv).astype(BF16)


def _fft_b(a4, tabs, scale):
    bsz, _, n1h, nc = a4.shape
    c = scale.shape[1]
    n2 = nc // c
    kb = 16
    return pl.pallas_call(
        functools.partial(_fft_b_body, kb=kb, n2=n2),
        grid=(bsz, n1h // kb),
        in_specs=[pl.BlockSpec((1, 1, kb, nc), lambda b, j: (b, 0, j, 0)),
                  pl.BlockSpec((1, 1, kb, nc), lambda b, j: (b, 1, j, 0)),
                  pl.BlockSpec((kb, 2 * n2, 2 * n2), lambda b, j: (j, 0, 0)),
                  pl.BlockSpec((2 * n2, 2 * n2), lambda b, j: (0, 0)),
                  pl.BlockSpec((1, c), lambda b, j: (0, 0))],
        out_specs=[pl.BlockSpec((1, kb, 2 * n2, c), lambda b, j: (b, j, 0, 0)),
                   pl.BlockSpec((1, 2 * n2, c), lambda b, j: (b, 0, 0))],
        out_shape=[jax.ShapeDtypeStruct((bsz, n1h, 2 * n2, c), BF16),
                   jax.ShapeDtypeStruct((bsz, 2 * n2, c), BF16)],
        compiler_params=_params(("parallel", "arbitrary"), 48),
        name="fft_stage_b",
    )(a4, a4, tabs["gf"], tabs["gnyq"], scale)


def _cmul(x, k, n2):
    xr, xi = x[:n2], x[n2:]
    kr, ki = k[:n2], k[n2:]
    return jnp.concatenate([xr * kr - xi * ki, xr * ki + xi * kr], axis=0).astype(BF16)


def _spec_body(ar_ref, ai_ref, k_ref, kn_ref, g_ref, gn_ref, gi_ref, m2_ref, o_ref, *, kb, n2):
    j = pl.program_id(1)
    c = ar_ref.shape[3] // n2
    a_re = ar_ref[0, 0].reshape(kb, n2, c)
    a_im = ai_ref[0, 0].reshape(kb, n2, c)

    def through(fwd, rhs, kf, inv):
        x = jnp.dot(fwd, rhs, preferred_element_type=F32)
        return jnp.dot(inv, _cmul(x, kf.astype(F32), n2), preferred_element_type=F32)

    re_rows, im_rows = [], []
    for kk in range(kb):
        rhs = jnp.concatenate([a_re[kk], a_im[kk]], axis=0)
        acc = through(g_ref[kk], rhs, k_ref[0, kk], gi_ref[kk])
        if kk == 0:
            nyq = through(gn_ref[...], rhs, kn_ref[0], m2_ref[...])
            acc = jnp.where(j == 0, jnp.concatenate([acc[:n2], nyq[n2:]], axis=0), acc)
        re_rows.append(acc[:n2].astype(BF16))
        im_rows.append(acc[n2:].astype(BF16))
    o_ref[0, 0] = jnp.stack(re_rows, axis=0).reshape(kb, n2 * c)
    o_ref[0, 1] = jnp.stack(im_rows, axis=0).reshape(kb, n2 * c)


def _spec_conv(a4, kf, kfnyq, order, tabs):
    bsz, _, n1h, nc = a4.shape
    tn2 = kf.shape[2]
    n2 = tn2 // 2
    c = nc // n2
    kb = 16
    return pl.pallas_call(
        functools.partial(_spec_body, kb=kb, n2=n2),
        grid=(bsz, n1h // kb),
        in_specs=[pl.BlockSpec((1, 1, kb, nc), lambda b, j: (b, 0, j, 0)),
                  pl.BlockSpec((1, 1, kb, nc), lambda b, j: (b, 1, j, 0)),
                  pl.BlockSpec((1, kb, tn2, c), lambda b, j: (0, j, 0, order)),
                  pl.BlockSpec((1, tn2, c), lambda b, j: (0, 0, order)),
                  pl.BlockSpec((kb, tn2, tn2), lambda b, j: (j, 0, 0)),
                  pl.BlockSpec((tn2, tn2), lambda b, j: (0, 0)),
                  pl.BlockSpec((kb, tn2, tn2), lambda b, j: (j, 0, 0)),
                  pl.BlockSpec((tn2, tn2), lambda b, j: (0, 0))],
        out_specs=pl.BlockSpec((1, 2, kb, nc), lambda b, j: (b, 0, j, 0)),
        out_shape=jax.ShapeDtypeStruct((bsz, 2, n1h, nc), BF16),
        compiler_params=_params(("parallel", "arbitrary"), 48),
        name="spectral_conv",
    )(a4, a4, kf, kfnyq, tabs["gf"], tabs["gnyq"], tabs["ginv"], tabs["m2"])


def _ifft_a_body(b_ref, f_ref, z_ref, gate_ref, bias_ref, o_ref):
    y = jnp.dot(f_ref[...], b_ref[0], preferred_element_type=F32)
    res = gate_ref[0] * (y + z_ref[0] * bias_ref[...])
    o_ref[0] = res.reshape(o_ref.shape[1:])


def _ifft_a(b3, fi, z3, gate3, bias_t, time_major):
    bsz, n1, nc = b3.shape
    n1h = n1 // 2
    tn = bias_t.shape[1]
    c = D_HYENA
    if time_major:
        out_spec = pl.BlockSpec((1, n1h, tn // c, c), lambda b, j: (b, 0, j, 0))
        out_shape = jax.ShapeDtypeStruct((bsz, n1h, nc // c, c), F32)
    else:
        out_spec = pl.BlockSpec((1, n1h, tn), lambda b, j: (b, 0, j))
        out_shape = jax.ShapeDtypeStruct((bsz, n1h, nc), F32)
    return pl.pallas_call(
        _ifft_a_body,
        grid=(bsz, nc // tn),
        in_specs=[pl.BlockSpec((1, n1, tn), lambda b, j: (b, 0, j)),
                  pl.BlockSpec((n1h, n1), lambda b, j: (0, 0)),
                  pl.BlockSpec((1, n1h, tn), lambda b, j: (b, 0, j)),
                  pl.BlockSpec((1, n1h, tn), lambda b, j: (b, 0, j)),
                  pl.BlockSpec((1, tn), lambda b, j: (0, 0))],
        out_specs=out_spec,
        out_shape=out_shape,
        compiler_params=_params(("parallel", "parallel"), 48),
        name="ifft_stage_a",
    )(b3, fi, z3, gate3, bias_t)


def _hyena(uh, conv_w, conv_b, fw, bias):
    bsz, seq, _ = uh.shape
    c = D_HYENA
    n1, n2 = _fft_dims(seq)
    n1h = n1 // 2
    tabs = _dft_tables(n1, n2)
    v, x1, x2 = _sconv(uh, conv_w, conv_b, n2)

    kcirc, ksum = _filters(seq, n1, n2, fw)
    cols = HYENA_ORDER * c
    ka = _fft_a(kcirc[None], tabs["fa_full"])
    kf, kfnyq = _fft_b(ka, tabs, ksum)

    tn = min(8192, n2 * c)
    z = v
    for order, gate in enumerate((x1, x2)):
        za = _fft_a(z, tabs["fa_half"])
        zb = _spec_conv(za, kf, kfnyq, order, tabs)
        bias_t = jnp.tile(bias[order][None, :], (1, tn // c))
        z = _ifft_a(zb.reshape(bsz, n1, n2 * c), tabs["fi"], z, gate, bias_t,
                    time_major=order == HYENA_ORDER - 1)
    return z.reshape(bsz, seq, c)


def _qkv_body(cq_ref, ckvr_ref, cos_ref, sin_ref, wq_ref, wk_ref, wv_ref, qa_ref, kva_ref, qg_ref, kg_ref,
              qt_ref, k_ref, vt_ref):
    cqn = _rms(cq_ref[0]) * qa_ref[...]
    qs = jnp.dot(cqn.astype(BF16), wq_ref[...], preferred_element_type=F32)
    ck = ckvr_ref[0]
    ckvn = (_rms(ck[:, :KV_LORA]) * kva_ref[...]).astype(BF16)
    ks = jnp.dot(ckvn, wk_ref[...], preferred_element_type=F32)
    vs = jnp.dot(ckvn, wv_ref[...], preferred_element_type=F32)
    krope = pltpu.roll(ck[:, KV_LORA:], QK_NOPE, axis=1)
    cos = cos_ref[...]
    sin = sin_ref[...]
    lane = lax.broadcasted_iota(I32, cos.shape, 1)
    half = QK_ROPE // 2
    first = (lane >= QK_NOPE) & (lane < QK_NOPE + half)
    second = (lane >= QK_NOPE + half) & (lane < QK_HEAD)

    def head(xh, gain, scale):
        ms = jnp.sum(xh * xh, axis=-1, keepdims=True) * (1.0 / QK_HEAD)
        xn = xh * lax.rsqrt(ms + EPS) * gain
        rot = jnp.where(first, -pltpu.roll(xn, SLOT - half, axis=1),
                        jnp.where(second, pltpu.roll(xn, half, axis=1), 0.0))
        return (xn * cos + rot * sin) * scale

    q_scale = QK_HEAD ** -0.5 * math.log2(math.e)
    eye = (lax.broadcasted_iota(I32, (SLOT, SLOT), 0) == lax.broadcasted_iota(I32, (SLOT, SLOT), 1)).astype(BF16)
    transpose = lambda x: lax.dot_general(eye, x.astype(BF16), (((1,), (1,)), ((), ())),
                                          preferred_element_type=F32).astype(BF16)
    for h in range(N_HEADS):
        sl = slice(h * SLOT, (h + 1) * SLOT)
        qt_ref[0, h] = transpose(head(qs[:, sl], qg_ref[...], q_scale))
        k_ref[0, h] = head(ks[:, sl] + krope, kg_ref[...], 1.0).astype(BF16)
    aug = lax.broadcasted_iota(I32, (V_AUG - V_HEAD, cos.shape[0]), 0)
    ones_row = jnp.where(aug == 0, 1.0, 0.0).astype(BF16)
    for hp in range(N_HEADS // 2):
        pair_t = transpose(vs[:, hp * 2 * V_HEAD:(hp + 1) * 2 * V_HEAD])
        vt_ref[0, hp] = jnp.concatenate([pair_t[:V_HEAD], ones_row, pair_t[V_HEAD:], ones_row], axis=0)


def _qkv(cq, ckvr, cos_t, sin_t, aw):
    bsz, seq, _ = cq.shape
    tm = min(512, seq)
    const = lambda shape: pl.BlockSpec(shape, lambda b, i: (0,) * len(shape))
    return pl.pallas_call(
        _qkv_body,
        grid=(bsz, seq // tm),
        in_specs=[pl.BlockSpec((1, tm, Q_LORA), lambda b, i: (b, i, 0)),
                  pl.BlockSpec((1, tm, D_IN_PAD - COL_Q), lambda b, i: (b, i, 0)),
                  pl.BlockSpec((tm, SLOT), lambda b, i: (i, 0)),
                  pl.BlockSpec((tm, SLOT), lambda b, i: (i, 0)),
                  const((Q_LORA, N_HEADS * SLOT)), const((KV_LORA, N_HEADS * SLOT)),
                  const((KV_LORA, D_ATTN)),
                  const((1, Q_LORA)), const((1, KV_LORA)), const((1, SLOT)), const((1, SLOT))],
        out_specs=[pl.BlockSpec((1, N_HEADS, SLOT, tm), lambda b, i: (b, 0, 0, i)),
                   pl.BlockSpec((1, N_HEADS, tm, SLOT), lambda b, i: (b, 0, i, 0)),
                   pl.BlockSpec((1, N_HEADS // 2, 2 * V_AUG, tm), lambda b, i: (b, 0, 0, i))],
        out_shape=[jax.ShapeDtypeStruct((bsz, N_HEADS, SLOT, seq), BF16),
                   jax.ShapeDtypeStruct((bsz, N_HEADS, seq, SLOT), BF16),
                   jax.ShapeDtypeStruct((bsz, N_HEADS // 2, 2 * V_AUG, seq), BF16)],
        compiler_params=_params(("parallel", "parallel"), 40),
        name="qkv_prep",
    )(cq, ckvr, cos_t, sin_t, aw["wq"], aw["wk"], aw["wv"], aw["qa"], aw["kva"], aw["qg"], aw["kg"])


def _flash_body(qt_ref, k_ref, vt_ref, o_ref, s0_ref, s1_ref, acc_ref, *, tk, nk, nqc, unroll):
    chains = [(hh, qc) for hh in range(2) for qc in range(nqc)]
    acc_ref[...] = jnp.zeros(acc_ref.shape, F32)

    def qk(t, s_ref, only=None):
        ks = pl.multiple_of(t * tk, tk)
        tile_max = []
        for c, (hh, qc) in enumerate(chains):
            if only is not None and c != only:
                continue
            k = k_ref[0, hh, pl.ds(ks, tk), :]
            s = jnp.dot(k, qt_ref[0, hh, :, qc * ATT_QC:(qc + 1) * ATT_QC], preferred_element_type=F32)
            s_ref[c] = s
            tile_max.append(jnp.max(s, axis=0, keepdims=True))
        return tile_max

    def softmax_pv(t, s_ref, tile_max, m, only=None):
        ks = pl.multiple_of(t * tk, tk)
        m_out = []
        for c, (hh, qc) in enumerate(chains):
            if only is not None and c != only:
                continue
            i = 0 if only is not None else c
            m_new = jnp.maximum(m[i], tile_max[i])
            a = jnp.exp2(m[i] - m_new)
            p = jnp.exp2(s_ref[c] - m_new)
            m_out.append(m_new)
            vt = vt_ref[0, 0, hh * V_AUG:(hh + 1) * V_AUG, pl.ds(ks, tk)]
            acc_ref[c] = acc_ref[c] * a + jnp.dot(vt, p.astype(BF16), preferred_element_type=F32)
        return m_out

    def group(u, carry):
        tile_max, m = carry
        for i in range(0, unroll, 2):
            t = unroll * u + i
            tm1, tm0 = [], []
            m = list(m)
            for c in range(len(chains)):
                tm1 += qk(t + 1, s1_ref, only=c)
                m[c] = softmax_pv(t, s0_ref, [tile_max[c]], [m[c]], only=c)[0]
            for c in range(len(chains)):
                tm0 += qk(jnp.minimum(t + 2, nk - 1), s0_ref, only=c)
                m[c] = softmax_pv(t + 1, s1_ref, [tm1[c]], [m[c]], only=c)[0]
            tile_max = tm0
        return tile_max, m

    init = (qk(0, s0_ref), [jnp.full((1, ATT_QC), -jnp.inf, F32)] * len(chains))
    lax.fori_loop(0, nk // unroll, group, init)

    def normalised(c):
        acc = acc_ref[c]
        return acc[:V_HEAD] * (1.0 / acc[V_HEAD:V_HEAD + 1])

    heads = [jnp.concatenate([normalised(hh * nqc + qc) for qc in range(nqc)], axis=1) for hh in range(2)]
    o_ref[0] = jnp.concatenate(heads, axis=0).T


def _flash(qt, k, vt):
    bsz, nh, _, seq = qt.shape
    tq = min(ATT_TQ, seq)
    tk = min(ATT_TK, seq)
    nqc = tq // ATT_QC
    nk = seq // tk
    unroll = min(ATT_UNROLL, nk)
    assert nk % unroll == 0 and unroll % 2 == 0
    return pl.pallas_call(
        functools.partial(_flash_body, tk=tk, nk=nk, nqc=nqc, unroll=unroll),
        grid=(bsz, nh // 2, seq // tq),
        in_specs=[pl.BlockSpec((1, 2, SLOT, tq), lambda b, hp, i: (b, hp, 0, i)),
                  pl.BlockSpec((1, 2, seq, SLOT), lambda b, hp, i: (b, hp, 0, 0)),
                  pl.BlockSpec((1, 1, 2 * V_AUG, seq), lambda b, hp, i: (b, hp, 0, 0))],
        out_specs=pl.BlockSpec((1, tq, 2 * V_HEAD), lambda b, hp, i: (b, i, hp)),
        out_shape=jax.ShapeDtypeStruct((bsz, seq, nh * V_HEAD), F32),
        scratch_shapes=[pltpu.VMEM((2 * nqc, tk, ATT_QC), F32), pltpu.VMEM((2 * nqc, tk, ATT_QC), F32),
                        pltpu.VMEM((2 * nqc, V_AUG, ATT_QC), F32)],
        compiler_params=_params(("parallel", "parallel", "parallel"), 48),
        name="flash_attn",
    )(qt, k, vt)


def _outproj_body(yh_ref, ya_ref, x_ref, og_ref, w_ref, mg_ref, wr_ref, xm_ref, xn_ref, aff_ref, affn_ref):
    og = og_ref[...]
    half = D_MODEL // 2
    y = jnp.concatenate([_rms(yh_ref[...]) * og[:, :half], _rms(ya_ref[...]) * og[:, half:]], axis=1)
    xm = x_ref[...] + jnp.dot(y.astype(BF16), w_ref[...], preferred_element_type=F32)
    xm_ref[...] = xm
    xn = _rms(xm) * mg_ref[...]
    xn_hi = xn.astype(BF16)
    xn_ref[...] = xn_hi
    xn_lo = (xn - xn_hi.astype(F32)).astype(BF16)
    logits = (jnp.dot(xn_hi, wr_ref[0], preferred_element_type=F32)
              + jnp.dot(xn_lo, wr_ref[0], preferred_element_type=F32)
              + jnp.dot(xn_hi, wr_ref[1], preferred_element_type=F32))
    lane = lax.broadcasted_iota(I32, logits.shape, 1)
    logits = jnp.where(lane < N_EXPERTS, logits, -jnp.inf)
    e = jnp.exp(logits - jnp.max(logits, axis=-1, keepdims=True))
    aff = e / jnp.sum(e, axis=-1, keepdims=True)
    aff_ref[...] = aff.T[:N_EXPERTS]
    affn_ref[...] = aff[:, :N_EXPERTS]


def _outproj(yh, ya, x2d, og, w_out, mg, wr_pad):
    n = x2d.shape[0]
    tm = min(512, n)
    half = D_MODEL // 2
    const = lambda shape: pl.BlockSpec(shape, lambda i: (0,) * len(shape))
    return pl.pallas_call(
        _outproj_body,
        grid=(n // tm,),
        in_specs=[pl.BlockSpec((tm, half), lambda i: (i, 0)),
                  pl.BlockSpec((tm, half), lambda i: (i, 0)),
                  pl.BlockSpec((tm, D_MODEL), lambda i: (i, 0)),
                  const((1, D_MODEL)), const((D_MODEL, D_MODEL)), const((1, D_MODEL)),
                  const((2, D_MODEL, LANES))],
        out_specs=[pl.BlockSpec((tm, D_MODEL), lambda i: (i, 0)),
                   pl.BlockSpec((tm, D_MODEL), lambda i: (i, 0)),
                   pl.BlockSpec((N_EXPERTS, tm), lambda i: (0, i)),
                   pl.BlockSpec((tm, N_EXPERTS), lambda i: (i, 0))],
        out_shape=[jax.ShapeDtypeStruct((n, D_MODEL), F32),
                   jax.ShapeDtypeStruct((n, D_MODEL), BF16),
                   jax.ShapeDtypeStruct((N_EXPERTS, n), F32),
                   jax.ShapeDtypeStruct((n, N_EXPERTS), F32)],
        compiler_params=_params(("parallel",), 40),
        name="outproj_router",
    )(yh, ya, x2d, og, w_out, mg, wr_pad)


def _select_body(aff_ref, upper_ref, lower_ref, pos_ref, off_ref, *, cap):
    group = aff_ref.shape[0]
    bits = [pltpu.bitcast(aff_ref[x], I32) for x in range(group)]
    upper = upper_ref[...]
    lower = lower_ref[...]

    def count(mask):
        return jnp.sum(jnp.sum(mask.astype(F32), axis=1, keepdims=True), axis=0, keepdims=True)

    def bit_step(i, thr):
        bit = jnp.left_shift(jnp.int32(1), 30 - i)
        return tuple(jnp.where(count(bits[x] >= (thr[x] | bit)) >= cap, thr[x] | bit, thr[x])
                     for x in range(group))

    thr = lax.fori_loop(0, 31, bit_step, tuple(jnp.zeros((1, 1), I32) for _ in range(group)))

    def prefix(mask):
        within = jnp.dot(mask.astype(BF16), upper, preferred_element_type=F32)
        total = within[:, LANES - 1:LANES]
        offs = jnp.dot(lower, jnp.broadcast_to(total, within.shape).astype(BF16), preferred_element_type=F32)
        return within, offs

    for x in range(group):
        gt = bits[x] > thr[x]
        eq = bits[x] == thr[x]
        need = cap - count(gt)
        w_eq, o_eq = prefix(eq)
        sel = gt | (eq & (o_eq + w_eq <= need))
        w_sel, o_sel = prefix(sel)
        pos_ref[x] = jnp.where(sel, o_sel + w_sel - 1.0, -1.0).astype(I32)
        off_ref[x] = o_sel.astype(I32)


def _select(aff_t, cap):
    ne, n = aff_t.shape
    rows = n // LANES
    upper = (jnp.arange(LANES)[:, None] <= jnp.arange(LANES)[None, :]).astype(BF16)
    lower = (jnp.arange(rows)[None, :] < jnp.arange(rows)[:, None]).astype(BF16)
    group = 4
    blk = pl.BlockSpec((group, rows, LANES), lambda e: (e, 0, 0))
    pos, off = pl.pallas_call(
        functools.partial(_select_body, cap=cap),
        grid=(ne // group,),
        in_specs=[blk, pl.BlockSpec((LANES, LANES), lambda e: (0, 0)),
                  pl.BlockSpec((rows, rows), lambda e: (0, 0))],
        out_specs=[blk, blk],
        out_shape=[jax.ShapeDtypeStruct((ne, rows, LANES), I32),
                   jax.ShapeDtypeStruct((ne, rows, LANES), I32)],
        compiler_params=_params(("parallel",), 32),
        name="ec_select",
    )(aff_t.reshape(ne, rows, LANES), upper, lower)
    return pos.reshape(ne, n), off[:, :, 0]


def _ffn_body(off_ref, nch_ref, pos_ref, x_ref, wg_ref, wu_ref, wd_ref, y_ref, xs_ref, *, nb, sub, cap):
    e = pl.program_id(0)
    sb = pl.program_id(1)

    @pl.when(sb == 0)
    def _():
        xs_ref[...] = jnp.zeros(xs_ref.shape, BF16)

    row = lax.broadcasted_iota(I32, (GATHER_ROWS, TOK_BLOCK), 0)

    def gather(i, c):
        blk = e * nb + sb * sub + i
        base = off_ref[blk] * ALIGN
        tok = slice(i * TOK_BLOCK, (i + 1) * TOK_BLOCK)
        start = pl.multiple_of(base + c * GATHER_ROWS, ALIGN)
        hit = (pos_ref[0, :, tok] - start) == row
        win = jnp.dot(hit.astype(BF16), x_ref[tok, :], preferred_element_type=F32)
        xs_ref[pl.ds(start, GATHER_ROWS), :] += win.astype(BF16)

    for i in range(sub):
        gather(i, 0)
    for i in range(sub):
        lax.fori_loop(1, nch_ref[e * nb + sb * sub + i], lambda c, carry, i=i: (gather(i, c), carry)[1], 0)

    @pl.when(sb == pl.num_programs(1) - 1)
    def _():
        for j in range(cap // SLOT_TILE):
            rows = slice(j * SLOT_TILE, (j + 1) * SLOT_TILE)
            xt = xs_ref[rows, :]
            g = jnp.dot(xt, wg_ref[0], preferred_element_type=F32)
            u = jnp.dot(xt, wu_ref[0], preferred_element_type=F32)
            h = (g * jax.nn.sigmoid(g) * u).astype(BF16)
            y_ref[0, rows, :] = jnp.dot(h, wd_ref[0], preferred_element_type=F32).astype(BF16)
        y_ref[0, cap:, :] = jnp.zeros((y_ref.shape[1] - cap, D_MODEL), BF16)


def _ffn(base, nchunk, pos3, xn, wg, wu, wd, cap):
    ne = wg.shape[0]
    n = xn.shape[0]
    nb = n // TOK_BLOCK
    sub = min(8, nb)
    cap_pad = cap + TOK_BLOCK + ALIGN
    grid_spec = pltpu.PrefetchScalarGridSpec(
        num_scalar_prefetch=2,
        grid=(ne, nb // sub),
        in_specs=[pl.BlockSpec((1, 1, sub * TOK_BLOCK), lambda e, s, off, nch: (e, 0, s)),
                  pl.BlockSpec((sub * TOK_BLOCK, D_MODEL), lambda e, s, off, nch: (s, 0)),
                  pl.BlockSpec((1, D_MODEL, D_EXPERT), lambda e, s, off, nch: (e, 0, 0)),
                  pl.BlockSpec((1, D_MODEL, D_EXPERT), lambda e, s, off, nch: (e, 0, 0)),
                  pl.BlockSpec((1, D_EXPERT, D_MODEL), lambda e, s, off, nch: (e, 0, 0))],
        out_specs=pl.BlockSpec((1, cap_pad, D_MODEL), lambda e, s, off, nch: (e, 0, 0)),
        scratch_shapes=[pltpu.VMEM((cap_pad, D_MODEL), BF16)])
    return pl.pallas_call(
        functools.partial(_ffn_body, nb=nb, sub=sub, cap=cap),
        grid_spec=grid_spec,
        out_shape=jax.ShapeDtypeStruct((ne, cap_pad, D_MODEL), BF16),
        compiler_params=_params(("parallel", "arbitrary"), 48),
        name="ec_ffn",
    )(base, nchunk, pos3, xn, wg, wu, wd)


def _combine_body(off_ref, spill_ref, post_ref, affn_ref, xm_ref, *refs, nb, ne, win_rows):
    win_refs, tail_refs, o_ref = refs[:ne], refs[ne:-1], refs[-1]
    b = pl.program_id(0)
    post = post_ref[...]
    affn = affn_ref[...]

    def expand(y_refs, shift):
        width = y_refs[0].shape[0]
        lane = lax.broadcasted_iota(I32, (TOK_BLOCK, width), 1)
        total = jnp.zeros((TOK_BLOCK, D_MODEL), F32)
        for e in range(ne):
            rel = post[:, e:e + 1] - (off_ref[e * nb + b] * ALIGN + shift)
            hit = (rel == lane).astype(BF16)
            total = total + affn[:, e:e + 1] * jnp.dot(hit, y_refs[e][...], preferred_element_type=F32)
        return total

    def expand_pairs(y_refs):
        width = y_refs[0].shape[0]
        lane = lax.broadcasted_iota(I32, (TOK_BLOCK, 2 * width), 1)
        total = jnp.zeros((TOK_BLOCK, D_MODEL), F32)
        for e in range(0, ne, 2):
            rel0 = post[:, e:e + 1] - off_ref[e * nb + b] * ALIGN
            rel1 = post[:, e + 1:e + 2] - off_ref[(e + 1) * nb + b] * ALIGN + width
            gates = (jnp.where(rel0 == lane, affn[:, e:e + 1], 0.0)
                     + jnp.where((rel1 == lane) & (rel1 >= width), affn[:, e + 1:e + 2], 0.0))
            rows = jnp.concatenate([y_refs[e][...], y_refs[e + 1][...]], axis=0)
            total = total + jnp.dot(gates.astype(BF16), rows, preferred_element_type=F32)
        return total

    if not tail_refs and 2 * win_rows <= TOK_BLOCK and ne % 2 == 0:
        o_ref[...] = xm_ref[...] + expand_pairs(win_refs)
    else:
        o_ref[...] = xm_ref[...] + expand(win_refs, 0)

    if tail_refs:
        @pl.when(spill_ref[b] != 0)
        def _():
            o_ref[...] += expand(tail_refs, win_rows)


def _combine(base, spill, pos_t, aff_n, xm, y, win_rows):
    ne = y.shape[0]
    n = xm.shape[0]
    nb = n // TOK_BLOCK
    tail_rows = ALIGN if win_rows == TOK_BLOCK else 0

    def window(e, rows, shift):
        return pl.BlockSpec((pl.Squeezed(), pl.Element(rows), pl.Element(D_MODEL)),
                            lambda b, off, sp: (e, (off[e * nb + b] + shift // ALIGN) * ALIGN, 0))

    tails = [window(e, tail_rows, win_rows) for e in range(ne)] if tail_rows else []
    grid_spec = pltpu.PrefetchScalarGridSpec(
        num_scalar_prefetch=2,
        grid=(nb,),
        in_specs=([pl.BlockSpec((TOK_BLOCK, ne), lambda b, off, sp: (b, 0)),
                   pl.BlockSpec((TOK_BLOCK, ne), lambda b, off, sp: (b, 0)),
                   pl.BlockSpec((TOK_BLOCK, D_MODEL), lambda b, off, sp: (b, 0))]
                  + [window(e, win_rows, 0) for e in range(ne)] + tails),
        out_specs=pl.BlockSpec((TOK_BLOCK, D_MODEL), lambda b, off, sp: (b, 0)))
    return pl.pallas_call(
        functools.partial(_combine_body, nb=nb, ne=ne, win_rows=win_rows),
        grid_spec=grid_spec,
        out_shape=jax.ShapeDtypeStruct((n, D_MODEL), F32),
        compiler_params=_params(("parallel",), 48),
        name="ec_combine",
    )(base, spill, pos_t, aff_n, xm, *([y] * (ne + len(tails))))


def _ec_moe(xm, xn, aff_t, aff_n, wg, wu, wd):
    ne, n = aff_t.shape
    cap = max(1, EC_CAPACITY * n // N_EXPERTS)
    nb = n // TOK_BLOCK
    pos, off = _select(aff_t, cap)
    first = off[:, ::TOK_BLOCK // LANES]
    count = jnp.concatenate([first[:, 1:], jnp.full((ne, 1), cap, I32)], axis=1) - first
    base = first // ALIGN
    span = jnp.where(count > 0, first - base * ALIGN + count, 0)
    spill = jnp.any(span > TOK_BLOCK, axis=0).astype(I32)
    nchunk = ((span + GATHER_ROWS - 1) // GATHER_ROWS).reshape(ne * nb).astype(I32)
    base = base.reshape(ne * nb).astype(I32)
    y = _ffn(base, nchunk, pos.reshape(ne, 1, n), xn, wg, wu, wd, cap)
    pos_t = pos.T
    return lax.cond(jnp.max(span) <= COMBINE_FAST_ROWS,
                    lambda: _combine(base, spill, pos_t, aff_n, xm, y, COMBINE_FAST_ROWS),
                    lambda: _combine(base, spill, pos_t, aff_n, xm, y, TOK_BLOCK))


def _slot_cols(w, head_width):
    k = w.shape[0]
    w3 = w.reshape(k, N_HEADS, head_width)
    return jnp.pad(w3, ((0, 0), (0, 0), (0, SLOT - head_width))).reshape(k, N_HEADS * SLOT)


def _attn_weights(q_a_norm, w_uq, kv_a_norm, w_ukv, q_norm, k_norm):
    wkv = w_ukv.reshape(KV_LORA, N_HEADS, QK_NOPE + V_HEAD)
    wk = _slot_cols(wkv[:, :, :QK_NOPE].reshape(KV_LORA, -1), QK_NOPE)
    wv = wkv[:, :, QK_NOPE:].reshape(KV_LORA, D_ATTN)
    pad_gain = lambda g: jnp.pad(g, (0, SLOT - QK_HEAD))[None, :]
    return dict(wq=_slot_cols(w_uq, QK_HEAD).astype(BF16), wk=wk.astype(BF16), wv=wv.astype(BF16),
                qa=q_a_norm[None, :], kva=kv_a_norm[None, :], qg=pad_gain(q_norm), kg=pad_gain(k_norm))


def _rope_tables(seq):
    pos = jnp.arange(seq, dtype=F32)
    inv_freq = ROPE_THETA ** (-jnp.arange(0, QK_ROPE, 2, dtype=F32) / QK_ROPE)
    ang = pos[:, None] * inv_freq
    ang = jnp.concatenate([ang, ang], axis=-1)
    pad = lambda t, fill: jnp.concatenate(
        [jnp.full((seq, QK_NOPE), fill, F32), t, jnp.full((seq, SLOT - QK_HEAD), fill, F32)], axis=1)
    return pad(jnp.cos(ang), 1.0), pad(jnp.sin(ang), 0.0)


def _hi_lo(w):
    hi = w.astype(BF16)
    return jnp.stack([hi, (w - hi.astype(F32)).astype(BF16)], axis=1)


def _filter_weights(w1, b1, w2, b2, w3, freq, decay):
    hid, nbands = FILTER_HIDDEN, FILTER_BANDS
    assert 2 * hid == LANES and 2 * nbands <= hid
    both = lambda row: jnp.concatenate([row, row])[None, :]
    blockdiag = lambda a: jnp.concatenate(
        [jnp.concatenate([a, jnp.zeros_like(a)], axis=1), jnp.concatenate([jnp.zeros_like(a), a], axis=1)], axis=0)
    bands = jnp.linspace(1e-4, nbands - 1, nbands, dtype=F32)
    rest = jnp.zeros((hid - 2 * nbands,), F32)
    band_half = jnp.concatenate([bands, bands, rest])
    phase_half = jnp.concatenate([jnp.zeros((nbands,), F32), jnp.full((nbands,), 0.5 * math.pi, F32), rest])
    w1_half = jnp.concatenate([-w1[1 + nbands:], w1[1:1 + nbands], jnp.zeros((hid - 2 * nbands, hid), F32)], axis=0)
    cols = HYENA_ORDER * D_HYENA
    w3d = jnp.transpose(w3.reshape(hid, N_DIR, cols), (1, 0, 2))
    w3p = jnp.stack([jnp.pad(w3d[0], ((0, hid), (0, 0))), jnp.pad(w3d[1], ((hid, 0), (0, 0)))], axis=0)
    w3_hi = w3p.astype(BF16)
    w3_lo = (w3p - w3_hi.astype(F32)).astype(BF16)
    return dict(bands=both(band_half), phase=both(phase_half), w1t=both(w1[0]), w1=blockdiag(w1_half),
                b1=both(b1), w2=blockdiag(w2), b2=both(b2), freq=both(freq), w3_hi=w3_hi, w3_lo=w3_lo,
                decay=decay.reshape(N_DIR, 1, cols))


def _trunk(x, p):
    bsz, seq, _ = x.shape
    n = bsz * seq
    cos_t, sin_t = _rope_tables(seq)
    x2 = x.reshape(n, D_MODEL)
    for l in range(DEPTH):
        uh, cq, ckvr = _inproj(x2, p["attn_norm"][l][None, :], p["w_in"][l])
        y_h = _hyena(uh.reshape(bsz, seq, COL_HYENA), p["conv_w"][l], p["conv_b"][l][None, :],
                     p["filt"][l], p["hyena_bias"][l])
        qt, k, vt = _qkv(cq.reshape(bsz, seq, -1), ckvr.reshape(bsz, seq, -1), cos_t, sin_t, p["attn"][l])
        y_a = _flash(qt, k, vt)
        xm, xn, aff_t, aff_n = _outproj(y_h.reshape(n, D_HYENA), y_a.reshape(n, D_ATTN), x2,
                                 p["out_norm"][l][None, :], p["w_out"][l], p["mlp_norm"][l][None, :],
                                 p["w_router"][l])
        x2 = _ec_moe(xm, xn, aff_t, aff_n, p["w_gate"][l], p["w_up"][l], p["w_down"][l])
    return x2.reshape(bsz, seq, D_MODEL)


def kernel(x_prompt, x_sample, attn_norm, w_in, conv_w, conv_b, filt_w1, filt_b1, filt_w2, filt_b2, filt_w3,
           filt_freq, filt_decay, hyena_bias, q_a_norm, w_uq, kv_a_norm, w_ukv, q_norm, k_norm, out_norm,
           w_out, mlp_norm, w_router, w_gate, w_up, w_down):
    p = dict(
        attn_norm=attn_norm,
        w_in=jnp.pad(w_in, ((0, 0), (0, 0), (0, D_IN_PAD - D_IN))).astype(BF16),
        conv_w=conv_w, conv_b=conv_b, hyena_bias=hyena_bias,
        filt=[_filter_weights(filt_w1[l], filt_b1[l], filt_w2[l], filt_b2[l], filt_w3[l], filt_freq[l],
                              filt_decay[l]) for l in range(DEPTH)],
        attn=[_attn_weights(q_a_norm[l], w_uq[l], kv_a_norm[l], w_ukv[l], q_norm[l], k_norm[l])
              for l in range(DEPTH)],
        out_norm=out_norm, w_out=w_out.astype(BF16), mlp_norm=mlp_norm,
        w_router=_hi_lo(jnp.pad(w_router, ((0, 0), (0, 0), (0, LANES - N_EXPERTS)))),
        w_gate=w_gate.astype(BF16), w_up=w_up.astype(BF16), w_down=w_down.astype(BF16))
    return (_trunk(x_prompt, p), _trunk(x_sample, p))
```

```python
import functools
import math

import jax
import jax.numpy as jnp
from jax import lax
from jax.experimental import pallas as pl
from jax.experimental.pallas import tpu as pltpu

F32 = jnp.float32
BF16 = jnp.bfloat16
I32 = jnp.int32
HIGHEST = lax.Precision.HIGHEST

D_MODEL = 1024
DEPTH = 2
D_HYENA = 512
HYENA_ORDER = 2
FILTER_BANDS = 16
FILTER_HIDDEN = 64
N_DIR = 2
DECAY_SHIFT = 0.05
N_HEADS = 8
QK_NOPE = 64
QK_ROPE = 32
QK_HEAD = QK_NOPE + QK_ROPE
V_HEAD = 64
V_AUG = V_HEAD + 16
D_ATTN = N_HEADS * V_HEAD
Q_LORA = 256
KV_LORA = 128
ROPE_THETA = 10000.0
N_EXPERTS = 16
EC_CAPACITY = 2
D_EXPERT = 512
EPS = 1e-6
COL_HYENA = (HYENA_ORDER + 1) * D_HYENA
COL_Q = COL_HYENA + Q_LORA
COL_KV = COL_Q + KV_LORA
D_IN = COL_KV + QK_ROPE
D_IN_PAD = 2048

LANES = 128
SLOT = 128
TOK_BLOCK = 256
SLOT_TILE = 256
ALIGN = 16
GATHER_ROWS = 64
COMBINE_FAST_ROWS = 128
ATT_TQ = 512
ATT_QC = 256
ATT_TK = 256
ATT_UNROLL = 16
MIB = 1024 * 1024


def _params(sem, vmem_mib):
    return pltpu.CompilerParams(dimension_semantics=sem, vmem_limit_bytes=vmem_mib * MIB)


def _rms(x):
    return x * lax.rsqrt(jnp.mean(x * x, axis=-1, keepdims=True) + EPS)


def _inproj_body(x_ref, g_ref, w_ref, uh_ref, cq_ref, ckvr_ref):
    xn = _rms(x_ref[...]) * g_ref[...]
    u = jnp.dot(xn.astype(BF16), w_ref[...], preferred_element_type=F32)
    uh_ref[...] = u[:, :COL_HYENA]
    cq_ref[...] = u[:, COL_HYENA:COL_Q]
    ckvr_ref[...] = u[:, COL_Q:]


def _inproj(x2d, g, w_pad):
    n = x2d.shape[0]
    tm = min(512, n)
    return pl.pallas_call(
        _inproj_body,
        grid=(n // tm,),
        in_specs=[pl.BlockSpec((tm, D_MODEL), lambda i: (i, 0)),
                  pl.BlockSpec((1, D_MODEL), lambda i: (0, 0)),
                  pl.BlockSpec((D_MODEL, D_IN_PAD), lambda i: (0, 0))],
        out_specs=[pl.BlockSpec((tm, COL_HYENA), lambda i: (i, 0)),
                   pl.BlockSpec((tm, Q_LORA), lambda i: (i, 0)),
                   pl.BlockSpec((tm, D_IN_PAD - COL_Q), lambda i: (i, 0))],
        out_shape=[jax.ShapeDtypeStruct((n, COL_HYENA), F32),
                   jax.ShapeDtypeStruct((n, Q_LORA), F32),
                   jax.ShapeDtypeStruct((n, D_IN_PAD - COL_Q), F32)],
        compiler_params=_params(("parallel",), 40),
        name="inproj",
    )(x2d, g, w_pad)


def _sconv_body(u_ref, prev_ref, next_ref, w_ref, b_ref, v_ref, x1_ref, x2_ref):
    i = pl.program_id(1)
    last = pl.num_programs(1) - 1
    u = u_ref[0]
    tl = u.shape[0]
    prev_row = jnp.where(i == 0, 0.0, prev_ref[0, 7:8, :])
    next_row = jnp.where(i == last, 0.0, next_ref[0, 0:1, :])
    row = lax.broadcasted_iota(I32, u.shape, 0)
    up = jnp.where(row == 0, prev_row, pltpu.roll(u, 1, axis=0))
    dn = jnp.where(row == tl - 1, next_row, pltpu.roll(u, tl - 1, axis=0))
    y = up * w_ref[0:1, :] + u * w_ref[1:2, :] + dn * w_ref[2:3, :] + b_ref[...]
    rows, flat = v_ref.shape[1:]
    for k, o_ref in enumerate((v_ref, x1_ref, x2_ref)):
        o_ref[0] = y[:, k * D_HYENA:(k + 1) * D_HYENA].reshape(rows, tl // rows, D_HYENA).reshape(rows, flat)


def _sconv(uh, w, b, n2):
    bsz, seq, c = uh.shape
    rows = 8
    tl = rows * n2
    r = tl // 8
    nblk8 = seq // 8
    out = jax.ShapeDtypeStruct((bsz, seq // n2, n2 * D_HYENA), F32)
    ospec = pl.BlockSpec((1, rows, n2 * D_HYENA), lambda bi, i: (bi, i, 0))
    return pl.pallas_call(
        _sconv_body,
        grid=(bsz, seq // tl),
        in_specs=[pl.BlockSpec((1, tl, c), lambda bi, i: (bi, i, 0)),
                  pl.BlockSpec((1, 8, c), lambda bi, i: (bi, jnp.maximum(i * r - 1, 0), 0)),
                  pl.BlockSpec((1, 8, c), lambda bi, i: (bi, jnp.minimum((i + 1) * r, nblk8 - 1), 0)),
                  pl.BlockSpec((3, c), lambda bi, i: (0, 0)),
                  pl.BlockSpec((1, c), lambda bi, i: (0, 0))],
        out_specs=[ospec, ospec, ospec],
        out_shape=[out, out, out],
        compiler_params=_params(("parallel", "parallel"), 40),
        name="sconv",
    )(uh, uh, uh, w, b)


def _filter_body(bands_ref, phase_ref, w1t_ref, w1_ref, b1_ref, w2_ref, b2_ref, fr_ref, w3h_ref, w3l_ref,
                 dec_ref, k_ref, sum_ref, *, seq, n2):
    i = pl.program_id(0)
    hp = k_ref.shape[0] // 2
    cols = sum_ref.shape[1]
    jb = k_ref.shape[1] // cols
    row = lax.broadcasted_iota(I32, (jb * hp, 1), 0)
    n_lo = (row % hp) * n2 + i * jb + row // hp
    n_hi = n_lo + seq
    t_of = lambda n: jnp.where(n < seq, n, 2 * seq - n).astype(F32)
    lane = lax.broadcasted_iota(I32, (jb * hp, LANES), 1)
    t_idx = jnp.where(lane < FILTER_HIDDEN, t_of(n_lo), t_of(n_hi))
    feats = jnp.sin(2.0 * math.pi * bands_ref[...] * t_idx / seq + phase_ref[...])
    fr = fr_ref[...]
    pre = (t_idx / (seq - 1)) * w1t_ref[...] + jnp.dot(feats, w1_ref[...], precision=HIGHEST,
                                                      preferred_element_type=F32)
    h = jnp.sin(fr * (pre + b1_ref[...]))
    h = jnp.sin(fr * (jnp.dot(h, w2_ref[...], precision=HIGHEST, preferred_element_type=F32) + b2_ref[...]))
    h_hi = h.astype(BF16)
    h_lo = (h - h_hi.astype(F32)).astype(BF16)

    @pl.when(i == 0)
    def _():
        sum_ref[...] = jnp.zeros_like(sum_ref)

    for j, n in enumerate((n_lo, n_hi)):
        out = (jnp.dot(h_hi, w3h_ref[j], preferred_element_type=F32)
               + jnp.dot(h_lo, w3h_ref[j], preferred_element_type=F32)
               + jnp.dot(h_hi, w3l_ref[j], preferred_element_type=F32))
        window = jnp.exp(-(t_of(n) / (seq - 1)) * dec_ref[j]) + DECAY_SHIFT
        k = jnp.where(n == seq, 0.0, out * window)
        sum_ref[...] += jnp.sum(jnp.abs(k), axis=0, keepdims=True)
        for jj in range(jb):
            k_ref[j * hp:(j + 1) * hp, jj * cols:(jj + 1) * cols] = k[jj * hp:(jj + 1) * hp].astype(BF16)


def _filters(seq, n1, n2, fw):
    cols = HYENA_ORDER * D_HYENA
    const = lambda shape: pl.BlockSpec(shape, lambda i: (0,) * len(shape))
    jb = max(1, 1024 // n1)
    return pl.pallas_call(
        functools.partial(_filter_body, seq=seq, n2=n2),
        grid=(n2 // jb,),
        in_specs=[const((1, LANES)), const((1, LANES)), const((1, LANES)), const((LANES, LANES)),
                  const((1, LANES)), const((LANES, LANES)), const((1, LANES)), const((1, LANES)),
                  const((N_DIR, LANES, cols)), const((N_DIR, LANES, cols)), const((N_DIR, 1, cols))],
        out_specs=[pl.BlockSpec((n1, jb * cols), lambda i: (0, i)),
                   pl.BlockSpec((1, cols), lambda i: (0, 0))],
        out_shape=[jax.ShapeDtypeStruct((n1, n2 * cols), BF16),
                   jax.ShapeDtypeStruct((1, cols), F32)],
        compiler_params=_params(("arbitrary",), 32),
        name="filter_gen",
    )(fw["bands"], fw["phase"], fw["w1t"], fw["w1"], fw["b1"], fw["w2"], fw["b2"], fw["freq"],
      fw["w3_hi"], fw["w3_lo"], fw["decay"])


def _fft_dims(seq):
    n2 = 128 if 2 * seq >= 32768 else 64
    n1 = 2 * seq // n2
    return n1, n2


def _dft_tables(n1, n2):
    n = n1 * n2
    n1h = n1 // 2
    k1 = jnp.arange(n1h, dtype=I32)[:, None]
    m1 = jnp.arange(n1, dtype=I32)[None, :]
    ang = (2.0 * math.pi / n1) * ((k1 * m1) % n1).astype(F32)
    top = jnp.cos(ang)
    bot = -jnp.sin(ang)
    nyq = jnp.where(m1 % 2 == 0, 1.0, -1.0).astype(F32)
    bot = jnp.concatenate([nyq, bot[1:]], axis=0)
    fa = jnp.concatenate([top, bot], axis=0)
    weight = jnp.where((jnp.arange(n1) % n1h) == 0, 1.0, 2.0).astype(F32) / n
    fi = (fa[:, :n1h] * weight[:, None]).T

    kk = jnp.arange(n1h + 1, dtype=I32)[:, None, None]
    k2 = jnp.arange(n2, dtype=I32)[None, :, None]
    m2 = jnp.arange(n2, dtype=I32)[None, None, :]
    phi = (2.0 * math.pi / n) * ((m2 * (kk + n1 * k2)) % n).astype(F32)
    gr = jnp.cos(phi)
    gi = -jnp.sin(phi)
    blk = jnp.concatenate([jnp.concatenate([gr, -gi], axis=2),
                           jnp.concatenate([gi, gr], axis=2)], axis=1)
    left = (jnp.arange(2 * n2) < n2)[None, :]
    g0 = jnp.where(left, blk[0], 0.0)
    gf = jnp.concatenate([g0[None], blk[1:n1h]], axis=0)
    gnyq = jnp.concatenate([jnp.zeros((2 * n2, n2), F32), blk[n1h][:, :n2]], axis=1)
    return dict(fa_full=fa.astype(BF16), fa_half=fa[:, :n1h].astype(BF16), fi=fi.astype(BF16),
                gf=gf.astype(BF16), gnyq=gnyq.astype(BF16),
                ginv=jnp.transpose(gf, (0, 2, 1)).astype(BF16), m2=gnyq.T.astype(BF16))


def _fft_a_body(x_ref, f_ref, o_ref):
    a = jnp.dot(f_ref[...], x_ref[0].astype(BF16), preferred_element_type=F32)
    n1h = o_ref.shape[2]
    o_ref[0, 0] = a[:n1h].astype(BF16)
    o_ref[0, 1] = a[n1h:].astype(BF16)


def _fft_a(x3, fmat):
    bsz, r, nc = x3.shape
    n1 = fmat.shape[0]
    tn = min(8192, nc)
    return pl.pallas_call(
        _fft_a_body,
        grid=(bsz, nc // tn),
        in_specs=[pl.BlockSpec((1, r, tn), lambda b, j: (b, 0, j)),
                  pl.BlockSpec((n1, r), lambda b, j: (0, 0))],
        out_specs=pl.BlockSpec((1, 2, n1 // 2, tn), lambda b, j: (b, 0, 0, j)),
        out_shape=jax.ShapeDtypeStruct((bsz, 2, n1 // 2, nc), BF16),
        compiler_params=_params(("parallel", "parallel"), 48),
        name="fft_stage_a",
    )(x3, fmat)


def _fft_b_body(ar_ref, ai_ref, g_ref, gn_ref, sc_ref, x_ref, xn_ref, *, kb, n2):
    j = pl.program_id(1)
    c = sc_ref.shape[1]
    inv = 1.0 / sc_ref[...]
    a_re = ar_ref[0, 0].reshape(kb, n2, c)
    a_im = ai_ref[0, 0].reshape(kb, n2, c)
    for kk in range(kb):
        rhs = jnp.concatenate([a_re[kk], a_im[kk]], axis=0)
        x_ref[0, kk] = (jnp.dot(g_ref[kk], rhs, preferred_element_type=F32) * inv).astype(BF16)

    @pl.when(j == 0)
    def _():
        rhs = jnp.concatenate([a_re[0], a_im[0]], axis=0)
        xn_ref[0] = (jnp.dot(gn_ref[...], rhs, preferred_element_type=F32) * inv).astype(BF16)


def _fft_b(a4, tabs, scale):
    bsz, _, n1h, nc = a4.shape
    c = scale.shape[1]
    n2 = nc // c
    kb = 16
    return pl.pallas_call(
        functools.partial(_fft_b_body, kb=kb, n2=n2),
        grid=(bsz, n1h // kb),
        in_specs=[pl.BlockSpec((1, 1, kb, nc), lambda b, j: (b, 0, j, 0)),
                  pl.BlockSpec((1, 1, kb, nc), lambda b, j: (b, 1, j, 0)),
                  pl.BlockSpec((kb, 2 * n2, 2 * n2), lambda b, j: (j, 0, 0)),
                  pl.BlockSpec((2 * n2, 2 * n2), lambda b, j: (0, 0)),
                  pl.BlockSpec((1, c), lambda b, j: (0, 0))],
        out_specs=[pl.BlockSpec((1, kb, 2 * n2, c), lambda b, j: (b, j, 0, 0)),
                   pl.BlockSpec((1, 2 * n2, c), lambda b, j: (b, 0, 0))],
        out_shape=[jax.ShapeDtypeStruct((bsz, n1h, 2 * n2, c), BF16),
                   jax.ShapeDtypeStruct((bsz, 2 * n2, c), BF16)],
        compiler_params=_params(("parallel", "arbitrary"), 48),
        name="fft_stage_b",
    )(a4, a4, tabs["gf"], tabs["gnyq"], scale)


def _cmul(x, k, n2):
    xr, xi = x[:n2], x[n2:]
    kr, ki = k[:n2], k[n2:]
    return jnp.concatenate([xr * kr - xi * ki, xr * ki + xi * kr], axis=0).astype(BF16)


def _spec_body(ar_ref, ai_ref, k_ref, kn_ref, g_ref, gn_ref, gi_ref, m2_ref, o_ref, *, kb, n2):
    j = pl.program_id(1)
    c = ar_ref.shape[3] // n2
    a_re = ar_ref[0, 0].reshape(kb, n2, c)
    a_im = ai_ref[0, 0].reshape(kb, n2, c)

    def through(fwd, rhs, kf, inv):
        x = jnp.dot(fwd, rhs, preferred_element_type=F32)
        return jnp.dot(inv, _cmul(x, kf.astype(F32), n2), preferred_element_type=F32)

    re_rows, im_rows = [], []
    for kk in range(kb):
        rhs = jnp.concatenate([a_re[kk], a_im[kk]], axis=0)
        acc = through(g_ref[kk], rhs, k_ref[0, kk], gi_ref[kk])
        if kk == 0:
            nyq = through(gn_ref[...], rhs, kn_ref[0], m2_ref[...])
            acc = jnp.where(j == 0, jnp.concatenate([acc[:n2], nyq[n2:]], axis=0), acc)
        re_rows.append(acc[:n2].astype(BF16))
        im_rows.append(acc[n2:].astype(BF16))
    o_ref[0, 0] = jnp.stack(re_rows, axis=0).reshape(kb, n2 * c)
    o_ref[0, 1] = jnp.stack(im_rows, axis=0).reshape(kb, n2 * c)


def _spec_conv(a4, kf, kfnyq, order, tabs):
    bsz, _, n1h, nc = a4.shape
    tn2 = kf.shape[2]
    n2 = tn2 // 2
    c = nc // n2
    kb = 16
    return pl.pallas_call(
        functools.partial(_spec_body, kb=kb, n2=n2),
        grid=(bsz, n1h // kb),
        in_specs=[pl.BlockSpec((1, 1, kb, nc), lambda b, j: (b, 0, j, 0)),
                  pl.BlockSpec((1, 1, kb, nc), lambda b, j: (b, 1, j, 0)),
                  pl.BlockSpec((1, kb, tn2, c), lambda b, j: (0, j, 0, order)),
                  pl.BlockSpec((1, tn2, c), lambda b, j: (0, 0, order)),
                  pl.BlockSpec((kb, tn2, tn2), lambda b, j: (j, 0, 0)),
                  pl.BlockSpec((tn2, tn2), lambda b, j: (0, 0)),
                  pl.BlockSpec((kb, tn2, tn2), lambda b, j: (j, 0, 0)),
                  pl.BlockSpec((tn2, tn2), lambda b, j: (0, 0))],
        out_specs=pl.BlockSpec((1, 2, kb, nc), lambda b, j: (b, 0, j, 0)),
        out_shape=jax.ShapeDtypeStruct((bsz, 2, n1h, nc), BF16),
        compiler_params=_params(("parallel", "arbitrary"), 48),
        name="spectral_conv",
    )(a4, a4, kf, kfnyq, tabs["gf"], tabs["gnyq"], tabs["ginv"], tabs["m2"])


def _ifft_a_body(b_ref, f_ref, z_ref, gate_ref, bias_ref, o_ref):
    y = jnp.dot(f_ref[...], b_ref[0], preferred_element_type=F32)
    res = gate_ref[0] * (y + z_ref[0] * bias_ref[...])
    o_ref[0] = res.reshape(o_ref.shape[1:])


def _ifft_a(b3, fi, z3, gate3, bias_t, time_major):
    bsz, n1, nc = b3.shape
    n1h = n1 // 2
    tn = bias_t.shape[1]
    c = D_HYENA
    if time_major:
        out_spec = pl.BlockSpec((1, n1h, tn // c, c), lambda b, j: (b, 0, j, 0))
        out_shape = jax.ShapeDtypeStruct((bsz, n1h, nc // c, c), F32)
    else:
        out_spec = pl.BlockSpec((1, n1h, tn), lambda b, j: (b, 0, j))
        out_shape = jax.ShapeDtypeStruct((bsz, n1h, nc), F32)
    return pl.pallas_call(
        _ifft_a_body,
        grid=(bsz, nc // tn),
        in_specs=[pl.BlockSpec((1, n1, tn), lambda b, j: (b, 0, j)),
                  pl.BlockSpec((n1h, n1), lambda b, j: (0, 0)),
                  pl.BlockSpec((1, n1h, tn), lambda b, j: (b, 0, j)),
                  pl.BlockSpec((1, n1h, tn), lambda b, j: (b, 0, j)),
                  pl.BlockSpec((1, tn), lambda b, j: (0, 0))],
        out_specs=out_spec,
        out_shape=out_shape,
        compiler_params=_params(("parallel", "parallel"), 48),
        name="ifft_stage_a",
    )(b3, fi, z3, gate3, bias_t)


def _hyena(uh, conv_w, conv_b, fw, bias):
    bsz, seq, _ = uh.shape
    c = D_HYENA
    n1, n2 = _fft_dims(seq)
    n1h = n1 // 2
    tabs = _dft_tables(n1, n2)
    v, x1, x2 = _sconv(uh, conv_w, conv_b, n2)

    kcirc, ksum = _filters(seq, n1, n2, fw)
    cols = HYENA_ORDER * c
    ka = _fft_a(kcirc[None], tabs["fa_full"])
    kf, kfnyq = _fft_b(ka, tabs, ksum)

    tn = min(8192, n2 * c)
    z = v
    for order, gate in enumerate((x1, x2)):
        za = _fft_a(z, tabs["fa_half"])
        zb = _spec_conv(za, kf, kfnyq, order, tabs)
        bias_t = jnp.tile(bias[order][None, :], (1, tn // c))
        z = _ifft_a(zb.reshape(bsz, n1, n2 * c), tabs["fi"], z, gate, bias_t,
                    time_major=order == HYENA_ORDER - 1)
    return z.reshape(bsz, seq, c)


def _qkv_body(cq_ref, ckvr_ref, cos_ref, sin_ref, wq_ref, wk_ref, wv_ref, qa_ref, kva_ref, qg_ref, kg_ref,
              qt_ref, k_ref, vt_ref):
    cqn = _rms(cq_ref[0]) * qa_ref[...]
    qs = jnp.dot(cqn.astype(BF16), wq_ref[...], preferred_element_type=F32)
    ck = ckvr_ref[0]
    ckvn = (_rms(ck[:, :KV_LORA]) * kva_ref[...]).astype(BF16)
    ks = jnp.dot(ckvn, wk_ref[...], preferred_element_type=F32)
    vs = jnp.dot(ckvn, wv_ref[...], preferred_element_type=F32)
    krope = pltpu.roll(ck[:, KV_LORA:], QK_NOPE, axis=1)
    cos = cos_ref[...]
    sin = sin_ref[...]
    lane = lax.broadcasted_iota(I32, cos.shape, 1)
    half = QK_ROPE // 2
    first = (lane >= QK_NOPE) & (lane < QK_NOPE + half)
    second = (lane >= QK_NOPE + half) & (lane < QK_HEAD)

    def head(xh, gain, scale):
        ms = jnp.sum(xh * xh, axis=-1, keepdims=True) * (1.0 / QK_HEAD)
        xn = xh * lax.rsqrt(ms + EPS) * gain
        rot = jnp.where(first, -pltpu.roll(xn, SLOT - half, axis=1),
                        jnp.where(second, pltpu.roll(xn, half, axis=1), 0.0))
        return (xn * cos + rot * sin) * scale

    q_scale = QK_HEAD ** -0.5 * math.log2(math.e)
    eye = (lax.broadcasted_iota(I32, (SLOT, SLOT), 0) == lax.broadcasted_iota(I32, (SLOT, SLOT), 1)).astype(BF16)
    transpose = lambda x: lax.dot_general(eye, x.astype(BF16), (((1,), (1,)), ((), ())),
                                          preferred_element_type=F32).astype(BF16)
    for h in range(N_HEADS):
        sl = slice(h * SLOT, (h + 1) * SLOT)
        qt_ref[0, h] = transpose(head(qs[:, sl], qg_ref[...], q_scale))
        k_ref[0, h] = head(ks[:, sl] + krope, kg_ref[...], 1.0).astype(BF16)
    aug = lax.broadcasted_iota(I32, (V_AUG - V_HEAD, cos.shape[0]), 0)
    ones_row = jnp.where(aug == 0, 1.0, 0.0).astype(BF16)
    for hp in range(N_HEADS // 2):
        pair_t = transpose(vs[:, hp * 2 * V_HEAD:(hp + 1) * 2 * V_HEAD])
        vt_ref[0, hp] = jnp.concatenate([pair_t[:V_HEAD], ones_row, pair_t[V_HEAD:], ones_row], axis=0)


def _qkv(cq, ckvr, cos_t, sin_t, aw):
    bsz, seq, _ = cq.shape
    tm = min(512, seq)
    const = lambda shape: pl.BlockSpec(shape, lambda b, i: (0,) * len(shape))
    return pl.pallas_call(
        _qkv_body,
        grid=(bsz, seq // tm),
        in_specs=[pl.BlockSpec((1, tm, Q_LORA), lambda b, i: (b, i, 0)),
                  pl.BlockSpec((1, tm, D_IN_PAD - COL_Q), lambda b, i: (b, i, 0)),
                  pl.BlockSpec((tm, SLOT), lambda b, i: (i, 0)),
                  pl.BlockSpec((tm, SLOT), lambda b, i: (i, 0)),
                  const((Q_LORA, N_HEADS * SLOT)), const((KV_LORA, N_HEADS * SLOT)),
                  const((KV_LORA, D_ATTN)),
                  const((1, Q_LORA)), const((1, KV_LORA)), const((1, SLOT)), const((1, SLOT))],
        out_specs=[pl.BlockSpec((1, N_HEADS, SLOT, tm), lambda b, i: (b, 0, 0, i)),
                   pl.BlockSpec((1, N_HEADS, tm, SLOT), lambda b, i: (b, 0, i, 0)),
                   pl.BlockSpec((1, N_HEADS // 2, 2 * V_AUG, tm), lambda b, i: (b, 0, 0, i))],
        out_shape=[jax.ShapeDtypeStruct((bsz, N_HEADS, SLOT, seq), BF16),
                   jax.ShapeDtypeStruct((bsz, N_HEADS, seq, SLOT), BF16),
                   jax.ShapeDtypeStruct((bsz, N_HEADS // 2, 2 * V_AUG, seq), BF16)],
        compiler_params=_params(("parallel", "parallel"), 40),
        name="qkv_prep",
    )(cq, ckvr, cos_t, sin_t, aw["wq"], aw["wk"], aw["wv"], aw["qa"], aw["kva"], aw["qg"], aw["kg"])


def _flash_body(qt_ref, k_ref, vt_ref, o_ref, s0_ref, s1_ref, acc_ref, *, tk, nk, nqc, unroll):
    chains = [(hh, qc) for hh in range(2) for qc in range(nqc)]
    acc_ref[...] = jnp.zeros(acc_ref.shape, F32)

    def qk(t, s_ref, only=None):
        ks = pl.multiple_of(t * tk, tk)
        tile_max = []
        for c, (hh, qc) in enumerate(chains):
            if only is not None and c != only:
                continue
            k = k_ref[0, hh, pl.ds(ks, tk), :]
            s = jnp.dot(k, qt_ref[0, hh, :, qc * ATT_QC:(qc + 1) * ATT_QC], preferred_element_type=F32)
            s_ref[c] = s
            tile_max.append(jnp.max(s, axis=0, keepdims=True))
        return tile_max

    def softmax_pv(t, s_ref, tile_max, m, only=None):
        ks = pl.multiple_of(t * tk, tk)
        m_out = []
        for c, (hh, qc) in enumerate(chains):
            if only is not None and c != only:
                continue
            i = 0 if only is not None else c
            m_new = jnp.maximum(m[i], tile_max[i])
            a = jnp.exp2(m[i] - m_new)
            p = jnp.exp2(s_ref[c] - m_new)
            m_out.append(m_new)
            vt = vt_ref[0, 0, hh * V_AUG:(hh + 1) * V_AUG, pl.ds(ks, tk)]
            acc_ref[c] = acc_ref[c] * a + jnp.dot(vt, p.astype(BF16), preferred_element_type=F32)
        return m_out

    def group(u, carry):
        tile_max, m = carry
        for i in range(0, unroll, 2):
            t = unroll * u + i
            tm1, tm0 = [], []
            m = list(m)
            for c in range(len(chains)):
                tm1 += qk(t + 1, s1_ref, only=c)
                m[c] = softmax_pv(t, s0_ref, [tile_max[c]], [m[c]], only=c)[0]
            for c in range(len(chains)):
                tm0 += qk(jnp.minimum(t + 2, nk - 1), s0_ref, only=c)
                m[c] = softmax_pv(t + 1, s1_ref, [tm1[c]], [m[c]], only=c)[0]
            tile_max = tm0
        return tile_max, m

    init = (qk(0, s0_ref), [jnp.full((1, ATT_QC), -jnp.inf, F32)] * len(chains))
    lax.fori_loop(0, nk // unroll, group, init)

    def normalised(c):
        acc = acc_ref[c]
        return acc[:V_HEAD] * (1.0 / acc[V_HEAD:V_HEAD + 1])

    heads = [jnp.concatenate([normalised(hh * nqc + qc) for qc in range(nqc)], axis=1) for hh in range(2)]
    o_ref[0] = jnp.concatenate(heads, axis=0).T


def _flash(qt, k, vt):
    bsz, nh, _, seq = qt.shape
    tq = min(ATT_TQ, seq)
    tk = min(ATT_TK, seq)
    nqc = tq // ATT_QC
    nk = seq // tk
    unroll = min(ATT_UNROLL, nk)
    assert nk % unroll == 0 and unroll % 2 == 0
    return pl.pallas_call(
        functools.partial(_flash_body, tk=tk, nk=nk, nqc=nqc, unroll=unroll),
        grid=(bsz, nh // 2, seq // tq),
        in_specs=[pl.BlockSpec((1, 2, SLOT, tq), lambda b, hp, i: (b, hp, 0, i)),
                  pl.BlockSpec((1, 2, seq, SLOT), lambda b, hp, i: (b, hp, 0, 0)),
                  pl.BlockSpec((1, 1, 2 * V_AUG, seq), lambda b, hp, i: (b, hp, 0, 0))],
        out_specs=pl.BlockSpec((1, tq, 2 * V_HEAD), lambda b, hp, i: (b, i, hp)),
        out_shape=jax.ShapeDtypeStruct((bsz, seq, nh * V_HEAD), F32),
        scratch_shapes=[pltpu.VMEM((2 * nqc, tk, ATT_QC), F32), pltpu.VMEM((2 * nqc, tk, ATT_QC), F32),
                        pltpu.VMEM((2 * nqc, V_AUG, ATT_QC), F32)],
        compiler_params=_params(("parallel", "parallel", "parallel"), 48),
        name="flash_attn",
    )(qt, k, vt)


def _outproj_body(yh_ref, ya_ref, x_ref, og_ref, w_ref, mg_ref, wr_ref, xm_ref, xn_ref, aff_ref, affn_ref):
    og = og_ref[...]
    half = D_MODEL // 2
    y = jnp.concatenate([_rms(yh_ref[...]) * og[:, :half], _rms(ya_ref[...]) * og[:, half:]], axis=1)
    xm = x_ref[...] + jnp.dot(y.astype(BF16), w_ref[...], preferred_element_type=F32)
    xm_ref[...] = xm
    xn = _rms(xm) * mg_ref[...]
    xn_hi = xn.astype(BF16)
    xn_ref[...] = xn_hi
    xn_lo = (xn - xn_hi.astype(F32)).astype(BF16)
    logits = (jnp.dot(xn_hi, wr_ref[0], preferred_element_type=F32)
              + jnp.dot(xn_lo, wr_ref[0], preferred_element_type=F32)
              + jnp.dot(xn_hi, wr_ref[1], preferred_element_type=F32))
    lane = lax.broadcasted_iota(I32, logits.shape, 1)
    logits = jnp.where(lane < N_EXPERTS, logits, -jnp.inf)
    e = jnp.exp(logits - jnp.max(logits, axis=-1, keepdims=True))
    aff = e / jnp.sum(e, axis=-1, keepdims=True)
    aff_ref[...] = aff.T[:N_EXPERTS]
    affn_ref[...] = aff[:, :N_EXPERTS]


def _outproj(yh, ya, x2d, og, w_out, mg, wr_pad):
    n = x2d.shape[0]
    tm = min(512, n)
    half = D_MODEL // 2
    const = lambda shape: pl.BlockSpec(shape, lambda i: (0,) * len(shape))
    return pl.pallas_call(
        _outproj_body,
        grid=(n // tm,),
        in_specs=[pl.BlockSpec((tm, half), lambda i: (i, 0)),
                  pl.BlockSpec((tm, half), lambda i: (i, 0)),
                  pl.BlockSpec((tm, D_MODEL), lambda i: (i, 0)),
                  const((1, D_MODEL)), const((D_MODEL, D_MODEL)), const((1, D_MODEL)),
                  const((2, D_MODEL, LANES))],
        out_specs=[pl.BlockSpec((tm, D_MODEL), lambda i: (i, 0)),
                   pl.BlockSpec((tm, D_MODEL), lambda i: (i, 0)),
                   pl.BlockSpec((N_EXPERTS, tm), lambda i: (0, i)),
                   pl.BlockSpec((tm, N_EXPERTS), lambda i: (i, 0))],
        out_shape=[jax.ShapeDtypeStruct((n, D_MODEL), F32),
                   jax.ShapeDtypeStruct((n, D_MODEL), BF16),
                   jax.ShapeDtypeStruct((N_EXPERTS, n), F32),
                   jax.ShapeDtypeStruct((n, N_EXPERTS), F32)],
        compiler_params=_params(("parallel",), 40),
        name="outproj_router",
    )(yh, ya, x2d, og, w_out, mg, wr_pad)


def _select_body(aff_ref, upper_ref, lower_ref, pos_ref, off_ref, *, cap):
    group = aff_ref.shape[0]
    bits = [pltpu.bitcast(aff_ref[x], I32) for x in range(group)]
    upper = upper_ref[...]
    lower = lower_ref[...]

    def count(mask):
        return jnp.sum(jnp.sum(mask.astype(F32), axis=1, keepdims=True), axis=0, keepdims=True)

    def bit_step(i, thr):
        bit = jnp.left_shift(jnp.int32(1), 30 - i)
        return tuple(jnp.where(count(bits[x] >= (thr[x] | bit)) >= cap, thr[x] | bit, thr[x])
                     for x in range(group))

    thr = lax.fori_loop(0, 31, bit_step, tuple(jnp.zeros((1, 1), I32) for _ in range(group)))

    def prefix(mask):
        within = jnp.dot(mask.astype(BF16), upper, preferred_element_type=F32)
        total = within[:, LANES - 1:LANES]
        offs = jnp.dot(lower, jnp.broadcast_to(total, within.shape).astype(BF16), preferred_element_type=F32)
        return within, offs

    for x in range(group):
        gt = bits[x] > thr[x]
        eq = bits[x] == thr[x]
        need = cap - count(gt)
        w_eq, o_eq = prefix(eq)
        sel = gt | (eq & (o_eq + w_eq <= need))
        w_sel, o_sel = prefix(sel)
        pos_ref[x] = jnp.where(sel, o_sel + w_sel - 1.0, -1.0).astype(I32)
        off_ref[x] = o_sel.astype(I32)


def _select(aff_t, cap):
    ne, n = aff_t.shape
    rows = n // LANES
    upper = (jnp.arange(LANES)[:, None] <= jnp.arange(LANES)[None, :]).astype(BF16)
    lower = (jnp.arange(rows)[None, :] < jnp.arange(rows)[:, None]).astype(BF16)
    group = 4
    blk = pl.BlockSpec((group, rows, LANES), lambda e: (e, 0, 0))
    pos, off = pl.pallas_call(
        functools.partial(_select_body, cap=cap),
        grid=(ne // group,),
        in_specs=[blk, pl.BlockSpec((LANES, LANES), lambda e: (0, 0)),
                  pl.BlockSpec((rows, rows), lambda e: (0, 0))],
        out_specs=[blk, blk],
        out_shape=[jax.ShapeDtypeStruct((ne, rows, LANES), I32),
                   jax.ShapeDtypeStruct((ne, rows, LANES), I32)],
        compiler_params=_params(("parallel",), 32),
        name="ec_select",
    )(aff_t.reshape(ne, rows, LANES), upper, lower)
    return pos.reshape(ne, n), off[:, :, 0]


def _ffn_body(off_ref, nch_ref, pos_ref, x_ref, wg_ref, wu_ref, wd_ref, y_ref, xs_ref, *, nb, sub, cap):
    group = y_ref.shape[0]
    e0 = pl.program_id(0) * group
    sb = pl.program_id(1)

    @pl.when(sb == 0)
    def _():
        xs_ref[...] = jnp.zeros(xs_ref.shape, BF16)

    row = lax.broadcasted_iota(I32, (GATHER_ROWS, TOK_BLOCK), 0)

    def first_chunks(i):
        tok = slice(i * TOK_BLOCK, (i + 1) * TOK_BLOCK)
        starts = [pl.multiple_of(off_ref[(e0 + x) * nb + sb * sub + i] * ALIGN, ALIGN) for x in range(group)]
        hits = [(pos_ref[x, :, tok] - starts[x]) == row for x in range(group)]
        win = jnp.dot(jnp.concatenate(hits, axis=0).astype(BF16), x_ref[tok, :], preferred_element_type=F32)
        for x in range(group):
            xs_ref[x, pl.ds(starts[x], GATHER_ROWS), :] += win[x * GATHER_ROWS:(x + 1) * GATHER_ROWS].astype(BF16)

    def later_chunk(x, i, c):
        tok = slice(i * TOK_BLOCK, (i + 1) * TOK_BLOCK)
        start = pl.multiple_of(off_ref[(e0 + x) * nb + sb * sub + i] * ALIGN + c * GATHER_ROWS, ALIGN)
        hit = (pos_ref[x, :, tok] - start) == row
        win = jnp.dot(hit.astype(BF16), x_ref[tok, :], preferred_element_type=F32)
        xs_ref[x, pl.ds(start, GATHER_ROWS), :] += win.astype(BF16)

    for i in range(sub):
        first_chunks(i)
    for x in range(group):
        for i in range(sub):
            lax.fori_loop(1, nch_ref[(e0 + x) * nb + sb * sub + i],
                          lambda c, carry, x=x, i=i: (later_chunk(x, i, c), carry)[1], 0)

    @pl.when(sb == pl.num_programs(1) - 1)
    def _():
        for x in range(group):
            for j in range(cap // SLOT_TILE):
                rows = slice(j * SLOT_TILE, (j + 1) * SLOT_TILE)
                xt = xs_ref[x, rows, :]
                g = jnp.dot(xt, wg_ref[x], preferred_element_type=F32)
                u = jnp.dot(xt, wu_ref[x], preferred_element_type=F32)
                h = (g * jax.nn.sigmoid(g) * u).astype(BF16)
                y_ref[x, rows, :] = jnp.dot(h, wd_ref[x], preferred_element_type=F32).astype(BF16)
            y_ref[x, cap:, :] = jnp.zeros((y_ref.shape[1] - cap, D_MODEL), BF16)


def _ffn(base, nchunk, pos3, xn, wg, wu, wd, cap):
    ne = wg.shape[0]
    n = xn.shape[0]
    nb = n // TOK_BLOCK
    sub = min(4, nb)
    group = 2
    cap_pad = cap + TOK_BLOCK + ALIGN
    grid_spec = pltpu.PrefetchScalarGridSpec(
        num_scalar_prefetch=2,
        grid=(ne // group, nb // sub),
        in_specs=[pl.BlockSpec((group, 1, sub * TOK_BLOCK), lambda e, s, off, nch: (e, 0, s)),
                  pl.BlockSpec((sub * TOK_BLOCK, D_MODEL), lambda e, s, off, nch: (s, 0)),
                  pl.BlockSpec((group, D_MODEL, D_EXPERT), lambda e, s, off, nch: (e, 0, 0)),
                  pl.BlockSpec((group, D_MODEL, D_EXPERT), lambda e, s, off, nch: (e, 0, 0)),
                  pl.BlockSpec((group, D_EXPERT, D_MODEL), lambda e, s, off, nch: (e, 0, 0))],
        out_specs=pl.BlockSpec((group, cap_pad, D_MODEL), lambda e, s, off, nch: (e, 0, 0)),
        scratch_shapes=[pltpu.VMEM((group, cap_pad, D_MODEL), BF16)])
    return pl.pallas_call(
        functools.partial(_ffn_body, nb=nb, sub=sub, cap=cap),
        grid_spec=grid_spec,
        out_shape=jax.ShapeDtypeStruct((ne, cap_pad, D_MODEL), BF16),
        compiler_params=_params(("parallel", "arbitrary"), 52),
        name="ec_ffn",
    )(base, nchunk, pos3, xn, wg, wu, wd)


def _combine_body(off_ref, spill_ref, post_ref, affn_ref, xm_ref, *refs, nb, ne, win_rows):
    win_refs, tail_refs, o_ref = refs[:ne], refs[ne:-1], refs[-1]
    b = pl.program_id(0)
    post = post_ref[...]
    affn = affn_ref[...]

    def expand(y_refs, shift):
        width = y_refs[0].shape[0]
        lane = lax.broadcasted_iota(I32, (TOK_BLOCK, width), 1)
        total = jnp.zeros((TOK_BLOCK, D_MODEL), F32)
        for e in range(ne):
            rel = post[:, e:e + 1] - (off_ref[e * nb + b] * ALIGN + shift)
            hit = (rel == lane).astype(BF16)
            total = total + affn[:, e:e + 1] * jnp.dot(hit, y_refs[e][...], preferred_element_type=F32)
        return total

    def expand_pairs(y_refs):
        width = y_refs[0].shape[0]
        lane = lax.broadcasted_iota(I32, (TOK_BLOCK, 2 * width), 1)
        total = jnp.zeros((TOK_BLOCK, D_MODEL), F32)
        for e in range(0, ne, 2):
            rel0 = post[:, e:e + 1] - off_ref[e * nb + b] * ALIGN
            rel1 = post[:, e + 1:e + 2] - off_ref[(e + 1) * nb + b] * ALIGN + width
            gates = (jnp.where(rel0 == lane, affn[:, e:e + 1], 0.0)
                     + jnp.where((rel1 == lane) & (rel1 >= width), affn[:, e + 1:e + 2], 0.0))
            rows = jnp.concatenate([y_refs[e][...], y_refs[e + 1][...]], axis=0)
            total = total + jnp.dot(gates.astype(BF16), rows, preferred_element_type=F32)
        return total

    if not tail_refs and 2 * win_rows <= TOK_BLOCK and ne % 2 == 0:
        o_ref[...] = xm_ref[...] + expand_pairs(win_refs)
    else:
        o_ref[...] = xm_ref[...] + expand(win_refs, 0)

    if tail_refs:
        @pl.when(spill_ref[b] != 0)
        def _():
            o_ref[...] += expand(tail_refs, win_rows)


def _combine(base, spill, pos_t, aff_n, xm, y, win_rows):
    ne = y.shape[0]
    n = xm.shape[0]
    nb = n // TOK_BLOCK
    tail_rows = ALIGN if win_rows == TOK_BLOCK else 0

    def window(e, rows, shift):
        return pl.BlockSpec((pl.Squeezed(), pl.Element(rows), pl.Element(D_MODEL)),
                            lambda b, off, sp: (e, (off[e * nb + b] + shift // ALIGN) * ALIGN, 0))

    tails = [window(e, tail_rows, win_rows) for e in range(ne)] if tail_rows else []
    grid_spec = pltpu.PrefetchScalarGridSpec(
        num_scalar_prefetch=2,
        grid=(nb,),
        in_specs=([pl.BlockSpec((TOK_BLOCK, ne), lambda b, off, sp: (b, 0)),
                   pl.BlockSpec((TOK_BLOCK, ne), lambda b, off, sp: (b, 0)),
                   pl.BlockSpec((TOK_BLOCK, D_MODEL), lambda b, off, sp: (b, 0))]
                  + [window(e, win_rows, 0) for e in range(ne)] + tails),
        out_specs=pl.BlockSpec((TOK_BLOCK, D_MODEL), lambda b, off, sp: (b, 0)))
    return pl.pallas_call(
        functools.partial(_combine_body, nb=nb, ne=ne, win_rows=win_rows),
        grid_spec=grid_spec,
        out_shape=jax.ShapeDtypeStruct((n, D_MODEL), F32),
        compiler_params=_params(("parallel",), 48),
        name="ec_combine",
    )(base, spill, pos_t, aff_n, xm, *([y] * (ne + len(tails))))


def _ec_moe(xm, xn, aff_t, aff_n, wg, wu, wd):
    ne, n = aff_t.shape
    cap = max(1, EC_CAPACITY * n // N_EXPERTS)
    nb = n // TOK_BLOCK
    pos, off = _select(aff_t, cap)
    first = off[:, ::TOK_BLOCK // LANES]
    count = jnp.concatenate([first[:, 1:], jnp.full((ne, 1), cap, I32)], axis=1) - first
    base = first // ALIGN
    span = jnp.where(count > 0, first - base * ALIGN + count, 0)
    spill = jnp.any(span > TOK_BLOCK, axis=0).astype(I32)
    nchunk = ((span + GATHER_ROWS - 1) // GATHER_ROWS).reshape(ne * nb).astype(I32)
    base = base.reshape(ne * nb).astype(I32)
    y = _ffn(base, nchunk, pos.reshape(ne, 1, n), xn, wg, wu, wd, cap)
    pos_t = pos.T
    return lax.cond(jnp.max(span) <= COMBINE_FAST_ROWS,
                    lambda: _combine(base, spill, pos_t, aff_n, xm, y, COMBINE_FAST_ROWS),
                    lambda: _combine(base, spill, pos_t, aff_n, xm, y, TOK_BLOCK))


def _slot_cols(w, head_width):
    k = w.shape[0]
    w3 = w.reshape(k, N_HEADS, head_width)
    return jnp.pad(w3, ((0, 0), (0, 0), (0, SLOT - head_width))).reshape(k, N_HEADS * SLOT)


def _attn_weights(q_a_norm, w_uq, kv_a_norm, w_ukv, q_norm, k_norm):
    wkv = w_ukv.reshape(KV_LORA, N_HEADS, QK_NOPE + V_HEAD)
    wk = _slot_cols(wkv[:, :, :QK_NOPE].reshape(KV_LORA, -1), QK_NOPE)
    wv = wkv[:, :, QK_NOPE:].reshape(KV_LORA, D_ATTN)
    pad_gain = lambda g: jnp.pad(g, (0, SLOT - QK_HEAD))[None, :]
    return dict(wq=_slot_cols(w_uq, QK_HEAD).astype(BF16), wk=wk.astype(BF16), wv=wv.astype(BF16),
                qa=q_a_norm[None, :], kva=kv_a_norm[None, :], qg=pad_gain(q_norm), kg=pad_gain(k_norm))


def _rope_tables(seq):
    pos = jnp.arange(seq, dtype=F32)
    inv_freq = ROPE_THETA ** (-jnp.arange(0, QK_ROPE, 2, dtype=F32) / QK_ROPE)
    ang = pos[:, None] * inv_freq
    ang = jnp.concatenate([ang, ang], axis=-1)
    pad = lambda t, fill: jnp.concatenate(
        [jnp.full((seq, QK_NOPE), fill, F32), t, jnp.full((seq, SLOT - QK_HEAD), fill, F32)], axis=1)
    return pad(jnp.cos(ang), 1.0), pad(jnp.sin(ang), 0.0)


def _hi_lo(w):
    hi = w.astype(BF16)
    return jnp.stack([hi, (w - hi.astype(F32)).astype(BF16)], axis=1)


def _filter_weights(w1, b1, w2, b2, w3, freq, decay):
    hid, nbands = FILTER_HIDDEN, FILTER_BANDS
    assert 2 * hid == LANES and 2 * nbands <= hid
    both = lambda row: jnp.concatenate([row, row])[None, :]
    blockdiag = lambda a: jnp.concatenate(
        [jnp.concatenate([a, jnp.zeros_like(a)], axis=1), jnp.concatenate([jnp.zeros_like(a), a], axis=1)], axis=0)
    bands = jnp.linspace(1e-4, nbands - 1, nbands, dtype=F32)
    rest = jnp.zeros((hid - 2 * nbands,), F32)
    band_half = jnp.concatenate([bands, bands, rest])
    phase_half = jnp.concatenate([jnp.zeros((nbands,), F32), jnp.full((nbands,), 0.5 * math.pi, F32), rest])
    w1_half = jnp.concatenate([-w1[1 + nbands:], w1[1:1 + nbands], jnp.zeros((hid - 2 * nbands, hid), F32)], axis=0)
    cols = HYENA_ORDER * D_HYENA
    w3d = jnp.transpose(w3.reshape(hid, N_DIR, cols), (1, 0, 2))
    w3p = jnp.stack([jnp.pad(w3d[0], ((0, hid), (0, 0))), jnp.pad(w3d[1], ((hid, 0), (0, 0)))], axis=0)
    w3_hi = w3p.astype(BF16)
    w3_lo = (w3p - w3_hi.astype(F32)).astype(BF16)
    return dict(bands=both(band_half), phase=both(phase_half), w1t=both(w1[0]), w1=blockdiag(w1_half),
                b1=both(b1), w2=blockdiag(w2), b2=both(b2), freq=both(freq), w3_hi=w3_hi, w3_lo=w3_lo,
                decay=decay.reshape(N_DIR, 1, cols))


def _trunk(x, p):
    bsz, seq, _ = x.shape
    n = bsz * seq
    cos_t, sin_t = _rope_tables(seq)
    x2 = x.reshape(n, D_MODEL)
    for l in range(DEPTH):
        uh, cq, ckvr = _inproj(x2, p["attn_norm"][l][None, :], p["w_in"][l])
        y_h = _hyena(uh.reshape(bsz, seq, COL_HYENA), p["conv_w"][l], p["conv_b"][l][None, :],
                     p["filt"][l], p["hyena_bias"][l])
        qt, k, vt = _qkv(cq.reshape(bsz, seq, -1), ckvr.reshape(bsz, seq, -1), cos_t, sin_t, p["attn"][l])
        y_a = _flash(qt, k, vt)
        xm, xn, aff_t, aff_n = _outproj(y_h.reshape(n, D_HYENA), y_a.reshape(n, D_ATTN), x2,
                                 p["out_norm"][l][None, :], p["w_out"][l], p["mlp_norm"][l][None, :],
                                 p["w_router"][l])
        x2 = _ec_moe(xm, xn, aff_t, aff_n, p["w_gate"][l], p["w_up"][l], p["w_down"][l])
    return x2.reshape(bsz, seq, D_MODEL)


def kernel(x_prompt, x_sample, attn_norm, w_in, conv_w, conv_b, filt_w1, filt_b1, filt_w2, filt_b2, filt_w3,
           filt_freq, filt_decay, hyena_bias, q_a_norm, w_uq, kv_a_norm, w_ukv, q_norm, k_norm, out_norm,
           w_out, mlp_norm, w_router, w_gate, w_up, w_down):
    p = dict(
        attn_norm=attn_norm,
        w_in=jnp.pad(w_in, ((0, 0), (0, 0), (0, D_IN_PAD - D_IN))).astype(BF16),
        conv_w=conv_w, conv_b=conv_b, hyena_bias=hyena_bias,
        filt=[_filter_weights(filt_w1[l], filt_b1[l], filt_w2[l], filt_b2[l], filt_w3[l], filt_freq[l],
                              filt_decay[l]) for l in range(DEPTH)],
        attn=[_attn_weights(q_a_norm[l], w_uq[l], kv_a_norm[l], w_ukv[l], q_norm[l], k_norm[l])
              for l in range(DEPTH)],
        out_norm=out_norm, w_out=w_out.astype(BF16), mlp_norm=mlp_norm,
        w_router=_hi_lo(jnp.pad(w_router, ((0, 0), (0, 0), (0, LANES - N_EXPERTS)))),
        w_gate=w_gate.astype(BF16), w_up=w_up.astype(BF16), w_down=w_down.astype(BF16))
    return (_trunk(x_prompt, p), _trunk(x_sample, p))
```

```python
import functools
import math

import jax
import jax.numpy as jnp
from jax import lax
from jax.experimental import pallas as pl
from jax.experimental.pallas import tpu as pltpu

F32 = jnp.float32
BF16 = jnp.bfloat16
I32 = jnp.int32
HIGHEST = lax.Precision.HIGHEST

D_MODEL = 1024
DEPTH = 2
D_HYENA = 512
HYENA_ORDER = 2
FILTER_BANDS = 16
FILTER_HIDDEN = 64
N_DIR = 2
DECAY_SHIFT = 0.05
N_HEADS = 8
QK_NOPE = 64
QK_ROPE = 32
QK_HEAD = QK_NOPE + QK_ROPE
V_HEAD = 64
V_AUG = V_HEAD + 16
D_ATTN = N_HEADS * V_HEAD
Q_LORA = 256
KV_LORA = 128
ROPE_THETA = 10000.0
N_EXPERTS = 16
EC_CAPACITY = 2
D_EXPERT = 512
EPS = 1e-6
COL_HYENA = (HYENA_ORDER + 1) * D_HYENA
COL_Q = COL_HYENA + Q_LORA
COL_KV = COL_Q + KV_LORA
D_IN = COL_KV + QK_ROPE
D_IN_PAD = 2048

LANES = 128
SLOT = 128
TOK_BLOCK = 256
SLOT_TILE = 256
ALIGN = 16
GATHER_ROWS = 64
COMBINE_FAST_ROWS = 128
ATT_TQ = 512
ATT_QC = 256
ATT_TK = 256
ATT_UNROLL = 16
MIB = 1024 * 1024


def _params(sem, vmem_mib):
    return pltpu.CompilerParams(dimension_semantics=sem, vmem_limit_bytes=vmem_mib * MIB)


def _rms(x):
    return x * lax.rsqrt(jnp.mean(x * x, axis=-1, keepdims=True) + EPS)


def _inproj_body(x_ref, g_ref, w_ref, uh_ref, cq_ref, ckvr_ref):
    xn = _rms(x_ref[...]) * g_ref[...]
    u = jnp.dot(xn.astype(BF16), w_ref[...], preferred_element_type=F32)
    uh_ref[...] = u[:, :COL_HYENA]
    cq_ref[...] = u[:, COL_HYENA:COL_Q]
    ckvr_ref[...] = u[:, COL_Q:]


def _inproj(x2d, g, w_pad):
    n = x2d.shape[0]
    tm = min(512, n)
    return pl.pallas_call(
        _inproj_body,
        grid=(n // tm,),
        in_specs=[pl.BlockSpec((tm, D_MODEL), lambda i: (i, 0)),
                  pl.BlockSpec((1, D_MODEL), lambda i: (0, 0)),
                  pl.BlockSpec((D_MODEL, D_IN_PAD), lambda i: (0, 0))],
        out_specs=[pl.BlockSpec((tm, COL_HYENA), lambda i: (i, 0)),
                   pl.BlockSpec((tm, Q_LORA), lambda i: (i, 0)),
                   pl.BlockSpec((tm, D_IN_PAD - COL_Q), lambda i: (i, 0))],
        out_shape=[jax.ShapeDtypeStruct((n, COL_HYENA), F32),
                   jax.ShapeDtypeStruct((n, Q_LORA), F32),
                   jax.ShapeDtypeStruct((n, D_IN_PAD - COL_Q), F32)],
        compiler_params=_params(("parallel",), 40),
        name="inproj",
    )(x2d, g, w_pad)


def _sconv_body(u_ref, prev_ref, next_ref, w_ref, b_ref, v_ref, x1_ref, x2_ref):
    i = pl.program_id(1)
    last = pl.num_programs(1) - 1
    u = u_ref[0]
    tl = u.shape[0]
    prev_row = jnp.where(i == 0, 0.0, prev_ref[0, 7:8, :])
    next_row = jnp.where(i == last, 0.0, next_ref[0, 0:1, :])
    row = lax.broadcasted_iota(I32, u.shape, 0)
    up = jnp.where(row == 0, prev_row, pltpu.roll(u, 1, axis=0))
    dn = jnp.where(row == tl - 1, next_row, pltpu.roll(u, tl - 1, axis=0))
    y = up * w_ref[0:1, :] + u * w_ref[1:2, :] + dn * w_ref[2:3, :] + b_ref[...]
    rows, flat = v_ref.shape[1:]
    for k, o_ref in enumerate((v_ref, x1_ref, x2_ref)):
        o_ref[0] = y[:, k * D_HYENA:(k + 1) * D_HYENA].reshape(rows, tl // rows, D_HYENA).reshape(rows, flat)


def _sconv(uh, w, b, n2):
    bsz, seq, c = uh.shape
    rows = 8
    tl = rows * n2
    r = tl // 8
    nblk8 = seq // 8
    out = jax.ShapeDtypeStruct((bsz, seq // n2, n2 * D_HYENA), F32)
    ospec = pl.BlockSpec((1, rows, n2 * D_HYENA), lambda bi, i: (bi, i, 0))
    return pl.pallas_call(
        _sconv_body,
        grid=(bsz, seq // tl),
        in_specs=[pl.BlockSpec((1, tl, c), lambda bi, i: (bi, i, 0)),
                  pl.BlockSpec((1, 8, c), lambda bi, i: (bi, jnp.maximum(i * r - 1, 0), 0)),
                  pl.BlockSpec((1, 8, c), lambda bi, i: (bi, jnp.minimum((i + 1) * r, nblk8 - 1), 0)),
                  pl.BlockSpec((3, c), lambda bi, i: (0, 0)),
                  pl.BlockSpec((1, c), lambda bi, i: (0, 0))],
        out_specs=[ospec, ospec, ospec],
        out_shape=[out, out, out],
        compiler_params=_params(("parallel", "parallel"), 40),
        name="sconv",
    )(uh, uh, uh, w, b)


def _filter_body(bands_ref, phase_ref, w1t_ref, w1_ref, b1_ref, w2_ref, b2_ref, fr_ref, w3h_ref, w3l_ref,
                 dec_ref, k_ref, sum_ref, *, seq, n2):
    i = pl.program_id(0)
    hp = k_ref.shape[0] // 2
    cols = sum_ref.shape[1]
    jb = k_ref.shape[1] // cols
    row = lax.broadcasted_iota(I32, (jb * hp, 1), 0)
    n_lo = (row % hp) * n2 + i * jb + row // hp
    n_hi = n_lo + seq
    t_of = lambda n: jnp.where(n < seq, n, 2 * seq - n).astype(F32)
    lane = lax.broadcasted_iota(I32, (jb * hp, LANES), 1)
    t_idx = jnp.where(lane < FILTER_HIDDEN, t_of(n_lo), t_of(n_hi))
    feats = jnp.sin(2.0 * math.pi * bands_ref[...] * t_idx / seq + phase_ref[...])
    fr = fr_ref[...]
    pre = (t_idx / (seq - 1)) * w1t_ref[...] + jnp.dot(feats, w1_ref[...], precision=HIGHEST,
                                                      preferred_element_type=F32)
    h = jnp.sin(fr * (pre + b1_ref[...]))
    h = jnp.sin(fr * (jnp.dot(h, w2_ref[...], precision=HIGHEST, preferred_element_type=F32) + b2_ref[...]))
    h_hi = h.astype(BF16)
    h_lo = (h - h_hi.astype(F32)).astype(BF16)

    @pl.when(i == 0)
    def _():
        sum_ref[...] = jnp.zeros_like(sum_ref)

    for j, n in enumerate((n_lo, n_hi)):
        out = (jnp.dot(h_hi, w3h_ref[j], preferred_element_type=F32)
               + jnp.dot(h_lo, w3h_ref[j], preferred_element_type=F32)
               + jnp.dot(h_hi, w3l_ref[j], preferred_element_type=F32))
        window = jnp.exp(-(t_of(n) / (seq - 1)) * dec_ref[j]) + DECAY_SHIFT
        k = jnp.where(n == seq, 0.0, out * window)
        sum_ref[...] += jnp.sum(jnp.abs(k), axis=0, keepdims=True)
        for jj in range(jb):
            k_ref[j * hp:(j + 1) * hp, jj * cols:(jj + 1) * cols] = k[jj * hp:(jj + 1) * hp].astype(BF16)


def _filters(seq, n1, n2, fw):
    cols = HYENA_ORDER * D_HYENA
    const = lambda shape: pl.BlockSpec(shape, lambda i: (0,) * len(shape))
    jb = max(1, 1024 // n1)
    return pl.pallas_call(
        functools.partial(_filter_body, seq=seq, n2=n2),
        grid=(n2 // jb,),
        in_specs=[const((1, LANES)), const((1, LANES)), const((1, LANES)), const((LANES, LANES)),
                  const((1, LANES)), const((LANES, LANES)), const((1, LANES)), const((1, LANES)),
                  const((N_DIR, LANES, cols)), const((N_DIR, LANES, cols)), const((N_DIR, 1, cols))],
        out_specs=[pl.BlockSpec((n1, jb * cols), lambda i: (0, i)),
                   pl.BlockSpec((1, cols), lambda i: (0, 0))],
        out_shape=[jax.ShapeDtypeStruct((n1, n2 * cols), BF16),
                   jax.ShapeDtypeStruct((1, cols), F32)],
        compiler_params=_params(("arbitrary",), 32),
        name="filter_gen",
    )(fw["bands"], fw["phase"], fw["w1t"], fw["w1"], fw["b1"], fw["w2"], fw["b2"], fw["freq"],
      fw["w3_hi"], fw["w3_lo"], fw["decay"])


def _fft_dims(seq):
    n2 = 128 if 2 * seq >= 32768 else 64
    n1 = 2 * seq // n2
    return n1, n2


def _dft_tables(n1, n2):
    n = n1 * n2
    n1h = n1 // 2
    k1 = jnp.arange(n1h, dtype=I32)[:, None]
    m1 = jnp.arange(n1, dtype=I32)[None, :]
    ang = (2.0 * math.pi / n1) * ((k1 * m1) % n1).astype(F32)
    top = jnp.cos(ang)
    bot = -jnp.sin(ang)
    nyq = jnp.where(m1 % 2 == 0, 1.0, -1.0).astype(F32)
    bot = jnp.concatenate([nyq, bot[1:]], axis=0)
    fa = jnp.concatenate([top, bot], axis=0)
    weight = jnp.where((jnp.arange(n1) % n1h) == 0, 1.0, 2.0).astype(F32) / n
    fi = (fa[:, :n1h] * weight[:, None]).T

    kk = jnp.arange(n1h + 1, dtype=I32)[:, None, None]
    k2 = jnp.arange(n2, dtype=I32)[None, :, None]
    m2 = jnp.arange(n2, dtype=I32)[None, None, :]
    phi = (2.0 * math.pi / n) * ((m2 * (kk + n1 * k2)) % n).astype(F32)
    gr = jnp.cos(phi)
    gi = -jnp.sin(phi)
    blk = jnp.concatenate([jnp.concatenate([gr, -gi], axis=2),
                           jnp.concatenate([gi, gr], axis=2)], axis=1)
    left = (jnp.arange(2 * n2) < n2)[None, :]
    g0 = jnp.where(left, blk[0], 0.0)
    gf = jnp.concatenate([g0[None], blk[1:n1h]], axis=0)
    gnyq = jnp.concatenate([jnp.zeros((2 * n2, n2), F32), blk[n1h][:, :n2]], axis=1)
    return dict(fa_full=fa.astype(BF16), fa_half=fa[:, :n1h].astype(BF16), fi=fi.astype(BF16),
                gf=gf.astype(BF16), gnyq=gnyq.astype(BF16),
                ginv=jnp.transpose(gf, (0, 2, 1)).astype(BF16), m2=gnyq.T.astype(BF16))


def _fft_a_body(x_ref, f_ref, o_ref):
    a = jnp.dot(f_ref[...], x_ref[0].astype(BF16), preferred_element_type=F32)
    n1h = o_ref.shape[2]
    o_ref[0, 0] = a[:n1h].astype(BF16)
    o_ref[0, 1] = a[n1h:].astype(BF16)


def _fft_a(x3, fmat):
    bsz, r, nc = x3.shape
    n1 = fmat.shape[0]
    tn = min(8192, nc)
    return pl.pallas_call(
        _fft_a_body,
        grid=(bsz, nc // tn),
        in_specs=[pl.BlockSpec((1, r, tn), lambda b, j: (b, 0, j)),
                  pl.BlockSpec((n1, r), lambda b, j: (0, 0))],
        out_specs=pl.BlockSpec((1, 2, n1 // 2, tn), lambda b, j: (b, 0, 0, j)),
        out_shape=jax.ShapeDtypeStruct((bsz, 2, n1 // 2, nc), BF16),
        compiler_params=_params(("parallel", "parallel"), 48),
        name="fft_stage_a",
    )(x3, fmat)


def _fft_b_body(ar_ref, ai_ref, g_ref, gn_ref, sc_ref, x_ref, xn_ref, *, kb, n2):
    j = pl.program_id(1)
    c = sc_ref.shape[1]
    inv = 1.0 / sc_ref[...]
    a_re = ar_ref[0, 0].reshape(kb, n2, c)
    a_im = ai_ref[0, 0].reshape(kb, n2, c)
    for kk in range(kb):
        rhs = jnp.concatenate([a_re[kk], a_im[kk]], axis=0)
        x_ref[0, kk] = (jnp.dot(g_ref[kk], rhs, preferred_element_type=F32) * inv).astype(BF16)

    @pl.when(j == 0)
    def _():
        rhs = jnp.concatenate([a_re[0], a_im[0]], axis=0)
        xn_ref[0] = (jnp.dot(gn_ref[...], rhs, preferred_element_type=F32) * inv).astype(BF16)


def _fft_b(a4, tabs, scale):
    bsz, _, n1h, nc = a4.shape
    c = scale.shape[1]
    n2 = nc // c
    kb = 16
    return pl.pallas_call(
        functools.partial(_fft_b_body, kb=kb, n2=n2),
        grid=(bsz, n1h // kb),
        in_specs=[pl.BlockSpec((1, 1, kb, nc), lambda b, j: (b, 0, j, 0)),
                  pl.BlockSpec((1, 1, kb, nc), lambda b, j: (b, 1, j, 0)),
                  pl.BlockSpec((kb, 2 * n2, 2 * n2), lambda b, j: (j, 0, 0)),
                  pl.BlockSpec((2 * n2, 2 * n2), lambda b, j: (0, 0)),
                  pl.BlockSpec((1, c), lambda b, j: (0, 0))],
        out_specs=[pl.BlockSpec((1, kb, 2 * n2, c), lambda b, j: (b, j, 0, 0)),
                   pl.BlockSpec((1, 2 * n2, c), lambda b, j: (b, 0, 0))],
        out_shape=[jax.ShapeDtypeStruct((bsz, n1h, 2 * n2, c), BF16),
                   jax.ShapeDtypeStruct((bsz, 2 * n2, c), BF16)],
        compiler_params=_params(("parallel", "arbitrary"), 48),
        name="fft_stage_b",
    )(a4, a4, tabs["gf"], tabs["gnyq"], scale)


def _cmul(x, k, n2):
    xr, xi = x[:n2], x[n2:]
    kr, ki = k[:n2], k[n2:]
    return jnp.concatenate([xr * kr - xi * ki, xr * ki + xi * kr], axis=0).astype(BF16)


def _spec_body(ar_ref, ai_ref, k_ref, kn_ref, g_ref, gn_ref, gi_ref, m2_ref, o_ref, *, kb, n2):
    j = pl.program_id(1)
    c = ar_ref.shape[3] // n2
    a_re = ar_ref[0, 0].reshape(kb, n2, c)
    a_im = ai_ref[0, 0].reshape(kb, n2, c)

    def through(fwd, rhs, kf, inv):
        x = jnp.dot(fwd, rhs, preferred_element_type=F32)
        return jnp.dot(inv, _cmul(x, kf.astype(F32), n2), preferred_element_type=F32)

    re_rows, im_rows = [], []
    for kk in range(kb):
        rhs = jnp.concatenate([a_re[kk], a_im[kk]], axis=0)
        acc = through(g_ref[kk], rhs, k_ref[0, kk], gi_ref[kk])
        if kk == 0:
            nyq = through(gn_ref[...], rhs, kn_ref[0], m2_ref[...])
            acc = jnp.where(j == 0, jnp.concatenate([acc[:n2], nyq[n2:]], axis=0), acc)
        re_rows.append(acc[:n2].astype(BF16))
        im_rows.append(acc[n2:].astype(BF16))
    o_ref[0, 0] = jnp.stack(re_rows, axis=0).reshape(kb, n2 * c)
    o_ref[0, 1] = jnp.stack(im_rows, axis=0).reshape(kb, n2 * c)


def _spec_conv(a4, kf, kfnyq, order, tabs):
    bsz, _, n1h, nc = a4.shape
    tn2 = kf.shape[2]
    n2 = tn2 // 2
    c = nc // n2
    kb = 16
    return pl.pallas_call(
        functools.partial(_spec_body, kb=kb, n2=n2),
        grid=(bsz, n1h // kb),
        in_specs=[pl.BlockSpec((1, 1, kb, nc), lambda b, j: (b, 0, j, 0)),
                  pl.BlockSpec((1, 1, kb, nc), lambda b, j: (b, 1, j, 0)),
                  pl.BlockSpec((1, kb, tn2, c), lambda b, j: (0, j, 0, order)),
                  pl.BlockSpec((1, tn2, c), lambda b, j: (0, 0, order)),
                  pl.BlockSpec((kb, tn2, tn2), lambda b, j: (j, 0, 0)),
                  pl.BlockSpec((tn2, tn2), lambda b, j: (0, 0)),
                  pl.BlockSpec((kb, tn2, tn2), lambda b, j: (j, 0, 0)),
                  pl.BlockSpec((tn2, tn2), lambda b, j: (0, 0))],
        out_specs=pl.BlockSpec((1, 2, kb, nc), lambda b, j: (b, 0, j, 0)),
        out_shape=jax.ShapeDtypeStruct((bsz, 2, n1h, nc), BF16),
        compiler_params=_params(("parallel", "arbitrary"), 48),
        name="spectral_conv",
    )(a4, a4, kf, kfnyq, tabs["gf"], tabs["gnyq"], tabs["ginv"], tabs["m2"])


def _ifft_a_body(b_ref, f_ref, z_ref, gate_ref, bias_ref, o_ref):
    y = jnp.dot(f_ref[...], b_ref[0], preferred_element_type=F32)
    res = gate_ref[0] * (y + z_ref[0] * bias_ref[...])
    o_ref[0] = res.reshape(o_ref.shape[1:])


def _ifft_a(b3, fi, z3, gate3, bias_t, time_major):
    bsz, n1, nc = b3.shape
    n1h = n1 // 2
    tn = bias_t.shape[1]
    c = D_HYENA
    if time_major:
        out_spec = pl.BlockSpec((1, n1h, tn // c, c), lambda b, j: (b, 0, j, 0))
        out_shape = jax.ShapeDtypeStruct((bsz, n1h, nc // c, c), F32)
    else:
        out_spec = pl.BlockSpec((1, n1h, tn), lambda b, j: (b, 0, j))
        out_shape = jax.ShapeDtypeStruct((bsz, n1h, nc), F32)
    return pl.pallas_call(
        _ifft_a_body,
        grid=(bsz, nc // tn),
        in_specs=[pl.BlockSpec((1, n1, tn), lambda b, j: (b, 0, j)),
                  pl.BlockSpec((n1h, n1), lambda b, j: (0, 0)),
                  pl.BlockSpec((1, n1h, tn), lambda b, j: (b, 0, j)),
                  pl.BlockSpec((1, n1h, tn), lambda b, j: (b, 0, j)),
                  pl.BlockSpec((1, tn), lambda b, j: (0, 0))],
        out_specs=out_spec,
        out_shape=out_shape,
        compiler_params=_params(("parallel", "parallel"), 48),
        name="ifft_stage_a",
    )(b3, fi, z3, gate3, bias_t)


def _hyena(uh, conv_w, conv_b, fw, bias):
    bsz, seq, _ = uh.shape
    c = D_HYENA
    n1, n2 = _fft_dims(seq)
    n1h = n1 // 2
    tabs = _dft_tables(n1, n2)
    v, x1, x2 = _sconv(uh, conv_w, conv_b, n2)

    kcirc, ksum = _filters(seq, n1, n2, fw)
    cols = HYENA_ORDER * c
    ka = _fft_a(kcirc[None], tabs["fa_full"])
    kf, kfnyq = _fft_b(ka, tabs, ksum)

    tn = min(8192, n2 * c)
    z = v
    for order, gate in enumerate((x1, x2)):
        za = _fft_a(z, tabs["fa_half"])
        zb = _spec_conv(za, kf, kfnyq, order, tabs)
        bias_t = jnp.tile(bias[order][None, :], (1, tn // c))
        z = _ifft_a(zb.reshape(bsz, n1, n2 * c), tabs["fi"], z, gate, bias_t,
                    time_major=order == HYENA_ORDER - 1)
    return z.reshape(bsz, seq, c)


def _qkv_body(cq_ref, ckvr_ref, cos_ref, sin_ref, wq_ref, wqr_ref, wk_ref, wv_ref, qa_ref, kva_ref, qg_ref,
              kg_ref, qt_ref, k_ref, vt_ref):
    cqn = _rms(cq_ref[0]) * qa_ref[...]
    qs = jnp.dot(cqn.astype(BF16), wq_ref[...], preferred_element_type=F32)
    ck = ckvr_ref[0]
    ckvn = (_rms(ck[:, :KV_LORA]) * kva_ref[...]).astype(BF16)
    ks = jnp.dot(ckvn, wk_ref[...], preferred_element_type=F32)
    vs = jnp.dot(ckvn, wv_ref[...], preferred_element_type=F32)
    krope = pltpu.roll(ck[:, KV_LORA:], QK_NOPE, axis=1)
    cos = cos_ref[...]
    sin = sin_ref[...]
    lane = lax.broadcasted_iota(I32, cos.shape, 1)
    half = QK_ROPE // 2
    first = (lane >= QK_NOPE) & (lane < QK_NOPE + half)
    second = (lane >= QK_NOPE + half) & (lane < QK_HEAD)

    def head(xh, xh_rot, gain_cos, gain_sin):
        rs = lax.rsqrt(jnp.sum(xh * xh, axis=-1, keepdims=True) * (1.0 / QK_HEAD) + EPS)
        return rs * (xh * gain_cos + xh_rot * gain_sin)

    q_scale = QK_HEAD ** -0.5 * math.log2(math.e)
    q_cos, q_sin = qg_ref[0:1, :] * cos * q_scale, qg_ref[1:2, :] * sin * q_scale
    k_cos, k_sin = kg_ref[0:1, :] * cos, kg_ref[1:2, :] * sin
    qs_rot = jnp.dot(cqn.astype(BF16), wqr_ref[...], preferred_element_type=F32)
    krope_rot = jnp.where(first, -pltpu.roll(krope, SLOT - half, axis=1),
                          jnp.where(second, pltpu.roll(krope, half, axis=1), 0.0))
    eye = (lax.broadcasted_iota(I32, (SLOT, SLOT), 0) == lax.broadcasted_iota(I32, (SLOT, SLOT), 1)).astype(BF16)
    transpose = lambda x: lax.dot_general(eye, x.astype(BF16), (((1,), (1,)), ((), ())),
                                          preferred_element_type=F32).astype(BF16)
    for h in range(N_HEADS):
        sl = slice(h * SLOT, (h + 1) * SLOT)
        qt_ref[0, h] = transpose(head(qs[:, sl], qs_rot[:, sl], q_cos, q_sin))
        k_ref[0, h] = head(ks[:, sl] + krope, krope_rot, k_cos, k_sin).astype(BF16)
    aug = lax.broadcasted_iota(I32, (V_AUG - V_HEAD, cos.shape[0]), 0)
    ones_row = jnp.where(aug == 0, 1.0, 0.0).astype(BF16)
    for hp in range(N_HEADS // 2):
        pair_t = transpose(vs[:, hp * 2 * V_HEAD:(hp + 1) * 2 * V_HEAD])
        vt_ref[0, hp] = jnp.concatenate([pair_t[:V_HEAD], ones_row, pair_t[V_HEAD:], ones_row], axis=0)


def _qkv(cq, ckvr, cos_t, sin_t, aw):
    bsz, seq, _ = cq.shape
    tm = min(512, seq)
    const = lambda shape: pl.BlockSpec(shape, lambda b, i: (0,) * len(shape))
    return pl.pallas_call(
        _qkv_body,
        grid=(bsz, seq // tm),
        in_specs=[pl.BlockSpec((1, tm, Q_LORA), lambda b, i: (b, i, 0)),
                  pl.BlockSpec((1, tm, D_IN_PAD - COL_Q), lambda b, i: (b, i, 0)),
                  pl.BlockSpec((tm, SLOT), lambda b, i: (i, 0)),
                  pl.BlockSpec((tm, SLOT), lambda b, i: (i, 0)),
                  const((Q_LORA, N_HEADS * SLOT)), const((Q_LORA, N_HEADS * SLOT)),
                  const((KV_LORA, N_HEADS * SLOT)), const((KV_LORA, D_ATTN)),
                  const((1, Q_LORA)), const((1, KV_LORA)), const((2, SLOT)), const((2, SLOT))],
        out_specs=[pl.BlockSpec((1, N_HEADS, SLOT, tm), lambda b, i: (b, 0, 0, i)),
                   pl.BlockSpec((1, N_HEADS, tm, SLOT), lambda b, i: (b, 0, i, 0)),
                   pl.BlockSpec((1, N_HEADS // 2, 2 * V_AUG, tm), lambda b, i: (b, 0, 0, i))],
        out_shape=[jax.ShapeDtypeStruct((bsz, N_HEADS, SLOT, seq), BF16),
                   jax.ShapeDtypeStruct((bsz, N_HEADS, seq, SLOT), BF16),
                   jax.ShapeDtypeStruct((bsz, N_HEADS // 2, 2 * V_AUG, seq), BF16)],
        compiler_params=_params(("parallel", "parallel"), 40),
        name="qkv_prep",
    )(cq, ckvr, cos_t, sin_t, aw["wq"], aw["wq_rot"], aw["wk"], aw["wv"], aw["qa"], aw["kva"], aw["qg"],
      aw["kg"])


def _flash_body(qt_ref, k_ref, vt_ref, o_ref, s0_ref, s1_ref, acc_ref, *, tk, nk, nqc, unroll):
    chains = [(hh, qc) for hh in range(2) for qc in range(nqc)]
    acc_ref[...] = jnp.zeros(acc_ref.shape, F32)

    def qk(t, s_ref, only=None):
        ks = pl.multiple_of(t * tk, tk)
        tile_max = []
        for c, (hh, qc) in enumerate(chains):
            if only is not None and c != only:
                continue
            k = k_ref[0, hh, pl.ds(ks, tk), :]
            s = jnp.dot(k, qt_ref[0, hh, :, qc * ATT_QC:(qc + 1) * ATT_QC], preferred_element_type=F32)
            s_ref[c] = s
            tile_max.append(jnp.max(s, axis=0, keepdims=True))
        return tile_max

    def softmax_pv(t, s_ref, tile_max, m, only=None):
        ks = pl.multiple_of(t * tk, tk)
        m_out = []
        for c, (hh, qc) in enumerate(chains):
            if only is not None and c != only:
                continue
            i = 0 if only is not None else c
            m_new = jnp.maximum(m[i], tile_max[i])
            a = jnp.exp2(m[i] - m_new)
            p = jnp.exp2(s_ref[c] - m_new)
            m_out.append(m_new)
            vt = vt_ref[0, 0, hh * V_AUG:(hh + 1) * V_AUG, pl.ds(ks, tk)]
            acc_ref[c] = acc_ref[c] * a + jnp.dot(vt, p.astype(BF16), preferred_element_type=F32)
        return m_out

    def group(u, carry):
        tile_max, m = carry
        for i in range(0, unroll, 2):
            t = unroll * u + i
            tm1, tm0 = [], []
            m = list(m)
            for c in range(len(chains)):
                tm1 += qk(t + 1, s1_ref, only=c)
                m[c] = softmax_pv(t, s0_ref, [tile_max[c]], [m[c]], only=c)[0]
            for c in range(len(chains)):
                tm0 += qk(jnp.minimum(t + 2, nk - 1), s0_ref, only=c)
                m[c] = softmax_pv(t + 1, s1_ref, [tm1[c]], [m[c]], only=c)[0]
            tile_max = tm0
        return tile_max, m

    init = (qk(0, s0_ref), [jnp.full((1, ATT_QC), -jnp.inf, F32)] * len(chains))
    lax.fori_loop(0, nk // unroll, group, init)

    def normalised(c):
        acc = acc_ref[c]
        return acc[:V_HEAD] * (1.0 / acc[V_HEAD:V_HEAD + 1])

    heads = [jnp.concatenate([normalised(hh * nqc + qc) for qc in range(nqc)], axis=1) for hh in range(2)]
    o_ref[0] = jnp.concatenate(heads, axis=0).T


def _flash(qt, k, vt):
    bsz, nh, _, seq = qt.shape
    tq = min(ATT_TQ, seq)
    tk = min(ATT_TK, seq)
    nqc = tq // ATT_QC
    nk = seq // tk
    unroll = min(ATT_UNROLL, nk)
    assert nk % unroll == 0 and unroll % 2 == 0
    return pl.pallas_call(
        functools.partial(_flash_body, tk=tk, nk=nk, nqc=nqc, unroll=unroll),
        grid=(bsz, nh // 2, seq // tq),
        in_specs=[pl.BlockSpec((1, 2, SLOT, tq), lambda b, hp, i: (b, hp, 0, i)),
                  pl.BlockSpec((1, 2, seq, SLOT), lambda b, hp, i: (b, hp, 0, 0)),
                  pl.BlockSpec((1, 1, 2 * V_AUG, seq), lambda b, hp, i: (b, hp, 0, 0))],
        out_specs=pl.BlockSpec((1, tq, 2 * V_HEAD), lambda b, hp, i: (b, i, hp)),
        out_shape=jax.ShapeDtypeStruct((bsz, seq, nh * V_HEAD), F32),
        scratch_shapes=[pltpu.VMEM((2 * nqc, tk, ATT_QC), F32), pltpu.VMEM((2 * nqc, tk, ATT_QC), F32),
                        pltpu.VMEM((2 * nqc, V_AUG, ATT_QC), F32)],
        compiler_params=_params(("parallel", "parallel", "parallel"), 48),
        name="flash_attn",
    )(qt, k, vt)


def _outproj_body(yh_ref, ya_ref, x_ref, og_ref, w_ref, mg_ref, wr_ref, xm_ref, xn_ref, aff_ref, affn_ref):
    og = og_ref[...]
    half = D_MODEL // 2
    y = jnp.concatenate([_rms(yh_ref[...]) * og[:, :half], _rms(ya_ref[...]) * og[:, half:]], axis=1)
    xm = x_ref[...] + jnp.dot(y.astype(BF16), w_ref[...], preferred_element_type=F32)
    xm_ref[...] = xm
    xn = _rms(xm) * mg_ref[...]
    xn_hi = xn.astype(BF16)
    xn_ref[...] = xn_hi
    xn_lo = (xn - xn_hi.astype(F32)).astype(BF16)
    logits = (jnp.dot(xn_hi, wr_ref[0], preferred_element_type=F32)
              + jnp.dot(xn_lo, wr_ref[0], preferred_element_type=F32)
              + jnp.dot(xn_hi, wr_ref[1], preferred_element_type=F32))
    lane = lax.broadcasted_iota(I32, logits.shape, 1)
    logits = jnp.where(lane < N_EXPERTS, logits, -jnp.inf)
    e = jnp.exp(logits - jnp.max(logits, axis=-1, keepdims=True))
    aff = e / jnp.sum(e, axis=-1, keepdims=True)
    aff_ref[...] = aff.T[:N_EXPERTS]
    affn_ref[...] = aff[:, :N_EXPERTS]


def _outproj(yh, ya, x2d, og, w_out, mg, wr_pad):
    n = x2d.shape[0]
    tm = min(512, n)
    half = D_MODEL // 2
    const = lambda shape: pl.BlockSpec(shape, lambda i: (0,) * len(shape))
    return pl.pallas_call(
        _outproj_body,
        grid=(n // tm,),
        in_specs=[pl.BlockSpec((tm, half), lambda i: (i, 0)),
                  pl.BlockSpec((tm, half), lambda i: (i, 0)),
                  pl.BlockSpec((tm, D_MODEL), lambda i: (i, 0)),
                  const((1, D_MODEL)), const((D_MODEL, D_MODEL)), const((1, D_MODEL)),
                  const((2, D_MODEL, LANES))],
        out_specs=[pl.BlockSpec((tm, D_MODEL), lambda i: (i, 0)),
                   pl.BlockSpec((tm, D_MODEL), lambda i: (i, 0)),
                   pl.BlockSpec((N_EXPERTS, tm), lambda i: (0, i)),
                   pl.BlockSpec((tm, N_EXPERTS), lambda i: (i, 0))],
        out_shape=[jax.ShapeDtypeStruct((n, D_MODEL), F32),
                   jax.ShapeDtypeStruct((n, D_MODEL), BF16),
                   jax.ShapeDtypeStruct((N_EXPERTS, n), F32),
                   jax.ShapeDtypeStruct((n, N_EXPERTS), F32)],
        compiler_params=_params(("parallel",), 40),
        name="outproj_router",
    )(yh, ya, x2d, og, w_out, mg, wr_pad)


def _select_body(aff_ref, upper_ref, lower_ref, pos_ref, off_ref, *, cap):
    group = aff_ref.shape[0]
    bits = [pltpu.bitcast(aff_ref[x], I32) for x in range(group)]
    upper = upper_ref[...]
    lower = lower_ref[...]

    def count(mask):
        return jnp.sum(jnp.sum(mask.astype(F32), axis=1, keepdims=True), axis=0, keepdims=True)

    def bit_step(i, thr):
        bit = jnp.left_shift(jnp.int32(1), 30 - i)
        return tuple(jnp.where(count(bits[x] >= (thr[x] | bit)) >= cap, thr[x] | bit, thr[x])
                     for x in range(group))

    thr = lax.fori_loop(0, 31, bit_step, tuple(jnp.zeros((1, 1), I32) for _ in range(group)))

    def prefix(mask):
        within = jnp.dot(mask.astype(BF16), upper, preferred_element_type=F32)
        total = within[:, LANES - 1:LANES]
        offs = jnp.dot(lower, jnp.broadcast_to(total, within.shape).astype(BF16), preferred_element_type=F32)
        return within, offs

    for x in range(group):
        gt = bits[x] > thr[x]
        eq = bits[x] == thr[x]
        need = cap - count(gt)
        w_eq, o_eq = prefix(eq)
        sel = gt | (eq & (o_eq + w_eq <= need))
        w_sel, o_sel = prefix(sel)
        pos_ref[x] = jnp.where(sel, o_sel + w_sel - 1.0, -1.0).astype(I32)
        off_ref[x] = o_sel.astype(I32)


def _select(aff_t, cap):
    ne, n = aff_t.shape
    rows = n // LANES
    upper = (jnp.arange(LANES)[:, None] <= jnp.arange(LANES)[None, :]).astype(BF16)
    lower = (jnp.arange(rows)[None, :] < jnp.arange(rows)[:, None]).astype(BF16)
    group = 4
    blk = pl.BlockSpec((group, rows, LANES), lambda e: (e, 0, 0))
    pos, off = pl.pallas_call(
        functools.partial(_select_body, cap=cap),
        grid=(ne // group,),
        in_specs=[blk, pl.BlockSpec((LANES, LANES), lambda e: (0, 0)),
                  pl.BlockSpec((rows, rows), lambda e: (0, 0))],
        out_specs=[blk, blk],
        out_shape=[jax.ShapeDtypeStruct((ne, rows, LANES), I32),
                   jax.ShapeDtypeStruct((ne, rows, LANES), I32)],
        compiler_params=_params(("parallel",), 32),
        name="ec_select",
    )(aff_t.reshape(ne, rows, LANES), upper, lower)
    return pos.reshape(ne, n), off[:, :, 0]


def _ffn_body(off_ref, nch_ref, pos_ref, x_ref, wg_ref, wu_ref, wd_ref, y_ref, xs_ref, *, nb, sub, cap):
    group = y_ref.shape[0]
    e0 = pl.program_id(0) * group
    sb = pl.program_id(1)

    @pl.when(sb == 0)
    def _():
        xs_ref[...] = jnp.zeros(xs_ref.shape, BF16)

    row = lax.broadcasted_iota(I32, (GATHER_ROWS, TOK_BLOCK), 0)

    def first_chunks(i):
        tok = slice(i * TOK_BLOCK, (i + 1) * TOK_BLOCK)
        starts = [pl.multiple_of(off_ref[(e0 + x) * nb + sb * sub + i] * ALIGN, ALIGN) for x in range(group)]
        hits = [(pos_ref[x, :, tok] - starts[x]) == row for x in range(group)]
        win = jnp.dot(jnp.concatenate(hits, axis=0).astype(BF16), x_ref[tok, :], preferred_element_type=F32)
        for x in range(group):
            xs_ref[x, pl.ds(starts[x], GATHER_ROWS), :] += win[x * GATHER_ROWS:(x + 1) * GATHER_ROWS].astype(BF16)

    def later_chunk(x, i, c):
        tok = slice(i * TOK_BLOCK, (i + 1) * TOK_BLOCK)
        start = pl.multiple_of(off_ref[(e0 + x) * nb + sb * sub + i] * ALIGN + c * GATHER_ROWS, ALIGN)
        hit = (pos_ref[x, :, tok] - start) == row
        win = jnp.dot(hit.astype(BF16), x_ref[tok, :], preferred_element_type=F32)
        xs_ref[x, pl.ds(start, GATHER_ROWS), :] += win.astype(BF16)

    for i in range(sub):
        first_chunks(i)
    for x in range(group):
        for i in range(sub):
            lax.fori_loop(1, nch_ref[(e0 + x) * nb + sb * sub + i],
                          lambda c, carry, x=x, i=i: (later_chunk(x, i, c), carry)[1], 0)

    @pl.when(sb == pl.num_programs(1) - 1)
    def _():
        for x in range(group):
            for j in range(cap // SLOT_TILE):
                rows = slice(j * SLOT_TILE, (j + 1) * SLOT_TILE)
                xt = xs_ref[x, rows, :]
                g = jnp.dot(xt, wg_ref[x], preferred_element_type=F32)
                u = jnp.dot(xt, wu_ref[x], preferred_element_type=F32)
                h = (g * jax.nn.sigmoid(g) * u).astype(BF16)
                y_ref[x, rows, :] = jnp.dot(h, wd_ref[x], preferred_element_type=F32).astype(BF16)
            y_ref[x, cap:, :] = jnp.zeros((y_ref.shape[1] - cap, D_MODEL), BF16)


def _ffn(base, nchunk, pos3, xn, wg, wu, wd, cap):
    ne = wg.shape[0]
    n = xn.shape[0]
    nb = n // TOK_BLOCK
    sub = min(4, nb)
    group = 2
    cap_pad = cap + TOK_BLOCK + ALIGN
    grid_spec = pltpu.PrefetchScalarGridSpec(
        num_scalar_prefetch=2,
        grid=(ne // group, nb // sub),
        in_specs=[pl.BlockSpec((group, 1, sub * TOK_BLOCK), lambda e, s, off, nch: (e, 0, s)),
                  pl.BlockSpec((sub * TOK_BLOCK, D_MODEL), lambda e, s, off, nch: (s, 0)),
                  pl.BlockSpec((group, D_MODEL, D_EXPERT), lambda e, s, off, nch: (e, 0, 0)),
                  pl.BlockSpec((group, D_MODEL, D_EXPERT), lambda e, s, off, nch: (e, 0, 0)),
                  pl.BlockSpec((group, D_EXPERT, D_MODEL), lambda e, s, off, nch: (e, 0, 0))],
        out_specs=pl.BlockSpec((group, cap_pad, D_MODEL), lambda e, s, off, nch: (e, 0, 0)),
        scratch_shapes=[pltpu.VMEM((group, cap_pad, D_MODEL), BF16)])
    return pl.pallas_call(
        functools.partial(_ffn_body, nb=nb, sub=sub, cap=cap),
        grid_spec=grid_spec,
        out_shape=jax.ShapeDtypeStruct((ne, cap_pad, D_MODEL), BF16),
        compiler_params=_params(("parallel", "arbitrary"), 52),
        name="ec_ffn",
    )(base, nchunk, pos3, xn, wg, wu, wd)


def _combine_body(off_ref, spill_ref, post_ref, affn_ref, xm_ref, *refs, nb, ne, win_rows):
    win_refs, tail_refs, o_ref = refs[:ne], refs[ne:-1], refs[-1]
    b = pl.program_id(0)
    post = post_ref[...]
    affn = affn_ref[...]

    def expand(y_refs, shift):
        width = y_refs[0].shape[0]
        lane = lax.broadcasted_iota(I32, (TOK_BLOCK, width), 1)
        total = jnp.zeros((TOK_BLOCK, D_MODEL), F32)
        for e in range(ne):
            rel = post[:, e:e + 1] - (off_ref[e * nb + b] * ALIGN + shift)
            hit = (rel == lane).astype(BF16)
            total = total + affn[:, e:e + 1] * jnp.dot(hit, y_refs[e][...], preferred_element_type=F32)
        return total

    def expand_pairs(y_refs):
        width = y_refs[0].shape[0]
        lane = lax.broadcasted_iota(I32, (TOK_BLOCK, 2 * width), 1)
        total = jnp.zeros((TOK_BLOCK, D_MODEL), F32)
        for e in range(0, ne, 2):
            rel0 = post[:, e:e + 1] - off_ref[e * nb + b] * ALIGN
            rel1 = post[:, e + 1:e + 2] - off_ref[(e + 1) * nb + b] * ALIGN + width
            gates = (jnp.where(rel0 == lane, affn[:, e:e + 1], 0.0)
                     + jnp.where((rel1 == lane) & (rel1 >= width), affn[:, e + 1:e + 2], 0.0))
            rows = jnp.concatenate([y_refs[e][...], y_refs[e + 1][...]], axis=0)
            total = total + jnp.dot(gates.astype(BF16), rows, preferred_element_type=F32)
        return total

    if not tail_refs and 2 * win_rows <= TOK_BLOCK and ne % 2 == 0:
        o_ref[...] = xm_ref[...] + expand_pairs(win_refs)
    else:
        o_ref[...] = xm_ref[...] + expand(win_refs, 0)

    if tail_refs:
        @pl.when(spill_ref[b] != 0)
        def _():
            o_ref[...] += expand(tail_refs, win_rows)


def _combine(base, spill, pos_t, aff_n, xm, y, win_rows):
    ne = y.shape[0]
    n = xm.shape[0]
    nb = n // TOK_BLOCK
    tail_rows = ALIGN if win_rows == TOK_BLOCK else 0

    def window(e, rows, shift):
        return pl.BlockSpec((pl.Squeezed(), pl.Element(rows), pl.Element(D_MODEL)),
                            lambda b, off, sp: (e, (off[e * nb + b] + shift // ALIGN) * ALIGN, 0))

    tails = [window(e, tail_rows, win_rows) for e in range(ne)] if tail_rows else []
    grid_spec = pltpu.PrefetchScalarGridSpec(
        num_scalar_prefetch=2,
        grid=(nb,),
        in_specs=([pl.BlockSpec((TOK_BLOCK, ne), lambda b, off, sp: (b, 0)),
                   pl.BlockSpec((TOK_BLOCK, ne), lambda b, off, sp: (b, 0)),
                   pl.BlockSpec((TOK_BLOCK, D_MODEL), lambda b, off, sp: (b, 0))]
                  + [window(e, win_rows, 0) for e in range(ne)] + tails),
        out_specs=pl.BlockSpec((TOK_BLOCK, D_MODEL), lambda b, off, sp: (b, 0)))
    return pl.pallas_call(
        functools.partial(_combine_body, nb=nb, ne=ne, win_rows=win_rows),
        grid_spec=grid_spec,
        out_shape=jax.ShapeDtypeStruct((n, D_MODEL), F32),
        compiler_params=_params(("parallel",), 48),
        name="ec_combine",
    )(base, spill, pos_t, aff_n, xm, *([y] * (ne + len(tails))))


def _ec_moe(xm, xn, aff_t, aff_n, wg, wu, wd):
    ne, n = aff_t.shape
    cap = max(1, EC_CAPACITY * n // N_EXPERTS)
    nb = n // TOK_BLOCK
    pos, off = _select(aff_t, cap)
    first = off[:, ::TOK_BLOCK // LANES]
    count = jnp.concatenate([first[:, 1:], jnp.full((ne, 1), cap, I32)], axis=1) - first
    base = first // ALIGN
    span = jnp.where(count > 0, first - base * ALIGN + count, 0)
    spill = jnp.any(span > TOK_BLOCK, axis=0).astype(I32)
    nchunk = ((span + GATHER_ROWS - 1) // GATHER_ROWS).reshape(ne * nb).astype(I32)
    base = base.reshape(ne * nb).astype(I32)
    y = _ffn(base, nchunk, pos.reshape(ne, 1, n), xn, wg, wu, wd, cap)
    pos_t = pos.T
    return lax.cond(jnp.max(span) <= COMBINE_FAST_ROWS,
                    lambda: _combine(base, spill, pos_t, aff_n, xm, y, COMBINE_FAST_ROWS),
                    lambda: _combine(base, spill, pos_t, aff_n, xm, y, TOK_BLOCK))


def _slot_cols(w, head_width):
    k = w.shape[0]
    w3 = w.reshape(k, N_HEADS, head_width)
    return jnp.pad(w3, ((0, 0), (0, 0), (0, SLOT - head_width))).reshape(k, N_HEADS * SLOT)


def _attn_weights(q_a_norm, w_uq, kv_a_norm, w_ukv, q_norm, k_norm):
    wkv = w_ukv.reshape(KV_LORA, N_HEADS, QK_NOPE + V_HEAD)
    wk = _slot_cols(wkv[:, :, :QK_NOPE].reshape(KV_LORA, -1), QK_NOPE)
    wv = wkv[:, :, QK_NOPE:].reshape(KV_LORA, D_ATTN)
    half = QK_ROPE // 2

    def rotate_half(a):
        return jnp.concatenate([jnp.zeros_like(a[..., :QK_NOPE]), -a[..., QK_NOPE + half:],
                                a[..., QK_NOPE:QK_NOPE + half]], axis=-1)

    def gains(g):
        swapped = jnp.concatenate([jnp.zeros_like(g[:QK_NOPE]), g[QK_NOPE + half:], g[QK_NOPE:QK_NOPE + half]])
        return jnp.pad(jnp.stack([g, swapped]), ((0, 0), (0, SLOT - QK_HEAD)))

    wq3 = w_uq.reshape(Q_LORA, N_HEADS, QK_HEAD)
    return dict(wq=_slot_cols(w_uq, QK_HEAD).astype(BF16),
                wq_rot=_slot_cols(rotate_half(wq3).reshape(Q_LORA, -1), QK_HEAD).astype(BF16),
                wk=wk.astype(BF16), wv=wv.astype(BF16), qa=q_a_norm[None, :], kva=kv_a_norm[None, :],
                qg=gains(q_norm), kg=gains(k_norm))


def _rope_tables(seq):
    pos = jnp.arange(seq, dtype=F32)
    inv_freq = ROPE_THETA ** (-jnp.arange(0, QK_ROPE, 2, dtype=F32) / QK_ROPE)
    ang = pos[:, None] * inv_freq
    ang = jnp.concatenate([ang, ang], axis=-1)
    pad = lambda t, fill: jnp.concatenate(
        [jnp.full((seq, QK_NOPE), fill, F32), t, jnp.full((seq, SLOT - QK_HEAD), fill, F32)], axis=1)
    return pad(jnp.cos(ang), 1.0), pad(jnp.sin(ang), 0.0)


def _hi_lo(w):
    hi = w.astype(BF16)
    return jnp.stack([hi, (w - hi.astype(F32)).astype(BF16)], axis=1)


def _filter_weights(w1, b1, w2, b2, w3, freq, decay):
    hid, nbands = FILTER_HIDDEN, FILTER_BANDS
    assert 2 * hid == LANES and 2 * nbands <= hid
    both = lambda row: jnp.concatenate([row, row])[None, :]
    blockdiag = lambda a: jnp.concatenate(
        [jnp.concatenate([a, jnp.zeros_like(a)], axis=1), jnp.concatenate([jnp.zeros_like(a), a], axis=1)], axis=0)
    bands = jnp.linspace(1e-4, nbands - 1, nbands, dtype=F32)
    rest = jnp.zeros((hid - 2 * nbands,), F32)
    band_half = jnp.concatenate([bands, bands, rest])
    phase_half = jnp.concatenate([jnp.zeros((nbands,), F32), jnp.full((nbands,), 0.5 * math.pi, F32), rest])
    w1_half = jnp.concatenate([-w1[1 + nbands:], w1[1:1 + nbands], jnp.zeros((hid - 2 * nbands, hid), F32)], axis=0)
    cols = HYENA_ORDER * D_HYENA
    w3d = jnp.transpose(w3.reshape(hid, N_DIR, cols), (1, 0, 2))
    w3p = jnp.stack([jnp.pad(w3d[0], ((0, hid), (0, 0))), jnp.pad(w3d[1], ((hid, 0), (0, 0)))], axis=0)
    w3_hi = w3p.astype(BF16)
    w3_lo = (w3p - w3_hi.astype(F32)).astype(BF16)
    return dict(bands=both(band_half), phase=both(phase_half), w1t=both(w1[0]), w1=blockdiag(w1_half),
                b1=both(b1), w2=blockdiag(w2), b2=both(b2), freq=both(freq), w3_hi=w3_hi, w3_lo=w3_lo,
                decay=decay.reshape(N_DIR, 1, cols))


def _trunk(x, p):
    bsz, seq, _ = x.shape
    n = bsz * seq
    cos_t, sin_t = _rope_tables(seq)
    x2 = x.reshape(n, D_MODEL)
    for l in range(DEPTH):
        uh, cq, ckvr = _inproj(x2, p["attn_norm"][l][None, :], p["w_in"][l])
        y_h = _hyena(uh.reshape(bsz, seq, COL_HYENA), p["conv_w"][l], p["conv_b"][l][None, :],
                     p["filt"][l], p["hyena_bias"][l])
        qt, k, vt = _qkv(cq.reshape(bsz, seq, -1), ckvr.reshape(bsz, seq, -1), cos_t, sin_t, p["attn"][l])
        y_a = _flash(qt, k, vt)
        xm, xn, aff_t, aff_n = _outproj(y_h.reshape(n, D_HYENA), y_a.reshape(n, D_ATTN), x2,
                                 p["out_norm"][l][None, :], p["w_out"][l], p["mlp_norm"][l][None, :],
                                 p["w_router"][l])
        x2 = _ec_moe(xm, xn, aff_t, aff_n, p["w_gate"][l], p["w_up"][l], p["w_down"][l])
    return x2.reshape(bsz, seq, D_MODEL)


def kernel(x_prompt, x_sample, attn_norm, w_in, conv_w, conv_b, filt_w1, filt_b1, filt_w2, filt_b2, filt_w3,
           filt_freq, filt_decay, hyena_bias, q_a_norm, w_uq, kv_a_norm, w_ukv, q_norm, k_norm, out_norm,
           w_out, mlp_norm, w_router, w_gate, w_up, w_down):
    p = dict(
        attn_norm=attn_norm,
        w_in=jnp.pad(w_in, ((0, 0), (0, 0), (0, D_IN_PAD - D_IN))).astype(BF16),
        conv_w=conv_w, conv_b=conv_b, hyena_bias=hyena_bias,
        filt=[_filter_weights(filt_w1[l], filt_b1[l], filt_w2[l], filt_b2[l], filt_w3[l], filt_freq[l],
                              filt_decay[l]) for l in range(DEPTH)],
        attn=[_attn_weights(q_a_norm[l], w_uq[l], kv_a_norm[l], w_ukv[l], q_norm[l], k_norm[l])
              for l in range(DEPTH)],
        out_norm=out_norm, w_out=w_out.astype(BF16), mlp_norm=mlp_norm,
        w_router=_hi_lo(jnp.pad(w_router, ((0, 0), (0, 0), (0, LANES - N_EXPERTS)))),
        w_gate=w_gate.astype(BF16), w_up=w_up.astype(BF16), w_down=w_down.astype(BF16))
    return (_trunk(x_prompt, p), _trunk(x_sample, p))
```

```python
import functools
import math

import jax
import jax.numpy as jnp
from jax import lax
from jax.experimental import pallas as pl
from jax.experimental.pallas import tpu as pltpu

F32 = jnp.float32
BF16 = jnp.bfloat16
I32 = jnp.int32
HIGHEST = lax.Precision.HIGHEST

D_MODEL = 1024
DEPTH = 2
D_HYENA = 512
HYENA_ORDER = 2
FILTER_BANDS = 16
FILTER_HIDDEN = 64
N_DIR = 2
DECAY_SHIFT = 0.05
N_HEADS = 8
QK_NOPE = 64
QK_ROPE = 32
QK_HEAD = QK_NOPE + QK_ROPE
V_HEAD = 64
V_AUG = V_HEAD + 16
D_ATTN = N_HEADS * V_HEAD
Q_LORA = 256
KV_LORA = 128
ROPE_THETA = 10000.0
N_EXPERTS = 16
EC_CAPACITY = 2
D_EXPERT = 512
EPS = 1e-6
COL_HYENA = (HYENA_ORDER + 1) * D_HYENA
COL_Q = COL_HYENA + Q_LORA
COL_KV = COL_Q + KV_LORA
D_IN = COL_KV + QK_ROPE
D_IN_PAD = 2048

LANES = 128
SLOT = 128
TOK_BLOCK = 256
SLOT_TILE = 256
ALIGN = 16
GATHER_ROWS = 64
COMBINE_FAST_ROWS = 128
OUTPROJ_CHUNKS = 2
ATT_TQ = 512
ATT_QC = 256
ATT_TK = 256
ATT_UNROLL = 16
MIB = 1024 * 1024


def _params(sem, vmem_mib):
    return pltpu.CompilerParams(dimension_semantics=sem, vmem_limit_bytes=vmem_mib * MIB)


def _rms(x):
    return x * lax.rsqrt(jnp.mean(x * x, axis=-1, keepdims=True) + EPS)


def _inproj_body(x_ref, g_ref, w_ref, uh_ref, cq_ref, ckvr_ref):
    xn = _rms(x_ref[...]) * g_ref[...]
    u = jnp.dot(xn.astype(BF16), w_ref[...], preferred_element_type=F32)
    uh_ref[...] = u[:, :COL_HYENA]
    cq_ref[...] = u[:, COL_HYENA:COL_Q]
    ckvr_ref[...] = u[:, COL_Q:]


def _inproj(x2d, g, w_pad):
    n = x2d.shape[0]
    tm = min(512, n)
    return pl.pallas_call(
        _inproj_body,
        grid=(n // tm,),
        in_specs=[pl.BlockSpec((tm, D_MODEL), lambda i: (i, 0)),
                  pl.BlockSpec((1, D_MODEL), lambda i: (0, 0)),
                  pl.BlockSpec((D_MODEL, D_IN_PAD), lambda i: (0, 0))],
        out_specs=[pl.BlockSpec((tm, COL_HYENA), lambda i: (i, 0)),
                   pl.BlockSpec((tm, Q_LORA), lambda i: (i, 0)),
                   pl.BlockSpec((tm, D_IN_PAD - COL_Q), lambda i: (i, 0))],
        out_shape=[jax.ShapeDtypeStruct((n, COL_HYENA), F32),
                   jax.ShapeDtypeStruct((n, Q_LORA), F32),
                   jax.ShapeDtypeStruct((n, D_IN_PAD - COL_Q), F32)],
        compiler_params=_params(("parallel",), 40),
        name="inproj",
    )(x2d, g, w_pad)


def _sconv_body(u_ref, prev_ref, next_ref, w_ref, b_ref, v_ref, x1_ref, x2_ref):
    i = pl.program_id(1)
    last = pl.num_programs(1) - 1
    u = u_ref[0]
    tl = u.shape[0]
    prev_row = jnp.where(i == 0, 0.0, prev_ref[0, 7:8, :])
    next_row = jnp.where(i == last, 0.0, next_ref[0, 0:1, :])
    row = lax.broadcasted_iota(I32, u.shape, 0)
    up = jnp.where(row == 0, prev_row, pltpu.roll(u, 1, axis=0))
    dn = jnp.where(row == tl - 1, next_row, pltpu.roll(u, tl - 1, axis=0))
    y = up * w_ref[0:1, :] + u * w_ref[1:2, :] + dn * w_ref[2:3, :] + b_ref[...]
    rows, flat = v_ref.shape[1:]
    for k, o_ref in enumerate((v_ref, x1_ref, x2_ref)):
        o_ref[0] = y[:, k * D_HYENA:(k + 1) * D_HYENA].reshape(rows, tl // rows, D_HYENA).reshape(rows, flat)


def _sconv(uh, w, b, n2):
    bsz, seq, c = uh.shape
    rows = 8
    tl = rows * n2
    r = tl // 8
    nblk8 = seq // 8
    out = jax.ShapeDtypeStruct((bsz, seq // n2, n2 * D_HYENA), F32)
    ospec = pl.BlockSpec((1, rows, n2 * D_HYENA), lambda bi, i: (bi, i, 0))
    return pl.pallas_call(
        _sconv_body,
        grid=(bsz, seq // tl),
        in_specs=[pl.BlockSpec((1, tl, c), lambda bi, i: (bi, i, 0)),
                  pl.BlockSpec((1, 8, c), lambda bi, i: (bi, jnp.maximum(i * r - 1, 0), 0)),
                  pl.BlockSpec((1, 8, c), lambda bi, i: (bi, jnp.minimum((i + 1) * r, nblk8 - 1), 0)),
                  pl.BlockSpec((3, c), lambda bi, i: (0, 0)),
                  pl.BlockSpec((1, c), lambda bi, i: (0, 0))],
        out_specs=[ospec, ospec, ospec],
        out_shape=[out, out, out],
        compiler_params=_params(("parallel", "parallel"), 40),
        name="sconv",
    )(uh, uh, uh, w, b)


def _filter_body(bands_ref, phase_ref, w1t_ref, w1_ref, b1_ref, w2_ref, b2_ref, fr_ref, w3h_ref, w3l_ref,
                 dec_ref, k_ref, sum_ref, *, seq, n2):
    i = pl.program_id(0)
    hp = k_ref.shape[0] // 2
    cols = sum_ref.shape[1]
    jb = k_ref.shape[1] // cols
    row = lax.broadcasted_iota(I32, (jb * hp, 1), 0)
    n_lo = (row % hp) * n2 + i * jb + row // hp
    n_hi = n_lo + seq
    t_of = lambda n: jnp.where(n < seq, n, 2 * seq - n).astype(F32)
    lane = lax.broadcasted_iota(I32, (jb * hp, LANES), 1)
    t_idx = jnp.where(lane < FILTER_HIDDEN, t_of(n_lo), t_of(n_hi))
    feats = jnp.sin(2.0 * math.pi * bands_ref[...] * t_idx / seq + phase_ref[...])
    fr = fr_ref[...]
    pre = (t_idx / (seq - 1)) * w1t_ref[...] + jnp.dot(feats, w1_ref[...], precision=HIGHEST,
                                                      preferred_element_type=F32)
    h = jnp.sin(fr * (pre + b1_ref[...]))
    h = jnp.sin(fr * (jnp.dot(h, w2_ref[...], precision=HIGHEST, preferred_element_type=F32) + b2_ref[...]))
    h_hi = h.astype(BF16)
    h_lo = (h - h_hi.astype(F32)).astype(BF16)

    @pl.when(i == 0)
    def _():
        sum_ref[...] = jnp.zeros_like(sum_ref)

    for j, n in enumerate((n_lo, n_hi)):
        out = (jnp.dot(h_hi, w3h_ref[j], preferred_element_type=F32)
               + jnp.dot(h_lo, w3h_ref[j], preferred_element_type=F32)
               + jnp.dot(h_hi, w3l_ref[j], preferred_element_type=F32))
        window = jnp.exp(-(t_of(n) / (seq - 1)) * dec_ref[j]) + DECAY_SHIFT
        k = jnp.where(n == seq, 0.0, out * window)
        sum_ref[...] += jnp.sum(jnp.abs(k), axis=0, keepdims=True)
        for jj in range(jb):
            k_ref[j * hp:(j + 1) * hp, jj * cols:(jj + 1) * cols] = k[jj * hp:(jj + 1) * hp].astype(BF16)


def _filters(seq, n1, n2, fw):
    cols = HYENA_ORDER * D_HYENA
    const = lambda shape: pl.BlockSpec(shape, lambda i: (0,) * len(shape))
    jb = max(1, 1024 // n1)
    return pl.pallas_call(
        functools.partial(_filter_body, seq=seq, n2=n2),
        grid=(n2 // jb,),
        in_specs=[const((1, LANES)), const((1, LANES)), const((1, LANES)), const((LANES, LANES)),
                  const((1, LANES)), const((LANES, LANES)), const((1, LANES)), const((1, LANES)),
                  const((N_DIR, LANES, cols)), const((N_DIR, LANES, cols)), const((N_DIR, 1, cols))],
        out_specs=[pl.BlockSpec((n1, jb * cols), lambda i: (0, i)),
                   pl.BlockSpec((1, cols), lambda i: (0, 0))],
        out_shape=[jax.ShapeDtypeStruct((n1, n2 * cols), BF16),
                   jax.ShapeDtypeStruct((1, cols), F32)],
        compiler_params=_params(("arbitrary",), 32),
        name="filter_gen",
    )(fw["bands"], fw["phase"], fw["w1t"], fw["w1"], fw["b1"], fw["w2"], fw["b2"], fw["freq"],
      fw["w3_hi"], fw["w3_lo"], fw["decay"])


def _fft_dims(seq):
    n2 = 128 if 2 * seq >= 32768 else 64
    n1 = 2 * seq // n2
    return n1, n2


def _dft_tables(n1, n2):
    n = n1 * n2
    n1h = n1 // 2
    k1 = jnp.arange(n1h, dtype=I32)[:, None]
    m1 = jnp.arange(n1, dtype=I32)[None, :]
    ang = (2.0 * math.pi / n1) * ((k1 * m1) % n1).astype(F32)
    top = jnp.cos(ang)
    bot = -jnp.sin(ang)
    nyq = jnp.where(m1 % 2 == 0, 1.0, -1.0).astype(F32)
    bot = jnp.concatenate([nyq, bot[1:]], axis=0)
    fa = jnp.concatenate([top, bot], axis=0)
    weight = jnp.where((jnp.arange(n1) % n1h) == 0, 1.0, 2.0).astype(F32) / n
    fi = (fa[:, :n1h] * weight[:, None]).T

    kk = jnp.arange(n1h + 1, dtype=I32)[:, None, None]
    k2 = jnp.arange(n2, dtype=I32)[None, :, None]
    m2 = jnp.arange(n2, dtype=I32)[None, None, :]
    phi = (2.0 * math.pi / n) * ((m2 * (kk + n1 * k2)) % n).astype(F32)
    gr = jnp.cos(phi)
    gi = -jnp.sin(phi)
    blk = jnp.concatenate([jnp.concatenate([gr, -gi], axis=2),
                           jnp.concatenate([gi, gr], axis=2)], axis=1)
    left = (jnp.arange(2 * n2) < n2)[None, :]
    g0 = jnp.where(left, blk[0], 0.0)
    gf = jnp.concatenate([g0[None], blk[1:n1h]], axis=0)
    gnyq = jnp.concatenate([jnp.zeros((2 * n2, n2), F32), blk[n1h][:, :n2]], axis=1)
    return dict(fa_full=fa.astype(BF16), fa_half=fa[:, :n1h].astype(BF16), fi=fi.astype(BF16),
                gf=gf.astype(BF16), gnyq=gnyq.astype(BF16),
                ginv=jnp.transpose(gf, (0, 2, 1)).astype(BF16), m2=gnyq.T.astype(BF16))


def _fft_a_body(x_ref, f_ref, o_ref):
    a = jnp.dot(f_ref[...], x_ref[0].astype(BF16), preferred_element_type=F32)
    n1h = o_ref.shape[2]
    o_ref[0, 0] = a[:n1h].astype(BF16)
    o_ref[0, 1] = a[n1h:].astype(BF16)


def _fft_a(x3, fmat):
    bsz, r, nc = x3.shape
    n1 = fmat.shape[0]
    tn = min(8192, nc)
    return pl.pallas_call(
        _fft_a_body,
        grid=(bsz, nc // tn),
        in_specs=[pl.BlockSpec((1, r, tn), lambda b, j: (b, 0, j)),
                  pl.BlockSpec((n1, r), lambda b, j: (0, 0))],
        out_specs=pl.BlockSpec((1, 2, n1 // 2, tn), lambda b, j: (b, 0, 0, j)),
        out_shape=jax.ShapeDtypeStruct((bsz, 2, n1 // 2, nc), BF16),
        compiler_params=_params(("parallel", "parallel"), 48),
        name="fft_stage_a",
    )(x3, fmat)


def _fft_b_body(ar_ref, ai_ref, g_ref, gn_ref, sc_ref, x_ref, xn_ref, *, kb, n2):
    j = pl.program_id(1)
    c = sc_ref.shape[1]
    inv = 1.0 / sc_ref[...]
    a_re = ar_ref[0, 0].reshape(kb, n2, c)
    a_im = ai_ref[0, 0].reshape(kb, n2, c)
    for kk in range(kb):
        rhs = jnp.concatenate([a_re[kk], a_im[kk]], axis=0)
        x_ref[0, kk] = (jnp.dot(g_ref[kk], rhs, preferred_element_type=F32) * inv).astype(BF16)

    @pl.when(j == 0)
    def _():
        rhs = jnp.concatenate([a_re[0], a_im[0]], axis=0)
        xn_ref[0] = (jnp.dot(gn_ref[...], rhs, preferred_element_type=F32) * inv).astype(BF16)


def _fft_b(a4, tabs, scale):
    bsz, _, n1h, nc = a4.shape
    c = scale.shape[1]
    n2 = nc // c
    kb = 16
    return pl.pallas_call(
        functools.partial(_fft_b_body, kb=kb, n2=n2),
        grid=(bsz, n1h // kb),
        in_specs=[pl.BlockSpec((1, 1, kb, nc), lambda b, j: (b, 0, j, 0)),
                  pl.BlockSpec((1, 1, kb, nc), lambda b, j: (b, 1, j, 0)),
                  pl.BlockSpec((kb, 2 * n2, 2 * n2), lambda b, j: (j, 0, 0)),
                  pl.BlockSpec((2 * n2, 2 * n2), lambda b, j: (0, 0)),
                  pl.BlockSpec((1, c), lambda b, j: (0, 0))],
        out_specs=[pl.BlockSpec((1, kb, 2 * n2, c), lambda b, j: (b, j, 0, 0)),
                   pl.BlockSpec((1, 2 * n2, c), lambda b, j: (b, 0, 0))],
        out_shape=[jax.ShapeDtypeStruct((bsz, n1h, 2 * n2, c), BF16),
                   jax.ShapeDtypeStruct((bsz, 2 * n2, c), BF16)],
        compiler_params=_params(("parallel", "arbitrary"), 48),
        name="fft_stage_b",
    )(a4, a4, tabs["gf"], tabs["gnyq"], scale)


def _cmul(x, k, n2):
    xr, xi = x[:n2], x[n2:]
    kr, ki = k[:n2], k[n2:]
    return jnp.concatenate([xr * kr - xi * ki, xr * ki + xi * kr], axis=0).astype(BF16)


def _spec_body(ar_ref, ai_ref, k_ref, kn_ref, g_ref, gn_ref, gi_ref, m2_ref, o_ref, *, kb, n2):
    j = pl.program_id(1)
    c = ar_ref.shape[3] // n2
    a_re = ar_ref[0, 0].reshape(kb, n2, c)
    a_im = ai_ref[0, 0].reshape(kb, n2, c)

    def through(fwd, rhs, kf, inv):
        x = jnp.dot(fwd, rhs, preferred_element_type=F32)
        return jnp.dot(inv, _cmul(x, kf.astype(F32), n2), preferred_element_type=F32)

    re_rows, im_rows = [], []
    for kk in range(kb):
        rhs = jnp.concatenate([a_re[kk], a_im[kk]], axis=0)
        acc = through(g_ref[kk], rhs, k_ref[0, kk], gi_ref[kk])
        if kk == 0:
            nyq = through(gn_ref[...], rhs, kn_ref[0], m2_ref[...])
            acc = jnp.where(j == 0, jnp.concatenate([acc[:n2], nyq[n2:]], axis=0), acc)
        re_rows.append(acc[:n2].astype(BF16))
        im_rows.append(acc[n2:].astype(BF16))
    o_ref[0, 0] = jnp.stack(re_rows, axis=0).reshape(kb, n2 * c)
    o_ref[0, 1] = jnp.stack(im_rows, axis=0).reshape(kb, n2 * c)


def _spec_conv(a4, kf, kfnyq, order, tabs):
    bsz, _, n1h, nc = a4.shape
    tn2 = kf.shape[2]
    n2 = tn2 // 2
    c = nc // n2
    kb = 16
    return pl.pallas_call(
        functools.partial(_spec_body, kb=kb, n2=n2),
        grid=(bsz, n1h // kb),
        in_specs=[pl.BlockSpec((1, 1, kb, nc), lambda b, j: (b, 0, j, 0)),
                  pl.BlockSpec((1, 1, kb, nc), lambda b, j: (b, 1, j, 0)),
                  pl.BlockSpec((1, kb, tn2, c), lambda b, j: (0, j, 0, order)),
                  pl.BlockSpec((1, tn2, c), lambda b, j: (0, 0, order)),
                  pl.BlockSpec((kb, tn2, tn2), lambda b, j: (j, 0, 0)),
                  pl.BlockSpec((tn2, tn2), lambda b, j: (0, 0)),
                  pl.BlockSpec((kb, tn2, tn2), lambda b, j: (j, 0, 0)),
                  pl.BlockSpec((tn2, tn2), lambda b, j: (0, 0))],
        out_specs=pl.BlockSpec((1, 2, kb, nc), lambda b, j: (b, 0, j, 0)),
        out_shape=jax.ShapeDtypeStruct((bsz, 2, n1h, nc), BF16),
        compiler_params=_params(("parallel", "arbitrary"), 48),
        name="spectral_conv",
    )(a4, a4, kf, kfnyq, tabs["gf"], tabs["gnyq"], tabs["ginv"], tabs["m2"])


def _ifft_a_body(b_ref, f_ref, z_ref, gate_ref, bias_ref, o_ref):
    y = jnp.dot(f_ref[...], b_ref[0], preferred_element_type=F32)
    res = gate_ref[0] * (y + z_ref[0] * bias_ref[...])
    o_ref[0] = res.reshape(o_ref.shape[1:])


def _ifft_a(b3, fi, z3, gate3, bias_t, time_major):
    bsz, n1, nc = b3.shape
    n1h = n1 // 2
    tn = bias_t.shape[1]
    c = D_HYENA
    if time_major:
        out_spec = pl.BlockSpec((1, n1h, tn // c, c), lambda b, j: (b, 0, j, 0))
        out_shape = jax.ShapeDtypeStruct((bsz, n1h, nc // c, c), F32)
    else:
        out_spec = pl.BlockSpec((1, n1h, tn), lambda b, j: (b, 0, j))
        out_shape = jax.ShapeDtypeStruct((bsz, n1h, nc), F32)
    return pl.pallas_call(
        _ifft_a_body,
        grid=(bsz, nc // tn),
        in_specs=[pl.BlockSpec((1, n1, tn), lambda b, j: (b, 0, j)),
                  pl.BlockSpec((n1h, n1), lambda b, j: (0, 0)),
                  pl.BlockSpec((1, n1h, tn), lambda b, j: (b, 0, j)),
                  pl.BlockSpec((1, n1h, tn), lambda b, j: (b, 0, j)),
                  pl.BlockSpec((1, tn), lambda b, j: (0, 0))],
        out_specs=out_spec,
        out_shape=out_shape,
        compiler_params=_params(("parallel", "parallel"), 48),
        name="ifft_stage_a",
    )(b3, fi, z3, gate3, bias_t)


def _hyena(uh, conv_w, conv_b, fw, bias):
    bsz, seq, _ = uh.shape
    c = D_HYENA
    n1, n2 = _fft_dims(seq)
    n1h = n1 // 2
    tabs = _dft_tables(n1, n2)
    v, x1, x2 = _sconv(uh, conv_w, conv_b, n2)

    kcirc, ksum = _filters(seq, n1, n2, fw)
    cols = HYENA_ORDER * c
    ka = _fft_a(kcirc[None], tabs["fa_full"])
    kf, kfnyq = _fft_b(ka, tabs, ksum)

    tn = min(8192, n2 * c)
    z = v
    for order, gate in enumerate((x1, x2)):
        za = _fft_a(z, tabs["fa_half"])
        zb = _spec_conv(za, kf, kfnyq, order, tabs)
        bias_t = jnp.tile(bias[order][None, :], (1, tn // c))
        z = _ifft_a(zb.reshape(bsz, n1, n2 * c), tabs["fi"], z, gate, bias_t,
                    time_major=order == HYENA_ORDER - 1)
    return z.reshape(bsz, seq, c)


def _qkv_body(cq_ref, ckvr_ref, cos_ref, sin_ref, wq_ref, wqr_ref, wk_ref, wv_ref, qa_ref, kva_ref, qg_ref,
              kg_ref, qt_ref, k_ref, vt_ref):
    cqn = _rms(cq_ref[0]) * qa_ref[...]
    qs = jnp.dot(cqn.astype(BF16), wq_ref[...], preferred_element_type=F32)
    ck = ckvr_ref[0]
    ckvn = (_rms(ck[:, :KV_LORA]) * kva_ref[...]).astype(BF16)
    ks = jnp.dot(ckvn, wk_ref[...], preferred_element_type=F32)
    vs = jnp.dot(ckvn, wv_ref[...], preferred_element_type=F32)
    krope = pltpu.roll(ck[:, KV_LORA:], QK_NOPE, axis=1)
    cos = cos_ref[...]
    sin = sin_ref[...]
    lane = lax.broadcasted_iota(I32, cos.shape, 1)
    half = QK_ROPE // 2
    first = (lane >= QK_NOPE) & (lane < QK_NOPE + half)
    second = (lane >= QK_NOPE + half) & (lane < QK_HEAD)

    def head(xh, xh_rot, gain_cos, gain_sin):
        rs = lax.rsqrt(jnp.sum(xh * xh, axis=-1, keepdims=True) * (1.0 / QK_HEAD) + EPS)
        return rs * (xh * gain_cos + xh_rot * gain_sin)

    q_scale = QK_HEAD ** -0.5 * math.log2(math.e)
    q_cos, q_sin = qg_ref[0:1, :] * cos * q_scale, qg_ref[1:2, :] * sin * q_scale
    k_cos, k_sin = kg_ref[0:1, :] * cos, kg_ref[1:2, :] * sin
    qs_rot = jnp.dot(cqn.astype(BF16), wqr_ref[...], preferred_element_type=F32)
    krope_rot = jnp.where(first, -pltpu.roll(krope, SLOT - half, axis=1),
                          jnp.where(second, pltpu.roll(krope, half, axis=1), 0.0))
    eye = (lax.broadcasted_iota(I32, (SLOT, SLOT), 0) == lax.broadcasted_iota(I32, (SLOT, SLOT), 1)).astype(BF16)
    transpose = lambda x: lax.dot_general(eye, x.astype(BF16), (((1,), (1,)), ((), ())),
                                          preferred_element_type=F32).astype(BF16)
    for h in range(N_HEADS):
        sl = slice(h * SLOT, (h + 1) * SLOT)
        qt_ref[0, h] = transpose(head(qs[:, sl], qs_rot[:, sl], q_cos, q_sin))
        k_ref[0, h] = head(ks[:, sl] + krope, krope_rot, k_cos, k_sin).astype(BF16)
    aug = lax.broadcasted_iota(I32, (V_AUG - V_HEAD, cos.shape[0]), 0)
    ones_row = jnp.where(aug == 0, 1.0, 0.0).astype(BF16)
    for hp in range(N_HEADS // 2):
        pair_t = transpose(vs[:, hp * 2 * V_HEAD:(hp + 1) * 2 * V_HEAD])
        vt_ref[0, hp] = jnp.concatenate([pair_t[:V_HEAD], ones_row, pair_t[V_HEAD:], ones_row], axis=0)


def _qkv(cq, ckvr, cos_t, sin_t, aw):
    bsz, seq, _ = cq.shape
    tm = min(512, seq)
    const = lambda shape: pl.BlockSpec(shape, lambda b, i: (0,) * len(shape))
    return pl.pallas_call(
        _qkv_body,
        grid=(bsz, seq // tm),
        in_specs=[pl.BlockSpec((1, tm, Q_LORA), lambda b, i: (b, i, 0)),
                  pl.BlockSpec((1, tm, D_IN_PAD - COL_Q), lambda b, i: (b, i, 0)),
                  pl.BlockSpec((tm, SLOT), lambda b, i: (i, 0)),
                  pl.BlockSpec((tm, SLOT), lambda b, i: (i, 0)),
                  const((Q_LORA, N_HEADS * SLOT)), const((Q_LORA, N_HEADS * SLOT)),
                  const((KV_LORA, N_HEADS * SLOT)), const((KV_LORA, D_ATTN)),
                  const((1, Q_LORA)), const((1, KV_LORA)), const((2, SLOT)), const((2, SLOT))],
        out_specs=[pl.BlockSpec((1, N_HEADS, SLOT, tm), lambda b, i: (b, 0, 0, i)),
                   pl.BlockSpec((1, N_HEADS, tm, SLOT), lambda b, i: (b, 0, i, 0)),
                   pl.BlockSpec((1, N_HEADS // 2, 2 * V_AUG, tm), lambda b, i: (b, 0, 0, i))],
        out_shape=[jax.ShapeDtypeStruct((bsz, N_HEADS, SLOT, seq), BF16),
                   jax.ShapeDtypeStruct((bsz, N_HEADS, seq, SLOT), BF16),
                   jax.ShapeDtypeStruct((bsz, N_HEADS // 2, 2 * V_AUG, seq), BF16)],
        compiler_params=_params(("parallel", "parallel"), 40),
        name="qkv_prep",
    )(cq, ckvr, cos_t, sin_t, aw["wq"], aw["wq_rot"], aw["wk"], aw["wv"], aw["qa"], aw["kva"], aw["qg"],
      aw["kg"])


def _flash_body(qt_ref, k_ref, vt_ref, o_ref, s0_ref, s1_ref, acc_ref, *, tk, nk, nqc, unroll):
    chains = [(hh, qc) for hh in range(2) for qc in range(nqc)]
    acc_ref[...] = jnp.zeros(acc_ref.shape, F32)

    def qk(t, s_ref, only=None):
        ks = pl.multiple_of(t * tk, tk)
        tile_max = []
        for c, (hh, qc) in enumerate(chains):
            if only is not None and c != only:
                continue
            k = k_ref[0, hh, pl.ds(ks, tk), :]
            s = jnp.dot(k, qt_ref[0, hh, :, qc * ATT_QC:(qc + 1) * ATT_QC], preferred_element_type=F32)
            s_ref[c] = s
            tile_max.append(jnp.max(s, axis=0, keepdims=True))
        return tile_max

    def softmax_pv(t, s_ref, tile_max, m, only=None):
        ks = pl.multiple_of(t * tk, tk)
        m_out = []
        for c, (hh, qc) in enumerate(chains):
            if only is not None and c != only:
                continue
            i = 0 if only is not None else c
            m_new = jnp.maximum(m[i], tile_max[i])
            a = jnp.exp2(m[i] - m_new)
            p = jnp.exp2(s_ref[c] - m_new)
            m_out.append(m_new)
            vt = vt_ref[0, 0, hh * V_AUG:(hh + 1) * V_AUG, pl.ds(ks, tk)]
            acc_ref[c] = acc_ref[c] * a + jnp.dot(vt, p.astype(BF16), preferred_element_type=F32)
        return m_out

    def group(u, carry):
        tile_max, m = carry
        for i in range(0, unroll, 2):
            t = unroll * u + i
            tm1, tm0 = [], []
            m = list(m)
            for c in range(len(chains)):
                tm1 += qk(t + 1, s1_ref, only=c)
                m[c] = softmax_pv(t, s0_ref, [tile_max[c]], [m[c]], only=c)[0]
            for c in range(len(chains)):
                tm0 += qk(jnp.minimum(t + 2, nk - 1), s0_ref, only=c)
                m[c] = softmax_pv(t + 1, s1_ref, [tm1[c]], [m[c]], only=c)[0]
            tile_max = tm0
        return tile_max, m

    init = (qk(0, s0_ref), [jnp.full((1, ATT_QC), -jnp.inf, F32)] * len(chains))
    lax.fori_loop(0, nk // unroll, group, init)

    def normalised(c):
        acc = acc_ref[c]
        return acc[:V_HEAD] * (1.0 / acc[V_HEAD:V_HEAD + 1])

    heads = [jnp.concatenate([normalised(hh * nqc + qc) for qc in range(nqc)], axis=1) for hh in range(2)]
    o_ref[0] = jnp.concatenate(heads, axis=0).T


def _flash(qt, k, vt):
    bsz, nh, _, seq = qt.shape
    tq = min(ATT_TQ, seq)
    tk = min(ATT_TK, seq)
    nqc = tq // ATT_QC
    nk = seq // tk
    unroll = min(ATT_UNROLL, nk)
    assert nk % unroll == 0 and unroll % 2 == 0
    return pl.pallas_call(
        functools.partial(_flash_body, tk=tk, nk=nk, nqc=nqc, unroll=unroll),
        grid=(bsz, nh // 2, seq // tq),
        in_specs=[pl.BlockSpec((1, 2, SLOT, tq), lambda b, hp, i: (b, hp, 0, i)),
                  pl.BlockSpec((1, 2, seq, SLOT), lambda b, hp, i: (b, hp, 0, 0)),
                  pl.BlockSpec((1, 1, 2 * V_AUG, seq), lambda b, hp, i: (b, hp, 0, 0))],
        out_specs=pl.BlockSpec((1, tq, 2 * V_HEAD), lambda b, hp, i: (b, i, hp)),
        out_shape=jax.ShapeDtypeStruct((bsz, seq, nh * V_HEAD), F32),
        scratch_shapes=[pltpu.VMEM((2 * nqc, tk, ATT_QC), F32), pltpu.VMEM((2 * nqc, tk, ATT_QC), F32),
                        pltpu.VMEM((2 * nqc, V_AUG, ATT_QC), F32)],
        compiler_params=_params(("parallel", "parallel", "parallel"), 48),
        name="flash_attn",
    )(qt, k, vt)


def _outproj_body(yh_ref, ya_ref, x_ref, og_ref, w_ref, mg_ref, wr_ref, xm_ref, xn_ref, aff_ref, affn_ref):
    og = og_ref[...]
    half = D_MODEL // 2
    tm = x_ref.shape[0]
    chunks = [slice(r, r + tm // OUTPROJ_CHUNKS) for r in range(0, tm, tm // OUTPROJ_CHUNKS)]
    y = [jnp.concatenate([_rms(yh_ref[rows, :]) * og[:, :half], _rms(ya_ref[rows, :]) * og[:, half:]],
                         axis=1).astype(BF16) for rows in chunks]
    xm = [x_ref[rows, :] + jnp.dot(yc, w_ref[...], preferred_element_type=F32) for rows, yc in zip(chunks, y)]
    xn = [_rms(xc) * mg_ref[...] for xc in xm]
    xn_hi = [xc.astype(BF16) for xc in xn]
    xn_lo = [(xc - hc.astype(F32)).astype(BF16) for xc, hc in zip(xn, xn_hi)]
    logits = [jnp.dot(hc, wr_ref[0], preferred_element_type=F32)
              + jnp.dot(lc, wr_ref[0], preferred_element_type=F32)
              + jnp.dot(hc, wr_ref[1], preferred_element_type=F32) for hc, lc in zip(xn_hi, xn_lo)]
    lane = lax.broadcasted_iota(I32, logits[0].shape, 1)
    for rows, xc, hc, lg in zip(chunks, xm, xn_hi, logits):
        xm_ref[rows, :] = xc
        xn_ref[rows, :] = hc
        lg = jnp.where(lane < N_EXPERTS, lg, -jnp.inf)
        e = jnp.exp(lg - jnp.max(lg, axis=-1, keepdims=True))
        aff = e / jnp.sum(e, axis=-1, keepdims=True)
        aff_ref[:, rows] = aff.T[:N_EXPERTS]
        affn_ref[rows, :] = aff[:, :N_EXPERTS]


def _outproj(yh, ya, x2d, og, w_out, mg, wr_pad):
    n = x2d.shape[0]
    tm = min(512, n)
    half = D_MODEL // 2
    const = lambda shape: pl.BlockSpec(shape, lambda i: (0,) * len(shape))
    return pl.pallas_call(
        _outproj_body,
        grid=(n // tm,),
        in_specs=[pl.BlockSpec((tm, half), lambda i: (i, 0)),
                  pl.BlockSpec((tm, half), lambda i: (i, 0)),
                  pl.BlockSpec((tm, D_MODEL), lambda i: (i, 0)),
                  const((1, D_MODEL)), const((D_MODEL, D_MODEL)), const((1, D_MODEL)),
                  const((2, D_MODEL, LANES))],
        out_specs=[pl.BlockSpec((tm, D_MODEL), lambda i: (i, 0)),
                   pl.BlockSpec((tm, D_MODEL), lambda i: (i, 0)),
                   pl.BlockSpec((N_EXPERTS, tm), lambda i: (0, i)),
                   pl.BlockSpec((tm, N_EXPERTS), lambda i: (i, 0))],
        out_shape=[jax.ShapeDtypeStruct((n, D_MODEL), F32),
                   jax.ShapeDtypeStruct((n, D_MODEL), BF16),
                   jax.ShapeDtypeStruct((N_EXPERTS, n), F32),
                   jax.ShapeDtypeStruct((n, N_EXPERTS), F32)],
        compiler_params=_params(("parallel",), 40),
        name="outproj_router",
    )(yh, ya, x2d, og, w_out, mg, wr_pad)


def _select_body(aff_ref, upper_ref, lower_ref, pos_ref, off_ref, *, cap):
    group = aff_ref.shape[0]
    bits = [pltpu.bitcast(aff_ref[x], I32) for x in range(group)]
    upper = upper_ref[...]
    lower = lower_ref[...]

    def count(mask):
        return jnp.sum(jnp.sum(mask.astype(F32), axis=1, keepdims=True), axis=0, keepdims=True)

    def bit_step(i, thr):
        bit = jnp.left_shift(jnp.int32(1), 30 - i)
        return tuple(jnp.where(count(bits[x] >= (thr[x] | bit)) >= cap, thr[x] | bit, thr[x])
                     for x in range(group))

    thr = lax.fori_loop(0, 31, bit_step, tuple(jnp.zeros((1, 1), I32) for _ in range(group)))

    def prefix(mask):
        within = jnp.dot(mask.astype(BF16), upper, preferred_element_type=F32)
        total = within[:, LANES - 1:LANES]
        offs = jnp.dot(lower, jnp.broadcast_to(total, within.shape).astype(BF16), preferred_element_type=F32)
        return within, offs

    for x in range(group):
        gt = bits[x] > thr[x]
        eq = bits[x] == thr[x]
        need = cap - count(gt)
        w_eq, o_eq = prefix(eq)
        sel = gt | (eq & (o_eq + w_eq <= need))
        w_sel, o_sel = prefix(sel)
        pos_ref[x] = jnp.where(sel, o_sel + w_sel - 1.0, -1.0).astype(I32)
        off_ref[x] = o_sel.astype(I32)


def _select(aff_t, cap):
    ne, n = aff_t.shape
    rows = n // LANES
    upper = (jnp.arange(LANES)[:, None] <= jnp.arange(LANES)[None, :]).astype(BF16)
    lower = (jnp.arange(rows)[None, :] < jnp.arange(rows)[:, None]).astype(BF16)
    group = 4
    blk = pl.BlockSpec((group, rows, LANES), lambda e: (e, 0, 0))
    pos, off = pl.pallas_call(
        functools.partial(_select_body, cap=cap),
        grid=(ne // group,),
        in_specs=[blk, pl.BlockSpec((LANES, LANES), lambda e: (0, 0)),
                  pl.BlockSpec((rows, rows), lambda e: (0, 0))],
        out_specs=[blk, blk],
        out_shape=[jax.ShapeDtypeStruct((ne, rows, LANES), I32),
                   jax.ShapeDtypeStruct((ne, rows, LANES), I32)],
        compiler_params=_params(("parallel",), 32),
        name="ec_select",
    )(aff_t.reshape(ne, rows, LANES), upper, lower)
    return pos.reshape(ne, n), off[:, :, 0]


def _ffn_body(off_ref, nch_ref, pos_ref, x_ref, wg_ref, wu_ref, wd_ref, y_ref, xs_ref, *, nb, sub, cap):
    group = y_ref.shape[0]
    e0 = pl.program_id(0) * group
    sb = pl.program_id(1)

    @pl.when(sb == 0)
    def _():
        xs_ref[...] = jnp.zeros(xs_ref.shape, BF16)

    row = lax.broadcasted_iota(I32, (GATHER_ROWS, TOK_BLOCK), 0)

    def first_chunks(i):
        tok = slice(i * TOK_BLOCK, (i + 1) * TOK_BLOCK)
        starts = [pl.multiple_of(off_ref[(e0 + x) * nb + sb * sub + i] * ALIGN, ALIGN) for x in range(group)]
        hits = [(pos_ref[x, :, tok] - starts[x]) == row for x in range(group)]
        win = jnp.dot(jnp.concatenate(hits, axis=0).astype(BF16), x_ref[tok, :], preferred_element_type=F32)
        for x in range(group):
            xs_ref[x, pl.ds(starts[x], GATHER_ROWS), :] += win[x * GATHER_ROWS:(x + 1) * GATHER_ROWS].astype(BF16)

    def later_chunk(x, i, c):
        tok = slice(i * TOK_BLOCK, (i + 1) * TOK_BLOCK)
        start = pl.multiple_of(off_ref[(e0 + x) * nb + sb * sub + i] * ALIGN + c * GATHER_ROWS, ALIGN)
        hit = (pos_ref[x, :, tok] - start) == row
        win = jnp.dot(hit.astype(BF16), x_ref[tok, :], preferred_element_type=F32)
        xs_ref[x, pl.ds(start, GATHER_ROWS), :] += win.astype(BF16)

    for i in range(sub):
        first_chunks(i)
    for x in range(group):
        for i in range(sub):
            lax.fori_loop(1, nch_ref[(e0 + x) * nb + sb * sub + i],
                          lambda c, carry, x=x, i=i: (later_chunk(x, i, c), carry)[1], 0)

    @pl.when(sb == pl.num_programs(1) - 1)
    def _():
        for x in range(group):
            for j in range(cap // SLOT_TILE):
                rows = slice(j * SLOT_TILE, (j + 1) * SLOT_TILE)
                xt = xs_ref[x, rows, :]
                g = jnp.dot(xt, wg_ref[x], preferred_element_type=F32)
                u = jnp.dot(xt, wu_ref[x], preferred_element_type=F32)
                h = (g * jax.nn.sigmoid(g) * u).astype(BF16)
                y_ref[x, rows, :] = jnp.dot(h, wd_ref[x], preferred_element_type=F32).astype(BF16)
            y_ref[x, cap:, :] = jnp.zeros((y_ref.shape[1] - cap, D_MODEL), BF16)


def _ffn(base, nchunk, pos3, xn, wg, wu, wd, cap):
    ne = wg.shape[0]
    n = xn.shape[0]
    nb = n // TOK_BLOCK
    sub = min(4, nb)
    group = 2
    cap_pad = cap + TOK_BLOCK + ALIGN
    grid_spec = pltpu.PrefetchScalarGridSpec(
        num_scalar_prefetch=2,
        grid=(ne // group, nb // sub),
        in_specs=[pl.BlockSpec((group, 1, sub * TOK_BLOCK), lambda e, s, off, nch: (e, 0, s)),
                  pl.BlockSpec((sub * TOK_BLOCK, D_MODEL), lambda e, s, off, nch: (s, 0)),
                  pl.BlockSpec((group, D_MODEL, D_EXPERT), lambda e, s, off, nch: (e, 0, 0)),
                  pl.BlockSpec((group, D_MODEL, D_EXPERT), lambda e, s, off, nch: (e, 0, 0)),
                  pl.BlockSpec((group, D_EXPERT, D_MODEL), lambda e, s, off, nch: (e, 0, 0))],
        out_specs=pl.BlockSpec((group, cap_pad, D_MODEL), lambda e, s, off, nch: (e, 0, 0)),
        scratch_shapes=[pltpu.VMEM((group, cap_pad, D_MODEL), BF16)])
    return pl.pallas_call(
        functools.partial(_ffn_body, nb=nb, sub=sub, cap=cap),
        grid_spec=grid_spec,
        out_shape=jax.ShapeDtypeStruct((ne, cap_pad, D_MODEL), BF16),
        compiler_params=_params(("parallel", "arbitrary"), 52),
        name="ec_ffn",
    )(base, nchunk, pos3, xn, wg, wu, wd)


def _combine_body(off_ref, spill_ref, post_ref, affn_ref, xm_ref, *refs, nb, ne, win_rows):
    win_refs, tail_refs, o_ref = refs[:ne], refs[ne:-1], refs[-1]
    b = pl.program_id(0)
    post = post_ref[...]
    affn = affn_ref[...]

    def expand(y_refs, shift):
        width = y_refs[0].shape[0]
        lane = lax.broadcasted_iota(I32, (TOK_BLOCK, width), 1)
        total = jnp.zeros((TOK_BLOCK, D_MODEL), F32)
        for e in range(ne):
            rel = post[:, e:e + 1] - (off_ref[e * nb + b] * ALIGN + shift)
            hit = (rel == lane).astype(BF16)
            total = total + affn[:, e:e + 1] * jnp.dot(hit, y_refs[e][...], preferred_element_type=F32)
        return total

    def expand_pairs(y_refs):
        width = y_refs[0].shape[0]
        lane = lax.broadcasted_iota(I32, (TOK_BLOCK, 2 * width), 1)
        total = jnp.zeros((TOK_BLOCK, D_MODEL), F32)
        for e in range(0, ne, 2):
            rel0 = post[:, e:e + 1] - off_ref[e * nb + b] * ALIGN
            rel1 = post[:, e + 1:e + 2] - off_ref[(e + 1) * nb + b] * ALIGN + width
            gates = (jnp.where(rel0 == lane, affn[:, e:e + 1], 0.0)
                     + jnp.where((rel1 == lane) & (rel1 >= width), affn[:, e + 1:e + 2], 0.0))
            rows = jnp.concatenate([y_refs[e][...], y_refs[e + 1][...]], axis=0)
            total = total + jnp.dot(gates.astype(BF16), rows, preferred_element_type=F32)
        return total

    if not tail_refs and 2 * win_rows <= TOK_BLOCK and ne % 2 == 0:
        o_ref[...] = xm_ref[...] + expand_pairs(win_refs)
    else:
        o_ref[...] = xm_ref[...] + expand(win_refs, 0)

    if tail_refs:
        @pl.when(spill_ref[b] != 0)
        def _():
            o_ref[...] += expand(tail_refs, win_rows)


def _combine(base, spill, pos_t, aff_n, xm, y, win_rows):
    ne = y.shape[0]
    n = xm.shape[0]
    nb = n // TOK_BLOCK
    tail_rows = ALIGN if win_rows == TOK_BLOCK else 0

    def window(e, rows, shift):
        return pl.BlockSpec((pl.Squeezed(), pl.Element(rows), pl.Element(D_MODEL)),
                            lambda b, off, sp: (e, (off[e * nb + b] + shift // ALIGN) * ALIGN, 0))

    tails = [window(e, tail_rows, win_rows) for e in range(ne)] if tail_rows else []
    grid_spec = pltpu.PrefetchScalarGridSpec(
        num_scalar_prefetch=2,
        grid=(nb,),
        in_specs=([pl.BlockSpec((TOK_BLOCK, ne), lambda b, off, sp: (b, 0)),
                   pl.BlockSpec((TOK_BLOCK, ne), lambda b, off, sp: (b, 0)),
                   pl.BlockSpec((TOK_BLOCK, D_MODEL), lambda b, off, sp: (b, 0))]
                  + [window(e, win_rows, 0) for e in range(ne)] + tails),
        out_specs=pl.BlockSpec((TOK_BLOCK, D_MODEL), lambda b, off, sp: (b, 0)))
    return pl.pallas_call(
        functools.partial(_combine_body, nb=nb, ne=ne, win_rows=win_rows),
        grid_spec=grid_spec,
        out_shape=jax.ShapeDtypeStruct((n, D_MODEL), F32),
        compiler_params=_params(("parallel",), 48),
        name="ec_combine",
    )(base, spill, pos_t, aff_n, xm, *([y] * (ne + len(tails))))


def _ec_moe(xm, xn, aff_t, aff_n, wg, wu, wd):
    ne, n = aff_t.shape
    cap = max(1, EC_CAPACITY * n // N_EXPERTS)
    nb = n // TOK_BLOCK
    pos, off = _select(aff_t, cap)
    first = off[:, ::TOK_BLOCK // LANES]
    count = jnp.concatenate([first[:, 1:], jnp.full((ne, 1), cap, I32)], axis=1) - first
    base = first // ALIGN
    span = jnp.where(count > 0, first - base * ALIGN + count, 0)
    spill = jnp.any(span > TOK_BLOCK, axis=0).astype(I32)
    nchunk = ((span + GATHER_ROWS - 1) // GATHER_ROWS).reshape(ne * nb).astype(I32)
    base = base.reshape(ne * nb).astype(I32)
    y = _ffn(base, nchunk, pos.reshape(ne, 1, n), xn, wg, wu, wd, cap)
    pos_t = pos.T
    return lax.cond(jnp.max(span) <= COMBINE_FAST_ROWS,
                    lambda: _combine(base, spill, pos_t, aff_n, xm, y, COMBINE_FAST_ROWS),
                    lambda: _combine(base, spill, pos_t, aff_n, xm, y, TOK_BLOCK))


def _slot_cols(w, head_width):
    k = w.shape[0]
    w3 = w.reshape(k, N_HEADS, head_width)
    return jnp.pad(w3, ((0, 0), (0, 0), (0, SLOT - head_width))).reshape(k, N_HEADS * SLOT)


def _attn_weights(q_a_norm, w_uq, kv_a_norm, w_ukv, q_norm, k_norm):
    wkv = w_ukv.reshape(KV_LORA, N_HEADS, QK_NOPE + V_HEAD)
    wk = _slot_cols(wkv[:, :, :QK_NOPE].reshape(KV_LORA, -1), QK_NOPE)
    wv = wkv[:, :, QK_NOPE:].reshape(KV_LORA, D_ATTN)
    half = QK_ROPE // 2

    def rotate_half(a):
        return jnp.concatenate([jnp.zeros_like(a[..., :QK_NOPE]), -a[..., QK_NOPE + half:],
                                a[..., QK_NOPE:QK_NOPE + half]], axis=-1)

    def gains(g):
        swapped = jnp.concatenate([jnp.zeros_like(g[:QK_NOPE]), g[QK_NOPE + half:], g[QK_NOPE:QK_NOPE + half]])
        return jnp.pad(jnp.stack([g, swapped]), ((0, 0), (0, SLOT - QK_HEAD)))

    wq3 = w_uq.reshape(Q_LORA, N_HEADS, QK_HEAD)
    return dict(wq=_slot_cols(w_uq, QK_HEAD).astype(BF16),
                wq_rot=_slot_cols(rotate_half(wq3).reshape(Q_LORA, -1), QK_HEAD).astype(BF16),
                wk=wk.astype(BF16), wv=wv.astype(BF16), qa=q_a_norm[None, :], kva=kv_a_norm[None, :],
                qg=gains(q_norm), kg=gains(k_norm))


def _rope_tables(seq):
    pos = jnp.arange(seq, dtype=F32)
    inv_freq = ROPE_THETA ** (-jnp.arange(0, QK_ROPE, 2, dtype=F32) / QK_ROPE)
    ang = pos[:, None] * inv_freq
    ang = jnp.concatenate([ang, ang], axis=-1)
    pad = lambda t, fill: jnp.concatenate(
        [jnp.full((seq, QK_NOPE), fill, F32), t, jnp.full((seq, SLOT - QK_HEAD), fill, F32)], axis=1)
    return pad(jnp.cos(ang), 1.0), pad(jnp.sin(ang), 0.0)


def _hi_lo(w):
    hi = w.astype(BF16)
    return jnp.stack([hi, (w - hi.astype(F32)).astype(BF16)], axis=1)


def _filter_weights(w1, b1, w2, b2, w3, freq, decay):
    hid, nbands = FILTER_HIDDEN, FILTER_BANDS
    assert 2 * hid == LANES and 2 * nbands <= hid
    both = lambda row: jnp.concatenate([row, row])[None, :]
    blockdiag = lambda a: jnp.concatenate(
        [jnp.concatenate([a, jnp.zeros_like(a)], axis=1), jnp.concatenate([jnp.zeros_like(a), a], axis=1)], axis=0)
    bands = jnp.linspace(1e-4, nbands - 1, nbands, dtype=F32)
    rest = jnp.zeros((hid - 2 * nbands,), F32)
    band_half = jnp.concatenate([bands, bands, rest])
    phase_half = jnp.concatenate([jnp.zeros((nbands,), F32), jnp.full((nbands,), 0.5 * math.pi, F32), rest])
    w1_half = jnp.concatenate([-w1[1 + nbands:], w1[1:1 + nbands], jnp.zeros((hid - 2 * nbands, hid), F32)], axis=0)
    cols = HYENA_ORDER * D_HYENA
    w3d = jnp.transpose(w3.reshape(hid, N_DIR, cols), (1, 0, 2))
    w3p = jnp.stack([jnp.pad(w3d[0], ((0, hid), (0, 0))), jnp.pad(w3d[1], ((hid, 0), (0, 0)))], axis=0)
    w3_hi = w3p.astype(BF16)
    w3_lo = (w3p - w3_hi.astype(F32)).astype(BF16)
    return dict(bands=both(band_half), phase=both(phase_half), w1t=both(w1[0]), w1=blockdiag(w1_half),
                b1=both(b1), w2=blockdiag(w2), b2=both(b2), freq=both(freq), w3_hi=w3_hi, w3_lo=w3_lo,
                decay=decay.reshape(N_DIR, 1, cols))


def _trunk(x, p):
    bsz, seq, _ = x.shape
    n = bsz * seq
    cos_t, sin_t = _rope_tables(seq)
    x2 = x.reshape(n, D_MODEL)
    for l in range(DEPTH):
        uh, cq, ckvr = _inproj(x2, p["attn_norm"][l][None, :], p["w_in"][l])
        y_h = _hyena(uh.reshape(bsz, seq, COL_HYENA), p["conv_w"][l], p["conv_b"][l][None, :],
                     p["filt"][l], p["hyena_bias"][l])
        qt, k, vt = _qkv(cq.reshape(bsz, seq, -1), ckvr.reshape(bsz, seq, -1), cos_t, sin_t, p["attn"][l])
        y_a = _flash(qt, k, vt)
        xm, xn, aff_t, aff_n = _outproj(y_h.reshape(n, D_HYENA), y_a.reshape(n, D_ATTN), x2,
                                 p["out_norm"][l][None, :], p["w_out"][l], p["mlp_norm"][l][None, :],
                                 p["w_router"][l])
        x2 = _ec_moe(xm, xn, aff_t, aff_n, p["w_gate"][l], p["w_up"][l], p["w_down"][l])
    return x2.reshape(bsz, seq, D_MODEL)


def kernel(x_prompt, x_sample, attn_norm, w_in, conv_w, conv_b, filt_w1, filt_b1, filt_w2, filt_b2, filt_w3,
           filt_freq, filt_decay, hyena_bias, q_a_norm, w_uq, kv_a_norm, w_ukv, q_norm, k_norm, out_norm,
           w_out, mlp_norm, w_router, w_gate, w_up, w_down):
    p = dict(
        attn_norm=attn_norm,
        w_in=jnp.pad(w_in, ((0, 0), (0, 0), (0, D_IN_PAD - D_IN))).astype(BF16),
        conv_w=conv_w, conv_b=conv_b, hyena_bias=hyena_bias,
        filt=[_filter_weights(filt_w1[l], filt_b1[l], filt_w2[l], filt_b2[l], filt_w3[l], filt_freq[l],
                              filt_decay[l]) for l in range(DEPTH)],
        attn=[_attn_weights(q_a_norm[l], w_uq[l], kv_a_norm[l], w_ukv[l], q_norm[l], k_norm[l])
              for l in range(DEPTH)],
        out_norm=out_norm, w_out=w_out.astype(BF16), mlp_norm=mlp_norm,
        w_router=_hi_lo(jnp.pad(w_router, ((0, 0), (0, 0), (0, LANES - N_EXPERTS)))),
        w_gate=w_gate.astype(BF16), w_up=w_up.astype(BF16), w_down=w_down.astype(BF16))
    return (_trunk(x_prompt, p), _trunk(x_sample, p))
```

```python
import functools
import math

import jax
import jax.numpy as jnp
from jax import lax
from jax.experimental import pallas as pl
from jax.experimental.pallas import tpu as pltpu

F32 = jnp.float32
BF16 = jnp.bfloat16
I32 = jnp.int32
HIGHEST = lax.Precision.HIGHEST

D_MODEL = 1024
DEPTH = 2
D_HYENA = 512
HYENA_ORDER = 2
FILTER_BANDS = 16
FILTER_HIDDEN = 64
N_DIR = 2
DECAY_SHIFT = 0.05
N_HEADS = 8
QK_NOPE = 64
QK_ROPE = 32
QK_HEAD = QK_NOPE + QK_ROPE
V_HEAD = 64
V_AUG = V_HEAD + 16
D_ATTN = N_HEADS * V_HEAD
Q_LORA = 256
KV_LORA = 128
ROPE_THETA = 10000.0
N_EXPERTS = 16
EC_CAPACITY = 2
D_EXPERT = 512
EPS = 1e-6
COL_HYENA = (HYENA_ORDER + 1) * D_HYENA
COL_Q = COL_HYENA + Q_LORA
COL_KV = COL_Q + KV_LORA
D_IN = COL_KV + QK_ROPE
D_IN_PAD = 2048

LANES = 128
SLOT = 128
TOK_BLOCK = 256
SLOT_TILE = 256
ALIGN = 16
GATHER_ROWS = 64
COMBINE_FAST_ROWS = 128
OUTPROJ_CHUNKS = 2
ATT_TQ = 512
ATT_QC = 256
ATT_TK = 256
ATT_UNROLL = 16
MIB = 1024 * 1024


def _params(sem, vmem_mib):
    return pltpu.CompilerParams(dimension_semantics=sem, vmem_limit_bytes=vmem_mib * MIB)


def _rms(x):
    return x * lax.rsqrt(jnp.mean(x * x, axis=-1, keepdims=True) + EPS)


def _inproj_body(x_ref, g_ref, w_ref, uh_ref, cq_ref, ckvr_ref):
    xn = _rms(x_ref[...]) * g_ref[...]
    u = jnp.dot(xn.astype(BF16), w_ref[...], preferred_element_type=F32)
    uh_ref[...] = u[:, :COL_HYENA]
    cq_ref[...] = u[:, COL_HYENA:COL_Q]
    ckvr_ref[...] = u[:, COL_Q:]


def _inproj(x2d, g, w_pad):
    n = x2d.shape[0]
    tm = min(512, n)
    return pl.pallas_call(
        _inproj_body,
        grid=(n // tm,),
        in_specs=[pl.BlockSpec((tm, D_MODEL), lambda i: (i, 0)),
                  pl.BlockSpec((1, D_MODEL), lambda i: (0, 0)),
                  pl.BlockSpec((D_MODEL, D_IN_PAD), lambda i: (0, 0))],
        out_specs=[pl.BlockSpec((tm, COL_HYENA), lambda i: (i, 0)),
                   pl.BlockSpec((tm, Q_LORA), lambda i: (i, 0)),
                   pl.BlockSpec((tm, D_IN_PAD - COL_Q), lambda i: (i, 0))],
        out_shape=[jax.ShapeDtypeStruct((n, COL_HYENA), F32),
                   jax.ShapeDtypeStruct((n, Q_LORA), F32),
                   jax.ShapeDtypeStruct((n, D_IN_PAD - COL_Q), F32)],
        compiler_params=_params(("parallel",), 40),
        name="inproj",
    )(x2d, g, w_pad)


def _sconv_body(u_ref, prev_ref, next_ref, w_ref, b_ref, v_ref, x1_ref, x2_ref):
    i = pl.program_id(1)
    last = pl.num_programs(1) - 1
    u = u_ref[0]
    tl = u.shape[0]
    prev_row = jnp.where(i == 0, 0.0, prev_ref[0, 7:8, :])
    next_row = jnp.where(i == last, 0.0, next_ref[0, 0:1, :])
    row = lax.broadcasted_iota(I32, u.shape, 0)
    up = jnp.where(row == 0, prev_row, pltpu.roll(u, 1, axis=0))
    dn = jnp.where(row == tl - 1, next_row, pltpu.roll(u, tl - 1, axis=0))
    y = up * w_ref[0:1, :] + u * w_ref[1:2, :] + dn * w_ref[2:3, :] + b_ref[...]
    rows, flat = v_ref.shape[1:]
    for k, o_ref in enumerate((v_ref, x1_ref, x2_ref)):
        o_ref[0] = y[:, k * D_HYENA:(k + 1) * D_HYENA].reshape(rows, tl // rows, D_HYENA).reshape(rows, flat)


def _sconv(uh, w, b, n2):
    bsz, seq, c = uh.shape
    rows = 8
    tl = rows * n2
    r = tl // 8
    nblk8 = seq // 8
    out = jax.ShapeDtypeStruct((bsz, seq // n2, n2 * D_HYENA), F32)
    ospec = pl.BlockSpec((1, rows, n2 * D_HYENA), lambda bi, i: (bi, i, 0))
    return pl.pallas_call(
        _sconv_body,
        grid=(bsz, seq // tl),
        in_specs=[pl.BlockSpec((1, tl, c), lambda bi, i: (bi, i, 0)),
                  pl.BlockSpec((1, 8, c), lambda bi, i: (bi, jnp.maximum(i * r - 1, 0), 0)),
                  pl.BlockSpec((1, 8, c), lambda bi, i: (bi, jnp.minimum((i + 1) * r, nblk8 - 1), 0)),
                  pl.BlockSpec((3, c), lambda bi, i: (0, 0)),
                  pl.BlockSpec((1, c), lambda bi, i: (0, 0))],
        out_specs=[ospec, ospec, ospec],
        out_shape=[out, out, out],
        compiler_params=_params(("parallel", "parallel"), 40),
        name="sconv",
    )(uh, uh, uh, w, b)


def _filter_body(bands_ref, phase_ref, w1t_ref, w1_ref, b1_ref, w2_ref, b2_ref, fr_ref, w3h_ref, w3l_ref,
                 dec_ref, k_ref, sum_ref, *, seq, n2):
    i = pl.program_id(0)
    hp = k_ref.shape[0] // 2
    cols = sum_ref.shape[1]
    jb = k_ref.shape[1] // cols
    row = lax.broadcasted_iota(I32, (jb * hp, 1), 0)
    n_lo = (row % hp) * n2 + i * jb + row // hp
    n_hi = n_lo + seq
    t_of = lambda n: jnp.where(n < seq, n, 2 * seq - n).astype(F32)
    lane = lax.broadcasted_iota(I32, (jb * hp, LANES), 1)
    t_idx = jnp.where(lane < FILTER_HIDDEN, t_of(n_lo), t_of(n_hi))
    feats = jnp.sin(2.0 * math.pi * bands_ref[...] * t_idx / seq + phase_ref[...])
    fr = fr_ref[...]
    pre = (t_idx / (seq - 1)) * w1t_ref[...] + jnp.dot(feats, w1_ref[...], precision=HIGHEST,
                                                      preferred_element_type=F32)
    h = jnp.sin(fr * (pre + b1_ref[...]))
    h = jnp.sin(fr * (jnp.dot(h, w2_ref[...], precision=HIGHEST, preferred_element_type=F32) + b2_ref[...]))
    h_hi = h.astype(BF16)
    h_lo = (h - h_hi.astype(F32)).astype(BF16)

    @pl.when(i == 0)
    def _():
        sum_ref[...] = jnp.zeros_like(sum_ref)

    for j, n in enumerate((n_lo, n_hi)):
        out = (jnp.dot(h_hi, w3h_ref[j], preferred_element_type=F32)
               + jnp.dot(h_lo, w3h_ref[j], preferred_element_type=F32)
               + jnp.dot(h_hi, w3l_ref[j], preferred_element_type=F32))
        window = jnp.exp(-(t_of(n) / (seq - 1)) * dec_ref[j]) + DECAY_SHIFT
        k = jnp.where(n == seq, 0.0, out * window)
        sum_ref[...] += jnp.sum(jnp.abs(k), axis=0, keepdims=True)
        for jj in range(jb):
            k_ref[j * hp:(j + 1) * hp, jj * cols:(jj + 1) * cols] = k[jj * hp:(jj + 1) * hp].astype(BF16)


def _filters(seq, n1, n2, fw):
    cols = HYENA_ORDER * D_HYENA
    const = lambda shape: pl.BlockSpec(shape, lambda i: (0,) * len(shape))
    jb = max(1, 1024 // n1)
    return pl.pallas_call(
        functools.partial(_filter_body, seq=seq, n2=n2),
        grid=(n2 // jb,),
        in_specs=[const((1, LANES)), const((1, LANES)), const((1, LANES)), const((LANES, LANES)),
                  const((1, LANES)), const((LANES, LANES)), const((1, LANES)), const((1, LANES)),
                  const((N_DIR, LANES, cols)), const((N_DIR, LANES, cols)), const((N_DIR, 1, cols))],
        out_specs=[pl.BlockSpec((n1, jb * cols), lambda i: (0, i)),
                   pl.BlockSpec((1, cols), lambda i: (0, 0))],
        out_shape=[jax.ShapeDtypeStruct((n1, n2 * cols), BF16),
                   jax.ShapeDtypeStruct((1, cols), F32)],
        compiler_params=_params(("arbitrary",), 32),
        name="filter_gen",
    )(fw["bands"], fw["phase"], fw["w1t"], fw["w1"], fw["b1"], fw["w2"], fw["b2"], fw["freq"],
      fw["w3_hi"], fw["w3_lo"], fw["decay"])


def _fft_dims(seq):
    n2 = 128 if 2 * seq >= 32768 else 64
    n1 = 2 * seq // n2
    return n1, n2


def _dft_tables(n1, n2):
    n = n1 * n2
    n1h = n1 // 2
    k1 = jnp.arange(n1h, dtype=I32)[:, None]
    m1 = jnp.arange(n1, dtype=I32)[None, :]
    ang = (2.0 * math.pi / n1) * ((k1 * m1) % n1).astype(F32)
    top = jnp.cos(ang)
    bot = -jnp.sin(ang)
    nyq = jnp.where(m1 % 2 == 0, 1.0, -1.0).astype(F32)
    bot = jnp.concatenate([nyq, bot[1:]], axis=0)
    fa = jnp.concatenate([top, bot], axis=0)
    weight = jnp.where((jnp.arange(n1) % n1h) == 0, 1.0, 2.0).astype(F32) / n
    fi = (fa[:, :n1h] * weight[:, None]).T

    kk = jnp.arange(n1h + 1, dtype=I32)[:, None, None]
    k2 = jnp.arange(n2, dtype=I32)[None, :, None]
    m2 = jnp.arange(n2, dtype=I32)[None, None, :]
    phi = (2.0 * math.pi / n) * ((m2 * (kk + n1 * k2)) % n).astype(F32)
    gr = jnp.cos(phi)
    gi = -jnp.sin(phi)
    blk = jnp.concatenate([jnp.concatenate([gr, -gi], axis=2),
                           jnp.concatenate([gi, gr], axis=2)], axis=1)
    left = (jnp.arange(2 * n2) < n2)[None, :]
    g0 = jnp.where(left, blk[0], 0.0)
    gf = jnp.concatenate([g0[None], blk[1:n1h]], axis=0)
    gnyq = jnp.concatenate([jnp.zeros((2 * n2, n2), F32), blk[n1h][:, :n2]], axis=1)
    return dict(fa_full=fa.astype(BF16), fa_half=fa[:, :n1h].astype(BF16), fi=fi.astype(BF16),
                gf=gf.astype(BF16), gnyq=gnyq.astype(BF16),
                ginv=jnp.transpose(gf, (0, 2, 1)).astype(BF16), m2=gnyq.T.astype(BF16))


def _fft_a_body(x_ref, f_ref, o_ref):
    a = jnp.dot(f_ref[...], x_ref[0].astype(BF16), preferred_element_type=F32)
    n1h = o_ref.shape[2]
    o_ref[0, 0] = a[:n1h].astype(BF16)
    o_ref[0, 1] = a[n1h:].astype(BF16)


def _fft_a(x3, fmat):
    bsz, r, nc = x3.shape
    n1 = fmat.shape[0]
    tn = min(8192, nc)
    return pl.pallas_call(
        _fft_a_body,
        grid=(bsz, nc // tn),
        in_specs=[pl.BlockSpec((1, r, tn), lambda b, j: (b, 0, j)),
                  pl.BlockSpec((n1, r), lambda b, j: (0, 0))],
        out_specs=pl.BlockSpec((1, 2, n1 // 2, tn), lambda b, j: (b, 0, 0, j)),
        out_shape=jax.ShapeDtypeStruct((bsz, 2, n1 // 2, nc), BF16),
        compiler_params=_params(("parallel", "parallel"), 48),
        name="fft_stage_a",
    )(x3, fmat)


def _fft_b_body(ar_ref, ai_ref, g_ref, gn_ref, sc_ref, x_ref, xn_ref, *, kb, n2):
    j = pl.program_id(1)
    c = sc_ref.shape[1]
    inv = 1.0 / sc_ref[...]
    a_re = ar_ref[0, 0].reshape(kb, n2, c)
    a_im = ai_ref[0, 0].reshape(kb, n2, c)
    for kk in range(kb):
        rhs = jnp.concatenate([a_re[kk], a_im[kk]], axis=0)
        x_ref[0, kk] = (jnp.dot(g_ref[kk], rhs, preferred_element_type=F32) * inv).astype(BF16)

    @pl.when(j == 0)
    def _():
        rhs = jnp.concatenate([a_re[0], a_im[0]], axis=0)
        xn_ref[0] = (jnp.dot(gn_ref[...], rhs, preferred_element_type=F32) * inv).astype(BF16)


def _fft_b(a4, tabs, scale):
    bsz, _, n1h, nc = a4.shape
    c = scale.shape[1]
    n2 = nc // c
    kb = 16
    return pl.pallas_call(
        functools.partial(_fft_b_body, kb=kb, n2=n2),
        grid=(bsz, n1h // kb),
        in_specs=[pl.BlockSpec((1, 1, kb, nc), lambda b, j: (b, 0, j, 0)),
                  pl.BlockSpec((1, 1, kb, nc), lambda b, j: (b, 1, j, 0)),
                  pl.BlockSpec((kb, 2 * n2, 2 * n2), lambda b, j: (j, 0, 0)),
                  pl.BlockSpec((2 * n2, 2 * n2), lambda b, j: (0, 0)),
                  pl.BlockSpec((1, c), lambda b, j: (0, 0))],
        out_specs=[pl.BlockSpec((1, kb, 2 * n2, c), lambda b, j: (b, j, 0, 0)),
                   pl.BlockSpec((1, 2 * n2, c), lambda b, j: (b, 0, 0))],
        out_shape=[jax.ShapeDtypeStruct((bsz, n1h, 2 * n2, c), BF16),
                   jax.ShapeDtypeStruct((bsz, 2 * n2, c), BF16)],
        compiler_params=_params(("parallel", "arbitrary"), 48),
        name="fft_stage_b",
    )(a4, a4, tabs["gf"], tabs["gnyq"], scale)


def _cmul(x, k, n2):
    xr, xi = x[:n2], x[n2:]
    kr, ki = k[:n2], k[n2:]
    return jnp.concatenate([xr * kr - xi * ki, xr * ki + xi * kr], axis=0).astype(BF16)


def _spec_body(ar_ref, ai_ref, k_ref, kn_ref, g_ref, gn_ref, gi_ref, m2_ref, o_ref, *, kb, n2):
    j = pl.program_id(1)
    c = ar_ref.shape[3] // n2
    a_re = ar_ref[0, 0].reshape(kb, n2, c)
    a_im = ai_ref[0, 0].reshape(kb, n2, c)

    def through(fwd, rhs, kf, inv):
        x = jnp.dot(fwd, rhs, preferred_element_type=F32)
        return jnp.dot(inv, _cmul(x, kf.astype(F32), n2), preferred_element_type=F32)

    re_rows, im_rows = [], []
    for k0 in range(0, kb, 2):
        pair = (k0, k0 + 1)
        rhs = [jnp.concatenate([a_re[kk], a_im[kk]], axis=0) for kk in pair]
        xs = [jnp.dot(g_ref[kk], r, preferred_element_type=F32) for kk, r in zip(pair, rhs)]
        ys = [_cmul(x, k_ref[0, kk].astype(F32), n2) for kk, x in zip(pair, xs)]
        accs = [jnp.dot(gi_ref[kk], y, preferred_element_type=F32) for kk, y in zip(pair, ys)]
        if k0 == 0:
            nyq = through(gn_ref[...], rhs[0], kn_ref[0], m2_ref[...])
            accs[0] = jnp.where(j == 0, jnp.concatenate([accs[0][:n2], nyq[n2:]], axis=0), accs[0])
        for acc in accs:
            re_rows.append(acc[:n2].astype(BF16))
            im_rows.append(acc[n2:].astype(BF16))
    o_ref[0, 0] = jnp.stack(re_rows, axis=0).reshape(kb, n2 * c)
    o_ref[0, 1] = jnp.stack(im_rows, axis=0).reshape(kb, n2 * c)


def _spec_conv(a4, kf, kfnyq, order, tabs):
    bsz, _, n1h, nc = a4.shape
    tn2 = kf.shape[2]
    n2 = tn2 // 2
    c = nc // n2
    kb = 16
    return pl.pallas_call(
        functools.partial(_spec_body, kb=kb, n2=n2),
        grid=(bsz, n1h // kb),
        in_specs=[pl.BlockSpec((1, 1, kb, nc), lambda b, j: (b, 0, j, 0)),
                  pl.BlockSpec((1, 1, kb, nc), lambda b, j: (b, 1, j, 0)),
                  pl.BlockSpec((1, kb, tn2, c), lambda b, j: (0, j, 0, order)),
                  pl.BlockSpec((1, tn2, c), lambda b, j: (0, 0, order)),
                  pl.BlockSpec((kb, tn2, tn2), lambda b, j: (j, 0, 0)),
                  pl.BlockSpec((tn2, tn2), lambda b, j: (0, 0)),
                  pl.BlockSpec((kb, tn2, tn2), lambda b, j: (j, 0, 0)),
                  pl.BlockSpec((tn2, tn2), lambda b, j: (0, 0))],
        out_specs=pl.BlockSpec((1, 2, kb, nc), lambda b, j: (b, 0, j, 0)),
        out_shape=jax.ShapeDtypeStruct((bsz, 2, n1h, nc), BF16),
        compiler_params=_params(("parallel", "arbitrary"), 48),
        name="spectral_conv",
    )(a4, a4, kf, kfnyq, tabs["gf"], tabs["gnyq"], tabs["ginv"], tabs["m2"])


def _ifft_a_body(b_ref, f_ref, z_ref, gate_ref, bias_ref, o_ref):
    y = jnp.dot(f_ref[...], b_ref[0], preferred_element_type=F32)
    res = gate_ref[0] * (y + z_ref[0] * bias_ref[...])
    o_ref[0] = res.reshape(o_ref.shape[1:])


def _ifft_a(b3, fi, z3, gate3, bias_t, time_major):
    bsz, n1, nc = b3.shape
    n1h = n1 // 2
    tn = bias_t.shape[1]
    c = D_HYENA
    if time_major:
        out_spec = pl.BlockSpec((1, n1h, tn // c, c), lambda b, j: (b, 0, j, 0))
        out_shape = jax.ShapeDtypeStruct((bsz, n1h, nc // c, c), F32)
    else:
        out_spec = pl.BlockSpec((1, n1h, tn), lambda b, j: (b, 0, j))
        out_shape = jax.ShapeDtypeStruct((bsz, n1h, nc), F32)
    return pl.pallas_call(
        _ifft_a_body,
        grid=(bsz, nc // tn),
        in_specs=[pl.BlockSpec((1, n1, tn), lambda b, j: (b, 0, j)),
                  pl.BlockSpec((n1h, n1), lambda b, j: (0, 0)),
                  pl.BlockSpec((1, n1h, tn), lambda b, j: (b, 0, j)),
                  pl.BlockSpec((1, n1h, tn), lambda b, j: (b, 0, j)),
                  pl.BlockSpec((1, tn), lambda b, j: (0, 0))],
        out_specs=out_spec,
        out_shape=out_shape,
        compiler_params=_params(("parallel", "parallel"), 48),
        name="ifft_stage_a",
    )(b3, fi, z3, gate3, bias_t)


def _hyena(uh, conv_w, conv_b, fw, bias):
    bsz, seq, _ = uh.shape
    c = D_HYENA
    n1, n2 = _fft_dims(seq)
    n1h = n1 // 2
    tabs = _dft_tables(n1, n2)
    v, x1, x2 = _sconv(uh, conv_w, conv_b, n2)

    kcirc, ksum = _filters(seq, n1, n2, fw)
    cols = HYENA_ORDER * c
    ka = _fft_a(kcirc[None], tabs["fa_full"])
    kf, kfnyq = _fft_b(ka, tabs, ksum)

    tn = min(8192, n2 * c)
    z = v
    for order, gate in enumerate((x1, x2)):
        za = _fft_a(z, tabs["fa_half"])
        zb = _spec_conv(za, kf, kfnyq, order, tabs)
        bias_t = jnp.tile(bias[order][None, :], (1, tn // c))
        z = _ifft_a(zb.reshape(bsz, n1, n2 * c), tabs["fi"], z, gate, bias_t,
                    time_major=order == HYENA_ORDER - 1)
    return z.reshape(bsz, seq, c)


def _qkv_body(cq_ref, ckvr_ref, cos_ref, sin_ref, wq_ref, wqr_ref, wk_ref, wv_ref, qa_ref, kva_ref, qg_ref,
              kg_ref, qt_ref, k_ref, vt_ref):
    cqn = _rms(cq_ref[0]) * qa_ref[...]
    qs = jnp.dot(cqn.astype(BF16), wq_ref[...], preferred_element_type=F32)
    ck = ckvr_ref[0]
    ckvn = (_rms(ck[:, :KV_LORA]) * kva_ref[...]).astype(BF16)
    ks = jnp.dot(ckvn, wk_ref[...], preferred_element_type=F32)
    vs = jnp.dot(ckvn, wv_ref[...], preferred_element_type=F32)
    krope = pltpu.roll(ck[:, KV_LORA:], QK_NOPE, axis=1)
    cos = cos_ref[...]
    sin = sin_ref[...]
    lane = lax.broadcasted_iota(I32, cos.shape, 1)
    half = QK_ROPE // 2
    first = (lane >= QK_NOPE) & (lane < QK_NOPE + half)
    second = (lane >= QK_NOPE + half) & (lane < QK_HEAD)

    def head(xh, xh_rot, gain_cos, gain_sin):
        rs = lax.rsqrt(jnp.sum(xh * xh, axis=-1, keepdims=True) * (1.0 / QK_HEAD) + EPS)
        return rs * (xh * gain_cos + xh_rot * gain_sin)

    q_scale = QK_HEAD ** -0.5 * math.log2(math.e)
    q_cos, q_sin = qg_ref[0:1, :] * cos * q_scale, qg_ref[1:2, :] * sin * q_scale
    k_cos, k_sin = kg_ref[0:1, :] * cos, kg_ref[1:2, :] * sin
    qs_rot = jnp.dot(cqn.astype(BF16), wqr_ref[...], preferred_element_type=F32)
    krope_rot = jnp.where(first, -pltpu.roll(krope, SLOT - half, axis=1),
                          jnp.where(second, pltpu.roll(krope, half, axis=1), 0.0))
    eye = (lax.broadcasted_iota(I32, (SLOT, SLOT), 0) == lax.broadcasted_iota(I32, (SLOT, SLOT), 1)).astype(BF16)
    transpose = lambda x: lax.dot_general(eye, x.astype(BF16), (((1,), (1,)), ((), ())),
                                          preferred_element_type=F32).astype(BF16)
    for h in range(N_HEADS):
        sl = slice(h * SLOT, (h + 1) * SLOT)
        qt_ref[0, h] = transpose(head(qs[:, sl], qs_rot[:, sl], q_cos, q_sin))
        k_ref[0, h] = head(ks[:, sl] + krope, krope_rot, k_cos, k_sin).astype(BF16)
    aug = lax.broadcasted_iota(I32, (V_AUG - V_HEAD, cos.shape[0]), 0)
    ones_row = jnp.where(aug == 0, 1.0, 0.0).astype(BF16)
    for hp in range(N_HEADS // 2):
        pair_t = transpose(vs[:, hp * 2 * V_HEAD:(hp + 1) * 2 * V_HEAD])
        vt_ref[0, hp] = jnp.concatenate([pair_t[:V_HEAD], ones_row, pair_t[V_HEAD:], ones_row], axis=0)


def _qkv(cq, ckvr, cos_t, sin_t, aw):
    bsz, seq, _ = cq.shape
    tm = min(512, seq)
    const = lambda shape: pl.BlockSpec(shape, lambda b, i: (0,) * len(shape))
    return pl.pallas_call(
        _qkv_body,
        grid=(bsz, seq // tm),
        in_specs=[pl.BlockSpec((1, tm, Q_LORA), lambda b, i: (b, i, 0)),
                  pl.BlockSpec((1, tm, D_IN_PAD - COL_Q), lambda b, i: (b, i, 0)),
                  pl.BlockSpec((tm, SLOT), lambda b, i: (i, 0)),
                  pl.BlockSpec((tm, SLOT), lambda b, i: (i, 0)),
                  const((Q_LORA, N_HEADS * SLOT)), const((Q_LORA, N_HEADS * SLOT)),
                  const((KV_LORA, N_HEADS * SLOT)), const((KV_LORA, D_ATTN)),
                  const((1, Q_LORA)), const((1, KV_LORA)), const((2, SLOT)), const((2, SLOT))],
        out_specs=[pl.BlockSpec((1, N_HEADS, SLOT, tm), lambda b, i: (b, 0, 0, i)),
                   pl.BlockSpec((1, N_HEADS, tm, SLOT), lambda b, i: (b, 0, i, 0)),
                   pl.BlockSpec((1, N_HEADS // 2, 2 * V_AUG, tm), lambda b, i: (b, 0, 0, i))],
        out_shape=[jax.ShapeDtypeStruct((bsz, N_HEADS, SLOT, seq), BF16),
                   jax.ShapeDtypeStruct((bsz, N_HEADS, seq, SLOT), BF16),
                   jax.ShapeDtypeStruct((bsz, N_HEADS // 2, 2 * V_AUG, seq), BF16)],
        compiler_params=_params(("parallel", "parallel"), 40),
        name="qkv_prep",
    )(cq, ckvr, cos_t, sin_t, aw["wq"], aw["wq_rot"], aw["wk"], aw["wv"], aw["qa"], aw["kva"], aw["qg"],
      aw["kg"])


def _flash_body(qt_ref, k_ref, vt_ref, o_ref, s0_ref, s1_ref, acc_ref, *, tk, nk, nqc, unroll):
    chains = [(hh, qc) for hh in range(2) for qc in range(nqc)]
    acc_ref[...] = jnp.zeros(acc_ref.shape, F32)

    def qk(t, s_ref, only=None):
        ks = pl.multiple_of(t * tk, tk)
        tile_max = []
        for c, (hh, qc) in enumerate(chains):
            if only is not None and c != only:
                continue
            k = k_ref[0, hh, pl.ds(ks, tk), :]
            s = jnp.dot(k, qt_ref[0, hh, :, qc * ATT_QC:(qc + 1) * ATT_QC], preferred_element_type=F32)
            s_ref[c] = s
            tile_max.append(jnp.max(s, axis=0, keepdims=True))
        return tile_max

    def softmax_pv(t, s_ref, tile_max, m, only=None):
        ks = pl.multiple_of(t * tk, tk)
        m_out = []
        for c, (hh, qc) in enumerate(chains):
            if only is not None and c != only:
                continue
            i = 0 if only is not None else c
            m_new = jnp.maximum(m[i], tile_max[i])
            a = jnp.exp2(m[i] - m_new)
            p = jnp.exp2(s_ref[c] - m_new)
            m_out.append(m_new)
            vt = vt_ref[0, 0, hh * V_AUG:(hh + 1) * V_AUG, pl.ds(ks, tk)]
            acc_ref[c] = acc_ref[c] * a + jnp.dot(vt, p.astype(BF16), preferred_element_type=F32)
        return m_out

    def group(u, carry):
        tile_max, m = carry
        for i in range(0, unroll, 2):
            t = unroll * u + i
            tm1, tm0 = [], []
            m = list(m)
            for c in range(len(chains)):
                tm1 += qk(t + 1, s1_ref, only=c)
                m[c] = softmax_pv(t, s0_ref, [tile_max[c]], [m[c]], only=c)[0]
            for c in range(len(chains)):
                tm0 += qk(jnp.minimum(t + 2, nk - 1), s0_ref, only=c)
                m[c] = softmax_pv(t + 1, s1_ref, [tm1[c]], [m[c]], only=c)[0]
            tile_max = tm0
        return tile_max, m

    init = (qk(0, s0_ref), [jnp.full((1, ATT_QC), -jnp.inf, F32)] * len(chains))
    lax.fori_loop(0, nk // unroll, group, init)

    def normalised(c):
        acc = acc_ref[c]
        return acc[:V_HEAD] * (1.0 / acc[V_HEAD:V_HEAD + 1])

    heads = [jnp.concatenate([normalised(hh * nqc + qc) for qc in range(nqc)], axis=1) for hh in range(2)]
    o_ref[0] = jnp.concatenate(heads, axis=0).T


def _flash(qt, k, vt):
    bsz, nh, _, seq = qt.shape
    tq = min(ATT_TQ, seq)
    tk = min(ATT_TK, seq)
    nqc = tq // ATT_QC
    nk = seq // tk
    unroll = min(ATT_UNROLL, nk)
    assert nk % unroll == 0 and unroll % 2 == 0
    return pl.pallas_call(
        functools.partial(_flash_body, tk=tk, nk=nk, nqc=nqc, unroll=unroll),
        grid=(bsz, nh // 2, seq // tq),
        in_specs=[pl.BlockSpec((1, 2, SLOT, tq), lambda b, hp, i: (b, hp, 0, i)),
                  pl.BlockSpec((1, 2, seq, SLOT), lambda b, hp, i: (b, hp, 0, 0)),
                  pl.BlockSpec((1, 1, 2 * V_AUG, seq), lambda b, hp, i: (b, hp, 0, 0))],
        out_specs=pl.BlockSpec((1, tq, 2 * V_HEAD), lambda b, hp, i: (b, i, hp)),
        out_shape=jax.ShapeDtypeStruct((bsz, seq, nh * V_HEAD), F32),
        scratch_shapes=[pltpu.VMEM((2 * nqc, tk, ATT_QC), F32), pltpu.VMEM((2 * nqc, tk, ATT_QC), F32),
                        pltpu.VMEM((2 * nqc, V_AUG, ATT_QC), F32)],
        compiler_params=_params(("parallel", "parallel", "parallel"), 48),
        name="flash_attn",
    )(qt, k, vt)


def _outproj_body(yh_ref, ya_ref, x_ref, og_ref, w_ref, mg_ref, wr_ref, xm_ref, xn_ref, aff_ref, affn_ref):
    og = og_ref[...]
    half = D_MODEL // 2
    tm = x_ref.shape[0]
    chunks = [slice(r, r + tm // OUTPROJ_CHUNKS) for r in range(0, tm, tm // OUTPROJ_CHUNKS)]
    y = [jnp.concatenate([_rms(yh_ref[rows, :]) * og[:, :half], _rms(ya_ref[rows, :]) * og[:, half:]],
                         axis=1).astype(BF16) for rows in chunks]
    xm = [x_ref[rows, :] + jnp.dot(yc, w_ref[...], preferred_element_type=F32) for rows, yc in zip(chunks, y)]
    xn = [_rms(xc) * mg_ref[...] for xc in xm]
    xn_hi = [xc.astype(BF16) for xc in xn]
    xn_lo = [(xc - hc.astype(F32)).astype(BF16) for xc, hc in zip(xn, xn_hi)]
    logits = [jnp.dot(hc, wr_ref[0], preferred_element_type=F32)
              + jnp.dot(lc, wr_ref[0], preferred_element_type=F32)
              + jnp.dot(hc, wr_ref[1], preferred_element_type=F32) for hc, lc in zip(xn_hi, xn_lo)]
    lane = lax.broadcasted_iota(I32, logits[0].shape, 1)
    for rows, xc, hc, lg in zip(chunks, xm, xn_hi, logits):
        xm_ref[rows, :] = xc
        xn_ref[rows, :] = hc
        lg = jnp.where(lane < N_EXPERTS, lg, -jnp.inf)
        e = jnp.exp(lg - jnp.max(lg, axis=-1, keepdims=True))
        aff = e / jnp.sum(e, axis=-1, keepdims=True)
        aff_ref[:, rows] = aff.T[:N_EXPERTS]
        affn_ref[rows, :] = aff[:, :N_EXPERTS]


def _outproj(yh, ya, x2d, og, w_out, mg, wr_pad):
    n = x2d.shape[0]
    tm = min(512, n)
    half = D_MODEL // 2
    const = lambda shape: pl.BlockSpec(shape, lambda i: (0,) * len(shape))
    return pl.pallas_call(
        _outproj_body,
        grid=(n // tm,),
        in_specs=[pl.BlockSpec((tm, half), lambda i: (i, 0)),
                  pl.BlockSpec((tm, half), lambda i: (i, 0)),
                  pl.BlockSpec((tm, D_MODEL), lambda i: (i, 0)),
                  const((1, D_MODEL)), const((D_MODEL, D_MODEL)), const((1, D_MODEL)),
                  const((2, D_MODEL, LANES))],
        out_specs=[pl.BlockSpec((tm, D_MODEL), lambda i: (i, 0)),
                   pl.BlockSpec((tm, D_MODEL), lambda i: (i, 0)),
                   pl.BlockSpec((N_EXPERTS, tm), lambda i: (0, i)),
                   pl.BlockSpec((tm, N_EXPERTS), lambda i: (i, 0))],
        out_shape=[jax.ShapeDtypeStruct((n, D_MODEL), F32),
                   jax.ShapeDtypeStruct((n, D_MODEL), BF16),
                   jax.ShapeDtypeStruct((N_EXPERTS, n), F32),
                   jax.ShapeDtypeStruct((n, N_EXPERTS), F32)],
        compiler_params=_params(("parallel",), 40),
        name="outproj_router",
    )(yh, ya, x2d, og, w_out, mg, wr_pad)


def _select_body(aff_ref, upper_ref, lower_ref, pos_ref, off_ref, *, cap):
    group = aff_ref.shape[0]
    bits = [pltpu.bitcast(aff_ref[x], I32) for x in range(group)]
    upper = upper_ref[...]
    lower = lower_ref[...]

    def count(mask):
        return jnp.sum(jnp.sum(mask.astype(F32), axis=1, keepdims=True), axis=0, keepdims=True)

    def bit_step(i, thr):
        bit = jnp.left_shift(jnp.int32(1), 30 - i)
        return tuple(jnp.where(count(bits[x] >= (thr[x] | bit)) >= cap, thr[x] | bit, thr[x])
                     for x in range(group))

    thr = lax.fori_loop(0, 31, bit_step, tuple(jnp.zeros((1, 1), I32) for _ in range(group)))

    def prefix(mask):
        within = jnp.dot(mask.astype(BF16), upper, preferred_element_type=F32)
        total = within[:, LANES - 1:LANES]
        offs = jnp.dot(lower, jnp.broadcast_to(total, within.shape).astype(BF16), preferred_element_type=F32)
        return within, offs

    for x in range(group):
        gt = bits[x] > thr[x]
        eq = bits[x] == thr[x]
        need = cap - count(gt)
        w_eq, o_eq = prefix(eq)
        sel = gt | (eq & (o_eq + w_eq <= need))
        w_sel, o_sel = prefix(sel)
        pos_ref[x] = jnp.where(sel, o_sel + w_sel - 1.0, -1.0).astype(I32)
        off_ref[x] = o_sel.astype(I32)


def _select(aff_t, cap):
    ne, n = aff_t.shape
    rows = n // LANES
    upper = (jnp.arange(LANES)[:, None] <= jnp.arange(LANES)[None, :]).astype(BF16)
    lower = (jnp.arange(rows)[None, :] < jnp.arange(rows)[:, None]).astype(BF16)
    group = 4
    blk = pl.BlockSpec((group, rows, LANES), lambda e: (e, 0, 0))
    pos, off = pl.pallas_call(
        functools.partial(_select_body, cap=cap),
        grid=(ne // group,),
        in_specs=[blk, pl.BlockSpec((LANES, LANES), lambda e: (0, 0)),
                  pl.BlockSpec((rows, rows), lambda e: (0, 0))],
        out_specs=[blk, blk],
        out_shape=[jax.ShapeDtypeStruct((ne, rows, LANES), I32),
                   jax.ShapeDtypeStruct((ne, rows, LANES), I32)],
        compiler_params=_params(("parallel",), 32),
        name="ec_select",
    )(aff_t.reshape(ne, rows, LANES), upper, lower)
    return pos.reshape(ne, n), off[:, :, 0]


def _ffn_body(off_ref, nch_ref, pos_ref, x_ref, wg_ref, wu_ref, wd_ref, y_ref, xs_ref, *, nb, sub, cap):
    group = y_ref.shape[0]
    e0 = pl.program_id(0) * group
    sb = pl.program_id(1)

    @pl.when(sb == 0)
    def _():
        xs_ref[...] = jnp.zeros(xs_ref.shape, BF16)

    row = lax.broadcasted_iota(I32, (GATHER_ROWS, TOK_BLOCK), 0)

    def first_chunks(i):
        tok = slice(i * TOK_BLOCK, (i + 1) * TOK_BLOCK)
        starts = [pl.multiple_of(off_ref[(e0 + x) * nb + sb * sub + i] * ALIGN, ALIGN) for x in range(group)]
        hits = [(pos_ref[x, :, tok] - starts[x]) == row for x in range(group)]
        win = jnp.dot(jnp.concatenate(hits, axis=0).astype(BF16), x_ref[tok, :], preferred_element_type=F32)
        for x in range(group):
            xs_ref[x, pl.ds(starts[x], GATHER_ROWS), :] += win[x * GATHER_ROWS:(x + 1) * GATHER_ROWS].astype(BF16)

    def later_chunk(x, i, c):
        tok = slice(i * TOK_BLOCK, (i + 1) * TOK_BLOCK)
        start = pl.multiple_of(off_ref[(e0 + x) * nb + sb * sub + i] * ALIGN + c * GATHER_ROWS, ALIGN)
        hit = (pos_ref[x, :, tok] - start) == row
        win = jnp.dot(hit.astype(BF16), x_ref[tok, :], preferred_element_type=F32)
        xs_ref[x, pl.ds(start, GATHER_ROWS), :] += win.astype(BF16)

    for i in range(sub):
        first_chunks(i)
    for x in range(group):
        for i in range(sub):
            lax.fori_loop(1, nch_ref[(e0 + x) * nb + sb * sub + i],
                          lambda c, carry, x=x, i=i: (later_chunk(x, i, c), carry)[1], 0)

    @pl.when(sb == pl.num_programs(1) - 1)
    def _():
        for x in range(group):
            for j in range(cap // SLOT_TILE):
                rows = slice(j * SLOT_TILE, (j + 1) * SLOT_TILE)
                xt = xs_ref[x, rows, :]
                g = jnp.dot(xt, wg_ref[x], preferred_element_type=F32)
                u = jnp.dot(xt, wu_ref[x], preferred_element_type=F32)
                h = (g * jax.nn.sigmoid(g) * u).astype(BF16)
                y_ref[x, rows, :] = jnp.dot(h, wd_ref[x], preferred_element_type=F32).astype(BF16)
            y_ref[x, cap:, :] = jnp.zeros((y_ref.shape[1] - cap, D_MODEL), BF16)


def _ffn(base, nchunk, pos3, xn, wg, wu, wd, cap):
    ne = wg.shape[0]
    n = xn.shape[0]
    nb = n // TOK_BLOCK
    sub = min(4, nb)
    group = 2
    cap_pad = cap + TOK_BLOCK + ALIGN
    grid_spec = pltpu.PrefetchScalarGridSpec(
        num_scalar_prefetch=2,
        grid=(ne // group, nb // sub),
        in_specs=[pl.BlockSpec((group, 1, sub * TOK_BLOCK), lambda e, s, off, nch: (e, 0, s)),
                  pl.BlockSpec((sub * TOK_BLOCK, D_MODEL), lambda e, s, off, nch: (s, 0)),
                  pl.BlockSpec((group, D_MODEL, D_EXPERT), lambda e, s, off, nch: (e, 0, 0)),
                  pl.BlockSpec((group, D_MODEL, D_EXPERT), lambda e, s, off, nch: (e, 0, 0)),
                  pl.BlockSpec((group, D_EXPERT, D_MODEL), lambda e, s, off, nch: (e, 0, 0))],
        out_specs=pl.BlockSpec((group, cap_pad, D_MODEL), lambda e, s, off, nch: (e, 0, 0)),
        scratch_shapes=[pltpu.VMEM((group, cap_pad, D_MODEL), BF16)])
    return pl.pallas_call(
        functools.partial(_ffn_body, nb=nb, sub=sub, cap=cap),
        grid_spec=grid_spec,
        out_shape=jax.ShapeDtypeStruct((ne, cap_pad, D_MODEL), BF16),
        compiler_params=_params(("parallel", "arbitrary"), 52),
        name="ec_ffn",
    )(base, nchunk, pos3, xn, wg, wu, wd)


def _combine_body(off_ref, spill_ref, post_ref, affn_ref, xm_ref, *refs, nb, ne, win_rows):
    win_refs, tail_refs, o_ref = refs[:ne], refs[ne:-1], refs[-1]
    b = pl.program_id(0)
    post = post_ref[...]
    affn = affn_ref[...]

    def expand(y_refs, shift):
        width = y_refs[0].shape[0]
        lane = lax.broadcasted_iota(I32, (TOK_BLOCK, width), 1)
        total = jnp.zeros((TOK_BLOCK, D_MODEL), F32)
        for e in range(ne):
            rel = post[:, e:e + 1] - (off_ref[e * nb + b] * ALIGN + shift)
            hit = (rel == lane).astype(BF16)
            total = total + affn[:, e:e + 1] * jnp.dot(hit, y_refs[e][...], preferred_element_type=F32)
        return total

    def expand_pairs(y_refs):
        width = y_refs[0].shape[0]
        lane = lax.broadcasted_iota(I32, (TOK_BLOCK, 2 * width), 1)
        total = jnp.zeros((TOK_BLOCK, D_MODEL), F32)
        for e in range(0, ne, 2):
            rel0 = post[:, e:e + 1] - off_ref[e * nb + b] * ALIGN
            rel1 = post[:, e + 1:e + 2] - off_ref[(e + 1) * nb + b] * ALIGN + width
            gates = (jnp.where(rel0 == lane, affn[:, e:e + 1], 0.0)
                     + jnp.where((rel1 == lane) & (rel1 >= width), affn[:, e + 1:e + 2], 0.0))
            rows = jnp.concatenate([y_refs[e][...], y_refs[e + 1][...]], axis=0)
            total = total + jnp.dot(gates.astype(BF16), rows, preferred_element_type=F32)
        return total

    if not tail_refs and 2 * win_rows <= TOK_BLOCK and ne % 2 == 0:
        o_ref[...] = xm_ref[...] + expand_pairs(win_refs)
    else:
        o_ref[...] = xm_ref[...] + expand(win_refs, 0)

    if tail_refs:
        @pl.when(spill_ref[b] != 0)
        def _():
            o_ref[...] += expand(tail_refs, win_rows)


def _combine(base, spill, pos_t, aff_n, xm, y, win_rows):
    ne = y.shape[0]
    n = xm.shape[0]
    nb = n // TOK_BLOCK
    tail_rows = ALIGN if win_rows == TOK_BLOCK else 0

    def window(e, rows, shift):
        return pl.BlockSpec((pl.Squeezed(), pl.Element(rows), pl.Element(D_MODEL)),
                            lambda b, off, sp: (e, (off[e * nb + b] + shift // ALIGN) * ALIGN, 0))

    tails = [window(e, tail_rows, win_rows) for e in range(ne)] if tail_rows else []
    grid_spec = pltpu.PrefetchScalarGridSpec(
        num_scalar_prefetch=2,
        grid=(nb,),
        in_specs=([pl.BlockSpec((TOK_BLOCK, ne), lambda b, off, sp: (b, 0)),
                   pl.BlockSpec((TOK_BLOCK, ne), lambda b, off, sp: (b, 0)),
                   pl.BlockSpec((TOK_BLOCK, D_MODEL), lambda b, off, sp: (b, 0))]
                  + [window(e, win_rows, 0) for e in range(ne)] + tails),
        out_specs=pl.BlockSpec((TOK_BLOCK, D_MODEL), lambda b, off, sp: (b, 0)))
    return pl.pallas_call(
        functools.partial(_combine_body, nb=nb, ne=ne, win_rows=win_rows),
        grid_spec=grid_spec,
        out_shape=jax.ShapeDtypeStruct((n, D_MODEL), F32),
        compiler_params=_params(("parallel",), 48),
        name="ec_combine",
    )(base, spill, pos_t, aff_n, xm, *([y] * (ne + len(tails))))


def _ec_moe(xm, xn, aff_t, aff_n, wg, wu, wd):
    ne, n = aff_t.shape
    cap = max(1, EC_CAPACITY * n // N_EXPERTS)
    nb = n // TOK_BLOCK
    pos, off = _select(aff_t, cap)
    first = off[:, ::TOK_BLOCK // LANES]
    count = jnp.concatenate([first[:, 1:], jnp.full((ne, 1), cap, I32)], axis=1) - first
    base = first // ALIGN
    span = jnp.where(count > 0, first - base * ALIGN + count, 0)
    spill = jnp.any(span > TOK_BLOCK, axis=0).astype(I32)
    nchunk = ((span + GATHER_ROWS - 1) // GATHER_ROWS).reshape(ne * nb).astype(I32)
    base = base.reshape(ne * nb).astype(I32)
    y = _ffn(base, nchunk, pos.reshape(ne, 1, n), xn, wg, wu, wd, cap)
    pos_t = pos.T
    return lax.cond(jnp.max(span) <= COMBINE_FAST_ROWS,
                    lambda: _combine(base, spill, pos_t, aff_n, xm, y, COMBINE_FAST_ROWS),
                    lambda: _combine(base, spill, pos_t, aff_n, xm, y, TOK_BLOCK))


def _slot_cols(w, head_width):
    k = w.shape[0]
    w3 = w.reshape(k, N_HEADS, head_width)
    return jnp.pad(w3, ((0, 0), (0, 0), (0, SLOT - head_width))).reshape(k, N_HEADS * SLOT)


def _attn_weights(q_a_norm, w_uq, kv_a_norm, w_ukv, q_norm, k_norm):
    wkv = w_ukv.reshape(KV_LORA, N_HEADS, QK_NOPE + V_HEAD)
    wk = _slot_cols(wkv[:, :, :QK_NOPE].reshape(KV_LORA, -1), QK_NOPE)
    wv = wkv[:, :, QK_NOPE:].reshape(KV_LORA, D_ATTN)
    half = QK_ROPE // 2

    def rotate_half(a):
        return jnp.concatenate([jnp.zeros_like(a[..., :QK_NOPE]), -a[..., QK_NOPE + half:],
                                a[..., QK_NOPE:QK_NOPE + half]], axis=-1)

    def gains(g):
        swapped = jnp.concatenate([jnp.zeros_like(g[:QK_NOPE]), g[QK_NOPE + half:], g[QK_NOPE:QK_NOPE + half]])
        return jnp.pad(jnp.stack([g, swapped]), ((0, 0), (0, SLOT - QK_HEAD)))

    wq3 = w_uq.reshape(Q_LORA, N_HEADS, QK_HEAD)
    return dict(wq=_slot_cols(w_uq, QK_HEAD).astype(BF16),
                wq_rot=_slot_cols(rotate_half(wq3).reshape(Q_LORA, -1), QK_HEAD).astype(BF16),
                wk=wk.astype(BF16), wv=wv.astype(BF16), qa=q_a_norm[None, :], kva=kv_a_norm[None, :],
                qg=gains(q_norm), kg=gains(k_norm))


def _rope_tables(seq):
    pos = jnp.arange(seq, dtype=F32)
    inv_freq = ROPE_THETA ** (-jnp.arange(0, QK_ROPE, 2, dtype=F32) / QK_ROPE)
    ang = pos[:, None] * inv_freq
    ang = jnp.concatenate([ang, ang], axis=-1)
    pad = lambda t, fill: jnp.concatenate(
        [jnp.full((seq, QK_NOPE), fill, F32), t, jnp.full((seq, SLOT - QK_HEAD), fill, F32)], axis=1)
    return pad(jnp.cos(ang), 1.0), pad(jnp.sin(ang), 0.0)


def _hi_lo(w):
    hi = w.astype(BF16)
    return jnp.stack([hi, (w - hi.astype(F32)).astype(BF16)], axis=1)


def _filter_weights(w1, b1, w2, b2, w3, freq, decay):
    hid, nbands = FILTER_HIDDEN, FILTER_BANDS
    assert 2 * hid == LANES and 2 * nbands <= hid
    both = lambda row: jnp.concatenate([row, row])[None, :]
    blockdiag = lambda a: jnp.concatenate(
        [jnp.concatenate([a, jnp.zeros_like(a)], axis=1), jnp.concatenate([jnp.zeros_like(a), a], axis=1)], axis=0)
    bands = jnp.linspace(1e-4, nbands - 1, nbands, dtype=F32)
    rest = jnp.zeros((hid - 2 * nbands,), F32)
    band_half = jnp.concatenate([bands, bands, rest])
    phase_half = jnp.concatenate([jnp.zeros((nbands,), F32), jnp.full((nbands,), 0.5 * math.pi, F32), rest])
    w1_half = jnp.concatenate([-w1[1 + nbands:], w1[1:1 + nbands], jnp.zeros((hid - 2 * nbands, hid), F32)], axis=0)
    cols = HYENA_ORDER * D_HYENA
    w3d = jnp.transpose(w3.reshape(hid, N_DIR, cols), (1, 0, 2))
    w3p = jnp.stack([jnp.pad(w3d[0], ((0, hid), (0, 0))), jnp.pad(w3d[1], ((hid, 0), (0, 0)))], axis=0)
    w3_hi = w3p.astype(BF16)
    w3_lo = (w3p - w3_hi.astype(F32)).astype(BF16)
    return dict(bands=both(band_half), phase=both(phase_half), w1t=both(w1[0]), w1=blockdiag(w1_half),
                b1=both(b1), w2=blockdiag(w2), b2=both(b2), freq=both(freq), w3_hi=w3_hi, w3_lo=w3_lo,
                decay=decay.reshape(N_DIR, 1, cols))


def _trunk(x, p):
    bsz, seq, _ = x.shape
    n = bsz * seq
    cos_t, sin_t = _rope_tables(seq)
    x2 = x.reshape(n, D_MODEL)
    for l in range(DEPTH):
        uh, cq, ckvr = _inproj(x2, p["attn_norm"][l][None, :], p["w_in"][l])
        y_h = _hyena(uh.reshape(bsz, seq, COL_HYENA), p["conv_w"][l], p["conv_b"][l][None, :],
                     p["filt"][l], p["hyena_bias"][l])
        qt, k, vt = _qkv(cq.reshape(bsz, seq, -1), ckvr.reshape(bsz, seq, -1), cos_t, sin_t, p["attn"][l])
        y_a = _flash(qt, k, vt)
        xm, xn, aff_t, aff_n = _outproj(y_h.reshape(n, D_HYENA), y_a.reshape(n, D_ATTN), x2,
                                 p["out_norm"][l][None, :], p["w_out"][l], p["mlp_norm"][l][None, :],
                                 p["w_router"][l])
        x2 = _ec_moe(xm, xn, aff_t, aff_n, p["w_gate"][l], p["w_up"][l], p["w_down"][l])
    return x2.reshape(bsz, seq, D_MODEL)


def kernel(x_prompt, x_sample, attn_norm, w_in, conv_w, conv_b, filt_w1, filt_b1, filt_w2, filt_b2, filt_w3,
           filt_freq, filt_decay, hyena_bias, q_a_norm, w_uq, kv_a_norm, w_ukv, q_norm, k_norm, out_norm,
           w_out, mlp_norm, w_router, w_gate, w_up, w_down):
    p = dict(
        attn_norm=attn_norm,
        w_in=jnp.pad(w_in, ((0, 0), (0, 0), (0, D_IN_PAD - D_IN))).astype(BF16),
        conv_w=conv_w, conv_b=conv_b, hyena_bias=hyena_bias,
        filt=[_filter_weights(filt_w1[l], filt_b1[l], filt_w2[l], filt_b2[l], filt_w3[l], filt_freq[l],
                              filt_decay[l]) for l in range(DEPTH)],
        attn=[_attn_weights(q_a_norm[l], w_uq[l], kv_a_norm[l], w_ukv[l], q_norm[l], k_norm[l])
              for l in range(DEPTH)],
        out_norm=out_norm, w_out=w_out.astype(BF16), mlp_norm=mlp_norm,
        w_router=_hi_lo(jnp.pad(w_router, ((0, 0), (0, 0), (0, LANES - N_EXPERTS)))),
        w_gate=w_gate.astype(BF16), w_up=w_up.astype(BF16), w_down=w_down.astype(BF16))
    return (_trunk(x_prompt, p), _trunk(x_sample, p))
```

```python
import functools
import math

import jax
import jax.numpy as jnp
from jax import lax
from jax.experimental import pallas as pl
from jax.experimental.pallas import tpu as pltpu

F32 = jnp.float32
BF16 = jnp.bfloat16
I32 = jnp.int32
HIGHEST = lax.Precision.HIGHEST

D_MODEL = 1024
DEPTH = 2
D_HYENA = 512
HYENA_ORDER = 2
FILTER_BANDS = 16
FILTER_HIDDEN = 64
N_DIR = 2
DECAY_SHIFT = 0.05
N_HEADS = 8
QK_NOPE = 64
QK_ROPE = 32
QK_HEAD = QK_NOPE + QK_ROPE
V_HEAD = 64
V_AUG = V_HEAD + 16
D_ATTN = N_HEADS * V_HEAD
Q_LORA = 256
KV_LORA = 128
ROPE_THETA = 10000.0
N_EXPERTS = 16
EC_CAPACITY = 2
D_EXPERT = 512
EPS = 1e-6
COL_HYENA = (HYENA_ORDER + 1) * D_HYENA
COL_Q = COL_HYENA + Q_LORA
COL_KV = COL_Q + KV_LORA
D_IN = COL_KV + QK_ROPE
D_IN_PAD = 2048

LANES = 128
SLOT = 128
TOK_BLOCK = 256
SLOT_TILE = 256
ALIGN = 16
GATHER_ROWS = 64
COMBINE_FAST_ROWS = 96
OUTPROJ_CHUNKS = 2
ATT_TQ = 512
ATT_QC = 256
ATT_TK = 256
ATT_UNROLL = 16
MIB = 1024 * 1024


def _params(sem, vmem_mib):
    return pltpu.CompilerParams(dimension_semantics=sem, vmem_limit_bytes=vmem_mib * MIB)


def _rms(x):
    return x * lax.rsqrt(jnp.mean(x * x, axis=-1, keepdims=True) + EPS)


def _inproj_body(x_ref, g_ref, w_ref, uh_ref, cq_ref, ckvr_ref):
    xn = _rms(x_ref[...]) * g_ref[...]
    u = jnp.dot(xn.astype(BF16), w_ref[...], preferred_element_type=F32)
    uh_ref[...] = u[:, :COL_HYENA]
    cq_ref[...] = u[:, COL_HYENA:COL_Q]
    ckvr_ref[...] = u[:, COL_Q:]


def _inproj(x2d, g, w_pad):
    n = x2d.shape[0]
    tm = min(512, n)
    return pl.pallas_call(
        _inproj_body,
        grid=(n // tm,),
        in_specs=[pl.BlockSpec((tm, D_MODEL), lambda i: (i, 0)),
                  pl.BlockSpec((1, D_MODEL), lambda i: (0, 0)),
                  pl.BlockSpec((D_MODEL, D_IN_PAD), lambda i: (0, 0))],
        out_specs=[pl.BlockSpec((tm, COL_HYENA), lambda i: (i, 0)),
                   pl.BlockSpec((tm, Q_LORA), lambda i: (i, 0)),
                   pl.BlockSpec((tm, D_IN_PAD - COL_Q), lambda i: (i, 0))],
        out_shape=[jax.ShapeDtypeStruct((n, COL_HYENA), F32),
                   jax.ShapeDtypeStruct((n, Q_LORA), F32),
                   jax.ShapeDtypeStruct((n, D_IN_PAD - COL_Q), F32)],
        compiler_params=_params(("parallel",), 40),
        name="inproj",
    )(x2d, g, w_pad)


def _sconv_body(u_ref, prev_ref, next_ref, w_ref, b_ref, v_ref, x1_ref, x2_ref):
    i = pl.program_id(1)
    last = pl.num_programs(1) - 1
    u = u_ref[0]
    tl = u.shape[0]
    prev_row = jnp.where(i == 0, 0.0, prev_ref[0, 7:8, :])
    next_row = jnp.where(i == last, 0.0, next_ref[0, 0:1, :])
    row = lax.broadcasted_iota(I32, u.shape, 0)
    up = jnp.where(row == 0, prev_row, pltpu.roll(u, 1, axis=0))
    dn = jnp.where(row == tl - 1, next_row, pltpu.roll(u, tl - 1, axis=0))
    y = up * w_ref[0:1, :] + u * w_ref[1:2, :] + dn * w_ref[2:3, :] + b_ref[...]
    rows, flat = v_ref.shape[1:]
    for k, o_ref in enumerate((v_ref, x1_ref, x2_ref)):
        o_ref[0] = y[:, k * D_HYENA:(k + 1) * D_HYENA].reshape(rows, tl // rows, D_HYENA).reshape(rows, flat)


def _sconv(uh, w, b, n2):
    bsz, seq, c = uh.shape
    rows = 8
    tl = rows * n2
    r = tl // 8
    nblk8 = seq // 8
    out = jax.ShapeDtypeStruct((bsz, seq // n2, n2 * D_HYENA), F32)
    ospec = pl.BlockSpec((1, rows, n2 * D_HYENA), lambda bi, i: (bi, i, 0))
    return pl.pallas_call(
        _sconv_body,
        grid=(bsz, seq // tl),
        in_specs=[pl.BlockSpec((1, tl, c), lambda bi, i: (bi, i, 0)),
                  pl.BlockSpec((1, 8, c), lambda bi, i: (bi, jnp.maximum(i * r - 1, 0), 0)),
                  pl.BlockSpec((1, 8, c), lambda bi, i: (bi, jnp.minimum((i + 1) * r, nblk8 - 1), 0)),
                  pl.BlockSpec((3, c), lambda bi, i: (0, 0)),
                  pl.BlockSpec((1, c), lambda bi, i: (0, 0))],
        out_specs=[ospec, ospec, ospec],
        out_shape=[out, out, out],
        compiler_params=_params(("parallel", "parallel"), 40),
        name="sconv",
    )(uh, uh, uh, w, b)


def _filter_body(bands_ref, phase_ref, w1t_ref, w1_ref, b1_ref, w2_ref, b2_ref, fr_ref, w3h_ref, w3l_ref,
                 dec_ref, k_ref, sum_ref, *, seq, n2):
    i = pl.program_id(0)
    hp = k_ref.shape[0] // 2
    cols = sum_ref.shape[1]
    jb = k_ref.shape[1] // cols
    row = lax.broadcasted_iota(I32, (jb * hp, 1), 0)
    n_lo = (row % hp) * n2 + i * jb + row // hp
    n_hi = n_lo + seq
    t_of = lambda n: jnp.where(n < seq, n, 2 * seq - n).astype(F32)
    lane = lax.broadcasted_iota(I32, (jb * hp, LANES), 1)
    t_idx = jnp.where(lane < FILTER_HIDDEN, t_of(n_lo), t_of(n_hi))
    feats = jnp.sin(2.0 * math.pi * bands_ref[...] * t_idx / seq + phase_ref[...])
    fr = fr_ref[...]
    pre = (t_idx / (seq - 1)) * w1t_ref[...] + jnp.dot(feats, w1_ref[...], precision=HIGHEST,
                                                      preferred_element_type=F32)
    h = jnp.sin(fr * (pre + b1_ref[...]))
    h = jnp.sin(fr * (jnp.dot(h, w2_ref[...], precision=HIGHEST, preferred_element_type=F32) + b2_ref[...]))
    h_hi = h.astype(BF16)
    h_lo = (h - h_hi.astype(F32)).astype(BF16)

    @pl.when(i == 0)
    def _():
        sum_ref[...] = jnp.zeros_like(sum_ref)

    for j, n in enumerate((n_lo, n_hi)):
        out = (jnp.dot(h_hi, w3h_ref[j], preferred_element_type=F32)
               + jnp.dot(h_lo, w3h_ref[j], preferred_element_type=F32)
               + jnp.dot(h_hi, w3l_ref[j], preferred_element_type=F32))
        window = jnp.exp(-(t_of(n) / (seq - 1)) * dec_ref[j]) + DECAY_SHIFT
        k = jnp.where(n == seq, 0.0, out * window)
        sum_ref[...] += jnp.sum(jnp.abs(k), axis=0, keepdims=True)
        for jj in range(jb):
            k_ref[j * hp:(j + 1) * hp, jj * cols:(jj + 1) * cols] = k[jj * hp:(jj + 1) * hp].astype(BF16)


def _filters(seq, n1, n2, fw):
    cols = HYENA_ORDER * D_HYENA
    const = lambda shape: pl.BlockSpec(shape, lambda i: (0,) * len(shape))
    jb = max(1, 1024 // n1)
    return pl.pallas_call(
        functools.partial(_filter_body, seq=seq, n2=n2),
        grid=(n2 // jb,),
        in_specs=[const((1, LANES)), const((1, LANES)), const((1, LANES)), const((LANES, LANES)),
                  const((1, LANES)), const((LANES, LANES)), const((1, LANES)), const((1, LANES)),
                  const((N_DIR, LANES, cols)), const((N_DIR, LANES, cols)), const((N_DIR, 1, cols))],
        out_specs=[pl.BlockSpec((n1, jb * cols), lambda i: (0, i)),
                   pl.BlockSpec((1, cols), lambda i: (0, 0))],
        out_shape=[jax.ShapeDtypeStruct((n1, n2 * cols), BF16),
                   jax.ShapeDtypeStruct((1, cols), F32)],
        compiler_params=_params(("arbitrary",), 32),
        name="filter_gen",
    )(fw["bands"], fw["phase"], fw["w1t"], fw["w1"], fw["b1"], fw["w2"], fw["b2"], fw["freq"],
      fw["w3_hi"], fw["w3_lo"], fw["decay"])


def _fft_dims(seq):
    n2 = 128 if 2 * seq >= 32768 else 64
    n1 = 2 * seq // n2
    return n1, n2


def _dft_tables(n1, n2):
    n = n1 * n2
    n1h = n1 // 2
    k1 = jnp.arange(n1h, dtype=I32)[:, None]
    m1 = jnp.arange(n1, dtype=I32)[None, :]
    ang = (2.0 * math.pi / n1) * ((k1 * m1) % n1).astype(F32)
    top = jnp.cos(ang)
    bot = -jnp.sin(ang)
    nyq = jnp.where(m1 % 2 == 0, 1.0, -1.0).astype(F32)
    bot = jnp.concatenate([nyq, bot[1:]], axis=0)
    fa = jnp.concatenate([top, bot], axis=0)
    weight = jnp.where((jnp.arange(n1) % n1h) == 0, 1.0, 2.0).astype(F32) / n
    fi = (fa[:, :n1h] * weight[:, None]).T

    kk = jnp.arange(n1h + 1, dtype=I32)[:, None, None]
    k2 = jnp.arange(n2, dtype=I32)[None, :, None]
    m2 = jnp.arange(n2, dtype=I32)[None, None, :]
    phi = (2.0 * math.pi / n) * ((m2 * (kk + n1 * k2)) % n).astype(F32)
    gr = jnp.cos(phi)
    gi = -jnp.sin(phi)
    blk = jnp.concatenate([jnp.concatenate([gr, -gi], axis=2),
                           jnp.concatenate([gi, gr], axis=2)], axis=1)
    left = (jnp.arange(2 * n2) < n2)[None, :]
    g0 = jnp.where(left, blk[0], 0.0)
    gf = jnp.concatenate([g0[None], blk[1:n1h]], axis=0)
    gnyq = jnp.concatenate([jnp.zeros((2 * n2, n2), F32), blk[n1h][:, :n2]], axis=1)
    return dict(fa_full=fa.astype(BF16), fa_half=fa[:, :n1h].astype(BF16), fi=fi.astype(BF16),
                gf=gf.astype(BF16), gnyq=gnyq.astype(BF16),
                ginv=jnp.transpose(gf, (0, 2, 1)).astype(BF16), m2=gnyq.T.astype(BF16))


def _fft_a_body(x_ref, f_ref, o_ref):
    a = jnp.dot(f_ref[...], x_ref[0].astype(BF16), preferred_element_type=F32)
    n1h = o_ref.shape[2]
    o_ref[0, 0] = a[:n1h].astype(BF16)
    o_ref[0, 1] = a[n1h:].astype(BF16)


def _fft_a(x3, fmat):
    bsz, r, nc = x3.shape
    n1 = fmat.shape[0]
    tn = min(8192, nc)
    return pl.pallas_call(
        _fft_a_body,
        grid=(bsz, nc // tn),
        in_specs=[pl.BlockSpec((1, r, tn), lambda b, j: (b, 0, j)),
                  pl.BlockSpec((n1, r), lambda b, j: (0, 0))],
        out_specs=pl.BlockSpec((1, 2, n1 // 2, tn), lambda b, j: (b, 0, 0, j)),
        out_shape=jax.ShapeDtypeStruct((bsz, 2, n1 // 2, nc), BF16),
        compiler_params=_params(("parallel", "parallel"), 48),
        name="fft_stage_a",
    )(x3, fmat)


def _fft_b_body(ar_ref, ai_ref, g_ref, gn_ref, sc_ref, x_ref, xn_ref, *, kb, n2):
    j = pl.program_id(1)
    c = sc_ref.shape[1]
    inv = 1.0 / sc_ref[...]
    a_re = ar_ref[0, 0].reshape(kb, n2, c)
    a_im = ai_ref[0, 0].reshape(kb, n2, c)
    for kk in range(kb):
        rhs = jnp.concatenate([a_re[kk], a_im[kk]], axis=0)
        x_ref[0, kk] = (jnp.dot(g_ref[kk], rhs, preferred_element_type=F32) * inv).astype(BF16)

    @pl.when(j == 0)
    def _():
        rhs = jnp.concatenate([a_re[0], a_im[0]], axis=0)
        xn_ref[0] = (jnp.dot(gn_ref[...], rhs, preferred_element_type=F32) * inv).astype(BF16)


def _fft_b(a4, tabs, scale):
    bsz, _, n1h, nc = a4.shape
    c = scale.shape[1]
    n2 = nc // c
    kb = 16
    return pl.pallas_call(
        functools.partial(_fft_b_body, kb=kb, n2=n2),
        grid=(bsz, n1h // kb),
        in_specs=[pl.BlockSpec((1, 1, kb, nc), lambda b, j: (b, 0, j, 0)),
                  pl.BlockSpec((1, 1, kb, nc), lambda b, j: (b, 1, j, 0)),
                  pl.BlockSpec((kb, 2 * n2, 2 * n2), lambda b, j: (j, 0, 0)),
                  pl.BlockSpec((2 * n2, 2 * n2), lambda b, j: (0, 0)),
                  pl.BlockSpec((1, c), lambda b, j: (0, 0))],
        out_specs=[pl.BlockSpec((1, kb, 2 * n2, c), lambda b, j: (b, j, 0, 0)),
                   pl.BlockSpec((1, 2 * n2, c), lambda b, j: (b, 0, 0))],
        out_shape=[jax.ShapeDtypeStruct((bsz, n1h, 2 * n2, c), BF16),
                   jax.ShapeDtypeStruct((bsz, 2 * n2, c), BF16)],
        compiler_params=_params(("parallel", "arbitrary"), 48),
        name="fft_stage_b",
    )(a4, a4, tabs["gf"], tabs["gnyq"], scale)


def _cmul(x, k, n2):
    xr, xi = x[:n2], x[n2:]
    kr, ki = k[:n2], k[n2:]
    return jnp.concatenate([xr * kr - xi * ki, xr * ki + xi * kr], axis=0).astype(BF16)


def _spec_body(ar_ref, ai_ref, k_ref, kn_ref, g_ref, gn_ref, gi_ref, m2_ref, o_ref, *, kb, n2):
    j = pl.program_id(1)
    c = ar_ref.shape[3] // n2
    a_re = ar_ref[0, 0].reshape(kb, n2, c)
    a_im = ai_ref[0, 0].reshape(kb, n2, c)

    def through(fwd, rhs, kf, inv):
        x = jnp.dot(fwd, rhs, preferred_element_type=F32)
        return jnp.dot(inv, _cmul(x, kf.astype(F32), n2), preferred_element_type=F32)

    re_rows, im_rows = [], []
    for k0 in range(0, kb, 2):
        pair = (k0, k0 + 1)
        rhs = [jnp.concatenate([a_re[kk], a_im[kk]], axis=0) for kk in pair]
        xs = [jnp.dot(g_ref[kk], r, preferred_element_type=F32) for kk, r in zip(pair, rhs)]
        ys = [_cmul(x, k_ref[0, kk].astype(F32), n2) for kk, x in zip(pair, xs)]
        accs = [jnp.dot(gi_ref[kk], y, preferred_element_type=F32) for kk, y in zip(pair, ys)]
        if k0 == 0:
            nyq = through(gn_ref[...], rhs[0], kn_ref[0], m2_ref[...])
            accs[0] = jnp.where(j == 0, jnp.concatenate([accs[0][:n2], nyq[n2:]], axis=0), accs[0])
        for acc in accs:
            re_rows.append(acc[:n2].astype(BF16))
            im_rows.append(acc[n2:].astype(BF16))
    o_ref[0, 0] = jnp.stack(re_rows, axis=0).reshape(kb, n2 * c)
    o_ref[0, 1] = jnp.stack(im_rows, axis=0).reshape(kb, n2 * c)


def _spec_conv(a4, kf, kfnyq, order, tabs):
    bsz, _, n1h, nc = a4.shape
    tn2 = kf.shape[2]
    n2 = tn2 // 2
    c = nc // n2
    kb = 16
    return pl.pallas_call(
        functools.partial(_spec_body, kb=kb, n2=n2),
        grid=(bsz, n1h // kb),
        in_specs=[pl.BlockSpec((1, 1, kb, nc), lambda b, j: (b, 0, j, 0)),
                  pl.BlockSpec((1, 1, kb, nc), lambda b, j: (b, 1, j, 0)),
                  pl.BlockSpec((1, kb, tn2, c), lambda b, j: (0, j, 0, order)),
                  pl.BlockSpec((1, tn2, c), lambda b, j: (0, 0, order)),
                  pl.BlockSpec((kb, tn2, tn2), lambda b, j: (j, 0, 0)),
                  pl.BlockSpec((tn2, tn2), lambda b, j: (0, 0)),
                  pl.BlockSpec((kb, tn2, tn2), lambda b, j: (j, 0, 0)),
                  pl.BlockSpec((tn2, tn2), lambda b, j: (0, 0))],
        out_specs=pl.BlockSpec((1, 2, kb, nc), lambda b, j: (b, 0, j, 0)),
        out_shape=jax.ShapeDtypeStruct((bsz, 2, n1h, nc), BF16),
        compiler_params=_params(("parallel", "arbitrary"), 48),
        name="spectral_conv",
    )(a4, a4, kf, kfnyq, tabs["gf"], tabs["gnyq"], tabs["ginv"], tabs["m2"])


def _ifft_a_body(b_ref, f_ref, z_ref, gate_ref, bias_ref, o_ref):
    y = jnp.dot(f_ref[...], b_ref[0], preferred_element_type=F32)
    res = gate_ref[0] * (y + z_ref[0] * bias_ref[...])
    o_ref[0] = res.reshape(o_ref.shape[1:])


def _ifft_a(b3, fi, z3, gate3, bias_t, time_major):
    bsz, n1, nc = b3.shape
    n1h = n1 // 2
    tn = bias_t.shape[1]
    c = D_HYENA
    if time_major:
        out_spec = pl.BlockSpec((1, n1h, tn // c, c), lambda b, j: (b, 0, j, 0))
        out_shape = jax.ShapeDtypeStruct((bsz, n1h, nc // c, c), F32)
    else:
        out_spec = pl.BlockSpec((1, n1h, tn), lambda b, j: (b, 0, j))
        out_shape = jax.ShapeDtypeStruct((bsz, n1h, nc), F32)
    return pl.pallas_call(
        _ifft_a_body,
        grid=(bsz, nc // tn),
        in_specs=[pl.BlockSpec((1, n1, tn), lambda b, j: (b, 0, j)),
                  pl.BlockSpec((n1h, n1), lambda b, j: (0, 0)),
                  pl.BlockSpec((1, n1h, tn), lambda b, j: (b, 0, j)),
                  pl.BlockSpec((1, n1h, tn), lambda b, j: (b, 0, j)),
                  pl.BlockSpec((1, tn), lambda b, j: (0, 0))],
        out_specs=out_spec,
        out_shape=out_shape,
        compiler_params=_params(("parallel", "parallel"), 48),
        name="ifft_stage_a",
    )(b3, fi, z3, gate3, bias_t)


def _hyena(uh, conv_w, conv_b, fw, bias):
    bsz, seq, _ = uh.shape
    c = D_HYENA
    n1, n2 = _fft_dims(seq)
    n1h = n1 // 2
    tabs = _dft_tables(n1, n2)
    v, x1, x2 = _sconv(uh, conv_w, conv_b, n2)

    kcirc, ksum = _filters(seq, n1, n2, fw)
    cols = HYENA_ORDER * c
    ka = _fft_a(kcirc[None], tabs["fa_full"])
    kf, kfnyq = _fft_b(ka, tabs, ksum)

    tn = min(8192, n2 * c)
    z = v
    for order, gate in enumerate((x1, x2)):
        za = _fft_a(z, tabs["fa_half"])
        zb = _spec_conv(za, kf, kfnyq, order, tabs)
        bias_t = jnp.tile(bias[order][None, :], (1, tn // c))
        z = _ifft_a(zb.reshape(bsz, n1, n2 * c), tabs["fi"], z, gate, bias_t,
                    time_major=order == HYENA_ORDER - 1)
    return z.reshape(bsz, seq, c)


def _qkv_body(cq_ref, ckvr_ref, cos_ref, sin_ref, wq_ref, wqr_ref, wk_ref, wv_ref, qa_ref, kva_ref, qg_ref,
              kg_ref, qt_ref, k_ref, vt_ref):
    cqn = _rms(cq_ref[0]) * qa_ref[...]
    qs = jnp.dot(cqn.astype(BF16), wq_ref[...], preferred_element_type=F32)
    ck = ckvr_ref[0]
    ckvn = (_rms(ck[:, :KV_LORA]) * kva_ref[...]).astype(BF16)
    ks = jnp.dot(ckvn, wk_ref[...], preferred_element_type=F32)
    vs = jnp.dot(ckvn, wv_ref[...], preferred_element_type=F32)
    krope = pltpu.roll(ck[:, KV_LORA:], QK_NOPE, axis=1)
    cos = cos_ref[...]
    sin = sin_ref[...]
    lane = lax.broadcasted_iota(I32, cos.shape, 1)
    half = QK_ROPE // 2
    first = (lane >= QK_NOPE) & (lane < QK_NOPE + half)
    second = (lane >= QK_NOPE + half) & (lane < QK_HEAD)

    def head(xh, xh_rot, gain_cos, gain_sin):
        rs = lax.rsqrt(jnp.sum(xh * xh, axis=-1, keepdims=True) * (1.0 / QK_HEAD) + EPS)
        return rs * (xh * gain_cos + xh_rot * gain_sin)

    q_scale = QK_HEAD ** -0.5 * math.log2(math.e)
    q_cos, q_sin = qg_ref[0:1, :] * cos * q_scale, qg_ref[1:2, :] * sin * q_scale
    k_cos, k_sin = kg_ref[0:1, :] * cos, kg_ref[1:2, :] * sin
    qs_rot = jnp.dot(cqn.astype(BF16), wqr_ref[...], preferred_element_type=F32)
    krope_rot = jnp.where(first, -pltpu.roll(krope, SLOT - half, axis=1),
                          jnp.where(second, pltpu.roll(krope, half, axis=1), 0.0))
    eye = (lax.broadcasted_iota(I32, (SLOT, SLOT), 0) == lax.broadcasted_iota(I32, (SLOT, SLOT), 1)).astype(BF16)
    transpose = lambda x: lax.dot_general(eye, x.astype(BF16), (((1,), (1,)), ((), ())),
                                          preferred_element_type=F32).astype(BF16)
    for h in range(N_HEADS):
        sl = slice(h * SLOT, (h + 1) * SLOT)
        qt_ref[0, h] = transpose(head(qs[:, sl], qs_rot[:, sl], q_cos, q_sin))
        k_ref[0, h] = head(ks[:, sl] + krope, krope_rot, k_cos, k_sin).astype(BF16)
    aug = lax.broadcasted_iota(I32, (V_AUG - V_HEAD, cos.shape[0]), 0)
    ones_row = jnp.where(aug == 0, 1.0, 0.0).astype(BF16)
    for hp in range(N_HEADS // 2):
        pair_t = transpose(vs[:, hp * 2 * V_HEAD:(hp + 1) * 2 * V_HEAD])
        vt_ref[0, hp] = jnp.concatenate([pair_t[:V_HEAD], ones_row, pair_t[V_HEAD:], ones_row], axis=0)


def _qkv(cq, ckvr, cos_t, sin_t, aw):
    bsz, seq, _ = cq.shape
    tm = min(512, seq)
    const = lambda shape: pl.BlockSpec(shape, lambda b, i: (0,) * len(shape))
    return pl.pallas_call(
        _qkv_body,
        grid=(bsz, seq // tm),
        in_specs=[pl.BlockSpec((1, tm, Q_LORA), lambda b, i: (b, i, 0)),
                  pl.BlockSpec((1, tm, D_IN_PAD - COL_Q), lambda b, i: (b, i, 0)),
                  pl.BlockSpec((tm, SLOT), lambda b, i: (i, 0)),
                  pl.BlockSpec((tm, SLOT), lambda b, i: (i, 0)),
                  const((Q_LORA, N_HEADS * SLOT)), const((Q_LORA, N_HEADS * SLOT)),
                  const((KV_LORA, N_HEADS * SLOT)), const((KV_LORA, D_ATTN)),
                  const((1, Q_LORA)), const((1, KV_LORA)), const((2, SLOT)), const((2, SLOT))],
        out_specs=[pl.BlockSpec((1, N_HEADS, SLOT, tm), lambda b, i: (b, 0, 0, i)),
                   pl.BlockSpec((1, N_HEADS, tm, SLOT), lambda b, i: (b, 0, i, 0)),
                   pl.BlockSpec((1, N_HEADS // 2, 2 * V_AUG, tm), lambda b, i: (b, 0, 0, i))],
        out_shape=[jax.ShapeDtypeStruct((bsz, N_HEADS, SLOT, seq), BF16),
                   jax.ShapeDtypeStruct((bsz, N_HEADS, seq, SLOT), BF16),
                   jax.ShapeDtypeStruct((bsz, N_HEADS // 2, 2 * V_AUG, seq), BF16)],
        compiler_params=_params(("parallel", "parallel"), 40),
        name="qkv_prep",
    )(cq, ckvr, cos_t, sin_t, aw["wq"], aw["wq_rot"], aw["wk"], aw["wv"], aw["qa"], aw["kva"], aw["qg"],
      aw["kg"])


def _flash_body(qt_ref, k_ref, vt_ref, o_ref, s0_ref, s1_ref, acc_ref, *, tk, nk, nqc, unroll):
    chains = [(hh, qc) for hh in range(2) for qc in range(nqc)]
    acc_ref[...] = jnp.zeros(acc_ref.shape, F32)

    def qk(t, s_ref, only=None):
        ks = pl.multiple_of(t * tk, tk)
        tile_max = []
        for c, (hh, qc) in enumerate(chains):
            if only is not None and c != only:
                continue
            k = k_ref[0, hh, pl.ds(ks, tk), :]
            s = jnp.dot(k, qt_ref[0, hh, :, qc * ATT_QC:(qc + 1) * ATT_QC], preferred_element_type=F32)
            s_ref[c] = s
            tile_max.append(jnp.max(s, axis=0, keepdims=True))
        return tile_max

    def softmax_pv(t, s_ref, tile_max, m, only=None):
        ks = pl.multiple_of(t * tk, tk)
        m_out = []
        for c, (hh, qc) in enumerate(chains):
            if only is not None and c != only:
                continue
            i = 0 if only is not None else c
            m_new = jnp.maximum(m[i], tile_max[i])
            a = jnp.exp2(m[i] - m_new)
            p = jnp.exp2(s_ref[c] - m_new)
            m_out.append(m_new)
            vt = vt_ref[0, 0, hh * V_AUG:(hh + 1) * V_AUG, pl.ds(ks, tk)]
            acc_ref[c] = acc_ref[c] * a + jnp.dot(vt, p.astype(BF16), preferred_element_type=F32)
        return m_out

    def group(u, carry):
        tile_max, m = carry
        for i in range(0, unroll, 2):
            t = unroll * u + i
            tm1, tm0 = [], []
            m = list(m)
            for c in range(len(chains)):
                tm1 += qk(t + 1, s1_ref, only=c)
                m[c] = softmax_pv(t, s0_ref, [tile_max[c]], [m[c]], only=c)[0]
            for c in range(len(chains)):
                tm0 += qk(jnp.minimum(t + 2, nk - 1), s0_ref, only=c)
                m[c] = softmax_pv(t + 1, s1_ref, [tm1[c]], [m[c]], only=c)[0]
            tile_max = tm0
        return tile_max, m

    init = (qk(0, s0_ref), [jnp.full((1, ATT_QC), -jnp.inf, F32)] * len(chains))
    lax.fori_loop(0, nk // unroll, group, init)

    def normalised(c):
        acc = acc_ref[c]
        return acc[:V_HEAD] * (1.0 / acc[V_HEAD:V_HEAD + 1])

    heads = [jnp.concatenate([normalised(hh * nqc + qc) for qc in range(nqc)], axis=1) for hh in range(2)]
    o_ref[0] = jnp.concatenate(heads, axis=0).T


def _flash(qt, k, vt):
    bsz, nh, _, seq = qt.shape
    tq = min(ATT_TQ, seq)
    tk = min(ATT_TK, seq)
    nqc = tq // ATT_QC
    nk = seq // tk
    unroll = min(ATT_UNROLL, nk)
    assert nk % unroll == 0 and unroll % 2 == 0
    return pl.pallas_call(
        functools.partial(_flash_body, tk=tk, nk=nk, nqc=nqc, unroll=unroll),
        grid=(bsz, nh // 2, seq // tq),
        in_specs=[pl.BlockSpec((1, 2, SLOT, tq), lambda b, hp, i: (b, hp, 0, i)),
                  pl.BlockSpec((1, 2, seq, SLOT), lambda b, hp, i: (b, hp, 0, 0)),
                  pl.BlockSpec((1, 1, 2 * V_AUG, seq), lambda b, hp, i: (b, hp, 0, 0))],
        out_specs=pl.BlockSpec((1, tq, 2 * V_HEAD), lambda b, hp, i: (b, i, hp)),
        out_shape=jax.ShapeDtypeStruct((bsz, seq, nh * V_HEAD), F32),
        scratch_shapes=[pltpu.VMEM((2 * nqc, tk, ATT_QC), F32), pltpu.VMEM((2 * nqc, tk, ATT_QC), F32),
                        pltpu.VMEM((2 * nqc, V_AUG, ATT_QC), F32)],
        compiler_params=_params(("parallel", "parallel", "parallel"), 48),
        name="flash_attn",
    )(qt, k, vt)


def _outproj_body(yh_ref, ya_ref, x_ref, og_ref, w_ref, mg_ref, wr_ref, xm_ref, xn_ref, aff_ref, affn_ref):
    og = og_ref[...]
    half = D_MODEL // 2
    tm = x_ref.shape[0]
    chunks = [slice(r, r + tm // OUTPROJ_CHUNKS) for r in range(0, tm, tm // OUTPROJ_CHUNKS)]
    y = [jnp.concatenate([_rms(yh_ref[rows, :]) * og[:, :half], _rms(ya_ref[rows, :]) * og[:, half:]],
                         axis=1).astype(BF16) for rows in chunks]
    xm = [x_ref[rows, :] + jnp.dot(yc, w_ref[...], preferred_element_type=F32) for rows, yc in zip(chunks, y)]
    xn = [_rms(xc) * mg_ref[...] for xc in xm]
    xn_hi = [xc.astype(BF16) for xc in xn]
    xn_lo = [(xc - hc.astype(F32)).astype(BF16) for xc, hc in zip(xn, xn_hi)]
    logits = [jnp.dot(hc, wr_ref[0], preferred_element_type=F32)
              + jnp.dot(lc, wr_ref[0], preferred_element_type=F32)
              + jnp.dot(hc, wr_ref[1], preferred_element_type=F32) for hc, lc in zip(xn_hi, xn_lo)]
    lane = lax.broadcasted_iota(I32, logits[0].shape, 1)
    for rows, xc, hc, lg in zip(chunks, xm, xn_hi, logits):
        xm_ref[rows, :] = xc
        xn_ref[rows, :] = hc
        lg = jnp.where(lane < N_EXPERTS, lg, -jnp.inf)
        e = jnp.exp(lg - jnp.max(lg, axis=-1, keepdims=True))
        aff = e / jnp.sum(e, axis=-1, keepdims=True)
        aff_ref[:, rows] = aff.T[:N_EXPERTS]
        affn_ref[rows, :] = aff[:, :N_EXPERTS]


def _outproj(yh, ya, x2d, og, w_out, mg, wr_pad):
    n = x2d.shape[0]
    tm = min(512, n)
    half = D_MODEL // 2
    const = lambda shape: pl.BlockSpec(shape, lambda i: (0,) * len(shape))
    return pl.pallas_call(
        _outproj_body,
        grid=(n // tm,),
        in_specs=[pl.BlockSpec((tm, half), lambda i: (i, 0)),
                  pl.BlockSpec((tm, half), lambda i: (i, 0)),
                  pl.BlockSpec((tm, D_MODEL), lambda i: (i, 0)),
                  const((1, D_MODEL)), const((D_MODEL, D_MODEL)), const((1, D_MODEL)),
                  const((2, D_MODEL, LANES))],
        out_specs=[pl.BlockSpec((tm, D_MODEL), lambda i: (i, 0)),
                   pl.BlockSpec((tm, D_MODEL), lambda i: (i, 0)),
                   pl.BlockSpec((N_EXPERTS, tm), lambda i: (0, i)),
                   pl.BlockSpec((tm, N_EXPERTS), lambda i: (i, 0))],
        out_shape=[jax.ShapeDtypeStruct((n, D_MODEL), F32),
                   jax.ShapeDtypeStruct((n, D_MODEL), BF16),
                   jax.ShapeDtypeStruct((N_EXPERTS, n), F32),
                   jax.ShapeDtypeStruct((n, N_EXPERTS), F32)],
        compiler_params=_params(("parallel",), 40),
        name="outproj_router",
    )(yh, ya, x2d, og, w_out, mg, wr_pad)


def _select_body(aff_ref, upper_ref, lower_ref, pos_ref, off_ref, *, cap):
    group = aff_ref.shape[0]
    bits = [pltpu.bitcast(aff_ref[x], I32) for x in range(group)]
    upper = upper_ref[...]
    lower = lower_ref[...]

    def count(mask):
        return jnp.sum(jnp.sum(mask.astype(F32), axis=1, keepdims=True), axis=0, keepdims=True)

    def bit_step(i, thr):
        bit = jnp.left_shift(jnp.int32(1), 30 - i)
        return tuple(jnp.where(count(bits[x] >= (thr[x] | bit)) >= cap, thr[x] | bit, thr[x])
                     for x in range(group))

    thr = lax.fori_loop(0, 31, bit_step, tuple(jnp.zeros((1, 1), I32) for _ in range(group)))

    def prefix(mask):
        within = jnp.dot(mask.astype(BF16), upper, preferred_element_type=F32)
        total = within[:, LANES - 1:LANES]
        offs = jnp.dot(lower, jnp.broadcast_to(total, within.shape).astype(BF16), preferred_element_type=F32)
        return within, offs

    for x in range(group):
        gt = bits[x] > thr[x]
        eq = bits[x] == thr[x]
        need = cap - count(gt)
        w_eq, o_eq = prefix(eq)
        sel = gt | (eq & (o_eq + w_eq <= need))
        w_sel, o_sel = prefix(sel)
        pos_ref[x] = jnp.where(sel, o_sel + w_sel - 1.0, -1.0).astype(I32)
        off_ref[x] = o_sel.astype(I32)


def _select(aff_t, cap):
    ne, n = aff_t.shape
    rows = n // LANES
    upper = (jnp.arange(LANES)[:, None] <= jnp.arange(LANES)[None, :]).astype(BF16)
    lower = (jnp.arange(rows)[None, :] < jnp.arange(rows)[:, None]).astype(BF16)
    group = 4
    blk = pl.BlockSpec((group, rows, LANES), lambda e: (e, 0, 0))
    pos, off = pl.pallas_call(
        functools.partial(_select_body, cap=cap),
        grid=(ne // group,),
        in_specs=[blk, pl.BlockSpec((LANES, LANES), lambda e: (0, 0)),
                  pl.BlockSpec((rows, rows), lambda e: (0, 0))],
        out_specs=[blk, blk],
        out_shape=[jax.ShapeDtypeStruct((ne, rows, LANES), I32),
                   jax.ShapeDtypeStruct((ne, rows, LANES), I32)],
        compiler_params=_params(("parallel",), 32),
        name="ec_select",
    )(aff_t.reshape(ne, rows, LANES), upper, lower)
    return pos.reshape(ne, n), off[:, :, 0]


def _ffn_body(off_ref, nch_ref, pos_ref, x_ref, wg_ref, wu_ref, wd_ref, y_ref, xs_ref, *, nb, sub, cap):
    group = y_ref.shape[0]
    e0 = pl.program_id(0) * group
    sb = pl.program_id(1)

    @pl.when(sb == 0)
    def _():
        xs_ref[...] = jnp.zeros(xs_ref.shape, BF16)

    row = lax.broadcasted_iota(I32, (GATHER_ROWS, TOK_BLOCK), 0)

    def first_chunks(i):
        tok = slice(i * TOK_BLOCK, (i + 1) * TOK_BLOCK)
        starts = [pl.multiple_of(off_ref[(e0 + x) * nb + sb * sub + i] * ALIGN, ALIGN) for x in range(group)]
        hits = [(pos_ref[x, :, tok] - starts[x]) == row for x in range(group)]
        win = jnp.dot(jnp.concatenate(hits, axis=0).astype(BF16), x_ref[tok, :], preferred_element_type=F32)
        for x in range(group):
            xs_ref[x, pl.ds(starts[x], GATHER_ROWS), :] += win[x * GATHER_ROWS:(x + 1) * GATHER_ROWS].astype(BF16)

    def later_chunk(x, i, c):
        tok = slice(i * TOK_BLOCK, (i + 1) * TOK_BLOCK)
        start = pl.multiple_of(off_ref[(e0 + x) * nb + sb * sub + i] * ALIGN + c * GATHER_ROWS, ALIGN)
        hit = (pos_ref[x, :, tok] - start) == row
        win = jnp.dot(hit.astype(BF16), x_ref[tok, :], preferred_element_type=F32)
        xs_ref[x, pl.ds(start, GATHER_ROWS), :] += win.astype(BF16)

    for i in range(sub):
        first_chunks(i)
    for x in range(group):
        for i in range(sub):
            lax.fori_loop(1, nch_ref[(e0 + x) * nb + sb * sub + i],
                          lambda c, carry, x=x, i=i: (later_chunk(x, i, c), carry)[1], 0)

    @pl.when(sb == pl.num_programs(1) - 1)
    def _():
        for x in range(group):
            for j in range(cap // SLOT_TILE):
                rows = slice(j * SLOT_TILE, (j + 1) * SLOT_TILE)
                xt = xs_ref[x, rows, :]
                g = jnp.dot(xt, wg_ref[x], preferred_element_type=F32)
                u = jnp.dot(xt, wu_ref[x], preferred_element_type=F32)
                h = (g * jax.nn.sigmoid(g) * u).astype(BF16)
                y_ref[x, rows, :] = jnp.dot(h, wd_ref[x], preferred_element_type=F32).astype(BF16)
            y_ref[x, cap:, :] = jnp.zeros((y_ref.shape[1] - cap, D_MODEL), BF16)


def _ffn(base, nchunk, pos3, xn, wg, wu, wd, cap):
    ne = wg.shape[0]
    n = xn.shape[0]
    nb = n // TOK_BLOCK
    sub = min(4, nb)
    group = 2
    cap_pad = cap + TOK_BLOCK + ALIGN
    grid_spec = pltpu.PrefetchScalarGridSpec(
        num_scalar_prefetch=2,
        grid=(ne // group, nb // sub),
        in_specs=[pl.BlockSpec((group, 1, sub * TOK_BLOCK), lambda e, s, off, nch: (e, 0, s)),
                  pl.BlockSpec((sub * TOK_BLOCK, D_MODEL), lambda e, s, off, nch: (s, 0)),
                  pl.BlockSpec((group, D_MODEL, D_EXPERT), lambda e, s, off, nch: (e, 0, 0)),
                  pl.BlockSpec((group, D_MODEL, D_EXPERT), lambda e, s, off, nch: (e, 0, 0)),
                  pl.BlockSpec((group, D_EXPERT, D_MODEL), lambda e, s, off, nch: (e, 0, 0))],
        out_specs=pl.BlockSpec((group, cap_pad, D_MODEL), lambda e, s, off, nch: (e, 0, 0)),
        scratch_shapes=[pltpu.VMEM((group, cap_pad, D_MODEL), BF16)])
    return pl.pallas_call(
        functools.partial(_ffn_body, nb=nb, sub=sub, cap=cap),
        grid_spec=grid_spec,
        out_shape=jax.ShapeDtypeStruct((ne, cap_pad, D_MODEL), BF16),
        compiler_params=_params(("parallel", "arbitrary"), 52),
        name="ec_ffn",
    )(base, nchunk, pos3, xn, wg, wu, wd)


def _combine_body(off_ref, spill_ref, post_ref, affn_ref, xm_ref, *refs, nb, ne, win_rows):
    win_refs, tail_refs, o_ref = refs[:ne], refs[ne:-1], refs[-1]
    b = pl.program_id(0)
    post = post_ref[...]
    affn = affn_ref[...]

    def expand(y_refs, shift):
        width = y_refs[0].shape[0]
        lane = lax.broadcasted_iota(I32, (TOK_BLOCK, width), 1)
        total = jnp.zeros((TOK_BLOCK, D_MODEL), F32)
        for e in range(ne):
            rel = post[:, e:e + 1] - (off_ref[e * nb + b] * ALIGN + shift)
            hit = (rel == lane).astype(BF16)
            total = total + affn[:, e:e + 1] * jnp.dot(hit, y_refs[e][...], preferred_element_type=F32)
        return total

    def expand_pairs(y_refs):
        width = y_refs[0].shape[0]
        lane = lax.broadcasted_iota(I32, (TOK_BLOCK, 2 * width), 1)
        total = jnp.zeros((TOK_BLOCK, D_MODEL), F32)
        for e in range(0, ne, 2):
            rel0 = post[:, e:e + 1] - off_ref[e * nb + b] * ALIGN
            rel1 = post[:, e + 1:e + 2] - off_ref[(e + 1) * nb + b] * ALIGN + width
            gates = (jnp.where(rel0 == lane, affn[:, e:e + 1], 0.0)
                     + jnp.where((rel1 == lane) & (rel1 >= width), affn[:, e + 1:e + 2], 0.0))
            rows = jnp.concatenate([y_refs[e][...], y_refs[e + 1][...]], axis=0)
            total = total + jnp.dot(gates.astype(BF16), rows, preferred_element_type=F32)
        return total

    if not tail_refs and 2 * win_rows <= TOK_BLOCK and ne % 2 == 0:
        o_ref[...] = xm_ref[...] + expand_pairs(win_refs)
    else:
        o_ref[...] = xm_ref[...] + expand(win_refs, 0)

    if tail_refs:
        @pl.when(spill_ref[b] != 0)
        def _():
            o_ref[...] += expand(tail_refs, win_rows)


def _combine(base, spill, pos_t, aff_n, xm, y, win_rows):
    ne = y.shape[0]
    n = xm.shape[0]
    nb = n // TOK_BLOCK
    tail_rows = ALIGN if win_rows == TOK_BLOCK else 0

    def window(e, rows, shift):
        return pl.BlockSpec((pl.Squeezed(), pl.Element(rows), pl.Element(D_MODEL)),
                            lambda b, off, sp: (e, (off[e * nb + b] + shift // ALIGN) * ALIGN, 0))

    tails = [window(e, tail_rows, win_rows) for e in range(ne)] if tail_rows else []
    grid_spec = pltpu.PrefetchScalarGridSpec(
        num_scalar_prefetch=2,
        grid=(nb,),
        in_specs=([pl.BlockSpec((TOK_BLOCK, ne), lambda b, off, sp: (b, 0)),
                   pl.BlockSpec((TOK_BLOCK, ne), lambda b, off, sp: (b, 0)),
                   pl.BlockSpec((TOK_BLOCK, D_MODEL), lambda b, off, sp: (b, 0))]
                  + [window(e, win_rows, 0) for e in range(ne)] + tails),
        out_specs=pl.BlockSpec((TOK_BLOCK, D_MODEL), lambda b, off, sp: (b, 0)))
    return pl.pallas_call(
        functools.partial(_combine_body, nb=nb, ne=ne, win_rows=win_rows),
        grid_spec=grid_spec,
        out_shape=jax.ShapeDtypeStruct((n, D_MODEL), F32),
        compiler_params=_params(("parallel",), 48),
        name="ec_combine",
    )(base, spill, pos_t, aff_n, xm, *([y] * (ne + len(tails))))


def _ec_moe(xm, xn, aff_t, aff_n, wg, wu, wd):
    ne, n = aff_t.shape
    cap = max(1, EC_CAPACITY * n // N_EXPERTS)
    nb = n // TOK_BLOCK
    pos, off = _select(aff_t, cap)
    first = off[:, ::TOK_BLOCK // LANES]
    count = jnp.concatenate([first[:, 1:], jnp.full((ne, 1), cap, I32)], axis=1) - first
    base = first // ALIGN
    span = jnp.where(count > 0, first - base * ALIGN + count, 0)
    spill = jnp.any(span > TOK_BLOCK, axis=0).astype(I32)
    nchunk = ((span + GATHER_ROWS - 1) // GATHER_ROWS).reshape(ne * nb).astype(I32)
    base = base.reshape(ne * nb).astype(I32)
    y = _ffn(base, nchunk, pos.reshape(ne, 1, n), xn, wg, wu, wd, cap)
    pos_t = pos.T
    return lax.cond(jnp.max(span) <= COMBINE_FAST_ROWS,
                    lambda: _combine(base, spill, pos_t, aff_n, xm, y, COMBINE_FAST_ROWS),
                    lambda: _combine(base, spill, pos_t, aff_n, xm, y, TOK_BLOCK))


def _slot_cols(w, head_width):
    k = w.shape[0]
    w3 = w.reshape(k, N_HEADS, head_width)
    return jnp.pad(w3, ((0, 0), (0, 0), (0, SLOT - head_width))).reshape(k, N_HEADS * SLOT)


def _attn_weights(q_a_norm, w_uq, kv_a_norm, w_ukv, q_norm, k_norm):
    wkv = w_ukv.reshape(KV_LORA, N_HEADS, QK_NOPE + V_HEAD)
    wk = _slot_cols(wkv[:, :, :QK_NOPE].reshape(KV_LORA, -1), QK_NOPE)
    wv = wkv[:, :, QK_NOPE:].reshape(KV_LORA, D_ATTN)
    half = QK_ROPE // 2

    def rotate_half(a):
        return jnp.concatenate([jnp.zeros_like(a[..., :QK_NOPE]), -a[..., QK_NOPE + half:],
                                a[..., QK_NOPE:QK_NOPE + half]], axis=-1)

    def gains(g):
        swapped = jnp.concatenate([jnp.zeros_like(g[:QK_NOPE]), g[QK_NOPE + half:], g[QK_NOPE:QK_NOPE + half]])
        return jnp.pad(jnp.stack([g, swapped]), ((0, 0), (0, SLOT - QK_HEAD)))

    wq3 = w_uq.reshape(Q_LORA, N_HEADS, QK_HEAD)
    return dict(wq=_slot_cols(w_uq, QK_HEAD).astype(BF16),
                wq_rot=_slot_cols(rotate_half(wq3).reshape(Q_LORA, -1), QK_HEAD).astype(BF16),
                wk=wk.astype(BF16), wv=wv.astype(BF16), qa=q_a_norm[None, :], kva=kv_a_norm[None, :],
                qg=gains(q_norm), kg=gains(k_norm))


def _rope_tables(seq):
    pos = jnp.arange(seq, dtype=F32)
    inv_freq = ROPE_THETA ** (-jnp.arange(0, QK_ROPE, 2, dtype=F32) / QK_ROPE)
    ang = pos[:, None] * inv_freq
    ang = jnp.concatenate([ang, ang], axis=-1)
    pad = lambda t, fill: jnp.concatenate(
        [jnp.full((seq, QK_NOPE), fill, F32), t, jnp.full((seq, SLOT - QK_HEAD), fill, F32)], axis=1)
    return pad(jnp.cos(ang), 1.0), pad(jnp.sin(ang), 0.0)


def _hi_lo(w):
    hi = w.astype(BF16)
    return jnp.stack([hi, (w - hi.astype(F32)).astype(BF16)], axis=1)


def _filter_weights(w1, b1, w2, b2, w3, freq, decay):
    hid, nbands = FILTER_HIDDEN, FILTER_BANDS
    assert 2 * hid == LANES and 2 * nbands <= hid
    both = lambda row: jnp.concatenate([row, row])[None, :]
    blockdiag = lambda a: jnp.concatenate(
        [jnp.concatenate([a, jnp.zeros_like(a)], axis=1), jnp.concatenate([jnp.zeros_like(a), a], axis=1)], axis=0)
    bands = jnp.linspace(1e-4, nbands - 1, nbands, dtype=F32)
    rest = jnp.zeros((hid - 2 * nbands,), F32)
    band_half = jnp.concatenate([bands, bands, rest])
    phase_half = jnp.concatenate([jnp.zeros((nbands,), F32), jnp.full((nbands,), 0.5 * math.pi, F32), rest])
    w1_half = jnp.concatenate([-w1[1 + nbands:], w1[1:1 + nbands], jnp.zeros((hid - 2 * nbands, hid), F32)], axis=0)
    cols = HYENA_ORDER * D_HYENA
    w3d = jnp.transpose(w3.reshape(hid, N_DIR, cols), (1, 0, 2))
    w3p = jnp.stack([jnp.pad(w3d[0], ((0, hid), (0, 0))), jnp.pad(w3d[1], ((hid, 0), (0, 0)))], axis=0)
    w3_hi = w3p.astype(BF16)
    w3_lo = (w3p - w3_hi.astype(F32)).astype(BF16)
    return dict(bands=both(band_half), phase=both(phase_half), w1t=both(w1[0]), w1=blockdiag(w1_half),
                b1=both(b1), w2=blockdiag(w2), b2=both(b2), freq=both(freq), w3_hi=w3_hi, w3_lo=w3_lo,
                decay=decay.reshape(N_DIR, 1, cols))


def _trunk(x, p):
    bsz, seq, _ = x.shape
    n = bsz * seq
    cos_t, sin_t = _rope_tables(seq)
    x2 = x.reshape(n, D_MODEL)
    for l in range(DEPTH):
        uh, cq, ckvr = _inproj(x2, p["attn_norm"][l][None, :], p["w_in"][l])
        y_h = _hyena(uh.reshape(bsz, seq, COL_HYENA), p["conv_w"][l], p["conv_b"][l][None, :],
                     p["filt"][l], p["hyena_bias"][l])
        qt, k, vt = _qkv(cq.reshape(bsz, seq, -1), ckvr.reshape(bsz, seq, -1), cos_t, sin_t, p["attn"][l])
        y_a = _flash(qt, k, vt)
        xm, xn, aff_t, aff_n = _outproj(y_h.reshape(n, D_HYENA), y_a.reshape(n, D_ATTN), x2,
                                 p["out_norm"][l][None, :], p["w_out"][l], p["mlp_norm"][l][None, :],
                                 p["w_router"][l])
        x2 = _ec_moe(xm, xn, aff_t, aff_n, p["w_gate"][l], p["w_up"][l], p["w_down"][l])
    return x2.reshape(bsz, seq, D_MODEL)


def kernel(x_prompt, x_sample, attn_norm, w_in, conv_w, conv_b, filt_w1, filt_b1, filt_w2, filt_b2, filt_w3,
           filt_freq, filt_decay, hyena_bias, q_a_norm, w_uq, kv_a_norm, w_ukv, q_norm, k_norm, out_norm,
           w_out, mlp_norm, w_router, w_gate, w_up, w_down):
    p = dict(
        attn_norm=attn_norm,
        w_in=jnp.pad(w_in, ((0, 0), (0, 0), (0, D_IN_PAD - D_IN))).astype(BF16),
        conv_w=conv_w, conv_b=conv_b, hyena_bias=hyena_bias,
        filt=[_filter_weights(filt_w1[l], filt_b1[l], filt_w2[l], filt_b2[l], filt_w3[l], filt_freq[l],
                              filt_decay[l]) for l in range(DEPTH)],
        attn=[_attn_weights(q_a_norm[l], w_uq[l], kv_a_norm[l], w_ukv[l], q_norm[l], k_norm[l])
              for l in range(DEPTH)],
        out_norm=out_norm, w_out=w_out.astype(BF16), mlp_norm=mlp_norm,
        w_router=_hi_lo(jnp.pad(w_router, ((0, 0), (0, 0), (0, LANES - N_EXPERTS)))),
        w_gate=w_gate.astype(BF16), w_up=w_up.astype(BF16), w_down=w_down.astype(BF16))
    return (_trunk(x_prompt, p), _trunk(x_sample, p))
```

```python
import functools
import math

import jax
import jax.numpy as jnp
from jax import lax
from jax.experimental import pallas as pl
from jax.experimental.pallas import tpu as pltpu

F32 = jnp.float32
BF16 = jnp.bfloat16
I32 = jnp.int32
HIGHEST = lax.Precision.HIGHEST

D_MODEL = 1024
DEPTH = 2
D_HYENA = 512
HYENA_ORDER = 2
FILTER_BANDS = 16
FILTER_HIDDEN = 64
N_DIR = 2
DECAY_SHIFT = 0.05
N_HEADS = 8
QK_NOPE = 64
QK_ROPE = 32
QK_HEAD = QK_NOPE + QK_ROPE
V_HEAD = 64
V_AUG = V_HEAD + 16
D_ATTN = N_HEADS * V_HEAD
Q_LORA = 256
KV_LORA = 128
ROPE_THETA = 10000.0
N_EXPERTS = 16
EC_CAPACITY = 2
D_EXPERT = 512
EPS = 1e-6
COL_HYENA = (HYENA_ORDER + 1) * D_HYENA
COL_Q = COL_HYENA + Q_LORA
COL_KV = COL_Q + KV_LORA
D_IN = COL_KV + QK_ROPE
D_IN_PAD = 2048

LANES = 128
SLOT = 128
TOK_BLOCK = 256
SLOT_TILE = 256
ALIGN = 16
GATHER_ROWS = 64
COMBINE_FAST_ROWS = 128
OUTPROJ_CHUNKS = 2
ATT_TQ = 512
ATT_QC = 256
ATT_TK = 256
ATT_UNROLL = 32
MIB = 1024 * 1024


def _params(sem, vmem_mib):
    return pltpu.CompilerParams(dimension_semantics=sem, vmem_limit_bytes=vmem_mib * MIB)


def _rms(x):
    return x * lax.rsqrt(jnp.mean(x * x, axis=-1, keepdims=True) + EPS)


def _inproj_body(x_ref, g_ref, w_ref, uh_ref, cq_ref, ckvr_ref):
    xn = _rms(x_ref[...]) * g_ref[...]
    u = jnp.dot(xn.astype(BF16), w_ref[...], preferred_element_type=F32)
    uh_ref[...] = u[:, :COL_HYENA]
    cq_ref[...] = u[:, COL_HYENA:COL_Q]
    ckvr_ref[...] = u[:, COL_Q:]


def _inproj(x2d, g, w_pad):
    n = x2d.shape[0]
    tm = min(512, n)
    return pl.pallas_call(
        _inproj_body,
        grid=(n // tm,),
        in_specs=[pl.BlockSpec((tm, D_MODEL), lambda i: (i, 0)),
                  pl.BlockSpec((1, D_MODEL), lambda i: (0, 0)),
                  pl.BlockSpec((D_MODEL, D_IN_PAD), lambda i: (0, 0))],
        out_specs=[pl.BlockSpec((tm, COL_HYENA), lambda i: (i, 0)),
                   pl.BlockSpec((tm, Q_LORA), lambda i: (i, 0)),
                   pl.BlockSpec((tm, D_IN_PAD - COL_Q), lambda i: (i, 0))],
        out_shape=[jax.ShapeDtypeStruct((n, COL_HYENA), F32),
                   jax.ShapeDtypeStruct((n, Q_LORA), F32),
                   jax.ShapeDtypeStruct((n, D_IN_PAD - COL_Q), F32)],
        compiler_params=_params(("parallel",), 40),
        name="inproj",
    )(x2d, g, w_pad)


def _sconv_body(u_ref, prev_ref, next_ref, w_ref, b_ref, v_ref, x1_ref, x2_ref):
    i = pl.program_id(1)
    last = pl.num_programs(1) - 1
    u = u_ref[0]
    tl = u.shape[0]
    prev_row = jnp.where(i == 0, 0.0, prev_ref[0, 7:8, :])
    next_row = jnp.where(i == last, 0.0, next_ref[0, 0:1, :])
    row = lax.broadcasted_iota(I32, u.shape, 0)
    up = jnp.where(row == 0, prev_row, pltpu.roll(u, 1, axis=0))
    dn = jnp.where(row == tl - 1, next_row, pltpu.roll(u, tl - 1, axis=0))
    y = up * w_ref[0:1, :] + u * w_ref[1:2, :] + dn * w_ref[2:3, :] + b_ref[...]
    rows, flat = v_ref.shape[1:]
    for k, o_ref in enumerate((v_ref, x1_ref, x2_ref)):
        o_ref[0] = y[:, k * D_HYENA:(k + 1) * D_HYENA].reshape(rows, tl // rows, D_HYENA).reshape(rows, flat)


def _sconv(uh, w, b, n2):
    bsz, seq, c = uh.shape
    rows = 8
    tl = rows * n2
    r = tl // 8
    nblk8 = seq // 8
    out = jax.ShapeDtypeStruct((bsz, seq // n2, n2 * D_HYENA), F32)
    ospec = pl.BlockSpec((1, rows, n2 * D_HYENA), lambda bi, i: (bi, i, 0))
    return pl.pallas_call(
        _sconv_body,
        grid=(bsz, seq // tl),
        in_specs=[pl.BlockSpec((1, tl, c), lambda bi, i: (bi, i, 0)),
                  pl.BlockSpec((1, 8, c), lambda bi, i: (bi, jnp.maximum(i * r - 1, 0), 0)),
                  pl.BlockSpec((1, 8, c), lambda bi, i: (bi, jnp.minimum((i + 1) * r, nblk8 - 1), 0)),
                  pl.BlockSpec((3, c), lambda bi, i: (0, 0)),
                  pl.BlockSpec((1, c), lambda bi, i: (0, 0))],
        out_specs=[ospec, ospec, ospec],
        out_shape=[out, out, out],
        compiler_params=_params(("parallel", "parallel"), 40),
        name="sconv",
    )(uh, uh, uh, w, b)


def _filter_body(bands_ref, phase_ref, w1t_ref, w1_ref, b1_ref, w2_ref, b2_ref, fr_ref, w3h_ref, w3l_ref,
                 dec_ref, k_ref, sum_ref, *, seq, n2):
    i = pl.program_id(0)
    hp = k_ref.shape[0] // 2
    cols = sum_ref.shape[1]
    jb = k_ref.shape[1] // cols
    row = lax.broadcasted_iota(I32, (jb * hp, 1), 0)
    n_lo = (row % hp) * n2 + i * jb + row // hp
    n_hi = n_lo + seq
    t_of = lambda n: jnp.where(n < seq, n, 2 * seq - n).astype(F32)
    lane = lax.broadcasted_iota(I32, (jb * hp, LANES), 1)
    t_idx = jnp.where(lane < FILTER_HIDDEN, t_of(n_lo), t_of(n_hi))
    feats = jnp.sin(2.0 * math.pi * bands_ref[...] * t_idx / seq + phase_ref[...])
    fr = fr_ref[...]
    pre = (t_idx / (seq - 1)) * w1t_ref[...] + jnp.dot(feats, w1_ref[...], precision=HIGHEST,
                                                      preferred_element_type=F32)
    h = jnp.sin(fr * (pre + b1_ref[...]))
    h = jnp.sin(fr * (jnp.dot(h, w2_ref[...], precision=HIGHEST, preferred_element_type=F32) + b2_ref[...]))
    h_hi = h.astype(BF16)
    h_lo = (h - h_hi.astype(F32)).astype(BF16)

    @pl.when(i == 0)
    def _():
        sum_ref[...] = jnp.zeros_like(sum_ref)

    for j, n in enumerate((n_lo, n_hi)):
        out = (jnp.dot(h_hi, w3h_ref[j], preferred_element_type=F32)
               + jnp.dot(h_lo, w3h_ref[j], preferred_element_type=F32)
               + jnp.dot(h_hi, w3l_ref[j], preferred_element_type=F32))
        window = jnp.exp(-(t_of(n) / (seq - 1)) * dec_ref[j]) + DECAY_SHIFT
        k = jnp.where(n == seq, 0.0, out * window)
        sum_ref[...] += jnp.sum(jnp.abs(k), axis=0, keepdims=True)
        for jj in range(jb):
            k_ref[j * hp:(j + 1) * hp, jj * cols:(jj + 1) * cols] = k[jj * hp:(jj + 1) * hp].astype(BF16)


def _filters(seq, n1, n2, fw):
    cols = HYENA_ORDER * D_HYENA
    const = lambda shape: pl.BlockSpec(shape, lambda i: (0,) * len(shape))
    jb = max(1, 1024 // n1)
    return pl.pallas_call(
        functools.partial(_filter_body, seq=seq, n2=n2),
        grid=(n2 // jb,),
        in_specs=[const((1, LANES)), const((1, LANES)), const((1, LANES)), const((LANES, LANES)),
                  const((1, LANES)), const((LANES, LANES)), const((1, LANES)), const((1, LANES)),
                  const((N_DIR, LANES, cols)), const((N_DIR, LANES, cols)), const((N_DIR, 1, cols))],
        out_specs=[pl.BlockSpec((n1, jb * cols), lambda i: (0, i)),
                   pl.BlockSpec((1, cols), lambda i: (0, 0))],
        out_shape=[jax.ShapeDtypeStruct((n1, n2 * cols), BF16),
                   jax.ShapeDtypeStruct((1, cols), F32)],
        compiler_params=_params(("arbitrary",), 32),
        name="filter_gen",
    )(fw["bands"], fw["phase"], fw["w1t"], fw["w1"], fw["b1"], fw["w2"], fw["b2"], fw["freq"],
      fw["w3_hi"], fw["w3_lo"], fw["decay"])


def _fft_dims(seq):
    n2 = 128 if 2 * seq >= 32768 else 64
    n1 = 2 * seq // n2
    return n1, n2


def _dft_tables(n1, n2):
    n = n1 * n2
    n1h = n1 // 2
    k1 = jnp.arange(n1h, dtype=I32)[:, None]
    m1 = jnp.arange(n1, dtype=I32)[None, :]
    ang = (2.0 * math.pi / n1) * ((k1 * m1) % n1).astype(F32)
    top = jnp.cos(ang)
    bot = -jnp.sin(ang)
    nyq = jnp.where(m1 % 2 == 0, 1.0, -1.0).astype(F32)
    bot = jnp.concatenate([nyq, bot[1:]], axis=0)
    fa = jnp.concatenate([top, bot], axis=0)
    weight = jnp.where((jnp.arange(n1) % n1h) == 0, 1.0, 2.0).astype(F32) / n
    fi = (fa[:, :n1h] * weight[:, None]).T

    kk = jnp.arange(n1h + 1, dtype=I32)[:, None, None]
    k2 = jnp.arange(n2, dtype=I32)[None, :, None]
    m2 = jnp.arange(n2, dtype=I32)[None, None, :]
    phi = (2.0 * math.pi / n) * ((m2 * (kk + n1 * k2)) % n).astype(F32)
    gr = jnp.cos(phi)
    gi = -jnp.sin(phi)
    blk = jnp.concatenate([jnp.concatenate([gr, -gi], axis=2),
                           jnp.concatenate([gi, gr], axis=2)], axis=1)
    left = (jnp.arange(2 * n2) < n2)[None, :]
    g0 = jnp.where(left, blk[0], 0.0)
    gf = jnp.concatenate([g0[None], blk[1:n1h]], axis=0)
    gnyq = jnp.concatenate([jnp.zeros((2 * n2, n2), F32), blk[n1h][:, :n2]], axis=1)
    return dict(fa_full=fa.astype(BF16), fa_half=fa[:, :n1h].astype(BF16), fi=fi.astype(BF16),
                gf=gf.astype(BF16), gnyq=gnyq.astype(BF16),
                ginv=jnp.transpose(gf, (0, 2, 1)).astype(BF16), m2=gnyq.T.astype(BF16))


def _fft_a_body(x_ref, f_ref, o_ref):
    a = jnp.dot(f_ref[...], x_ref[0].astype(BF16), preferred_element_type=F32)
    n1h = o_ref.shape[2]
    o_ref[0, 0] = a[:n1h].astype(BF16)
    o_ref[0, 1] = a[n1h:].astype(BF16)


def _fft_a(x3, fmat):
    bsz, r, nc = x3.shape
    n1 = fmat.shape[0]
    tn = min(8192, nc)
    return pl.pallas_call(
        _fft_a_body,
        grid=(bsz, nc // tn),
        in_specs=[pl.BlockSpec((1, r, tn), lambda b, j: (b, 0, j)),
                  pl.BlockSpec((n1, r), lambda b, j: (0, 0))],
        out_specs=pl.BlockSpec((1, 2, n1 // 2, tn), lambda b, j: (b, 0, 0, j)),
        out_shape=jax.ShapeDtypeStruct((bsz, 2, n1 // 2, nc), BF16),
        compiler_params=_params(("parallel", "parallel"), 48),
        name="fft_stage_a",
    )(x3, fmat)


def _fft_b_body(ar_ref, ai_ref, g_ref, gn_ref, sc_ref, x_ref, xn_ref, *, kb, n2):
    j = pl.program_id(1)
    c = sc_ref.shape[1]
    inv = 1.0 / sc_ref[...]
    a_re = ar_ref[0, 0].reshape(kb, n2, c)
    a_im = ai_ref[0, 0].reshape(kb, n2, c)
    for kk in range(kb):
        rhs = jnp.concatenate([a_re[kk], a_im[kk]], axis=0)
        x_ref[0, kk] = (jnp.dot(g_ref[kk], rhs, preferred_element_type=F32) * inv).astype(BF16)

    @pl.when(j == 0)
    def _():
        rhs = jnp.concatenate([a_re[0], a_im[0]], axis=0)
        xn_ref[0] = (jnp.dot(gn_ref[...], rhs, preferred_element_type=F32) * inv).astype(BF16)


def _fft_b(a4, tabs, scale):
    bsz, _, n1h, nc = a4.shape
    c = scale.shape[1]
    n2 = nc // c
    kb = 16
    return pl.pallas_call(
        functools.partial(_fft_b_body, kb=kb, n2=n2),
        grid=(bsz, n1h // kb),
        in_specs=[pl.BlockSpec((1, 1, kb, nc), lambda b, j: (b, 0, j, 0)),
                  pl.BlockSpec((1, 1, kb, nc), lambda b, j: (b, 1, j, 0)),
                  pl.BlockSpec((kb, 2 * n2, 2 * n2), lambda b, j: (j, 0, 0)),
                  pl.BlockSpec((2 * n2, 2 * n2), lambda b, j: (0, 0)),
                  pl.BlockSpec((1, c), lambda b, j: (0, 0))],
        out_specs=[pl.BlockSpec((1, kb, 2 * n2, c), lambda b, j: (b, j, 0, 0)),
                   pl.BlockSpec((1, 2 * n2, c), lambda b, j: (b, 0, 0))],
        out_shape=[jax.ShapeDtypeStruct((bsz, n1h, 2 * n2, c), BF16),
                   jax.ShapeDtypeStruct((bsz, 2 * n2, c), BF16)],
        compiler_params=_params(("parallel", "arbitrary"), 48),
        name="fft_stage_b",
    )(a4, a4, tabs["gf"], tabs["gnyq"], scale)


def _cmul(x, k, n2):
    xr, xi = x[:n2], x[n2:]
    kr, ki = k[:n2], k[n2:]
    return jnp.concatenate([xr * kr - xi * ki, xr * ki + xi * kr], axis=0).astype(BF16)


def _spec_body(ar_ref, ai_ref, k_ref, kn_ref, g_ref, gn_ref, gi_ref, m2_ref, o_ref, *, kb, n2):
    j = pl.program_id(1)
    c = ar_ref.shape[3] // n2
    a_re = ar_ref[0, 0].reshape(kb, n2, c)
    a_im = ai_ref[0, 0].reshape(kb, n2, c)

    def through(fwd, rhs, kf, inv):
        x = jnp.dot(fwd, rhs, preferred_element_type=F32)
        return jnp.dot(inv, _cmul(x, kf.astype(F32), n2), preferred_element_type=F32)

    re_rows, im_rows = [], []
    for k0 in range(0, kb, 2):
        pair = (k0, k0 + 1)
        rhs = [jnp.concatenate([a_re[kk], a_im[kk]], axis=0) for kk in pair]
        xs = [jnp.dot(g_ref[kk], r, preferred_element_type=F32) for kk, r in zip(pair, rhs)]
        ys = [_cmul(x, k_ref[0, kk].astype(F32), n2) for kk, x in zip(pair, xs)]
        accs = [jnp.dot(gi_ref[kk], y, preferred_element_type=F32) for kk, y in zip(pair, ys)]
        if k0 == 0:
            nyq = through(gn_ref[...], rhs[0], kn_ref[0], m2_ref[...])
            accs[0] = jnp.where(j == 0, jnp.concatenate([accs[0][:n2], nyq[n2:]], axis=0), accs[0])
        for acc in accs:
            re_rows.append(acc[:n2].astype(BF16))
            im_rows.append(acc[n2:].astype(BF16))
    o_ref[0, 0] = jnp.stack(re_rows, axis=0).reshape(kb, n2 * c)
    o_ref[0, 1] = jnp.stack(im_rows, axis=0).reshape(kb, n2 * c)


def _spec_conv(a4, kf, kfnyq, order, tabs):
    bsz, _, n1h, nc = a4.shape
    tn2 = kf.shape[2]
    n2 = tn2 // 2
    c = nc // n2
    kb = 16
    return pl.pallas_call(
        functools.partial(_spec_body, kb=kb, n2=n2),
        grid=(bsz, n1h // kb),
        in_specs=[pl.BlockSpec((1, 1, kb, nc), lambda b, j: (b, 0, j, 0)),
                  pl.BlockSpec((1, 1, kb, nc), lambda b, j: (b, 1, j, 0)),
                  pl.BlockSpec((1, kb, tn2, c), lambda b, j: (0, j, 0, order)),
                  pl.BlockSpec((1, tn2, c), lambda b, j: (0, 0, order)),
                  pl.BlockSpec((kb, tn2, tn2), lambda b, j: (j, 0, 0)),
                  pl.BlockSpec((tn2, tn2), lambda b, j: (0, 0)),
                  pl.BlockSpec((kb, tn2, tn2), lambda b, j: (j, 0, 0)),
                  pl.BlockSpec((tn2, tn2), lambda b, j: (0, 0))],
        out_specs=pl.BlockSpec((1, 2, kb, nc), lambda b, j: (b, 0, j, 0)),
        out_shape=jax.ShapeDtypeStruct((bsz, 2, n1h, nc), BF16),
        compiler_params=_params(("parallel", "arbitrary"), 48),
        name="spectral_conv",
    )(a4, a4, kf, kfnyq, tabs["gf"], tabs["gnyq"], tabs["ginv"], tabs["m2"])


def _ifft_a_body(b_ref, f_ref, z_ref, gate_ref, bias_ref, o_ref):
    y = jnp.dot(f_ref[...], b_ref[0], preferred_element_type=F32)
    res = gate_ref[0] * (y + z_ref[0] * bias_ref[...])
    o_ref[0] = res.reshape(o_ref.shape[1:])


def _ifft_a(b3, fi, z3, gate3, bias_t, time_major):
    bsz, n1, nc = b3.shape
    n1h = n1 // 2
    tn = bias_t.shape[1]
    c = D_HYENA
    if time_major:
        out_spec = pl.BlockSpec((1, n1h, tn // c, c), lambda b, j: (b, 0, j, 0))
        out_shape = jax.ShapeDtypeStruct((bsz, n1h, nc // c, c), F32)
    else:
        out_spec = pl.BlockSpec((1, n1h, tn), lambda b, j: (b, 0, j))
        out_shape = jax.ShapeDtypeStruct((bsz, n1h, nc), F32)
    return pl.pallas_call(
        _ifft_a_body,
        grid=(bsz, nc // tn),
        in_specs=[pl.BlockSpec((1, n1, tn), lambda b, j: (b, 0, j)),
                  pl.BlockSpec((n1h, n1), lambda b, j: (0, 0)),
                  pl.BlockSpec((1, n1h, tn), lambda b, j: (b, 0, j)),
                  pl.BlockSpec((1, n1h, tn), lambda b, j: (b, 0, j)),
                  pl.BlockSpec((1, tn), lambda b, j: (0, 0))],
        out_specs=out_spec,
        out_shape=out_shape,
        compiler_params=_params(("parallel", "parallel"), 48),
        name="ifft_stage_a",
    )(b3, fi, z3, gate3, bias_t)


def _hyena(uh, conv_w, conv_b, fw, bias):
    bsz, seq, _ = uh.shape
    c = D_HYENA
    n1, n2 = _fft_dims(seq)
    n1h = n1 // 2
    tabs = _dft_tables(n1, n2)
    v, x1, x2 = _sconv(uh, conv_w, conv_b, n2)

    kcirc, ksum = _filters(seq, n1, n2, fw)
    cols = HYENA_ORDER * c
    ka = _fft_a(kcirc[None], tabs["fa_full"])
    kf, kfnyq = _fft_b(ka, tabs, ksum)

    tn = min(8192, n2 * c)
    z = v
    for order, gate in enumerate((x1, x2)):
        za = _fft_a(z, tabs["fa_half"])
        zb = _spec_conv(za, kf, kfnyq, order, tabs)
        bias_t = jnp.tile(bias[order][None, :], (1, tn // c))
        z = _ifft_a(zb.reshape(bsz, n1, n2 * c), tabs["fi"], z, gate, bias_t,
                    time_major=order == HYENA_ORDER - 1)
    return z.reshape(bsz, seq, c)


def _qkv_body(cq_ref, ckvr_ref, cos_ref, sin_ref, wq_ref, wqr_ref, wk_ref, wv_ref, qa_ref, kva_ref, qg_ref,
              kg_ref, qt_ref, k_ref, vt_ref):
    cqn = _rms(cq_ref[0]) * qa_ref[...]
    qs = jnp.dot(cqn.astype(BF16), wq_ref[...], preferred_element_type=F32)
    ck = ckvr_ref[0]
    ckvn = (_rms(ck[:, :KV_LORA]) * kva_ref[...]).astype(BF16)
    ks = jnp.dot(ckvn, wk_ref[...], preferred_element_type=F32)
    vs = jnp.dot(ckvn, wv_ref[...], preferred_element_type=F32)
    krope = pltpu.roll(ck[:, KV_LORA:], QK_NOPE, axis=1)
    cos = cos_ref[...]
    sin = sin_ref[...]
    lane = lax.broadcasted_iota(I32, cos.shape, 1)
    half = QK_ROPE // 2
    first = (lane >= QK_NOPE) & (lane < QK_NOPE + half)
    second = (lane >= QK_NOPE + half) & (lane < QK_HEAD)

    def head(xh, xh_rot, gain_cos, gain_sin):
        rs = lax.rsqrt(jnp.sum(xh * xh, axis=-1, keepdims=True) * (1.0 / QK_HEAD) + EPS)
        return rs * (xh * gain_cos + xh_rot * gain_sin)

    q_scale = QK_HEAD ** -0.5 * math.log2(math.e)
    q_cos, q_sin = qg_ref[0:1, :] * cos * q_scale, qg_ref[1:2, :] * sin * q_scale
    k_cos, k_sin = kg_ref[0:1, :] * cos, kg_ref[1:2, :] * sin
    qs_rot = jnp.dot(cqn.astype(BF16), wqr_ref[...], preferred_element_type=F32)
    krope_rot = jnp.where(first, -pltpu.roll(krope, SLOT - half, axis=1),
                          jnp.where(second, pltpu.roll(krope, half, axis=1), 0.0))
    eye = (lax.broadcasted_iota(I32, (SLOT, SLOT), 0) == lax.broadcasted_iota(I32, (SLOT, SLOT), 1)).astype(BF16)
    transpose = lambda x: lax.dot_general(eye, x.astype(BF16), (((1,), (1,)), ((), ())),
                                          preferred_element_type=F32).astype(BF16)
    for h in range(N_HEADS):
        sl = slice(h * SLOT, (h + 1) * SLOT)
        qt_ref[0, h] = transpose(head(qs[:, sl], qs_rot[:, sl], q_cos, q_sin))
        k_ref[0, h] = head(ks[:, sl] + krope, krope_rot, k_cos, k_sin).astype(BF16)
    aug = lax.broadcasted_iota(I32, (V_AUG - V_HEAD, cos.shape[0]), 0)
    ones_row = jnp.where(aug == 0, 1.0, 0.0).astype(BF16)
    for hp in range(N_HEADS // 2):
        pair_t = transpose(vs[:, hp * 2 * V_HEAD:(hp + 1) * 2 * V_HEAD])
        vt_ref[0, hp] = jnp.concatenate([pair_t[:V_HEAD], ones_row, pair_t[V_HEAD:], ones_row], axis=0)


def _qkv(cq, ckvr, cos_t, sin_t, aw):
    bsz, seq, _ = cq.shape
    tm = min(512, seq)
    const = lambda shape: pl.BlockSpec(shape, lambda b, i: (0,) * len(shape))
    return pl.pallas_call(
        _qkv_body,
        grid=(bsz, seq // tm),
        in_specs=[pl.BlockSpec((1, tm, Q_LORA), lambda b, i: (b, i, 0)),
                  pl.BlockSpec((1, tm, D_IN_PAD - COL_Q), lambda b, i: (b, i, 0)),
                  pl.BlockSpec((tm, SLOT), lambda b, i: (i, 0)),
                  pl.BlockSpec((tm, SLOT), lambda b, i: (i, 0)),
                  const((Q_LORA, N_HEADS * SLOT)), const((Q_LORA, N_HEADS * SLOT)),
                  const((KV_LORA, N_HEADS * SLOT)), const((KV_LORA, D_ATTN)),
                  const((1, Q_LORA)), const((1, KV_LORA)), const((2, SLOT)), const((2, SLOT))],
        out_specs=[pl.BlockSpec((1, N_HEADS, SLOT, tm), lambda b, i: (b, 0, 0, i)),
                   pl.BlockSpec((1, N_HEADS, tm, SLOT), lambda b, i: (b, 0, i, 0)),
                   pl.BlockSpec((1, N_HEADS // 2, 2 * V_AUG, tm), lambda b, i: (b, 0, 0, i))],
        out_shape=[jax.ShapeDtypeStruct((bsz, N_HEADS, SLOT, seq), BF16),
                   jax.ShapeDtypeStruct((bsz, N_HEADS, seq, SLOT), BF16),
                   jax.ShapeDtypeStruct((bsz, N_HEADS // 2, 2 * V_AUG, seq), BF16)],
        compiler_params=_params(("parallel", "parallel"), 40),
        name="qkv_prep",
    )(cq, ckvr, cos_t, sin_t, aw["wq"], aw["wq_rot"], aw["wk"], aw["wv"], aw["qa"], aw["kva"], aw["qg"],
      aw["kg"])


def _flash_body(qt_ref, k_ref, vt_ref, o_ref, s0_ref, s1_ref, acc_ref, *, tk, nk, nqc, unroll):
    chains = [(hh, qc) for hh in range(2) for qc in range(nqc)]
    acc_ref[...] = jnp.zeros(acc_ref.shape, F32)

    def qk(t, s_ref, only=None):
        ks = pl.multiple_of(t * tk, tk)
        tile_max = []
        for c, (hh, qc) in enumerate(chains):
            if only is not None and c != only:
                continue
            k = k_ref[0, hh, pl.ds(ks, tk), :]
            s = jnp.dot(k, qt_ref[0, hh, :, qc * ATT_QC:(qc + 1) * ATT_QC], preferred_element_type=F32)
            s_ref[c] = s
            tile_max.append(jnp.max(s, axis=0, keepdims=True))
        return tile_max

    def softmax_pv(t, s_ref, tile_max, m, only=None):
        ks = pl.multiple_of(t * tk, tk)
        m_out = []
        for c, (hh, qc) in enumerate(chains):
            if only is not None and c != only:
                continue
            i = 0 if only is not None else c
            m_new = jnp.maximum(m[i], tile_max[i])
            a = jnp.exp2(m[i] - m_new)
            p = jnp.exp2(s_ref[c] - m_new)
            m_out.append(m_new)
            vt = vt_ref[0, 0, hh * V_AUG:(hh + 1) * V_AUG, pl.ds(ks, tk)]
            acc_ref[c] = acc_ref[c] * a + jnp.dot(vt, p.astype(BF16), preferred_element_type=F32)
        return m_out

    def group(u, carry):
        tile_max, m = carry
        for i in range(0, unroll, 2):
            t = unroll * u + i
            tm1, tm0 = [], []
            m = list(m)
            for c in range(len(chains)):
                tm1 += qk(t + 1, s1_ref, only=c)
                m[c] = softmax_pv(t, s0_ref, [tile_max[c]], [m[c]], only=c)[0]
            for c in range(len(chains)):
                tm0 += qk(jnp.minimum(t + 2, nk - 1), s0_ref, only=c)
                m[c] = softmax_pv(t + 1, s1_ref, [tm1[c]], [m[c]], only=c)[0]
            tile_max = tm0
        return tile_max, m

    init = (qk(0, s0_ref), [jnp.full((1, ATT_QC), -jnp.inf, F32)] * len(chains))
    lax.fori_loop(0, nk // unroll, group, init)

    def normalised(c):
        acc = acc_ref[c]
        return acc[:V_HEAD] * (1.0 / acc[V_HEAD:V_HEAD + 1])

    heads = [jnp.concatenate([normalised(hh * nqc + qc) for qc in range(nqc)], axis=1) for hh in range(2)]
    o_ref[0] = jnp.concatenate(heads, axis=0).T


def _flash(qt, k, vt):
    bsz, nh, _, seq = qt.shape
    tq = min(ATT_TQ, seq)
    tk = min(ATT_TK, seq)
    nqc = tq // ATT_QC
    nk = seq // tk
    unroll = min(ATT_UNROLL, nk)
    assert nk % unroll == 0 and unroll % 2 == 0
    return pl.pallas_call(
        functools.partial(_flash_body, tk=tk, nk=nk, nqc=nqc, unroll=unroll),
        grid=(bsz, nh // 2, seq // tq),
        in_specs=[pl.BlockSpec((1, 2, SLOT, tq), lambda b, hp, i: (b, hp, 0, i)),
                  pl.BlockSpec((1, 2, seq, SLOT), lambda b, hp, i: (b, hp, 0, 0)),
                  pl.BlockSpec((1, 1, 2 * V_AUG, seq), lambda b, hp, i: (b, hp, 0, 0))],
        out_specs=pl.BlockSpec((1, tq, 2 * V_HEAD), lambda b, hp, i: (b, i, hp)),
        out_shape=jax.ShapeDtypeStruct((bsz, seq, nh * V_HEAD), F32),
        scratch_shapes=[pltpu.VMEM((2 * nqc, tk, ATT_QC), F32), pltpu.VMEM((2 * nqc, tk, ATT_QC), F32),
                        pltpu.VMEM((2 * nqc, V_AUG, ATT_QC), F32)],
        compiler_params=_params(("parallel", "parallel", "parallel"), 48),
        name="flash_attn",
    )(qt, k, vt)


def _outproj_body(yh_ref, ya_ref, x_ref, og_ref, w_ref, mg_ref, wr_ref, xm_ref, xn_ref, aff_ref, affn_ref):
    og = og_ref[...]
    half = D_MODEL // 2
    tm = x_ref.shape[0]
    chunks = [slice(r, r + tm // OUTPROJ_CHUNKS) for r in range(0, tm, tm // OUTPROJ_CHUNKS)]
    y = [jnp.concatenate([_rms(yh_ref[rows, :]) * og[:, :half], _rms(ya_ref[rows, :]) * og[:, half:]],
                         axis=1).astype(BF16) for rows in chunks]
    xm = [x_ref[rows, :] + jnp.dot(yc, w_ref[...], preferred_element_type=F32) for rows, yc in zip(chunks, y)]
    xn = [_rms(xc) * mg_ref[...] for xc in xm]
    xn_hi = [xc.astype(BF16) for xc in xn]
    xn_lo = [(xc - hc.astype(F32)).astype(BF16) for xc, hc in zip(xn, xn_hi)]
    logits = [jnp.dot(hc, wr_ref[0], preferred_element_type=F32)
              + jnp.dot(lc, wr_ref[0], preferred_element_type=F32)
              + jnp.dot(hc, wr_ref[1], preferred_element_type=F32) for hc, lc in zip(xn_hi, xn_lo)]
    lane = lax.broadcasted_iota(I32, logits[0].shape, 1)
    for rows, xc, hc, lg in zip(chunks, xm, xn_hi, logits):
        xm_ref[rows, :] = xc
        xn_ref[rows, :] = hc
        lg = jnp.where(lane < N_EXPERTS, lg, -jnp.inf)
        e = jnp.exp(lg - jnp.max(lg, axis=-1, keepdims=True))
        aff = e / jnp.sum(e, axis=-1, keepdims=True)
        aff_ref[:, rows] = aff.T[:N_EXPERTS]
        affn_ref[rows, :] = aff[:, :N_EXPERTS]


def _outproj(yh, ya, x2d, og, w_out, mg, wr_pad):
    n = x2d.shape[0]
    tm = min(512, n)
    half = D_MODEL // 2
    const = lambda shape: pl.BlockSpec(shape, lambda i: (0,) * len(shape))
    return pl.pallas_call(
        _outproj_body,
        grid=(n // tm,),
        in_specs=[pl.BlockSpec((tm, half), lambda i: (i, 0)),
                  pl.BlockSpec((tm, half), lambda i: (i, 0)),
                  pl.BlockSpec((tm, D_MODEL), lambda i: (i, 0)),
                  const((1, D_MODEL)), const((D_MODEL, D_MODEL)), const((1, D_MODEL)),
                  const((2, D_MODEL, LANES))],
        out_specs=[pl.BlockSpec((tm, D_MODEL), lambda i: (i, 0)),
                   pl.BlockSpec((tm, D_MODEL), lambda i: (i, 0)),
                   pl.BlockSpec((N_EXPERTS, tm), lambda i: (0, i)),
                   pl.BlockSpec((tm, N_EXPERTS), lambda i: (i, 0))],
        out_shape=[jax.ShapeDtypeStruct((n, D_MODEL), F32),
                   jax.ShapeDtypeStruct((n, D_MODEL), BF16),
                   jax.ShapeDtypeStruct((N_EXPERTS, n), F32),
                   jax.ShapeDtypeStruct((n, N_EXPERTS), F32)],
        compiler_params=_params(("parallel",), 40),
        name="outproj_router",
    )(yh, ya, x2d, og, w_out, mg, wr_pad)


def _select_body(aff_ref, upper_ref, lower_ref, pos_ref, off_ref, *, cap):
    group = aff_ref.shape[0]
    bits = [pltpu.bitcast(aff_ref[x], I32) for x in range(group)]
    upper = upper_ref[...]
    lower = lower_ref[...]

    def count(mask):
        return jnp.sum(jnp.sum(mask.astype(F32), axis=1, keepdims=True), axis=0, keepdims=True)

    def bit_step(i, thr):
        bit = jnp.left_shift(jnp.int32(1), 30 - i)
        return tuple(jnp.where(count(bits[x] >= (thr[x] | bit)) >= cap, thr[x] | bit, thr[x])
                     for x in range(group))

    thr = lax.fori_loop(0, 31, bit_step, tuple(jnp.zeros((1, 1), I32) for _ in range(group)))

    def prefix(mask):
        within = jnp.dot(mask.astype(BF16), upper, preferred_element_type=F32)
        total = within[:, LANES - 1:LANES]
        offs = jnp.dot(lower, jnp.broadcast_to(total, within.shape).astype(BF16), preferred_element_type=F32)
        return within, offs

    for x in range(group):
        gt = bits[x] > thr[x]
        eq = bits[x] == thr[x]
        need = cap - count(gt)
        w_eq, o_eq = prefix(eq)
        sel = gt | (eq & (o_eq + w_eq <= need))
        w_sel, o_sel = prefix(sel)
        pos_ref[x] = jnp.where(sel, o_sel + w_sel - 1.0, -1.0).astype(I32)
        off_ref[x] = o_sel.astype(I32)


def _select(aff_t, cap):
    ne, n = aff_t.shape
    rows = n // LANES
    upper = (jnp.arange(LANES)[:, None] <= jnp.arange(LANES)[None, :]).astype(BF16)
    lower = (jnp.arange(rows)[None, :] < jnp.arange(rows)[:, None]).astype(BF16)
    group = 4
    blk = pl.BlockSpec((group, rows, LANES), lambda e: (e, 0, 0))
    pos, off = pl.pallas_call(
        functools.partial(_select_body, cap=cap),
        grid=(ne // group,),
        in_specs=[blk, pl.BlockSpec((LANES, LANES), lambda e: (0, 0)),
                  pl.BlockSpec((rows, rows), lambda e: (0, 0))],
        out_specs=[blk, blk],
        out_shape=[jax.ShapeDtypeStruct((ne, rows, LANES), I32),
                   jax.ShapeDtypeStruct((ne, rows, LANES), I32)],
        compiler_params=_params(("parallel",), 32),
        name="ec_select",
    )(aff_t.reshape(ne, rows, LANES), upper, lower)
    return pos.reshape(ne, n), off[:, :, 0]


def _ffn_body(off_ref, nch_ref, pos_ref, x_ref, wg_ref, wu_ref, wd_ref, y_ref, xs_ref, *, nb, sub, cap):
    group = y_ref.shape[0]
    e0 = pl.program_id(0) * group
    sb = pl.program_id(1)

    @pl.when(sb == 0)
    def _():
        xs_ref[...] = jnp.zeros(xs_ref.shape, BF16)

    row = lax.broadcasted_iota(I32, (GATHER_ROWS, TOK_BLOCK), 0)

    def first_chunks(i):
        tok = slice(i * TOK_BLOCK, (i + 1) * TOK_BLOCK)
        starts = [pl.multiple_of(off_ref[(e0 + x) * nb + sb * sub + i] * ALIGN, ALIGN) for x in range(group)]
        hits = [(pos_ref[x, :, tok] - starts[x]) == row for x in range(group)]
        win = jnp.dot(jnp.concatenate(hits, axis=0).astype(BF16), x_ref[tok, :], preferred_element_type=F32)
        for x in range(group):
            xs_ref[x, pl.ds(starts[x], GATHER_ROWS), :] += win[x * GATHER_ROWS:(x + 1) * GATHER_ROWS].astype(BF16)

    def later_chunk(x, i, c):
        tok = slice(i * TOK_BLOCK, (i + 1) * TOK_BLOCK)
        start = pl.multiple_of(off_ref[(e0 + x) * nb + sb * sub + i] * ALIGN + c * GATHER_ROWS, ALIGN)
        hit = (pos_ref[x, :, tok] - start) == row
        win = jnp.dot(hit.astype(BF16), x_ref[tok, :], preferred_element_type=F32)
        xs_ref[x, pl.ds(start, GATHER_ROWS), :] += win.astype(BF16)

    for i in range(sub):
        first_chunks(i)
    for x in range(group):
        for i in range(sub):
            lax.fori_loop(1, nch_ref[(e0 + x) * nb + sb * sub + i],
                          lambda c, carry, x=x, i=i: (later_chunk(x, i, c), carry)[1], 0)

    @pl.when(sb == pl.num_programs(1) - 1)
    def _():
        for x in range(group):
            for j in range(cap // SLOT_TILE):
                rows = slice(j * SLOT_TILE, (j + 1) * SLOT_TILE)
                xt = xs_ref[x, rows, :]
                g = jnp.dot(xt, wg_ref[x], preferred_element_type=F32)
                u = jnp.dot(xt, wu_ref[x], preferred_element_type=F32)
                h = (g * jax.nn.sigmoid(g) * u).astype(BF16)
                y_ref[x, rows, :] = jnp.dot(h, wd_ref[x], preferred_element_type=F32).astype(BF16)
            y_ref[x, cap:, :] = jnp.zeros((y_ref.shape[1] - cap, D_MODEL), BF16)


def _ffn(base, nchunk, pos3, xn, wg, wu, wd, cap):
    ne = wg.shape[0]
    n = xn.shape[0]
    nb = n // TOK_BLOCK
    sub = min(4, nb)
    group = 2
    cap_pad = cap + TOK_BLOCK + ALIGN
    grid_spec = pltpu.PrefetchScalarGridSpec(
        num_scalar_prefetch=2,
        grid=(ne // group, nb // sub),
        in_specs=[pl.BlockSpec((group, 1, sub * TOK_BLOCK), lambda e, s, off, nch: (e, 0, s)),
                  pl.BlockSpec((sub * TOK_BLOCK, D_MODEL), lambda e, s, off, nch: (s, 0)),
                  pl.BlockSpec((group, D_MODEL, D_EXPERT), lambda e, s, off, nch: (e, 0, 0)),
                  pl.BlockSpec((group, D_MODEL, D_EXPERT), lambda e, s, off, nch: (e, 0, 0)),
                  pl.BlockSpec((group, D_EXPERT, D_MODEL), lambda e, s, off, nch: (e, 0, 0))],
        out_specs=pl.BlockSpec((group, cap_pad, D_MODEL), lambda e, s, off, nch: (e, 0, 0)),
        scratch_shapes=[pltpu.VMEM((group, cap_pad, D_MODEL), BF16)])
    return pl.pallas_call(
        functools.partial(_ffn_body, nb=nb, sub=sub, cap=cap),
        grid_spec=grid_spec,
        out_shape=jax.ShapeDtypeStruct((ne, cap_pad, D_MODEL), BF16),
        compiler_params=_params(("parallel", "arbitrary"), 52),
        name="ec_ffn",
    )(base, nchunk, pos3, xn, wg, wu, wd)


def _combine_body(off_ref, spill_ref, post_ref, affn_ref, xm_ref, *refs, nb, ne, win_rows):
    win_refs, tail_refs, o_ref = refs[:ne], refs[ne:-1], refs[-1]
    b = pl.program_id(0)
    post = post_ref[...]
    affn = affn_ref[...]

    def expand(y_refs, shift):
        width = y_refs[0].shape[0]
        lane = lax.broadcasted_iota(I32, (TOK_BLOCK, width), 1)
        total = jnp.zeros((TOK_BLOCK, D_MODEL), F32)
        for e in range(ne):
            rel = post[:, e:e + 1] - (off_ref[e * nb + b] * ALIGN + shift)
            hit = (rel == lane).astype(BF16)
            total = total + affn[:, e:e + 1] * jnp.dot(hit, y_refs[e][...], preferred_element_type=F32)
        return total

    def expand_pairs(y_refs):
        width = y_refs[0].shape[0]
        lane = lax.broadcasted_iota(I32, (TOK_BLOCK, 2 * width), 1)
        total = jnp.zeros((TOK_BLOCK, D_MODEL), F32)
        for e in range(0, ne, 2):
            rel0 = post[:, e:e + 1] - off_ref[e * nb + b] * ALIGN
            rel1 = post[:, e + 1:e + 2] - off_ref[(e + 1) * nb + b] * ALIGN + width
            gates = (jnp.where(rel0 == lane, affn[:, e:e + 1], 0.0)
                     + jnp.where((rel1 == lane) & (rel1 >= width), affn[:, e + 1:e + 2], 0.0))
            rows = jnp.concatenate([y_refs[e][...], y_refs[e + 1][...]], axis=0)
            total = total + jnp.dot(gates.astype(BF16), rows, preferred_element_type=F32)
        return total

    if not tail_refs and 2 * win_rows <= TOK_BLOCK and ne % 2 == 0:
        o_ref[...] = xm_ref[...] + expand_pairs(win_refs)
    else:
        o_ref[...] = xm_ref[...] + expand(win_refs, 0)

    if tail_refs:
        @pl.when(spill_ref[b] != 0)
        def _():
            o_ref[...] += expand(tail_refs, win_rows)


def _combine(base, spill, pos_t, aff_n, xm, y, win_rows):
    ne = y.shape[0]
    n = xm.shape[0]
    nb = n // TOK_BLOCK
    tail_rows = ALIGN if win_rows == TOK_BLOCK else 0

    def window(e, rows, shift):
        return pl.BlockSpec((pl.Squeezed(), pl.Element(rows), pl.Element(D_MODEL)),
                            lambda b, off, sp: (e, (off[e * nb + b] + shift // ALIGN) * ALIGN, 0))

    tails = [window(e, tail_rows, win_rows) for e in range(ne)] if tail_rows else []
    grid_spec = pltpu.PrefetchScalarGridSpec(
        num_scalar_prefetch=2,
        grid=(nb,),
        in_specs=([pl.BlockSpec((TOK_BLOCK, ne), lambda b, off, sp: (b, 0)),
                   pl.BlockSpec((TOK_BLOCK, ne), lambda b, off, sp: (b, 0)),
                   pl.BlockSpec((TOK_BLOCK, D_MODEL), lambda b, off, sp: (b, 0))]
                  + [window(e, win_rows, 0) for e in range(ne)] + tails),
        out_specs=pl.BlockSpec((TOK_BLOCK, D_MODEL), lambda b, off, sp: (b, 0)))
    return pl.pallas_call(
        functools.partial(_combine_body, nb=nb, ne=ne, win_rows=win_rows),
        grid_spec=grid_spec,
        out_shape=jax.ShapeDtypeStruct((n, D_MODEL), F32),
        compiler_params=_params(("parallel",), 48),
        name="ec_combine",
    )(base, spill, pos_t, aff_n, xm, *([y] * (ne + len(tails))))


def _ec_moe(xm, xn, aff_t, aff_n, wg, wu, wd):
    ne, n = aff_t.shape
    cap = max(1, EC_CAPACITY * n // N_EXPERTS)
    nb = n // TOK_BLOCK
    pos, off = _select(aff_t, cap)
    first = off[:, ::TOK_BLOCK // LANES]
    count = jnp.concatenate([first[:, 1:], jnp.full((ne, 1), cap, I32)], axis=1) - first
    base = first // ALIGN
    span = jnp.where(count > 0, first - base * ALIGN + count, 0)
    spill = jnp.any(span > TOK_BLOCK, axis=0).astype(I32)
    nchunk = ((span + GATHER_ROWS - 1) // GATHER_ROWS).reshape(ne * nb).astype(I32)
    base = base.reshape(ne * nb).astype(I32)
    y = _ffn(base, nchunk, pos.reshape(ne, 1, n), xn, wg, wu, wd, cap)
    pos_t = pos.T
    return lax.cond(jnp.max(span) <= COMBINE_FAST_ROWS,
                    lambda: _combine(base, spill, pos_t, aff_n, xm, y, COMBINE_FAST_ROWS),
                    lambda: _combine(base, spill, pos_t, aff_n, xm, y, TOK_BLOCK))


def _slot_cols(w, head_width):
    k = w.shape[0]
    w3 = w.reshape(k, N_HEADS, head_width)
    return jnp.pad(w3, ((0, 0), (0, 0), (0, SLOT - head_width))).reshape(k, N_HEADS * SLOT)


def _attn_weights(q_a_norm, w_uq, kv_a_norm, w_ukv, q_norm, k_norm):
    wkv = w_ukv.reshape(KV_LORA, N_HEADS, QK_NOPE + V_HEAD)
    wk = _slot_cols(wkv[:, :, :QK_NOPE].reshape(KV_LORA, -1), QK_NOPE)
    wv = wkv[:, :, QK_NOPE:].reshape(KV_LORA, D_ATTN)
    half = QK_ROPE // 2

    def rotate_half(a):
        return jnp.concatenate([jnp.zeros_like(a[..., :QK_NOPE]), -a[..., QK_NOPE + half:],
                                a[..., QK_NOPE:QK_NOPE + half]], axis=-1)

    def gains(g):
        swapped = jnp.concatenate([jnp.zeros_like(g[:QK_NOPE]), g[QK_NOPE + half:], g[QK_NOPE:QK_NOPE + half]])
        return jnp.pad(jnp.stack([g, swapped]), ((0, 0), (0, SLOT - QK_HEAD)))

    wq3 = w_uq.reshape(Q_LORA, N_HEADS, QK_HEAD)
    return dict(wq=_slot_cols(w_uq, QK_HEAD).astype(BF16),
                wq_rot=_slot_cols(rotate_half(wq3).reshape(Q_LORA, -1), QK_HEAD).astype(BF16),
                wk=wk.astype(BF16), wv=wv.astype(BF16), qa=q_a_norm[None, :], kva=kv_a_norm[None, :],
                qg=gains(q_norm), kg=gains(k_norm))


def _rope_tables(seq):
    pos = jnp.arange(seq, dtype=F32)
    inv_freq = ROPE_THETA ** (-jnp.arange(0, QK_ROPE, 2, dtype=F32) / QK_ROPE)
    ang = pos[:, None] * inv_freq
    ang = jnp.concatenate([ang, ang], axis=-1)
    pad = lambda t, fill: jnp.concatenate(
        [jnp.full((seq, QK_NOPE), fill, F32), t, jnp.full((seq, SLOT - QK_HEAD), fill, F32)], axis=1)
    return pad(jnp.cos(ang), 1.0), pad(jnp.sin(ang), 0.0)


def _hi_lo(w):
    hi = w.astype(BF16)
    return jnp.stack([hi, (w - hi.astype(F32)).astype(BF16)], axis=1)


def _filter_weights(w1, b1, w2, b2, w3, freq, decay):
    hid, nbands = FILTER_HIDDEN, FILTER_BANDS
    assert 2 * hid == LANES and 2 * nbands <= hid
    both = lambda row: jnp.concatenate([row, row])[None, :]
    blockdiag = lambda a: jnp.concatenate(
        [jnp.concatenate([a, jnp.zeros_like(a)], axis=1), jnp.concatenate([jnp.zeros_like(a), a], axis=1)], axis=0)
    bands = jnp.linspace(1e-4, nbands - 1, nbands, dtype=F32)
    rest = jnp.zeros((hid - 2 * nbands,), F32)
    band_half = jnp.concatenate([bands, bands, rest])
    phase_half = jnp.concatenate([jnp.zeros((nbands,), F32), jnp.full((nbands,), 0.5 * math.pi, F32), rest])
    w1_half = jnp.concatenate([-w1[1 + nbands:], w1[1:1 + nbands], jnp.zeros((hid - 2 * nbands, hid), F32)], axis=0)
    cols = HYENA_ORDER * D_HYENA
    w3d = jnp.transpose(w3.reshape(hid, N_DIR, cols), (1, 0, 2))
    w3p = jnp.stack([jnp.pad(w3d[0], ((0, hid), (0, 0))), jnp.pad(w3d[1], ((hid, 0), (0, 0)))], axis=0)
    w3_hi = w3p.astype(BF16)
    w3_lo = (w3p - w3_hi.astype(F32)).astype(BF16)
    return dict(bands=both(band_half), phase=both(phase_half), w1t=both(w1[0]), w1=blockdiag(w1_half),
                b1=both(b1), w2=blockdiag(w2), b2=both(b2), freq=both(freq), w3_hi=w3_hi, w3_lo=w3_lo,
                decay=decay.reshape(N_DIR, 1, cols))


def _trunk(x, p):
    bsz, seq, _ = x.shape
    n = bsz * seq
    cos_t, sin_t = _rope_tables(seq)
    x2 = x.reshape(n, D_MODEL)
    for l in range(DEPTH):
        uh, cq, ckvr = _inproj(x2, p["attn_norm"][l][None, :], p["w_in"][l])
        y_h = _hyena(uh.reshape(bsz, seq, COL_HYENA), p["conv_w"][l], p["conv_b"][l][None, :],
                     p["filt"][l], p["hyena_bias"][l])
        qt, k, vt = _qkv(cq.reshape(bsz, seq, -1), ckvr.reshape(bsz, seq, -1), cos_t, sin_t, p["attn"][l])
        y_a = _flash(qt, k, vt)
        xm, xn, aff_t, aff_n = _outproj(y_h.reshape(n, D_HYENA), y_a.reshape(n, D_ATTN), x2,
                                 p["out_norm"][l][None, :], p["w_out"][l], p["mlp_norm"][l][None, :],
                                 p["w_router"][l])
        x2 = _ec_moe(xm, xn, aff_t, aff_n, p["w_gate"][l], p["w_up"][l], p["w_down"][l])
    return x2.reshape(bsz, seq, D_MODEL)


def kernel(x_prompt, x_sample, attn_norm, w_in, conv_w, conv_b, filt_w1, filt_b1, filt_w2, filt_b2, filt_w3,
           filt_freq, filt_decay, hyena_bias, q_a_norm, w_uq, kv_a_norm, w_ukv, q_norm, k_norm, out_norm,
           w_out, mlp_norm, w_router, w_gate, w_up, w_down):
    p = dict(
        attn_norm=attn_norm,
        w_in=jnp.pad(w_in, ((0, 0), (0, 0), (0, D_IN_PAD - D_IN))).astype(BF16),
        conv_w=conv_w, conv_b=conv_b, hyena_bias=hyena_bias,
        filt=[_filter_weights(filt_w1[l], filt_b1[l], filt_w2[l], filt_b2[l], filt_w3[l], filt_freq[l],
                              filt_decay[l]) for l in range(DEPTH)],
        attn=[_attn_weights(q_a_norm[l], w_uq[l], kv_a_norm[l], w_ukv[l], q_norm[l], k_norm[l])
              for l in range(DEPTH)],
        out_norm=out_norm, w_out=w_out.astype(BF16), mlp_norm=mlp_norm,
        w_router=_hi_lo(jnp.pad(w_router, ((0, 0), (0, 0), (0, LANES - N_EXPERTS)))),
        w_gate=w_gate.astype(BF16), w_up=w_up.astype(BF16), w_down=w_down.astype(BF16))
    return (_trunk(x_prompt, p), _trunk(x_sample, p))
```

```python
import functools
import math

import jax
import jax.numpy as jnp
from jax import lax
from jax.experimental import pallas as pl
from jax.experimental.pallas import tpu as pltpu

F32 = jnp.float32
BF16 = jnp.bfloat16
I32 = jnp.int32
HIGHEST = lax.Precision.HIGHEST

D_MODEL = 1024
DEPTH = 2
D_HYENA = 512
HYENA_ORDER = 2
FILTER_BANDS = 16
FILTER_HIDDEN = 64
N_DIR = 2
DECAY_SHIFT = 0.05
N_HEADS = 8
QK_NOPE = 64
QK_ROPE = 32
QK_HEAD = QK_NOPE + QK_ROPE
V_HEAD = 64
V_AUG = V_HEAD + 16
D_ATTN = N_HEADS * V_HEAD
Q_LORA = 256
KV_LORA = 128
ROPE_THETA = 10000.0
N_EXPERTS = 16
EC_CAPACITY = 2
D_EXPERT = 512
EPS = 1e-6
COL_HYENA = (HYENA_ORDER + 1) * D_HYENA
COL_Q = COL_HYENA + Q_LORA
COL_KV = COL_Q + KV_LORA
D_IN = COL_KV + QK_ROPE
D_IN_PAD = 2048

LANES = 128
SLOT = 128
TOK_BLOCK = 256
SLOT_TILE = 256
ALIGN = 16
GATHER_ROWS = 64
COMBINE_FAST_ROWS = 128
OUTPROJ_CHUNKS = 2
ATT_TQ = 512
ATT_QC = 256
ATT_TK = 256
ATT_UNROLL = 32
MIB = 1024 * 1024


def _params(sem, vmem_mib):
    return pltpu.CompilerParams(dimension_semantics=sem, vmem_limit_bytes=vmem_mib * MIB)


def _rms(x):
    return x * lax.rsqrt(jnp.mean(x * x, axis=-1, keepdims=True) + EPS)


def _inproj_body(x_ref, g_ref, w_ref, uh_ref, cq_ref, ckvr_ref):
    xn = _rms(x_ref[...]) * g_ref[...]
    u = jnp.dot(xn.astype(BF16), w_ref[...], preferred_element_type=F32)
    uh_ref[...] = u[:, :COL_HYENA]
    cq_ref[...] = u[:, COL_HYENA:COL_Q]
    ckvr_ref[...] = u[:, COL_Q:]


def _inproj(x2d, g, w_pad):
    n = x2d.shape[0]
    tm = min(512, n)
    return pl.pallas_call(
        _inproj_body,
        grid=(n // tm,),
        in_specs=[pl.BlockSpec((tm, D_MODEL), lambda i: (i, 0)),
                  pl.BlockSpec((1, D_MODEL), lambda i: (0, 0)),
                  pl.BlockSpec((D_MODEL, D_IN_PAD), lambda i: (0, 0))],
        out_specs=[pl.BlockSpec((tm, COL_HYENA), lambda i: (i, 0)),
                   pl.BlockSpec((tm, Q_LORA), lambda i: (i, 0)),
                   pl.BlockSpec((tm, D_IN_PAD - COL_Q), lambda i: (i, 0))],
        out_shape=[jax.ShapeDtypeStruct((n, COL_HYENA), F32),
                   jax.ShapeDtypeStruct((n, Q_LORA), F32),
                   jax.ShapeDtypeStruct((n, D_IN_PAD - COL_Q), F32)],
        compiler_params=_params(("parallel",), 40),
        name="inproj",
    )(x2d, g, w_pad)


def _sconv_body(u_ref, prev_ref, next_ref, w_ref, b_ref, v_ref, x1_ref, x2_ref):
    i = pl.program_id(1)
    last = pl.num_programs(1) - 1
    u = u_ref[0]
    tl = u.shape[0]
    prev_row = jnp.where(i == 0, 0.0, prev_ref[0, 7:8, :])
    next_row = jnp.where(i == last, 0.0, next_ref[0, 0:1, :])
    row = lax.broadcasted_iota(I32, u.shape, 0)
    up = jnp.where(row == 0, prev_row, pltpu.roll(u, 1, axis=0))
    dn = jnp.where(row == tl - 1, next_row, pltpu.roll(u, tl - 1, axis=0))
    y = up * w_ref[0:1, :] + u * w_ref[1:2, :] + dn * w_ref[2:3, :] + b_ref[...]
    rows, flat = v_ref.shape[1:]
    for k, o_ref in enumerate((v_ref, x1_ref, x2_ref)):
        o_ref[0] = y[:, k * D_HYENA:(k + 1) * D_HYENA].reshape(rows, tl // rows, D_HYENA).reshape(rows, flat)


def _sconv(uh, w, b, n2):
    bsz, seq, c = uh.shape
    rows = 8
    tl = rows * n2
    r = tl // 8
    nblk8 = seq // 8
    out = jax.ShapeDtypeStruct((bsz, seq // n2, n2 * D_HYENA), F32)
    ospec = pl.BlockSpec((1, rows, n2 * D_HYENA), lambda bi, i: (bi, i, 0))
    return pl.pallas_call(
        _sconv_body,
        grid=(bsz, seq // tl),
        in_specs=[pl.BlockSpec((1, tl, c), lambda bi, i: (bi, i, 0)),
                  pl.BlockSpec((1, 8, c), lambda bi, i: (bi, jnp.maximum(i * r - 1, 0), 0)),
                  pl.BlockSpec((1, 8, c), lambda bi, i: (bi, jnp.minimum((i + 1) * r, nblk8 - 1), 0)),
                  pl.BlockSpec((3, c), lambda bi, i: (0, 0)),
                  pl.BlockSpec((1, c), lambda bi, i: (0, 0))],
        out_specs=[ospec, ospec, ospec],
        out_shape=[out, out, out],
        compiler_params=_params(("parallel", "parallel"), 40),
        name="sconv",
    )(uh, uh, uh, w, b)


def _filter_body(bands_ref, phase_ref, w1t_ref, w1_ref, b1_ref, w2_ref, b2_ref, fr_ref, w3h_ref, w3l_ref,
                 dec_ref, k_ref, sum_ref, *, seq, n2):
    i = pl.program_id(0)
    hp = k_ref.shape[0] // 2
    cols = sum_ref.shape[1]
    jb = k_ref.shape[1] // cols
    row = lax.broadcasted_iota(I32, (jb * hp, 1), 0)
    n_lo = (row % hp) * n2 + i * jb + row // hp
    n_hi = n_lo + seq
    t_of = lambda n: jnp.where(n < seq, n, 2 * seq - n).astype(F32)
    lane = lax.broadcasted_iota(I32, (jb * hp, LANES), 1)
    t_idx = jnp.where(lane < FILTER_HIDDEN, t_of(n_lo), t_of(n_hi))
    feats = jnp.sin(2.0 * math.pi * bands_ref[...] * t_idx / seq + phase_ref[...])
    fr = fr_ref[...]
    pre = (t_idx / (seq - 1)) * w1t_ref[...] + jnp.dot(feats, w1_ref[...], precision=HIGHEST,
                                                      preferred_element_type=F32)
    h = jnp.sin(fr * (pre + b1_ref[...]))
    h = jnp.sin(fr * (jnp.dot(h, w2_ref[...], precision=HIGHEST, preferred_element_type=F32) + b2_ref[...]))
    h_hi = h.astype(BF16)
    h_lo = (h - h_hi.astype(F32)).astype(BF16)

    @pl.when(i == 0)
    def _():
        sum_ref[...] = jnp.zeros_like(sum_ref)

    for j, n in enumerate((n_lo, n_hi)):
        out = (jnp.dot(h_hi, w3h_ref[j], preferred_element_type=F32)
               + jnp.dot(h_lo, w3h_ref[j], preferred_element_type=F32)
               + jnp.dot(h_hi, w3l_ref[j], preferred_element_type=F32))
        window = jnp.exp(-(t_of(n) / (seq - 1)) * dec_ref[j]) + DECAY_SHIFT
        k = jnp.where(n == seq, 0.0, out * window)
        sum_ref[...] += jnp.sum(jnp.abs(k), axis=0, keepdims=True)
        for jj in range(jb):
            k_ref[j * hp:(j + 1) * hp, jj * cols:(jj + 1) * cols] = k[jj * hp:(jj + 1) * hp].astype(BF16)


def _filters(seq, n1, n2, fw):
    cols = HYENA_ORDER * D_HYENA
    const = lambda shape: pl.BlockSpec(shape, lambda i: (0,) * len(shape))
    jb = max(1, 1024 // n1)
    return pl.pallas_call(
        functools.partial(_filter_body, seq=seq, n2=n2),
        grid=(n2 // jb,),
        in_specs=[const((1, LANES)), const((1, LANES)), const((1, LANES)), const((LANES, LANES)),
                  const((1, LANES)), const((LANES, LANES)), const((1, LANES)), const((1, LANES)),
                  const((N_DIR, LANES, cols)), const((N_DIR, LANES, cols)), const((N_DIR, 1, cols))],
        out_specs=[pl.BlockSpec((n1, jb * cols), lambda i: (0, i)),
                   pl.BlockSpec((1, cols), lambda i: (0, 0))],
        out_shape=[jax.ShapeDtypeStruct((n1, n2 * cols), BF16),
                   jax.ShapeDtypeStruct((1, cols), F32)],
        compiler_params=_params(("arbitrary",), 32),
        name="filter_gen",
    )(fw["bands"], fw["phase"], fw["w1t"], fw["w1"], fw["b1"], fw["w2"], fw["b2"], fw["freq"],
      fw["w3_hi"], fw["w3_lo"], fw["decay"])


def _fft_dims(seq):
    n2 = 128 if 2 * seq >= 32768 else 64
    n1 = 2 * seq // n2
    return n1, n2


def _dft_tables(n1, n2):
    n = n1 * n2
    n1h = n1 // 2
    k1 = jnp.arange(n1h, dtype=I32)[:, None]
    m1 = jnp.arange(n1, dtype=I32)[None, :]
    ang = (2.0 * math.pi / n1) * ((k1 * m1) % n1).astype(F32)
    top = jnp.cos(ang)
    bot = -jnp.sin(ang)
    nyq = jnp.where(m1 % 2 == 0, 1.0, -1.0).astype(F32)
    bot = jnp.concatenate([nyq, bot[1:]], axis=0)
    fa = jnp.concatenate([top, bot], axis=0)
    weight = jnp.where((jnp.arange(n1) % n1h) == 0, 1.0, 2.0).astype(F32) / n
    fi = (fa[:, :n1h] * weight[:, None]).T

    kk = jnp.arange(n1h + 1, dtype=I32)[:, None, None]
    k2 = jnp.arange(n2, dtype=I32)[None, :, None]
    m2 = jnp.arange(n2, dtype=I32)[None, None, :]
    phi = (2.0 * math.pi / n) * ((m2 * (kk + n1 * k2)) % n).astype(F32)
    gr = jnp.cos(phi)
    gi = -jnp.sin(phi)
    blk = jnp.concatenate([jnp.concatenate([gr, -gi], axis=2),
                           jnp.concatenate([gi, gr], axis=2)], axis=1)
    left = (jnp.arange(2 * n2) < n2)[None, :]
    g0 = jnp.where(left, blk[0], 0.0)
    gf = jnp.concatenate([g0[None], blk[1:n1h]], axis=0)
    gnyq = jnp.concatenate([jnp.zeros((2 * n2, n2), F32), blk[n1h][:, :n2]], axis=1)
    return dict(fa_full=fa.astype(BF16), fa_half=fa[:, :n1h].astype(BF16), fi=fi.astype(BF16),
                gf=gf.astype(BF16), gnyq=gnyq.astype(BF16),
                ginv=jnp.transpose(gf, (0, 2, 1)).astype(BF16), m2=gnyq.T.astype(BF16))


def _fft_a_body(x_ref, f_ref, o_ref):
    a = jnp.dot(f_ref[...], x_ref[0].astype(BF16), preferred_element_type=F32)
    n1h = o_ref.shape[2]
    o_ref[0, 0] = a[:n1h].astype(BF16)
    o_ref[0, 1] = a[n1h:].astype(BF16)


def _fft_a(x3, fmat):
    bsz, r, nc = x3.shape
    n1 = fmat.shape[0]
    tn = min(8192, nc)
    return pl.pallas_call(
        _fft_a_body,
        grid=(bsz, nc // tn),
        in_specs=[pl.BlockSpec((1, r, tn), lambda b, j: (b, 0, j)),
                  pl.BlockSpec((n1, r), lambda b, j: (0, 0))],
        out_specs=pl.BlockSpec((1, 2, n1 // 2, tn), lambda b, j: (b, 0, 0, j)),
        out_shape=jax.ShapeDtypeStruct((bsz, 2, n1 // 2, nc), BF16),
        compiler_params=_params(("parallel", "parallel"), 48),
        name="fft_stage_a",
    )(x3, fmat)


def _fft_b_body(ar_ref, ai_ref, g_ref, gn_ref, sc_ref, x_ref, xn_ref, *, kb, n2):
    j = pl.program_id(1)
    c = sc_ref.shape[1]
    inv = 1.0 / sc_ref[...]
    a_re = ar_ref[0, 0].reshape(kb, n2, c)
    a_im = ai_ref[0, 0].reshape(kb, n2, c)
    for kk in range(kb):
        rhs = jnp.concatenate([a_re[kk], a_im[kk]], axis=0)
        x_ref[0, kk] = (jnp.dot(g_ref[kk], rhs, preferred_element_type=F32) * inv).astype(BF16)

    @pl.when(j == 0)
    def _():
        rhs = jnp.concatenate([a_re[0], a_im[0]], axis=0)
        xn_ref[0] = (jnp.dot(gn_ref[...], rhs, preferred_element_type=F32) * inv).astype(BF16)


def _fft_b(a4, tabs, scale):
    bsz, _, n1h, nc = a4.shape
    c = scale.shape[1]
    n2 = nc // c
    kb = 16
    return pl.pallas_call(
        functools.partial(_fft_b_body, kb=kb, n2=n2),
        grid=(bsz, n1h // kb),
        in_specs=[pl.BlockSpec((1, 1, kb, nc), lambda b, j: (b, 0, j, 0)),
                  pl.BlockSpec((1, 1, kb, nc), lambda b, j: (b, 1, j, 0)),
                  pl.BlockSpec((kb, 2 * n2, 2 * n2), lambda b, j: (j, 0, 0)),
                  pl.BlockSpec((2 * n2, 2 * n2), lambda b, j: (0, 0)),
                  pl.BlockSpec((1, c), lambda b, j: (0, 0))],
        out_specs=[pl.BlockSpec((1, kb, 2 * n2, c), lambda b, j: (b, j, 0, 0)),
                   pl.BlockSpec((1, 2 * n2, c), lambda b, j: (b, 0, 0))],
        out_shape=[jax.ShapeDtypeStruct((bsz, n1h, 2 * n2, c), BF16),
                   jax.ShapeDtypeStruct((bsz, 2 * n2, c), BF16)],
        compiler_params=_params(("parallel", "arbitrary"), 48),
        name="fft_stage_b",
    )(a4, a4, tabs["gf"], tabs["gnyq"], scale)


def _cmul(x, k, n2):
    xr, xi = x[:n2], x[n2:]
    kr, ki = k[:n2], k[n2:]
    return jnp.concatenate([xr * kr - xi * ki, xr * ki + xi * kr], axis=0).astype(BF16)


def _spec_body(ar_ref, ai_ref, k_ref, kn_ref, g_ref, gn_ref, gi_ref, m2_ref, o_ref, *, kb, n2):
    j = pl.program_id(1)
    c = ar_ref.shape[3] // n2
    a_re = ar_ref[0, 0].reshape(kb, n2, c)
    a_im = ai_ref[0, 0].reshape(kb, n2, c)

    def through(fwd, rhs, kf, inv):
        x = jnp.dot(fwd, rhs, preferred_element_type=F32)
        return jnp.dot(inv, _cmul(x, kf.astype(F32), n2), preferred_element_type=F32)

    re_rows, im_rows = [], []
    for k0 in range(0, kb, 2):
        pair = (k0, k0 + 1)
        rhs = [jnp.concatenate([a_re[kk], a_im[kk]], axis=0) for kk in pair]
        xs = [jnp.dot(g_ref[kk], r, preferred_element_type=F32) for kk, r in zip(pair, rhs)]
        ys = [_cmul(x, k_ref[0, kk].astype(F32), n2) for kk, x in zip(pair, xs)]
        accs = [jnp.dot(gi_ref[kk], y, preferred_element_type=F32) for kk, y in zip(pair, ys)]
        if k0 == 0:
            nyq = through(gn_ref[...], rhs[0], kn_ref[0], m2_ref[...])
            accs[0] = jnp.where(j == 0, jnp.concatenate([accs[0][:n2], nyq[n2:]], axis=0), accs[0])
        for acc in accs:
            re_rows.append(acc[:n2].astype(BF16))
            im_rows.append(acc[n2:].astype(BF16))
    o_ref[0, 0] = jnp.stack(re_rows, axis=0).reshape(kb, n2 * c)
    o_ref[0, 1] = jnp.stack(im_rows, axis=0).reshape(kb, n2 * c)


def _spec_conv(a4, kf, kfnyq, order, tabs):
    bsz, _, n1h, nc = a4.shape
    tn2 = kf.shape[2]
    n2 = tn2 // 2
    c = nc // n2
    kb = 16
    return pl.pallas_call(
        functools.partial(_spec_body, kb=kb, n2=n2),
        grid=(bsz, n1h // kb),
        in_specs=[pl.BlockSpec((1, 1, kb, nc), lambda b, j: (b, 0, j, 0)),
                  pl.BlockSpec((1, 1, kb, nc), lambda b, j: (b, 1, j, 0)),
                  pl.BlockSpec((1, kb, tn2, c), lambda b, j: (0, j, 0, order)),
                  pl.BlockSpec((1, tn2, c), lambda b, j: (0, 0, order)),
                  pl.BlockSpec((kb, tn2, tn2), lambda b, j: (j, 0, 0)),
                  pl.BlockSpec((tn2, tn2), lambda b, j: (0, 0)),
                  pl.BlockSpec((kb, tn2, tn2), lambda b, j: (j, 0, 0)),
                  pl.BlockSpec((tn2, tn2), lambda b, j: (0, 0))],
        out_specs=pl.BlockSpec((1, 2, kb, nc), lambda b, j: (b, 0, j, 0)),
        out_shape=jax.ShapeDtypeStruct((bsz, 2, n1h, nc), BF16),
        compiler_params=_params(("parallel", "arbitrary"), 48),
        name="spectral_conv",
    )(a4, a4, kf, kfnyq, tabs["gf"], tabs["gnyq"], tabs["ginv"], tabs["m2"])


def _ifft_a_body(b_ref, f_ref, z_ref, gate_ref, bias_ref, *refs):
    y = jnp.dot(f_ref[...], b_ref[0], preferred_element_type=F32)
    res = gate_ref[0] * (y + z_ref[0] * bias_ref[...])
    if len(refs) == 1:
        refs[0][0] = res.reshape(refs[0].shape[1:])
    else:
        fa_ref, o_ref, a_ref = refs
        o_ref[0] = res
        a = jnp.dot(fa_ref[...], res.astype(BF16), preferred_element_type=F32)
        n1h = a_ref.shape[2]
        a_ref[0, 0] = a[:n1h].astype(BF16)
        a_ref[0, 1] = a[n1h:].astype(BF16)


def _ifft_a(b3, fi, z3, gate3, bias_t, next_fa=None):
    bsz, n1, nc = b3.shape
    n1h = n1 // 2
    tn = bias_t.shape[1]
    c = D_HYENA
    in_specs = [pl.BlockSpec((1, n1, tn), lambda b, j: (b, 0, j)),
                pl.BlockSpec((n1h, n1), lambda b, j: (0, 0)),
                pl.BlockSpec((1, n1h, tn), lambda b, j: (b, 0, j)),
                pl.BlockSpec((1, n1h, tn), lambda b, j: (b, 0, j)),
                pl.BlockSpec((1, tn), lambda b, j: (0, 0))]
    args = [b3, fi, z3, gate3, bias_t]
    if next_fa is None:
        out_specs = pl.BlockSpec((1, n1h, tn // c, c), lambda b, j: (b, 0, j, 0))
        out_shape = jax.ShapeDtypeStruct((bsz, n1h, nc // c, c), F32)
    else:
        in_specs.append(pl.BlockSpec((n1, n1h), lambda b, j: (0, 0)))
        args.append(next_fa)
        out_specs = [pl.BlockSpec((1, n1h, tn), lambda b, j: (b, 0, j)),
                     pl.BlockSpec((1, 2, n1h, tn), lambda b, j: (b, 0, 0, j))]
        out_shape = [jax.ShapeDtypeStruct((bsz, n1h, nc), F32),
                     jax.ShapeDtypeStruct((bsz, 2, n1h, nc), BF16)]
    return pl.pallas_call(
        _ifft_a_body,
        grid=(bsz, nc // tn),
        in_specs=in_specs,
        out_specs=out_specs,
        out_shape=out_shape,
        compiler_params=_params(("parallel", "parallel"), 52),
        name="ifft_stage_a",
    )(*args)


def _hyena(uh, conv_w, conv_b, fw, bias):
    bsz, seq, _ = uh.shape
    c = D_HYENA
    n1, n2 = _fft_dims(seq)
    n1h = n1 // 2
    tabs = _dft_tables(n1, n2)
    v, x1, x2 = _sconv(uh, conv_w, conv_b, n2)

    kcirc, ksum = _filters(seq, n1, n2, fw)
    cols = HYENA_ORDER * c
    ka = _fft_a(kcirc[None], tabs["fa_full"])
    kf, kfnyq = _fft_b(ka, tabs, ksum)

    tn = min(8192, n2 * c)
    z = v
    za = _fft_a(z, tabs["fa_half"])
    for order, gate in enumerate((x1, x2)):
        zb = _spec_conv(za, kf, kfnyq, order, tabs)
        bias_t = jnp.tile(bias[order][None, :], (1, tn // c))
        if order < HYENA_ORDER - 1:
            z, za = _ifft_a(zb.reshape(bsz, n1, n2 * c), tabs["fi"], z, gate, bias_t, tabs["fa_half"])
        else:
            z = _ifft_a(zb.reshape(bsz, n1, n2 * c), tabs["fi"], z, gate, bias_t)
    return z.reshape(bsz, seq, c)


def _qkv_body(cq_ref, ckvr_ref, cos_ref, sin_ref, wq_ref, wqr_ref, wk_ref, wv_ref, qa_ref, kva_ref, qg_ref,
              kg_ref, qt_ref, k_ref, vt_ref):
    cqn = _rms(cq_ref[0]) * qa_ref[...]
    qs = jnp.dot(cqn.astype(BF16), wq_ref[...], preferred_element_type=F32)
    ck = ckvr_ref[0]
    ckvn = (_rms(ck[:, :KV_LORA]) * kva_ref[...]).astype(BF16)
    ks = jnp.dot(ckvn, wk_ref[...], preferred_element_type=F32)
    vs = jnp.dot(ckvn, wv_ref[...], preferred_element_type=F32)
    krope = pltpu.roll(ck[:, KV_LORA:], QK_NOPE, axis=1)
    cos = cos_ref[...]
    sin = sin_ref[...]
    lane = lax.broadcasted_iota(I32, cos.shape, 1)
    half = QK_ROPE // 2
    first = (lane >= QK_NOPE) & (lane < QK_NOPE + half)
    second = (lane >= QK_NOPE + half) & (lane < QK_HEAD)

    def head(xh, xh_rot, gain_cos, gain_sin):
        rs = lax.rsqrt(jnp.sum(xh * xh, axis=-1, keepdims=True) * (1.0 / QK_HEAD) + EPS)
        return rs * (xh * gain_cos + xh_rot * gain_sin)

    q_scale = QK_HEAD ** -0.5 * math.log2(math.e)
    q_cos, q_sin = qg_ref[0:1, :] * cos * q_scale, qg_ref[1:2, :] * sin * q_scale
    k_cos, k_sin = kg_ref[0:1, :] * cos, kg_ref[1:2, :] * sin
    qs_rot = jnp.dot(cqn.astype(BF16), wqr_ref[...], preferred_element_type=F32)
    krope_rot = jnp.where(first, -pltpu.roll(krope, SLOT - half, axis=1),
                          jnp.where(second, pltpu.roll(krope, half, axis=1), 0.0))
    eye = (lax.broadcasted_iota(I32, (SLOT, SLOT), 0) == lax.broadcasted_iota(I32, (SLOT, SLOT), 1)).astype(BF16)
    transpose = lambda x: lax.dot_general(eye, x.astype(BF16), (((1,), (1,)), ((), ())),
                                          preferred_element_type=F32).astype(BF16)
    for h in range(N_HEADS):
        sl = slice(h * SLOT, (h + 1) * SLOT)
        qt_ref[0, h] = transpose(head(qs[:, sl], qs_rot[:, sl], q_cos, q_sin))
        k_ref[0, h] = head(ks[:, sl] + krope, krope_rot, k_cos, k_sin).astype(BF16)
    aug = lax.broadcasted_iota(I32, (V_AUG - V_HEAD, cos.shape[0]), 0)
    ones_row = jnp.where(aug == 0, 1.0, 0.0).astype(BF16)
    for hp in range(N_HEADS // 2):
        pair_t = transpose(vs[:, hp * 2 * V_HEAD:(hp + 1) * 2 * V_HEAD])
        vt_ref[0, hp] = jnp.concatenate([pair_t[:V_HEAD], ones_row, pair_t[V_HEAD:], ones_row], axis=0)


def _qkv(cq, ckvr, cos_t, sin_t, aw):
    bsz, seq, _ = cq.shape
    tm = min(512, seq)
    const = lambda shape: pl.BlockSpec(shape, lambda b, i: (0,) * len(shape))
    return pl.pallas_call(
        _qkv_body,
        grid=(bsz, seq // tm),
        in_specs=[pl.BlockSpec((1, tm, Q_LORA), lambda b, i: (b, i, 0)),
                  pl.BlockSpec((1, tm, D_IN_PAD - COL_Q), lambda b, i: (b, i, 0)),
                  pl.BlockSpec((tm, SLOT), lambda b, i: (i, 0)),
                  pl.BlockSpec((tm, SLOT), lambda b, i: (i, 0)),
                  const((Q_LORA, N_HEADS * SLOT)), const((Q_LORA, N_HEADS * SLOT)),
                  const((KV_LORA, N_HEADS * SLOT)), const((KV_LORA, D_ATTN)),
                  const((1, Q_LORA)), const((1, KV_LORA)), const((2, SLOT)), const((2, SLOT))],
        out_specs=[pl.BlockSpec((1, N_HEADS, SLOT, tm), lambda b, i: (b, 0, 0, i)),
                   pl.BlockSpec((1, N_HEADS, tm, SLOT), lambda b, i: (b, 0, i, 0)),
                   pl.BlockSpec((1, N_HEADS // 2, 2 * V_AUG, tm), lambda b, i: (b, 0, 0, i))],
        out_shape=[jax.ShapeDtypeStruct((bsz, N_HEADS, SLOT, seq), BF16),
                   jax.ShapeDtypeStruct((bsz, N_HEADS, seq, SLOT), BF16),
                   jax.ShapeDtypeStruct((bsz, N_HEADS // 2, 2 * V_AUG, seq), BF16)],
        compiler_params=_params(("parallel", "parallel"), 40),
        name="qkv_prep",
    )(cq, ckvr, cos_t, sin_t, aw["wq"], aw["wq_rot"], aw["wk"], aw["wv"], aw["qa"], aw["kva"], aw["qg"],
      aw["kg"])


def _flash_body(qt_ref, k_ref, vt_ref, o_ref, s0_ref, s1_ref, acc_ref, *, tk, nk, nqc, unroll):
    chains = [(hh, qc) for hh in range(2) for qc in range(nqc)]
    acc_ref[...] = jnp.zeros(acc_ref.shape, F32)

    def qk(t, s_ref, only=None):
        ks = pl.multiple_of(t * tk, tk)
        tile_max = []
        for c, (hh, qc) in enumerate(chains):
            if only is not None and c != only:
                continue
            k = k_ref[0, hh, pl.ds(ks, tk), :]
            s = jnp.dot(k, qt_ref[0, hh, :, qc * ATT_QC:(qc + 1) * ATT_QC], preferred_element_type=F32)
            s_ref[c] = s
            tile_max.append(jnp.max(s, axis=0, keepdims=True))
        return tile_max

    def softmax_pv(t, s_ref, tile_max, m, only=None):
        ks = pl.multiple_of(t * tk, tk)
        m_out = []
        for c, (hh, qc) in enumerate(chains):
            if only is not None and c != only:
                continue
            i = 0 if only is not None else c
            m_new = jnp.maximum(m[i], tile_max[i])
            a = jnp.exp2(m[i] - m_new)
            p = jnp.exp2(s_ref[c] - m_new)
            m_out.append(m_new)
            vt = vt_ref[0, 0, hh * V_AUG:(hh + 1) * V_AUG, pl.ds(ks, tk)]
            acc_ref[c] = acc_ref[c] * a + jnp.dot(vt, p.astype(BF16), preferred_element_type=F32)
        return m_out

    def group(u, carry):
        tile_max, m = carry
        for i in range(0, unroll, 2):
            t = unroll * u + i
            tm1, tm0 = [], []
            m = list(m)
            for c in range(len(chains)):
                tm1 += qk(t + 1, s1_ref, only=c)
                m[c] = softmax_pv(t, s0_ref, [tile_max[c]], [m[c]], only=c)[0]
            for c in range(len(chains)):
                tm0 += qk(jnp.minimum(t + 2, nk - 1), s0_ref, only=c)
                m[c] = softmax_pv(t + 1, s1_ref, [tm1[c]], [m[c]], only=c)[0]
            tile_max = tm0
        return tile_max, m

    init = (qk(0, s0_ref), [jnp.full((1, ATT_QC), -jnp.inf, F32)] * len(chains))
    lax.fori_loop(0, nk // unroll, group, init)

    def normalised(c):
        acc = acc_ref[c]
        return acc[:V_HEAD] * (1.0 / acc[V_HEAD:V_HEAD + 1])

    heads = [jnp.concatenate([normalised(hh * nqc + qc) for qc in range(nqc)], axis=1) for hh in range(2)]
    o_ref[0] = jnp.concatenate(heads, axis=0).T


def _flash(qt, k, vt):
    bsz, nh, _, seq = qt.shape
    tq = min(ATT_TQ, seq)
    tk = min(ATT_TK, seq)
    nqc = tq // ATT_QC
    nk = seq // tk
    unroll = min(ATT_UNROLL, nk)
    assert nk % unroll == 0 and unroll % 2 == 0
    return pl.pallas_call(
        functools.partial(_flash_body, tk=tk, nk=nk, nqc=nqc, unroll=unroll),
        grid=(bsz, nh // 2, seq // tq),
        in_specs=[pl.BlockSpec((1, 2, SLOT, tq), lambda b, hp, i: (b, hp, 0, i)),
                  pl.BlockSpec((1, 2, seq, SLOT), lambda b, hp, i: (b, hp, 0, 0)),
                  pl.BlockSpec((1, 1, 2 * V_AUG, seq), lambda b, hp, i: (b, hp, 0, 0))],
        out_specs=pl.BlockSpec((1, tq, 2 * V_HEAD), lambda b, hp, i: (b, i, hp)),
        out_shape=jax.ShapeDtypeStruct((bsz, seq, nh * V_HEAD), F32),
        scratch_shapes=[pltpu.VMEM((2 * nqc, tk, ATT_QC), F32), pltpu.VMEM((2 * nqc, tk, ATT_QC), F32),
                        pltpu.VMEM((2 * nqc, V_AUG, ATT_QC), F32)],
        compiler_params=_params(("parallel", "parallel", "parallel"), 48),
        name="flash_attn",
    )(qt, k, vt)


def _outproj_body(yh_ref, ya_ref, x_ref, og_ref, w_ref, mg_ref, wr_ref, xm_ref, xn_ref, aff_ref, affn_ref):
    og = og_ref[...]
    half = D_MODEL // 2
    tm = x_ref.shape[0]
    chunks = [slice(r, r + tm // OUTPROJ_CHUNKS) for r in range(0, tm, tm // OUTPROJ_CHUNKS)]
    y = [jnp.concatenate([_rms(yh_ref[rows, :]) * og[:, :half], _rms(ya_ref[rows, :]) * og[:, half:]],
                         axis=1).astype(BF16) for rows in chunks]
    xm = [x_ref[rows, :] + jnp.dot(yc, w_ref[...], preferred_element_type=F32) for rows, yc in zip(chunks, y)]
    xn = [_rms(xc) * mg_ref[...] for xc in xm]
    xn_hi = [xc.astype(BF16) for xc in xn]
    xn_lo = [(xc - hc.astype(F32)).astype(BF16) for xc, hc in zip(xn, xn_hi)]
    logits = [jnp.dot(hc, wr_ref[0], preferred_element_type=F32)
              + jnp.dot(lc, wr_ref[0], preferred_element_type=F32)
              + jnp.dot(hc, wr_ref[1], preferred_element_type=F32) for hc, lc in zip(xn_hi, xn_lo)]
    lane = lax.broadcasted_iota(I32, logits[0].shape, 1)
    for rows, xc, hc, lg in zip(chunks, xm, xn_hi, logits):
        xm_ref[rows, :] = xc
        xn_ref[rows, :] = hc
        lg = jnp.where(lane < N_EXPERTS, lg, -jnp.inf)
        e = jnp.exp(lg - jnp.max(lg, axis=-1, keepdims=True))
        aff = e / jnp.sum(e, axis=-1, keepdims=True)
        aff_ref[:, rows] = aff.T[:N_EXPERTS]
        affn_ref[rows, :] = aff[:, :N_EXPERTS]


def _outproj(yh, ya, x2d, og, w_out, mg, wr_pad):
    n = x2d.shape[0]
    tm = min(512, n)
    half = D_MODEL // 2
    const = lambda shape: pl.BlockSpec(shape, lambda i: (0,) * len(shape))
    return pl.pallas_call(
        _outproj_body,
        grid=(n // tm,),
        in_specs=[pl.BlockSpec((tm, half), lambda i: (i, 0)),
                  pl.BlockSpec((tm, half), lambda i: (i, 0)),
                  pl.BlockSpec((tm, D_MODEL), lambda i: (i, 0)),
                  const((1, D_MODEL)), const((D_MODEL, D_MODEL)), const((1, D_MODEL)),
                  const((2, D_MODEL, LANES))],
        out_specs=[pl.BlockSpec((tm, D_MODEL), lambda i: (i, 0)),
                   pl.BlockSpec((tm, D_MODEL), lambda i: (i, 0)),
                   pl.BlockSpec((N_EXPERTS, tm), lambda i: (0, i)),
                   pl.BlockSpec((tm, N_EXPERTS), lambda i: (i, 0))],
        out_shape=[jax.ShapeDtypeStruct((n, D_MODEL), F32),
                   jax.ShapeDtypeStruct((n, D_MODEL), BF16),
                   jax.ShapeDtypeStruct((N_EXPERTS, n), F32),
                   jax.ShapeDtypeStruct((n, N_EXPERTS), F32)],
        compiler_params=_params(("parallel",), 40),
        name="outproj_router",
    )(yh, ya, x2d, og, w_out, mg, wr_pad)


def _select_body(aff_ref, upper_ref, lower_ref, pos_ref, off_ref, *, cap):
    group = aff_ref.shape[0]
    bits = [pltpu.bitcast(aff_ref[x], I32) for x in range(group)]
    upper = upper_ref[...]
    lower = lower_ref[...]

    def count(mask):
        return jnp.sum(jnp.sum(mask.astype(F32), axis=1, keepdims=True), axis=0, keepdims=True)

    def bit_step(i, thr):
        bit = jnp.left_shift(jnp.int32(1), 30 - i)
        return tuple(jnp.where(count(bits[x] >= (thr[x] | bit)) >= cap, thr[x] | bit, thr[x])
                     for x in range(group))

    thr = lax.fori_loop(0, 31, bit_step, tuple(jnp.zeros((1, 1), I32) for _ in range(group)))

    def prefix(mask):
        within = jnp.dot(mask.astype(BF16), upper, preferred_element_type=F32)
        total = within[:, LANES - 1:LANES]
        offs = jnp.dot(lower, jnp.broadcast_to(total, within.shape).astype(BF16), preferred_element_type=F32)
        return within, offs

    for x in range(group):
        gt = bits[x] > thr[x]
        eq = bits[x] == thr[x]
        need = cap - count(gt)
        w_eq, o_eq = prefix(eq)
        sel = gt | (eq & (o_eq + w_eq <= need))
        w_sel, o_sel = prefix(sel)
        pos_ref[x] = jnp.where(sel, o_sel + w_sel - 1.0, -1.0).astype(I32)
        off_ref[x] = o_sel.astype(I32)


def _select(aff_t, cap):
    ne, n = aff_t.shape
    rows = n // LANES
    upper = (jnp.arange(LANES)[:, None] <= jnp.arange(LANES)[None, :]).astype(BF16)
    lower = (jnp.arange(rows)[None, :] < jnp.arange(rows)[:, None]).astype(BF16)
    group = 4
    blk = pl.BlockSpec((group, rows, LANES), lambda e: (e, 0, 0))
    pos, off = pl.pallas_call(
        functools.partial(_select_body, cap=cap),
        grid=(ne // group,),
        in_specs=[blk, pl.BlockSpec((LANES, LANES), lambda e: (0, 0)),
                  pl.BlockSpec((rows, rows), lambda e: (0, 0))],
        out_specs=[blk, blk],
        out_shape=[jax.ShapeDtypeStruct((ne, rows, LANES), I32),
                   jax.ShapeDtypeStruct((ne, rows, LANES), I32)],
        compiler_params=_params(("parallel",), 32),
        name="ec_select",
    )(aff_t.reshape(ne, rows, LANES), upper, lower)
    return pos.reshape(ne, n), off[:, :, 0]


def _ffn_body(off_ref, nch_ref, pos_ref, x_ref, wg_ref, wu_ref, wd_ref, y_ref, xs_ref, *, nb, sub, cap):
    group = y_ref.shape[0]
    e0 = pl.program_id(0) * group
    sb = pl.program_id(1)

    @pl.when(sb == 0)
    def _():
        xs_ref[...] = jnp.zeros(xs_ref.shape, BF16)

    row = lax.broadcasted_iota(I32, (GATHER_ROWS, TOK_BLOCK), 0)

    def first_chunks(i):
        tok = slice(i * TOK_BLOCK, (i + 1) * TOK_BLOCK)
        starts = [pl.multiple_of(off_ref[(e0 + x) * nb + sb * sub + i] * ALIGN, ALIGN) for x in range(group)]
        hits = [(pos_ref[x, :, tok] - starts[x]) == row for x in range(group)]
        win = jnp.dot(jnp.concatenate(hits, axis=0).astype(BF16), x_ref[tok, :], preferred_element_type=F32)
        for x in range(group):
            xs_ref[x, pl.ds(starts[x], GATHER_ROWS), :] += win[x * GATHER_ROWS:(x + 1) * GATHER_ROWS].astype(BF16)

    def later_chunk(x, i, c):
        tok = slice(i * TOK_BLOCK, (i + 1) * TOK_BLOCK)
        start = pl.multiple_of(off_ref[(e0 + x) * nb + sb * sub + i] * ALIGN + c * GATHER_ROWS, ALIGN)
        hit = (pos_ref[x, :, tok] - start) == row
        win = jnp.dot(hit.astype(BF16), x_ref[tok, :], preferred_element_type=F32)
        xs_ref[x, pl.ds(start, GATHER_ROWS), :] += win.astype(BF16)

    for i in range(sub):
        first_chunks(i)
    for x in range(group):
        for i in range(sub):
            lax.fori_loop(1, nch_ref[(e0 + x) * nb + sb * sub + i],
                          lambda c, carry, x=x, i=i: (later_chunk(x, i, c), carry)[1], 0)

    @pl.when(sb == pl.num_programs(1) - 1)
    def _():
        for x in range(group):
            for j in range(cap // SLOT_TILE):
                rows = slice(j * SLOT_TILE, (j + 1) * SLOT_TILE)
                xt = xs_ref[x, rows, :]
                g = jnp.dot(xt, wg_ref[x], preferred_element_type=F32)
                u = jnp.dot(xt, wu_ref[x], preferred_element_type=F32)
                h = (g * jax.nn.sigmoid(g) * u).astype(BF16)
                y_ref[x, rows, :] = jnp.dot(h, wd_ref[x], preferred_element_type=F32).astype(BF16)
            y_ref[x, cap:, :] = jnp.zeros((y_ref.shape[1] - cap, D_MODEL), BF16)


def _ffn(base, nchunk, pos3, xn, wg, wu, wd, cap):
    ne = wg.shape[0]
    n = xn.shape[0]
    nb = n // TOK_BLOCK
    sub = min(4, nb)
    group = 2
    cap_pad = cap + TOK_BLOCK + ALIGN
    grid_spec = pltpu.PrefetchScalarGridSpec(
        num_scalar_prefetch=2,
        grid=(ne // group, nb // sub),
        in_specs=[pl.BlockSpec((group, 1, sub * TOK_BLOCK), lambda e, s, off, nch: (e, 0, s)),
                  pl.BlockSpec((sub * TOK_BLOCK, D_MODEL), lambda e, s, off, nch: (s, 0)),
                  pl.BlockSpec((group, D_MODEL, D_EXPERT), lambda e, s, off, nch: (e, 0, 0)),
                  pl.BlockSpec((group, D_MODEL, D_EXPERT), lambda e, s, off, nch: (e, 0, 0)),
                  pl.BlockSpec((group, D_EXPERT, D_MODEL), lambda e, s, off, nch: (e, 0, 0))],
        out_specs=pl.BlockSpec((group, cap_pad, D_MODEL), lambda e, s, off, nch: (e, 0, 0)),
        scratch_shapes=[pltpu.VMEM((group, cap_pad, D_MODEL), BF16)])
    return pl.pallas_call(
        functools.partial(_ffn_body, nb=nb, sub=sub, cap=cap),
        grid_spec=grid_spec,
        out_shape=jax.ShapeDtypeStruct((ne, cap_pad, D_MODEL), BF16),
        compiler_params=_params(("parallel", "arbitrary"), 52),
        name="ec_ffn",
    )(base, nchunk, pos3, xn, wg, wu, wd)


def _combine_body(off_ref, spill_ref, post_ref, affn_ref, xm_ref, *refs, nb, ne, win_rows):
    win_refs, tail_refs, o_ref = refs[:ne], refs[ne:-1], refs[-1]
    b = pl.program_id(0)
    post = post_ref[...]
    affn = affn_ref[...]

    def expand(y_refs, shift):
        width = y_refs[0].shape[0]
        lane = lax.broadcasted_iota(I32, (TOK_BLOCK, width), 1)
        total = jnp.zeros((TOK_BLOCK, D_MODEL), F32)
        for e in range(ne):
            rel = post[:, e:e + 1] - (off_ref[e * nb + b] * ALIGN + shift)
            hit = (rel == lane).astype(BF16)
            total = total + affn[:, e:e + 1] * jnp.dot(hit, y_refs[e][...], preferred_element_type=F32)
        return total

    def expand_pairs(y_refs):
        width = y_refs[0].shape[0]
        lane = lax.broadcasted_iota(I32, (TOK_BLOCK, 2 * width), 1)
        total = jnp.zeros((TOK_BLOCK, D_MODEL), F32)
        for e in range(0, ne, 2):
            rel0 = post[:, e:e + 1] - off_ref[e * nb + b] * ALIGN
            rel1 = post[:, e + 1:e + 2] - off_ref[(e + 1) * nb + b] * ALIGN + width
            gates = (jnp.where(rel0 == lane, affn[:, e:e + 1], 0.0)
                     + jnp.where((rel1 == lane) & (rel1 >= width), affn[:, e + 1:e + 2], 0.0))
            rows = jnp.concatenate([y_refs[e][...], y_refs[e + 1][...]], axis=0)
            total = total + jnp.dot(gates.astype(BF16), rows, preferred_element_type=F32)
        return total

    if not tail_refs and 2 * win_rows <= TOK_BLOCK and ne % 2 == 0:
        o_ref[...] = xm_ref[...] + expand_pairs(win_refs)
    else:
        o_ref[...] = xm_ref[...] + expand(win_refs, 0)

    if tail_refs:
        @pl.when(spill_ref[b] != 0)
        def _():
            o_ref[...] += expand(tail_refs, win_rows)


def _combine(base, spill, pos_t, aff_n, xm, y, win_rows):
    ne = y.shape[0]
    n = xm.shape[0]
    nb = n // TOK_BLOCK
    tail_rows = ALIGN if win_rows == TOK_BLOCK else 0

    def window(e, rows, shift):
        return pl.BlockSpec((pl.Squeezed(), pl.Element(rows), pl.Element(D_MODEL)),
                            lambda b, off, sp: (e, (off[e * nb + b] + shift // ALIGN) * ALIGN, 0))

    tails = [window(e, tail_rows, win_rows) for e in range(ne)] if tail_rows else []
    grid_spec = pltpu.PrefetchScalarGridSpec(
        num_scalar_prefetch=2,
        grid=(nb,),
        in_specs=([pl.BlockSpec((TOK_BLOCK, ne), lambda b, off, sp: (b, 0)),
                   pl.BlockSpec((TOK_BLOCK, ne), lambda b, off, sp: (b, 0)),
                   pl.BlockSpec((TOK_BLOCK, D_MODEL), lambda b, off, sp: (b, 0))]
                  + [window(e, win_rows, 0) for e in range(ne)] + tails),
        out_specs=pl.BlockSpec((TOK_BLOCK, D_MODEL), lambda b, off, sp: (b, 0)))
    return pl.pallas_call(
        functools.partial(_combine_body, nb=nb, ne=ne, win_rows=win_rows),
        grid_spec=grid_spec,
        out_shape=jax.ShapeDtypeStruct((n, D_MODEL), F32),
        compiler_params=_params(("parallel",), 48),
        name="ec_combine",
    )(base, spill, pos_t, aff_n, xm, *([y] * (ne + len(tails))))


def _ec_moe(xm, xn, aff_t, aff_n, wg, wu, wd):
    ne, n = aff_t.shape
    cap = max(1, EC_CAPACITY * n // N_EXPERTS)
    nb = n // TOK_BLOCK
    pos, off = _select(aff_t, cap)
    first = off[:, ::TOK_BLOCK // LANES]
    count = jnp.concatenate([first[:, 1:], jnp.full((ne, 1), cap, I32)], axis=1) - first
    base = first // ALIGN
    span = jnp.where(count > 0, first - base * ALIGN + count, 0)
    spill = jnp.any(span > TOK_BLOCK, axis=0).astype(I32)
    nchunk = ((span + GATHER_ROWS - 1) // GATHER_ROWS).reshape(ne * nb).astype(I32)
    base = base.reshape(ne * nb).astype(I32)
    y = _ffn(base, nchunk, pos.reshape(ne, 1, n), xn, wg, wu, wd, cap)
    pos_t = pos.T
    return lax.cond(jnp.max(span) <= COMBINE_FAST_ROWS,
                    lambda: _combine(base, spill, pos_t, aff_n, xm, y, COMBINE_FAST_ROWS),
                    lambda: _combine(base, spill, pos_t, aff_n, xm, y, TOK_BLOCK))


def _slot_cols(w, head_width):
    k = w.shape[0]
    w3 = w.reshape(k, N_HEADS, head_width)
    return jnp.pad(w3, ((0, 0), (0, 0), (0, SLOT - head_width))).reshape(k, N_HEADS * SLOT)


def _attn_weights(q_a_norm, w_uq, kv_a_norm, w_ukv, q_norm, k_norm):
    wkv = w_ukv.reshape(KV_LORA, N_HEADS, QK_NOPE + V_HEAD)
    wk = _slot_cols(wkv[:, :, :QK_NOPE].reshape(KV_LORA, -1), QK_NOPE)
    wv = wkv[:, :, QK_NOPE:].reshape(KV_LORA, D_ATTN)
    half = QK_ROPE // 2

    def rotate_half(a):
        return jnp.concatenate([jnp.zeros_like(a[..., :QK_NOPE]), -a[..., QK_NOPE + half:],
                                a[..., QK_NOPE:QK_NOPE + half]], axis=-1)

    def gains(g):
        swapped = jnp.concatenate([jnp.zeros_like(g[:QK_NOPE]), g[QK_NOPE + half:], g[QK_NOPE:QK_NOPE + half]])
        return jnp.pad(jnp.stack([g, swapped]), ((0, 0), (0, SLOT - QK_HEAD)))

    wq3 = w_uq.reshape(Q_LORA, N_HEADS, QK_HEAD)
    return dict(wq=_slot_cols(w_uq, QK_HEAD).astype(BF16),
                wq_rot=_slot_cols(rotate_half(wq3).reshape(Q_LORA, -1), QK_HEAD).astype(BF16),
                wk=wk.astype(BF16), wv=wv.astype(BF16), qa=q_a_norm[None, :], kva=kv_a_norm[None, :],
                qg=gains(q_norm), kg=gains(k_norm))


def _rope_tables(seq):
    pos = jnp.arange(seq, dtype=F32)
    inv_freq = ROPE_THETA ** (-jnp.arange(0, QK_ROPE, 2, dtype=F32) / QK_ROPE)
    ang = pos[:, None] * inv_freq
    ang = jnp.concatenate([ang, ang], axis=-1)
    pad = lambda t, fill: jnp.concatenate(
        [jnp.full((seq, QK_NOPE), fill, F32), t, jnp.full((seq, SLOT - QK_HEAD), fill, F32)], axis=1)
    return pad(jnp.cos(ang), 1.0), pad(jnp.sin(ang), 0.0)


def _hi_lo(w):
    hi = w.astype(BF16)
    return jnp.stack([hi, (w - hi.astype(F32)).astype(BF16)], axis=1)


def _filter_weights(w1, b1, w2, b2, w3, freq, decay):
    hid, nbands = FILTER_HIDDEN, FILTER_BANDS
    assert 2 * hid == LANES and 2 * nbands <= hid
    both = lambda row: jnp.concatenate([row, row])[None, :]
    blockdiag = lambda a: jnp.concatenate(
        [jnp.concatenate([a, jnp.zeros_like(a)], axis=1), jnp.concatenate([jnp.zeros_like(a), a], axis=1)], axis=0)
    bands = jnp.linspace(1e-4, nbands - 1, nbands, dtype=F32)
    rest = jnp.zeros((hid - 2 * nbands,), F32)
    band_half = jnp.concatenate([bands, bands, rest])
    phase_half = jnp.concatenate([jnp.zeros((nbands,), F32), jnp.full((nbands,), 0.5 * math.pi, F32), rest])
    w1_half = jnp.concatenate([-w1[1 + nbands:], w1[1:1 + nbands], jnp.zeros((hid - 2 * nbands, hid), F32)], axis=0)
    cols = HYENA_ORDER * D_HYENA
    w3d = jnp.transpose(w3.reshape(hid, N_DIR, cols), (1, 0, 2))
    w3p = jnp.stack([jnp.pad(w3d[0], ((0, hid), (0, 0))), jnp.pad(w3d[1], ((hid, 0), (0, 0)))], axis=0)
    w3_hi = w3p.astype(BF16)
    w3_lo = (w3p - w3_hi.astype(F32)).astype(BF16)
    return dict(bands=both(band_half), phase=both(phase_half), w1t=both(w1[0]), w1=blockdiag(w1_half),
                b1=both(b1), w2=blockdiag(w2), b2=both(b2), freq=both(freq), w3_hi=w3_hi, w3_lo=w3_lo,
                decay=decay.reshape(N_DIR, 1, cols))


def _trunk(x, p):
    bsz, seq, _ = x.shape
    n = bsz * seq
    cos_t, sin_t = _rope_tables(seq)
    x2 = x.reshape(n, D_MODEL)
    for l in range(DEPTH):
        uh, cq, ckvr = _inproj(x2, p["attn_norm"][l][None, :], p["w_in"][l])
        y_h = _hyena(uh.reshape(bsz, seq, COL_HYENA), p["conv_w"][l], p["conv_b"][l][None, :],
                     p["filt"][l], p["hyena_bias"][l])
        qt, k, vt = _qkv(cq.reshape(bsz, seq, -1), ckvr.reshape(bsz, seq, -1), cos_t, sin_t, p["attn"][l])
        y_a = _flash(qt, k, vt)
        xm, xn, aff_t, aff_n = _outproj(y_h.reshape(n, D_HYENA), y_a.reshape(n, D_ATTN), x2,
                                 p["out_norm"][l][None, :], p["w_out"][l], p["mlp_norm"][l][None, :],
                                 p["w_router"][l])
        x2 = _ec_moe(xm, xn, aff_t, aff_n, p["w_gate"][l], p["w_up"][l], p["w_down"][l])
    return x2.reshape(bsz, seq, D_MODEL)


def kernel(x_prompt, x_sample, attn_norm, w_in, conv_w, conv_b, filt_w1, filt_b1, filt_w2, filt_b2, filt_w3,
           filt_freq, filt_decay, hyena_bias, q_a_norm, w_uq, kv_a_norm, w_ukv, q_norm, k_norm, out_norm,
           w_out, mlp_norm, w_router, w_gate, w_up, w_down):
    p = dict(
        attn_norm=attn_norm,
        w_in=jnp.pad(w_in, ((0, 0), (0, 0), (0, D_IN_PAD - D_IN))).astype(BF16),
        conv_w=conv_w, conv_b=conv_b, hyena_bias=hyena_bias,
        filt=[_filter_weights(filt_w1[l], filt_b1[l], filt_w2[l], filt_b2[l], filt_w3[l], filt_freq[l],
                              filt_decay[l]) for l in range(DEPTH)],
        attn=[_attn_weights(q_a_norm[l], w_uq[l], kv_a_norm[l], w_ukv[l], q_norm[l], k_norm[l])
              for l in range(DEPTH)],
        out_norm=out_norm, w_out=w_out.astype(BF16), mlp_norm=mlp_norm,
        w_router=_hi_lo(jnp.pad(w_router, ((0, 0), (0, 0), (0, LANES - N_EXPERTS)))),
        w_gate=w_gate.astype(BF16), w_up=w_up.astype(BF16), w_down=w_down.astype(BF16))
    return (_trunk(x_prompt, p), _trunk(x_sample, p))
```
